```python
import math
import jax, jax.numpy as jnp
from jax import lax
import numpy as np

D_MODEL = 2048
BATCH = 8
SEQ = 8192
DEPTH = 4

HGRN_WIDTH = D_MODEL // 2
HGRN_HEAD_DIM = 128
HGRN_HEADS = HGRN_WIDTH // HGRN_HEAD_DIM
HGRN_CHUNK = 64
ATTN_HEAD_DIM = 64
ATTN_HEADS = (D_MODEL // 2) // ATTN_HEAD_DIM
ATTN_KV_HEADS = 4
ATTN_WIDTH = ATTN_HEADS * ATTN_HEAD_DIM
KV_WIDTH = ATTN_KV_HEADS * ATTN_HEAD_DIM
WINDOW = 128
CONV_WIDTH = D_MODEL // 2
CONV_K = 3
N_BUCKETS = 32
MAX_DISTANCE = 128
ALPHA = (2.0 * DEPTH) ** 0.25
BETA = (8.0 * DEPTH) ** -0.25
LN_EPS = 1e-5
RMS_EPS = 1e-6
SPLIT_SIZES = (
    HGRN_WIDTH, HGRN_WIDTH, HGRN_WIDTH, HGRN_WIDTH,
    ATTN_WIDTH, KV_WIDTH, KV_WIDTH, ATTN_WIDTH,
    CONV_WIDTH, CONV_WIDTH, CONV_WIDTH, CONV_WIDTH,
    D_MODEL, D_MODEL, D_MODEL,
)
N_IN = sum(SPLIT_SIZES)

kernel_name = "hybrid_hgrn2_swa_sink_shortconv_gated_merge"


def layer_norm(x, g, b):
    xf = x.astype(jnp.float32)
    mu = jnp.mean(xf, axis=-1, keepdims=True)
    var = jnp.mean(jnp.square(xf - mu), axis=-1, keepdims=True)
    return ((xf - mu) * lax.rsqrt(var + LN_EPS) * g.astype(jnp.float32) + b.astype(jnp.float32)).astype(x.dtype)


def t5_bucket(dist):
    max_exact = N_BUCKETS // 2
    is_small = dist < max_exact
    logd = jnp.log(jnp.maximum(dist, 1).astype(jnp.float32) / max_exact) / math.log(MAX_DISTANCE / max_exact)
    large = max_exact + (logd * (N_BUCKETS - max_exact)).astype(jnp.int32)
    large = jnp.minimum(large, N_BUCKETS - 1)
    return jnp.where(is_small, dist, large)


def band_relative_bias(rel_bias):
    i = jnp.arange(WINDOW)[:, None]
    j = jnp.arange(2 * WINDOW)[None, :]
    rel = jnp.clip(WINDOW + i - j, 0, WINDOW - 1)
    bucket = t5_bucket(rel)
    return jnp.transpose(rel_bias[bucket], (2, 0, 1)).astype(jnp.float32)


def hgrn2_mixer(q, f_logit, inp, lb):
    bsz, t, _ = q.shape
    h, d, c = HGRN_HEADS, HGRN_HEAD_DIM, HGRN_CHUNK
    nc = t // c
    qf = jax.nn.silu(q.astype(jnp.float32)) * (d ** -0.5)
    f = lb + (1.0 - lb) * jax.nn.sigmoid(f_logit.astype(jnp.float32))
    k = 1.0 - f
    g = jnp.log(f)
    v = inp.astype(jnp.float32)

    def to_chunks(a):
        return jnp.transpose(a.reshape(bsz, nc, c, h, d), (1, 0, 3, 2, 4))

    causal = jnp.tril(jnp.ones((c, c), dtype=bool))

    def step(state, chunk):
        qc, kc, vc, gc = chunk
        b = jnp.cumsum(gc, axis=2)
        inter = jnp.einsum('bhtk,bhkv->bhtv', qc * jnp.exp(b), state)
        diff = b[:, :, :, None, :] - b[:, :, None, :, :]
        decay = jnp.exp(jnp.where(causal[None, None, :, :, None], diff, -jnp.inf))
        scores = jnp.einsum('bhtk,bhsk,bhtsk->bhts', qc, kc, decay)
        intra = jnp.einsum('bhts,bhsv->bhtv', scores, vc)
        b_end = b[:, :, -1, :]
        k_to_end = kc * jnp.exp(b_end[:, :, None, :] - b)
        new_state = jnp.exp(b_end)[..., None] * state + jnp.einsum('bhsk,bhsv->bhkv', k_to_end, vc)
        return new_state, inter + intra

    s0 = jnp.zeros((bsz, h, d, d), jnp.float32)
    _, o = lax.scan(step, s0, (to_chunks(qf), to_chunks(k), to_chunks(v), to_chunks(g)))
    return jnp.transpose(o, (1, 0, 3, 2, 4)).reshape(bsz, t, h, d)


def swa_sink_attention(q, k, v, bias, sinks):
    bsz, t, _ = q.shape
    w, dh, kvh = WINDOW, ATTN_HEAD_DIM, ATTN_KV_HEADS
    grp = ATTN_HEADS // kvh
    nb = t // w
    qb = q.reshape(bsz, nb, w, kvh, grp, dh)
    kb = k.reshape(bsz, nb, w, kvh, dh)
    vb = v.reshape(bsz, nb, w, kvh, dh)

    def with_prev(a):
        prev = jnp.concatenate([jnp.zeros_like(a[:, :1]), a[:, :-1]], axis=1)
        return jnp.concatenate([prev, a], axis=2)

    kw, vw = with_prev(kb), with_prev(vb)
    s = jnp.einsum('bnqhgd,bnkhd->bhgnqk', qb, kw).astype(jnp.float32) * (dh ** -0.5)
    s = s + bias.reshape(kvh, grp, 1, w, 2 * w)
    i = jnp.arange(w)[:, None]
    j = jnp.arange(2 * w)[None, :]
    rel = w + i - j
    band = (rel >= 0) & (rel < w)
    key_pos = jnp.arange(nb)[:, None, None] * w - w + j[None]
    mask = band[None] & (key_pos >= 0)
    s = jnp.where(mask, s, -jnp.inf)
    sink = sinks.astype(jnp.float32).reshape(kvh, grp, 1, 1, 1)
    m = jnp.maximum(jnp.max(s, axis=-1, keepdims=True), sink)
    p = jnp.exp(s - m)
    p = p / (jnp.sum(p, axis=-1, keepdims=True) + jnp.exp(sink - m))
    o = jnp.einsum('bhgnqk,bnkhd->bnqhgd', p.astype(v.dtype), vw)
    return o.reshape(bsz, t, ATTN_WIDTH)


def short_gated_conv(b_gate, c_gate, xin, conv_w):
    h = c_gate * xin
    hp = jnp.pad(h, ((0, 0), (CONV_K - 1, 0), (0, 0)))
    t = h.shape[1]
    y = conv_w[0] * hp[:, 0:t] + conv_w[1] * hp[:, 1:t + 1] + conv_w[2] * hp[:, 2:t + 2]
    return b_gate * y


def _fwd_setup_inputs(seed: int = 0) -> dict:
    key = jax.random.key(seed)
    ks = jax.random.split(key, 13)
    nrm = jax.random.normal
    f32 = jnp.float32
    return {
        "x": nrm(ks[0], (BATCH, SEQ, D_MODEL), f32),
        "w_in": nrm(ks[1], (DEPTH, D_MODEL, N_IN), f32) * D_MODEL ** -0.5,
        "w_proj_hgrn": nrm(ks[2], (DEPTH, HGRN_WIDTH, D_MODEL), f32) * (HGRN_WIDTH ** -0.5 * BETA),
        "w_proj_attn": nrm(ks[3], (DEPTH, ATTN_WIDTH, D_MODEL), f32) * (ATTN_WIDTH ** -0.5 * BETA),
        "w_proj_conv": nrm(ks[4], (DEPTH, CONV_WIDTH, D_MODEL), f32) * (CONV_WIDTH ** -0.5 * BETA),
        "w_out": nrm(ks[5], (DEPTH, D_MODEL, D_MODEL), f32) * (D_MODEL ** -0.5 * BETA),
        "lb_param": nrm(ks[6], (DEPTH, HGRN_WIDTH), f32) * 0.5,
        "hgrn_norm_g": 1.0 + 0.02 * nrm(ks[7], (DEPTH, HGRN_WIDTH), f32),
        "attn_sinks": nrm(ks[8], (DEPTH, ATTN_HEADS), f32),
        "conv_w": nrm(ks[9], (DEPTH, CONV_K, CONV_WIDTH), f32) * CONV_K ** -0.5,
        "rel_bias": nrm(ks[10], (N_BUCKETS, ATTN_HEADS), f32) * 0.1,
        "ln_g": 1.0 + 0.02 * nrm(ks[11], (DEPTH, D_MODEL), f32),
        "ln_b": 0.02 * nrm(ks[12], (DEPTH, D_MODEL), f32),
    }


def _fwd_reference(x, w_in, w_proj_hgrn, w_proj_attn, w_proj_conv, w_out, lb_param, hgrn_norm_g,
              attn_sinks, conv_w, rel_bias, ln_g, ln_b):
    bsz, t, _ = x.shape
    split_idx = [int(s) for s in np.cumsum(SPLIT_SIZES)[:-1]]
    lb_soft = jax.nn.softmax(lb_param.astype(jnp.float32), axis=0)
    lower_bounds = jnp.cumsum(lb_soft, axis=0) - lb_soft[0:1]
    bias = band_relative_bias(rel_bias)
    for l in range(DEPTH):
        u = x @ w_in[l]
        (a_q, a_f, a_i, a_g, b_q, b_k, b_v, b_g,
         c_b, c_c, c_x, c_g, m_a, m_b, m_c) = jnp.split(u, split_idx, axis=-1)
        o_a = hgrn2_mixer(a_q, a_f, a_i, lower_bounds[l])
        o_a = o_a * lax.rsqrt(jnp.mean(jnp.square(o_a), axis=-1, keepdims=True) + RMS_EPS)
        o_a = o_a.reshape(bsz, t, HGRN_WIDTH) * hgrn_norm_g[l].astype(jnp.float32)
        y_a = (o_a.astype(x.dtype) * jax.nn.silu(a_g)) @ w_proj_hgrn[l]
        o_b = swa_sink_attention(b_q, b_k, b_v, bias, attn_sinks[l])
        y_b = (o_b * jax.nn.silu(b_g)) @ w_proj_attn[l]
        o_c = short_gated_conv(c_b, c_c, c_x, conv_w[l])
        y_c = (o_c * jax.nn.silu(c_g)) @ w_proj_conv[l]
        merged = jax.nn.sigmoid(m_a) * y_a + jax.nn.sigmoid(m_b) * y_b + jax.nn.sigmoid(m_c) * y_c
        y = merged @ w_out[l]
        x = layer_norm(ALPHA * x + y, ln_g[l], ln_b[l])
    return x


import jax as _jax
import jax.numpy as _jnp

TWIN_FORMAT = 'train_step'
FWD_PARAMS = ['x', 'w_in', 'w_proj_hgrn', 'w_proj_attn', 'w_proj_conv', 'w_out', 'lb_param', 'hgrn_norm_g', 'attn_sinks', 'conv_w', 'rel_bias', 'ln_g', 'ln_b']
TWIN_WEIGHTS = ['w_in', 'w_proj_hgrn', 'w_proj_attn', 'w_proj_conv', 'w_out', 'lb_param', 'hgrn_norm_g', 'attn_sinks', 'conv_w', 'rel_bias', 'ln_g', 'ln_b']
TWIN_DIFF_INPUT = 'x'
TWIN_INPUTS = ['x', 'w_in', 'w_proj_hgrn', 'w_proj_attn', 'w_proj_conv', 'w_out', 'lb_param', 'hgrn_norm_g', 'attn_sinks', 'conv_w', 'rel_bias', 'ln_g', 'ln_b', 'loss_target', 'm_w_in', 'm_w_proj_hgrn', 'm_w_proj_attn', 'm_w_proj_conv', 'm_w_out', 'm_lb_param', 'm_hgrn_norm_g', 'm_attn_sinks', 'm_conv_w', 'm_rel_bias', 'm_ln_g', 'm_ln_b', 'v_w_in', 'v_w_proj_hgrn', 'v_w_proj_attn', 'v_w_proj_conv', 'v_w_out', 'v_lb_param', 'v_hgrn_norm_g', 'v_attn_sinks', 'v_conv_w', 'v_rel_bias', 'v_ln_g', 'v_ln_b']
TWIN_OUTPUTS = ['loss', 'grad_x', 'grad_w_in', 'grad_w_proj_hgrn', 'grad_w_proj_attn', 'grad_w_proj_conv', 'grad_w_out', 'grad_lb_param', 'grad_hgrn_norm_g', 'grad_attn_sinks', 'grad_conv_w', 'grad_rel_bias', 'grad_ln_g', 'grad_ln_b', 'delta_w_in', 'delta_w_proj_hgrn', 'delta_w_proj_attn', 'delta_w_proj_conv', 'delta_w_out', 'delta_lb_param', 'delta_hgrn_norm_g', 'delta_attn_sinks', 'delta_conv_w', 'delta_rel_bias', 'delta_ln_g', 'delta_ln_b', 'new_m_w_in', 'new_m_w_proj_hgrn', 'new_m_w_proj_attn', 'new_m_w_proj_conv', 'new_m_w_out', 'new_m_lb_param', 'new_m_hgrn_norm_g', 'new_m_attn_sinks', 'new_m_conv_w', 'new_m_rel_bias', 'new_m_ln_g', 'new_m_ln_b', 'new_v_w_in', 'new_v_w_proj_hgrn', 'new_v_w_proj_attn', 'new_v_w_proj_conv', 'new_v_w_out', 'new_v_lb_param', 'new_v_hgrn_norm_g', 'new_v_attn_sinks', 'new_v_conv_w', 'new_v_rel_bias', 'new_v_ln_g', 'new_v_ln_b']
TWIN_LEAF_KINDS = {'loss': 'loss', 'grad_x': 'grad_x', 'grad_w_in': 'grad_w', 'grad_w_proj_hgrn': 'grad_w', 'grad_w_proj_attn': 'grad_w', 'grad_w_proj_conv': 'grad_w', 'grad_w_out': 'grad_w', 'grad_lb_param': 'grad_w', 'grad_hgrn_norm_g': 'grad_w', 'grad_attn_sinks': 'grad_w', 'grad_conv_w': 'grad_w', 'grad_rel_bias': 'grad_w', 'grad_ln_g': 'grad_w', 'grad_ln_b': 'grad_w', 'delta_w_in': 'delta_w', 'delta_w_proj_hgrn': 'delta_w', 'delta_w_proj_attn': 'delta_w', 'delta_w_proj_conv': 'delta_w', 'delta_w_out': 'delta_w', 'delta_lb_param': 'delta_w', 'delta_hgrn_norm_g': 'delta_w', 'delta_attn_sinks': 'delta_w', 'delta_conv_w': 'delta_w', 'delta_rel_bias': 'delta_w', 'delta_ln_g': 'delta_w', 'delta_ln_b': 'delta_w', 'new_m_w_in': 'new_m', 'new_m_w_proj_hgrn': 'new_m', 'new_m_w_proj_attn': 'new_m', 'new_m_w_proj_conv': 'new_m', 'new_m_w_out': 'new_m', 'new_m_lb_param': 'new_m', 'new_m_hgrn_norm_g': 'new_m', 'new_m_attn_sinks': 'new_m', 'new_m_conv_w': 'new_m', 'new_m_rel_bias': 'new_m', 'new_m_ln_g': 'new_m', 'new_m_ln_b': 'new_m', 'new_v_w_in': 'new_v', 'new_v_w_proj_hgrn': 'new_v', 'new_v_w_proj_attn': 'new_v', 'new_v_w_proj_conv': 'new_v', 'new_v_w_out': 'new_v', 'new_v_lb_param': 'new_v', 'new_v_hgrn_norm_g': 'new_v', 'new_v_attn_sinks': 'new_v', 'new_v_conv_w': 'new_v', 'new_v_rel_bias': 'new_v', 'new_v_ln_g': 'new_v', 'new_v_ln_b': 'new_v'}


def _forward(args):
    return _fwd_reference(*[args[k] for k in FWD_PARAMS])


def _output_shape():
    def fwd():
        inp = _fwd_setup_inputs(0)
        return _fwd_reference(*[inp[k] for k in FWD_PARAMS])
    out = _jax.eval_shape(fwd)
    return out.shape, out.dtype

N_MICROBATCH = 1
ADAM_LR = 0.001
ADAM_B1 = 0.9
ADAM_B2 = 0.999
ADAM_EPS = 1e-08
ADAM_WD = 0.01
ADAM_STEP = 10
PER_EXAMPLE_BATCH_AXIS = {'x': 0, 'loss_target': 0}
SHARED_INPUTS = []
_WEIGHT_DTYPES = {'w_in': _jnp.float32, 'w_proj_hgrn': _jnp.float32, 'w_proj_attn': _jnp.float32, 'w_proj_conv': _jnp.float32, 'w_out': _jnp.float32, 'lb_param': _jnp.float32, 'hgrn_norm_g': _jnp.float32, 'attn_sinks': _jnp.float32, 'conv_w': _jnp.float32, 'rel_bias': _jnp.float32, 'ln_g': _jnp.float32, 'ln_b': _jnp.float32}
MOMENT_SCALE = {'w_in': 3.803815e-03, 'w_proj_hgrn': 1.016057e-02, 'w_proj_attn': 2.103617e-03, 'w_proj_conv': 1.014570e-02, 'w_out': 1.445256e-02, 'lb_param': 6.432579e-04, 'hgrn_norm_g': 6.074680e-03, 'attn_sinks': 1.094177e-03, 'conv_w': 6.114874e-03, 'rel_bias': 3.019723e-03, 'ln_g': 1.603364e+01, 'ln_b': 7.096808e-01}


def _to_microbatches(a, axis):
    t = _jnp.moveaxis(a, axis, 0)
    t = t.reshape((N_MICROBATCH, t.shape[0] // N_MICROBATCH) + t.shape[1:])
    return _jnp.moveaxis(t, 1, axis + 1)


def setup_inputs(seed: int = 0) -> dict:
    inp = _fwd_setup_inputs(seed)
    key = _jax.random.fold_in(_jax.random.key(seed), 7919)
    shape, _ = _output_shape()
    out = dict(inp)
    out["loss_target"] = _jax.random.normal(_jax.random.fold_in(key, 0), shape, _jnp.float32)
    for i, name in enumerate(TWIN_WEIGHTS):
        w = inp[name].astype(_jnp.float32)
        if MOMENT_SCALE is None:
            s = _jnp.sqrt(_jnp.mean(_jnp.square(w)) + 1e-30)
        else:
            s = MOMENT_SCALE[name]
        km, kv = _jax.random.split(_jax.random.fold_in(key, i + 1))
        out[name] = w
        out["m_" + name] = s * _jax.random.normal(km, w.shape, _jnp.float32)
        out["v_" + name] = (s * s) * _jax.random.uniform(kv, w.shape, _jnp.float32, 0.5, 1.5)
    if N_MICROBATCH > 1:
        for name, axis in PER_EXAMPLE_BATCH_AXIS.items():
            out[name] = _to_microbatches(out[name], axis)
    return {'x': out['x'], 'w_in': out['w_in'], 'w_proj_hgrn': out['w_proj_hgrn'], 'w_proj_attn': out['w_proj_attn'], 'w_proj_conv': out['w_proj_conv'], 'w_out': out['w_out'], 'lb_param': out['lb_param'], 'hgrn_norm_g': out['hgrn_norm_g'], 'attn_sinks': out['attn_sinks'], 'conv_w': out['conv_w'], 'rel_bias': out['rel_bias'], 'ln_g': out['ln_g'], 'ln_b': out['ln_b'], 'loss_target': out['loss_target'], 'm_w_in': out['m_w_in'], 'm_w_proj_hgrn': out['m_w_proj_hgrn'], 'm_w_proj_attn': out['m_w_proj_attn'], 'm_w_proj_conv': out['m_w_proj_conv'], 'm_w_out': out['m_w_out'], 'm_lb_param': out['m_lb_param'], 'm_hgrn_norm_g': out['m_hgrn_norm_g'], 'm_attn_sinks': out['m_attn_sinks'], 'm_conv_w': out['m_conv_w'], 'm_rel_bias': out['m_rel_bias'], 'm_ln_g': out['m_ln_g'], 'm_ln_b': out['m_ln_b'], 'v_w_in': out['v_w_in'], 'v_w_proj_hgrn': out['v_w_proj_hgrn'], 'v_w_proj_attn': out['v_w_proj_attn'], 'v_w_proj_conv': out['v_w_proj_conv'], 'v_w_out': out['v_w_out'], 'v_lb_param': out['v_lb_param'], 'v_hgrn_norm_g': out['v_hgrn_norm_g'], 'v_attn_sinks': out['v_attn_sinks'], 'v_conv_w': out['v_conv_w'], 'v_rel_bias': out['v_rel_bias'], 'v_ln_g': out['v_ln_g'], 'v_ln_b': out['v_ln_b']}


def _loss(weights, diff, rest, loss_target):
    with _jax.named_scope("forward"):
        args = {**rest, TWIN_DIFF_INPUT: diff, **{k: w.astype(_WEIGHT_DTYPES[k]) for k, w in weights.items()}}
        y = _forward(args)
    with _jax.named_scope("loss_head"):
        err = _jnp.square(y.astype(_jnp.float32) - loss_target)
        return 0.5 * _jnp.sum(_jnp.mean(err, axis=-1)) if err.ndim else 0.5 * err


def _adamw(w, g, m, v):
    m = ADAM_B1 * m + (1.0 - ADAM_B1) * g
    v = ADAM_B2 * v + (1.0 - ADAM_B2) * _jnp.square(g)
    m_hat = m / (1.0 - ADAM_B1 ** ADAM_STEP)
    v_hat = v / (1.0 - ADAM_B2 ** ADAM_STEP)
    delta = -ADAM_LR * (m_hat / (_jnp.sqrt(v_hat) + ADAM_EPS) + ADAM_WD * w)
    return delta, m, v


def reference(x, w_in, w_proj_hgrn, w_proj_attn, w_proj_conv, w_out, lb_param, hgrn_norm_g, attn_sinks, conv_w, rel_bias, ln_g, ln_b, loss_target, m_w_in, m_w_proj_hgrn, m_w_proj_attn, m_w_proj_conv, m_w_out, m_lb_param, m_hgrn_norm_g, m_attn_sinks, m_conv_w, m_rel_bias, m_ln_g, m_ln_b, v_w_in, v_w_proj_hgrn, v_w_proj_attn, v_w_proj_conv, v_w_out, v_lb_param, v_hgrn_norm_g, v_attn_sinks, v_conv_w, v_rel_bias, v_ln_g, v_ln_b):
    given = dict(x=x, w_in=w_in, w_proj_hgrn=w_proj_hgrn, w_proj_attn=w_proj_attn, w_proj_conv=w_proj_conv, w_out=w_out, lb_param=lb_param, hgrn_norm_g=hgrn_norm_g, attn_sinks=attn_sinks, conv_w=conv_w, rel_bias=rel_bias, ln_g=ln_g, ln_b=ln_b, loss_target=loss_target, m_w_in=m_w_in, m_w_proj_hgrn=m_w_proj_hgrn, m_w_proj_attn=m_w_proj_attn, m_w_proj_conv=m_w_proj_conv, m_w_out=m_w_out, m_lb_param=m_lb_param, m_hgrn_norm_g=m_hgrn_norm_g, m_attn_sinks=m_attn_sinks, m_conv_w=m_conv_w, m_rel_bias=m_rel_bias, m_ln_g=m_ln_g, m_ln_b=m_ln_b, v_w_in=v_w_in, v_w_proj_hgrn=v_w_proj_hgrn, v_w_proj_attn=v_w_proj_attn, v_w_proj_conv=v_w_proj_conv, v_w_out=v_w_out, v_lb_param=v_lb_param, v_hgrn_norm_g=v_hgrn_norm_g, v_attn_sinks=v_attn_sinks, v_conv_w=v_conv_w, v_rel_bias=v_rel_bias, v_ln_g=v_ln_g, v_ln_b=v_ln_b)
    weights = {n: given[n] for n in TWIN_WEIGHTS}
    shared = {n: given[n] for n in SHARED_INPUTS}
    per_example = {n: given[n] for n in ['x']}
    grad_fn = _jax.value_and_grad(_loss, argnums=(0, 1))

    def one_microbatch(ex, loss_target):
        ex = dict(ex)
        diff = ex.pop(TWIN_DIFF_INPUT)
        return grad_fn(weights, diff, {**shared, **ex}, loss_target)

    if N_MICROBATCH == 1:
        loss, (grad_w, grad_x) = one_microbatch(per_example, given["loss_target"])
    else:
        def body(carry, xs):
            loss_sum, grad_sum = carry
            l_k, (gw_k, gx_k) = one_microbatch(xs[0], xs[1])
            with _jax.named_scope("update"):
                return (loss_sum + l_k, _jax.tree.map(_jnp.add, grad_sum, gw_k)), gx_k

        init = (_jnp.zeros((), _jnp.float32), _jax.tree.map(_jnp.zeros_like, weights))
        (loss, grad_w), grad_x = _jax.lax.scan(body, init, (per_example, given["loss_target"]))
    with _jax.named_scope("update"):
        delta_w, new_m, new_v = {}, {}, {}
        for n in TWIN_WEIGHTS:
            delta_w[n], new_m[n], new_v[n] = _adamw(weights[n], grad_w[n], given["m_" + n], given["v_" + n])
    return (loss, grad_x, *[grad_w[n] for n in TWIN_WEIGHTS], *[delta_w[n] for n in TWIN_WEIGHTS],
            *[new_m[n] for n in TWIN_WEIGHTS], *[new_v[n] for n in TWIN_WEIGHTS])
```

```python
import functools
import math

import numpy as np
import jax
import jax.numpy as jnp
from jax import lax
from jax.experimental import pallas as pl
from jax.experimental.pallas import tpu as pltpu

F32 = jnp.float32
BF16 = jnp.bfloat16

N_DEV = 8
DEPTH = 4
HGRN_HEAD_DIM = 128
HGRN_CHUNK = 64
ATTN_HEAD_DIM = 64
ATTN_KV_HEADS = 4
KV_WIDTH = ATTN_KV_HEADS * ATTN_HEAD_DIM
WINDOW = 128
WINDOW_SHIFT = 7
N_BUCKETS = 32
MAX_DISTANCE = 128
ALPHA = (2.0 * DEPTH) ** 0.25
LN_EPS = 1e-5
RMS_EPS = 1e-6
ADAM_LR = 0.001
ADAM_B1 = 0.9
ADAM_B2 = 0.999
ADAM_EPS = 1e-08
ADAM_WD = 0.01
ADAM_STEP = 10

LANES = 128
SUBLANES = 8
VMEM_LIMIT = 56 << 20
NEG_INF = float("-inf")


def _offsets(d_model):
    w = d_model // 2
    sizes = (w, w, w, w, w, KV_WIDTH, KV_WIDTH, w, w, w, w, w, d_model, d_model, d_model)
    names = ("a_q", "a_f", "a_i", "a_g", "b_q", "b_k", "b_v", "b_g", "c_b", "c_c", "c_x", "c_g", "m_a", "m_b", "m_c")
    off, o = {}, 0
    for n, s in zip(names, sizes):
        off[n] = o
        o += s
    return off, o


def _tile(n, pref):
    t = min(pref, n)
    while n % t:
        t //= 2
    return t


def _cp(sem=None, vmem=VMEM_LIMIT):
    return pltpu.CompilerParams(dimension_semantics=sem, vmem_limit_bytes=vmem)


def _sigmoid(x):
    return 1.0 / (1.0 + jnp.exp(-x))


def _dot_nn(a, b):
    return jnp.dot(a, b, preferred_element_type=F32)


def _dot_nt(a, b):
    return lax.dot_general(a, b, (((1,), (1,)), ((), ())), preferred_element_type=F32)


def _dot_tn(a, b):
    return lax.dot_general(a, b, (((0,), (0,)), ((), ())), preferred_element_type=F32)


def _mm_nn(a, b, name, out_dtype=F32, tm=1024, tn=1536):
    m, k = a.shape
    _, n = b.shape
    tm, tn = _tile(m, tm), _tile(n, tn)

    def body(a_ref, b_ref, o_ref):
        o_ref[...] = _dot_nn(a_ref[...], b_ref[...]).astype(o_ref.dtype)

    return pl.pallas_call(
        body, grid=(n // tn, m // tm),
        in_specs=[pl.BlockSpec((tm, k), lambda j, i: (i, 0)), pl.BlockSpec((k, tn), lambda j, i: (0, j))],
        out_specs=pl.BlockSpec((tm, tn), lambda j, i: (i, j)),
        out_shape=jax.ShapeDtypeStruct((m, n), out_dtype), name=name,
        compiler_params=_cp(("parallel", "parallel")))(a, b)


def _mm_nt(a, b, name, tm=1024, tk=1536, add=None, add_scale=1.0):
    m, k = a.shape
    n, _ = b.shape
    tm, tk = _tile(m, tm), _tile(k, tk)
    has_add = add is not None

    def body(*refs):
        if has_add:
            a_ref, b_ref, add_ref, o_ref = refs
        else:
            a_ref, b_ref, o_ref = refs

        @pl.when(pl.program_id(1) == 0)
        def _():
            if has_add:
                o_ref[...] = add_ref[...] * add_scale
            else:
                o_ref[...] = jnp.zeros_like(o_ref)

        o_ref[...] += _dot_nt(a_ref[...], b_ref[...])

    in_specs = [pl.BlockSpec((tm, tk), lambda i, kk: (i, kk)), pl.BlockSpec((n, tk), lambda i, kk: (0, kk))]
    args = [a, b]
    if has_add:
        in_specs.append(pl.BlockSpec((tm, n), lambda i, kk: (i, 0)))
        args.append(add)
    return pl.pallas_call(
        body, grid=(m // tm, k // tk), in_specs=in_specs,
        out_specs=pl.BlockSpec((tm, n), lambda i, kk: (i, 0)),
        out_shape=jax.ShapeDtypeStruct((m, n), F32), name=name,
        compiler_params=_cp(("parallel", "arbitrary")))(*args)


def _mm_tn(a, b, name, tt=512, tn=1536):
    t, k = a.shape
    _, n = b.shape
    tt, tn = _tile(t, tt), _tile(n, tn)

    def body(a_ref, b_ref, o_ref):
        @pl.when(pl.program_id(1) == 0)
        def _():
            o_ref[...] = jnp.zeros_like(o_ref)

        o_ref[...] += _dot_tn(a_ref[...], b_ref[...])

    return pl.pallas_call(
        body, grid=(n // tn, t // tt),
        in_specs=[pl.BlockSpec((tt, k), lambda j, s: (s, 0)), pl.BlockSpec((tt, tn), lambda j, s: (s, j))],
        out_specs=pl.BlockSpec((k, tn), lambda j, s: (0, j)),
        out_shape=jax.ShapeDtypeStruct((k, n), F32), name=name,
        compiler_params=_cp(("parallel", "arbitrary")))(a, b)


def _ew(body, name, t, ncol, wb, ins, outs, accs=(), tt=256):
    tt = _tile(t, tt)
    nt = t // tt
    in_specs, args = [], []
    for arr, kind, coff in ins:
        if kind == "tile":
            spec = pl.BlockSpec((tt, wb), lambda j, i, c=coff: (i, c + j))
        elif kind == "prev":
            spec = pl.BlockSpec((tt, wb), lambda j, i, c=coff: (jnp.maximum(i - 1, 0), c + j))
        elif kind == "next":
            spec = pl.BlockSpec((tt, wb), lambda j, i, c=coff: (jnp.minimum(i + 1, nt - 1), c + j))
        else:
            spec = pl.BlockSpec((arr.shape[0], wb), lambda j, i, c=coff: (0, c + j))
        in_specs.append(spec)
        args.append(arr)
    out_specs = [pl.BlockSpec((tt, wb), lambda j, i: (i, j)) for _ in outs]
    out_shape = [jax.ShapeDtypeStruct((t, ncol * wb), d) for d in outs]
    for r in accs:
        out_specs.append(pl.BlockSpec((r, wb), lambda j, i: (0, j)))
        out_shape.append(jax.ShapeDtypeStruct((r, ncol * wb), F32))

    def kern(*refs):
        body(pl.program_id(1), nt, *refs)

    res = pl.pallas_call(
        kern, grid=(ncol, nt), in_specs=in_specs, out_specs=out_specs, out_shape=out_shape, name=name,
        compiler_params=_cp(("parallel", "arbitrary")))(*args)
    return res


def _silu_parts(x):
    s = _sigmoid(x)
    return x * s, s + x * s * (1.0 - s)


def _ln_fwd(x, y, g, b, name):
    t, d = x.shape
    tt = _tile(t, 256)

    def body(x_ref, y_ref, g_ref, b_ref, o_ref, ob_ref, xh_ref, r_ref):
        z = ALPHA * x_ref[...] + y_ref[...]
        mu = jnp.mean(z, axis=1, keepdims=True)
        zc = z - mu
        var = jnp.mean(zc * zc, axis=1, keepdims=True)
        rstd = lax.rsqrt(var + LN_EPS)
        xh = zc * rstd
        o = xh * g_ref[...] + b_ref[...]
        o_ref[...] = o
        ob_ref[...] = o.astype(BF16)
        xh_ref[...] = xh
        r_ref[...] = rstd

    row = pl.BlockSpec((tt, d), lambda i: (i, 0))
    vec = pl.BlockSpec((1, d), lambda i: (0, 0))
    return pl.pallas_call(
        body, grid=(t // tt,), in_specs=[row, row, vec, vec],
        out_specs=[row, row, row, pl.BlockSpec((tt, 1), lambda i: (i, 0))],
        out_shape=[jax.ShapeDtypeStruct((t, d), F32), jax.ShapeDtypeStruct((t, d), BF16),
                   jax.ShapeDtypeStruct((t, d), F32), jax.ShapeDtypeStruct((t, 1), F32)],
        name=name, compiler_params=_cp(("parallel",)))(x, y, g, b)


def _ln_bwd(dout, xhat, rstd, g, name):
    t, d = dout.shape
    tt = _tile(t, 256)

    def body(do_ref, xh_ref, r_ref, g_ref, dz_ref, dzb_ref, acc_ref):
        @pl.when(pl.program_id(0) == 0)
        def _():
            acc_ref[...] = jnp.zeros_like(acc_ref)

        do = do_ref[...]
        xh = xh_ref[...]
        dxh = do * g_ref[...]
        m1 = jnp.mean(dxh, axis=1, keepdims=True)
        m2 = jnp.mean(dxh * xh, axis=1, keepdims=True)
        dz = r_ref[...] * (dxh - m1 - xh * m2)
        dz_ref[...] = dz
        dzb_ref[...] = dz.astype(BF16)
        acc_ref[0:1, :] += jnp.sum(do * xh, axis=0, keepdims=True)
        acc_ref[1:2, :] += jnp.sum(do, axis=0, keepdims=True)

    row = pl.BlockSpec((tt, d), lambda i: (i, 0))
    return pl.pallas_call(
        body, grid=(t // tt,),
        in_specs=[row, row, pl.BlockSpec((tt, 1), lambda i: (i, 0)), pl.BlockSpec((1, d), lambda i: (0, 0))],
        out_specs=[row, row, pl.BlockSpec((SUBLANES, d), lambda i: (0, 0))],
        out_shape=[jax.ShapeDtypeStruct((t, d), F32), jax.ShapeDtypeStruct((t, d), BF16),
                   jax.ShapeDtypeStruct((SUBLANES, d), F32)],
        name=name, compiler_params=_cp(("arbitrary",)))(dout, xhat, rstd, g)


def _loss_head(y, target):
    t, d = y.shape
    tt = _tile(t, 256)

    def body(y_ref, t_ref, acc_ref, dy_ref):
        @pl.when(pl.program_id(0) == 0)
        def _():
            acc_ref[...] = jnp.zeros_like(acc_ref)

        err = y_ref[...] - t_ref[...]
        dy_ref[...] = err * (1.0 / d)
        acc_ref[0:1, :] += jnp.sum(err * err, axis=0, keepdims=True)

    row = pl.BlockSpec((tt, d), lambda i: (i, 0))
    acc, dy = pl.pallas_call(
        body, grid=(t // tt,), in_specs=[row, row],
        out_specs=[pl.BlockSpec((SUBLANES, d), lambda i: (0, 0)), row],
        out_shape=[jax.ShapeDtypeStruct((SUBLANES, d), F32), jax.ShapeDtypeStruct((t, d), F32)],
        name="loss_head", compiler_params=_cp(("arbitrary",)))(y, target)
    return acc, dy


def _tri(lower):
    r = lax.broadcasted_iota(jnp.int32, (HGRN_CHUNK, HGRN_CHUNK), 0)
    c = lax.broadcasted_iota(jnp.int32, (HGRN_CHUNK, HGRN_CHUNK), 1)
    return jnp.where((r >= c) if lower else (r <= c), 1.0, 0.0).astype(BF16)


def _exact_tri_matmul(tri, x):
    hi = x.astype(BF16)
    r1 = x - hi.astype(F32)
    mid = r1.astype(BF16)
    lo = (r1 - mid.astype(F32)).astype(BF16)
    return _dot_nn(tri, hi) + _dot_nn(tri, mid) + _dot_nn(tri, lo)


def _hgrn_gates(q_raw, fl, lb):
    sq = _sigmoid(q_raw)
    qf = q_raw * sq * (HGRN_HEAD_DIM ** -0.5)
    sg = _sigmoid(fl)
    f = lb + (1.0 - lb) * sg
    return qf, sq, sg, f


def _hgrn_fwd(u, lb, off, name):
    t = u.shape[0]
    w = lb.shape[1]
    hd, ch = HGRN_HEAD_DIM, HGRN_CHUNK
    nh, nc = w // hd, t // ch
    cq, cf, cv = off["a_q"] // hd, off["a_f"] // hd, off["a_i"] // hd

    def body(q_ref, f_ref, v_ref, lb_ref, o_ref, st_ref, state, b_scr, k_scr):
        @pl.when(pl.program_id(1) == 0)
        def _():
            state[...] = jnp.zeros_like(state)

        st = state[...]
        st_ref[0, 0] = st
        qf, _, _, f = _hgrn_gates(q_ref[...], f_ref[...], lb_ref[...])
        k = 1.0 - f
        b = _exact_tri_matmul(_tri(True), jnp.log(f))
        b_scr[...] = b
        k_scr[...] = k
        a = jnp.exp(b)
        inter = _dot_nt((qf * a).astype(BF16), st.astype(BF16))
        rows = lax.broadcasted_iota(jnp.int32, (ch, hd), 0)

        def step(s, acc):
            e = jnp.exp(jnp.where(rows >= s, b - b_scr[pl.ds(s, 1), :], NEG_INF))
            pcol = jnp.sum(qf * e * k_scr[pl.ds(s, 1), :], axis=1, keepdims=True)
            return acc + pcol * v_ref[pl.ds(s, 1), :]

        intra = lax.fori_loop(0, ch, step, jnp.zeros((ch, hd), F32))
        o_ref[...] = inter + intra
        b_end = b[ch - 1:ch, :]
        kd = k * jnp.exp(b_end - b)
        state[...] = st * jnp.exp(b_end) + _dot_tn(v_ref[...].astype(BF16), kd.astype(BF16))

    return pl.pallas_call(
        body, grid=(nh, nc),
        in_specs=[pl.BlockSpec((ch, hd), lambda h, n: (n, cq + h)),
                  pl.BlockSpec((ch, hd), lambda h, n: (n, cf + h)),
                  pl.BlockSpec((ch, hd), lambda h, n: (n, cv + h)),
                  pl.BlockSpec((1, hd), lambda h, n: (0, h))],
        out_specs=[pl.BlockSpec((ch, hd), lambda h, n: (n, h)),
                   pl.BlockSpec((1, 1, hd, hd), lambda h, n: (h, n, 0, 0))],
        out_shape=[jax.ShapeDtypeStruct((t, w), F32), jax.ShapeDtypeStruct((nh, nc, hd, hd), F32)],
        scratch_shapes=[pltpu.VMEM((hd, hd), F32), pltpu.VMEM((ch, hd), F32), pltpu.VMEM((ch, hd), F32)],
        name=name, compiler_params=_cp(("parallel", "arbitrary")))(u, u, u, lb)


def _hgrn_bwd(u, lb, states, do, off, name):
    t = u.shape[0]
    w = lb.shape[1]
    hd, ch = HGRN_HEAD_DIM, HGRN_CHUNK
    nh, nc = w // hd, t // ch
    cq, cf, cv = off["a_q"] // hd, off["a_f"] // hd, off["a_i"] // hd

    def body(q_ref, f_ref, v_ref, do_ref, st_ref, lb_ref, dq_ref, df_ref, dv_ref, dlb_ref,
             dstate, b_scr, k_scr, dk_scr, dv_scr):
        @pl.when(pl.program_id(1) == 0)
        def _():
            dstate[...] = jnp.zeros_like(dstate)
            dlb_ref[...] = jnp.zeros_like(dlb_ref)

        lb_row = lb_ref[...]
        q_raw = q_ref[...]
        qf, sq, sg, f = _hgrn_gates(q_raw, f_ref[...], lb_row)
        k = 1.0 - f
        b = _exact_tri_matmul(_tri(True), jnp.log(f))
        b_scr[...] = b
        k_scr[...] = k
        a = jnp.exp(b)
        b_end = b[ch - 1:ch, :]
        a_end = jnp.exp(b_end)
        to_end = jnp.exp(b_end - b)
        kd = k * to_end
        st0 = st_ref[0, 0]
        ds = dstate[...]
        do_v = do_ref[...]
        v = v_ref[...]
        do_b, v_b, ds_b, kd_b = do_v.astype(BF16), v.astype(BF16), ds.astype(BF16), kd.astype(BF16)
        qa = qf * a

        dq_inter = a * _dot_nn(do_b, st0.astype(BF16))
        dk_end = to_end * _dot_nn(v_b, ds_b)
        dv_end = _dot_nt(kd_b, ds_b)
        st_end = st0 * a_end + _dot_tn(v_b, kd_b)
        db_end = jnp.sum(ds * st_end, axis=0, keepdims=True)
        dstate[...] = ds * a_end + _dot_tn(do_b, qa.astype(BF16))

        rows = lax.broadcasted_iota(jnp.int32, (ch, hd), 0)

        def step(s, dq_acc):
            ks = k_scr[pl.ds(s, 1), :]
            e = jnp.exp(jnp.where(rows >= s, b - b_scr[pl.ds(s, 1), :], NEG_INF))
            qe = qf * e
            pcol = jnp.sum(qe * ks, axis=1, keepdims=True)
            dpcol = jnp.sum(do_v * v_ref[pl.ds(s, 1), :], axis=1, keepdims=True)
            dk_scr[pl.ds(s, 1), :] = jnp.sum(dpcol * qe, axis=0, keepdims=True)
            dv_scr[pl.ds(s, 1), :] = jnp.sum(pcol * do_v, axis=0, keepdims=True)
            return dq_acc + dpcol * (ks * e)

        dq_intra = lax.fori_loop(0, ch, step, jnp.zeros((ch, hd), F32))
        dqf = dq_inter + dq_intra
        dk = dk_end + dk_scr[...]
        dv = dv_end + dv_scr[...]
        db = qf * dqf - k * dk
        db = db + jnp.where(rows == ch - 1, db_end, 0.0)
        dg = _exact_tri_matmul(_tri(False), db)
        df = dg / f - dk
        dq_ref[...] = (dqf * (HGRN_HEAD_DIM ** -0.5) * (sq + q_raw * sq * (1.0 - sq))).astype(BF16)
        df_ref[...] = (df * (1.0 - lb_row) * sg * (1.0 - sg)).astype(BF16)
        dv_ref[...] = dv.astype(BF16)
        dlb_ref[0:1, :] += jnp.sum(df * (1.0 - sg), axis=0, keepdims=True)

    rev = lambda n: nc - 1 - n
    tile = lambda c: pl.BlockSpec((ch, hd), lambda h, n, c=c: (rev(n), c + h))
    return pl.pallas_call(
        body, grid=(nh, nc),
        in_specs=[tile(cq), tile(cf), tile(cv), tile(0),
                  pl.BlockSpec((1, 1, hd, hd), lambda h, n: (h, rev(n), 0, 0)),
                  pl.BlockSpec((1, hd), lambda h, n: (0, h))],
        out_specs=[tile(0), tile(0), tile(0), pl.BlockSpec((SUBLANES, hd), lambda h, n: (0, h))],
        out_shape=[jax.ShapeDtypeStruct((t, w), BF16)] * 3 + [jax.ShapeDtypeStruct((SUBLANES, w), F32)],
        scratch_shapes=[pltpu.VMEM((hd, hd), F32)] + [pltpu.VMEM((ch, hd), F32)] * 4,
        name=name, compiler_params=_cp(("parallel", "arbitrary")))(u, u, u, do, states, lb)


def _gate_a_fwd(o, u, gain, off, name):
    t, w = o.shape
    hd = HGRN_HEAD_DIM

    def body(i, nt, o_ref, g_ref, gain_ref, p_ref):
        silu, _ = _silu_parts(g_ref[...])
        for h in range(w // hd):
            sl = slice(h * hd, (h + 1) * hd)
            oh = o_ref[:, sl]
            r = lax.rsqrt(jnp.mean(oh * oh, axis=1, keepdims=True) + RMS_EPS)
            p_ref[:, sl] = (oh * r * gain_ref[:, sl] * silu[:, sl]).astype(BF16)

    return _ew(body, name, t, 1, w, [(o, "tile", 0), (u, "tile", off["a_g"] // w), (gain, "row", 0)], [BF16])[0]


def _gate_a_bwd(dp, o, u, gain, off, name):
    t, w = o.shape
    hd = HGRN_HEAD_DIM

    def body(i, nt, dp_ref, o_ref, g_ref, gain_ref, do_ref, dg_ref, acc_ref):
        @pl.when(i == 0)
        def _():
            acc_ref[...] = jnp.zeros_like(acc_ref)

        silu, dsilu = _silu_parts(g_ref[...])
        dp_v = dp_ref[...]
        for h in range(w // hd):
            sl = slice(h * hd, (h + 1) * hd)
            oh = o_ref[:, sl]
            r = lax.rsqrt(jnp.mean(oh * oh, axis=1, keepdims=True) + RMS_EPS)
            nrm = oh * r
            gn = gain_ref[:, sl]
            dph = dp_v[:, sl]
            dg_ref[:, sl] = (dph * nrm * gn * dsilu[:, sl]).astype(BF16)
            acc_ref[0:1, sl] += jnp.sum(dph * nrm * silu[:, sl], axis=0, keepdims=True)
            dn = dph * gn * silu[:, sl]
            do_ref[:, sl] = r * (dn - nrm * jnp.mean(dn * nrm, axis=1, keepdims=True))

    return _ew(body, name, t, 1, w,
               [(dp, "tile", 0), (o, "tile", 0), (u, "tile", off["a_g"] // w), (gain, "row", 0)],
               [F32, BF16], accs=[SUBLANES])


def _bucket_map():
    i = np.arange(WINDOW)[:, None]
    j = np.arange(2 * WINDOW)[None, :]
    dist = np.clip(WINDOW + i - j, 0, WINDOW - 1)
    max_exact = N_BUCKETS // 2
    logd = (np.log(np.maximum(dist, 1).astype(np.float32) / max_exact) / math.log(MAX_DISTANCE / max_exact))
    large = np.minimum(max_exact + (logd.astype(np.float32) * (N_BUCKETS - max_exact)).astype(np.int32), N_BUCKETS - 1)
    return np.where(dist < max_exact, dist, large).astype(np.int32)


def _bias_table(rel_bias, n_heads):
    bucket = jnp.asarray(_bucket_map())

    def body(rb_ref, bk_ref, o_ref):
        bk = bk_ref[...]
        for h in range(n_heads):
            def step(bi, acc):
                return jnp.where(bk == bi, rb_ref[bi, h], acc)
            o_ref[h] = lax.fori_loop(0, N_BUCKETS, step, jnp.zeros((WINDOW, 2 * WINDOW), F32))

    return pl.pallas_call(
        body, in_specs=[pl.BlockSpec(memory_space=pltpu.SMEM), pl.BlockSpec(memory_space=pltpu.VMEM)],
        out_specs=pl.BlockSpec(memory_space=pltpu.VMEM),
        out_shape=jax.ShapeDtypeStruct((n_heads, WINDOW, 2 * WINDOW), F32), name="bias_table",
        compiler_params=_cp())(rel_bias, bucket)


def _bias_grad(dbias, n_heads):
    bucket = jnp.asarray(_bucket_map())

    def body(db_ref, bk_ref, o_ref):
        bk = bk_ref[...]
        lane = lax.broadcasted_iota(jnp.int32, (1, LANES), 1)

        def step(bi, carry):
            row = jnp.zeros((1, LANES), F32)
            for h in range(n_heads):
                val = jnp.sum(jnp.where(bk == bi, db_ref[h], 0.0))
                row = jnp.where(lane == h, val, row)
            o_ref[pl.ds(bi, 1), :] = row
            return carry

        lax.fori_loop(0, N_BUCKETS, step, 0)

    return pl.pallas_call(
        body, in_specs=[pl.BlockSpec(memory_space=pltpu.VMEM), pl.BlockSpec(memory_space=pltpu.VMEM)],
        out_specs=pl.BlockSpec(memory_space=pltpu.VMEM),
        out_shape=jax.ShapeDtypeStruct((N_BUCKETS, LANES), F32), name="bias_grad",
        compiler_params=_cp())(dbias, bucket)


def _attn_probs(n, q_ref, kp_ref, kc_ref, bias_ref, sink_ref, hh, grp):
    ad, wn = ATTN_HEAD_DIM, WINDOW
    ksl = slice(hh * ad, (hh + 1) * ad)
    kw = jnp.concatenate([kp_ref[:, ksl], kc_ref[:, ksl]], axis=0).astype(BF16)
    qs = jnp.concatenate([q_ref[:, (hh * grp + g) * ad:(hh * grp + g + 1) * ad] for g in range(grp)], axis=0).astype(BF16)
    s = _dot_nt(qs, kw) * (ad ** -0.5) + bias_ref[hh]
    r = lax.broadcasted_iota(jnp.int32, (grp * wn, 2 * wn), 0)
    j = lax.broadcasted_iota(jnp.int32, (grp * wn, 2 * wn), 1)
    i = r & (wn - 1)
    valid = ((j >= wn) & (j - wn <= i)) | ((j < wn) & (j > i) & (n > 0))
    s = jnp.where(valid, s, NEG_INF)
    rr = lax.broadcasted_iota(jnp.int32, (grp * wn, 1), 0) >> WINDOW_SHIFT
    sink = jnp.zeros((grp * wn, 1), F32)
    for g in range(grp):
        sink = jnp.where(rr == g, sink_ref[hh * grp + g], sink)
    m = jnp.maximum(jnp.max(s, axis=1, keepdims=True), sink)
    p = jnp.exp(s - m)
    es = jnp.exp(sink - m)
    inv = 1.0 / (jnp.sum(p, axis=1, keepdims=True) + es)
    return qs, kw, p * inv, es * inv


def _attn_fwd(u, bias_g, sinks, off, w, name):
    t = u.shape[0]
    wn, ad, kvw = WINDOW, ATTN_HEAD_DIM, KV_WIDTH
    grp = (w // ad) // ATTN_KV_HEADS
    nb = t // wn
    cq, ck, cv = off["b_q"] // w, off["b_k"] // kvw, off["b_v"] // kvw

    def body(q_ref, kp_ref, kc_ref, vp_ref, vc_ref, bias_ref, sink_ref, o_ref):
        n = pl.program_id(0)
        for hh in range(ATTN_KV_HEADS):
            _, _, p, _ = _attn_probs(n, q_ref, kp_ref, kc_ref, bias_ref, sink_ref, hh, grp)
            ksl = slice(hh * ad, (hh + 1) * ad)
            vw = jnp.concatenate([vp_ref[:, ksl], vc_ref[:, ksl]], axis=0).astype(BF16)
            o = _dot_nn(p.astype(BF16), vw)
            for g in range(grp):
                o_ref[:, (hh * grp + g) * ad:(hh * grp + g + 1) * ad] = o[g * wn:(g + 1) * wn]

    prev = lambda n: jnp.maximum(n - 1, 0)
    return pl.pallas_call(
        body, grid=(nb,),
        in_specs=[pl.BlockSpec((wn, w), lambda n: (n, cq)),
                  pl.BlockSpec((wn, kvw), lambda n: (prev(n), ck)), pl.BlockSpec((wn, kvw), lambda n: (n, ck)),
                  pl.BlockSpec((wn, kvw), lambda n: (prev(n), cv)), pl.BlockSpec((wn, kvw), lambda n: (n, cv)),
                  pl.BlockSpec((ATTN_KV_HEADS, grp * wn, 2 * wn), lambda n: (0, 0, 0)),
                  pl.BlockSpec(memory_space=pltpu.SMEM)],
        out_specs=pl.BlockSpec((wn, w), lambda n: (n, 0)),
        out_shape=jax.ShapeDtypeStruct((t, w), F32), name=name,
        compiler_params=_cp(("parallel",)))(u, u, u, u, u, bias_g, sinks)


def _attn_bwd(u, o, do, bias_g, sinks, off, w, name):
    t = u.shape[0]
    wn, ad, kvw = WINDOW, ATTN_HEAD_DIM, KV_WIDTH
    grp = (w // ad) // ATTN_KV_HEADS
    nb = t // wn
    cq, ck, cv = off["b_q"] // w, off["b_k"] // kvw, off["b_v"] // kvw

    def body(q_ref, kp_ref, kc_ref, vp_ref, vc_ref, o_ref, do_ref, bias_ref, sink_ref,
             dq_ref, dkc_ref, dkp_ref, dvc_ref, dvp_ref, dbias_ref, dsink_ref):
        n = pl.program_id(0)

        @pl.when(n == 0)
        def _():
            dbias_ref[...] = jnp.zeros_like(dbias_ref)
            dsink_ref[...] = jnp.zeros_like(dsink_ref)

        lane = lax.broadcasted_iota(jnp.int32, (1, LANES), 1)
        rr = lax.broadcasted_iota(jnp.int32, (grp * wn, 1), 0) >> WINDOW_SHIFT
        dsink_row = jnp.zeros((1, LANES), F32)
        for hh in range(ATTN_KV_HEADS):
            qs, kw, p, psink = _attn_probs(n, q_ref, kp_ref, kc_ref, bias_ref, sink_ref, hh, grp)
            ksl = slice(hh * ad, (hh + 1) * ad)
            vw = jnp.concatenate([vp_ref[:, ksl], vc_ref[:, ksl]], axis=0).astype(BF16)
            hs = [slice((hh * grp + g) * ad, (hh * grp + g + 1) * ad) for g in range(grp)]
            dos = jnp.concatenate([do_ref[:, sl] for sl in hs], axis=0)
            os_ = jnp.concatenate([o_ref[:, sl] for sl in hs], axis=0)
            delta = jnp.sum(dos * os_, axis=1, keepdims=True)
            dos_b = dos.astype(BF16)
            dp = _dot_nt(dos_b, vw)
            ds = p * (dp - delta)
            dbias_ref[hh] += ds
            sd = psink * delta
            for g in range(grp):
                val = -jnp.sum(jnp.where(rr == g, sd, 0.0))
                dsink_row = jnp.where(lane == hh * grp + g, val, dsink_row)
            ds_b = (ds * (ad ** -0.5)).astype(BF16)
            dq = _dot_nn(ds_b, kw)
            for g in range(grp):
                dq_ref[:, hs[g]] = dq[g * wn:(g + 1) * wn].astype(BF16)
            dkw = _dot_tn(ds_b, qs)
            dvw = _dot_tn(p.astype(BF16), dos_b)
            dkp_ref[:, ksl] = dkw[:wn]
            dkc_ref[:, ksl] = dkw[wn:]
            dvp_ref[:, ksl] = dvw[:wn]
            dvc_ref[:, ksl] = dvw[wn:]
        dsink_ref[0:1, :] += dsink_row

    prev = lambda n: jnp.maximum(n - 1, 0)
    kv_out = pl.BlockSpec((wn, kvw), lambda n: (n, 0))
    return pl.pallas_call(
        body, grid=(nb,),
        in_specs=[pl.BlockSpec((wn, w), lambda n: (n, cq)),
                  pl.BlockSpec((wn, kvw), lambda n: (prev(n), ck)), pl.BlockSpec((wn, kvw), lambda n: (n, ck)),
                  pl.BlockSpec((wn, kvw), lambda n: (prev(n), cv)), pl.BlockSpec((wn, kvw), lambda n: (n, cv)),
                  pl.BlockSpec((wn, w), lambda n: (n, 0)), pl.BlockSpec((wn, w), lambda n: (n, 0)),
                  pl.BlockSpec((ATTN_KV_HEADS, grp * wn, 2 * wn), lambda n: (0, 0, 0)),
                  pl.BlockSpec(memory_space=pltpu.SMEM)],
        out_specs=[pl.BlockSpec((wn, w), lambda n: (n, 0)), kv_out, kv_out, kv_out, kv_out,
                   pl.BlockSpec((ATTN_KV_HEADS, grp * wn, 2 * wn), lambda n: (0, 0, 0)),
                   pl.BlockSpec((SUBLANES, LANES), lambda n: (0, 0))],
        out_shape=[jax.ShapeDtypeStruct((t, w), BF16)] + [jax.ShapeDtypeStruct((t, kvw), F32)] * 4
        + [jax.ShapeDtypeStruct((ATTN_KV_HEADS, grp * wn, 2 * wn), F32), jax.ShapeDtypeStruct((SUBLANES, LANES), F32)],
        name=name, compiler_params=_cp(("arbitrary",)))(u, u, u, u, u, o, do, bias_g, sinks)


def _kv_combine(cur, prv, name):
    t, kvw = cur.shape

    def body(i, nt, c_ref, p_ref, o_ref):
        nxt = jnp.where(i < nt - 1, p_ref[...], 0.0)
        o_ref[...] = (c_ref[...] + nxt).astype(BF16)

    return _ew(body, name, t, 1, kvw, [(cur, "tile", 0), (prv, "next", 0)], [BF16], tt=WINDOW)[0]


def _gate_b_fwd(o, u, off, name):
    t, w = o.shape
    wb = 512

    def body(i, nt, o_ref, g_ref, p_ref):
        silu, _ = _silu_parts(g_ref[...])
        p_ref[...] = (o_ref[...] * silu).astype(BF16)

    return _ew(body, name, t, w // wb, wb, [(o, "tile", 0), (u, "tile", off["b_g"] // wb)], [BF16])[0]


def _gate_b_bwd(dp, o, u, off, name):
    t, w = o.shape
    wb = 512

    def body(i, nt, dp_ref, o_ref, g_ref, do_ref, dg_ref):
        silu, dsilu = _silu_parts(g_ref[...])
        dp_v = dp_ref[...]
        do_ref[...] = dp_v * silu
        dg_ref[...] = (dp_v * o_ref[...] * dsilu).astype(BF16)

    return _ew(body, name, t, w // wb, wb,
               [(dp, "tile", 0), (o, "tile", 0), (u, "tile", off["b_g"] // wb)], [F32, BF16])


def _shift_down(h, tail, k, rows):
    tt = h.shape[0]
    out = pltpu.roll(h, k, 0)
    for r in range(k):
        out = jnp.where(rows == r, tail[tt - k + r:tt - k + r + 1, :], out)
    return out


def _shift_up(h, head, k, rows):
    tt = h.shape[0]
    out = pltpu.roll(h, tt - k, 0)
    for r in range(k):
        out = jnp.where(rows == tt - k + r, head[r:r + 1, :], out)
    return out


def _conv_fwd(u, conv_w, off, w, name):
    t = u.shape[0]
    wb = 512
    c = lambda nme: off[nme] // wb

    def body(i, nt, cb_ref, cc_ref, ccp_ref, cx_ref, cxp_ref, cg_ref, w_ref, p_ref):
        h = cc_ref[...] * cx_ref[...]
        hp = jnp.where(i > 0, ccp_ref[...] * cxp_ref[...], 0.0)
        rows = lax.broadcasted_iota(jnp.int32, h.shape, 0)
        y = w_ref[0:1, :] * _shift_down(h, hp, 2, rows) + w_ref[1:2, :] * _shift_down(h, hp, 1, rows) + w_ref[2:3, :] * h
        silu, _ = _silu_parts(cg_ref[...])
        p_ref[...] = (cb_ref[...] * y * silu).astype(BF16)

    return _ew(body, name, t, w // wb, wb,
               [(u, "tile", c("c_b")), (u, "tile", c("c_c")), (u, "prev", c("c_c")), (u, "tile", c("c_x")),
                (u, "prev", c("c_x")), (u, "tile", c("c_g")), (conv_w, "row", 0)], [BF16])[0]


def _conv_bwd(dp, u, conv_w, off, w, name):
    t = u.shape[0]
    wb = 512
    c = lambda nme: off[nme] // wb

    def body(i, nt, dp_ref, dpn_ref, cb_ref, cbn_ref, cg_ref, cgn_ref, cc_ref, ccp_ref, cx_ref, cxp_ref, w_ref,
             dcb_ref, dcc_ref, dcx_ref, dcg_ref, acc_ref):
        @pl.when(i == 0)
        def _():
            acc_ref[...] = jnp.zeros_like(acc_ref)

        cc, cx, cb = cc_ref[...], cx_ref[...], cb_ref[...]
        h = cc * cx
        hp = jnp.where(i > 0, ccp_ref[...] * cxp_ref[...], 0.0)
        rows = lax.broadcasted_iota(jnp.int32, h.shape, 0)
        h1 = _shift_down(h, hp, 1, rows)
        h2 = _shift_down(h, hp, 2, rows)
        w0, w1, w2 = w_ref[0:1, :], w_ref[1:2, :], w_ref[2:3, :]
        y = w0 * h2 + w1 * h1 + w2 * h
        silu, dsilu = _silu_parts(cg_ref[...])
        dp_v = dp_ref[...]
        dcg_ref[...] = (dp_v * cb * y * dsilu).astype(BF16)
        dcb_ref[...] = (dp_v * y * silu).astype(BF16)
        dy = dp_v * cb * silu
        silu_n, _ = _silu_parts(cgn_ref[...])
        dyn = jnp.where(i < nt - 1, dpn_ref[...] * cbn_ref[...] * silu_n, 0.0)
        dh = w2 * dy + w1 * _shift_up(dy, dyn, 1, rows) + w0 * _shift_up(dy, dyn, 2, rows)
        dcc_ref[...] = (dh * cx).astype(BF16)
        dcx_ref[...] = (dh * cc).astype(BF16)
        acc_ref[0:1, :] += jnp.sum(dy * h2, axis=0, keepdims=True)
        acc_ref[1:2, :] += jnp.sum(dy * h1, axis=0, keepdims=True)
        acc_ref[2:3, :] += jnp.sum(dy * h, axis=0, keepdims=True)

    return _ew(body, name, t, w // wb, wb,
               [(dp, "tile", 0), (dp, "next", 0), (u, "tile", c("c_b")), (u, "next", c("c_b")),
                (u, "tile", c("c_g")), (u, "next", c("c_g")), (u, "tile", c("c_c")), (u, "prev", c("c_c")),
                (u, "tile", c("c_x")), (u, "prev", c("c_x")), (conv_w, "row", 0)],
               [BF16] * 4, accs=[SUBLANES])


def _merge_fwd(u, ya, yb, yc, off, d, name):
    t = u.shape[0]
    wb = 512
    c = lambda nme: off[nme] // wb

    def body(i, nt, ma_ref, mb_ref, mc_ref, ya_ref, yb_ref, yc_ref, o_ref):
        o_ref[...] = (_sigmoid(ma_ref[...]) * ya_ref[...] + _sigmoid(mb_ref[...]) * yb_ref[...]
                      + _sigmoid(mc_ref[...]) * yc_ref[...]).astype(BF16)

    return _ew(body, name, t, d // wb, wb,
               [(u, "tile", c("m_a")), (u, "tile", c("m_b")), (u, "tile", c("m_c")),
                (ya, "tile", 0), (yb, "tile", 0), (yc, "tile", 0)], [BF16])[0]


def _merge_bwd(dm, u, ya, yb, yc, off, d, name):
    t = u.shape[0]
    wb = 512
    c = lambda nme: off[nme] // wb

    def body(i, nt, dm_ref, ma_ref, mb_ref, mc_ref, ya_ref, yb_ref, yc_ref, da_ref, db_ref, dc_ref, ga_ref, gb_ref, gc_ref):
        dm_v = dm_ref[...]
        for m_ref, y_ref, dy_ref, dg_ref in ((ma_ref, ya_ref, da_ref, ga_ref), (mb_ref, yb_ref, db_ref, gb_ref),
                                             (mc_ref, yc_ref, dc_ref, gc_ref)):
            s = _sigmoid(m_ref[...])
            dy_ref[...] = (dm_v * s).astype(BF16)
            dg_ref[...] = (dm_v * y_ref[...] * s * (1.0 - s)).astype(BF16)

    return _ew(body, name, t, d // wb, wb,
               [(dm, "tile", 0), (u, "tile", c("m_a")), (u, "tile", c("m_b")), (u, "tile", c("m_c")),
                (ya, "tile", 0), (yb, "tile", 0), (yc, "tile", 0)], [BF16] * 6)


def _lower_bounds(lb_param):
    def body(p_ref, o_ref):
        p = p_ref[...]
        e = jnp.exp(p - jnp.max(p, axis=0, keepdims=True))
        soft = e / jnp.sum(e, axis=0, keepdims=True)
        acc = jnp.zeros_like(soft[0:1])
        o_ref[0:1, :] = acc
        for l in range(1, DEPTH):
            acc = acc + soft[l:l + 1]
            o_ref[l:l + 1, :] = acc

    return pl.pallas_call(body, out_shape=jax.ShapeDtypeStruct(lb_param.shape, F32), name="lower_bounds",
                          compiler_params=_cp())(lb_param)


def _lower_bounds_bwd(lb_param, dlower):
    def body(p_ref, d_ref, o_ref):
        p = p_ref[...]
        e = jnp.exp(p - jnp.max(p, axis=0, keepdims=True))
        soft = e / jnp.sum(e, axis=0, keepdims=True)
        dl = d_ref[...]
        ds = [jnp.zeros_like(dl[0:1])]
        for j in range(1, DEPTH):
            acc = dl[j:j + 1]
            for l in range(j + 1, DEPTH):
                acc = acc + dl[l:l + 1]
            ds.append(acc)
        inner = ds[0] * soft[0:1]
        for j in range(1, DEPTH):
            inner = inner + ds[j] * soft[j:j + 1]
        for j in range(DEPTH):
            o_ref[j:j + 1, :] = soft[j:j + 1] * (ds[j] - inner)

    return pl.pallas_call(body, out_shape=jax.ShapeDtypeStruct(lb_param.shape, F32), name="lower_bounds_bwd",
                          compiler_params=_cp())(lb_param, dlower)


def _exchange(arrays, scatter, name):
    n_arr = len(arrays)

    def body(*refs):
        srcs, dsts = refs[:n_arr], refs[n_arr:2 * n_arr]
        send_sems, recv_sems, local_sems = refs[2 * n_arr:]
        x, y, c = lax.axis_index("x"), lax.axis_index("y"), lax.axis_index("c")
        me = 4 * x + 2 * y + c
        copies = []
        for a in range(n_arr):
            for k in range(1, N_DEV):
                px = 1 - x if k & 4 else x
                py = 1 - y if k & 2 else y
                pc = 1 - c if k & 1 else c
                src = srcs[a].at[4 * px + 2 * py + pc] if scatter else srcs[a]
                cp = pltpu.make_async_remote_copy(
                    src_ref=src, dst_ref=dsts[a].at[me],
                    send_sem=send_sems.at[a * (N_DEV - 1) + k - 1], recv_sem=recv_sems.at[a * (N_DEV - 1) + k - 1],
                    device_id=(px, py, pc), device_id_type=pl.DeviceIdType.MESH)
                cp.start()
                copies.append(cp)
            own = pltpu.make_async_copy(srcs[a].at[me] if scatter else srcs[a], dsts[a].at[me], local_sems.at[a])
            own.start()
            copies.append(own)
        for cp in copies:
            cp.wait()

    out_shape = [jax.ShapeDtypeStruct(a.shape if scatter else (N_DEV,) + a.shape, a.dtype) for a in arrays]
    anyspec = pl.BlockSpec(memory_space=pl.ANY)
    res = pl.pallas_call(
        body, in_specs=[anyspec] * n_arr, out_specs=[anyspec] * n_arr, out_shape=out_shape,
        scratch_shapes=[pltpu.SemaphoreType.DMA((n_arr * (N_DEV - 1),)), pltpu.SemaphoreType.DMA((n_arr * (N_DEV - 1),)),
                        pltpu.SemaphoreType.DMA((n_arr,))],
        name=name)(*arrays)
    return list(res)


def _unshard_cols(g, name):
    nd, r, s = g.shape
    tr = _tile(r, 64)

    def body(i_ref, o_ref):
        for p in range(nd):
            o_ref[:, p * s:(p + 1) * s] = i_ref[p]

    return pl.pallas_call(
        body, grid=(r // tr,), in_specs=[pl.BlockSpec((nd, tr, s), lambda i: (0, i, 0))],
        out_specs=pl.BlockSpec((tr, nd * s), lambda i: (i, 0)),
        out_shape=jax.ShapeDtypeStruct((r, nd * s), g.dtype), name=name, compiler_params=_cp(("parallel",)))(g)


def _shard_cols(g, name):
    r, n = g.shape
    s = n // N_DEV
    tr = _tile(r, 64)

    def body(i_ref, o_ref):
        for p in range(N_DEV):
            o_ref[p] = i_ref[:, p * s:(p + 1) * s]

    return pl.pallas_call(
        body, grid=(r // tr,), in_specs=[pl.BlockSpec((tr, n), lambda i: (i, 0))],
        out_specs=pl.BlockSpec((N_DEV, tr, s), lambda i: (0, i, 0)),
        out_shape=jax.ShapeDtypeStruct((N_DEV, r, s), g.dtype), name=name, compiler_params=_cp(("parallel",)))(g)


def _slot_sum(slots, name):
    nd, r, c = slots.shape
    tr = _tile(r, 64)

    def body(s_ref, o_ref):
        acc = s_ref[0]
        for p in range(1, nd):
            acc = acc + s_ref[p]
        o_ref[...] = acc

    return pl.pallas_call(
        body, grid=(r // tr,), in_specs=[pl.BlockSpec((nd, tr, c), lambda i: (0, i, 0))],
        out_specs=pl.BlockSpec((tr, c), lambda i: (i, 0)),
        out_shape=jax.ShapeDtypeStruct((r, c), F32), name=name, compiler_params=_cp(("parallel",)))(slots)


def _adamw(w, g, m, v, name):
    r, c = w.shape
    tr = _tile(r, 256)
    c1 = 1.0 - ADAM_B1 ** ADAM_STEP
    c2 = 1.0 - ADAM_B2 ** ADAM_STEP

    def body(w_ref, g_ref, m_ref, v_ref, d_ref, nm_ref, nv_ref):
        gv = g_ref[...]
        nm = ADAM_B1 * m_ref[...] + (1.0 - ADAM_B1) * gv
        nv = ADAM_B2 * v_ref[...] + (1.0 - ADAM_B2) * (gv * gv)
        nm_ref[...] = nm
        nv_ref[...] = nv
        d_ref[...] = -ADAM_LR * ((nm / c1) / (jnp.sqrt(nv / c2) + ADAM_EPS) + ADAM_WD * w_ref[...])

    spec = pl.BlockSpec((tr, c), lambda i: (i, 0))
    return pl.pallas_call(
        body, grid=(r // tr,), in_specs=[spec] * 4, out_specs=[spec] * 3,
        out_shape=[jax.ShapeDtypeStruct((r, c), F32)] * 3, name=name, compiler_params=_cp(("parallel",)))(w, g, m, v)


def _forward_backward(x, target, weights, lb_param, hgrn_norm_g, attn_sinks, conv_w, rel_bias, ln_g, ln_b):
    t, d = x.shape
    w = d // 2
    off, n_in = _offsets(d)
    n_heads = w // ATTN_HEAD_DIM
    grp = n_heads // ATTN_KV_HEADS

    lower = _lower_bounds(lb_param)
    bias = _bias_table(rel_bias, n_heads)
    bias_g = bias.reshape(ATTN_KV_HEADS, grp * WINDOW, 2 * WINDOW)

    saved = []
    xb = x.astype(BF16)
    for l in range(DEPTH):
        wl = weights[l]
        s = {"x": x, "xb": xb}
        u = _mm_nn(xb, wl["w_in"], f"in_proj_{l}")
        s["u"] = u
        lb_l, gain_l, cw_l = lower[l:l + 1], hgrn_norm_g[l:l + 1], conv_w[l]
        o_a, states = _hgrn_fwd(u, lb_l, off, f"hgrn_fwd_{l}")
        p_a = _gate_a_fwd(o_a, u, gain_l, off, f"gate_a_fwd_{l}")
        o_b = _attn_fwd(u, bias_g, attn_sinks[l], off, w, f"attn_fwd_{l}")
        p_b = _gate_b_fwd(o_b, u, off, f"gate_b_fwd_{l}")
        p_c = _conv_fwd(u, cw_l, off, w, f"conv_fwd_{l}")
        y_a = _mm_nn(p_a, wl["w_proj_hgrn"], f"proj_a_{l}", tn=2048)
        y_b = _mm_nn(p_b, wl["w_proj_attn"], f"proj_b_{l}", tn=2048)
        y_c = _mm_nn(p_c, wl["w_proj_conv"], f"proj_c_{l}", tn=2048)
        merged = _merge_fwd(u, y_a, y_b, y_c, off, d, f"merge_fwd_{l}")
        y = _mm_nn(merged, wl["w_out"], f"out_proj_{l}", tm=512, tn=2048)
        x, xb, xhat, rstd = _ln_fwd(x, y, ln_g[l:l + 1], ln_b[l:l + 1], f"ln_fwd_{l}")
        s.update(o_a=o_a, states=states, p_a=p_a, o_b=o_b, p_b=p_b, p_c=p_c, y_a=y_a, y_b=y_b, y_c=y_c,
                 merged=merged, xhat=xhat, rstd=rstd)
        saved.append(s)

    loss_acc, dx = _loss_head(x, target)

    big = [None] * DEPTH
    d_ln, d_lower, d_gain, d_sink, d_conv = [None] * DEPTH, [None] * DEPTH, [None] * DEPTH, [None] * DEPTH, [None] * DEPTH
    dbias_total = None
    for l in reversed(range(DEPTH)):
        wl, s = weights[l], saved[l]
        u = s["u"]
        lb_l, gain_l, cw_l = lower[l:l + 1], hgrn_norm_g[l:l + 1], conv_w[l]
        dz, dzb, d_ln[l] = _ln_bwd(dx, s["xhat"], s["rstd"], ln_g[l:l + 1], f"ln_bwd_{l}")
        g_out = _mm_tn(s["merged"], dzb, f"g_out_{l}", tn=2048)
        dmerged = _mm_nt(dzb, wl["w_out"], f"d_merged_{l}", tk=2048)
        dya, dyb, dyc, dma, dmb, dmc = _merge_bwd(dmerged, u, s["y_a"], s["y_b"], s["y_c"], off, d, f"merge_bwd_{l}")
        g_pa = _mm_tn(s["p_a"], dya, f"g_proj_a_{l}", tn=2048)
        g_pb = _mm_tn(s["p_b"], dyb, f"g_proj_b_{l}", tn=2048)
        g_pc = _mm_tn(s["p_c"], dyc, f"g_proj_c_{l}", tn=2048)
        dpa = _mm_nt(dya, wl["w_proj_hgrn"], f"d_p_a_{l}", tk=2048)
        dpb = _mm_nt(dyb, wl["w_proj_attn"], f"d_p_b_{l}", tk=2048)
        dpc = _mm_nt(dyc, wl["w_proj_conv"], f"d_p_c_{l}", tk=2048)
        do_a, d_ag, d_gain[l] = _gate_a_bwd(dpa, s["o_a"], u, gain_l, off, f"gate_a_bwd_{l}")
        d_aq, d_af, d_ai, d_lower[l] = _hgrn_bwd(u, lb_l, s["states"], do_a, off, f"hgrn_bwd_{l}")
        do_b, d_bg = _gate_b_bwd(dpb, s["o_b"], u, off, f"gate_b_bwd_{l}")
        d_bq, dkc, dkp, dvc, dvp, dbias_l, d_sink[l] = _attn_bwd(u, s["o_b"], do_b, bias_g, attn_sinks[l], off, w, f"attn_bwd_{l}")
        d_bk = _kv_combine(dkc, dkp, f"k_combine_{l}")
        d_bv = _kv_combine(dvc, dvp, f"v_combine_{l}")
        dbias_total = dbias_l if dbias_total is None else dbias_total + dbias_l
        d_cb, d_cc, d_cx, d_cg, d_conv[l] = _conv_bwd(dpc, u, cw_l, off, w, f"conv_bwd_{l}")
        du = jnp.concatenate([d_aq, d_af, d_ai, d_ag, d_bq, d_bk, d_bv, d_bg, d_cb, d_cc, d_cx, d_cg, dma, dmb, dmc], axis=1)
        g_in = _mm_tn(s["xb"], du, f"g_in_{l}")
        dx = _mm_nt(du, wl["w_in"], f"d_x_{l}", add=dz, add_scale=ALPHA)
        big[l] = {"w_in": g_in, "w_proj_hgrn": g_pa, "w_proj_attn": g_pb, "w_proj_conv": g_pc, "w_out": g_out}

    d_lower_all = jnp.concatenate([a[0:1] for a in d_lower], axis=0)
    small = {
        "lb_param": _lower_bounds_bwd(lb_param, d_lower_all),
        "hgrn_norm_g": jnp.concatenate([a[0:1] for a in d_gain], axis=0),
        "attn_sinks": jnp.concatenate([a[0:1, :n_heads] for a in d_sink], axis=0),
        "conv_w": jnp.stack([a[0:3] for a in d_conv], axis=0),
        "rel_bias": _bias_grad(dbias_total.reshape(n_heads, WINDOW, 2 * WINDOW), n_heads)[:, :n_heads],
        "ln_g": jnp.concatenate([a[0:1] for a in d_ln], axis=0),
        "ln_b": jnp.concatenate([a[1:2] for a in d_ln], axis=0),
    }
    return loss_acc, dx, big, small


BIG = ("w_in", "w_proj_hgrn", "w_proj_attn", "w_proj_conv", "w_out")
SMALL = ("lb_param", "hgrn_norm_g", "attn_sinks", "conv_w", "rel_bias", "ln_g", "ln_b")
ORDER = ("w_in", "w_proj_hgrn", "w_proj_attn", "w_proj_conv", "w_out", "lb_param", "hgrn_norm_g", "attn_sinks",
         "conv_w", "rel_bias", "ln_g", "ln_b")


def _pack(parts):
    flat = jnp.concatenate([p.reshape(-1) for p in parts])
    n = flat.shape[0]
    unit = SUBLANES * LANES
    total = -(-n // unit) * unit
    return jnp.pad(flat, (0, total - n)).reshape(total // LANES, LANES)


def _unpack(packed, shapes):
    flat = packed.reshape(-1)
    out, o = [], 0
    for shp in shapes:
        n = int(np.prod(shp))
        out.append(flat[o:o + n].reshape(shp))
        o += n
    return out


def kernel(x, w_in, w_proj_hgrn, w_proj_attn, w_proj_conv, w_out, lb_param, hgrn_norm_g, attn_sinks, conv_w, rel_bias, ln_g, ln_b, loss_target, m_w_in, m_w_proj_hgrn, m_w_proj_attn, m_w_proj_conv, m_w_out, m_lb_param, m_hgrn_norm_g, m_attn_sinks, m_conv_w, m_rel_bias, m_ln_g, m_ln_b, v_w_in, v_w_proj_hgrn, v_w_proj_attn, v_w_proj_conv, v_w_out, v_lb_param, v_hgrn_norm_g, v_attn_sinks, v_conv_w, v_rel_bias, v_ln_g, v_ln_b):
    params = dict(w_in=w_in, w_proj_hgrn=w_proj_hgrn, w_proj_attn=w_proj_attn, w_proj_conv=w_proj_conv, w_out=w_out,
                  lb_param=lb_param, hgrn_norm_g=hgrn_norm_g, attn_sinks=attn_sinks, conv_w=conv_w, rel_bias=rel_bias,
                  ln_g=ln_g, ln_b=ln_b)
    mom_m = dict(w_in=m_w_in, w_proj_hgrn=m_w_proj_hgrn, w_proj_attn=m_w_proj_attn, w_proj_conv=m_w_proj_conv,
                 w_out=m_w_out, lb_param=m_lb_param, hgrn_norm_g=m_hgrn_norm_g, attn_sinks=m_attn_sinks,
                 conv_w=m_conv_w, rel_bias=m_rel_bias, ln_g=m_ln_g, ln_b=m_ln_b)
    mom_v = dict(w_in=v_w_in, w_proj_hgrn=v_w_proj_hgrn, w_proj_attn=v_w_proj_attn, w_proj_conv=v_w_proj_conv,
                 w_out=v_w_out, lb_param=v_lb_param, hgrn_norm_g=v_hgrn_norm_g, attn_sinks=v_attn_sinks,
                 conv_w=v_conv_w, rel_bias=v_rel_bias, ln_g=v_ln_g, ln_b=v_ln_b)
    d = x.shape[-1]
    me = 4 * lax.axis_index("x") + 2 * lax.axis_index("y") + lax.axis_index("c")

    weights = []
    for l in range(DEPTH):
        shards = [params[n][l].astype(BF16) for n in BIG] + [conv_w[l]]
        got = _exchange(shards, False, f"gather_weights_{l}")
        wl = {
            "w_in": _unshard_cols(got[0], f"unshard_w_in_{l}"),
            "w_proj_hgrn": _unshard_cols(got[1], f"unshard_w_proj_hgrn_{l}"),
            "w_proj_attn": _unshard_cols(got[2], f"unshard_w_proj_attn_{l}"),
            "w_proj_conv": _unshard_cols(got[3], f"unshard_w_proj_conv_{l}"),
            "w_out": got[4].reshape(d, d),
            "conv_w": _unshard_cols(got[5], f"unshard_conv_w_{l}"),
        }
        weights.append(wl)
    conv_full = jnp.stack([wl["conv_w"] for wl in weights], axis=0)

    loss_acc, dx, big, small = _forward_backward(
        x[0], loss_target[0], weights, lb_param, hgrn_norm_g, attn_sinks, conv_full, rel_bias, ln_g, ln_b)
    loss = lax.psum(0.5 * jnp.sum(loss_acc[0]) / d, ("x", "y", "c"))

    grads = {n: [] for n in ORDER}
    for l in range(DEPTH):
        g = big[l]
        send = [_shard_cols(g["w_in"], f"shard_g_in_{l}"), _shard_cols(g["w_proj_hgrn"], f"shard_g_proj_a_{l}"),
                _shard_cols(g["w_proj_attn"], f"shard_g_proj_b_{l}"), _shard_cols(g["w_proj_conv"], f"shard_g_proj_c_{l}"),
                g["w_out"].reshape(N_DEV, d // N_DEV, d)]
        got = _exchange(send, True, f"scatter_grads_{l}")
        for n, slots in zip(BIG, got):
            grads[n].append(_slot_sum(slots, f"sum_{n}_{l}"))
    for n in BIG:
        grads[n] = jnp.stack(grads[n], axis=0)

    small_shapes = [small[n].shape for n in SMALL]
    packed = _pack([small[n] for n in SMALL])
    got = _exchange([packed], False, "gather_small_grads")[0]
    summed = _unpack(_slot_sum(got, "sum_small_grads"), small_shapes)
    for n, g in zip(SMALL, summed):
        grads[n] = g
    cs = conv_w.shape[-1]
    grads["conv_w"] = lax.dynamic_slice_in_dim(grads["conv_w"], me * cs, cs, axis=2)

    delta, new_m, new_v = {}, {}, {}
    for n in BIG:
        shp = params[n].shape
        flat = lambda a: a.reshape(-1, shp[-1])
        dl, nm, nv = _adamw(flat(params[n]), flat(grads[n]), flat(mom_m[n]), flat(mom_v[n]), f"adamw_{n}")
        delta[n], new_m[n], new_v[n] = dl.reshape(shp), nm.reshape(shp), nv.reshape(shp)
    shapes = [params[n].shape for n in SMALL]
    res = _adamw(_pack([params[n] for n in SMALL]), _pack([grads[n] for n in SMALL]),
                 _pack([mom_m[n] for n in SMALL]), _pack([mom_v[n] for n in SMALL]), "adamw_small")
    for dst, packed_res in zip((delta, new_m, new_v), res):
        for n, a in zip(SMALL, _unpack(packed_res, shapes)):
            dst[n] = a

    return (loss, dx[None], *[grads[n] for n in ORDER], *[delta[n] for n in ORDER],
            *[new_m[n] for n in ORDER], *[new_v[n] for n in ORDER])
```

```python
import functools
import math

import numpy as np
import jax
import jax.numpy as jnp
from jax import lax
from jax.experimental import pallas as pl
from jax.experimental.pallas import tpu as pltpu

F32 = jnp.float32
BF16 = jnp.bfloat16

N_DEV = 8
DEPTH = 4
HGRN_HEAD_DIM = 128
HGRN_CHUNK = 64
ATTN_HEAD_DIM = 64
ATTN_KV_HEADS = 4
KV_WIDTH = ATTN_KV_HEADS * ATTN_HEAD_DIM
WINDOW = 128
WINDOW_SHIFT = 7
N_BUCKETS = 32
MAX_DISTANCE = 128
ALPHA = (2.0 * DEPTH) ** 0.25
LN_EPS = 1e-5
RMS_EPS = 1e-6
ADAM_LR = 0.001
ADAM_B1 = 0.9
ADAM_B2 = 0.999
ADAM_EPS = 1e-08
ADAM_WD = 0.01
ADAM_STEP = 10

LANES = 128
SUBLANES = 8
VMEM_LIMIT = 56 << 20
NEG_INF = float("-inf")


def _offsets(d_model):
    w = d_model // 2
    sizes = (w, w, w, w, w, KV_WIDTH, KV_WIDTH, w, w, w, w, w, d_model, d_model, d_model)
    names = ("a_q", "a_f", "a_i", "a_g", "b_q", "b_k", "b_v", "b_g", "c_b", "c_c", "c_x", "c_g", "m_a", "m_b", "m_c")
    off, o = {}, 0
    for n, s in zip(names, sizes):
        off[n] = o
        o += s
    return off, o


def _tile(n, pref):
    t = min(pref, n)
    while n % t:
        t //= 2
    return t


def _cp(sem=None, vmem=VMEM_LIMIT):
    return pltpu.CompilerParams(dimension_semantics=sem, vmem_limit_bytes=vmem)


def _sigmoid(x):
    return 1.0 / (1.0 + jnp.exp(-x))


def _dot_nn(a, b):
    return jnp.dot(a, b, preferred_element_type=F32)


def _dot_nt(a, b):
    return lax.dot_general(a, b, (((1,), (1,)), ((), ())), preferred_element_type=F32)


def _dot_tn(a, b):
    return lax.dot_general(a, b, (((0,), (0,)), ((), ())), preferred_element_type=F32)


def _mm_nn(a, b, name, out_dtype=F32, tm=1024, tn=1536):
    m, k = a.shape
    _, n = b.shape
    tm, tn = _tile(m, tm), _tile(n, tn)

    def body(a_ref, b_ref, o_ref):
        o_ref[...] = _dot_nn(a_ref[...], b_ref[...]).astype(o_ref.dtype)

    return pl.pallas_call(
        body, grid=(n // tn, m // tm),
        in_specs=[pl.BlockSpec((tm, k), lambda j, i: (i, 0)), pl.BlockSpec((k, tn), lambda j, i: (0, j))],
        out_specs=pl.BlockSpec((tm, tn), lambda j, i: (i, j)),
        out_shape=jax.ShapeDtypeStruct((m, n), out_dtype), name=name,
        compiler_params=_cp(("parallel", "parallel")))(a, b)


def _mm_nt(a, b, name, tm=1024, tk=1536, add=None, add_scale=1.0):
    m, k = a.shape
    n, _ = b.shape
    tm, tk = _tile(m, tm), _tile(k, tk)
    has_add = add is not None

    def body(*refs):
        if has_add:
            a_ref, b_ref, add_ref, o_ref = refs
        else:
            a_ref, b_ref, o_ref = refs

        @pl.when(pl.program_id(1) == 0)
        def _():
            if has_add:
                o_ref[...] = add_ref[...] * add_scale
            else:
                o_ref[...] = jnp.zeros_like(o_ref)

        o_ref[...] += _dot_nt(a_ref[...], b_ref[...])

    in_specs = [pl.BlockSpec((tm, tk), lambda i, kk: (i, kk)), pl.BlockSpec((n, tk), lambda i, kk: (0, kk))]
    args = [a, b]
    if has_add:
        in_specs.append(pl.BlockSpec((tm, n), lambda i, kk: (i, 0)))
        args.append(add)
    return pl.pallas_call(
        body, grid=(m // tm, k // tk), in_specs=in_specs,
        out_specs=pl.BlockSpec((tm, n), lambda i, kk: (i, 0)),
        out_shape=jax.ShapeDtypeStruct((m, n), F32), name=name,
        compiler_params=_cp(("parallel", "arbitrary")))(*args)


def _mm_tn(a, b, name, tt=512, tn=1536):
    t, k = a.shape
    _, n = b.shape
    tt, tn = _tile(t, tt), _tile(n, tn)

    def body(a_ref, b_ref, o_ref):
        @pl.when(pl.program_id(1) == 0)
        def _():
            o_ref[...] = jnp.zeros_like(o_ref)

        o_ref[...] += _dot_tn(a_ref[...], b_ref[...])

    return pl.pallas_call(
        body, grid=(n // tn, t // tt),
        in_specs=[pl.BlockSpec((tt, k), lambda j, s: (s, 0)), pl.BlockSpec((tt, tn), lambda j, s: (s, j))],
        out_specs=pl.BlockSpec((k, tn), lambda j, s: (0, j)),
        out_shape=jax.ShapeDtypeStruct((k, n), F32), name=name,
        compiler_params=_cp(("parallel", "arbitrary")))(a, b)


def _ew(body, name, t, ncol, wb, ins, outs, accs=(), tt=256):
    tt = _tile(t, tt)
    nt = t // tt
    in_specs, args = [], []
    for arr, kind, coff in ins:
        if kind == "tile":
            spec = pl.BlockSpec((tt, wb), lambda j, i, c=coff: (i, c + j))
        elif kind == "prev":
            spec = pl.BlockSpec((tt, wb), lambda j, i, c=coff: (jnp.maximum(i - 1, 0), c + j))
        elif kind == "next":
            spec = pl.BlockSpec((tt, wb), lambda j, i, c=coff: (jnp.minimum(i + 1, nt - 1), c + j))
        else:
            spec = pl.BlockSpec((arr.shape[0], wb), lambda j, i, c=coff: (0, c + j))
        in_specs.append(spec)
        args.append(arr)
    out_specs = [pl.BlockSpec((tt, wb), lambda j, i: (i, j)) for _ in outs]
    out_shape = [jax.ShapeDtypeStruct((t, ncol * wb), d) for d in outs]
    for r in accs:
        out_specs.append(pl.BlockSpec((r, wb), lambda j, i: (0, j)))
        out_shape.append(jax.ShapeDtypeStruct((r, ncol * wb), F32))

    def kern(*refs):
        body(pl.program_id(1), nt, *refs)

    res = pl.pallas_call(
        kern, grid=(ncol, nt), in_specs=in_specs, out_specs=out_specs, out_shape=out_shape, name=name,
        compiler_params=_cp(("parallel", "arbitrary")))(*args)
    return res


def _silu_parts(x):
    s = _sigmoid(x)
    return x * s, s + x * s * (1.0 - s)


def _ln_fwd(x, y, g, b, name):
    t, d = x.shape
    tt = _tile(t, 256)

    def body(x_ref, y_ref, g_ref, b_ref, o_ref, ob_ref, xh_ref, r_ref):
        z = ALPHA * x_ref[...] + y_ref[...]
        mu = jnp.mean(z, axis=1, keepdims=True)
        zc = z - mu
        var = jnp.mean(zc * zc, axis=1, keepdims=True)
        rstd = lax.rsqrt(var + LN_EPS)
        xh = zc * rstd
        o = xh * g_ref[...] + b_ref[...]
        o_ref[...] = o
        ob_ref[...] = o.astype(BF16)
        xh_ref[...] = xh
        r_ref[...] = rstd

    row = pl.BlockSpec((tt, d), lambda i: (i, 0))
    vec = pl.BlockSpec((1, d), lambda i: (0, 0))
    return pl.pallas_call(
        body, grid=(t // tt,), in_specs=[row, row, vec, vec],
        out_specs=[row, row, row, pl.BlockSpec((tt, 1), lambda i: (i, 0))],
        out_shape=[jax.ShapeDtypeStruct((t, d), F32), jax.ShapeDtypeStruct((t, d), BF16),
                   jax.ShapeDtypeStruct((t, d), F32), jax.ShapeDtypeStruct((t, 1), F32)],
        name=name, compiler_params=_cp(("parallel",)))(x, y, g, b)


def _ln_bwd(dout, xhat, rstd, g, name):
    t, d = dout.shape
    tt = _tile(t, 256)

    def body(do_ref, xh_ref, r_ref, g_ref, dz_ref, dzb_ref, acc_ref):
        @pl.when(pl.program_id(0) == 0)
        def _():
            acc_ref[...] = jnp.zeros_like(acc_ref)

        do = do_ref[...]
        xh = xh_ref[...]
        dxh = do * g_ref[...]
        m1 = jnp.mean(dxh, axis=1, keepdims=True)
        m2 = jnp.mean(dxh * xh, axis=1, keepdims=True)
        dz = r_ref[...] * (dxh - m1 - xh * m2)
        dz_ref[...] = dz
        dzb_ref[...] = dz.astype(BF16)
        acc_ref[0:1, :] += jnp.sum(do * xh, axis=0, keepdims=True)
        acc_ref[1:2, :] += jnp.sum(do, axis=0, keepdims=True)

    row = pl.BlockSpec((tt, d), lambda i: (i, 0))
    return pl.pallas_call(
        body, grid=(t // tt,),
        in_specs=[row, row, pl.BlockSpec((tt, 1), lambda i: (i, 0)), pl.BlockSpec((1, d), lambda i: (0, 0))],
        out_specs=[row, row, pl.BlockSpec((SUBLANES, d), lambda i: (0, 0))],
        out_shape=[jax.ShapeDtypeStruct((t, d), F32), jax.ShapeDtypeStruct((t, d), BF16),
                   jax.ShapeDtypeStruct((SUBLANES, d), F32)],
        name=name, compiler_params=_cp(("arbitrary",)))(dout, xhat, rstd, g)


def _loss_head(y, target):
    t, d = y.shape
    tt = _tile(t, 256)

    def body(y_ref, t_ref, acc_ref, dy_ref):
        @pl.when(pl.program_id(0) == 0)
        def _():
            acc_ref[...] = jnp.zeros_like(acc_ref)

        err = y_ref[...] - t_ref[...]
        dy_ref[...] = err * (1.0 / d)
        acc_ref[0:1, :] += jnp.sum(err * err, axis=0, keepdims=True)

    row = pl.BlockSpec((tt, d), lambda i: (i, 0))
    acc, dy = pl.pallas_call(
        body, grid=(t // tt,), in_specs=[row, row],
        out_specs=[pl.BlockSpec((SUBLANES, d), lambda i: (0, 0)), row],
        out_shape=[jax.ShapeDtypeStruct((SUBLANES, d), F32), jax.ShapeDtypeStruct((t, d), F32)],
        name="loss_head", compiler_params=_cp(("arbitrary",)))(y, target)
    return acc, dy


def _tri(lower):
    r = lax.broadcasted_iota(jnp.int32, (HGRN_CHUNK, HGRN_CHUNK), 0)
    c = lax.broadcasted_iota(jnp.int32, (HGRN_CHUNK, HGRN_CHUNK), 1)
    return jnp.where((r >= c) if lower else (r <= c), 1.0, 0.0).astype(BF16)


def _exact_tri_matmul(tri, x):
    hi = x.astype(BF16)
    r1 = x - hi.astype(F32)
    mid = r1.astype(BF16)
    lo = (r1 - mid.astype(F32)).astype(BF16)
    return _dot_nn(tri, hi) + _dot_nn(tri, mid) + _dot_nn(tri, lo)


def _hgrn_gates(q_raw, fl, lb):
    sq = _sigmoid(q_raw)
    qf = q_raw * sq * (HGRN_HEAD_DIM ** -0.5)
    sg = _sigmoid(fl)
    f = lb + (1.0 - lb) * sg
    return qf, sq, sg, f


HGRN_SUB = 16
HGRN_NSUB = HGRN_CHUNK // HGRN_SUB


def _block_row(x, r):
    d = x.shape[1]
    x3 = x.reshape(HGRN_NSUB, HGRN_SUB, d)
    return jnp.broadcast_to(x3[:, r:r + 1, :], (HGRN_NSUB, HGRN_SUB, d)).reshape(HGRN_CHUNK, d)


def _block_sum(x):
    return jnp.sum(x.reshape(HGRN_NSUB, HGRN_SUB, x.shape[1]), axis=1, keepdims=True)


def _hgrn_intra_fwd(qf, k, v, b):
    ch, sub, hd = HGRN_CHUNK, HGRN_SUB, qf.shape[1]
    tl = lax.broadcasted_iota(jnp.int32, (ch, hd), 0) & (sub - 1)
    acc = jnp.zeros((ch, hd), F32)
    for r in range(sub):
        e = jnp.exp(jnp.where(tl >= r, b - _block_row(b, r), NEG_INF))
        pcol = jnp.sum(qf * e * _block_row(k, r), axis=1, keepdims=True)
        acc = acc + pcol * _block_row(v, r)
    for j in range(HGRN_NSUB - 1):
        lo = sub * (j + 1)
        c = b[lo - 1:lo, :]
        qj = (qf[lo:] * jnp.exp(b[lo:] - c)).astype(BF16)
        kj = (k[lo - sub:lo] * jnp.exp(c - b[lo - sub:lo])).astype(BF16)
        pj = _dot_nt(qj, kj)
        contrib = _dot_nn(pj.astype(BF16), v[lo - sub:lo].astype(BF16))
        acc = acc + jnp.concatenate([jnp.zeros((lo, hd), F32), contrib], axis=0)
    return acc


def _hgrn_intra_bwd(qf, k, v, b, do_v, dk_scr, dv_scr):
    ch, sub, hd = HGRN_CHUNK, HGRN_SUB, qf.shape[1]
    tl = lax.broadcasted_iota(jnp.int32, (ch, hd), 0) & (sub - 1)
    dq = jnp.zeros((ch, hd), F32)
    for r in range(sub):
        kr = _block_row(k, r)
        e = jnp.exp(jnp.where(tl >= r, b - _block_row(b, r), NEG_INF))
        qe = qf * e
        pcol = jnp.sum(qe * kr, axis=1, keepdims=True)
        dpcol = jnp.sum(do_v * _block_row(v, r), axis=1, keepdims=True)
        dq = dq + dpcol * (kr * e)
        dk_scr[:, r:r + 1, :] = _block_sum(dpcol * qe)
        dv_scr[:, r:r + 1, :] = _block_sum(pcol * do_v)
    dk = dk_scr[...].reshape(ch, hd)
    dv = dv_scr[...].reshape(ch, hd)
    do_b, v_b = do_v.astype(BF16), v.astype(BF16)
    dk_off, dv_off = [], []
    for j in range(HGRN_NSUB - 1):
        lo = sub * (j + 1)
        c = b[lo - 1:lo, :]
        eq = jnp.exp(b[lo:] - c)
        ek = jnp.exp(c - b[lo - sub:lo])
        qj = (qf[lo:] * eq).astype(BF16)
        kj = (k[lo - sub:lo] * ek).astype(BF16)
        p_t = _dot_nt(kj, qj).astype(BF16)
        dp = _dot_nt(do_b[lo:], v_b[lo - sub:lo]).astype(BF16)
        dp_t = _dot_nt(v_b[lo - sub:lo], do_b[lo:]).astype(BF16)
        dq = dq + jnp.concatenate([jnp.zeros((lo, hd), F32), _dot_nn(dp, kj) * eq], axis=0)
        dk_off.append(_dot_nn(dp_t, qj) * ek)
        dv_off.append(_dot_nn(p_t, do_b[lo:]))
    zero = jnp.zeros((sub, hd), F32)
    dk = dk + jnp.concatenate(dk_off + [zero], axis=0)
    dv = dv + jnp.concatenate(dv_off + [zero], axis=0)
    return dq, dk, dv


def _hgrn_fwd(u, lb, off, name):
    t = u.shape[0]
    w = lb.shape[1]
    hd, ch = HGRN_HEAD_DIM, HGRN_CHUNK
    nh, nc = w // hd, t // ch
    cq, cf, cv = off["a_q"] // hd, off["a_f"] // hd, off["a_i"] // hd

    def body(q_ref, f_ref, v_ref, lb_ref, o_ref, st_ref, state):
        @pl.when(pl.program_id(1) == 0)
        def _():
            state[...] = jnp.zeros_like(state)

        st = state[...]
        st_ref[0, 0] = st
        qf, _, _, f = _hgrn_gates(q_ref[...], f_ref[...], lb_ref[...])
        k = 1.0 - f
        v = v_ref[...]
        b = _exact_tri_matmul(_tri(True), jnp.log(f))
        inter = _dot_nt((qf * jnp.exp(b)).astype(BF16), st.astype(BF16))
        o_ref[...] = inter + _hgrn_intra_fwd(qf, k, v, b)
        b_end = b[ch - 1:ch, :]
        kd = k * jnp.exp(b_end - b)
        state[...] = st * jnp.exp(b_end) + _dot_tn(v.astype(BF16), kd.astype(BF16))

    return pl.pallas_call(
        body, grid=(nh, nc),
        in_specs=[pl.BlockSpec((ch, hd), lambda h, n: (n, cq + h)),
                  pl.BlockSpec((ch, hd), lambda h, n: (n, cf + h)),
                  pl.BlockSpec((ch, hd), lambda h, n: (n, cv + h)),
                  pl.BlockSpec((1, hd), lambda h, n: (0, h))],
        out_specs=[pl.BlockSpec((ch, hd), lambda h, n: (n, h)),
                   pl.BlockSpec((1, 1, hd, hd), lambda h, n: (h, n, 0, 0))],
        out_shape=[jax.ShapeDtypeStruct((t, w), F32), jax.ShapeDtypeStruct((nh, nc, hd, hd), F32)],
        scratch_shapes=[pltpu.VMEM((hd, hd), F32)],
        name=name, compiler_params=_cp(("parallel", "arbitrary")))(u, u, u, lb)


def _hgrn_bwd(u, lb, states, do, off, name):
    t = u.shape[0]
    w = lb.shape[1]
    hd, ch = HGRN_HEAD_DIM, HGRN_CHUNK
    nh, nc = w // hd, t // ch
    cq, cf, cv = off["a_q"] // hd, off["a_f"] // hd, off["a_i"] // hd

    def body(q_ref, f_ref, v_ref, do_ref, st_ref, lb_ref, dq_ref, df_ref, dv_ref, dlb_ref,
             dstate, dk_scr, dv_scr):
        @pl.when(pl.program_id(1) == 0)
        def _():
            dstate[...] = jnp.zeros_like(dstate)
            dlb_ref[...] = jnp.zeros_like(dlb_ref)

        lb_row = lb_ref[...]
        q_raw = q_ref[...]
        qf, sq, sg, f = _hgrn_gates(q_raw, f_ref[...], lb_row)
        k = 1.0 - f
        b = _exact_tri_matmul(_tri(True), jnp.log(f))
        a = jnp.exp(b)
        b_end = b[ch - 1:ch, :]
        a_end = jnp.exp(b_end)
        to_end = jnp.exp(b_end - b)
        kd = k * to_end
        st0 = st_ref[0, 0]
        ds = dstate[...]
        do_v = do_ref[...]
        v = v_ref[...]
        do_b, v_b, ds_b, kd_b = do_v.astype(BF16), v.astype(BF16), ds.astype(BF16), kd.astype(BF16)
        qa = qf * a

        dq_inter = a * _dot_nn(do_b, st0.astype(BF16))
        dk_end = to_end * _dot_nn(v_b, ds_b)
        dv_end = _dot_nt(kd_b, ds_b)
        st_end = st0 * a_end + _dot_tn(v_b, kd_b)
        db_end = jnp.sum(ds * st_end, axis=0, keepdims=True)
        dstate[...] = ds * a_end + _dot_tn(do_b, qa.astype(BF16))

        rows = lax.broadcasted_iota(jnp.int32, (ch, hd), 0)
        dq_intra, dk_intra, dv_intra = _hgrn_intra_bwd(qf, k, v, b, do_v, dk_scr, dv_scr)
        dqf = dq_inter + dq_intra
        dk = dk_end + dk_intra
        dv = dv_end + dv_intra
        db = qf * dqf - k * dk
        db = db + jnp.where(rows == ch - 1, db_end, 0.0)
        dg = _exact_tri_matmul(_tri(False), db)
        df = dg / f - dk
        dq_ref[...] = (dqf * (HGRN_HEAD_DIM ** -0.5) * (sq + q_raw * sq * (1.0 - sq))).astype(BF16)
        df_ref[...] = (df * (1.0 - lb_row) * sg * (1.0 - sg)).astype(BF16)
        dv_ref[...] = dv.astype(BF16)
        dlb_ref[0:1, :] += jnp.sum(df * (1.0 - sg), axis=0, keepdims=True)

    rev = lambda n: nc - 1 - n
    tile = lambda c: pl.BlockSpec((ch, hd), lambda h, n, c=c: (rev(n), c + h))
    return pl.pallas_call(
        body, grid=(nh, nc),
        in_specs=[tile(cq), tile(cf), tile(cv), tile(0),
                  pl.BlockSpec((1, 1, hd, hd), lambda h, n: (h, rev(n), 0, 0)),
                  pl.BlockSpec((1, hd), lambda h, n: (0, h))],
        out_specs=[tile(0), tile(0), tile(0), pl.BlockSpec((SUBLANES, hd), lambda h, n: (0, h))],
        out_shape=[jax.ShapeDtypeStruct((t, w), BF16)] * 3 + [jax.ShapeDtypeStruct((SUBLANES, w), F32)],
        scratch_shapes=[pltpu.VMEM((hd, hd), F32)] + [pltpu.VMEM((HGRN_NSUB, HGRN_SUB, hd), F32)] * 2,
        name=name, compiler_params=_cp(("parallel", "arbitrary")))(u, u, u, do, states, lb)


def _gate_a_fwd(o, u, gain, off, name):
    t, w = o.shape
    hd = HGRN_HEAD_DIM

    def body(i, nt, o_ref, g_ref, gain_ref, p_ref):
        silu, _ = _silu_parts(g_ref[...])
        for h in range(w // hd):
            sl = slice(h * hd, (h + 1) * hd)
            oh = o_ref[:, sl]
            r = lax.rsqrt(jnp.mean(oh * oh, axis=1, keepdims=True) + RMS_EPS)
            p_ref[:, sl] = (oh * r * gain_ref[:, sl] * silu[:, sl]).astype(BF16)

    return _ew(body, name, t, 1, w, [(o, "tile", 0), (u, "tile", off["a_g"] // w), (gain, "row", 0)], [BF16])[0]


def _gate_a_bwd(dp, o, u, gain, off, name):
    t, w = o.shape
    hd = HGRN_HEAD_DIM

    def body(i, nt, dp_ref, o_ref, g_ref, gain_ref, do_ref, dg_ref, acc_ref):
        @pl.when(i == 0)
        def _():
            acc_ref[...] = jnp.zeros_like(acc_ref)

        silu, dsilu = _silu_parts(g_ref[...])
        dp_v = dp_ref[...]
        for h in range(w // hd):
            sl = slice(h * hd, (h + 1) * hd)
            oh = o_ref[:, sl]
            r = lax.rsqrt(jnp.mean(oh * oh, axis=1, keepdims=True) + RMS_EPS)
            nrm = oh * r
            gn = gain_ref[:, sl]
            dph = dp_v[:, sl]
            dg_ref[:, sl] = (dph * nrm * gn * dsilu[:, sl]).astype(BF16)
            acc_ref[0:1, sl] += jnp.sum(dph * nrm * silu[:, sl], axis=0, keepdims=True)
            dn = dph * gn * silu[:, sl]
            do_ref[:, sl] = r * (dn - nrm * jnp.mean(dn * nrm, axis=1, keepdims=True))

    return _ew(body, name, t, 1, w,
               [(dp, "tile", 0), (o, "tile", 0), (u, "tile", off["a_g"] // w), (gain, "row", 0)],
               [F32, BF16], accs=[SUBLANES])


def _bucket_map():
    i = np.arange(WINDOW)[:, None]
    j = np.arange(2 * WINDOW)[None, :]
    dist = np.clip(WINDOW + i - j, 0, WINDOW - 1)
    max_exact = N_BUCKETS // 2
    logd = (np.log(np.maximum(dist, 1).astype(np.float32) / max_exact) / math.log(MAX_DISTANCE / max_exact))
    large = np.minimum(max_exact + (logd.astype(np.float32) * (N_BUCKETS - max_exact)).astype(np.int32), N_BUCKETS - 1)
    return np.where(dist < max_exact, dist, large).astype(np.int32)


def _bias_table(rel_bias, n_heads):
    bucket = jnp.asarray(_bucket_map())

    def body(rb_ref, bk_ref, o_ref):
        bk = bk_ref[...]
        for h in range(n_heads):
            def step(bi, acc):
                return jnp.where(bk == bi, rb_ref[bi, h], acc)
            o_ref[h] = lax.fori_loop(0, N_BUCKETS, step, jnp.zeros((WINDOW, 2 * WINDOW), F32))

    return pl.pallas_call(
        body, in_specs=[pl.BlockSpec(memory_space=pltpu.SMEM), pl.BlockSpec(memory_space=pltpu.VMEM)],
        out_specs=pl.BlockSpec(memory_space=pltpu.VMEM),
        out_shape=jax.ShapeDtypeStruct((n_heads, WINDOW, 2 * WINDOW), F32), name="bias_table",
        compiler_params=_cp())(rel_bias, bucket)


def _bias_grad(dbias, n_heads):
    bucket = jnp.asarray(_bucket_map())

    def body(db_ref, bk_ref, o_ref):
        bk = bk_ref[...]
        lane = lax.broadcasted_iota(jnp.int32, (1, LANES), 1)

        def step(bi, carry):
            row = jnp.zeros((1, LANES), F32)
            for h in range(n_heads):
                val = jnp.sum(jnp.where(bk == bi, db_ref[h], 0.0))
                row = jnp.where(lane == h, val, row)
            o_ref[pl.ds(bi, 1), :] = row
            return carry

        lax.fori_loop(0, N_BUCKETS, step, 0)

    return pl.pallas_call(
        body, in_specs=[pl.BlockSpec(memory_space=pltpu.VMEM), pl.BlockSpec(memory_space=pltpu.VMEM)],
        out_specs=pl.BlockSpec(memory_space=pltpu.VMEM),
        out_shape=jax.ShapeDtypeStruct((N_BUCKETS, LANES), F32), name="bias_grad",
        compiler_params=_cp())(dbias, bucket)


def _attn_probs(n, q_ref, kp_ref, kc_ref, bias_ref, sink_ref, hh, grp):
    ad, wn = ATTN_HEAD_DIM, WINDOW
    ksl = slice(hh * ad, (hh + 1) * ad)
    kw = jnp.concatenate([kp_ref[:, ksl], kc_ref[:, ksl]], axis=0).astype(BF16)
    qs = jnp.concatenate([q_ref[:, (hh * grp + g) * ad:(hh * grp + g + 1) * ad] for g in range(grp)], axis=0).astype(BF16)
    s = _dot_nt(qs, kw) * (ad ** -0.5) + bias_ref[hh]
    r = lax.broadcasted_iota(jnp.int32, (grp * wn, 2 * wn), 0)
    j = lax.broadcasted_iota(jnp.int32, (grp * wn, 2 * wn), 1)
    i = r & (wn - 1)
    valid = ((j >= wn) & (j - wn <= i)) | ((j < wn) & (j > i) & (n > 0))
    s = jnp.where(valid, s, NEG_INF)
    rr = lax.broadcasted_iota(jnp.int32, (grp * wn, 1), 0) >> WINDOW_SHIFT
    sink = jnp.zeros((grp * wn, 1), F32)
    for g in range(grp):
        sink = jnp.where(rr == g, sink_ref[hh * grp + g], sink)
    m = jnp.maximum(jnp.max(s, axis=1, keepdims=True), sink)
    p = jnp.exp(s - m)
    es = jnp.exp(sink - m)
    inv = 1.0 / (jnp.sum(p, axis=1, keepdims=True) + es)
    return qs, kw, p * inv, es * inv


def _attn_fwd(u, bias_g, sinks, off, w, name):
    t = u.shape[0]
    wn, ad, kvw = WINDOW, ATTN_HEAD_DIM, KV_WIDTH
    grp = (w // ad) // ATTN_KV_HEADS
    nb = t // wn
    cq, ck, cv = off["b_q"] // w, off["b_k"] // kvw, off["b_v"] // kvw

    def body(q_ref, kp_ref, kc_ref, vp_ref, vc_ref, bias_ref, sink_ref, o_ref):
        n = pl.program_id(0)
        for hh in range(ATTN_KV_HEADS):
            _, _, p, _ = _attn_probs(n, q_ref, kp_ref, kc_ref, bias_ref, sink_ref, hh, grp)
            ksl = slice(hh * ad, (hh + 1) * ad)
            vw = jnp.concatenate([vp_ref[:, ksl], vc_ref[:, ksl]], axis=0).astype(BF16)
            o = _dot_nn(p.astype(BF16), vw)
            for g in range(grp):
                o_ref[:, (hh * grp + g) * ad:(hh * grp + g + 1) * ad] = o[g * wn:(g + 1) * wn]

    prev = lambda n: jnp.maximum(n - 1, 0)
    return pl.pallas_call(
        body, grid=(nb,),
        in_specs=[pl.BlockSpec((wn, w), lambda n: (n, cq)),
                  pl.BlockSpec((wn, kvw), lambda n: (prev(n), ck)), pl.BlockSpec((wn, kvw), lambda n: (n, ck)),
                  pl.BlockSpec((wn, kvw), lambda n: (prev(n), cv)), pl.BlockSpec((wn, kvw), lambda n: (n, cv)),
                  pl.BlockSpec((ATTN_KV_HEADS, grp * wn, 2 * wn), lambda n: (0, 0, 0)),
                  pl.BlockSpec(memory_space=pltpu.SMEM)],
        out_specs=pl.BlockSpec((wn, w), lambda n: (n, 0)),
        out_shape=jax.ShapeDtypeStruct((t, w), F32), name=name,
        compiler_params=_cp(("parallel",)))(u, u, u, u, u, bias_g, sinks)


def _attn_bwd(u, o, do, bias_g, sinks, off, w, name):
    t = u.shape[0]
    wn, ad, kvw = WINDOW, ATTN_HEAD_DIM, KV_WIDTH
    grp = (w // ad) // ATTN_KV_HEADS
    nb = t // wn
    cq, ck, cv = off["b_q"] // w, off["b_k"] // kvw, off["b_v"] // kvw

    def body(q_ref, kp_ref, kc_ref, vp_ref, vc_ref, o_ref, do_ref, bias_ref, sink_ref,
             dq_ref, dkc_ref, dkp_ref, dvc_ref, dvp_ref, dbias_ref, dsink_ref):
        n = pl.program_id(0)

        @pl.when(n == 0)
        def _():
            dbias_ref[...] = jnp.zeros_like(dbias_ref)
            dsink_ref[...] = jnp.zeros_like(dsink_ref)

        lane = lax.broadcasted_iota(jnp.int32, (1, LANES), 1)
        rr = lax.broadcasted_iota(jnp.int32, (grp * wn, 1), 0) >> WINDOW_SHIFT
        dsink_row = jnp.zeros((1, LANES), F32)
        for hh in range(ATTN_KV_HEADS):
            qs, kw, p, psink = _attn_probs(n, q_ref, kp_ref, kc_ref, bias_ref, sink_ref, hh, grp)
            ksl = slice(hh * ad, (hh + 1) * ad)
            vw = jnp.concatenate([vp_ref[:, ksl], vc_ref[:, ksl]], axis=0).astype(BF16)
            hs = [slice((hh * grp + g) * ad, (hh * grp + g + 1) * ad) for g in range(grp)]
            dos = jnp.concatenate([do_ref[:, sl] for sl in hs], axis=0)
            os_ = jnp.concatenate([o_ref[:, sl] for sl in hs], axis=0)
            delta = jnp.sum(dos * os_, axis=1, keepdims=True)
            dos_b = dos.astype(BF16)
            dp = _dot_nt(dos_b, vw)
            ds = p * (dp - delta)
            dbias_ref[hh] += ds
            sd = psink * delta
            for g in range(grp):
                val = -jnp.sum(jnp.where(rr == g, sd, 0.0))
                dsink_row = jnp.where(lane == hh * grp + g, val, dsink_row)
            ds_b = (ds * (ad ** -0.5)).astype(BF16)
            dq = _dot_nn(ds_b, kw)
            for g in range(grp):
                dq_ref[:, hs[g]] = dq[g * wn:(g + 1) * wn].astype(BF16)
            dkw = _dot_tn(ds_b, qs)
            dvw = _dot_tn(p.astype(BF16), dos_b)
            dkp_ref[:, ksl] = dkw[:wn]
            dkc_ref[:, ksl] = dkw[wn:]
            dvp_ref[:, ksl] = dvw[:wn]
            dvc_ref[:, ksl] = dvw[wn:]
        dsink_ref[0:1, :] += dsink_row

    prev = lambda n: jnp.maximum(n - 1, 0)
    kv_out = pl.BlockSpec((wn, kvw), lambda n: (n, 0))
    return pl.pallas_call(
        body, grid=(nb,),
        in_specs=[pl.BlockSpec((wn, w), lambda n: (n, cq)),
                  pl.BlockSpec((wn, kvw), lambda n: (prev(n), ck)), pl.BlockSpec((wn, kvw), lambda n: (n, ck)),
                  pl.BlockSpec((wn, kvw), lambda n: (prev(n), cv)), pl.BlockSpec((wn, kvw), lambda n: (n, cv)),
                  pl.BlockSpec((wn, w), lambda n: (n, 0)), pl.BlockSpec((wn, w), lambda n: (n, 0)),
                  pl.BlockSpec((ATTN_KV_HEADS, grp * wn, 2 * wn), lambda n: (0, 0, 0)),
                  pl.BlockSpec(memory_space=pltpu.SMEM)],
        out_specs=[pl.BlockSpec((wn, w), lambda n: (n, 0)), kv_out, kv_out, kv_out, kv_out,
                   pl.BlockSpec((ATTN_KV_HEADS, grp * wn, 2 * wn), lambda n: (0, 0, 0)),
                   pl.BlockSpec((SUBLANES, LANES), lambda n: (0, 0))],
        out_shape=[jax.ShapeDtypeStruct((t, w), BF16)] + [jax.ShapeDtypeStruct((t, kvw), F32)] * 4
        + [jax.ShapeDtypeStruct((ATTN_KV_HEADS, grp * wn, 2 * wn), F32), jax.ShapeDtypeStruct((SUBLANES, LANES), F32)],
        name=name, compiler_params=_cp(("arbitrary",)))(u, u, u, u, u, o, do, bias_g, sinks)


def _kv_combine(cur, prv, name):
    t, kvw = cur.shape

    def body(i, nt, c_ref, p_ref, o_ref):
        nxt = jnp.where(i < nt - 1, p_ref[...], 0.0)
        o_ref[...] = (c_ref[...] + nxt).astype(BF16)

    return _ew(body, name, t, 1, kvw, [(cur, "tile", 0), (prv, "next", 0)], [BF16], tt=WINDOW)[0]


def _gate_b_fwd(o, u, off, name):
    t, w = o.shape
    wb = 512

    def body(i, nt, o_ref, g_ref, p_ref):
        silu, _ = _silu_parts(g_ref[...])
        p_ref[...] = (o_ref[...] * silu).astype(BF16)

    return _ew(body, name, t, w // wb, wb, [(o, "tile", 0), (u, "tile", off["b_g"] // wb)], [BF16])[0]


def _gate_b_bwd(dp, o, u, off, name):
    t, w = o.shape
    wb = 512

    def body(i, nt, dp_ref, o_ref, g_ref, do_ref, dg_ref):
        silu, dsilu = _silu_parts(g_ref[...])
        dp_v = dp_ref[...]
        do_ref[...] = dp_v * silu
        dg_ref[...] = (dp_v * o_ref[...] * dsilu).astype(BF16)

    return _ew(body, name, t, w // wb, wb,
               [(dp, "tile", 0), (o, "tile", 0), (u, "tile", off["b_g"] // wb)], [F32, BF16])


def _shift_down(h, tail, k, rows):
    tt = h.shape[0]
    out = pltpu.roll(h, k, 0)
    for r in range(k):
        out = jnp.where(rows == r, tail[tt - k + r:tt - k + r + 1, :], out)
    return out


def _shift_up(h, head, k, rows):
    tt = h.shape[0]
    out = pltpu.roll(h, tt - k, 0)
    for r in range(k):
        out = jnp.where(rows == tt - k + r, head[r:r + 1, :], out)
    return out


def _conv_fwd(u, conv_w, off, w, name):
    t = u.shape[0]
    wb = 512
    c = lambda nme: off[nme] // wb

    def body(i, nt, cb_ref, cc_ref, ccp_ref, cx_ref, cxp_ref, cg_ref, w_ref, p_ref):
        h = cc_ref[...] * cx_ref[...]
        hp = jnp.where(i > 0, ccp_ref[...] * cxp_ref[...], 0.0)
        rows = lax.broadcasted_iota(jnp.int32, h.shape, 0)
        y = w_ref[0:1, :] * _shift_down(h, hp, 2, rows) + w_ref[1:2, :] * _shift_down(h, hp, 1, rows) + w_ref[2:3, :] * h
        silu, _ = _silu_parts(cg_ref[...])
        p_ref[...] = (cb_ref[...] * y * silu).astype(BF16)

    return _ew(body, name, t, w // wb, wb,
               [(u, "tile", c("c_b")), (u, "tile", c("c_c")), (u, "prev", c("c_c")), (u, "tile", c("c_x")),
                (u, "prev", c("c_x")), (u, "tile", c("c_g")), (conv_w, "row", 0)], [BF16])[0]


def _conv_bwd(dp, u, conv_w, off, w, name):
    t = u.shape[0]
    wb = 512
    c = lambda nme: off[nme] // wb

    def body(i, nt, dp_ref, dpn_ref, cb_ref, cbn_ref, cg_ref, cgn_ref, cc_ref, ccp_ref, cx_ref, cxp_ref, w_ref,
             dcb_ref, dcc_ref, dcx_ref, dcg_ref, acc_ref):
        @pl.when(i == 0)
        def _():
            acc_ref[...] = jnp.zeros_like(acc_ref)

        cc, cx, cb = cc_ref[...], cx_ref[...], cb_ref[...]
        h = cc * cx
        hp = jnp.where(i > 0, ccp_ref[...] * cxp_ref[...], 0.0)
        rows = lax.broadcasted_iota(jnp.int32, h.shape, 0)
        h1 = _shift_down(h, hp, 1, rows)
        h2 = _shift_down(h, hp, 2, rows)
        w0, w1, w2 = w_ref[0:1, :], w_ref[1:2, :], w_ref[2:3, :]
        y = w0 * h2 + w1 * h1 + w2 * h
        silu, dsilu = _silu_parts(cg_ref[...])
        dp_v = dp_ref[...]
        dcg_ref[...] = (dp_v * cb * y * dsilu).astype(BF16)
        dcb_ref[...] = (dp_v * y * silu).astype(BF16)
        dy = dp_v * cb * silu
        silu_n, _ = _silu_parts(cgn_ref[...])
        dyn = jnp.where(i < nt - 1, dpn_ref[...] * cbn_ref[...] * silu_n, 0.0)
        dh = w2 * dy + w1 * _shift_up(dy, dyn, 1, rows) + w0 * _shift_up(dy, dyn, 2, rows)
        dcc_ref[...] = (dh * cx).astype(BF16)
        dcx_ref[...] = (dh * cc).astype(BF16)
        acc_ref[0:1, :] += jnp.sum(dy * h2, axis=0, keepdims=True)
        acc_ref[1:2, :] += jnp.sum(dy * h1, axis=0, keepdims=True)
        acc_ref[2:3, :] += jnp.sum(dy * h, axis=0, keepdims=True)

    return _ew(body, name, t, w // wb, wb,
               [(dp, "tile", 0), (dp, "next", 0), (u, "tile", c("c_b")), (u, "next", c("c_b")),
                (u, "tile", c("c_g")), (u, "next", c("c_g")), (u, "tile", c("c_c")), (u, "prev", c("c_c")),
                (u, "tile", c("c_x")), (u, "prev", c("c_x")), (conv_w, "row", 0)],
               [BF16] * 4, accs=[SUBLANES])


def _merge_fwd(u, ya, yb, yc, off, d, name):
    t = u.shape[0]
    wb = 512
    c = lambda nme: off[nme] // wb

    def body(i, nt, ma_ref, mb_ref, mc_ref, ya_ref, yb_ref, yc_ref, o_ref):
        o_ref[...] = (_sigmoid(ma_ref[...]) * ya_ref[...] + _sigmoid(mb_ref[...]) * yb_ref[...]
                      + _sigmoid(mc_ref[...]) * yc_ref[...]).astype(BF16)

    return _ew(body, name, t, d // wb, wb,
               [(u, "tile", c("m_a")), (u, "tile", c("m_b")), (u, "tile", c("m_c")),
                (ya, "tile", 0), (yb, "tile", 0), (yc, "tile", 0)], [BF16])[0]


def _merge_bwd(dm, u, ya, yb, yc, off, d, name):
    t = u.shape[0]
    wb = 512
    c = lambda nme: off[nme] // wb

    def body(i, nt, dm_ref, ma_ref, mb_ref, mc_ref, ya_ref, yb_ref, yc_ref, da_ref, db_ref, dc_ref, ga_ref, gb_ref, gc_ref):
        dm_v = dm_ref[...]
        for m_ref, y_ref, dy_ref, dg_ref in ((ma_ref, ya_ref, da_ref, ga_ref), (mb_ref, yb_ref, db_ref, gb_ref),
                                             (mc_ref, yc_ref, dc_ref, gc_ref)):
            s = _sigmoid(m_ref[...])
            dy_ref[...] = (dm_v * s).astype(BF16)
            dg_ref[...] = (dm_v * y_ref[...] * s * (1.0 - s)).astype(BF16)

    return _ew(body, name, t, d // wb, wb,
               [(dm, "tile", 0), (u, "tile", c("m_a")), (u, "tile", c("m_b")), (u, "tile", c("m_c")),
                (ya, "tile", 0), (yb, "tile", 0), (yc, "tile", 0)], [BF16] * 6)


def _lower_bounds(lb_param):
    def body(p_ref, o_ref):
        p = p_ref[...]
        e = jnp.exp(p - jnp.max(p, axis=0, keepdims=True))
        soft = e / jnp.sum(e, axis=0, keepdims=True)
        acc = jnp.zeros_like(soft[0:1])
        o_ref[0:1, :] = acc
        for l in range(1, DEPTH):
            acc = acc + soft[l:l + 1]
            o_ref[l:l + 1, :] = acc

    return pl.pallas_call(body, out_shape=jax.ShapeDtypeStruct(lb_param.shape, F32), name="lower_bounds",
                          compiler_params=_cp())(lb_param)


def _lower_bounds_bwd(lb_param, dlower):
    def body(p_ref, d_ref, o_ref):
        p = p_ref[...]
        e = jnp.exp(p - jnp.max(p, axis=0, keepdims=True))
        soft = e / jnp.sum(e, axis=0, keepdims=True)
        dl = d_ref[...]
        ds = [jnp.zeros_like(dl[0:1])]
        for j in range(1, DEPTH):
            acc = dl[j:j + 1]
            for l in range(j + 1, DEPTH):
                acc = acc + dl[l:l + 1]
            ds.append(acc)
        inner = ds[0] * soft[0:1]
        for j in range(1, DEPTH):
            inner = inner + ds[j] * soft[j:j + 1]
        for j in range(DEPTH):
            o_ref[j:j + 1, :] = soft[j:j + 1] * (ds[j] - inner)

    return pl.pallas_call(body, out_shape=jax.ShapeDtypeStruct(lb_param.shape, F32), name="lower_bounds_bwd",
                          compiler_params=_cp())(lb_param, dlower)


def _exchange(arrays, scatter, name):
    n_arr = len(arrays)

    def body(*refs):
        srcs, dsts = refs[:n_arr], refs[n_arr:2 * n_arr]
        send_sems, recv_sems, local_sems = refs[2 * n_arr:]
        x, y, c = lax.axis_index("x"), lax.axis_index("y"), lax.axis_index("c")
        me = 4 * x + 2 * y + c
        copies = []
        for a in range(n_arr):
            for k in range(1, N_DEV):
                px = 1 - x if k & 4 else x
                py = 1 - y if k & 2 else y
                pc = 1 - c if k & 1 else c
                src = srcs[a].at[4 * px + 2 * py + pc] if scatter else srcs[a]
                cp = pltpu.make_async_remote_copy(
                    src_ref=src, dst_ref=dsts[a].at[me],
                    send_sem=send_sems.at[a * (N_DEV - 1) + k - 1], recv_sem=recv_sems.at[a * (N_DEV - 1) + k - 1],
                    device_id=(px, py, pc), device_id_type=pl.DeviceIdType.MESH)
                cp.start()
                copies.append(cp)
            own = pltpu.make_async_copy(srcs[a].at[me] if scatter else srcs[a], dsts[a].at[me], local_sems.at[a])
            own.start()
            copies.append(own)
        for cp in copies:
            cp.wait()

    out_shape = [jax.ShapeDtypeStruct(a.shape if scatter else (N_DEV,) + a.shape, a.dtype) for a in arrays]
    anyspec = pl.BlockSpec(memory_space=pl.ANY)
    res = pl.pallas_call(
        body, in_specs=[anyspec] * n_arr, out_specs=[anyspec] * n_arr, out_shape=out_shape,
        scratch_shapes=[pltpu.SemaphoreType.DMA((n_arr * (N_DEV - 1),)), pltpu.SemaphoreType.DMA((n_arr * (N_DEV - 1),)),
                        pltpu.SemaphoreType.DMA((n_arr,))],
        name=name)(*arrays)
    return list(res)


def _unshard_cols(g, name):
    nd, r, s = g.shape
    tr = _tile(r, 64)

    def body(i_ref, o_ref):
        for p in range(nd):
            o_ref[:, p * s:(p + 1) * s] = i_ref[p]

    return pl.pallas_call(
        body, grid=(r // tr,), in_specs=[pl.BlockSpec((nd, tr, s), lambda i: (0, i, 0))],
        out_specs=pl.BlockSpec((tr, nd * s), lambda i: (i, 0)),
        out_shape=jax.ShapeDtypeStruct((r, nd * s), g.dtype), name=name, compiler_params=_cp(("parallel",)))(g)


def _shard_cols(g, name):
    r, n = g.shape
    s = n // N_DEV
    tr = _tile(r, 64)

    def body(i_ref, o_ref):
        for p in range(N_DEV):
            o_ref[p] = i_ref[:, p * s:(p + 1) * s]

    return pl.pallas_call(
        body, grid=(r // tr,), in_specs=[pl.BlockSpec((tr, n), lambda i: (i, 0))],
        out_specs=pl.BlockSpec((N_DEV, tr, s), lambda i: (0, i, 0)),
        out_shape=jax.ShapeDtypeStruct((N_DEV, r, s), g.dtype), name=name, compiler_params=_cp(("parallel",)))(g)


def _slot_sum(slots, name):
    nd, r, c = slots.shape
    tr = _tile(r, 64)

    def body(s_ref, o_ref):
        acc = s_ref[0]
        for p in range(1, nd):
            acc = acc + s_ref[p]
        o_ref[...] = acc

    return pl.pallas_call(
        body, grid=(r // tr,), in_specs=[pl.BlockSpec((nd, tr, c), lambda i: (0, i, 0))],
        out_specs=pl.BlockSpec((tr, c), lambda i: (i, 0)),
        out_shape=jax.ShapeDtypeStruct((r, c), F32), name=name, compiler_params=_cp(("parallel",)))(slots)


def _adamw(w, g, m, v, name):
    r, c = w.shape
    tr = _tile(r, 256)
    c1 = 1.0 - ADAM_B1 ** ADAM_STEP
    c2 = 1.0 - ADAM_B2 ** ADAM_STEP

    def body(w_ref, g_ref, m_ref, v_ref, d_ref, nm_ref, nv_ref):
        gv = g_ref[...]
        nm = ADAM_B1 * m_ref[...] + (1.0 - ADAM_B1) * gv
        nv = ADAM_B2 * v_ref[...] + (1.0 - ADAM_B2) * (gv * gv)
        nm_ref[...] = nm
        nv_ref[...] = nv
        d_ref[...] = -ADAM_LR * ((nm / c1) / (jnp.sqrt(nv / c2) + ADAM_EPS) + ADAM_WD * w_ref[...])

    spec = pl.BlockSpec((tr, c), lambda i: (i, 0))
    return pl.pallas_call(
        body, grid=(r // tr,), in_specs=[spec] * 4, out_specs=[spec] * 3,
        out_shape=[jax.ShapeDtypeStruct((r, c), F32)] * 3, name=name, compiler_params=_cp(("parallel",)))(w, g, m, v)


def _forward_backward(x, target, weights, lb_param, hgrn_norm_g, attn_sinks, conv_w, rel_bias, ln_g, ln_b):
    t, d = x.shape
    w = d // 2
    off, n_in = _offsets(d)
    n_heads = w // ATTN_HEAD_DIM
    grp = n_heads // ATTN_KV_HEADS

    lower = _lower_bounds(lb_param)
    bias = _bias_table(rel_bias, n_heads)
    bias_g = bias.reshape(ATTN_KV_HEADS, grp * WINDOW, 2 * WINDOW)

    saved = []
    xb = x.astype(BF16)
    for l in range(DEPTH):
        wl = weights[l]
        s = {"x": x, "xb": xb}
        u = _mm_nn(xb, wl["w_in"], f"in_proj")
        s["u"] = u
        lb_l, gain_l, cw_l = lower[l:l + 1], hgrn_norm_g[l:l + 1], conv_w[l]
        o_a, states = _hgrn_fwd(u, lb_l, off, f"hgrn_fwd")
        p_a = _gate_a_fwd(o_a, u, gain_l, off, f"gate_a_fwd")
        o_b = _attn_fwd(u, bias_g, attn_sinks[l], off, w, f"attn_fwd")
        p_b = _gate_b_fwd(o_b, u, off, f"gate_b_fwd")
        p_c = _conv_fwd(u, cw_l, off, w, f"conv_fwd")
        y_a = _mm_nn(p_a, wl["w_proj_hgrn"], f"proj_a", tn=2048)
        y_b = _mm_nn(p_b, wl["w_proj_attn"], f"proj_b", tn=2048)
        y_c = _mm_nn(p_c, wl["w_proj_conv"], f"proj_c", tn=2048)
        merged = _merge_fwd(u, y_a, y_b, y_c, off, d, f"merge_fwd")
        y = _mm_nn(merged, wl["w_out"], f"out_proj", tm=512, tn=2048)
        x, xb, xhat, rstd = _ln_fwd(x, y, ln_g[l:l + 1], ln_b[l:l + 1], f"ln_fwd")
        s.update(o_a=o_a, states=states, p_a=p_a, o_b=o_b, p_b=p_b, p_c=p_c, y_a=y_a, y_b=y_b, y_c=y_c,
                 merged=merged, xhat=xhat, rstd=rstd)
        saved.append(s)

    loss_acc, dx = _loss_head(x, target)

    big = [None] * DEPTH
    d_ln, d_lower, d_gain, d_sink, d_conv = [None] * DEPTH, [None] * DEPTH, [None] * DEPTH, [None] * DEPTH, [None] * DEPTH
    dbias_total = None
    for l in reversed(range(DEPTH)):
        wl, s = weights[l], saved[l]
        u = s["u"]
        lb_l, gain_l, cw_l = lower[l:l + 1], hgrn_norm_g[l:l + 1], conv_w[l]
        dz, dzb, d_ln[l] = _ln_bwd(dx, s["xhat"], s["rstd"], ln_g[l:l + 1], f"ln_bwd")
        g_out = _mm_tn(s["merged"], dzb, f"g_out", tn=2048)
        dmerged = _mm_nt(dzb, wl["w_out"], f"d_merged", tk=2048)
        dya, dyb, dyc, dma, dmb, dmc = _merge_bwd(dmerged, u, s["y_a"], s["y_b"], s["y_c"], off, d, f"merge_bwd")
        g_pa = _mm_tn(s["p_a"], dya, f"g_proj_a", tn=2048)
        g_pb = _mm_tn(s["p_b"], dyb, f"g_proj_b", tn=2048)
        g_pc = _mm_tn(s["p_c"], dyc, f"g_proj_c", tn=2048)
        dpa = _mm_nt(dya, wl["w_proj_hgrn"], f"d_p_a", tk=2048)
        dpb = _mm_nt(dyb, wl["w_proj_attn"], f"d_p_b", tk=2048)
        dpc = _mm_nt(dyc, wl["w_proj_conv"], f"d_p_c", tk=2048)
        do_a, d_ag, d_gain[l] = _gate_a_bwd(dpa, s["o_a"], u, gain_l, off, f"gate_a_bwd")
        d_aq, d_af, d_ai, d_lower[l] = _hgrn_bwd(u, lb_l, s["states"], do_a, off, f"hgrn_bwd")
        do_b, d_bg = _gate_b_bwd(dpb, s["o_b"], u, off, f"gate_b_bwd")
        d_bq, dkc, dkp, dvc, dvp, dbias_l, d_sink[l] = _attn_bwd(u, s["o_b"], do_b, bias_g, attn_sinks[l], off, w, f"attn_bwd")
        d_bk = _kv_combine(dkc, dkp, f"k_combine")
        d_bv = _kv_combine(dvc, dvp, f"v_combine")
        dbias_total = dbias_l if dbias_total is None else dbias_total + dbias_l
        d_cb, d_cc, d_cx, d_cg, d_conv[l] = _conv_bwd(dpc, u, cw_l, off, w, f"conv_bwd")
        du = jnp.concatenate([d_aq, d_af, d_ai, d_ag, d_bq, d_bk, d_bv, d_bg, d_cb, d_cc, d_cx, d_cg, dma, dmb, dmc], axis=1)
        g_in = _mm_tn(s["xb"], du, f"g_in")
        dx = _mm_nt(du, wl["w_in"], f"d_x", add=dz, add_scale=ALPHA)
        big[l] = {"w_in": g_in, "w_proj_hgrn": g_pa, "w_proj_attn": g_pb, "w_proj_conv": g_pc, "w_out": g_out}

    d_lower_all = jnp.concatenate([a[0:1] for a in d_lower], axis=0)
    small = {
        "lb_param": _lower_bounds_bwd(lb_param, d_lower_all),
        "hgrn_norm_g": jnp.concatenate([a[0:1] for a in d_gain], axis=0),
        "attn_sinks": jnp.concatenate([a[0:1, :n_heads] for a in d_sink], axis=0),
        "conv_w": jnp.stack([a[0:3] for a in d_conv], axis=0),
        "rel_bias": _bias_grad(dbias_total.reshape(n_heads, WINDOW, 2 * WINDOW), n_heads)[:, :n_heads],
        "ln_g": jnp.concatenate([a[0:1] for a in d_ln], axis=0),
        "ln_b": jnp.concatenate([a[1:2] for a in d_ln], axis=0),
    }
    return loss_acc, dx, big, small


BIG = ("w_in", "w_proj_hgrn", "w_proj_attn", "w_proj_conv", "w_out")
SMALL = ("lb_param", "hgrn_norm_g", "attn_sinks", "conv_w", "rel_bias", "ln_g", "ln_b")
ORDER = ("w_in", "w_proj_hgrn", "w_proj_attn", "w_proj_conv", "w_out", "lb_param", "hgrn_norm_g", "attn_sinks",
         "conv_w", "rel_bias", "ln_g", "ln_b")


def _pack(parts):
    flat = jnp.concatenate([p.reshape(-1) for p in parts])
    n = flat.shape[0]
    unit = SUBLANES * LANES
    total = -(-n // unit) * unit
    return jnp.pad(flat, (0, total - n)).reshape(total // LANES, LANES)


def _unpack(packed, shapes):
    flat = packed.reshape(-1)
    out, o = [], 0
    for shp in shapes:
        n = int(np.prod(shp))
        out.append(flat[o:o + n].reshape(shp))
        o += n
    return out


def kernel(x, w_in, w_proj_hgrn, w_proj_attn, w_proj_conv, w_out, lb_param, hgrn_norm_g, attn_sinks, conv_w, rel_bias, ln_g, ln_b, loss_target, m_w_in, m_w_proj_hgrn, m_w_proj_attn, m_w_proj_conv, m_w_out, m_lb_param, m_hgrn_norm_g, m_attn_sinks, m_conv_w, m_rel_bias, m_ln_g, m_ln_b, v_w_in, v_w_proj_hgrn, v_w_proj_attn, v_w_proj_conv, v_w_out, v_lb_param, v_hgrn_norm_g, v_attn_sinks, v_conv_w, v_rel_bias, v_ln_g, v_ln_b):
    params = dict(w_in=w_in, w_proj_hgrn=w_proj_hgrn, w_proj_attn=w_proj_attn, w_proj_conv=w_proj_conv, w_out=w_out,
                  lb_param=lb_param, hgrn_norm_g=hgrn_norm_g, attn_sinks=attn_sinks, conv_w=conv_w, rel_bias=rel_bias,
                  ln_g=ln_g, ln_b=ln_b)
    mom_m = dict(w_in=m_w_in, w_proj_hgrn=m_w_proj_hgrn, w_proj_attn=m_w_proj_attn, w_proj_conv=m_w_proj_conv,
                 w_out=m_w_out, lb_param=m_lb_param, hgrn_norm_g=m_hgrn_norm_g, attn_sinks=m_attn_sinks,
                 conv_w=m_conv_w, rel_bias=m_rel_bias, ln_g=m_ln_g, ln_b=m_ln_b)
    mom_v = dict(w_in=v_w_in, w_proj_hgrn=v_w_proj_hgrn, w_proj_attn=v_w_proj_attn, w_proj_conv=v_w_proj_conv,
                 w_out=v_w_out, lb_param=v_lb_param, hgrn_norm_g=v_hgrn_norm_g, attn_sinks=v_attn_sinks,
                 conv_w=v_conv_w, rel_bias=v_rel_bias, ln_g=v_ln_g, ln_b=v_ln_b)
    d = x.shape[-1]
    me = 4 * lax.axis_index("x") + 2 * lax.axis_index("y") + lax.axis_index("c")

    weights = []
    for l in range(DEPTH):
        shards = [params[n][l].astype(BF16) for n in BIG] + [conv_w[l]]
        got = _exchange(shards, False, f"gather_weights")
        wl = {
            "w_in": _unshard_cols(got[0], f"unshard_w_in"),
            "w_proj_hgrn": _unshard_cols(got[1], f"unshard_w_proj_hgrn"),
            "w_proj_attn": _unshard_cols(got[2], f"unshard_w_proj_attn"),
            "w_proj_conv": _unshard_cols(got[3], f"unshard_w_proj_conv"),
            "w_out": got[4].reshape(d, d),
            "conv_w": _unshard_cols(got[5], f"unshard_conv_w"),
        }
        weights.append(wl)
    conv_full = jnp.stack([wl["conv_w"] for wl in weights], axis=0)

    loss_acc, dx, big, small = _forward_backward(
        x[0], loss_target[0], weights, lb_param, hgrn_norm_g, attn_sinks, conv_full, rel_bias, ln_g, ln_b)
    loss = lax.psum(0.5 * jnp.sum(loss_acc[0]) / d, ("x", "y", "c"))

    grads = {n: [] for n in ORDER}
    for l in range(DEPTH):
        g = big[l]
        send = [_shard_cols(g["w_in"], f"shard_g_in"), _shard_cols(g["w_proj_hgrn"], f"shard_g_proj_a"),
                _shard_cols(g["w_proj_attn"], f"shard_g_proj_b"), _shard_cols(g["w_proj_conv"], f"shard_g_proj_c"),
                g["w_out"].reshape(N_DEV, d // N_DEV, d)]
        got = _exchange(send, True, f"scatter_grads")
        for n, slots in zip(BIG, got):
            grads[n].append(_slot_sum(slots, f"sum_{n}"))
    for n in BIG:
        grads[n] = jnp.stack(grads[n], axis=0)

    small_shapes = [small[n].shape for n in SMALL]
    packed = _pack([small[n] for n in SMALL])
    got = _exchange([packed], False, "gather_small_grads")[0]
    summed = _unpack(_slot_sum(got, "sum_small_grads"), small_shapes)
    for n, g in zip(SMALL, summed):
        grads[n] = g
    cs = conv_w.shape[-1]
    grads["conv_w"] = lax.dynamic_slice_in_dim(grads["conv_w"], me * cs, cs, axis=2)

    delta, new_m, new_v = {}, {}, {}
    for n in BIG:
        shp = params[n].shape
        flat = lambda a: a.reshape(-1, shp[-1])
        dl, nm, nv = _adamw(flat(params[n]), flat(grads[n]), flat(mom_m[n]), flat(mom_v[n]), f"adamw_{n}")
        delta[n], new_m[n], new_v[n] = dl.reshape(shp), nm.reshape(shp), nv.reshape(shp)
    shapes = [params[n].shape for n in SMALL]
    res = _adamw(_pack([params[n] for n in SMALL]), _pack([grads[n] for n in SMALL]),
                 _pack([mom_m[n] for n in SMALL]), _pack([mom_v[n] for n in SMALL]), "adamw_small")
    for dst, packed_res in zip((delta, new_m, new_v), res):
        for n, a in zip(SMALL, _unpack(packed_res, shapes)):
            dst[n] = a

    return (loss, dx[None], *[grads[n] for n in ORDER], *[delta[n] for n in ORDER],
            *[new_m[n] for n in ORDER], *[new_v[n] for n in ORDER])
```

```python
import functools
import math

import numpy as np
import jax
import jax.numpy as jnp
from jax import lax
from jax.experimental import pallas as pl
from jax.experimental.pallas import tpu as pltpu

F32 = jnp.float32
BF16 = jnp.bfloat16

N_DEV = 8
DEPTH = 4
HGRN_HEAD_DIM = 128
HGRN_CHUNK = 64
ATTN_HEAD_DIM = 64
ATTN_KV_HEADS = 4
KV_WIDTH = ATTN_KV_HEADS * ATTN_HEAD_DIM
WINDOW = 128
WINDOW_SHIFT = 7
N_BUCKETS = 32
MAX_DISTANCE = 128
ALPHA = (2.0 * DEPTH) ** 0.25
LN_EPS = 1e-5
RMS_EPS = 1e-6
ADAM_LR = 0.001
ADAM_B1 = 0.9
ADAM_B2 = 0.999
ADAM_EPS = 1e-08
ADAM_WD = 0.01
ADAM_STEP = 10

LANES = 128
SUBLANES = 8
VMEM_LIMIT = 56 << 20
NEG_INF = float("-inf")


def _offsets(d_model):
    w = d_model // 2
    sizes = (w, w, w, w, w, KV_WIDTH, KV_WIDTH, w, w, w, w, w, d_model, d_model, d_model)
    names = ("a_q", "a_f", "a_i", "a_g", "b_q", "b_k", "b_v", "b_g", "c_b", "c_c", "c_x", "c_g", "m_a", "m_b", "m_c")
    off, o = {}, 0
    for n, s in zip(names, sizes):
        off[n] = o
        o += s
    return off, o


def _tile(n, pref):
    t = min(pref, n)
    while n % t:
        t //= 2
    return t


def _cp(sem=None, vmem=VMEM_LIMIT):
    return pltpu.CompilerParams(dimension_semantics=sem, vmem_limit_bytes=vmem)


def _sigmoid(x):
    return 1.0 / (1.0 + jnp.exp(-x))


def _dot_nn(a, b):
    return jnp.dot(a, b, preferred_element_type=F32)


def _dot_nt(a, b):
    return lax.dot_general(a, b, (((1,), (1,)), ((), ())), preferred_element_type=F32)


def _dot_tn(a, b):
    return lax.dot_general(a, b, (((0,), (0,)), ((), ())), preferred_element_type=F32)


def _dep_specs(dep):
    return ([], []) if dep is None else ([pl.BlockSpec(memory_space=pl.ANY)], [dep])


def _mm_nn(a, b, name, out_dtype=F32, tm=1024, tn=1536, dep=None):
    m, k = a.shape
    _, n = b.shape
    tm, tn = _tile(m, tm), _tile(n, tn)
    dep_specs, dep_args = _dep_specs(dep)

    def body(a_ref, b_ref, *rest):
        o_ref = rest[-1]
        o_ref[...] = _dot_nn(a_ref[...], b_ref[...]).astype(o_ref.dtype)

    return pl.pallas_call(
        body, grid=(n // tn, m // tm),
        in_specs=[pl.BlockSpec((tm, k), lambda j, i: (i, 0)), pl.BlockSpec((k, tn), lambda j, i: (0, j))] + dep_specs,
        out_specs=pl.BlockSpec((tm, tn), lambda j, i: (i, j)),
        out_shape=jax.ShapeDtypeStruct((m, n), out_dtype), name=name,
        compiler_params=_cp(("parallel", "parallel")))(a, b, *dep_args)


def _mm_nt(a, b, name, tm=1024, tk=1536, add=None, add_scale=1.0, dep=None):
    m, k = a.shape
    n, _ = b.shape
    tm, tk = _tile(m, tm), _tile(k, tk)
    has_add = add is not None
    dep_specs, dep_args = _dep_specs(dep)

    def body(*refs):
        if has_add:
            a_ref, b_ref, add_ref = refs[:3]
        else:
            a_ref, b_ref = refs[:2]
        o_ref = refs[-1]

        @pl.when(pl.program_id(1) == 0)
        def _():
            if has_add:
                o_ref[...] = add_ref[...] * add_scale
            else:
                o_ref[...] = jnp.zeros_like(o_ref)

        o_ref[...] += _dot_nt(a_ref[...], b_ref[...])

    in_specs = [pl.BlockSpec((tm, tk), lambda i, kk: (i, kk)), pl.BlockSpec((n, tk), lambda i, kk: (0, kk))]
    args = [a, b]
    if has_add:
        in_specs.append(pl.BlockSpec((tm, n), lambda i, kk: (i, 0)))
        args.append(add)
    in_specs += dep_specs
    args += dep_args
    return pl.pallas_call(
        body, grid=(m // tm, k // tk), in_specs=in_specs,
        out_specs=pl.BlockSpec((tm, n), lambda i, kk: (i, 0)),
        out_shape=jax.ShapeDtypeStruct((m, n), F32), name=name,
        compiler_params=_cp(("parallel", "arbitrary")))(*args)


def _mm_tn(a, b, name, tt=512, tn=1536):
    t, k = a.shape
    _, n = b.shape
    tt, tn = _tile(t, tt), _tile(n, tn)

    def body(a_ref, b_ref, o_ref):
        @pl.when(pl.program_id(1) == 0)
        def _():
            o_ref[...] = jnp.zeros_like(o_ref)

        o_ref[...] += _dot_tn(a_ref[...], b_ref[...])

    return pl.pallas_call(
        body, grid=(n // tn, t // tt),
        in_specs=[pl.BlockSpec((tt, k), lambda j, s: (s, 0)), pl.BlockSpec((tt, tn), lambda j, s: (s, j))],
        out_specs=pl.BlockSpec((k, tn), lambda j, s: (0, j)),
        out_shape=jax.ShapeDtypeStruct((k, n), F32), name=name,
        compiler_params=_cp(("parallel", "arbitrary")))(a, b)


def _ew(body, name, t, ncol, wb, ins, outs, accs=(), tt=256):
    tt = _tile(t, tt)
    nt = t // tt
    in_specs, args = [], []
    for arr, kind, coff in ins:
        if kind == "tile":
            spec = pl.BlockSpec((tt, wb), lambda j, i, c=coff: (i, c + j))
        elif kind == "prev":
            spec = pl.BlockSpec((tt, wb), lambda j, i, c=coff: (jnp.maximum(i - 1, 0), c + j))
        elif kind == "next":
            spec = pl.BlockSpec((tt, wb), lambda j, i, c=coff: (jnp.minimum(i + 1, nt - 1), c + j))
        else:
            spec = pl.BlockSpec((arr.shape[0], wb), lambda j, i, c=coff: (0, c + j))
        in_specs.append(spec)
        args.append(arr)
    out_specs = [pl.BlockSpec((tt, wb), lambda j, i: (i, j)) for _ in outs]
    out_shape = [jax.ShapeDtypeStruct((t, ncol * wb), d) for d in outs]
    for r in accs:
        out_specs.append(pl.BlockSpec((r, wb), lambda j, i: (0, j)))
        out_shape.append(jax.ShapeDtypeStruct((r, ncol * wb), F32))

    def kern(*refs):
        body(pl.program_id(1), nt, *refs)

    res = pl.pallas_call(
        kern, grid=(ncol, nt), in_specs=in_specs, out_specs=out_specs, out_shape=out_shape, name=name,
        compiler_params=_cp(("parallel", "arbitrary")))(*args)
    return res


def _silu_parts(x):
    s = _sigmoid(x)
    return x * s, s + x * s * (1.0 - s)


def _ln_fwd(x, y, g, b, name):
    t, d = x.shape
    tt = _tile(t, 256)

    def body(x_ref, y_ref, g_ref, b_ref, o_ref, ob_ref, xh_ref, r_ref):
        z = ALPHA * x_ref[...] + y_ref[...]
        mu = jnp.mean(z, axis=1, keepdims=True)
        zc = z - mu
        var = jnp.mean(zc * zc, axis=1, keepdims=True)
        rstd = lax.rsqrt(var + LN_EPS)
        xh = zc * rstd
        o = xh * g_ref[...] + b_ref[...]
        o_ref[...] = o
        ob_ref[...] = o.astype(BF16)
        xh_ref[...] = xh
        r_ref[...] = rstd

    row = pl.BlockSpec((tt, d), lambda i: (i, 0))
    vec = pl.BlockSpec((1, d), lambda i: (0, 0))
    return pl.pallas_call(
        body, grid=(t // tt,), in_specs=[row, row, vec, vec],
        out_specs=[row, row, row, pl.BlockSpec((tt, 1), lambda i: (i, 0))],
        out_shape=[jax.ShapeDtypeStruct((t, d), F32), jax.ShapeDtypeStruct((t, d), BF16),
                   jax.ShapeDtypeStruct((t, d), F32), jax.ShapeDtypeStruct((t, 1), F32)],
        name=name, compiler_params=_cp(("parallel",)))(x, y, g, b)


def _ln_bwd(dout, xhat, rstd, g, name):
    t, d = dout.shape
    tt = _tile(t, 256)

    def body(do_ref, xh_ref, r_ref, g_ref, dz_ref, dzb_ref, acc_ref):
        @pl.when(pl.program_id(0) == 0)
        def _():
            acc_ref[...] = jnp.zeros_like(acc_ref)

        do = do_ref[...]
        xh = xh_ref[...]
        dxh = do * g_ref[...]
        m1 = jnp.mean(dxh, axis=1, keepdims=True)
        m2 = jnp.mean(dxh * xh, axis=1, keepdims=True)
        dz = r_ref[...] * (dxh - m1 - xh * m2)
        dz_ref[...] = dz
        dzb_ref[...] = dz.astype(BF16)
        acc_ref[0:1, :] += jnp.sum(do * xh, axis=0, keepdims=True)
        acc_ref[1:2, :] += jnp.sum(do, axis=0, keepdims=True)

    row = pl.BlockSpec((tt, d), lambda i: (i, 0))
    return pl.pallas_call(
        body, grid=(t // tt,),
        in_specs=[row, row, pl.BlockSpec((tt, 1), lambda i: (i, 0)), pl.BlockSpec((1, d), lambda i: (0, 0))],
        out_specs=[row, row, pl.BlockSpec((SUBLANES, d), lambda i: (0, 0))],
        out_shape=[jax.ShapeDtypeStruct((t, d), F32), jax.ShapeDtypeStruct((t, d), BF16),
                   jax.ShapeDtypeStruct((SUBLANES, d), F32)],
        name=name, compiler_params=_cp(("arbitrary",)))(dout, xhat, rstd, g)


def _loss_head(y, target):
    t, d = y.shape
    tt = _tile(t, 256)

    def body(y_ref, t_ref, acc_ref, dy_ref):
        @pl.when(pl.program_id(0) == 0)
        def _():
            acc_ref[...] = jnp.zeros_like(acc_ref)

        err = y_ref[...] - t_ref[...]
        dy_ref[...] = err * (1.0 / d)
        acc_ref[0:1, :] += jnp.sum(err * err, axis=0, keepdims=True)

    row = pl.BlockSpec((tt, d), lambda i: (i, 0))
    acc, dy = pl.pallas_call(
        body, grid=(t // tt,), in_specs=[row, row],
        out_specs=[pl.BlockSpec((SUBLANES, d), lambda i: (0, 0)), row],
        out_shape=[jax.ShapeDtypeStruct((SUBLANES, d), F32), jax.ShapeDtypeStruct((t, d), F32)],
        name="loss_head", compiler_params=_cp(("arbitrary",)))(y, target)
    return acc, dy


def _tri(lower):
    r = lax.broadcasted_iota(jnp.int32, (HGRN_CHUNK, HGRN_CHUNK), 0)
    c = lax.broadcasted_iota(jnp.int32, (HGRN_CHUNK, HGRN_CHUNK), 1)
    return jnp.where((r >= c) if lower else (r <= c), 1.0, 0.0).astype(BF16)


def _exact_tri_matmul(tri, x):
    hi = x.astype(BF16)
    r1 = x - hi.astype(F32)
    mid = r1.astype(BF16)
    lo = (r1 - mid.astype(F32)).astype(BF16)
    return _dot_nn(tri, hi) + _dot_nn(tri, mid) + _dot_nn(tri, lo)


def _hgrn_gates(q_raw, fl, lb):
    sq = _sigmoid(q_raw)
    qf = q_raw * sq * (HGRN_HEAD_DIM ** -0.5)
    sg = _sigmoid(fl)
    f = lb + (1.0 - lb) * sg
    return qf, sq, sg, f


HGRN_SUB = 16
HGRN_NSUB = HGRN_CHUNK // HGRN_SUB


def _block_row(x, r):
    d = x.shape[1]
    x3 = x.reshape(HGRN_NSUB, HGRN_SUB, d)
    return jnp.broadcast_to(x3[:, r:r + 1, :], (HGRN_NSUB, HGRN_SUB, d)).reshape(HGRN_CHUNK, d)


def _block_sum(x):
    return jnp.sum(x.reshape(HGRN_NSUB, HGRN_SUB, x.shape[1]), axis=1, keepdims=True)


def _hgrn_intra_fwd(qf, k, v, b):
    ch, sub, hd = HGRN_CHUNK, HGRN_SUB, qf.shape[1]
    tl = lax.broadcasted_iota(jnp.int32, (ch, hd), 0) & (sub - 1)
    acc = jnp.zeros((ch, hd), F32)
    for r in range(sub):
        e = jnp.exp(jnp.where(tl >= r, b - _block_row(b, r), NEG_INF))
        pcol = jnp.sum(qf * e * _block_row(k, r), axis=1, keepdims=True)
        acc = acc + pcol * _block_row(v, r)
    for j in range(HGRN_NSUB - 1):
        lo = sub * (j + 1)
        c = b[lo - 1:lo, :]
        qj = (qf[lo:] * jnp.exp(b[lo:] - c)).astype(BF16)
        kj = (k[lo - sub:lo] * jnp.exp(c - b[lo - sub:lo])).astype(BF16)
        pj = _dot_nt(qj, kj)
        contrib = _dot_nn(pj.astype(BF16), v[lo - sub:lo].astype(BF16))
        acc = acc + jnp.concatenate([jnp.zeros((lo, hd), F32), contrib], axis=0)
    return acc


def _hgrn_intra_bwd(qf, k, v, b, do_v, dk_scr, dv_scr):
    ch, sub, hd = HGRN_CHUNK, HGRN_SUB, qf.shape[1]
    tl = lax.broadcasted_iota(jnp.int32, (ch, hd), 0) & (sub - 1)
    dq = jnp.zeros((ch, hd), F32)
    for r in range(sub):
        kr = _block_row(k, r)
        e = jnp.exp(jnp.where(tl >= r, b - _block_row(b, r), NEG_INF))
        qe = qf * e
        pcol = jnp.sum(qe * kr, axis=1, keepdims=True)
        dpcol = jnp.sum(do_v * _block_row(v, r), axis=1, keepdims=True)
        dq = dq + dpcol * (kr * e)
        dk_scr[:, r:r + 1, :] = _block_sum(dpcol * qe)
        dv_scr[:, r:r + 1, :] = _block_sum(pcol * do_v)
    dk = dk_scr[...].reshape(ch, hd)
    dv = dv_scr[...].reshape(ch, hd)
    do_b, v_b = do_v.astype(BF16), v.astype(BF16)
    dk_off, dv_off = [], []
    for j in range(HGRN_NSUB - 1):
        lo = sub * (j + 1)
        c = b[lo - 1:lo, :]
        eq = jnp.exp(b[lo:] - c)
        ek = jnp.exp(c - b[lo - sub:lo])
        qj = (qf[lo:] * eq).astype(BF16)
        kj = (k[lo - sub:lo] * ek).astype(BF16)
        p_t = _dot_nt(kj, qj).astype(BF16)
        dp = _dot_nt(do_b[lo:], v_b[lo - sub:lo]).astype(BF16)
        dp_t = _dot_nt(v_b[lo - sub:lo], do_b[lo:]).astype(BF16)
        dq = dq + jnp.concatenate([jnp.zeros((lo, hd), F32), _dot_nn(dp, kj) * eq], axis=0)
        dk_off.append(_dot_nn(dp_t, qj) * ek)
        dv_off.append(_dot_nn(p_t, do_b[lo:]))
    zero = jnp.zeros((sub, hd), F32)
    dk = dk + jnp.concatenate(dk_off + [zero], axis=0)
    dv = dv + jnp.concatenate(dv_off + [zero], axis=0)
    return dq, dk, dv


def _hgrn_fwd(u, lb, off, name):
    t = u.shape[0]
    w = lb.shape[1]
    hd, ch = HGRN_HEAD_DIM, HGRN_CHUNK
    nh, nc = w // hd, t // ch
    cq, cf, cv = off["a_q"] // hd, off["a_f"] // hd, off["a_i"] // hd

    def body(q_ref, f_ref, v_ref, lb_ref, o_ref, st_ref, state):
        @pl.when(pl.program_id(1) == 0)
        def _():
            state[...] = jnp.zeros_like(state)

        st = state[...]
        st_ref[0, 0] = st
        qf, _, _, f = _hgrn_gates(q_ref[...], f_ref[...], lb_ref[...])
        k = 1.0 - f
        v = v_ref[...]
        b = _exact_tri_matmul(_tri(True), jnp.log(f))
        inter = _dot_nt((qf * jnp.exp(b)).astype(BF16), st.astype(BF16))
        o_ref[...] = inter + _hgrn_intra_fwd(qf, k, v, b)
        b_end = b[ch - 1:ch, :]
        kd = k * jnp.exp(b_end - b)
        state[...] = st * jnp.exp(b_end) + _dot_tn(v.astype(BF16), kd.astype(BF16))

    return pl.pallas_call(
        body, grid=(nh, nc),
        in_specs=[pl.BlockSpec((ch, hd), lambda h, n: (n, cq + h)),
                  pl.BlockSpec((ch, hd), lambda h, n: (n, cf + h)),
                  pl.BlockSpec((ch, hd), lambda h, n: (n, cv + h)),
                  pl.BlockSpec((1, hd), lambda h, n: (0, h))],
        out_specs=[pl.BlockSpec((ch, hd), lambda h, n: (n, h)),
                   pl.BlockSpec((1, 1, hd, hd), lambda h, n: (h, n, 0, 0))],
        out_shape=[jax.ShapeDtypeStruct((t, w), F32), jax.ShapeDtypeStruct((nh, nc, hd, hd), F32)],
        scratch_shapes=[pltpu.VMEM((hd, hd), F32)],
        name=name, compiler_params=_cp(("parallel", "arbitrary")))(u, u, u, lb)


def _hgrn_bwd(u, lb, states, do, off, name):
    t = u.shape[0]
    w = lb.shape[1]
    hd, ch = HGRN_HEAD_DIM, HGRN_CHUNK
    nh, nc = w // hd, t // ch
    cq, cf, cv = off["a_q"] // hd, off["a_f"] // hd, off["a_i"] // hd

    def body(q_ref, f_ref, v_ref, do_ref, st_ref, lb_ref, dq_ref, df_ref, dv_ref, dlb_ref,
             dstate, dk_scr, dv_scr):
        @pl.when(pl.program_id(1) == 0)
        def _():
            dstate[...] = jnp.zeros_like(dstate)
            dlb_ref[...] = jnp.zeros_like(dlb_ref)

        lb_row = lb_ref[...]
        q_raw = q_ref[...]
        qf, sq, sg, f = _hgrn_gates(q_raw, f_ref[...], lb_row)
        k = 1.0 - f
        b = _exact_tri_matmul(_tri(True), jnp.log(f))
        a = jnp.exp(b)
        b_end = b[ch - 1:ch, :]
        a_end = jnp.exp(b_end)
        to_end = jnp.exp(b_end - b)
        kd = k * to_end
        st0 = st_ref[0, 0]
        ds = dstate[...]
        do_v = do_ref[...]
        v = v_ref[...]
        do_b, v_b, ds_b, kd_b = do_v.astype(BF16), v.astype(BF16), ds.astype(BF16), kd.astype(BF16)
        qa = qf * a

        dq_inter = a * _dot_nn(do_b, st0.astype(BF16))
        dk_end = to_end * _dot_nn(v_b, ds_b)
        dv_end = _dot_nt(kd_b, ds_b)
        st_end = st0 * a_end + _dot_tn(v_b, kd_b)
        db_end = jnp.sum(ds * st_end, axis=0, keepdims=True)
        dstate[...] = ds * a_end + _dot_tn(do_b, qa.astype(BF16))

        rows = lax.broadcasted_iota(jnp.int32, (ch, hd), 0)
        dq_intra, dk_intra, dv_intra = _hgrn_intra_bwd(qf, k, v, b, do_v, dk_scr, dv_scr)
        dqf = dq_inter + dq_intra
        dk = dk_end + dk_intra
        dv = dv_end + dv_intra
        db = qf * dqf - k * dk
        db = db + jnp.where(rows == ch - 1, db_end, 0.0)
        dg = _exact_tri_matmul(_tri(False), db)
        df = dg / f - dk
        dq_ref[...] = (dqf * (HGRN_HEAD_DIM ** -0.5) * (sq + q_raw * sq * (1.0 - sq))).astype(BF16)
        df_ref[...] = (df * (1.0 - lb_row) * sg * (1.0 - sg)).astype(BF16)
        dv_ref[...] = dv.astype(BF16)
        dlb_ref[0:1, :] += jnp.sum(df * (1.0 - sg), axis=0, keepdims=True)

    rev = lambda n: nc - 1 - n
    tile = lambda c: pl.BlockSpec((ch, hd), lambda h, n, c=c: (rev(n), c + h))
    return pl.pallas_call(
        body, grid=(nh, nc),
        in_specs=[tile(cq), tile(cf), tile(cv), tile(0),
                  pl.BlockSpec((1, 1, hd, hd), lambda h, n: (h, rev(n), 0, 0)),
                  pl.BlockSpec((1, hd), lambda h, n: (0, h))],
        out_specs=[tile(0), tile(0), tile(0), pl.BlockSpec((SUBLANES, hd), lambda h, n: (0, h))],
        out_shape=[jax.ShapeDtypeStruct((t, w), BF16)] * 3 + [jax.ShapeDtypeStruct((SUBLANES, w), F32)],
        scratch_shapes=[pltpu.VMEM((hd, hd), F32)] + [pltpu.VMEM((HGRN_NSUB, HGRN_SUB, hd), F32)] * 2,
        name=name, compiler_params=_cp(("parallel", "arbitrary")))(u, u, u, do, states, lb)


def _gate_a_fwd(o, u, gain, off, name):
    t, w = o.shape
    hd = HGRN_HEAD_DIM

    def body(i, nt, o_ref, g_ref, gain_ref, p_ref):
        silu, _ = _silu_parts(g_ref[...])
        for h in range(w // hd):
            sl = slice(h * hd, (h + 1) * hd)
            oh = o_ref[:, sl]
            r = lax.rsqrt(jnp.mean(oh * oh, axis=1, keepdims=True) + RMS_EPS)
            p_ref[:, sl] = (oh * r * gain_ref[:, sl] * silu[:, sl]).astype(BF16)

    return _ew(body, name, t, 1, w, [(o, "tile", 0), (u, "tile", off["a_g"] // w), (gain, "row", 0)], [BF16])[0]


def _gate_a_bwd(dp, o, u, gain, off, name):
    t, w = o.shape
    hd = HGRN_HEAD_DIM

    def body(i, nt, dp_ref, o_ref, g_ref, gain_ref, do_ref, dg_ref, acc_ref):
        @pl.when(i == 0)
        def _():
            acc_ref[...] = jnp.zeros_like(acc_ref)

        silu, dsilu = _silu_parts(g_ref[...])
        dp_v = dp_ref[...]
        for h in range(w // hd):
            sl = slice(h * hd, (h + 1) * hd)
            oh = o_ref[:, sl]
            r = lax.rsqrt(jnp.mean(oh * oh, axis=1, keepdims=True) + RMS_EPS)
            nrm = oh * r
            gn = gain_ref[:, sl]
            dph = dp_v[:, sl]
            dg_ref[:, sl] = (dph * nrm * gn * dsilu[:, sl]).astype(BF16)
            acc_ref[0:1, sl] += jnp.sum(dph * nrm * silu[:, sl], axis=0, keepdims=True)
            dn = dph * gn * silu[:, sl]
            do_ref[:, sl] = r * (dn - nrm * jnp.mean(dn * nrm, axis=1, keepdims=True))

    return _ew(body, name, t, 1, w,
               [(dp, "tile", 0), (o, "tile", 0), (u, "tile", off["a_g"] // w), (gain, "row", 0)],
               [F32, BF16], accs=[SUBLANES])


def _bucket_map():
    i = np.arange(WINDOW)[:, None]
    j = np.arange(2 * WINDOW)[None, :]
    dist = np.clip(WINDOW + i - j, 0, WINDOW - 1)
    max_exact = N_BUCKETS // 2
    logd = (np.log(np.maximum(dist, 1).astype(np.float32) / max_exact) / math.log(MAX_DISTANCE / max_exact))
    large = np.minimum(max_exact + (logd.astype(np.float32) * (N_BUCKETS - max_exact)).astype(np.int32), N_BUCKETS - 1)
    return np.where(dist < max_exact, dist, large).astype(np.int32)


def _bias_table(rel_bias, n_heads):
    bucket = jnp.asarray(_bucket_map())

    def body(rb_ref, bk_ref, o_ref):
        bk = bk_ref[...]
        for h in range(n_heads):
            def step(bi, acc):
                return jnp.where(bk == bi, rb_ref[bi, h], acc)
            o_ref[h] = lax.fori_loop(0, N_BUCKETS, step, jnp.zeros((WINDOW, 2 * WINDOW), F32))

    return pl.pallas_call(
        body, in_specs=[pl.BlockSpec(memory_space=pltpu.SMEM), pl.BlockSpec(memory_space=pltpu.VMEM)],
        out_specs=pl.BlockSpec(memory_space=pltpu.VMEM),
        out_shape=jax.ShapeDtypeStruct((n_heads, WINDOW, 2 * WINDOW), F32), name="bias_table",
        compiler_params=_cp())(rel_bias, bucket)


def _bias_grad(dbias, n_heads):
    bucket = jnp.asarray(_bucket_map())

    def body(db_ref, bk_ref, o_ref):
        bk = bk_ref[...]
        lane = lax.broadcasted_iota(jnp.int32, (1, LANES), 1)

        def step(bi, carry):
            row = jnp.zeros((1, LANES), F32)
            for h in range(n_heads):
                val = jnp.sum(jnp.where(bk == bi, db_ref[h], 0.0))
                row = jnp.where(lane == h, val, row)
            o_ref[pl.ds(bi, 1), :] = row
            return carry

        lax.fori_loop(0, N_BUCKETS, step, 0)

    return pl.pallas_call(
        body, in_specs=[pl.BlockSpec(memory_space=pltpu.VMEM), pl.BlockSpec(memory_space=pltpu.VMEM)],
        out_specs=pl.BlockSpec(memory_space=pltpu.VMEM),
        out_shape=jax.ShapeDtypeStruct((N_BUCKETS, LANES), F32), name="bias_grad",
        compiler_params=_cp())(dbias, bucket)


def _attn_probs(n, q_ref, kp_ref, kc_ref, bias_ref, sink_ref, hh, grp):
    ad, wn = ATTN_HEAD_DIM, WINDOW
    ksl = slice(hh * ad, (hh + 1) * ad)
    kw = jnp.concatenate([kp_ref[:, ksl], kc_ref[:, ksl]], axis=0).astype(BF16)
    qs = jnp.concatenate([q_ref[:, (hh * grp + g) * ad:(hh * grp + g + 1) * ad] for g in range(grp)], axis=0).astype(BF16)
    s = _dot_nt(qs, kw) * (ad ** -0.5) + bias_ref[hh]
    r = lax.broadcasted_iota(jnp.int32, (grp * wn, 2 * wn), 0)
    j = lax.broadcasted_iota(jnp.int32, (grp * wn, 2 * wn), 1)
    i = r & (wn - 1)
    valid = ((j >= wn) & (j - wn <= i)) | ((j < wn) & (j > i) & (n > 0))
    s = jnp.where(valid, s, NEG_INF)
    rr = lax.broadcasted_iota(jnp.int32, (grp * wn, 1), 0) >> WINDOW_SHIFT
    sink = jnp.zeros((grp * wn, 1), F32)
    for g in range(grp):
        sink = jnp.where(rr == g, sink_ref[hh * grp + g], sink)
    m = jnp.maximum(jnp.max(s, axis=1, keepdims=True), sink)
    p = jnp.exp(s - m)
    es = jnp.exp(sink - m)
    inv = 1.0 / (jnp.sum(p, axis=1, keepdims=True) + es)
    return qs, kw, p * inv, es * inv


def _attn_fwd(u, bias_g, sinks, off, w, name):
    t = u.shape[0]
    wn, ad, kvw = WINDOW, ATTN_HEAD_DIM, KV_WIDTH
    grp = (w // ad) // ATTN_KV_HEADS
    nb = t // wn
    cq, ck, cv = off["b_q"] // w, off["b_k"] // kvw, off["b_v"] // kvw

    def body(q_ref, kp_ref, kc_ref, vp_ref, vc_ref, bias_ref, sink_ref, o_ref):
        n = pl.program_id(0)
        for hh in range(ATTN_KV_HEADS):
            _, _, p, _ = _attn_probs(n, q_ref, kp_ref, kc_ref, bias_ref, sink_ref, hh, grp)
            ksl = slice(hh * ad, (hh + 1) * ad)
            vw = jnp.concatenate([vp_ref[:, ksl], vc_ref[:, ksl]], axis=0).astype(BF16)
            o = _dot_nn(p.astype(BF16), vw)
            for g in range(grp):
                o_ref[:, (hh * grp + g) * ad:(hh * grp + g + 1) * ad] = o[g * wn:(g + 1) * wn]

    prev = lambda n: jnp.maximum(n - 1, 0)
    return pl.pallas_call(
        body, grid=(nb,),
        in_specs=[pl.BlockSpec((wn, w), lambda n: (n, cq)),
                  pl.BlockSpec((wn, kvw), lambda n: (prev(n), ck)), pl.BlockSpec((wn, kvw), lambda n: (n, ck)),
                  pl.BlockSpec((wn, kvw), lambda n: (prev(n), cv)), pl.BlockSpec((wn, kvw), lambda n: (n, cv)),
                  pl.BlockSpec((ATTN_KV_HEADS, grp * wn, 2 * wn), lambda n: (0, 0, 0)),
                  pl.BlockSpec(memory_space=pltpu.SMEM)],
        out_specs=pl.BlockSpec((wn, w), lambda n: (n, 0)),
        out_shape=jax.ShapeDtypeStruct((t, w), F32), name=name,
        compiler_params=_cp(("parallel",)))(u, u, u, u, u, bias_g, sinks)


def _attn_bwd(u, o, do, bias_g, sinks, off, w, name):
    t = u.shape[0]
    wn, ad, kvw = WINDOW, ATTN_HEAD_DIM, KV_WIDTH
    grp = (w // ad) // ATTN_KV_HEADS
    nb = t // wn
    cq, ck, cv = off["b_q"] // w, off["b_k"] // kvw, off["b_v"] // kvw

    def body(q_ref, kp_ref, kc_ref, vp_ref, vc_ref, o_ref, do_ref, bias_ref, sink_ref,
             dq_ref, dkc_ref, dkp_ref, dvc_ref, dvp_ref, dbias_ref, dsink_ref):
        n = pl.program_id(0)

        @pl.when(n == 0)
        def _():
            dbias_ref[...] = jnp.zeros_like(dbias_ref)
            dsink_ref[...] = jnp.zeros_like(dsink_ref)

        lane = lax.broadcasted_iota(jnp.int32, (1, LANES), 1)
        rr = lax.broadcasted_iota(jnp.int32, (grp * wn, 1), 0) >> WINDOW_SHIFT
        dsink_row = jnp.zeros((1, LANES), F32)
        for hh in range(ATTN_KV_HEADS):
            qs, kw, p, psink = _attn_probs(n, q_ref, kp_ref, kc_ref, bias_ref, sink_ref, hh, grp)
            ksl = slice(hh * ad, (hh + 1) * ad)
            vw = jnp.concatenate([vp_ref[:, ksl], vc_ref[:, ksl]], axis=0).astype(BF16)
            hs = [slice((hh * grp + g) * ad, (hh * grp + g + 1) * ad) for g in range(grp)]
            dos = jnp.concatenate([do_ref[:, sl] for sl in hs], axis=0)
            os_ = jnp.concatenate([o_ref[:, sl] for sl in hs], axis=0)
            delta = jnp.sum(dos * os_, axis=1, keepdims=True)
            dos_b = dos.astype(BF16)
            dp = _dot_nt(dos_b, vw)
            ds = p * (dp - delta)
            dbias_ref[hh] += ds
            sd = psink * delta
            for g in range(grp):
                val = -jnp.sum(jnp.where(rr == g, sd, 0.0))
                dsink_row = jnp.where(lane == hh * grp + g, val, dsink_row)
            ds_b = (ds * (ad ** -0.5)).astype(BF16)
            dq = _dot_nn(ds_b, kw)
            for g in range(grp):
                dq_ref[:, hs[g]] = dq[g * wn:(g + 1) * wn].astype(BF16)
            dkw = _dot_tn(ds_b, qs)
            dvw = _dot_tn(p.astype(BF16), dos_b)
            dkp_ref[:, ksl] = dkw[:wn]
            dkc_ref[:, ksl] = dkw[wn:]
            dvp_ref[:, ksl] = dvw[:wn]
            dvc_ref[:, ksl] = dvw[wn:]
        dsink_ref[0:1, :] += dsink_row

    prev = lambda n: jnp.maximum(n - 1, 0)
    kv_out = pl.BlockSpec((wn, kvw), lambda n: (n, 0))
    return pl.pallas_call(
        body, grid=(nb,),
        in_specs=[pl.BlockSpec((wn, w), lambda n: (n, cq)),
                  pl.BlockSpec((wn, kvw), lambda n: (prev(n), ck)), pl.BlockSpec((wn, kvw), lambda n: (n, ck)),
                  pl.BlockSpec((wn, kvw), lambda n: (prev(n), cv)), pl.BlockSpec((wn, kvw), lambda n: (n, cv)),
                  pl.BlockSpec((wn, w), lambda n: (n, 0)), pl.BlockSpec((wn, w), lambda n: (n, 0)),
                  pl.BlockSpec((ATTN_KV_HEADS, grp * wn, 2 * wn), lambda n: (0, 0, 0)),
                  pl.BlockSpec(memory_space=pltpu.SMEM)],
        out_specs=[pl.BlockSpec((wn, w), lambda n: (n, 0)), kv_out, kv_out, kv_out, kv_out,
                   pl.BlockSpec((ATTN_KV_HEADS, grp * wn, 2 * wn), lambda n: (0, 0, 0)),
                   pl.BlockSpec((SUBLANES, LANES), lambda n: (0, 0))],
        out_shape=[jax.ShapeDtypeStruct((t, w), BF16)] + [jax.ShapeDtypeStruct((t, kvw), F32)] * 4
        + [jax.ShapeDtypeStruct((ATTN_KV_HEADS, grp * wn, 2 * wn), F32), jax.ShapeDtypeStruct((SUBLANES, LANES), F32)],
        name=name, compiler_params=_cp(("arbitrary",)))(u, u, u, u, u, o, do, bias_g, sinks)


def _kv_combine(cur, prv, name):
    t, kvw = cur.shape

    def body(i, nt, c_ref, p_ref, o_ref):
        nxt = jnp.where(i < nt - 1, p_ref[...], 0.0)
        o_ref[...] = (c_ref[...] + nxt).astype(BF16)

    return _ew(body, name, t, 1, kvw, [(cur, "tile", 0), (prv, "next", 0)], [BF16], tt=WINDOW)[0]


def _gate_b_fwd(o, u, off, name):
    t, w = o.shape
    wb = 512

    def body(i, nt, o_ref, g_ref, p_ref):
        silu, _ = _silu_parts(g_ref[...])
        p_ref[...] = (o_ref[...] * silu).astype(BF16)

    return _ew(body, name, t, w // wb, wb, [(o, "tile", 0), (u, "tile", off["b_g"] // wb)], [BF16])[0]


def _gate_b_bwd(dp, o, u, off, name):
    t, w = o.shape
    wb = 512

    def body(i, nt, dp_ref, o_ref, g_ref, do_ref, dg_ref):
        silu, dsilu = _silu_parts(g_ref[...])
        dp_v = dp_ref[...]
        do_ref[...] = dp_v * silu
        dg_ref[...] = (dp_v * o_ref[...] * dsilu).astype(BF16)

    return _ew(body, name, t, w // wb, wb,
               [(dp, "tile", 0), (o, "tile", 0), (u, "tile", off["b_g"] // wb)], [F32, BF16])


def _shift_down(h, tail, k, rows):
    tt = h.shape[0]
    out = pltpu.roll(h, k, 0)
    for r in range(k):
        out = jnp.where(rows == r, tail[tt - k + r:tt - k + r + 1, :], out)
    return out


def _shift_up(h, head, k, rows):
    tt = h.shape[0]
    out = pltpu.roll(h, tt - k, 0)
    for r in range(k):
        out = jnp.where(rows == tt - k + r, head[r:r + 1, :], out)
    return out


def _conv_fwd(u, conv_w, off, w, name):
    t = u.shape[0]
    wb = 512
    c = lambda nme: off[nme] // wb

    def body(i, nt, cb_ref, cc_ref, ccp_ref, cx_ref, cxp_ref, cg_ref, w_ref, p_ref):
        h = cc_ref[...] * cx_ref[...]
        hp = jnp.where(i > 0, ccp_ref[...] * cxp_ref[...], 0.0)
        rows = lax.broadcasted_iota(jnp.int32, h.shape, 0)
        y = w_ref[0:1, :] * _shift_down(h, hp, 2, rows) + w_ref[1:2, :] * _shift_down(h, hp, 1, rows) + w_ref[2:3, :] * h
        silu, _ = _silu_parts(cg_ref[...])
        p_ref[...] = (cb_ref[...] * y * silu).astype(BF16)

    return _ew(body, name, t, w // wb, wb,
               [(u, "tile", c("c_b")), (u, "tile", c("c_c")), (u, "prev", c("c_c")), (u, "tile", c("c_x")),
                (u, "prev", c("c_x")), (u, "tile", c("c_g")), (conv_w, "row", 0)], [BF16])[0]


def _conv_bwd(dp, u, conv_w, off, w, name):
    t = u.shape[0]
    wb = 512
    c = lambda nme: off[nme] // wb

    def body(i, nt, dp_ref, dpn_ref, cb_ref, cbn_ref, cg_ref, cgn_ref, cc_ref, ccp_ref, cx_ref, cxp_ref, w_ref,
             dcb_ref, dcc_ref, dcx_ref, dcg_ref, acc_ref):
        @pl.when(i == 0)
        def _():
            acc_ref[...] = jnp.zeros_like(acc_ref)

        cc, cx, cb = cc_ref[...], cx_ref[...], cb_ref[...]
        h = cc * cx
        hp = jnp.where(i > 0, ccp_ref[...] * cxp_ref[...], 0.0)
        rows = lax.broadcasted_iota(jnp.int32, h.shape, 0)
        h1 = _shift_down(h, hp, 1, rows)
        h2 = _shift_down(h, hp, 2, rows)
        w0, w1, w2 = w_ref[0:1, :], w_ref[1:2, :], w_ref[2:3, :]
        y = w0 * h2 + w1 * h1 + w2 * h
        silu, dsilu = _silu_parts(cg_ref[...])
        dp_v = dp_ref[...]
        dcg_ref[...] = (dp_v * cb * y * dsilu).astype(BF16)
        dcb_ref[...] = (dp_v * y * silu).astype(BF16)
        dy = dp_v * cb * silu
        silu_n, _ = _silu_parts(cgn_ref[...])
        dyn = jnp.where(i < nt - 1, dpn_ref[...] * cbn_ref[...] * silu_n, 0.0)
        dh = w2 * dy + w1 * _shift_up(dy, dyn, 1, rows) + w0 * _shift_up(dy, dyn, 2, rows)
        dcc_ref[...] = (dh * cx).astype(BF16)
        dcx_ref[...] = (dh * cc).astype(BF16)
        acc_ref[0:1, :] += jnp.sum(dy * h2, axis=0, keepdims=True)
        acc_ref[1:2, :] += jnp.sum(dy * h1, axis=0, keepdims=True)
        acc_ref[2:3, :] += jnp.sum(dy * h, axis=0, keepdims=True)

    return _ew(body, name, t, w // wb, wb,
               [(dp, "tile", 0), (dp, "next", 0), (u, "tile", c("c_b")), (u, "next", c("c_b")),
                (u, "tile", c("c_g")), (u, "next", c("c_g")), (u, "tile", c("c_c")), (u, "prev", c("c_c")),
                (u, "tile", c("c_x")), (u, "prev", c("c_x")), (conv_w, "row", 0)],
               [BF16] * 4, accs=[SUBLANES])


def _merge_fwd(u, ya, yb, yc, off, d, name):
    t = u.shape[0]
    wb = 512
    c = lambda nme: off[nme] // wb

    def body(i, nt, ma_ref, mb_ref, mc_ref, ya_ref, yb_ref, yc_ref, o_ref):
        o_ref[...] = (_sigmoid(ma_ref[...]) * ya_ref[...] + _sigmoid(mb_ref[...]) * yb_ref[...]
                      + _sigmoid(mc_ref[...]) * yc_ref[...]).astype(BF16)

    return _ew(body, name, t, d // wb, wb,
               [(u, "tile", c("m_a")), (u, "tile", c("m_b")), (u, "tile", c("m_c")),
                (ya, "tile", 0), (yb, "tile", 0), (yc, "tile", 0)], [BF16])[0]


def _merge_bwd(dm, u, ya, yb, yc, off, d, name):
    t = u.shape[0]
    wb = 512
    c = lambda nme: off[nme] // wb

    def body(i, nt, dm_ref, ma_ref, mb_ref, mc_ref, ya_ref, yb_ref, yc_ref, da_ref, db_ref, dc_ref, ga_ref, gb_ref, gc_ref):
        dm_v = dm_ref[...]
        for m_ref, y_ref, dy_ref, dg_ref in ((ma_ref, ya_ref, da_ref, ga_ref), (mb_ref, yb_ref, db_ref, gb_ref),
                                             (mc_ref, yc_ref, dc_ref, gc_ref)):
            s = _sigmoid(m_ref[...])
            dy_ref[...] = (dm_v * s).astype(BF16)
            dg_ref[...] = (dm_v * y_ref[...] * s * (1.0 - s)).astype(BF16)

    return _ew(body, name, t, d // wb, wb,
               [(dm, "tile", 0), (u, "tile", c("m_a")), (u, "tile", c("m_b")), (u, "tile", c("m_c")),
                (ya, "tile", 0), (yb, "tile", 0), (yc, "tile", 0)], [BF16] * 6)


def _lower_bounds(lb_param):
    def body(p_ref, o_ref):
        p = p_ref[...]
        e = jnp.exp(p - jnp.max(p, axis=0, keepdims=True))
        soft = e / jnp.sum(e, axis=0, keepdims=True)
        acc = jnp.zeros_like(soft[0:1])
        o_ref[0:1, :] = acc
        for l in range(1, DEPTH):
            acc = acc + soft[l:l + 1]
            o_ref[l:l + 1, :] = acc

    return pl.pallas_call(body, out_shape=jax.ShapeDtypeStruct(lb_param.shape, F32), name="lower_bounds",
                          compiler_params=_cp())(lb_param)


def _lower_bounds_bwd(lb_param, dlower):
    def body(p_ref, d_ref, o_ref):
        p = p_ref[...]
        e = jnp.exp(p - jnp.max(p, axis=0, keepdims=True))
        soft = e / jnp.sum(e, axis=0, keepdims=True)
        dl = d_ref[...]
        ds = [jnp.zeros_like(dl[0:1])]
        for j in range(1, DEPTH):
            acc = dl[j:j + 1]
            for l in range(j + 1, DEPTH):
                acc = acc + dl[l:l + 1]
            ds.append(acc)
        inner = ds[0] * soft[0:1]
        for j in range(1, DEPTH):
            inner = inner + ds[j] * soft[j:j + 1]
        for j in range(DEPTH):
            o_ref[j:j + 1, :] = soft[j:j + 1] * (ds[j] - inner)

    return pl.pallas_call(body, out_shape=jax.ShapeDtypeStruct(lb_param.shape, F32), name="lower_bounds_bwd",
                          compiler_params=_cp())(lb_param, dlower)


def _exchange(arrays, scatter, name):
    n_arr = len(arrays)

    def body(*refs):
        srcs, dsts = refs[:n_arr], refs[n_arr:2 * n_arr]
        send_sems, recv_sems, local_sems = refs[2 * n_arr:]
        me = 4 * lax.axis_index("x") + 2 * lax.axis_index("y") + lax.axis_index("c")
        copies = _peer_copies(srcs, dsts, send_sems, recv_sems, scatter)
        for a in range(n_arr):
            copies.append(pltpu.make_async_copy(srcs[a].at[me] if scatter else srcs[a], dsts[a].at[me], local_sems.at[a]))
        for cp in copies:
            cp.start()
        for cp in copies:
            cp.wait()

    out_shape = [jax.ShapeDtypeStruct(a.shape if scatter else (N_DEV,) + a.shape, a.dtype) for a in arrays]
    anyspec = pl.BlockSpec(memory_space=pl.ANY)
    res = pl.pallas_call(
        body, in_specs=[anyspec] * n_arr, out_specs=[anyspec] * n_arr, out_shape=out_shape,
        scratch_shapes=[pltpu.SemaphoreType.DMA((n_arr * (N_DEV - 1),)), pltpu.SemaphoreType.DMA((n_arr * (N_DEV - 1),)),
                        pltpu.SemaphoreType.DMA((n_arr,))],
        name=name)(*arrays)
    return list(res)


def _peer_copies(srcs, lands, send_sems, recv_sems, scatter):
    x, y, c = lax.axis_index("x"), lax.axis_index("y"), lax.axis_index("c")
    me = 4 * x + 2 * y + c
    copies = []
    for a in range(len(srcs)):
        for k in range(1, N_DEV):
            px = 1 - x if k & 4 else x
            py = 1 - y if k & 2 else y
            pc = 1 - c if k & 1 else c
            src = srcs[a].at[4 * px + 2 * py + pc] if scatter else srcs[a]
            copies.append(pltpu.make_async_remote_copy(
                src_ref=src, dst_ref=lands[a].at[me],
                send_sem=send_sems.at[a * (N_DEV - 1) + k - 1], recv_sem=recv_sems.at[a * (N_DEV - 1) + k - 1],
                device_id=(px, py, pc), device_id_type=pl.DeviceIdType.MESH))
    return copies


_HBM_SPEC = pl.BlockSpec(memory_space=pltpu.HBM)
_SEM_SPEC = pl.BlockSpec(memory_space=pltpu.SEMAPHORE)
_ANY_SPEC = pl.BlockSpec(memory_space=pl.ANY)
_DATAFLOW = pltpu.SideEffectType.DATAFLOW_SIDE_EFFECTING


def _exchange_start(arrays, scatter, name, dep=None):
    n_arr = len(arrays)
    n_sem = n_arr * (N_DEV - 1)
    me = 4 * lax.axis_index("x") + 2 * lax.axis_index("y") + lax.axis_index("c")
    lands = []
    for a in arrays:
        own = lax.dynamic_index_in_dim(a, me, 0, keepdims=False) if scatter else a
        shape = a.shape if scatter else (N_DEV,) + a.shape
        lands.append(lax.dynamic_update_index_in_dim(lax.empty(shape, a.dtype), own, me, 0))
    dep_specs, dep_args = _dep_specs(dep)

    def body(*refs):
        srcs, lnds = refs[:n_arr], refs[n_arr:2 * n_arr]
        outs = refs[2 * n_arr + len(dep_args):]
        send_sems, recv_sems, token = outs[0], outs[1], outs[2 + 2 * n_arr]
        for cp in _peer_copies(srcs, lnds, send_sems, recv_sems, scatter):
            cp.start()
        token[...] = jnp.zeros_like(token)

    thru = [pltpu.HBM(a.shape, a.dtype) for a in list(arrays) + lands]
    return pl.pallas_call(
        body, name=name,
        out_shape=(pltpu.SemaphoreType.DMA((n_sem,)), pltpu.SemaphoreType.DMA((n_sem,)), *thru,
                   jax.ShapeDtypeStruct((SUBLANES, LANES), F32)),
        in_specs=[_HBM_SPEC] * (2 * n_arr) + dep_specs,
        out_specs=(_SEM_SPEC, _SEM_SPEC, *[_HBM_SPEC] * (2 * n_arr), pl.BlockSpec(memory_space=pltpu.VMEM)),
        input_output_aliases={i: 2 + i for i in range(2 * n_arr)},
        compiler_params=pltpu.CompilerParams(has_side_effects=_DATAFLOW),
    )(*[pltpu.with_memory_space_constraint(a, pltpu.HBM) for a in list(arrays) + lands], *dep_args)


def _exchange_wait(started, scatter, name, after):
    send_sems, recv_sems = started[0], started[1]
    thru = list(started[2:-1])
    n_arr = len(thru) // 2

    def body(*refs):
        srcs, lnds = refs[:n_arr], refs[n_arr:2 * n_arr]
        for cp in _peer_copies(srcs, lnds, refs[2 * n_arr], refs[2 * n_arr + 1], scatter):
            cp.wait_send()
            cp.wait_recv()

    res = pl.pallas_call(
        body, name=name, out_shape=tuple(pltpu.HBM(a.shape, a.dtype) for a in thru),
        in_specs=[_HBM_SPEC] * (2 * n_arr) + [_SEM_SPEC, _SEM_SPEC, _ANY_SPEC],
        out_specs=tuple([_HBM_SPEC] * (2 * n_arr)),
        input_output_aliases={i: i for i in range(2 * n_arr)},
        compiler_params=pltpu.CompilerParams(has_side_effects=_DATAFLOW),
    )(*thru, send_sems, recv_sems, after)
    return list(res[n_arr:])


def _unshard_cols(g, name):
    nd, r, s = g.shape
    tr = _tile(r, 64)

    def body(i_ref, o_ref):
        for p in range(nd):
            o_ref[:, p * s:(p + 1) * s] = i_ref[p]

    return pl.pallas_call(
        body, grid=(r // tr,), in_specs=[pl.BlockSpec((nd, tr, s), lambda i: (0, i, 0))],
        out_specs=pl.BlockSpec((tr, nd * s), lambda i: (i, 0)),
        out_shape=jax.ShapeDtypeStruct((r, nd * s), g.dtype), name=name, compiler_params=_cp(("parallel",)))(g)


def _shard_cols(g, name):
    r, n = g.shape
    s = n // N_DEV
    tr = _tile(r, 64)

    def body(i_ref, o_ref):
        for p in range(N_DEV):
            o_ref[p] = i_ref[:, p * s:(p + 1) * s]

    return pl.pallas_call(
        body, grid=(r // tr,), in_specs=[pl.BlockSpec((tr, n), lambda i: (i, 0))],
        out_specs=pl.BlockSpec((N_DEV, tr, s), lambda i: (0, i, 0)),
        out_shape=jax.ShapeDtypeStruct((N_DEV, r, s), g.dtype), name=name, compiler_params=_cp(("parallel",)))(g)


def _slot_sum(slots, name):
    nd, r, c = slots.shape
    tr = _tile(r, 64)

    def body(s_ref, o_ref):
        acc = s_ref[0]
        for p in range(1, nd):
            acc = acc + s_ref[p]
        o_ref[...] = acc

    return pl.pallas_call(
        body, grid=(r // tr,), in_specs=[pl.BlockSpec((nd, tr, c), lambda i: (0, i, 0))],
        out_specs=pl.BlockSpec((tr, c), lambda i: (i, 0)),
        out_shape=jax.ShapeDtypeStruct((r, c), F32), name=name, compiler_params=_cp(("parallel",)))(slots)


def _adamw(w, g, m, v, name):
    r, c = w.shape
    tr = _tile(r, 256)
    c1 = 1.0 - ADAM_B1 ** ADAM_STEP
    c2 = 1.0 - ADAM_B2 ** ADAM_STEP

    def body(w_ref, g_ref, m_ref, v_ref, d_ref, nm_ref, nv_ref):
        gv = g_ref[...]
        nm = ADAM_B1 * m_ref[...] + (1.0 - ADAM_B1) * gv
        nv = ADAM_B2 * v_ref[...] + (1.0 - ADAM_B2) * (gv * gv)
        nm_ref[...] = nm
        nv_ref[...] = nv
        d_ref[...] = -ADAM_LR * ((nm / c1) / (jnp.sqrt(nv / c2) + ADAM_EPS) + ADAM_WD * w_ref[...])

    spec = pl.BlockSpec((tr, c), lambda i: (i, 0))
    return pl.pallas_call(
        body, grid=(r // tr,), in_specs=[spec] * 4, out_specs=[spec] * 3,
        out_shape=[jax.ShapeDtypeStruct((r, c), F32)] * 3, name=name, compiler_params=_cp(("parallel",)))(w, g, m, v)


def _forward_backward(x, target, weights_hook, grads_hook, lb_param, hgrn_norm_g, attn_sinks, rel_bias, ln_g, ln_b):
    t, d = x.shape
    w = d // 2
    off, n_in = _offsets(d)
    n_heads = w // ATTN_HEAD_DIM
    grp = n_heads // ATTN_KV_HEADS

    lower = _lower_bounds(lb_param)
    bias = _bias_table(rel_bias, n_heads)
    bias_g = bias.reshape(ATTN_KV_HEADS, grp * WINDOW, 2 * WINDOW)

    saved, weights = [], []
    xb = x.astype(BF16)
    for l in range(DEPTH):
        wl, token = weights_hook(l, x)
        weights.append(wl)
        s = {"x": x, "xb": xb}
        u = _mm_nn(xb, wl["w_in"], f"in_proj", dep=token)
        s["u"] = u
        lb_l, gain_l, cw_l = lower[l:l + 1], hgrn_norm_g[l:l + 1], wl["conv_w"]
        o_a, states = _hgrn_fwd(u, lb_l, off, f"hgrn_fwd")
        p_a = _gate_a_fwd(o_a, u, gain_l, off, f"gate_a_fwd")
        o_b = _attn_fwd(u, bias_g, attn_sinks[l], off, w, f"attn_fwd")
        p_b = _gate_b_fwd(o_b, u, off, f"gate_b_fwd")
        p_c = _conv_fwd(u, cw_l, off, w, f"conv_fwd")
        y_a = _mm_nn(p_a, wl["w_proj_hgrn"], f"proj_a", tn=2048)
        y_b = _mm_nn(p_b, wl["w_proj_attn"], f"proj_b", tn=2048)
        y_c = _mm_nn(p_c, wl["w_proj_conv"], f"proj_c", tn=2048)
        merged = _merge_fwd(u, y_a, y_b, y_c, off, d, f"merge_fwd")
        y = _mm_nn(merged, wl["w_out"], f"out_proj", tm=512, tn=2048)
        x, xb, xhat, rstd = _ln_fwd(x, y, ln_g[l:l + 1], ln_b[l:l + 1], f"ln_fwd")
        s.update(o_a=o_a, states=states, p_a=p_a, o_b=o_b, p_b=p_b, p_c=p_c, y_a=y_a, y_b=y_b, y_c=y_c,
                 merged=merged, xhat=xhat, rstd=rstd)
        saved.append(s)

    loss_acc, dx = _loss_head(x, target)

    d_ln, d_lower, d_gain, d_sink, d_conv = [None] * DEPTH, [None] * DEPTH, [None] * DEPTH, [None] * DEPTH, [None] * DEPTH
    dbias_total = None
    for l in reversed(range(DEPTH)):
        wl, s = weights[l], saved[l]
        u = s["u"]
        lb_l, gain_l, cw_l = lower[l:l + 1], hgrn_norm_g[l:l + 1], wl["conv_w"]
        dz, dzb, d_ln[l] = _ln_bwd(dx, s["xhat"], s["rstd"], ln_g[l:l + 1], f"ln_bwd")
        g_out = _mm_tn(s["merged"], dzb, f"g_out", tn=2048)
        dmerged = _mm_nt(dzb, wl["w_out"], f"d_merged", tk=2048)
        dya, dyb, dyc, dma, dmb, dmc = _merge_bwd(dmerged, u, s["y_a"], s["y_b"], s["y_c"], off, d, f"merge_bwd")
        g_pa = _mm_tn(s["p_a"], dya, f"g_proj_a", tn=2048)
        g_pb = _mm_tn(s["p_b"], dyb, f"g_proj_b", tn=2048)
        g_pc = _mm_tn(s["p_c"], dyc, f"g_proj_c", tn=2048)
        dpa = _mm_nt(dya, wl["w_proj_hgrn"], f"d_p_a", tk=2048)
        dpb = _mm_nt(dyb, wl["w_proj_attn"], f"d_p_b", tk=2048)
        dpc = _mm_nt(dyc, wl["w_proj_conv"], f"d_p_c", tk=2048)
        do_a, d_ag, d_gain[l] = _gate_a_bwd(dpa, s["o_a"], u, gain_l, off, f"gate_a_bwd")
        d_aq, d_af, d_ai, d_lower[l] = _hgrn_bwd(u, lb_l, s["states"], do_a, off, f"hgrn_bwd")
        do_b, d_bg = _gate_b_bwd(dpb, s["o_b"], u, off, f"gate_b_bwd")
        d_bq, dkc, dkp, dvc, dvp, dbias_l, d_sink[l] = _attn_bwd(u, s["o_b"], do_b, bias_g, attn_sinks[l], off, w, f"attn_bwd")
        d_bk = _kv_combine(dkc, dkp, f"k_combine")
        d_bv = _kv_combine(dvc, dvp, f"v_combine")
        dbias_total = dbias_l if dbias_total is None else dbias_total + dbias_l
        d_cb, d_cc, d_cx, d_cg, d_conv[l] = _conv_bwd(dpc, u, cw_l, off, w, f"conv_bwd")
        du = jnp.concatenate([d_aq, d_af, d_ai, d_ag, d_bq, d_bk, d_bv, d_bg, d_cb, d_cc, d_cx, d_cg, dma, dmb, dmc], axis=1)
        g_in = _mm_tn(s["xb"], du, f"g_in")
        token = grads_hook(l, {"w_in": g_in, "w_proj_hgrn": g_pa, "w_proj_attn": g_pb, "w_proj_conv": g_pc, "w_out": g_out})
        dx = _mm_nt(du, wl["w_in"], f"d_x", add=dz, add_scale=ALPHA, dep=token)

    d_lower_all = jnp.concatenate([a[0:1] for a in d_lower], axis=0)
    small = {
        "lb_param": _lower_bounds_bwd(lb_param, d_lower_all),
        "hgrn_norm_g": jnp.concatenate([a[0:1] for a in d_gain], axis=0),
        "attn_sinks": jnp.concatenate([a[0:1, :n_heads] for a in d_sink], axis=0),
        "conv_w": jnp.stack([a[0:3] for a in d_conv], axis=0),
        "rel_bias": _bias_grad(dbias_total.reshape(n_heads, WINDOW, 2 * WINDOW), n_heads)[:, :n_heads],
        "ln_g": jnp.concatenate([a[0:1] for a in d_ln], axis=0),
        "ln_b": jnp.concatenate([a[1:2] for a in d_ln], axis=0),
    }
    return loss_acc, dx, small


BIG = ("w_in", "w_proj_hgrn", "w_proj_attn", "w_proj_conv", "w_out")
SMALL = ("lb_param", "hgrn_norm_g", "attn_sinks", "conv_w", "rel_bias", "ln_g", "ln_b")
ORDER = ("w_in", "w_proj_hgrn", "w_proj_attn", "w_proj_conv", "w_out", "lb_param", "hgrn_norm_g", "attn_sinks",
         "conv_w", "rel_bias", "ln_g", "ln_b")


def _pack(parts):
    flat = jnp.concatenate([p.reshape(-1) for p in parts])
    n = flat.shape[0]
    unit = SUBLANES * LANES
    total = -(-n // unit) * unit
    return jnp.pad(flat, (0, total - n)).reshape(total // LANES, LANES)


def _unpack(packed, shapes):
    flat = packed.reshape(-1)
    out, o = [], 0
    for shp in shapes:
        n = int(np.prod(shp))
        out.append(flat[o:o + n].reshape(shp))
        o += n
    return out


def kernel(x, w_in, w_proj_hgrn, w_proj_attn, w_proj_conv, w_out, lb_param, hgrn_norm_g, attn_sinks, conv_w, rel_bias, ln_g, ln_b, loss_target, m_w_in, m_w_proj_hgrn, m_w_proj_attn, m_w_proj_conv, m_w_out, m_lb_param, m_hgrn_norm_g, m_attn_sinks, m_conv_w, m_rel_bias, m_ln_g, m_ln_b, v_w_in, v_w_proj_hgrn, v_w_proj_attn, v_w_proj_conv, v_w_out, v_lb_param, v_hgrn_norm_g, v_attn_sinks, v_conv_w, v_rel_bias, v_ln_g, v_ln_b):
    params = dict(w_in=w_in, w_proj_hgrn=w_proj_hgrn, w_proj_attn=w_proj_attn, w_proj_conv=w_proj_conv, w_out=w_out,
                  lb_param=lb_param, hgrn_norm_g=hgrn_norm_g, attn_sinks=attn_sinks, conv_w=conv_w, rel_bias=rel_bias,
                  ln_g=ln_g, ln_b=ln_b)
    mom_m = dict(w_in=m_w_in, w_proj_hgrn=m_w_proj_hgrn, w_proj_attn=m_w_proj_attn, w_proj_conv=m_w_proj_conv,
                 w_out=m_w_out, lb_param=m_lb_param, hgrn_norm_g=m_hgrn_norm_g, attn_sinks=m_attn_sinks,
                 conv_w=m_conv_w, rel_bias=m_rel_bias, ln_g=m_ln_g, ln_b=m_ln_b)
    mom_v = dict(w_in=v_w_in, w_proj_hgrn=v_w_proj_hgrn, w_proj_attn=v_w_proj_attn, w_proj_conv=v_w_proj_conv,
                 w_out=v_w_out, lb_param=v_lb_param, hgrn_norm_g=v_hgrn_norm_g, attn_sinks=v_attn_sinks,
                 conv_w=v_conv_w, rel_bias=v_rel_bias, ln_g=v_ln_g, ln_b=v_ln_b)
    d = x.shape[-1]
    me = 4 * lax.axis_index("x") + 2 * lax.axis_index("y") + lax.axis_index("c")

    def shards_of(l):
        return [params[n][l].astype(BF16) for n in BIG] + [conv_w[l]]

    gathers = {0: _exchange_start(shards_of(0), False, "gather_start_0")}

    def weights_hook(l, x_in):
        got = _exchange_wait(gathers.pop(l), False, f"gather_wait_{l}", x_in)
        token = None
        if l + 1 < DEPTH:
            gathers[l + 1] = _exchange_start(shards_of(l + 1), False, f"gather_start_{l + 1}", dep=got[0])
            token = gathers[l + 1][-1]
        wl = {
            "w_in": _unshard_cols(got[0], "unshard_w_in"),
            "w_proj_hgrn": _unshard_cols(got[1], "unshard_w_proj_hgrn"),
            "w_proj_attn": _unshard_cols(got[2], "unshard_w_proj_attn"),
            "w_proj_conv": _unshard_cols(got[3], "unshard_w_proj_conv"),
            "w_out": got[4].reshape(d, d),
            "conv_w": _unshard_cols(got[5], "unshard_conv_w"),
        }
        return wl, token

    grads = {n: [None] * DEPTH for n in BIG}
    scatters = {}

    def finish_scatter(l, after):
        got = _exchange_wait(scatters.pop(l), True, f"scatter_wait_{l}", after)
        for n, slots in zip(BIG, got):
            grads[n][l] = _slot_sum(slots, f"sum_{n}")
        return got[0]

    def grads_hook(l, g):
        send = [_shard_cols(g["w_in"], "shard_g_in"), _shard_cols(g["w_proj_hgrn"], "shard_g_proj_a"),
                _shard_cols(g["w_proj_attn"], "shard_g_proj_b"), _shard_cols(g["w_proj_conv"], "shard_g_proj_c"),
                g["w_out"].reshape(N_DEV, d // N_DEV, d)]
        dep = finish_scatter(l + 1, send[0]) if l + 1 < DEPTH else None
        scatters[l] = _exchange_start(send, True, f"scatter_start_{l}", dep=dep)
        return scatters[l][-1]

    loss_acc, dx, small = _forward_backward(
        x[0], loss_target[0], weights_hook, grads_hook, lb_param, hgrn_norm_g, attn_sinks, rel_bias, ln_g, ln_b)
    loss = lax.psum(0.5 * jnp.sum(loss_acc[0]) / d, ("x", "y", "c"))
    finish_scatter(0, dx)
    for n in BIG:
        grads[n] = jnp.stack(grads[n], axis=0)

    small_shapes = [small[n].shape for n in SMALL]
    packed = _pack([small[n] for n in SMALL])
    got = _exchange([packed], False, "gather_small_grads")[0]
    summed = _unpack(_slot_sum(got, "sum_small_grads"), small_shapes)
    for n, g in zip(SMALL, summed):
        grads[n] = g
    cs = conv_w.shape[-1]
    grads["conv_w"] = lax.dynamic_slice_in_dim(grads["conv_w"], me * cs, cs, axis=2)

    delta, new_m, new_v = {}, {}, {}
    for n in BIG:
        shp = params[n].shape
        flat = lambda a: a.reshape(-1, shp[-1])
        dl, nm, nv = _adamw(flat(params[n]), flat(grads[n]), flat(mom_m[n]), flat(mom_v[n]), f"adamw_{n}")
        delta[n], new_m[n], new_v[n] = dl.reshape(shp), nm.reshape(shp), nv.reshape(shp)
    shapes = [params[n].shape for n in SMALL]
    res = _adamw(_pack([params[n] for n in SMALL]), _pack([grads[n] for n in SMALL]),
                 _pack([mom_m[n] for n in SMALL]), _pack([mom_v[n] for n in SMALL]), "adamw_small")
    for dst, packed_res in zip((delta, new_m, new_v), res):
        for n, a in zip(SMALL, _unpack(packed_res, shapes)):
            dst[n] = a

    return (loss, dx[None], *[grads[n] for n in ORDER], *[delta[n] for n in ORDER],
            *[new_m[n] for n in ORDER], *[new_v[n] for n in ORDER])
```

```python
import functools
import math

import numpy as np
import jax
import jax.numpy as jnp
from jax import lax
from jax.experimental import pallas as pl
from jax.experimental.pallas import tpu as pltpu

F32 = jnp.float32
BF16 = jnp.bfloat16

N_DEV = 8
DEPTH = 4
HGRN_HEAD_DIM = 128
HGRN_CHUNK = 64
ATTN_HEAD_DIM = 64
ATTN_KV_HEADS = 4
KV_WIDTH = ATTN_KV_HEADS * ATTN_HEAD_DIM
WINDOW = 128
WINDOW_SHIFT = 7
N_BUCKETS = 32
MAX_DISTANCE = 128
ALPHA = (2.0 * DEPTH) ** 0.25
LN_EPS = 1e-5
RMS_EPS = 1e-6
ADAM_LR = 0.001
ADAM_B1 = 0.9
ADAM_B2 = 0.999
ADAM_EPS = 1e-08
ADAM_WD = 0.01
ADAM_STEP = 10

LANES = 128
SUBLANES = 8
VMEM_LIMIT = 56 << 20
NEG_INF = float("-inf")


def _offsets(d_model):
    w = d_model // 2
    sizes = (w, w, w, w, w, KV_WIDTH, KV_WIDTH, w, w, w, w, w, d_model, d_model, d_model)
    names = ("a_q", "a_f", "a_i", "a_g", "b_q", "b_k", "b_v", "b_g", "c_b", "c_c", "c_x", "c_g", "m_a", "m_b", "m_c")
    off, o = {}, 0
    for n, s in zip(names, sizes):
        off[n] = o
        o += s
    return off, o


def _tile(n, pref):
    t = min(pref, n)
    while n % t:
        t //= 2
    return t


def _cp(sem=None, vmem=VMEM_LIMIT):
    return pltpu.CompilerParams(dimension_semantics=sem, vmem_limit_bytes=vmem)


def _sigmoid(x):
    return 1.0 / (1.0 + jnp.exp(-x))


def _dot_nn(a, b):
    return jnp.dot(a, b, preferred_element_type=F32)


def _dot_nt(a, b):
    return lax.dot_general(a, b, (((1,), (1,)), ((), ())), preferred_element_type=F32)


def _dot_tn(a, b):
    return lax.dot_general(a, b, (((0,), (0,)), ((), ())), preferred_element_type=F32)


def _dep_specs(dep):
    return ([], []) if dep is None else ([pl.BlockSpec(memory_space=pl.ANY)], [dep])


def _mm_nn(a, b, name, out_dtype=F32, tm=1024, tn=1536, dep=None):
    m, k = a.shape
    _, n = b.shape
    tm, tn = _tile(m, tm), _tile(n, tn)
    dep_specs, dep_args = _dep_specs(dep)

    def body(a_ref, b_ref, *rest):
        o_ref = rest[-1]
        o_ref[...] = _dot_nn(a_ref[...], b_ref[...]).astype(o_ref.dtype)

    return pl.pallas_call(
        body, grid=(n // tn, m // tm),
        in_specs=[pl.BlockSpec((tm, k), lambda j, i: (i, 0)), pl.BlockSpec((k, tn), lambda j, i: (0, j))] + dep_specs,
        out_specs=pl.BlockSpec((tm, tn), lambda j, i: (i, j)),
        out_shape=jax.ShapeDtypeStruct((m, n), out_dtype), name=name,
        compiler_params=_cp(("parallel", "parallel")))(a, b, *dep_args)


def _mm_nt(a, b, name, tm=1024, tk=1536, add=None, add_scale=1.0, dep=None):
    m, k = a.shape
    n, _ = b.shape
    tm, tk = _tile(m, tm), _tile(k, tk)
    has_add = add is not None
    dep_specs, dep_args = _dep_specs(dep)

    def body(*refs):
        if has_add:
            a_ref, b_ref, add_ref = refs[:3]
        else:
            a_ref, b_ref = refs[:2]
        o_ref = refs[-1]

        @pl.when(pl.program_id(1) == 0)
        def _():
            if has_add:
                o_ref[...] = add_ref[...] * add_scale
            else:
                o_ref[...] = jnp.zeros_like(o_ref)

        o_ref[...] += _dot_nt(a_ref[...], b_ref[...])

    in_specs = [pl.BlockSpec((tm, tk), lambda i, kk: (i, kk)), pl.BlockSpec((n, tk), lambda i, kk: (0, kk))]
    args = [a, b]
    if has_add:
        in_specs.append(pl.BlockSpec((tm, n), lambda i, kk: (i, 0)))
        args.append(add)
    in_specs += dep_specs
    args += dep_args
    return pl.pallas_call(
        body, grid=(m // tm, k // tk), in_specs=in_specs,
        out_specs=pl.BlockSpec((tm, n), lambda i, kk: (i, 0)),
        out_shape=jax.ShapeDtypeStruct((m, n), F32), name=name,
        compiler_params=_cp(("parallel", "arbitrary")))(*args)


def _mm_tn(a, b, name, tt=512, tn=1536):
    t, k = a.shape
    _, n = b.shape
    tt, tn = _tile(t, tt), _tile(n, tn)

    def body(a_ref, b_ref, o_ref):
        @pl.when(pl.program_id(1) == 0)
        def _():
            o_ref[...] = jnp.zeros_like(o_ref)

        o_ref[...] += _dot_tn(a_ref[...], b_ref[...])

    return pl.pallas_call(
        body, grid=(n // tn, t // tt),
        in_specs=[pl.BlockSpec((tt, k), lambda j, s: (s, 0)), pl.BlockSpec((tt, tn), lambda j, s: (s, j))],
        out_specs=pl.BlockSpec((k, tn), lambda j, s: (0, j)),
        out_shape=jax.ShapeDtypeStruct((k, n), F32), name=name,
        compiler_params=_cp(("parallel", "arbitrary")))(a, b)


def _ew(body, name, t, ncol, wb, ins, outs, accs=(), tt=256):
    tt = _tile(t, tt)
    nt = t // tt
    in_specs, args = [], []
    for arr, kind, coff in ins:
        if kind == "tile":
            spec = pl.BlockSpec((tt, wb), lambda j, i, c=coff: (i, c + j))
        elif kind == "prev":
            spec = pl.BlockSpec((tt, wb), lambda j, i, c=coff: (jnp.maximum(i - 1, 0), c + j))
        elif kind == "next":
            spec = pl.BlockSpec((tt, wb), lambda j, i, c=coff: (jnp.minimum(i + 1, nt - 1), c + j))
        else:
            spec = pl.BlockSpec((arr.shape[0], wb), lambda j, i, c=coff: (0, c + j))
        in_specs.append(spec)
        args.append(arr)
    out_specs = [pl.BlockSpec((tt, wb), lambda j, i: (i, j)) for _ in outs]
    out_shape = [jax.ShapeDtypeStruct((t, ncol * wb), d) for d in outs]
    for r in accs:
        out_specs.append(pl.BlockSpec((r, wb), lambda j, i: (0, j)))
        out_shape.append(jax.ShapeDtypeStruct((r, ncol * wb), F32))

    def kern(*refs):
        body(pl.program_id(1), nt, *refs)

    res = pl.pallas_call(
        kern, grid=(ncol, nt), in_specs=in_specs, out_specs=out_specs, out_shape=out_shape, name=name,
        compiler_params=_cp(("parallel", "arbitrary")))(*args)
    return res


def _silu_parts(x):
    s = _sigmoid(x)
    return x * s, s + x * s * (1.0 - s)


def _ln_fwd(x, y, g, b, name):
    t, d = x.shape
    tt = _tile(t, 256)

    def body(x_ref, y_ref, g_ref, b_ref, o_ref, ob_ref, xh_ref, r_ref):
        z = ALPHA * x_ref[...] + y_ref[...]
        mu = jnp.mean(z, axis=1, keepdims=True)
        zc = z - mu
        var = jnp.mean(zc * zc, axis=1, keepdims=True)
        rstd = lax.rsqrt(var + LN_EPS)
        xh = zc * rstd
        o = xh * g_ref[...] + b_ref[...]
        o_ref[...] = o
        ob_ref[...] = o.astype(BF16)
        xh_ref[...] = xh
        r_ref[...] = rstd

    row = pl.BlockSpec((tt, d), lambda i: (i, 0))
    vec = pl.BlockSpec((1, d), lambda i: (0, 0))
    return pl.pallas_call(
        body, grid=(t // tt,), in_specs=[row, row, vec, vec],
        out_specs=[row, row, row, pl.BlockSpec((tt, 1), lambda i: (i, 0))],
        out_shape=[jax.ShapeDtypeStruct((t, d), F32), jax.ShapeDtypeStruct((t, d), BF16),
                   jax.ShapeDtypeStruct((t, d), F32), jax.ShapeDtypeStruct((t, 1), F32)],
        name=name, compiler_params=_cp(("parallel",)))(x, y, g, b)


def _ln_bwd(dout, xhat, rstd, g, name):
    t, d = dout.shape
    tt = _tile(t, 256)

    def body(do_ref, xh_ref, r_ref, g_ref, dz_ref, dzb_ref, acc_ref):
        @pl.when(pl.program_id(0) == 0)
        def _():
            acc_ref[...] = jnp.zeros_like(acc_ref)

        do = do_ref[...]
        xh = xh_ref[...]
        dxh = do * g_ref[...]
        m1 = jnp.mean(dxh, axis=1, keepdims=True)
        m2 = jnp.mean(dxh * xh, axis=1, keepdims=True)
        dz = r_ref[...] * (dxh - m1 - xh * m2)
        dz_ref[...] = dz
        dzb_ref[...] = dz.astype(BF16)
        acc_ref[0:1, :] += jnp.sum(do * xh, axis=0, keepdims=True)
        acc_ref[1:2, :] += jnp.sum(do, axis=0, keepdims=True)

    row = pl.BlockSpec((tt, d), lambda i: (i, 0))
    return pl.pallas_call(
        body, grid=(t // tt,),
        in_specs=[row, row, pl.BlockSpec((tt, 1), lambda i: (i, 0)), pl.BlockSpec((1, d), lambda i: (0, 0))],
        out_specs=[row, row, pl.BlockSpec((SUBLANES, d), lambda i: (0, 0))],
        out_shape=[jax.ShapeDtypeStruct((t, d), F32), jax.ShapeDtypeStruct((t, d), BF16),
                   jax.ShapeDtypeStruct((SUBLANES, d), F32)],
        name=name, compiler_params=_cp(("arbitrary",)))(dout, xhat, rstd, g)


def _loss_head(y, target):
    t, d = y.shape
    tt = _tile(t, 256)

    def body(y_ref, t_ref, acc_ref, dy_ref):
        @pl.when(pl.program_id(0) == 0)
        def _():
            acc_ref[...] = jnp.zeros_like(acc_ref)

        err = y_ref[...] - t_ref[...]
        dy_ref[...] = err * (1.0 / d)
        acc_ref[0:1, :] += jnp.sum(err * err, axis=0, keepdims=True)

    row = pl.BlockSpec((tt, d), lambda i: (i, 0))
    acc, dy = pl.pallas_call(
        body, grid=(t // tt,), in_specs=[row, row],
        out_specs=[pl.BlockSpec((SUBLANES, d), lambda i: (0, 0)), row],
        out_shape=[jax.ShapeDtypeStruct((SUBLANES, d), F32), jax.ShapeDtypeStruct((t, d), F32)],
        name="loss_head", compiler_params=_cp(("arbitrary",)))(y, target)
    return acc, dy


def _tri(lower):
    r = lax.broadcasted_iota(jnp.int32, (HGRN_CHUNK, HGRN_CHUNK), 0)
    c = lax.broadcasted_iota(jnp.int32, (HGRN_CHUNK, HGRN_CHUNK), 1)
    return jnp.where((r >= c) if lower else (r <= c), 1.0, 0.0).astype(BF16)


def _exact_tri_matmul(tri, x):
    hi = x.astype(BF16)
    r1 = x - hi.astype(F32)
    mid = r1.astype(BF16)
    lo = (r1 - mid.astype(F32)).astype(BF16)
    return _dot_nn(tri, hi) + _dot_nn(tri, mid) + _dot_nn(tri, lo)


def _hgrn_gates(q_raw, fl, lb):
    sq = _sigmoid(q_raw)
    qf = q_raw * sq * (HGRN_HEAD_DIM ** -0.5)
    sg = _sigmoid(fl)
    f = lb + (1.0 - lb) * sg
    return qf, sq, sg, f


HGRN_SUB = 16
HGRN_NSUB = HGRN_CHUNK // HGRN_SUB
HGRN_HEADS_PER_STEP = 4


def _diag_rows(r):
    return (r // SUBLANES) * SUBLANES


def _heads(x):
    hd = HGRN_HEAD_DIM
    return [x[:, i * hd:(i + 1) * hd] for i in range(x.shape[1] // hd)]


def _per_head(fn, *xs):
    split = [x if isinstance(x, (list, tuple)) else _heads(x) for x in xs]
    return jnp.concatenate([fn(*hs) for hs in zip(*split)], axis=1)


def _head_lane_sum(x):
    return _per_head(lambda h: jnp.broadcast_to(jnp.sum(h, axis=1, keepdims=True), h.shape), x)


def _hgrn_intra_fwd(qf, k, v, b):
    ch, sub, wd = HGRN_CHUNK, HGRN_SUB, qf.shape[1]
    tl = lax.broadcasted_iota(jnp.int32, (sub, wd), 0)
    blocks = []
    for m in range(HGRN_NSUB):
        rs = slice(m * sub, (m + 1) * sub)
        bm, qm, km, vm = b[rs], qf[rs], k[rs], v[rs]
        parts = {0: jnp.zeros((sub, wd), F32), SUBLANES: jnp.zeros((sub - SUBLANES, wd), F32)}
        for r in range(sub):
            lo = _diag_rows(r)
            e = jnp.exp(jnp.where(tl[lo:] >= r, bm[lo:] - bm[r:r + 1], NEG_INF))
            parts[lo] = parts[lo] + _head_lane_sum(qm[lo:] * e * km[r:r + 1]) * vm[r:r + 1]
        blocks.append(parts[0] + jnp.concatenate([jnp.zeros((SUBLANES, wd), F32), parts[SUBLANES]], axis=0))
    acc = jnp.concatenate(blocks, axis=0)
    for j in range(HGRN_NSUB - 1):
        lo = sub * (j + 1)
        c = b[lo - 1:lo, :]
        qj = (qf[lo:] * jnp.exp(b[lo:] - c)).astype(BF16)
        kj = (k[lo - sub:lo] * jnp.exp(c - b[lo - sub:lo])).astype(BF16)
        vj = v[lo - sub:lo].astype(BF16)
        contrib = _per_head(lambda q_, k_, v_: _dot_nn(_dot_nt(q_, k_).astype(BF16), v_), qj, kj, vj)
        acc = acc + jnp.concatenate([jnp.zeros((lo, wd), F32), contrib], axis=0)
    return acc


def _hgrn_intra_bwd(qf, k, v, b, do_v):
    ch, sub, wd = HGRN_CHUNK, HGRN_SUB, qf.shape[1]
    tl = lax.broadcasted_iota(jnp.int32, (sub, wd), 0)
    dq_blocks, dk_blocks, dv_blocks = [], [], []
    for m in range(HGRN_NSUB):
        rs = slice(m * sub, (m + 1) * sub)
        bm, qm, km, vm, dom = b[rs], qf[rs], k[rs], v[rs], do_v[rs]
        parts = {0: jnp.zeros((sub, wd), F32), SUBLANES: jnp.zeros((sub - SUBLANES, wd), F32)}
        dkm = jnp.zeros((sub, wd), F32)
        dvm = jnp.zeros((sub, wd), F32)
        for r in range(sub):
            lo = _diag_rows(r)
            kr = km[r:r + 1]
            e = jnp.exp(jnp.where(tl[lo:] >= r, bm[lo:] - bm[r:r + 1], NEG_INF))
            qe = qm[lo:] * e
            pcol = _head_lane_sum(qe * kr)
            dpcol = _head_lane_sum(dom[lo:] * vm[r:r + 1])
            parts[lo] = parts[lo] + dpcol * (kr * e)
            dkm = jnp.where(tl == r, jnp.sum(dpcol * qe, axis=0, keepdims=True), dkm)
            dvm = jnp.where(tl == r, jnp.sum(pcol * dom[lo:], axis=0, keepdims=True), dvm)
        dq_blocks.append(parts[0] + jnp.concatenate([jnp.zeros((SUBLANES, wd), F32), parts[SUBLANES]], axis=0))
        dk_blocks.append(dkm)
        dv_blocks.append(dvm)
    dq = jnp.concatenate(dq_blocks, axis=0)
    dk = jnp.concatenate(dk_blocks, axis=0)
    dv = jnp.concatenate(dv_blocks, axis=0)
    do_b, v_b = do_v.astype(BF16), v.astype(BF16)
    dk_off, dv_off = [], []
    for j in range(HGRN_NSUB - 1):
        lo = sub * (j + 1)
        c = b[lo - 1:lo, :]
        eq = jnp.exp(b[lo:] - c)
        ek = jnp.exp(c - b[lo - sub:lo])
        qj = (qf[lo:] * eq).astype(BF16)
        kj = (k[lo - sub:lo] * ek).astype(BF16)
        doj, vj = do_b[lo:], v_b[lo - sub:lo]
        dq_j = _per_head(lambda do_, v_, k_: _dot_nn(_dot_nt(do_, v_).astype(BF16), k_), doj, vj, kj)
        dk_j = _per_head(lambda do_, v_, q_: _dot_nn(_dot_nt(v_, do_).astype(BF16), q_), doj, vj, qj)
        dv_j = _per_head(lambda do_, k_, q_: _dot_nn(_dot_nt(k_, q_).astype(BF16), do_), doj, kj, qj)
        dq = dq + jnp.concatenate([jnp.zeros((lo, wd), F32), dq_j * eq], axis=0)
        dk_off.append(dk_j * ek)
        dv_off.append(dv_j)
    zero = jnp.zeros((sub, wd), F32)
    dk = dk + jnp.concatenate(dk_off + [zero], axis=0)
    dv = dv + jnp.concatenate(dv_off + [zero], axis=0)
    return dq, dk, dv


def _hgrn_fwd(u, lb, off, name):
    t = u.shape[0]
    w = lb.shape[1]
    hd, ch, hp = HGRN_HEAD_DIM, HGRN_CHUNK, HGRN_HEADS_PER_STEP
    nh, nc = w // hd, t // ch
    wb = hp * hd
    cq, cf, cv = off["a_q"] // wb, off["a_f"] // wb, off["a_i"] // wb

    def body(q_ref, f_ref, v_ref, lb_ref, o_ref, st_ref, state):
        @pl.when(pl.program_id(1) == 0)
        def _():
            state[...] = jnp.zeros_like(state)

        sts = [state[i] for i in range(hp)]
        qf, _, _, f = _hgrn_gates(q_ref[...], f_ref[...], lb_ref[...])
        k = 1.0 - f
        v = v_ref[...]
        b = _exact_tri_matmul(_tri(True), jnp.log(f))
        inter = _per_head(lambda qa_, st_: _dot_nt(qa_, st_.astype(BF16)), (qf * jnp.exp(b)).astype(BF16), sts)
        o_ref[...] = inter + _hgrn_intra_fwd(qf, k, v, b)
        b_end = b[ch - 1:ch, :]
        a_end = _heads(jnp.exp(b_end))
        kd = _heads((k * jnp.exp(b_end - b)).astype(BF16))
        v_b = _heads(v.astype(BF16))
        for i in range(hp):
            st_ref[i, 0] = sts[i]
            state[i] = sts[i] * a_end[i] + _dot_tn(v_b[i], kd[i])

    return pl.pallas_call(
        body, grid=(nh // hp, nc),
        in_specs=[pl.BlockSpec((ch, wb), lambda h, n: (n, cq + h)),
                  pl.BlockSpec((ch, wb), lambda h, n: (n, cf + h)),
                  pl.BlockSpec((ch, wb), lambda h, n: (n, cv + h)),
                  pl.BlockSpec((1, wb), lambda h, n: (0, h))],
        out_specs=[pl.BlockSpec((ch, wb), lambda h, n: (n, h)),
                   pl.BlockSpec((hp, 1, hd, hd), lambda h, n: (h, n, 0, 0))],
        out_shape=[jax.ShapeDtypeStruct((t, w), F32), jax.ShapeDtypeStruct((nh, nc, hd, hd), F32)],
        scratch_shapes=[pltpu.VMEM((hp, hd, hd), F32)],
        name=name, compiler_params=_cp(("parallel", "arbitrary")))(u, u, u, lb)


def _hgrn_bwd(u, lb, states, do, off, name):
    t = u.shape[0]
    w = lb.shape[1]
    hd, ch, hp = HGRN_HEAD_DIM, HGRN_CHUNK, HGRN_HEADS_PER_STEP
    nh, nc = w // hd, t // ch
    wb = hp * hd
    cq, cf, cv = off["a_q"] // wb, off["a_f"] // wb, off["a_i"] // wb

    def body(q_ref, f_ref, v_ref, do_ref, st_ref, lb_ref, dq_ref, df_ref, dv_ref, dlb_ref, dstate):
        @pl.when(pl.program_id(1) == 0)
        def _():
            dstate[...] = jnp.zeros_like(dstate)
            dlb_ref[...] = jnp.zeros_like(dlb_ref)

        rows = lax.broadcasted_iota(jnp.int32, (ch, wb), 0)
        lb_row = lb_ref[...]
        q_raw = q_ref[...]
        qf, sq, sg, f = _hgrn_gates(q_raw, f_ref[...], lb_row)
        k = 1.0 - f
        b = _exact_tri_matmul(_tri(True), jnp.log(f))
        a = jnp.exp(b)
        b_end = b[ch - 1:ch, :]
        a_end = jnp.exp(b_end)
        to_end = jnp.exp(b_end - b)
        do_v = do_ref[...]
        v = v_ref[...]
        st0 = [st_ref[i, 0] for i in range(hp)]
        ds = [dstate[i] for i in range(hp)]
        st0_b = [s_.astype(BF16) for s_ in st0]
        ds_b = [s_.astype(BF16) for s_ in ds]
        do_b, v_b, kd_b, qa_b = do_v.astype(BF16), v.astype(BF16), (k * to_end).astype(BF16), (qf * a).astype(BF16)

        dq_inter = a * _per_head(_dot_nn, do_b, st0_b)
        dk_end = to_end * _per_head(_dot_nn, v_b, ds_b)
        dv_end = _per_head(_dot_nt, kd_b, ds_b)
        a_end_h = _heads(a_end)
        st_end = [st0[i] * a_end_h[i] + _dot_tn(_heads(v_b)[i], _heads(kd_b)[i]) for i in range(hp)]
        db_end = jnp.concatenate([jnp.sum(ds[i] * st_end[i], axis=0, keepdims=True) for i in range(hp)], axis=1)
        ds_new = [ds[i] * a_end_h[i] + _dot_tn(_heads(do_b)[i], _heads(qa_b)[i]) for i in range(hp)]

        dq_intra, dk_intra, dv_intra = _hgrn_intra_bwd(qf, k, v, b, do_v)
        dqf = dq_inter + dq_intra
        dk = dk_end + dk_intra
        dv = dv_end + dv_intra
        db = qf * dqf - k * dk
        db = db + jnp.where(rows == ch - 1, db_end, 0.0)
        dg = _exact_tri_matmul(_tri(False), db)
        df = dg / f - dk
        for i in range(hp):
            dstate[i] = ds_new[i]
        dq_ref[...] = (dqf * (HGRN_HEAD_DIM ** -0.5) * (sq + q_raw * sq * (1.0 - sq))).astype(BF16)
        df_ref[...] = (df * (1.0 - lb_row) * sg * (1.0 - sg)).astype(BF16)
        dv_ref[...] = dv.astype(BF16)
        dlb_ref[0:1, :] += jnp.sum(df * (1.0 - sg), axis=0, keepdims=True)

    rev = lambda n: nc - 1 - n
    tile = lambda c: pl.BlockSpec((ch, wb), lambda h, n, c=c: (rev(n), c + h))
    return pl.pallas_call(
        body, grid=(nh // hp, nc),
        in_specs=[tile(cq), tile(cf), tile(cv), tile(0),
                  pl.BlockSpec((hp, 1, hd, hd), lambda h, n: (h, rev(n), 0, 0)),
                  pl.BlockSpec((1, wb), lambda h, n: (0, h))],
        out_specs=[tile(0), tile(0), tile(0), pl.BlockSpec((SUBLANES, wb), lambda h, n: (0, h))],
        out_shape=[jax.ShapeDtypeStruct((t, w), BF16)] * 3 + [jax.ShapeDtypeStruct((SUBLANES, w), F32)],
        scratch_shapes=[pltpu.VMEM((hp, hd, hd), F32)],
        name=name, compiler_params=_cp(("parallel", "arbitrary")))(u, u, u, do, states, lb)


def _gate_a_fwd(o, u, gain, off, name):
    t, w = o.shape
    hd = HGRN_HEAD_DIM

    def body(i, nt, o_ref, g_ref, gain_ref, p_ref):
        silu, _ = _silu_parts(g_ref[...])
        for h in range(w // hd):
            sl = slice(h * hd, (h + 1) * hd)
            oh = o_ref[:, sl]
            r = lax.rsqrt(jnp.mean(oh * oh, axis=1, keepdims=True) + RMS_EPS)
            p_ref[:, sl] = (oh * r * gain_ref[:, sl] * silu[:, sl]).astype(BF16)

    return _ew(body, name, t, 1, w, [(o, "tile", 0), (u, "tile", off["a_g"] // w), (gain, "row", 0)], [BF16])[0]


def _gate_a_bwd(dp, o, u, gain, off, name):
    t, w = o.shape
    hd = HGRN_HEAD_DIM

    def body(i, nt, dp_ref, o_ref, g_ref, gain_ref, do_ref, dg_ref, acc_ref):
        @pl.when(i == 0)
        def _():
            acc_ref[...] = jnp.zeros_like(acc_ref)

        silu, dsilu = _silu_parts(g_ref[...])
        dp_v = dp_ref[...]
        for h in range(w // hd):
            sl = slice(h * hd, (h + 1) * hd)
            oh = o_ref[:, sl]
            r = lax.rsqrt(jnp.mean(oh * oh, axis=1, keepdims=True) + RMS_EPS)
            nrm = oh * r
            gn = gain_ref[:, sl]
            dph = dp_v[:, sl]
            dg_ref[:, sl] = (dph * nrm * gn * dsilu[:, sl]).astype(BF16)
            acc_ref[0:1, sl] += jnp.sum(dph * nrm * silu[:, sl], axis=0, keepdims=True)
            dn = dph * gn * silu[:, sl]
            do_ref[:, sl] = r * (dn - nrm * jnp.mean(dn * nrm, axis=1, keepdims=True))

    return _ew(body, name, t, 1, w,
               [(dp, "tile", 0), (o, "tile", 0), (u, "tile", off["a_g"] // w), (gain, "row", 0)],
               [F32, BF16], accs=[SUBLANES])


def _bucket_map():
    i = np.arange(WINDOW)[:, None]
    j = np.arange(2 * WINDOW)[None, :]
    dist = np.clip(WINDOW + i - j, 0, WINDOW - 1)
    max_exact = N_BUCKETS // 2
    logd = (np.log(np.maximum(dist, 1).astype(np.float32) / max_exact) / math.log(MAX_DISTANCE / max_exact))
    large = np.minimum(max_exact + (logd.astype(np.float32) * (N_BUCKETS - max_exact)).astype(np.int32), N_BUCKETS - 1)
    return np.where(dist < max_exact, dist, large).astype(np.int32)


def _bias_table(rel_bias, n_heads):
    bucket = jnp.asarray(_bucket_map())

    def body(rb_ref, bk_ref, o_ref):
        bk = bk_ref[...]
        for h in range(n_heads):
            def step(bi, acc):
                return jnp.where(bk == bi, rb_ref[bi, h], acc)
            o_ref[h] = lax.fori_loop(0, N_BUCKETS, step, jnp.zeros((WINDOW, 2 * WINDOW), F32))

    return pl.pallas_call(
        body, in_specs=[pl.BlockSpec(memory_space=pltpu.SMEM), pl.BlockSpec(memory_space=pltpu.VMEM)],
        out_specs=pl.BlockSpec(memory_space=pltpu.VMEM),
        out_shape=jax.ShapeDtypeStruct((n_heads, WINDOW, 2 * WINDOW), F32), name="bias_table",
        compiler_params=_cp())(rel_bias, bucket)


def _bias_grad(dbias, n_heads):
    bucket = jnp.asarray(_bucket_map())

    def body(db_ref, bk_ref, o_ref):
        bk = bk_ref[...]
        lane = lax.broadcasted_iota(jnp.int32, (1, LANES), 1)

        def step(bi, carry):
            row = jnp.zeros((1, LANES), F32)
            for h in range(n_heads):
                val = jnp.sum(jnp.where(bk == bi, db_ref[h], 0.0))
                row = jnp.where(lane == h, val, row)
            o_ref[pl.ds(bi, 1), :] = row
            return carry

        lax.fori_loop(0, N_BUCKETS, step, 0)

    return pl.pallas_call(
        body, in_specs=[pl.BlockSpec(memory_space=pltpu.VMEM), pl.BlockSpec(memory_space=pltpu.VMEM)],
        out_specs=pl.BlockSpec(memory_space=pltpu.VMEM),
        out_shape=jax.ShapeDtypeStruct((N_BUCKETS, LANES), F32), name="bias_grad",
        compiler_params=_cp())(dbias, bucket)


def _attn_probs(n, q_ref, kp_ref, kc_ref, bias_ref, sink_ref, hh, grp):
    ad, wn = ATTN_HEAD_DIM, WINDOW
    ksl = slice(hh * ad, (hh + 1) * ad)
    kw = jnp.concatenate([kp_ref[:, ksl], kc_ref[:, ksl]], axis=0).astype(BF16)
    qs = jnp.concatenate([q_ref[:, (hh * grp + g) * ad:(hh * grp + g + 1) * ad] for g in range(grp)], axis=0).astype(BF16)
    s = _dot_nt(qs, kw) * (ad ** -0.5) + bias_ref[hh]
    r = lax.broadcasted_iota(jnp.int32, (grp * wn, 2 * wn), 0)
    j = lax.broadcasted_iota(jnp.int32, (grp * wn, 2 * wn), 1)
    i = r & (wn - 1)
    valid = ((j >= wn) & (j - wn <= i)) | ((j < wn) & (j > i) & (n > 0))
    s = jnp.where(valid, s, NEG_INF)
    rr = lax.broadcasted_iota(jnp.int32, (grp * wn, 1), 0) >> WINDOW_SHIFT
    sink = jnp.zeros((grp * wn, 1), F32)
    for g in range(grp):
        sink = jnp.where(rr == g, sink_ref[hh * grp + g], sink)
    m = jnp.maximum(jnp.max(s, axis=1, keepdims=True), sink)
    p = jnp.exp(s - m)
    es = jnp.exp(sink - m)
    inv = 1.0 / (jnp.sum(p, axis=1, keepdims=True) + es)
    return qs, kw, p * inv, es * inv


def _attn_fwd(u, bias_g, sinks, off, w, name):
    t = u.shape[0]
    wn, ad, kvw = WINDOW, ATTN_HEAD_DIM, KV_WIDTH
    grp = (w // ad) // ATTN_KV_HEADS
    nb = t // wn
    cq, ck, cv = off["b_q"] // w, off["b_k"] // kvw, off["b_v"] // kvw

    def body(q_ref, kp_ref, kc_ref, vp_ref, vc_ref, bias_ref, sink_ref, o_ref):
        n = pl.program_id(0)
        for hh in range(ATTN_KV_HEADS):
            _, _, p, _ = _attn_probs(n, q_ref, kp_ref, kc_ref, bias_ref, sink_ref, hh, grp)
            ksl = slice(hh * ad, (hh + 1) * ad)
            vw = jnp.concatenate([vp_ref[:, ksl], vc_ref[:, ksl]], axis=0).astype(BF16)
            o = _dot_nn(p.astype(BF16), vw)
            for g in range(grp):
                o_ref[:, (hh * grp + g) * ad:(hh * grp + g + 1) * ad] = o[g * wn:(g + 1) * wn]

    prev = lambda n: jnp.maximum(n - 1, 0)
    return pl.pallas_call(
        body, grid=(nb,),
        in_specs=[pl.BlockSpec((wn, w), lambda n: (n, cq)),
                  pl.BlockSpec((wn, kvw), lambda n: (prev(n), ck)), pl.BlockSpec((wn, kvw), lambda n: (n, ck)),
                  pl.BlockSpec((wn, kvw), lambda n: (prev(n), cv)), pl.BlockSpec((wn, kvw), lambda n: (n, cv)),
                  pl.BlockSpec((ATTN_KV_HEADS, grp * wn, 2 * wn), lambda n: (0, 0, 0)),
                  pl.BlockSpec(memory_space=pltpu.SMEM)],
        out_specs=pl.BlockSpec((wn, w), lambda n: (n, 0)),
        out_shape=jax.ShapeDtypeStruct((t, w), F32), name=name,
        compiler_params=_cp(("parallel",)))(u, u, u, u, u, bias_g, sinks)


def _attn_bwd(u, o, do, bias_g, sinks, off, w, name):
    t = u.shape[0]
    wn, ad, kvw = WINDOW, ATTN_HEAD_DIM, KV_WIDTH
    grp = (w // ad) // ATTN_KV_HEADS
    nb = t // wn
    cq, ck, cv = off["b_q"] // w, off["b_k"] // kvw, off["b_v"] // kvw

    def body(q_ref, kp_ref, kc_ref, vp_ref, vc_ref, o_ref, do_ref, bias_ref, sink_ref,
             dq_ref, dkc_ref, dkp_ref, dvc_ref, dvp_ref, dbias_ref, dsink_ref):
        n = pl.program_id(0)

        @pl.when(n == 0)
        def _():
            dbias_ref[...] = jnp.zeros_like(dbias_ref)
            dsink_ref[...] = jnp.zeros_like(dsink_ref)

        lane = lax.broadcasted_iota(jnp.int32, (1, LANES), 1)
        rr = lax.broadcasted_iota(jnp.int32, (grp * wn, 1), 0) >> WINDOW_SHIFT
        dsink_row = jnp.zeros((1, LANES), F32)
        for hh in range(ATTN_KV_HEADS):
            qs, kw, p, psink = _attn_probs(n, q_ref, kp_ref, kc_ref, bias_ref, sink_ref, hh, grp)
            ksl = slice(hh * ad, (hh + 1) * ad)
            vw = jnp.concatenate([vp_ref[:, ksl], vc_ref[:, ksl]], axis=0).astype(BF16)
            hs = [slice((hh * grp + g) * ad, (hh * grp + g + 1) * ad) for g in range(grp)]
            dos = jnp.concatenate([do_ref[:, sl] for sl in hs], axis=0)
            os_ = jnp.concatenate([o_ref[:, sl] for sl in hs], axis=0)
            delta = jnp.sum(dos * os_, axis=1, keepdims=True)
            dos_b = dos.astype(BF16)
            dp = _dot_nt(dos_b, vw)
            ds = p * (dp - delta)
            dbias_ref[hh] += ds
            sd = psink * delta
            for g in range(grp):
                val = -jnp.sum(jnp.where(rr == g, sd, 0.0))
                dsink_row = jnp.where(lane == hh * grp + g, val, dsink_row)
            ds_b = (ds * (ad ** -0.5)).astype(BF16)
            dq = _dot_nn(ds_b, kw)
            for g in range(grp):
                dq_ref[:, hs[g]] = dq[g * wn:(g + 1) * wn].astype(BF16)
            dkw = _dot_tn(ds_b, qs)
            dvw = _dot_tn(p.astype(BF16), dos_b)
            dkp_ref[:, ksl] = dkw[:wn]
            dkc_ref[:, ksl] = dkw[wn:]
            dvp_ref[:, ksl] = dvw[:wn]
            dvc_ref[:, ksl] = dvw[wn:]
        dsink_ref[0:1, :] += dsink_row

    prev = lambda n: jnp.maximum(n - 1, 0)
    kv_out = pl.BlockSpec((wn, kvw), lambda n: (n, 0))
    return pl.pallas_call(
        body, grid=(nb,),
        in_specs=[pl.BlockSpec((wn, w), lambda n: (n, cq)),
                  pl.BlockSpec((wn, kvw), lambda n: (prev(n), ck)), pl.BlockSpec((wn, kvw), lambda n: (n, ck)),
                  pl.BlockSpec((wn, kvw), lambda n: (prev(n), cv)), pl.BlockSpec((wn, kvw), lambda n: (n, cv)),
                  pl.BlockSpec((wn, w), lambda n: (n, 0)), pl.BlockSpec((wn, w), lambda n: (n, 0)),
                  pl.BlockSpec((ATTN_KV_HEADS, grp * wn, 2 * wn), lambda n: (0, 0, 0)),
                  pl.BlockSpec(memory_space=pltpu.SMEM)],
        out_specs=[pl.BlockSpec((wn, w), lambda n: (n, 0)), kv_out, kv_out, kv_out, kv_out,
                   pl.BlockSpec((ATTN_KV_HEADS, grp * wn, 2 * wn), lambda n: (0, 0, 0)),
                   pl.BlockSpec((SUBLANES, LANES), lambda n: (0, 0))],
        out_shape=[jax.ShapeDtypeStruct((t, w), BF16)] + [jax.ShapeDtypeStruct((t, kvw), F32)] * 4
        + [jax.ShapeDtypeStruct((ATTN_KV_HEADS, grp * wn, 2 * wn), F32), jax.ShapeDtypeStruct((SUBLANES, LANES), F32)],
        name=name, compiler_params=_cp(("arbitrary",)))(u, u, u, u, u, o, do, bias_g, sinks)


def _kv_combine(cur, prv, name):
    t, kvw = cur.shape

    def body(i, nt, c_ref, p_ref, o_ref):
        nxt = jnp.where(i < nt - 1, p_ref[...], 0.0)
        o_ref[...] = (c_ref[...] + nxt).astype(BF16)

    return _ew(body, name, t, 1, kvw, [(cur, "tile", 0), (prv, "next", 0)], [BF16], tt=WINDOW)[0]


def _gate_b_fwd(o, u, off, name):
    t, w = o.shape
    wb = 512

    def body(i, nt, o_ref, g_ref, p_ref):
        silu, _ = _silu_parts(g_ref[...])
        p_ref[...] = (o_ref[...] * silu).astype(BF16)

    return _ew(body, name, t, w // wb, wb, [(o, "tile", 0), (u, "tile", off["b_g"] // wb)], [BF16])[0]


def _gate_b_bwd(dp, o, u, off, name):
    t, w = o.shape
    wb = 512

    def body(i, nt, dp_ref, o_ref, g_ref, do_ref, dg_ref):
        silu, dsilu = _silu_parts(g_ref[...])
        dp_v = dp_ref[...]
        do_ref[...] = dp_v * silu
        dg_ref[...] = (dp_v * o_ref[...] * dsilu).astype(BF16)

    return _ew(body, name, t, w // wb, wb,
               [(dp, "tile", 0), (o, "tile", 0), (u, "tile", off["b_g"] // wb)], [F32, BF16])


def _shift_down(h, tail, k, rows):
    tt = h.shape[0]
    out = pltpu.roll(h, k, 0)
    for r in range(k):
        out = jnp.where(rows == r, tail[tt - k + r:tt - k + r + 1, :], out)
    return out


def _shift_up(h, head, k, rows):
    tt = h.shape[0]
    out = pltpu.roll(h, tt - k, 0)
    for r in range(k):
        out = jnp.where(rows == tt - k + r, head[r:r + 1, :], out)
    return out


def _conv_fwd(u, conv_w, off, w, name):
    t = u.shape[0]
    wb = 512
    c = lambda nme: off[nme] // wb

    def body(i, nt, cb_ref, cc_ref, ccp_ref, cx_ref, cxp_ref, cg_ref, w_ref, p_ref):
        h = cc_ref[...] * cx_ref[...]
        hp = jnp.where(i > 0, ccp_ref[...] * cxp_ref[...], 0.0)
        rows = lax.broadcasted_iota(jnp.int32, h.shape, 0)
        y = w_ref[0:1, :] * _shift_down(h, hp, 2, rows) + w_ref[1:2, :] * _shift_down(h, hp, 1, rows) + w_ref[2:3, :] * h
        silu, _ = _silu_parts(cg_ref[...])
        p_ref[...] = (cb_ref[...] * y * silu).astype(BF16)

    return _ew(body, name, t, w // wb, wb,
               [(u, "tile", c("c_b")), (u, "tile", c("c_c")), (u, "prev", c("c_c")), (u, "tile", c("c_x")),
                (u, "prev", c("c_x")), (u, "tile", c("c_g")), (conv_w, "row", 0)], [BF16])[0]


def _conv_bwd(dp, u, conv_w, off, w, name):
    t = u.shape[0]
    wb = 512
    c = lambda nme: off[nme] // wb

    def body(i, nt, dp_ref, dpn_ref, cb_ref, cbn_ref, cg_ref, cgn_ref, cc_ref, ccp_ref, cx_ref, cxp_ref, w_ref,
             dcb_ref, dcc_ref, dcx_ref, dcg_ref, acc_ref):
        @pl.when(i == 0)
        def _():
            acc_ref[...] = jnp.zeros_like(acc_ref)

        cc, cx, cb = cc_ref[...], cx_ref[...], cb_ref[...]
        h = cc * cx
        hp = jnp.where(i > 0, ccp_ref[...] * cxp_ref[...], 0.0)
        rows = lax.broadcasted_iota(jnp.int32, h.shape, 0)
        h1 = _shift_down(h, hp, 1, rows)
        h2 = _shift_down(h, hp, 2, rows)
        w0, w1, w2 = w_ref[0:1, :], w_ref[1:2, :], w_ref[2:3, :]
        y = w0 * h2 + w1 * h1 + w2 * h
        silu, dsilu = _silu_parts(cg_ref[...])
        dp_v = dp_ref[...]
        dcg_ref[...] = (dp_v * cb * y * dsilu).astype(BF16)
        dcb_ref[...] = (dp_v * y * silu).astype(BF16)
        dy = dp_v * cb * silu
        silu_n, _ = _silu_parts(cgn_ref[...])
        dyn = jnp.where(i < nt - 1, dpn_ref[...] * cbn_ref[...] * silu_n, 0.0)
        dh = w2 * dy + w1 * _shift_up(dy, dyn, 1, rows) + w0 * _shift_up(dy, dyn, 2, rows)
        dcc_ref[...] = (dh * cx).astype(BF16)
        dcx_ref[...] = (dh * cc).astype(BF16)
        acc_ref[0:1, :] += jnp.sum(dy * h2, axis=0, keepdims=True)
        acc_ref[1:2, :] += jnp.sum(dy * h1, axis=0, keepdims=True)
        acc_ref[2:3, :] += jnp.sum(dy * h, axis=0, keepdims=True)

    return _ew(body, name, t, w // wb, wb,
               [(dp, "tile", 0), (dp, "next", 0), (u, "tile", c("c_b")), (u, "next", c("c_b")),
                (u, "tile", c("c_g")), (u, "next", c("c_g")), (u, "tile", c("c_c")), (u, "prev", c("c_c")),
                (u, "tile", c("c_x")), (u, "prev", c("c_x")), (conv_w, "row", 0)],
               [BF16] * 4, accs=[SUBLANES])


def _merge_fwd(u, ya, yb, yc, off, d, name):
    t = u.shape[0]
    wb = 512
    c = lambda nme: off[nme] // wb

    def body(i, nt, ma_ref, mb_ref, mc_ref, ya_ref, yb_ref, yc_ref, o_ref):
        o_ref[...] = (_sigmoid(ma_ref[...]) * ya_ref[...] + _sigmoid(mb_ref[...]) * yb_ref[...]
                      + _sigmoid(mc_ref[...]) * yc_ref[...]).astype(BF16)

    return _ew(body, name, t, d // wb, wb,
               [(u, "tile", c("m_a")), (u, "tile", c("m_b")), (u, "tile", c("m_c")),
                (ya, "tile", 0), (yb, "tile", 0), (yc, "tile", 0)], [BF16])[0]


def _merge_bwd(dm, u, ya, yb, yc, off, d, name):
    t = u.shape[0]
    wb = 512
    c = lambda nme: off[nme] // wb

    def body(i, nt, dm_ref, ma_ref, mb_ref, mc_ref, ya_ref, yb_ref, yc_ref, da_ref, db_ref, dc_ref, ga_ref, gb_ref, gc_ref):
        dm_v = dm_ref[...]
        for m_ref, y_ref, dy_ref, dg_ref in ((ma_ref, ya_ref, da_ref, ga_ref), (mb_ref, yb_ref, db_ref, gb_ref),
                                             (mc_ref, yc_ref, dc_ref, gc_ref)):
            s = _sigmoid(m_ref[...])
            dy_ref[...] = (dm_v * s).astype(BF16)
            dg_ref[...] = (dm_v * y_ref[...] * s * (1.0 - s)).astype(BF16)

    return _ew(body, name, t, d // wb, wb,
               [(dm, "tile", 0), (u, "tile", c("m_a")), (u, "tile", c("m_b")), (u, "tile", c("m_c")),
                (ya, "tile", 0), (yb, "tile", 0), (yc, "tile", 0)], [BF16] * 6)


def _lower_bounds(lb_param):
    def body(p_ref, o_ref):
        p = p_ref[...]
        e = jnp.exp(p - jnp.max(p, axis=0, keepdims=True))
        soft = e / jnp.sum(e, axis=0, keepdims=True)
        acc = jnp.zeros_like(soft[0:1])
        o_ref[0:1, :] = acc
        for l in range(1, DEPTH):
            acc = acc + soft[l:l + 1]
            o_ref[l:l + 1, :] = acc

    return pl.pallas_call(body, out_shape=jax.ShapeDtypeStruct(lb_param.shape, F32), name="lower_bounds",
                          compiler_params=_cp())(lb_param)


def _lower_bounds_bwd(lb_param, dlower):
    def body(p_ref, d_ref, o_ref):
        p = p_ref[...]
        e = jnp.exp(p - jnp.max(p, axis=0, keepdims=True))
        soft = e / jnp.sum(e, axis=0, keepdims=True)
        dl = d_ref[...]
        ds = [jnp.zeros_like(dl[0:1])]
        for j in range(1, DEPTH):
            acc = dl[j:j + 1]
            for l in range(j + 1, DEPTH):
                acc = acc + dl[l:l + 1]
            ds.append(acc)
        inner = ds[0] * soft[0:1]
        for j in range(1, DEPTH):
            inner = inner + ds[j] * soft[j:j + 1]
        for j in range(DEPTH):
            o_ref[j:j + 1, :] = soft[j:j + 1] * (ds[j] - inner)

    return pl.pallas_call(body, out_shape=jax.ShapeDtypeStruct(lb_param.shape, F32), name="lower_bounds_bwd",
                          compiler_params=_cp())(lb_param, dlower)


def _exchange(arrays, scatter, name):
    n_arr = len(arrays)

    def body(*refs):
        srcs, dsts = refs[:n_arr], refs[n_arr:2 * n_arr]
        send_sems, recv_sems, local_sems = refs[2 * n_arr:]
        me = 4 * lax.axis_index("x") + 2 * lax.axis_index("y") + lax.axis_index("c")
        copies = _peer_copies(srcs, dsts, send_sems, recv_sems, scatter)
        for a in range(n_arr):
            copies.append(pltpu.make_async_copy(srcs[a].at[me] if scatter else srcs[a], dsts[a].at[me], local_sems.at[a]))
        for cp in copies:
            cp.start()
        for cp in copies:
            cp.wait()

    out_shape = [jax.ShapeDtypeStruct(a.shape if scatter else (N_DEV,) + a.shape, a.dtype) for a in arrays]
    anyspec = pl.BlockSpec(memory_space=pl.ANY)
    res = pl.pallas_call(
        body, in_specs=[anyspec] * n_arr, out_specs=[anyspec] * n_arr, out_shape=out_shape,
        scratch_shapes=[pltpu.SemaphoreType.DMA((n_arr * (N_DEV - 1),)), pltpu.SemaphoreType.DMA((n_arr * (N_DEV - 1),)),
                        pltpu.SemaphoreType.DMA((n_arr,))],
        name=name)(*arrays)
    return list(res)


def _peer_copies(srcs, lands, send_sems, recv_sems, scatter):
    x, y, c = lax.axis_index("x"), lax.axis_index("y"), lax.axis_index("c")
    me = 4 * x + 2 * y + c
    copies = []
    for a in range(len(srcs)):
        for k in range(1, N_DEV):
            px = 1 - x if k & 4 else x
            py = 1 - y if k & 2 else y
            pc = 1 - c if k & 1 else c
            src = srcs[a].at[4 * px + 2 * py + pc] if scatter else srcs[a]
            copies.append(pltpu.make_async_remote_copy(
                src_ref=src, dst_ref=lands[a].at[me],
                send_sem=send_sems.at[a * (N_DEV - 1) + k - 1], recv_sem=recv_sems.at[a * (N_DEV - 1) + k - 1],
                device_id=(px, py, pc), device_id_type=pl.DeviceIdType.MESH))
    return copies


_HBM_SPEC = pl.BlockSpec(memory_space=pltpu.HBM)
_SEM_SPEC = pl.BlockSpec(memory_space=pltpu.SEMAPHORE)
_ANY_SPEC = pl.BlockSpec(memory_space=pl.ANY)
_DATAFLOW = pltpu.SideEffectType.DATAFLOW_SIDE_EFFECTING


def _exchange_start(arrays, scatter, name, dep=None):
    n_arr = len(arrays)
    n_sem = n_arr * (N_DEV - 1)
    me = 4 * lax.axis_index("x") + 2 * lax.axis_index("y") + lax.axis_index("c")
    lands = []
    for a in arrays:
        own = lax.dynamic_index_in_dim(a, me, 0, keepdims=False) if scatter else a
        shape = a.shape if scatter else (N_DEV,) + a.shape
        lands.append(lax.dynamic_update_index_in_dim(lax.empty(shape, a.dtype), own, me, 0))
    dep_specs, dep_args = _dep_specs(dep)

    def body(*refs):
        srcs, lnds = refs[:n_arr], refs[n_arr:2 * n_arr]
        outs = refs[2 * n_arr + len(dep_args):]
        send_sems, recv_sems, token = outs[0], outs[1], outs[2 + 2 * n_arr]
        for cp in _peer_copies(srcs, lnds, send_sems, recv_sems, scatter):
            cp.start()
        token[...] = jnp.zeros_like(token)

    thru = [pltpu.HBM(a.shape, a.dtype) for a in list(arrays) + lands]
    return pl.pallas_call(
        body, name=name,
        out_shape=(pltpu.SemaphoreType.DMA((n_sem,)), pltpu.SemaphoreType.DMA((n_sem,)), *thru,
                   jax.ShapeDtypeStruct((SUBLANES, LANES), F32)),
        in_specs=[_HBM_SPEC] * (2 * n_arr) + dep_specs,
        out_specs=(_SEM_SPEC, _SEM_SPEC, *[_HBM_SPEC] * (2 * n_arr), pl.BlockSpec(memory_space=pltpu.VMEM)),
        input_output_aliases={i: 2 + i for i in range(2 * n_arr)},
        compiler_params=pltpu.CompilerParams(has_side_effects=_DATAFLOW),
    )(*[pltpu.with_memory_space_constraint(a, pltpu.HBM) for a in list(arrays) + lands], *dep_args)


def _exchange_wait(started, scatter, name, after):
    send_sems, recv_sems = started[0], started[1]
    thru = list(started[2:-1])
    n_arr = len(thru) // 2

    def body(*refs):
        srcs, lnds = refs[:n_arr], refs[n_arr:2 * n_arr]
        for cp in _peer_copies(srcs, lnds, refs[2 * n_arr], refs[2 * n_arr + 1], scatter):
            cp.wait_send()
            cp.wait_recv()

    res = pl.pallas_call(
        body, name=name, out_shape=tuple(pltpu.HBM(a.shape, a.dtype) for a in thru),
        in_specs=[_HBM_SPEC] * (2 * n_arr) + [_SEM_SPEC, _SEM_SPEC, _ANY_SPEC],
        out_specs=tuple([_HBM_SPEC] * (2 * n_arr)),
        input_output_aliases={i: i for i in range(2 * n_arr)},
        compiler_params=pltpu.CompilerParams(has_side_effects=_DATAFLOW),
    )(*thru, send_sems, recv_sems, after)
    return list(res[n_arr:])


def _unshard_cols(g, name):
    nd, r, s = g.shape
    tr = _tile(r, 64)

    def body(i_ref, o_ref):
        for p in range(nd):
            o_ref[:, p * s:(p + 1) * s] = i_ref[p]

    return pl.pallas_call(
        body, grid=(r // tr,), in_specs=[pl.BlockSpec((nd, tr, s), lambda i: (0, i, 0))],
        out_specs=pl.BlockSpec((tr, nd * s), lambda i: (i, 0)),
        out_shape=jax.ShapeDtypeStruct((r, nd * s), g.dtype), name=name, compiler_params=_cp(("parallel",)))(g)


def _shard_cols(g, name):
    r, n = g.shape
    s = n // N_DEV
    tr = _tile(r, 64)

    def body(i_ref, o_ref):
        for p in range(N_DEV):
            o_ref[p] = i_ref[:, p * s:(p + 1) * s]

    return pl.pallas_call(
        body, grid=(r // tr,), in_specs=[pl.BlockSpec((tr, n), lambda i: (i, 0))],
        out_specs=pl.BlockSpec((N_DEV, tr, s), lambda i: (0, i, 0)),
        out_shape=jax.ShapeDtypeStruct((N_DEV, r, s), g.dtype), name=name, compiler_params=_cp(("parallel",)))(g)


def _slot_sum(slots, name):
    nd, r, c = slots.shape
    tr = _tile(r, 64)

    def body(s_ref, o_ref):
        acc = s_ref[0]
        for p in range(1, nd):
            acc = acc + s_ref[p]
        o_ref[...] = acc

    return pl.pallas_call(
        body, grid=(r // tr,), in_specs=[pl.BlockSpec((nd, tr, c), lambda i: (0, i, 0))],
        out_specs=pl.BlockSpec((tr, c), lambda i: (i, 0)),
        out_shape=jax.ShapeDtypeStruct((r, c), F32), name=name, compiler_params=_cp(("parallel",)))(slots)


def _adamw(w, g, m, v, name):
    r, c = w.shape
    tr = _tile(r, 256)
    c1 = 1.0 - ADAM_B1 ** ADAM_STEP
    c2 = 1.0 - ADAM_B2 ** ADAM_STEP

    def body(w_ref, g_ref, m_ref, v_ref, d_ref, nm_ref, nv_ref):
        gv = g_ref[...]
        nm = ADAM_B1 * m_ref[...] + (1.0 - ADAM_B1) * gv
        nv = ADAM_B2 * v_ref[...] + (1.0 - ADAM_B2) * (gv * gv)
        nm_ref[...] = nm
        nv_ref[...] = nv
        d_ref[...] = -ADAM_LR * ((nm / c1) / (jnp.sqrt(nv / c2) + ADAM_EPS) + ADAM_WD * w_ref[...])

    spec = pl.BlockSpec((tr, c), lambda i: (i, 0))
    return pl.pallas_call(
        body, grid=(r // tr,), in_specs=[spec] * 4, out_specs=[spec] * 3,
        out_shape=[jax.ShapeDtypeStruct((r, c), F32)] * 3, name=name, compiler_params=_cp(("parallel",)))(w, g, m, v)


def _forward_backward(x, target, weights_hook, grads_hook, lb_param, hgrn_norm_g, attn_sinks, rel_bias, ln_g, ln_b):
    t, d = x.shape
    w = d // 2
    off, n_in = _offsets(d)
    n_heads = w // ATTN_HEAD_DIM
    grp = n_heads // ATTN_KV_HEADS

    lower = _lower_bounds(lb_param)
    bias = _bias_table(rel_bias, n_heads)
    bias_g = bias.reshape(ATTN_KV_HEADS, grp * WINDOW, 2 * WINDOW)

    saved, weights = [], []
    xb = x.astype(BF16)
    for l in range(DEPTH):
        wl, token = weights_hook(l, x)
        weights.append(wl)
        s = {"x": x, "xb": xb}
        u = _mm_nn(xb, wl["w_in"], f"in_proj", dep=token)
        s["u"] = u
        lb_l, gain_l, cw_l = lower[l:l + 1], hgrn_norm_g[l:l + 1], wl["conv_w"]
        o_a, states = _hgrn_fwd(u, lb_l, off, f"hgrn_fwd")
        p_a = _gate_a_fwd(o_a, u, gain_l, off, f"gate_a_fwd")
        o_b = _attn_fwd(u, bias_g, attn_sinks[l], off, w, f"attn_fwd")
        p_b = _gate_b_fwd(o_b, u, off, f"gate_b_fwd")
        p_c = _conv_fwd(u, cw_l, off, w, f"conv_fwd")
        y_a = _mm_nn(p_a, wl["w_proj_hgrn"], f"proj_a", tn=2048)
        y_b = _mm_nn(p_b, wl["w_proj_attn"], f"proj_b", tn=2048)
        y_c = _mm_nn(p_c, wl["w_proj_conv"], f"proj_c", tn=2048)
        merged = _merge_fwd(u, y_a, y_b, y_c, off, d, f"merge_fwd")
        y = _mm_nn(merged, wl["w_out"], f"out_proj", tm=512, tn=2048)
        x, xb, xhat, rstd = _ln_fwd(x, y, ln_g[l:l + 1], ln_b[l:l + 1], f"ln_fwd")
        s.update(o_a=o_a, states=states, p_a=p_a, o_b=o_b, p_b=p_b, p_c=p_c, y_a=y_a, y_b=y_b, y_c=y_c,
                 merged=merged, xhat=xhat, rstd=rstd)
        saved.append(s)

    loss_acc, dx = _loss_head(x, target)

    d_ln, d_lower, d_gain, d_sink, d_conv = [None] * DEPTH, [None] * DEPTH, [None] * DEPTH, [None] * DEPTH, [None] * DEPTH
    dbias_total = None
    for l in reversed(range(DEPTH)):
        wl, s = weights[l], saved[l]
        u = s["u"]
        lb_l, gain_l, cw_l = lower[l:l + 1], hgrn_norm_g[l:l + 1], wl["conv_w"]
        dz, dzb, d_ln[l] = _ln_bwd(dx, s["xhat"], s["rstd"], ln_g[l:l + 1], f"ln_bwd")
        g_out = _mm_tn(s["merged"], dzb, f"g_out", tn=2048)
        dmerged = _mm_nt(dzb, wl["w_out"], f"d_merged", tk=2048)
        dya, dyb, dyc, dma, dmb, dmc = _merge_bwd(dmerged, u, s["y_a"], s["y_b"], s["y_c"], off, d, f"merge_bwd")
        g_pa = _mm_tn(s["p_a"], dya, f"g_proj_a", tn=2048)
        g_pb = _mm_tn(s["p_b"], dyb, f"g_proj_b", tn=2048)
        g_pc = _mm_tn(s["p_c"], dyc, f"g_proj_c", tn=2048)
        dpa = _mm_nt(dya, wl["w_proj_hgrn"], f"d_p_a", tk=2048)
        dpb = _mm_nt(dyb, wl["w_proj_attn"], f"d_p_b", tk=2048)
        dpc = _mm_nt(dyc, wl["w_proj_conv"], f"d_p_c", tk=2048)
        do_a, d_ag, d_gain[l] = _gate_a_bwd(dpa, s["o_a"], u, gain_l, off, f"gate_a_bwd")
        d_aq, d_af, d_ai, d_lower[l] = _hgrn_bwd(u, lb_l, s["states"], do_a, off, f"hgrn_bwd")
        do_b, d_bg = _gate_b_bwd(dpb, s["o_b"], u, off, f"gate_b_bwd")
        d_bq, dkc, dkp, dvc, dvp, dbias_l, d_sink[l] = _attn_bwd(u, s["o_b"], do_b, bias_g, attn_sinks[l], off, w, f"attn_bwd")
        d_bk = _kv_combine(dkc, dkp, f"k_combine")
        d_bv = _kv_combine(dvc, dvp, f"v_combine")
        dbias_total = dbias_l if dbias_total is None else dbias_total + dbias_l
        d_cb, d_cc, d_cx, d_cg, d_conv[l] = _conv_bwd(dpc, u, cw_l, off, w, f"conv_bwd")
        du = jnp.concatenate([d_aq, d_af, d_ai, d_ag, d_bq, d_bk, d_bv, d_bg, d_cb, d_cc, d_cx, d_cg, dma, dmb, dmc], axis=1)
        g_in = _mm_tn(s["xb"], du, f"g_in")
        token = grads_hook(l, {"w_in": g_in, "w_proj_hgrn": g_pa, "w_proj_attn": g_pb, "w_proj_conv": g_pc, "w_out": g_out})
        dx = _mm_nt(du, wl["w_in"], f"d_x", add=dz, add_scale=ALPHA, dep=token)

    d_lower_all = jnp.concatenate([a[0:1] for a in d_lower], axis=0)
    small = {
        "lb_param": _lower_bounds_bwd(lb_param, d_lower_all),
        "hgrn_norm_g": jnp.concatenate([a[0:1] for a in d_gain], axis=0),
        "attn_sinks": jnp.concatenate([a[0:1, :n_heads] for a in d_sink], axis=0),
        "conv_w": jnp.stack([a[0:3] for a in d_conv], axis=0),
        "rel_bias": _bias_grad(dbias_total.reshape(n_heads, WINDOW, 2 * WINDOW), n_heads)[:, :n_heads],
        "ln_g": jnp.concatenate([a[0:1] for a in d_ln], axis=0),
        "ln_b": jnp.concatenate([a[1:2] for a in d_ln], axis=0),
    }
    return loss_acc, dx, small


BIG = ("w_in", "w_proj_hgrn", "w_proj_attn", "w_proj_conv", "w_out")
SMALL = ("lb_param", "hgrn_norm_g", "attn_sinks", "conv_w", "rel_bias", "ln_g", "ln_b")
ORDER = ("w_in", "w_proj_hgrn", "w_proj_attn", "w_proj_conv", "w_out", "lb_param", "hgrn_norm_g", "attn_sinks",
         "conv_w", "rel_bias", "ln_g", "ln_b")


def _pack(parts):
    flat = jnp.concatenate([p.reshape(-1) for p in parts])
    n = flat.shape[0]
    unit = SUBLANES * LANES
    total = -(-n // unit) * unit
    return jnp.pad(flat, (0, total - n)).reshape(total // LANES, LANES)


def _unpack(packed, shapes):
    flat = packed.reshape(-1)
    out, o = [], 0
    for shp in shapes:
        n = int(np.prod(shp))
        out.append(flat[o:o + n].reshape(shp))
        o += n
    return out


def kernel(x, w_in, w_proj_hgrn, w_proj_attn, w_proj_conv, w_out, lb_param, hgrn_norm_g, attn_sinks, conv_w, rel_bias, ln_g, ln_b, loss_target, m_w_in, m_w_proj_hgrn, m_w_proj_attn, m_w_proj_conv, m_w_out, m_lb_param, m_hgrn_norm_g, m_attn_sinks, m_conv_w, m_rel_bias, m_ln_g, m_ln_b, v_w_in, v_w_proj_hgrn, v_w_proj_attn, v_w_proj_conv, v_w_out, v_lb_param, v_hgrn_norm_g, v_attn_sinks, v_conv_w, v_rel_bias, v_ln_g, v_ln_b):
    params = dict(w_in=w_in, w_proj_hgrn=w_proj_hgrn, w_proj_attn=w_proj_attn, w_proj_conv=w_proj_conv, w_out=w_out,
                  lb_param=lb_param, hgrn_norm_g=hgrn_norm_g, attn_sinks=attn_sinks, conv_w=conv_w, rel_bias=rel_bias,
                  ln_g=ln_g, ln_b=ln_b)
    mom_m = dict(w_in=m_w_in, w_proj_hgrn=m_w_proj_hgrn, w_proj_attn=m_w_proj_attn, w_proj_conv=m_w_proj_conv,
                 w_out=m_w_out, lb_param=m_lb_param, hgrn_norm_g=m_hgrn_norm_g, attn_sinks=m_attn_sinks,
                 conv_w=m_conv_w, rel_bias=m_rel_bias, ln_g=m_ln_g, ln_b=m_ln_b)
    mom_v = dict(w_in=v_w_in, w_proj_hgrn=v_w_proj_hgrn, w_proj_attn=v_w_proj_attn, w_proj_conv=v_w_proj_conv,
                 w_out=v_w_out, lb_param=v_lb_param, hgrn_norm_g=v_hgrn_norm_g, attn_sinks=v_attn_sinks,
                 conv_w=v_conv_w, rel_bias=v_rel_bias, ln_g=v_ln_g, ln_b=v_ln_b)
    d = x.shape[-1]
    me = 4 * lax.axis_index("x") + 2 * lax.axis_index("y") + lax.axis_index("c")

    def shards_of(l):
        return [params[n][l].astype(BF16) for n in BIG] + [conv_w[l]]

    gathers = {0: _exchange_start(shards_of(0), False, "gather_start_0")}

    def weights_hook(l, x_in):
        got = _exchange_wait(gathers.pop(l), False, f"gather_wait_{l}", x_in)
        token = None
        if l + 1 < DEPTH:
            gathers[l + 1] = _exchange_start(shards_of(l + 1), False, f"gather_start_{l + 1}", dep=got[0])
            token = gathers[l + 1][-1]
        wl = {
            "w_in": _unshard_cols(got[0], "unshard_w_in"),
            "w_proj_hgrn": _unshard_cols(got[1], "unshard_w_proj_hgrn"),
            "w_proj_attn": _unshard_cols(got[2], "unshard_w_proj_attn"),
            "w_proj_conv": _unshard_cols(got[3], "unshard_w_proj_conv"),
            "w_out": got[4].reshape(d, d),
            "conv_w": _unshard_cols(got[5], "unshard_conv_w"),
        }
        return wl, token

    grads = {n: [None] * DEPTH for n in BIG}
    scatters = {}

    def finish_scatter(l, after):
        got = _exchange_wait(scatters.pop(l), True, f"scatter_wait_{l}", after)
        for n, slots in zip(BIG, got):
            grads[n][l] = _slot_sum(slots, f"sum_{n}")
        return got[0]

    def grads_hook(l, g):
        send = [_shard_cols(g["w_in"], "shard_g_in"), _shard_cols(g["w_proj_hgrn"], "shard_g_proj_a"),
                _shard_cols(g["w_proj_attn"], "shard_g_proj_b"), _shard_cols(g["w_proj_conv"], "shard_g_proj_c"),
                g["w_out"].reshape(N_DEV, d // N_DEV, d)]
        dep = finish_scatter(l + 1, send[0]) if l + 1 < DEPTH else None
        scatters[l] = _exchange_start(send, True, f"scatter_start_{l}", dep=dep)
        return scatters[l][-1]

    loss_acc, dx, small = _forward_backward(
        x[0], loss_target[0], weights_hook, grads_hook, lb_param, hgrn_norm_g, attn_sinks, rel_bias, ln_g, ln_b)
    loss = lax.psum(0.5 * jnp.sum(loss_acc[0]) / d, ("x", "y", "c"))
    finish_scatter(0, dx)
    for n in BIG:
        grads[n] = jnp.stack(grads[n], axis=0)

    small_shapes = [small[n].shape for n in SMALL]
    packed = _pack([small[n] for n in SMALL])
    got = _exchange([packed], False, "gather_small_grads")[0]
    summed = _unpack(_slot_sum(got, "sum_small_grads"), small_shapes)
    for n, g in zip(SMALL, summed):
        grads[n] = g
    cs = conv_w.shape[-1]
    grads["conv_w"] = lax.dynamic_slice_in_dim(grads["conv_w"], me * cs, cs, axis=2)

    delta, new_m, new_v = {}, {}, {}
    for n in BIG:
        shp = params[n].shape
        flat = lambda a: a.reshape(-1, shp[-1])
        dl, nm, nv = _adamw(flat(params[n]), flat(grads[n]), flat(mom_m[n]), flat(mom_v[n]), f"adamw_{n}")
        delta[n], new_m[n], new_v[n] = dl.reshape(shp), nm.reshape(shp), nv.reshape(shp)
    shapes = [params[n].shape for n in SMALL]
    res = _adamw(_pack([params[n] for n in SMALL]), _pack([grads[n] for n in SMALL]),
                 _pack([mom_m[n] for n in SMALL]), _pack([mom_v[n] for n in SMALL]), "adamw_small")
    for dst, packed_res in zip((delta, new_m, new_v), res):
        for n, a in zip(SMALL, _unpack(packed_res, shapes)):
            dst[n] = a

    return (loss, dx[None], *[grads[n] for n in ORDER], *[delta[n] for n in ORDER],
            *[new_m[n] for n in ORDER], *[new_v[n] for n in ORDER])
```

```python
import functools
import math

import numpy as np
import jax
import jax.numpy as jnp
from jax import lax
from jax.experimental import pallas as pl
from jax.experimental.pallas import tpu as pltpu

F32 = jnp.float32
BF16 = jnp.bfloat16

N_DEV = 8
DEPTH = 4
HGRN_HEAD_DIM = 128
HGRN_CHUNK = 64
ATTN_HEAD_DIM = 64
ATTN_KV_HEADS = 4
KV_WIDTH = ATTN_KV_HEADS * ATTN_HEAD_DIM
WINDOW = 128
WINDOW_SHIFT = 7
N_BUCKETS = 32
MAX_DISTANCE = 128
ALPHA = (2.0 * DEPTH) ** 0.25
LN_EPS = 1e-5
RMS_EPS = 1e-6
ADAM_LR = 0.001
ADAM_B1 = 0.9
ADAM_B2 = 0.999
ADAM_EPS = 1e-08
ADAM_WD = 0.01
ADAM_STEP = 10

LANES = 128
SUBLANES = 8
VMEM_LIMIT = 56 << 20
NEG_INF = float("-inf")


def _offsets(d_model):
    w = d_model // 2
    sizes = (w, w, w, w, w, KV_WIDTH, KV_WIDTH, w, w, w, w, w, d_model, d_model, d_model)
    names = ("a_q", "a_f", "a_i", "a_g", "b_q", "b_k", "b_v", "b_g", "c_b", "c_c", "c_x", "c_g", "m_a", "m_b", "m_c")
    off, o = {}, 0
    for n, s in zip(names, sizes):
        off[n] = o
        o += s
    return off, o


def _tile(n, pref):
    t = min(pref, n)
    while n % t:
        t //= 2
    return t


def _cp(sem=None, vmem=VMEM_LIMIT):
    return pltpu.CompilerParams(dimension_semantics=sem, vmem_limit_bytes=vmem)


def _sigmoid(x):
    return 1.0 / (1.0 + jnp.exp(-x))


def _dot_nn(a, b):
    return jnp.dot(a, b, preferred_element_type=F32)


def _dot_nt(a, b):
    return lax.dot_general(a, b, (((1,), (1,)), ((), ())), preferred_element_type=F32)


def _dot_tn(a, b):
    return lax.dot_general(a, b, (((0,), (0,)), ((), ())), preferred_element_type=F32)


def _dep_specs(dep):
    return ([], []) if dep is None else ([pl.BlockSpec(memory_space=pl.ANY)], [dep])


def _mm_nn(a, b, name, out_dtype=F32, tm=1024, tn=1536, dep=None):
    m, k = a.shape
    _, n = b.shape
    tm, tn = _tile(m, tm), _tile(n, tn)
    dep_specs, dep_args = _dep_specs(dep)

    def body(a_ref, b_ref, *rest):
        o_ref = rest[-1]
        o_ref[...] = _dot_nn(a_ref[...], b_ref[...]).astype(o_ref.dtype)

    return pl.pallas_call(
        body, grid=(n // tn, m // tm),
        in_specs=[pl.BlockSpec((tm, k), lambda j, i: (i, 0)), pl.BlockSpec((k, tn), lambda j, i: (0, j))] + dep_specs,
        out_specs=pl.BlockSpec((tm, tn), lambda j, i: (i, j)),
        out_shape=jax.ShapeDtypeStruct((m, n), out_dtype), name=name,
        compiler_params=_cp(("parallel", "parallel")))(a, b, *dep_args)


def _mm_nt(a, b, name, tm=1024, tk=1536, add=None, add_scale=1.0, dep=None):
    m, k = a.shape
    n, _ = b.shape
    tm, tk = _tile(m, tm), _tile(k, tk)
    has_add = add is not None
    dep_specs, dep_args = _dep_specs(dep)

    def body(*refs):
        if has_add:
            a_ref, b_ref, add_ref = refs[:3]
        else:
            a_ref, b_ref = refs[:2]
        o_ref = refs[-1]

        @pl.when(pl.program_id(1) == 0)
        def _():
            if has_add:
                o_ref[...] = add_ref[...] * add_scale
            else:
                o_ref[...] = jnp.zeros_like(o_ref)

        o_ref[...] += _dot_nt(a_ref[...], b_ref[...])

    in_specs = [pl.BlockSpec((tm, tk), lambda i, kk: (i, kk)), pl.BlockSpec((n, tk), lambda i, kk: (0, kk))]
    args = [a, b]
    if has_add:
        in_specs.append(pl.BlockSpec((tm, n), lambda i, kk: (i, 0)))
        args.append(add)
    in_specs += dep_specs
    args += dep_args
    return pl.pallas_call(
        body, grid=(m // tm, k // tk), in_specs=in_specs,
        out_specs=pl.BlockSpec((tm, n), lambda i, kk: (i, 0)),
        out_shape=jax.ShapeDtypeStruct((m, n), F32), name=name,
        compiler_params=_cp(("parallel", "arbitrary")))(*args)


def _mm_tn(a, b, name, tt=512, tn=1536):
    t, k = a.shape
    _, n = b.shape
    tt, tn = _tile(t, tt), _tile(n, tn)

    def body(a_ref, b_ref, o_ref):
        @pl.when(pl.program_id(1) == 0)
        def _():
            o_ref[...] = jnp.zeros_like(o_ref)

        o_ref[...] += _dot_tn(a_ref[...], b_ref[...])

    return pl.pallas_call(
        body, grid=(n // tn, t // tt),
        in_specs=[pl.BlockSpec((tt, k), lambda j, s: (s, 0)), pl.BlockSpec((tt, tn), lambda j, s: (s, j))],
        out_specs=pl.BlockSpec((k, tn), lambda j, s: (0, j)),
        out_shape=jax.ShapeDtypeStruct((k, n), F32), name=name,
        compiler_params=_cp(("parallel", "arbitrary")))(a, b)


def _ew(body, name, t, ncol, wb, ins, outs, accs=(), tt=256):
    tt = _tile(t, tt)
    nt = t // tt
    in_specs, args = [], []
    for arr, kind, coff in ins:
        if kind == "tile":
            spec = pl.BlockSpec((tt, wb), lambda j, i, c=coff: (i, c + j))
        elif kind == "prev":
            spec = pl.BlockSpec((tt, wb), lambda j, i, c=coff: (jnp.maximum(i - 1, 0), c + j))
        elif kind == "next":
            spec = pl.BlockSpec((tt, wb), lambda j, i, c=coff: (jnp.minimum(i + 1, nt - 1), c + j))
        else:
            spec = pl.BlockSpec((arr.shape[0], wb), lambda j, i, c=coff: (0, c + j))
        in_specs.append(spec)
        args.append(arr)
    out_specs = [pl.BlockSpec((tt, wb), lambda j, i: (i, j)) for _ in outs]
    out_shape = [jax.ShapeDtypeStruct((t, ncol * wb), d) for d in outs]
    for r in accs:
        out_specs.append(pl.BlockSpec((r, wb), lambda j, i: (0, j)))
        out_shape.append(jax.ShapeDtypeStruct((r, ncol * wb), F32))

    def kern(*refs):
        body(pl.program_id(1), nt, *refs)

    res = pl.pallas_call(
        kern, grid=(ncol, nt), in_specs=in_specs, out_specs=out_specs, out_shape=out_shape, name=name,
        compiler_params=_cp(("parallel", "arbitrary")))(*args)
    return res


def _silu_parts(x):
    s = _sigmoid(x)
    return x * s, s + x * s * (1.0 - s)


def _ln_fwd(x, y, g, b, name):
    t, d = x.shape
    tt = _tile(t, 256)

    def body(x_ref, y_ref, g_ref, b_ref, o_ref, ob_ref, xh_ref, r_ref):
        z = ALPHA * x_ref[...] + y_ref[...]
        mu = jnp.mean(z, axis=1, keepdims=True)
        zc = z - mu
        var = jnp.mean(zc * zc, axis=1, keepdims=True)
        rstd = lax.rsqrt(var + LN_EPS)
        xh = zc * rstd
        o = xh * g_ref[...] + b_ref[...]
        o_ref[...] = o
        ob_ref[...] = o.astype(BF16)
        xh_ref[...] = xh
        r_ref[...] = rstd

    row = pl.BlockSpec((tt, d), lambda i: (i, 0))
    vec = pl.BlockSpec((1, d), lambda i: (0, 0))
    return pl.pallas_call(
        body, grid=(t // tt,), in_specs=[row, row, vec, vec],
        out_specs=[row, row, row, pl.BlockSpec((tt, 1), lambda i: (i, 0))],
        out_shape=[jax.ShapeDtypeStruct((t, d), F32), jax.ShapeDtypeStruct((t, d), BF16),
                   jax.ShapeDtypeStruct((t, d), F32), jax.ShapeDtypeStruct((t, 1), F32)],
        name=name, compiler_params=_cp(("parallel",)))(x, y, g, b)


def _ln_bwd(dout, xhat, rstd, g, name):
    t, d = dout.shape
    tt = _tile(t, 256)

    def body(do_ref, xh_ref, r_ref, g_ref, dz_ref, dzb_ref, acc_ref):
        @pl.when(pl.program_id(0) == 0)
        def _():
            acc_ref[...] = jnp.zeros_like(acc_ref)

        do = do_ref[...]
        xh = xh_ref[...]
        dxh = do * g_ref[...]
        m1 = jnp.mean(dxh, axis=1, keepdims=True)
        m2 = jnp.mean(dxh * xh, axis=1, keepdims=True)
        dz = r_ref[...] * (dxh - m1 - xh * m2)
        dz_ref[...] = dz
        dzb_ref[...] = dz.astype(BF16)
        acc_ref[0:1, :] += jnp.sum(do * xh, axis=0, keepdims=True)
        acc_ref[1:2, :] += jnp.sum(do, axis=0, keepdims=True)

    row = pl.BlockSpec((tt, d), lambda i: (i, 0))
    return pl.pallas_call(
        body, grid=(t // tt,),
        in_specs=[row, row, pl.BlockSpec((tt, 1), lambda i: (i, 0)), pl.BlockSpec((1, d), lambda i: (0, 0))],
        out_specs=[row, row, pl.BlockSpec((SUBLANES, d), lambda i: (0, 0))],
        out_shape=[jax.ShapeDtypeStruct((t, d), F32), jax.ShapeDtypeStruct((t, d), BF16),
                   jax.ShapeDtypeStruct((SUBLANES, d), F32)],
        name=name, compiler_params=_cp(("arbitrary",)))(dout, xhat, rstd, g)


def _loss_head(y, target):
    t, d = y.shape
    tt = _tile(t, 256)

    def body(y_ref, t_ref, acc_ref, dy_ref):
        @pl.when(pl.program_id(0) == 0)
        def _():
            acc_ref[...] = jnp.zeros_like(acc_ref)

        err = y_ref[...] - t_ref[...]
        dy_ref[...] = err * (1.0 / d)
        acc_ref[0:1, :] += jnp.sum(err * err, axis=0, keepdims=True)

    row = pl.BlockSpec((tt, d), lambda i: (i, 0))
    acc, dy = pl.pallas_call(
        body, grid=(t // tt,), in_specs=[row, row],
        out_specs=[pl.BlockSpec((SUBLANES, d), lambda i: (0, 0)), row],
        out_shape=[jax.ShapeDtypeStruct((SUBLANES, d), F32), jax.ShapeDtypeStruct((t, d), F32)],
        name="loss_head", compiler_params=_cp(("arbitrary",)))(y, target)
    return acc, dy


def _tri(lower):
    r = lax.broadcasted_iota(jnp.int32, (HGRN_CHUNK, HGRN_CHUNK), 0)
    c = lax.broadcasted_iota(jnp.int32, (HGRN_CHUNK, HGRN_CHUNK), 1)
    return jnp.where((r >= c) if lower else (r <= c), 1.0, 0.0).astype(BF16)


def _exact_tri_matmul(tri, x):
    hi = x.astype(BF16)
    r1 = x - hi.astype(F32)
    mid = r1.astype(BF16)
    lo = (r1 - mid.astype(F32)).astype(BF16)
    return _dot_nn(tri, hi) + _dot_nn(tri, mid) + _dot_nn(tri, lo)


def _hgrn_gates(q_raw, fl, lb):
    sq = _sigmoid(q_raw)
    qf = q_raw * sq * (HGRN_HEAD_DIM ** -0.5)
    sg = _sigmoid(fl)
    f = lb + (1.0 - lb) * sg
    return qf, sq, sg, f


HGRN_SUB = 16
HGRN_NSUB = HGRN_CHUNK // HGRN_SUB
HGRN_HEADS_PER_STEP = 8
HGRN_HEADS_PER_STEP_BWD = 4


def _diag_rows(r):
    return (r // SUBLANES) * SUBLANES


def _heads(x):
    hd = HGRN_HEAD_DIM
    return [x[:, i * hd:(i + 1) * hd] for i in range(x.shape[1] // hd)]


def _per_head(fn, *xs):
    split = [x if isinstance(x, (list, tuple)) else _heads(x) for x in xs]
    return jnp.concatenate([fn(*hs) for hs in zip(*split)], axis=1)


def _head_lane_sum(x):
    return _per_head(lambda h: jnp.broadcast_to(jnp.sum(h, axis=1, keepdims=True), h.shape), x)


def _hgrn_intra_fwd(qf, k, v, b):
    ch, sub, wd = HGRN_CHUNK, HGRN_SUB, qf.shape[1]
    tl = lax.broadcasted_iota(jnp.int32, (sub, wd), 0)
    blocks = []
    for m in range(HGRN_NSUB):
        rs = slice(m * sub, (m + 1) * sub)
        bm, qm, km, vm = b[rs], qf[rs], k[rs], v[rs]
        parts = {0: jnp.zeros((sub, wd), F32), SUBLANES: jnp.zeros((sub - SUBLANES, wd), F32)}
        for r in range(sub):
            lo = _diag_rows(r)
            e = jnp.exp(jnp.where(tl[lo:] >= r, bm[lo:] - bm[r:r + 1], NEG_INF))
            parts[lo] = parts[lo] + _head_lane_sum(qm[lo:] * e * km[r:r + 1]) * vm[r:r + 1]
        blocks.append(parts[0] + jnp.concatenate([jnp.zeros((SUBLANES, wd), F32), parts[SUBLANES]], axis=0))
    acc = jnp.concatenate(blocks, axis=0)
    for j in range(HGRN_NSUB - 1):
        lo = sub * (j + 1)
        c = b[lo - 1:lo, :]
        qj = (qf[lo:] * jnp.exp(b[lo:] - c)).astype(BF16)
        kj = (k[lo - sub:lo] * jnp.exp(c - b[lo - sub:lo])).astype(BF16)
        vj = v[lo - sub:lo].astype(BF16)
        contrib = _per_head(lambda q_, k_, v_: _dot_nn(_dot_nt(q_, k_).astype(BF16), v_), qj, kj, vj)
        acc = acc + jnp.concatenate([jnp.zeros((lo, wd), F32), contrib], axis=0)
    return acc


def _hgrn_intra_bwd(qf, k, v, b, do_v):
    ch, sub, wd = HGRN_CHUNK, HGRN_SUB, qf.shape[1]
    tl = lax.broadcasted_iota(jnp.int32, (sub, wd), 0)
    dq_blocks, dk_blocks, dv_blocks = [], [], []
    for m in range(HGRN_NSUB):
        rs = slice(m * sub, (m + 1) * sub)
        bm, qm, km, vm, dom = b[rs], qf[rs], k[rs], v[rs], do_v[rs]
        parts = {0: jnp.zeros((sub, wd), F32), SUBLANES: jnp.zeros((sub - SUBLANES, wd), F32)}
        dk_parts = {sub: jnp.zeros((sub, wd), F32), SUBLANES: jnp.zeros((SUBLANES, wd), F32)}
        dv_parts = {sub: jnp.zeros((sub, wd), F32), SUBLANES: jnp.zeros((SUBLANES, wd), F32)}
        for r in range(sub):
            lo = _diag_rows(r)
            b_r, k_r, v_r, q_r, do_r = bm[r:r + 1], km[r:r + 1], vm[r:r + 1], qm[r:r + 1], dom[r:r + 1]
            e = jnp.exp(jnp.where(tl[lo:] >= r, bm[lo:] - b_r, NEG_INF))
            parts[lo] = parts[lo] + _head_lane_sum(dom[lo:] * v_r) * (k_r * e)
            hi = lo + SUBLANES
            e2 = jnp.exp(jnp.where(tl[:hi] <= r, b_r - bm[:hi], NEG_INF))
            qe2 = q_r * e2
            dk_parts[hi] = dk_parts[hi] + _head_lane_sum(vm[:hi] * do_r) * qe2
            dv_parts[hi] = dv_parts[hi] + _head_lane_sum(km[:hi] * qe2) * do_r
        pad = jnp.zeros((SUBLANES, wd), F32)
        dq_blocks.append(parts[0] + jnp.concatenate([pad, parts[SUBLANES]], axis=0))
        dk_blocks.append(dk_parts[sub] + jnp.concatenate([dk_parts[SUBLANES], pad], axis=0))
        dv_blocks.append(dv_parts[sub] + jnp.concatenate([dv_parts[SUBLANES], pad], axis=0))
    dq = jnp.concatenate(dq_blocks, axis=0)
    dk = jnp.concatenate(dk_blocks, axis=0)
    dv = jnp.concatenate(dv_blocks, axis=0)
    do_b, v_b = do_v.astype(BF16), v.astype(BF16)
    dk_off, dv_off = [], []
    for j in range(HGRN_NSUB - 1):
        lo = sub * (j + 1)
        c = b[lo - 1:lo, :]
        eq = jnp.exp(b[lo:] - c)
        ek = jnp.exp(c - b[lo - sub:lo])
        qj = (qf[lo:] * eq).astype(BF16)
        kj = (k[lo - sub:lo] * ek).astype(BF16)
        doj, vj = do_b[lo:], v_b[lo - sub:lo]
        dq_j = _per_head(lambda do_, v_, k_: _dot_nn(_dot_nt(do_, v_).astype(BF16), k_), doj, vj, kj)
        dk_j = _per_head(lambda do_, v_, q_: _dot_nn(_dot_nt(v_, do_).astype(BF16), q_), doj, vj, qj)
        dv_j = _per_head(lambda do_, k_, q_: _dot_nn(_dot_nt(k_, q_).astype(BF16), do_), doj, kj, qj)
        dq = dq + jnp.concatenate([jnp.zeros((lo, wd), F32), dq_j * eq], axis=0)
        dk_off.append(dk_j * ek)
        dv_off.append(dv_j)
    zero = jnp.zeros((sub, wd), F32)
    dk = dk + jnp.concatenate(dk_off + [zero], axis=0)
    dv = dv + jnp.concatenate(dv_off + [zero], axis=0)
    return dq, dk, dv


def _hgrn_fwd(u, lb, off, name):
    t = u.shape[0]
    w = lb.shape[1]
    hd, ch, hp = HGRN_HEAD_DIM, HGRN_CHUNK, HGRN_HEADS_PER_STEP
    nh, nc = w // hd, t // ch
    wb = hp * hd
    cq, cf, cv = off["a_q"] // wb, off["a_f"] // wb, off["a_i"] // wb

    def body(q_ref, f_ref, v_ref, lb_ref, o_ref, st_ref, state):
        @pl.when(pl.program_id(1) == 0)
        def _():
            state[...] = jnp.zeros_like(state)

        sts = [state[i] for i in range(hp)]
        qf, _, _, f = _hgrn_gates(q_ref[...], f_ref[...], lb_ref[...])
        k = 1.0 - f
        v = v_ref[...]
        b = _exact_tri_matmul(_tri(True), jnp.log(f))
        inter = _per_head(lambda qa_, st_: _dot_nt(qa_, st_.astype(BF16)), (qf * jnp.exp(b)).astype(BF16), sts)
        o_ref[...] = inter + _hgrn_intra_fwd(qf, k, v, b)
        b_end = b[ch - 1:ch, :]
        a_end = _heads(jnp.exp(b_end))
        kd = _heads((k * jnp.exp(b_end - b)).astype(BF16))
        v_b = _heads(v.astype(BF16))
        for i in range(hp):
            st_ref[i, 0] = sts[i]
            state[i] = sts[i] * a_end[i] + _dot_tn(v_b[i], kd[i])

    return pl.pallas_call(
        body, grid=(nh // hp, nc),
        in_specs=[pl.BlockSpec((ch, wb), lambda h, n: (n, cq + h)),
                  pl.BlockSpec((ch, wb), lambda h, n: (n, cf + h)),
                  pl.BlockSpec((ch, wb), lambda h, n: (n, cv + h)),
                  pl.BlockSpec((1, wb), lambda h, n: (0, h))],
        out_specs=[pl.BlockSpec((ch, wb), lambda h, n: (n, h)),
                   pl.BlockSpec((hp, 1, hd, hd), lambda h, n: (h, n, 0, 0))],
        out_shape=[jax.ShapeDtypeStruct((t, w), F32), jax.ShapeDtypeStruct((nh, nc, hd, hd), F32)],
        scratch_shapes=[pltpu.VMEM((hp, hd, hd), F32)],
        name=name, compiler_params=_cp(("parallel", "arbitrary")))(u, u, u, lb)


def _hgrn_bwd(u, lb, states, do, off, name):
    t = u.shape[0]
    w = lb.shape[1]
    hd, ch, hp = HGRN_HEAD_DIM, HGRN_CHUNK, HGRN_HEADS_PER_STEP_BWD
    nh, nc = w // hd, t // ch
    wb = hp * hd
    cq, cf, cv = off["a_q"] // wb, off["a_f"] // wb, off["a_i"] // wb

    def body(q_ref, f_ref, v_ref, do_ref, st_ref, lb_ref, dq_ref, df_ref, dv_ref, dlb_ref, dstate):
        @pl.when(pl.program_id(1) == 0)
        def _():
            dstate[...] = jnp.zeros_like(dstate)
            dlb_ref[...] = jnp.zeros_like(dlb_ref)

        rows = lax.broadcasted_iota(jnp.int32, (ch, wb), 0)
        lb_row = lb_ref[...]
        q_raw = q_ref[...]
        qf, sq, sg, f = _hgrn_gates(q_raw, f_ref[...], lb_row)
        k = 1.0 - f
        b = _exact_tri_matmul(_tri(True), jnp.log(f))
        a = jnp.exp(b)
        b_end = b[ch - 1:ch, :]
        a_end = jnp.exp(b_end)
        to_end = jnp.exp(b_end - b)
        do_v = do_ref[...]
        v = v_ref[...]
        st0 = [st_ref[i, 0] for i in range(hp)]
        ds = [dstate[i] for i in range(hp)]
        st0_b = [s_.astype(BF16) for s_ in st0]
        ds_b = [s_.astype(BF16) for s_ in ds]
        do_b, v_b, kd_b, qa_b = do_v.astype(BF16), v.astype(BF16), (k * to_end).astype(BF16), (qf * a).astype(BF16)

        dq_inter = a * _per_head(_dot_nn, do_b, st0_b)
        dk_end = to_end * _per_head(_dot_nn, v_b, ds_b)
        dv_end = _per_head(_dot_nt, kd_b, ds_b)
        a_end_h = _heads(a_end)
        st_end = [st0[i] * a_end_h[i] + _dot_tn(_heads(v_b)[i], _heads(kd_b)[i]) for i in range(hp)]
        db_end = jnp.concatenate([jnp.sum(ds[i] * st_end[i], axis=0, keepdims=True) for i in range(hp)], axis=1)
        ds_new = [ds[i] * a_end_h[i] + _dot_tn(_heads(do_b)[i], _heads(qa_b)[i]) for i in range(hp)]

        dq_intra, dk_intra, dv_intra = _hgrn_intra_bwd(qf, k, v, b, do_v)
        dqf = dq_inter + dq_intra
        dk = dk_end + dk_intra
        dv = dv_end + dv_intra
        db = qf * dqf - k * dk
        db = db + jnp.where(rows == ch - 1, db_end, 0.0)
        dg = _exact_tri_matmul(_tri(False), db)
        df = dg / f - dk
        for i in range(hp):
            dstate[i] = ds_new[i]
        dq_ref[...] = (dqf * (HGRN_HEAD_DIM ** -0.5) * (sq + q_raw * sq * (1.0 - sq))).astype(BF16)
        df_ref[...] = (df * (1.0 - lb_row) * sg * (1.0 - sg)).astype(BF16)
        dv_ref[...] = dv.astype(BF16)
        dlb_ref[0:1, :] += jnp.sum(df * (1.0 - sg), axis=0, keepdims=True)

    rev = lambda n: nc - 1 - n
    tile = lambda c: pl.BlockSpec((ch, wb), lambda h, n, c=c: (rev(n), c + h))
    return pl.pallas_call(
        body, grid=(nh // hp, nc),
        in_specs=[tile(cq), tile(cf), tile(cv), tile(0),
                  pl.BlockSpec((hp, 1, hd, hd), lambda h, n: (h, rev(n), 0, 0)),
                  pl.BlockSpec((1, wb), lambda h, n: (0, h))],
        out_specs=[tile(0), tile(0), tile(0), pl.BlockSpec((SUBLANES, wb), lambda h, n: (0, h))],
        out_shape=[jax.ShapeDtypeStruct((t, w), BF16)] * 3 + [jax.ShapeDtypeStruct((SUBLANES, w), F32)],
        scratch_shapes=[pltpu.VMEM((hp, hd, hd), F32)],
        name=name, compiler_params=_cp(("parallel", "arbitrary")))(u, u, u, do, states, lb)


def _gate_a_fwd(o, u, gain, off, name):
    t, w = o.shape
    hd = HGRN_HEAD_DIM

    def body(i, nt, o_ref, g_ref, gain_ref, p_ref):
        silu, _ = _silu_parts(g_ref[...])
        for h in range(w // hd):
            sl = slice(h * hd, (h + 1) * hd)
            oh = o_ref[:, sl]
            r = lax.rsqrt(jnp.mean(oh * oh, axis=1, keepdims=True) + RMS_EPS)
            p_ref[:, sl] = (oh * r * gain_ref[:, sl] * silu[:, sl]).astype(BF16)

    return _ew(body, name, t, 1, w, [(o, "tile", 0), (u, "tile", off["a_g"] // w), (gain, "row", 0)], [BF16])[0]


def _gate_a_bwd(dp, o, u, gain, off, name):
    t, w = o.shape
    hd = HGRN_HEAD_DIM

    def body(i, nt, dp_ref, o_ref, g_ref, gain_ref, do_ref, dg_ref, acc_ref):
        @pl.when(i == 0)
        def _():
            acc_ref[...] = jnp.zeros_like(acc_ref)

        silu, dsilu = _silu_parts(g_ref[...])
        dp_v = dp_ref[...]
        for h in range(w // hd):
            sl = slice(h * hd, (h + 1) * hd)
            oh = o_ref[:, sl]
            r = lax.rsqrt(jnp.mean(oh * oh, axis=1, keepdims=True) + RMS_EPS)
            nrm = oh * r
            gn = gain_ref[:, sl]
            dph = dp_v[:, sl]
            dg_ref[:, sl] = (dph * nrm * gn * dsilu[:, sl]).astype(BF16)
            acc_ref[0:1, sl] += jnp.sum(dph * nrm * silu[:, sl], axis=0, keepdims=True)
            dn = dph * gn * silu[:, sl]
            do_ref[:, sl] = r * (dn - nrm * jnp.mean(dn * nrm, axis=1, keepdims=True))

    return _ew(body, name, t, 1, w,
               [(dp, "tile", 0), (o, "tile", 0), (u, "tile", off["a_g"] // w), (gain, "row", 0)],
               [F32, BF16], accs=[SUBLANES])


def _bucket_map():
    i = np.arange(WINDOW)[:, None]
    j = np.arange(2 * WINDOW)[None, :]
    dist = np.clip(WINDOW + i - j, 0, WINDOW - 1)
    max_exact = N_BUCKETS // 2
    logd = (np.log(np.maximum(dist, 1).astype(np.float32) / max_exact) / math.log(MAX_DISTANCE / max_exact))
    large = np.minimum(max_exact + (logd.astype(np.float32) * (N_BUCKETS - max_exact)).astype(np.int32), N_BUCKETS - 1)
    return np.where(dist < max_exact, dist, large).astype(np.int32)


def _bias_table(rel_bias, n_heads):
    bucket = jnp.asarray(_bucket_map())

    def body(rb_ref, bk_ref, o_ref):
        bk = bk_ref[...]
        for h in range(n_heads):
            def step(bi, acc):
                return jnp.where(bk == bi, rb_ref[bi, h], acc)
            o_ref[h] = lax.fori_loop(0, N_BUCKETS, step, jnp.zeros((WINDOW, 2 * WINDOW), F32))

    return pl.pallas_call(
        body, in_specs=[pl.BlockSpec(memory_space=pltpu.SMEM), pl.BlockSpec(memory_space=pltpu.VMEM)],
        out_specs=pl.BlockSpec(memory_space=pltpu.VMEM),
        out_shape=jax.ShapeDtypeStruct((n_heads, WINDOW, 2 * WINDOW), F32), name="bias_table",
        compiler_params=_cp())(rel_bias, bucket)


def _bias_grad(dbias, n_heads):
    bucket = jnp.asarray(_bucket_map())

    def body(db_ref, bk_ref, o_ref):
        bk = bk_ref[...]
        lane = lax.broadcasted_iota(jnp.int32, (1, LANES), 1)

        def step(bi, carry):
            row = jnp.zeros((1, LANES), F32)
            for h in range(n_heads):
                val = jnp.sum(jnp.where(bk == bi, db_ref[h], 0.0))
                row = jnp.where(lane == h, val, row)
            o_ref[pl.ds(bi, 1), :] = row
            return carry

        lax.fori_loop(0, N_BUCKETS, step, 0)

    return pl.pallas_call(
        body, in_specs=[pl.BlockSpec(memory_space=pltpu.VMEM), pl.BlockSpec(memory_space=pltpu.VMEM)],
        out_specs=pl.BlockSpec(memory_space=pltpu.VMEM),
        out_shape=jax.ShapeDtypeStruct((N_BUCKETS, LANES), F32), name="bias_grad",
        compiler_params=_cp())(dbias, bucket)


def _attn_probs(n, q_ref, kp_ref, kc_ref, bias_ref, sink_ref, hh, grp):
    ad, wn = ATTN_HEAD_DIM, WINDOW
    ksl = slice(hh * ad, (hh + 1) * ad)
    kw = jnp.concatenate([kp_ref[:, ksl], kc_ref[:, ksl]], axis=0).astype(BF16)
    qs = jnp.concatenate([q_ref[:, (hh * grp + g) * ad:(hh * grp + g + 1) * ad] for g in range(grp)], axis=0).astype(BF16)
    s = _dot_nt(qs, kw) * (ad ** -0.5) + bias_ref[hh]
    r = lax.broadcasted_iota(jnp.int32, (grp * wn, 2 * wn), 0)
    j = lax.broadcasted_iota(jnp.int32, (grp * wn, 2 * wn), 1)
    i = r & (wn - 1)
    valid = ((j >= wn) & (j - wn <= i)) | ((j < wn) & (j > i) & (n > 0))
    s = jnp.where(valid, s, NEG_INF)
    rr = lax.broadcasted_iota(jnp.int32, (grp * wn, 1), 0) >> WINDOW_SHIFT
    sink = jnp.zeros((grp * wn, 1), F32)
    for g in range(grp):
        sink = jnp.where(rr == g, sink_ref[hh * grp + g], sink)
    m = jnp.maximum(jnp.max(s, axis=1, keepdims=True), sink)
    p = jnp.exp(s - m)
    es = jnp.exp(sink - m)
    inv = 1.0 / (jnp.sum(p, axis=1, keepdims=True) + es)
    return qs, kw, p * inv, es * inv


def _attn_fwd(u, bias_g, sinks, off, w, name):
    t = u.shape[0]
    wn, ad, kvw = WINDOW, ATTN_HEAD_DIM, KV_WIDTH
    grp = (w // ad) // ATTN_KV_HEADS
    nb = t // wn
    cq, ck, cv = off["b_q"] // w, off["b_k"] // kvw, off["b_v"] // kvw

    def body(q_ref, kp_ref, kc_ref, vp_ref, vc_ref, bias_ref, sink_ref, o_ref):
        n = pl.program_id(0)
        for hh in range(ATTN_KV_HEADS):
            _, _, p, _ = _attn_probs(n, q_ref, kp_ref, kc_ref, bias_ref, sink_ref, hh, grp)
            ksl = slice(hh * ad, (hh + 1) * ad)
            vw = jnp.concatenate([vp_ref[:, ksl], vc_ref[:, ksl]], axis=0).astype(BF16)
            o = _dot_nn(p.astype(BF16), vw)
            for g in range(grp):
                o_ref[:, (hh * grp + g) * ad:(hh * grp + g + 1) * ad] = o[g * wn:(g + 1) * wn]

    prev = lambda n: jnp.maximum(n - 1, 0)
    return pl.pallas_call(
        body, grid=(nb,),
        in_specs=[pl.BlockSpec((wn, w), lambda n: (n, cq)),
                  pl.BlockSpec((wn, kvw), lambda n: (prev(n), ck)), pl.BlockSpec((wn, kvw), lambda n: (n, ck)),
                  pl.BlockSpec((wn, kvw), lambda n: (prev(n), cv)), pl.BlockSpec((wn, kvw), lambda n: (n, cv)),
                  pl.BlockSpec((ATTN_KV_HEADS, grp * wn, 2 * wn), lambda n: (0, 0, 0)),
                  pl.BlockSpec(memory_space=pltpu.SMEM)],
        out_specs=pl.BlockSpec((wn, w), lambda n: (n, 0)),
        out_shape=jax.ShapeDtypeStruct((t, w), F32), name=name,
        compiler_params=_cp(("parallel",)))(u, u, u, u, u, bias_g, sinks)


def _attn_bwd(u, o, do, bias_g, sinks, off, w, name):
    t = u.shape[0]
    wn, ad, kvw = WINDOW, ATTN_HEAD_DIM, KV_WIDTH
    grp = (w // ad) // ATTN_KV_HEADS
    nb = t // wn
    cq, ck, cv = off["b_q"] // w, off["b_k"] // kvw, off["b_v"] // kvw

    def body(q_ref, kp_ref, kc_ref, vp_ref, vc_ref, o_ref, do_ref, bias_ref, sink_ref,
             dq_ref, dkc_ref, dkp_ref, dvc_ref, dvp_ref, dbias_ref, dsink_ref):
        n = pl.program_id(0)

        @pl.when(n == 0)
        def _():
            dbias_ref[...] = jnp.zeros_like(dbias_ref)
            dsink_ref[...] = jnp.zeros_like(dsink_ref)

        lane = lax.broadcasted_iota(jnp.int32, (1, LANES), 1)
        rr = lax.broadcasted_iota(jnp.int32, (grp * wn, 1), 0) >> WINDOW_SHIFT
        dsink_row = jnp.zeros((1, LANES), F32)
        for hh in range(ATTN_KV_HEADS):
            qs, kw, p, psink = _attn_probs(n, q_ref, kp_ref, kc_ref, bias_ref, sink_ref, hh, grp)
            ksl = slice(hh * ad, (hh + 1) * ad)
            vw = jnp.concatenate([vp_ref[:, ksl], vc_ref[:, ksl]], axis=0).astype(BF16)
            hs = [slice((hh * grp + g) * ad, (hh * grp + g + 1) * ad) for g in range(grp)]
            dos = jnp.concatenate([do_ref[:, sl] for sl in hs], axis=0)
            os_ = jnp.concatenate([o_ref[:, sl] for sl in hs], axis=0)
            delta = jnp.sum(dos * os_, axis=1, keepdims=True)
            dos_b = dos.astype(BF16)
            dp = _dot_nt(dos_b, vw)
            ds = p * (dp - delta)
            dbias_ref[hh] += ds
            sd = psink * delta
            for g in range(grp):
                val = -jnp.sum(jnp.where(rr == g, sd, 0.0))
                dsink_row = jnp.where(lane == hh * grp + g, val, dsink_row)
            ds_b = (ds * (ad ** -0.5)).astype(BF16)
            dq = _dot_nn(ds_b, kw)
            for g in range(grp):
                dq_ref[:, hs[g]] = dq[g * wn:(g + 1) * wn].astype(BF16)
            dkw = _dot_tn(ds_b, qs)
            dvw = _dot_tn(p.astype(BF16), dos_b)
            dkp_ref[:, ksl] = dkw[:wn]
            dkc_ref[:, ksl] = dkw[wn:]
            dvp_ref[:, ksl] = dvw[:wn]
            dvc_ref[:, ksl] = dvw[wn:]
        dsink_ref[0:1, :] += dsink_row

    prev = lambda n: jnp.maximum(n - 1, 0)
    kv_out = pl.BlockSpec((wn, kvw), lambda n: (n, 0))
    return pl.pallas_call(
        body, grid=(nb,),
        in_specs=[pl.BlockSpec((wn, w), lambda n: (n, cq)),
                  pl.BlockSpec((wn, kvw), lambda n: (prev(n), ck)), pl.BlockSpec((wn, kvw), lambda n: (n, ck)),
                  pl.BlockSpec((wn, kvw), lambda n: (prev(n), cv)), pl.BlockSpec((wn, kvw), lambda n: (n, cv)),
                  pl.BlockSpec((wn, w), lambda n: (n, 0)), pl.BlockSpec((wn, w), lambda n: (n, 0)),
                  pl.BlockSpec((ATTN_KV_HEADS, grp * wn, 2 * wn), lambda n: (0, 0, 0)),
                  pl.BlockSpec(memory_space=pltpu.SMEM)],
        out_specs=[pl.BlockSpec((wn, w), lambda n: (n, 0)), kv_out, kv_out, kv_out, kv_out,
                   pl.BlockSpec((ATTN_KV_HEADS, grp * wn, 2 * wn), lambda n: (0, 0, 0)),
                   pl.BlockSpec((SUBLANES, LANES), lambda n: (0, 0))],
        out_shape=[jax.ShapeDtypeStruct((t, w), BF16)] + [jax.ShapeDtypeStruct((t, kvw), F32)] * 4
        + [jax.ShapeDtypeStruct((ATTN_KV_HEADS, grp * wn, 2 * wn), F32), jax.ShapeDtypeStruct((SUBLANES, LANES), F32)],
        name=name, compiler_params=_cp(("arbitrary",)))(u, u, u, u, u, o, do, bias_g, sinks)


def _kv_combine(cur, prv, name):
    t, kvw = cur.shape

    def body(i, nt, c_ref, p_ref, o_ref):
        nxt = jnp.where(i < nt - 1, p_ref[...], 0.0)
        o_ref[...] = (c_ref[...] + nxt).astype(BF16)

    return _ew(body, name, t, 1, kvw, [(cur, "tile", 0), (prv, "next", 0)], [BF16], tt=WINDOW)[0]


def _gate_b_fwd(o, u, off, name):
    t, w = o.shape
    wb = 512

    def body(i, nt, o_ref, g_ref, p_ref):
        silu, _ = _silu_parts(g_ref[...])
        p_ref[...] = (o_ref[...] * silu).astype(BF16)

    return _ew(body, name, t, w // wb, wb, [(o, "tile", 0), (u, "tile", off["b_g"] // wb)], [BF16])[0]


def _gate_b_bwd(dp, o, u, off, name):
    t, w = o.shape
    wb = 512

    def body(i, nt, dp_ref, o_ref, g_ref, do_ref, dg_ref):
        silu, dsilu = _silu_parts(g_ref[...])
        dp_v = dp_ref[...]
        do_ref[...] = dp_v * silu
        dg_ref[...] = (dp_v * o_ref[...] * dsilu).astype(BF16)

    return _ew(body, name, t, w // wb, wb,
               [(dp, "tile", 0), (o, "tile", 0), (u, "tile", off["b_g"] // wb)], [F32, BF16])


def _shift_down(h, tail, k, rows):
    tt = h.shape[0]
    out = pltpu.roll(h, k, 0)
    for r in range(k):
        out = jnp.where(rows == r, tail[tt - k + r:tt - k + r + 1, :], out)
    return out


def _shift_up(h, head, k, rows):
    tt = h.shape[0]
    out = pltpu.roll(h, tt - k, 0)
    for r in range(k):
        out = jnp.where(rows == tt - k + r, head[r:r + 1, :], out)
    return out


def _conv_fwd(u, conv_w, off, w, name):
    t = u.shape[0]
    wb = 512
    c = lambda nme: off[nme] // wb

    def body(i, nt, cb_ref, cc_ref, ccp_ref, cx_ref, cxp_ref, cg_ref, w_ref, p_ref):
        h = cc_ref[...] * cx_ref[...]
        hp = jnp.where(i > 0, ccp_ref[...] * cxp_ref[...], 0.0)
        rows = lax.broadcasted_iota(jnp.int32, h.shape, 0)
        y = w_ref[0:1, :] * _shift_down(h, hp, 2, rows) + w_ref[1:2, :] * _shift_down(h, hp, 1, rows) + w_ref[2:3, :] * h
        silu, _ = _silu_parts(cg_ref[...])
        p_ref[...] = (cb_ref[...] * y * silu).astype(BF16)

    return _ew(body, name, t, w // wb, wb,
               [(u, "tile", c("c_b")), (u, "tile", c("c_c")), (u, "prev", c("c_c")), (u, "tile", c("c_x")),
                (u, "prev", c("c_x")), (u, "tile", c("c_g")), (conv_w, "row", 0)], [BF16])[0]


def _conv_bwd(dp, u, conv_w, off, w, name):
    t = u.shape[0]
    wb = 512
    c = lambda nme: off[nme] // wb

    def body(i, nt, dp_ref, dpn_ref, cb_ref, cbn_ref, cg_ref, cgn_ref, cc_ref, ccp_ref, cx_ref, cxp_ref, w_ref,
             dcb_ref, dcc_ref, dcx_ref, dcg_ref, acc_ref):
        @pl.when(i == 0)
        def _():
            acc_ref[...] = jnp.zeros_like(acc_ref)

        cc, cx, cb = cc_ref[...], cx_ref[...], cb_ref[...]
        h = cc * cx
        hp = jnp.where(i > 0, ccp_ref[...] * cxp_ref[...], 0.0)
        rows = lax.broadcasted_iota(jnp.int32, h.shape, 0)
        h1 = _shift_down(h, hp, 1, rows)
        h2 = _shift_down(h, hp, 2, rows)
        w0, w1, w2 = w_ref[0:1, :], w_ref[1:2, :], w_ref[2:3, :]
        y = w0 * h2 + w1 * h1 + w2 * h
        silu, dsilu = _silu_parts(cg_ref[...])
        dp_v = dp_ref[...]
        dcg_ref[...] = (dp_v * cb * y * dsilu).astype(BF16)
        dcb_ref[...] = (dp_v * y * silu).astype(BF16)
        dy = dp_v * cb * silu
        silu_n, _ = _silu_parts(cgn_ref[...])
        dyn = jnp.where(i < nt - 1, dpn_ref[...] * cbn_ref[...] * silu_n, 0.0)
        dh = w2 * dy + w1 * _shift_up(dy, dyn, 1, rows) + w0 * _shift_up(dy, dyn, 2, rows)
        dcc_ref[...] = (dh * cx).astype(BF16)
        dcx_ref[...] = (dh * cc).astype(BF16)
        acc_ref[0:1, :] += jnp.sum(dy * h2, axis=0, keepdims=True)
        acc_ref[1:2, :] += jnp.sum(dy * h1, axis=0, keepdims=True)
        acc_ref[2:3, :] += jnp.sum(dy * h, axis=0, keepdims=True)

    return _ew(body, name, t, w // wb, wb,
               [(dp, "tile", 0), (dp, "next", 0), (u, "tile", c("c_b")), (u, "next", c("c_b")),
                (u, "tile", c("c_g")), (u, "next", c("c_g")), (u, "tile", c("c_c")), (u, "prev", c("c_c")),
                (u, "tile", c("c_x")), (u, "prev", c("c_x")), (conv_w, "row", 0)],
               [BF16] * 4, accs=[SUBLANES])


def _merge_fwd(u, ya, yb, yc, off, d, name):
    t = u.shape[0]
    wb = 512
    c = lambda nme: off[nme] // wb

    def body(i, nt, ma_ref, mb_ref, mc_ref, ya_ref, yb_ref, yc_ref, o_ref):
        o_ref[...] = (_sigmoid(ma_ref[...]) * ya_ref[...] + _sigmoid(mb_ref[...]) * yb_ref[...]
                      + _sigmoid(mc_ref[...]) * yc_ref[...]).astype(BF16)

    return _ew(body, name, t, d // wb, wb,
               [(u, "tile", c("m_a")), (u, "tile", c("m_b")), (u, "tile", c("m_c")),
                (ya, "tile", 0), (yb, "tile", 0), (yc, "tile", 0)], [BF16])[0]


def _merge_bwd(dm, u, ya, yb, yc, off, d, name):
    t = u.shape[0]
    wb = 512
    c = lambda nme: off[nme] // wb

    def body(i, nt, dm_ref, ma_ref, mb_ref, mc_ref, ya_ref, yb_ref, yc_ref, da_ref, db_ref, dc_ref, ga_ref, gb_ref, gc_ref):
        dm_v = dm_ref[...]
        for m_ref, y_ref, dy_ref, dg_ref in ((ma_ref, ya_ref, da_ref, ga_ref), (mb_ref, yb_ref, db_ref, gb_ref),
                                             (mc_ref, yc_ref, dc_ref, gc_ref)):
            s = _sigmoid(m_ref[...])
            dy_ref[...] = (dm_v * s).astype(BF16)
            dg_ref[...] = (dm_v * y_ref[...] * s * (1.0 - s)).astype(BF16)

    return _ew(body, name, t, d // wb, wb,
               [(dm, "tile", 0), (u, "tile", c("m_a")), (u, "tile", c("m_b")), (u, "tile", c("m_c")),
                (ya, "tile", 0), (yb, "tile", 0), (yc, "tile", 0)], [BF16] * 6)


def _lower_bounds(lb_param):
    def body(p_ref, o_ref):
        p = p_ref[...]
        e = jnp.exp(p - jnp.max(p, axis=0, keepdims=True))
        soft = e / jnp.sum(e, axis=0, keepdims=True)
        acc = jnp.zeros_like(soft[0:1])
        o_ref[0:1, :] = acc
        for l in range(1, DEPTH):
            acc = acc + soft[l:l + 1]
            o_ref[l:l + 1, :] = acc

    return pl.pallas_call(body, out_shape=jax.ShapeDtypeStruct(lb_param.shape, F32), name="lower_bounds",
                          compiler_params=_cp())(lb_param)


def _lower_bounds_bwd(lb_param, dlower):
    def body(p_ref, d_ref, o_ref):
        p = p_ref[...]
        e = jnp.exp(p - jnp.max(p, axis=0, keepdims=True))
        soft = e / jnp.sum(e, axis=0, keepdims=True)
        dl = d_ref[...]
        ds = [jnp.zeros_like(dl[0:1])]
        for j in range(1, DEPTH):
            acc = dl[j:j + 1]
            for l in range(j + 1, DEPTH):
                acc = acc + dl[l:l + 1]
            ds.append(acc)
        inner = ds[0] * soft[0:1]
        for j in range(1, DEPTH):
            inner = inner + ds[j] * soft[j:j + 1]
        for j in range(DEPTH):
            o_ref[j:j + 1, :] = soft[j:j + 1] * (ds[j] - inner)

    return pl.pallas_call(body, out_shape=jax.ShapeDtypeStruct(lb_param.shape, F32), name="lower_bounds_bwd",
                          compiler_params=_cp())(lb_param, dlower)


def _exchange(arrays, scatter, name, chips=False):
    n_arr = len(arrays)
    n_slot = N_DEV // 2 if chips else N_DEV

    def body(*refs):
        srcs, dsts = refs[:n_arr], refs[n_arr:2 * n_arr]
        send_sems, recv_sems, local_sems = refs[2 * n_arr:]
        me = (2 * lax.axis_index("x") + lax.axis_index("y") if chips
              else 4 * lax.axis_index("x") + 2 * lax.axis_index("y") + lax.axis_index("c"))
        copies = _peer_copies(srcs, dsts, send_sems, recv_sems, scatter, chips)
        for a in range(n_arr):
            copies.append(pltpu.make_async_copy(srcs[a].at[me] if scatter else srcs[a], dsts[a].at[me], local_sems.at[a]))
        for cp in copies:
            cp.start()
        for cp in copies:
            cp.wait()

    out_shape = [jax.ShapeDtypeStruct(a.shape if scatter else (n_slot,) + a.shape, a.dtype) for a in arrays]
    anyspec = pl.BlockSpec(memory_space=pl.ANY)
    res = pl.pallas_call(
        body, in_specs=[anyspec] * n_arr, out_specs=[anyspec] * n_arr, out_shape=out_shape,
        scratch_shapes=[pltpu.SemaphoreType.DMA((n_arr * (n_slot - 1),)), pltpu.SemaphoreType.DMA((n_arr * (n_slot - 1),)),
                        pltpu.SemaphoreType.DMA((n_arr,))],
        name=name)(*arrays)
    return list(res)


def _peer_copies(srcs, lands, send_sems, recv_sems, scatter, chips=False):
    x, y, c = lax.axis_index("x"), lax.axis_index("y"), lax.axis_index("c")
    flips = [k for k in range(1, N_DEV) if not (chips and k & 1)]
    slot = (lambda px, py, pc: 2 * px + py) if chips else (lambda px, py, pc: 4 * px + 2 * py + pc)
    copies = []
    for a in range(len(srcs)):
        for i, k in enumerate(flips):
            px = 1 - x if k & 4 else x
            py = 1 - y if k & 2 else y
            pc = 1 - c if k & 1 else c
            src = srcs[a].at[slot(px, py, pc)] if scatter else srcs[a]
            copies.append(pltpu.make_async_remote_copy(
                src_ref=src, dst_ref=lands[a].at[slot(x, y, c)],
                send_sem=send_sems.at[a * len(flips) + i], recv_sem=recv_sems.at[a * len(flips) + i],
                device_id=(px, py, pc), device_id_type=pl.DeviceIdType.MESH))
    return copies


def _gather_two_level(arrays, name):
    n_arr = len(arrays)
    per = N_DEV - 1

    def body(*refs):
        srcs, outs = refs[:n_arr], refs[n_arr:2 * n_arr]
        send_sems, recv_sems, local_sems = refs[2 * n_arr:]
        x, y, c = lax.axis_index("x"), lax.axis_index("y"), lax.axis_index("c")
        me, sibling = (x, y, c), (x, y, 1 - c)
        chips = [(1 - x, y), (x, 1 - y), (1 - x, 1 - y)]

        def copy(a, k, block, to, src=None):
            dst = outs[a].at[4 * block[0] + 2 * block[1] + block[2]]
            return pltpu.make_async_remote_copy(
                src_ref=dst if src is None else src, dst_ref=dst,
                send_sem=send_sems.at[a * per + k], recv_sem=recv_sems.at[a * per + k],
                device_id=to, device_id_type=pl.DeviceIdType.MESH)

        own, first, passed = [], [], []
        for a in range(n_arr):
            own.append(pltpu.make_async_copy(srcs[a], outs[a].at[4 * x + 2 * y + c], local_sems.at[a]))
            first.append(copy(a, 0, me, sibling, src=srcs[a]))
            first += [copy(a, 1 + j, me, (*chip, c), src=srcs[a]) for j, chip in enumerate(chips)]
        for cp in own + first:
            cp.start()
        for a in range(n_arr):
            for j, chip in enumerate(chips):
                copy(a, 1 + j, (*chip, c), me).wait_recv()
                passed.append(copy(a, 4 + j, (*chip, c), sibling))
                passed[-1].start()
        for a in range(n_arr):
            copy(a, 0, sibling, me).wait_recv()
            for j, chip in enumerate(chips):
                copy(a, 4 + j, (*chip, 1 - c), me).wait_recv()
        for cp in first + passed:
            cp.wait_send()
        for cp in own:
            cp.wait()

    anyspec = pl.BlockSpec(memory_space=pl.ANY)
    res = pl.pallas_call(
        body, in_specs=[anyspec] * n_arr, out_specs=[anyspec] * n_arr,
        out_shape=[jax.ShapeDtypeStruct((N_DEV,) + a.shape, a.dtype) for a in arrays],
        scratch_shapes=[pltpu.SemaphoreType.DMA((n_arr * per,)), pltpu.SemaphoreType.DMA((n_arr * per,)),
                        pltpu.SemaphoreType.DMA((n_arr,))],
        name=name)(*arrays)
    return list(res)


def _sibling_swap(arrays, name):
    n_arr = len(arrays)
    n_chip = N_DEV // 2

    def body(*refs):
        srcs, outs = refs[:n_arr], refs[n_arr:2 * n_arr]
        send_sems, recv_sems = refs[2 * n_arr:]
        x, y, c = lax.axis_index("x"), lax.axis_index("y"), lax.axis_index("c")
        copies = []
        for a in range(n_arr):
            for j in range(n_chip):
                copies.append(pltpu.make_async_remote_copy(
                    src_ref=srcs[a].at[2 * j + 1 - c], dst_ref=outs[a].at[j],
                    send_sem=send_sems.at[a * n_chip + j], recv_sem=recv_sems.at[a * n_chip + j],
                    device_id=(x, y, 1 - c), device_id_type=pl.DeviceIdType.MESH))
        for cp in copies:
            cp.start()
        for cp in copies:
            cp.wait()

    anyspec = pl.BlockSpec(memory_space=pl.ANY)
    res = pl.pallas_call(
        body, in_specs=[anyspec] * n_arr, out_specs=[anyspec] * n_arr,
        out_shape=[jax.ShapeDtypeStruct((n_chip,) + a.shape[1:], a.dtype) for a in arrays],
        scratch_shapes=[pltpu.SemaphoreType.DMA((n_arr * n_chip,)), pltpu.SemaphoreType.DMA((n_arr * n_chip,))],
        name=name)(*arrays)
    return list(res)


def _pair_sum(send, stage, core, name):
    _, r, c = send.shape
    n_chip = stage.shape[0]
    tr = _tile(r, 128)

    def body(core_ref, a_ref, b_ref, o_ref):
        o_ref[...] = a_ref[...] + b_ref[...]

    return pl.pallas_call(
        body,
        grid_spec=pltpu.PrefetchScalarGridSpec(
            num_scalar_prefetch=1, grid=(n_chip, r // tr),
            in_specs=[pl.BlockSpec((1, tr, c), lambda j, i, core_ref: (2 * j + core_ref[0], i, 0)),
                      pl.BlockSpec((1, tr, c), lambda j, i, core_ref: (j, i, 0))],
            out_specs=pl.BlockSpec((1, tr, c), lambda j, i, core_ref: (j, i, 0))),
        out_shape=jax.ShapeDtypeStruct(stage.shape, F32), name=name,
        compiler_params=_cp(("parallel", "parallel")))(core, send, stage)


_HBM_SPEC = pl.BlockSpec(memory_space=pltpu.HBM)
_SEM_SPEC = pl.BlockSpec(memory_space=pltpu.SEMAPHORE)
_ANY_SPEC = pl.BlockSpec(memory_space=pl.ANY)
_DATAFLOW = pltpu.SideEffectType.DATAFLOW_SIDE_EFFECTING


def _exchange_start(arrays, scatter, name, dep=None, chips=False):
    n_arr = len(arrays)
    n_slot = N_DEV // 2 if chips else N_DEV
    n_sem = n_arr * (n_slot - 1)
    me = (2 * lax.axis_index("x") + lax.axis_index("y") if chips
          else 4 * lax.axis_index("x") + 2 * lax.axis_index("y") + lax.axis_index("c"))
    lands = []
    for a in arrays:
        own = lax.dynamic_index_in_dim(a, me, 0, keepdims=False) if scatter else a
        shape = a.shape if scatter else (n_slot,) + a.shape
        lands.append(lax.dynamic_update_index_in_dim(lax.empty(shape, a.dtype), own, me, 0))
    dep_specs, dep_args = _dep_specs(dep)

    def body(*refs):
        srcs, lnds = refs[:n_arr], refs[n_arr:2 * n_arr]
        outs = refs[2 * n_arr + len(dep_args):]
        send_sems, recv_sems, token = outs[0], outs[1], outs[2 + 2 * n_arr]
        for cp in _peer_copies(srcs, lnds, send_sems, recv_sems, scatter, chips):
            cp.start()
        token[...] = jnp.zeros_like(token)

    thru = [pltpu.HBM(a.shape, a.dtype) for a in list(arrays) + lands]
    return pl.pallas_call(
        body, name=name,
        out_shape=(pltpu.SemaphoreType.DMA((n_sem,)), pltpu.SemaphoreType.DMA((n_sem,)), *thru,
                   jax.ShapeDtypeStruct((SUBLANES, LANES), F32)),
        in_specs=[_HBM_SPEC] * (2 * n_arr) + dep_specs,
        out_specs=(_SEM_SPEC, _SEM_SPEC, *[_HBM_SPEC] * (2 * n_arr), pl.BlockSpec(memory_space=pltpu.VMEM)),
        input_output_aliases={i: 2 + i for i in range(2 * n_arr)},
        compiler_params=pltpu.CompilerParams(has_side_effects=_DATAFLOW),
    )(*[pltpu.with_memory_space_constraint(a, pltpu.HBM) for a in list(arrays) + lands], *dep_args)


def _exchange_wait(started, scatter, name, after, chips=False):
    send_sems, recv_sems = started[0], started[1]
    thru = list(started[2:-1])
    n_arr = len(thru) // 2

    def body(*refs):
        srcs, lnds = refs[:n_arr], refs[n_arr:2 * n_arr]
        for cp in _peer_copies(srcs, lnds, refs[2 * n_arr], refs[2 * n_arr + 1], scatter, chips):
            cp.wait_send()
            cp.wait_recv()

    res = pl.pallas_call(
        body, name=name, out_shape=tuple(pltpu.HBM(a.shape, a.dtype) for a in thru),
        in_specs=[_HBM_SPEC] * (2 * n_arr) + [_SEM_SPEC, _SEM_SPEC, _ANY_SPEC],
        out_specs=tuple([_HBM_SPEC] * (2 * n_arr)),
        input_output_aliases={i: i for i in range(2 * n_arr)},
        compiler_params=pltpu.CompilerParams(has_side_effects=_DATAFLOW),
    )(*thru, send_sems, recv_sems, after)
    return list(res[n_arr:])


def _unshard_cols(g, name):
    nd, r, s = g.shape
    tr = _tile(r, 64)

    def body(i_ref, o_ref):
        for p in range(nd):
            o_ref[:, p * s:(p + 1) * s] = i_ref[p]

    return pl.pallas_call(
        body, grid=(r // tr,), in_specs=[pl.BlockSpec((nd, tr, s), lambda i: (0, i, 0))],
        out_specs=pl.BlockSpec((tr, nd * s), lambda i: (i, 0)),
        out_shape=jax.ShapeDtypeStruct((r, nd * s), g.dtype), name=name, compiler_params=_cp(("parallel",)))(g)


def _shard_cols(g, name):
    r, n = g.shape
    s = n // N_DEV
    tr = _tile(r, 64)

    def body(i_ref, o_ref):
        for p in range(N_DEV):
            o_ref[p] = i_ref[:, p * s:(p + 1) * s]

    return pl.pallas_call(
        body, grid=(r // tr,), in_specs=[pl.BlockSpec((tr, n), lambda i: (i, 0))],
        out_specs=pl.BlockSpec((N_DEV, tr, s), lambda i: (0, i, 0)),
        out_shape=jax.ShapeDtypeStruct((N_DEV, r, s), g.dtype), name=name, compiler_params=_cp(("parallel",)))(g)


def _slot_sum(slots, name):
    nd, r, c = slots.shape
    tr = _tile(r, 64)

    def body(s_ref, o_ref):
        acc = s_ref[0]
        for p in range(1, nd):
            acc = acc + s_ref[p]
        o_ref[...] = acc

    return pl.pallas_call(
        body, grid=(r // tr,), in_specs=[pl.BlockSpec((nd, tr, c), lambda i: (0, i, 0))],
        out_specs=pl.BlockSpec((tr, c), lambda i: (i, 0)),
        out_shape=jax.ShapeDtypeStruct((r, c), F32), name=name, compiler_params=_cp(("parallel",)))(slots)


def _adamw(w, g, m, v, name):
    r, c = w.shape
    tr = _tile(r, 256)
    c1 = 1.0 - ADAM_B1 ** ADAM_STEP
    c2 = 1.0 - ADAM_B2 ** ADAM_STEP

    def body(w_ref, g_ref, m_ref, v_ref, d_ref, nm_ref, nv_ref):
        gv = g_ref[...]
        nm = ADAM_B1 * m_ref[...] + (1.0 - ADAM_B1) * gv
        nv = ADAM_B2 * v_ref[...] + (1.0 - ADAM_B2) * (gv * gv)
        nm_ref[...] = nm
        nv_ref[...] = nv
        d_ref[...] = -ADAM_LR * ((nm / c1) / (jnp.sqrt(nv / c2) + ADAM_EPS) + ADAM_WD * w_ref[...])

    spec = pl.BlockSpec((tr, c), lambda i: (i, 0))
    return pl.pallas_call(
        body, grid=(r // tr,), in_specs=[spec] * 4, out_specs=[spec] * 3,
        out_shape=[jax.ShapeDtypeStruct((r, c), F32)] * 3, name=name, compiler_params=_cp(("parallel",)))(w, g, m, v)


def _forward_backward(x, target, weights_hook, grads_hook, lb_param, hgrn_norm_g, attn_sinks, rel_bias, ln_g, ln_b):
    t, d = x.shape
    w = d // 2
    off, n_in = _offsets(d)
    n_heads = w // ATTN_HEAD_DIM
    grp = n_heads // ATTN_KV_HEADS

    lower = _lower_bounds(lb_param)
    bias = _bias_table(rel_bias, n_heads)
    bias_g = bias.reshape(ATTN_KV_HEADS, grp * WINDOW, 2 * WINDOW)

    saved, weights = [], []
    xb = x.astype(BF16)
    for l in range(DEPTH):
        wl, token = weights_hook(l, x)
        weights.append(wl)
        s = {"x": x, "xb": xb}
        u = _mm_nn(xb, wl["w_in"], f"in_proj", dep=token)
        s["u"] = u
        lb_l, gain_l, cw_l = lower[l:l + 1], hgrn_norm_g[l:l + 1], wl["conv_w"]
        o_a, states = _hgrn_fwd(u, lb_l, off, f"hgrn_fwd")
        p_a = _gate_a_fwd(o_a, u, gain_l, off, f"gate_a_fwd")
        o_b = _attn_fwd(u, bias_g, attn_sinks[l], off, w, f"attn_fwd")
        p_b = _gate_b_fwd(o_b, u, off, f"gate_b_fwd")
        p_c = _conv_fwd(u, cw_l, off, w, f"conv_fwd")
        y_a = _mm_nn(p_a, wl["w_proj_hgrn"], f"proj_a", tn=2048)
        y_b = _mm_nn(p_b, wl["w_proj_attn"], f"proj_b", tn=2048)
        y_c = _mm_nn(p_c, wl["w_proj_conv"], f"proj_c", tn=2048)
        merged = _merge_fwd(u, y_a, y_b, y_c, off, d, f"merge_fwd")
        y = _mm_nn(merged, wl["w_out"], f"out_proj", tm=512, tn=2048)
        x, xb, xhat, rstd = _ln_fwd(x, y, ln_g[l:l + 1], ln_b[l:l + 1], f"ln_fwd")
        s.update(o_a=o_a, states=states, p_a=p_a, o_b=o_b, p_b=p_b, p_c=p_c, y_a=y_a, y_b=y_b, y_c=y_c,
                 merged=merged, xhat=xhat, rstd=rstd)
        saved.append(s)

    loss_acc, dx = _loss_head(x, target)

    d_ln, d_lower, d_gain, d_sink, d_conv = [None] * DEPTH, [None] * DEPTH, [None] * DEPTH, [None] * DEPTH, [None] * DEPTH
    dbias_total = None
    for l in reversed(range(DEPTH)):
        wl, s = weights[l], saved[l]
        u = s["u"]
        lb_l, gain_l, cw_l = lower[l:l + 1], hgrn_norm_g[l:l + 1], wl["conv_w"]
        dz, dzb, d_ln[l] = _ln_bwd(dx, s["xhat"], s["rstd"], ln_g[l:l + 1], f"ln_bwd")
        g_out = _mm_tn(s["merged"], dzb, f"g_out", tn=2048)
        dmerged = _mm_nt(dzb, wl["w_out"], f"d_merged", tk=2048)
        dya, dyb, dyc, dma, dmb, dmc = _merge_bwd(dmerged, u, s["y_a"], s["y_b"], s["y_c"], off, d, f"merge_bwd")
        g_pa = _mm_tn(s["p_a"], dya, f"g_proj_a", tn=2048)
        g_pb = _mm_tn(s["p_b"], dyb, f"g_proj_b", tn=2048)
        g_pc = _mm_tn(s["p_c"], dyc, f"g_proj_c", tn=2048)
        dpa = _mm_nt(dya, wl["w_proj_hgrn"], f"d_p_a", tk=2048)
        dpb = _mm_nt(dyb, wl["w_proj_attn"], f"d_p_b", tk=2048)
        dpc = _mm_nt(dyc, wl["w_proj_conv"], f"d_p_c", tk=2048)
        do_a, d_ag, d_gain[l] = _gate_a_bwd(dpa, s["o_a"], u, gain_l, off, f"gate_a_bwd")
        d_aq, d_af, d_ai, d_lower[l] = _hgrn_bwd(u, lb_l, s["states"], do_a, off, f"hgrn_bwd")
        do_b, d_bg = _gate_b_bwd(dpb, s["o_b"], u, off, f"gate_b_bwd")
        d_bq, dkc, dkp, dvc, dvp, dbias_l, d_sink[l] = _attn_bwd(u, s["o_b"], do_b, bias_g, attn_sinks[l], off, w, f"attn_bwd")
        d_bk = _kv_combine(dkc, dkp, f"k_combine")
        d_bv = _kv_combine(dvc, dvp, f"v_combine")
        dbias_total = dbias_l if dbias_total is None else dbias_total + dbias_l
        d_cb, d_cc, d_cx, d_cg, d_conv[l] = _conv_bwd(dpc, u, cw_l, off, w, f"conv_bwd")
        du = jnp.concatenate([d_aq, d_af, d_ai, d_ag, d_bq, d_bk, d_bv, d_bg, d_cb, d_cc, d_cx, d_cg, dma, dmb, dmc], axis=1)
        g_in = _mm_tn(s["xb"], du, f"g_in")
        token = grads_hook(l, {"w_in": g_in, "w_proj_hgrn": g_pa, "w_proj_attn": g_pb, "w_proj_conv": g_pc, "w_out": g_out})
        dx = _mm_nt(du, wl["w_in"], f"d_x", add=dz, add_scale=ALPHA, dep=token)

    d_lower_all = jnp.concatenate([a[0:1] for a in d_lower], axis=0)
    small = {
        "lb_param": _lower_bounds_bwd(lb_param, d_lower_all),
        "hgrn_norm_g": jnp.concatenate([a[0:1] for a in d_gain], axis=0),
        "attn_sinks": jnp.concatenate([a[0:1, :n_heads] for a in d_sink], axis=0),
        "conv_w": jnp.stack([a[0:3] for a in d_conv], axis=0),
        "rel_bias": _bias_grad(dbias_total.reshape(n_heads, WINDOW, 2 * WINDOW), n_heads)[:, :n_heads],
        "ln_g": jnp.concatenate([a[0:1] for a in d_ln], axis=0),
        "ln_b": jnp.concatenate([a[1:2] for a in d_ln], axis=0),
    }
    return loss_acc, dx, small


BIG = ("w_in", "w_proj_hgrn", "w_proj_attn", "w_proj_conv", "w_out")
SMALL = ("lb_param", "hgrn_norm_g", "attn_sinks", "conv_w", "rel_bias", "ln_g", "ln_b")
ORDER = ("w_in", "w_proj_hgrn", "w_proj_attn", "w_proj_conv", "w_out", "lb_param", "hgrn_norm_g", "attn_sinks",
         "conv_w", "rel_bias", "ln_g", "ln_b")


def _pack(parts):
    flat = jnp.concatenate([p.reshape(-1) for p in parts])
    n = flat.shape[0]
    unit = SUBLANES * LANES
    total = -(-n // unit) * unit
    return jnp.pad(flat, (0, total - n)).reshape(total // LANES, LANES)


def _unpack(packed, shapes):
    flat = packed.reshape(-1)
    out, o = [], 0
    for shp in shapes:
        n = int(np.prod(shp))
        out.append(flat[o:o + n].reshape(shp))
        o += n
    return out


def kernel(x, w_in, w_proj_hgrn, w_proj_attn, w_proj_conv, w_out, lb_param, hgrn_norm_g, attn_sinks, conv_w, rel_bias, ln_g, ln_b, loss_target, m_w_in, m_w_proj_hgrn, m_w_proj_attn, m_w_proj_conv, m_w_out, m_lb_param, m_hgrn_norm_g, m_attn_sinks, m_conv_w, m_rel_bias, m_ln_g, m_ln_b, v_w_in, v_w_proj_hgrn, v_w_proj_attn, v_w_proj_conv, v_w_out, v_lb_param, v_hgrn_norm_g, v_attn_sinks, v_conv_w, v_rel_bias, v_ln_g, v_ln_b):
    params = dict(w_in=w_in, w_proj_hgrn=w_proj_hgrn, w_proj_attn=w_proj_attn, w_proj_conv=w_proj_conv, w_out=w_out,
                  lb_param=lb_param, hgrn_norm_g=hgrn_norm_g, attn_sinks=attn_sinks, conv_w=conv_w, rel_bias=rel_bias,
                  ln_g=ln_g, ln_b=ln_b)
    mom_m = dict(w_in=m_w_in, w_proj_hgrn=m_w_proj_hgrn, w_proj_attn=m_w_proj_attn, w_proj_conv=m_w_proj_conv,
                 w_out=m_w_out, lb_param=m_lb_param, hgrn_norm_g=m_hgrn_norm_g, attn_sinks=m_attn_sinks,
                 conv_w=m_conv_w, rel_bias=m_rel_bias, ln_g=m_ln_g, ln_b=m_ln_b)
    mom_v = dict(w_in=v_w_in, w_proj_hgrn=v_w_proj_hgrn, w_proj_attn=v_w_proj_attn, w_proj_conv=v_w_proj_conv,
                 w_out=v_w_out, lb_param=v_lb_param, hgrn_norm_g=v_hgrn_norm_g, attn_sinks=v_attn_sinks,
                 conv_w=v_conv_w, rel_bias=v_rel_bias, ln_g=v_ln_g, ln_b=v_ln_b)
    d = x.shape[-1]
    me = 4 * lax.axis_index("x") + 2 * lax.axis_index("y") + lax.axis_index("c")

    def shards_of(l):
        return [params[n][l].astype(BF16) for n in BIG] + [conv_w[l]]

    gathers = {}

    def weights_hook(l, x_in):
        if l == 0:
            got = _gather_two_level(shards_of(0), "gather_weights_0")
        else:
            got = _exchange_wait(gathers.pop(l), False, f"gather_wait_{l}", x_in)
        token = None
        if l + 1 < DEPTH:
            gathers[l + 1] = _exchange_start(shards_of(l + 1), False, f"gather_start_{l + 1}", dep=got[0])
            token = gathers[l + 1][-1]
        wl = {
            "w_in": _unshard_cols(got[0], "unshard_w_in"),
            "w_proj_hgrn": _unshard_cols(got[1], "unshard_w_proj_hgrn"),
            "w_proj_attn": _unshard_cols(got[2], "unshard_w_proj_attn"),
            "w_proj_conv": _unshard_cols(got[3], "unshard_w_proj_conv"),
            "w_out": got[4].reshape(d, d),
            "conv_w": _unshard_cols(got[5], "unshard_conv_w"),
        }
        return wl, token

    grads = {n: [None] * DEPTH for n in BIG}
    scatters = {}

    def finish_scatter(l, after):
        got = _exchange_wait(scatters.pop(l), True, f"scatter_wait_{l}", after, chips=(l == 0))
        for n, slots in zip(BIG, got):
            grads[n][l] = _slot_sum(slots, f"sum_{n}_chips" if l == 0 else f"sum_{n}")
        return got[0]

    def grads_hook(l, g):
        send = [_shard_cols(g["w_in"], "shard_g_in"), _shard_cols(g["w_proj_hgrn"], "shard_g_proj_a"),
                _shard_cols(g["w_proj_attn"], "shard_g_proj_b"), _shard_cols(g["w_proj_conv"], "shard_g_proj_c"),
                g["w_out"].reshape(N_DEV, d // N_DEV, d)]
        dep = finish_scatter(l + 1, send[0]) if l + 1 < DEPTH else None
        if l == 0:
            core = lax.axis_index("c").astype(jnp.int32).reshape(1)
            staged = _sibling_swap(send, "pair_swap_grads")
            send = [_pair_sum(s, st, core, f"pair_sum_{n}") for n, s, st in zip(BIG, send, staged)]
        scatters[l] = _exchange_start(send, True, f"scatter_start_{l}", dep=dep, chips=(l == 0))
        return scatters[l][-1]

    loss_acc, dx, small = _forward_backward(
        x[0], loss_target[0], weights_hook, grads_hook, lb_param, hgrn_norm_g, attn_sinks, rel_bias, ln_g, ln_b)
    loss = lax.psum(0.5 * jnp.sum(loss_acc[0]) / d, ("x", "y", "c"))
    finish_scatter(0, dx)
    for n in BIG:
        grads[n] = jnp.stack(grads[n], axis=0)

    small_shapes = [small[n].shape for n in SMALL]
    packed = _pack([small[n] for n in SMALL])
    got = _exchange([packed], False, "gather_small_grads")[0]
    summed = _unpack(_slot_sum(got, "sum_small_grads"), small_shapes)
    for n, g in zip(SMALL, summed):
        grads[n] = g
    cs = conv_w.shape[-1]
    grads["conv_w"] = lax.dynamic_slice_in_dim(grads["conv_w"], me * cs, cs, axis=2)

    delta, new_m, new_v = {}, {}, {}
    for n in BIG:
        shp = params[n].shape
        flat = lambda a: a.reshape(-1, shp[-1])
        dl, nm, nv = _adamw(flat(params[n]), flat(grads[n]), flat(mom_m[n]), flat(mom_v[n]), f"adamw_{n}")
        delta[n], new_m[n], new_v[n] = dl.reshape(shp), nm.reshape(shp), nv.reshape(shp)
    shapes = [params[n].shape for n in SMALL]
    res = _adamw(_pack([params[n] for n in SMALL]), _pack([grads[n] for n in SMALL]),
                 _pack([mom_m[n] for n in SMALL]), _pack([mom_v[n] for n in SMALL]), "adamw_small")
    for dst, packed_res in zip((delta, new_m, new_v), res):
        for n, a in zip(SMALL, _unpack(packed_res, shapes)):
            dst[n] = a

    return (loss, dx[None], *[grads[n] for n in ORDER], *[delta[n] for n in ORDER],
            *[new_m[n] for n in ORDER], *[new_v[n] for n in ORDER])
```

```python
import functools
import math

import numpy as np
import jax
import jax.numpy as jnp
from jax import lax
from jax.experimental import pallas as pl
from jax.experimental.pallas import tpu as pltpu

F32 = jnp.float32
BF16 = jnp.bfloat16

N_DEV = 8
DEPTH = 4
HGRN_HEAD_DIM = 128
HGRN_CHUNK = 64
ATTN_HEAD_DIM = 64
ATTN_KV_HEADS = 4
KV_WIDTH = ATTN_KV_HEADS * ATTN_HEAD_DIM
WINDOW = 128
WINDOW_SHIFT = 7
N_BUCKETS = 32
MAX_DISTANCE = 128
ALPHA = (2.0 * DEPTH) ** 0.25
LN_EPS = 1e-5
RMS_EPS = 1e-6
ADAM_LR = 0.001
ADAM_B1 = 0.9
ADAM_B2 = 0.999
ADAM_EPS = 1e-08
ADAM_WD = 0.01
ADAM_STEP = 10

LANES = 128
SUBLANES = 8
VMEM_LIMIT = 56 << 20
NEG_INF = float("-inf")


def _offsets(d_model):
    w = d_model // 2
    sizes = (w, w, w, w, w, KV_WIDTH, KV_WIDTH, w, w, w, w, w, d_model, d_model, d_model)
    names = ("a_q", "a_f", "a_i", "a_g", "b_q", "b_k", "b_v", "b_g", "c_b", "c_c", "c_x", "c_g", "m_a", "m_b", "m_c")
    off, o = {}, 0
    for n, s in zip(names, sizes):
        off[n] = o
        o += s
    return off, o


def _tile(n, pref):
    t = min(pref, n)
    while n % t:
        t //= 2
    return t


def _cp(sem=None, vmem=VMEM_LIMIT):
    return pltpu.CompilerParams(dimension_semantics=sem, vmem_limit_bytes=vmem)


def _sigmoid(x):
    return 1.0 / (1.0 + jnp.exp(-x))


def _dot_nn(a, b):
    return jnp.dot(a, b, preferred_element_type=F32)


def _dot_nt(a, b):
    return lax.dot_general(a, b, (((1,), (1,)), ((), ())), preferred_element_type=F32)


def _dot_tn(a, b):
    return lax.dot_general(a, b, (((0,), (0,)), ((), ())), preferred_element_type=F32)


def _dep_specs(dep):
    return ([], []) if dep is None else ([pl.BlockSpec(memory_space=pl.ANY)], [dep])


def _mm_nn(a, b, name, out_dtype=F32, tm=1024, tn=1536, dep=None):
    m, k = a.shape
    _, n = b.shape
    tm, tn = _tile(m, tm), _tile(n, tn)
    dep_specs, dep_args = _dep_specs(dep)

    def body(a_ref, b_ref, *rest):
        o_ref = rest[-1]
        o_ref[...] = _dot_nn(a_ref[...], b_ref[...]).astype(o_ref.dtype)

    return pl.pallas_call(
        body, grid=(n // tn, m // tm),
        in_specs=[pl.BlockSpec((tm, k), lambda j, i: (i, 0)), pl.BlockSpec((k, tn), lambda j, i: (0, j))] + dep_specs,
        out_specs=pl.BlockSpec((tm, tn), lambda j, i: (i, j)),
        out_shape=jax.ShapeDtypeStruct((m, n), out_dtype), name=name,
        compiler_params=_cp(("parallel", "parallel")))(a, b, *dep_args)


def _mm_nt(a, b, name, tm=1024, tk=1536, add=None, add_scale=1.0, dep=None):
    m, k = a.shape
    n, _ = b.shape
    tm, tk = _tile(m, tm), _tile(k, tk)
    has_add = add is not None
    dep_specs, dep_args = _dep_specs(dep)

    def body(*refs):
        if has_add:
            a_ref, b_ref, add_ref = refs[:3]
        else:
            a_ref, b_ref = refs[:2]
        o_ref = refs[-1]

        @pl.when(pl.program_id(1) == 0)
        def _():
            if has_add:
                o_ref[...] = add_ref[...] * add_scale
            else:
                o_ref[...] = jnp.zeros_like(o_ref)

        o_ref[...] += _dot_nt(a_ref[...], b_ref[...])

    in_specs = [pl.BlockSpec((tm, tk), lambda i, kk: (i, kk)), pl.BlockSpec((n, tk), lambda i, kk: (0, kk))]
    args = [a, b]
    if has_add:
        in_specs.append(pl.BlockSpec((tm, n), lambda i, kk: (i, 0)))
        args.append(add)
    in_specs += dep_specs
    args += dep_args
    return pl.pallas_call(
        body, grid=(m // tm, k // tk), in_specs=in_specs,
        out_specs=pl.BlockSpec((tm, n), lambda i, kk: (i, 0)),
        out_shape=jax.ShapeDtypeStruct((m, n), F32), name=name,
        compiler_params=_cp(("parallel", "arbitrary")))(*args)


def _mm_tn(a, b, name, tt=512, tn=1536):
    t, k = a.shape
    _, n = b.shape
    tt, tn = _tile(t, tt), _tile(n, tn)

    def body(a_ref, b_ref, o_ref):
        @pl.when(pl.program_id(1) == 0)
        def _():
            o_ref[...] = jnp.zeros_like(o_ref)

        o_ref[...] += _dot_tn(a_ref[...], b_ref[...])

    return pl.pallas_call(
        body, grid=(n // tn, t // tt),
        in_specs=[pl.BlockSpec((tt, k), lambda j, s: (s, 0)), pl.BlockSpec((tt, tn), lambda j, s: (s, j))],
        out_specs=pl.BlockSpec((k, tn), lambda j, s: (0, j)),
        out_shape=jax.ShapeDtypeStruct((k, n), F32), name=name,
        compiler_params=_cp(("parallel", "arbitrary")))(a, b)


def _ew(body, name, t, ncol, wb, ins, outs, accs=(), tt=256):
    tt = _tile(t, tt)
    nt = t // tt
    in_specs, args = [], []
    for arr, kind, coff in ins:
        if kind == "tile":
            spec = pl.BlockSpec((tt, wb), lambda j, i, c=coff: (i, c + j))
        elif kind == "prev":
            spec = pl.BlockSpec((tt, wb), lambda j, i, c=coff: (jnp.maximum(i - 1, 0), c + j))
        elif kind == "next":
            spec = pl.BlockSpec((tt, wb), lambda j, i, c=coff: (jnp.minimum(i + 1, nt - 1), c + j))
        else:
            spec = pl.BlockSpec((arr.shape[0], wb), lambda j, i, c=coff: (0, c + j))
        in_specs.append(spec)
        args.append(arr)
    out_specs = [pl.BlockSpec((tt, wb), lambda j, i: (i, j)) for _ in outs]
    out_shape = [jax.ShapeDtypeStruct((t, ncol * wb), d) for d in outs]
    for r in accs:
        out_specs.append(pl.BlockSpec((r, wb), lambda j, i: (0, j)))
        out_shape.append(jax.ShapeDtypeStruct((r, ncol * wb), F32))

    def kern(*refs):
        body(pl.program_id(1), nt, *refs)

    res = pl.pallas_call(
        kern, grid=(ncol, nt), in_specs=in_specs, out_specs=out_specs, out_shape=out_shape, name=name,
        compiler_params=_cp(("parallel", "arbitrary")))(*args)
    return res


def _silu_parts(x):
    s = _sigmoid(x)
    return x * s, s + x * s * (1.0 - s)


def _ln_fwd(x, y, g, b, name):
    t, d = x.shape
    tt = _tile(t, 256)

    def body(x_ref, y_ref, g_ref, b_ref, o_ref, ob_ref, xh_ref, r_ref):
        z = ALPHA * x_ref[...] + y_ref[...]
        mu = jnp.mean(z, axis=1, keepdims=True)
        zc = z - mu
        var = jnp.mean(zc * zc, axis=1, keepdims=True)
        rstd = lax.rsqrt(var + LN_EPS)
        xh = zc * rstd
        o = xh * g_ref[...] + b_ref[...]
        o_ref[...] = o
        ob_ref[...] = o.astype(BF16)
        xh_ref[...] = xh
        r_ref[...] = rstd

    row = pl.BlockSpec((tt, d), lambda i: (i, 0))
    vec = pl.BlockSpec((1, d), lambda i: (0, 0))
    return pl.pallas_call(
        body, grid=(t // tt,), in_specs=[row, row, vec, vec],
        out_specs=[row, row, row, pl.BlockSpec((tt, 1), lambda i: (i, 0))],
        out_shape=[jax.ShapeDtypeStruct((t, d), F32), jax.ShapeDtypeStruct((t, d), BF16),
                   jax.ShapeDtypeStruct((t, d), F32), jax.ShapeDtypeStruct((t, 1), F32)],
        name=name, compiler_params=_cp(("parallel",)))(x, y, g, b)


def _ln_bwd(dout, xhat, rstd, g, name):
    t, d = dout.shape
    tt = _tile(t, 256)

    def body(do_ref, xh_ref, r_ref, g_ref, dz_ref, dzb_ref, acc_ref):
        @pl.when(pl.program_id(0) == 0)
        def _():
            acc_ref[...] = jnp.zeros_like(acc_ref)

        do = do_ref[...]
        xh = xh_ref[...]
        dxh = do * g_ref[...]
        m1 = jnp.mean(dxh, axis=1, keepdims=True)
        m2 = jnp.mean(dxh * xh, axis=1, keepdims=True)
        dz = r_ref[...] * (dxh - m1 - xh * m2)
        dz_ref[...] = dz
        dzb_ref[...] = dz.astype(BF16)
        acc_ref[0:1, :] += jnp.sum(do * xh, axis=0, keepdims=True)
        acc_ref[1:2, :] += jnp.sum(do, axis=0, keepdims=True)

    row = pl.BlockSpec((tt, d), lambda i: (i, 0))
    return pl.pallas_call(
        body, grid=(t // tt,),
        in_specs=[row, row, pl.BlockSpec((tt, 1), lambda i: (i, 0)), pl.BlockSpec((1, d), lambda i: (0, 0))],
        out_specs=[row, row, pl.BlockSpec((SUBLANES, d), lambda i: (0, 0))],
        out_shape=[jax.ShapeDtypeStruct((t, d), F32), jax.ShapeDtypeStruct((t, d), BF16),
                   jax.ShapeDtypeStruct((SUBLANES, d), F32)],
        name=name, compiler_params=_cp(("arbitrary",)))(dout, xhat, rstd, g)


def _loss_head(y, target):
    t, d = y.shape
    tt = _tile(t, 256)

    def body(y_ref, t_ref, acc_ref, dy_ref):
        @pl.when(pl.program_id(0) == 0)
        def _():
            acc_ref[...] = jnp.zeros_like(acc_ref)

        err = y_ref[...] - t_ref[...]
        dy_ref[...] = err * (1.0 / d)
        acc_ref[0:1, :] += jnp.sum(err * err, axis=0, keepdims=True)

    row = pl.BlockSpec((tt, d), lambda i: (i, 0))
    acc, dy = pl.pallas_call(
        body, grid=(t // tt,), in_specs=[row, row],
        out_specs=[pl.BlockSpec((SUBLANES, d), lambda i: (0, 0)), row],
        out_shape=[jax.ShapeDtypeStruct((SUBLANES, d), F32), jax.ShapeDtypeStruct((t, d), F32)],
        name="loss_head", compiler_params=_cp(("arbitrary",)))(y, target)
    return acc, dy


def _tri(lower):
    r = lax.broadcasted_iota(jnp.int32, (HGRN_CHUNK, HGRN_CHUNK), 0)
    c = lax.broadcasted_iota(jnp.int32, (HGRN_CHUNK, HGRN_CHUNK), 1)
    return jnp.where((r >= c) if lower else (r <= c), 1.0, 0.0).astype(BF16)


def _exact_tri_matmul(tri, x):
    hi = x.astype(BF16)
    r1 = x - hi.astype(F32)
    mid = r1.astype(BF16)
    lo = (r1 - mid.astype(F32)).astype(BF16)
    return _dot_nn(tri, hi) + _dot_nn(tri, mid) + _dot_nn(tri, lo)


def _hgrn_gates(q_raw, fl, lb):
    sq = _sigmoid(q_raw)
    qf = q_raw * sq * (HGRN_HEAD_DIM ** -0.5)
    sg = _sigmoid(fl)
    f = lb + (1.0 - lb) * sg
    return qf, sq, sg, f


HGRN_SUB = 16
HGRN_NSUB = HGRN_CHUNK // HGRN_SUB
HGRN_HEADS_PER_STEP = 8
HGRN_HEADS_PER_STEP_BWD = 8


def _diag_rows(r):
    return (r // SUBLANES) * SUBLANES


def _heads(x):
    hd = HGRN_HEAD_DIM
    return [x[:, i * hd:(i + 1) * hd] for i in range(x.shape[1] // hd)]


def _per_head(fn, *xs):
    split = [x if isinstance(x, (list, tuple)) else _heads(x) for x in xs]
    return jnp.concatenate([fn(*hs) for hs in zip(*split)], axis=1)


def _head_lane_sum(x):
    return _per_head(lambda h: jnp.broadcast_to(jnp.sum(h, axis=1, keepdims=True), h.shape), x)


def _hgrn_intra_fwd(qf, k, v, b):
    ch, sub, wd = HGRN_CHUNK, HGRN_SUB, qf.shape[1]
    tl = lax.broadcasted_iota(jnp.int32, (sub, wd), 0)
    blocks = []
    for m in range(HGRN_NSUB):
        rs = slice(m * sub, (m + 1) * sub)
        bm, qm, km, vm = b[rs], qf[rs], k[rs], v[rs]
        parts = {0: jnp.zeros((sub, wd), F32), SUBLANES: jnp.zeros((sub - SUBLANES, wd), F32)}
        for r in range(sub):
            lo = _diag_rows(r)
            e = jnp.exp(jnp.where(tl[lo:] >= r, bm[lo:] - bm[r:r + 1], NEG_INF))
            parts[lo] = parts[lo] + _head_lane_sum(qm[lo:] * e * km[r:r + 1]) * vm[r:r + 1]
        blocks.append(parts[0] + jnp.concatenate([jnp.zeros((SUBLANES, wd), F32), parts[SUBLANES]], axis=0))
    acc = jnp.concatenate(blocks, axis=0)
    for j in range(HGRN_NSUB - 1):
        lo = sub * (j + 1)
        c = b[lo - 1:lo, :]
        qj = (qf[lo:] * jnp.exp(b[lo:] - c)).astype(BF16)
        kj = (k[lo - sub:lo] * jnp.exp(c - b[lo - sub:lo])).astype(BF16)
        vj = v[lo - sub:lo].astype(BF16)
        contrib = _per_head(lambda q_, k_, v_: _dot_nn(_dot_nt(q_, k_).astype(BF16), v_), qj, kj, vj)
        acc = acc + jnp.concatenate([jnp.zeros((lo, wd), F32), contrib], axis=0)
    return acc


def _hgrn_intra_bwd(qf, k, v, b, do_v):
    ch, sub, wd = HGRN_CHUNK, HGRN_SUB, qf.shape[1]
    tl = lax.broadcasted_iota(jnp.int32, (sub, wd), 0)
    dq_blocks, dk_blocks, dv_blocks = [], [], []
    for m in range(HGRN_NSUB):
        rs = slice(m * sub, (m + 1) * sub)
        bm, qm, km, vm, dom = b[rs], qf[rs], k[rs], v[rs], do_v[rs]
        parts = {0: jnp.zeros((sub, wd), F32), SUBLANES: jnp.zeros((sub - SUBLANES, wd), F32)}
        dk_parts = {sub: jnp.zeros((sub, wd), F32), SUBLANES: jnp.zeros((SUBLANES, wd), F32)}
        dv_parts = {sub: jnp.zeros((sub, wd), F32), SUBLANES: jnp.zeros((SUBLANES, wd), F32)}
        for r in range(sub):
            lo = _diag_rows(r)
            b_r, k_r, v_r, q_r, do_r = bm[r:r + 1], km[r:r + 1], vm[r:r + 1], qm[r:r + 1], dom[r:r + 1]
            e = jnp.exp(jnp.where(tl[lo:] >= r, bm[lo:] - b_r, NEG_INF))
            parts[lo] = parts[lo] + _head_lane_sum(dom[lo:] * v_r) * (k_r * e)
            hi = lo + SUBLANES
            e2 = jnp.exp(jnp.where(tl[:hi] <= r, b_r - bm[:hi], NEG_INF))
            qe2 = q_r * e2
            dk_parts[hi] = dk_parts[hi] + _head_lane_sum(vm[:hi] * do_r) * qe2
            dv_parts[hi] = dv_parts[hi] + _head_lane_sum(km[:hi] * qe2) * do_r
        pad = jnp.zeros((SUBLANES, wd), F32)
        dq_blocks.append(parts[0] + jnp.concatenate([pad, parts[SUBLANES]], axis=0))
        dk_blocks.append(dk_parts[sub] + jnp.concatenate([dk_parts[SUBLANES], pad], axis=0))
        dv_blocks.append(dv_parts[sub] + jnp.concatenate([dv_parts[SUBLANES], pad], axis=0))
    dq = jnp.concatenate(dq_blocks, axis=0)
    dk = jnp.concatenate(dk_blocks, axis=0)
    dv = jnp.concatenate(dv_blocks, axis=0)
    do_b, v_b = do_v.astype(BF16), v.astype(BF16)
    dk_off, dv_off = [], []
    for j in range(HGRN_NSUB - 1):
        lo = sub * (j + 1)
        c = b[lo - 1:lo, :]
        eq = jnp.exp(b[lo:] - c)
        ek = jnp.exp(c - b[lo - sub:lo])
        qj = (qf[lo:] * eq).astype(BF16)
        kj = (k[lo - sub:lo] * ek).astype(BF16)
        doj, vj = do_b[lo:], v_b[lo - sub:lo]
        dq_j = _per_head(lambda do_, v_, k_: _dot_nn(_dot_nt(do_, v_).astype(BF16), k_), doj, vj, kj)
        dk_j = _per_head(lambda do_, v_, q_: _dot_nn(_dot_nt(v_, do_).astype(BF16), q_), doj, vj, qj)
        dv_j = _per_head(lambda do_, k_, q_: _dot_nn(_dot_nt(k_, q_).astype(BF16), do_), doj, kj, qj)
        dq = dq + jnp.concatenate([jnp.zeros((lo, wd), F32), dq_j * eq], axis=0)
        dk_off.append(dk_j * ek)
        dv_off.append(dv_j)
    zero = jnp.zeros((sub, wd), F32)
    dk = dk + jnp.concatenate(dk_off + [zero], axis=0)
    dv = dv + jnp.concatenate(dv_off + [zero], axis=0)
    return dq, dk, dv


def _head_rms(o):
    return lax.rsqrt(_head_lane_sum(o * o) * (1.0 / HGRN_HEAD_DIM) + RMS_EPS)


def _hgrn_fwd(u, lb, gain, off, name):
    t = u.shape[0]
    w = lb.shape[1]
    hd, ch, hp = HGRN_HEAD_DIM, HGRN_CHUNK, HGRN_HEADS_PER_STEP
    nh, nc = w // hd, t // ch
    wb = hp * hd
    cq, cf, cv, cg = off["a_q"] // wb, off["a_f"] // wb, off["a_i"] // wb, off["a_g"] // wb

    def body(q_ref, f_ref, v_ref, g_ref, lb_ref, gain_ref, o_ref, st_ref, p_ref, state):
        @pl.when(pl.program_id(1) == 0)
        def _():
            state[...] = jnp.zeros_like(state)

        sts = [state[i] for i in range(hp)]
        qf, _, _, f = _hgrn_gates(q_ref[...], f_ref[...], lb_ref[...])
        k = 1.0 - f
        v = v_ref[...]
        b = _exact_tri_matmul(_tri(True), jnp.log(f))
        inter = _per_head(lambda qa_, st_: _dot_nt(qa_, st_.astype(BF16)), (qf * jnp.exp(b)).astype(BF16), sts)
        o = inter + _hgrn_intra_fwd(qf, k, v, b)
        o_ref[...] = o
        silu, _ = _silu_parts(g_ref[...])
        p_ref[...] = (o * _head_rms(o) * gain_ref[...] * silu).astype(BF16)
        b_end = b[ch - 1:ch, :]
        a_end = _heads(jnp.exp(b_end))
        kd = _heads((k * jnp.exp(b_end - b)).astype(BF16))
        v_b = _heads(v.astype(BF16))
        for i in range(hp):
            st_ref[i, 0] = sts[i]
            state[i] = sts[i] * a_end[i] + _dot_tn(v_b[i], kd[i])

    return pl.pallas_call(
        body, grid=(nh // hp, nc),
        in_specs=[pl.BlockSpec((ch, wb), lambda h, n: (n, cq + h)),
                  pl.BlockSpec((ch, wb), lambda h, n: (n, cf + h)),
                  pl.BlockSpec((ch, wb), lambda h, n: (n, cv + h)),
                  pl.BlockSpec((ch, wb), lambda h, n: (n, cg + h)),
                  pl.BlockSpec((1, wb), lambda h, n: (0, h)),
                  pl.BlockSpec((1, wb), lambda h, n: (0, h))],
        out_specs=[pl.BlockSpec((ch, wb), lambda h, n: (n, h)),
                   pl.BlockSpec((hp, 1, hd, hd), lambda h, n: (h, n, 0, 0)),
                   pl.BlockSpec((ch, wb), lambda h, n: (n, h))],
        out_shape=[jax.ShapeDtypeStruct((t, w), F32), jax.ShapeDtypeStruct((nh, nc, hd, hd), F32),
                   jax.ShapeDtypeStruct((t, w), BF16)],
        scratch_shapes=[pltpu.VMEM((hp, hd, hd), F32)],
        name=name, compiler_params=_cp(("parallel", "arbitrary")))(u, u, u, u, lb, gain)


def _hgrn_bwd(u, lb, gain, states, o, dp, off, name):
    t = u.shape[0]
    w = lb.shape[1]
    hd, ch, hp = HGRN_HEAD_DIM, HGRN_CHUNK, HGRN_HEADS_PER_STEP_BWD
    nh, nc = w // hd, t // ch
    wb = hp * hd
    cq, cf, cv, cg = off["a_q"] // wb, off["a_f"] // wb, off["a_i"] // wb, off["a_g"] // wb

    def body(q_ref, f_ref, v_ref, g_ref, o_ref, dp_ref, st_ref, lb_ref, gain_ref,
             dq_ref, df_ref, dv_ref, dg_ref, dlb_ref, dstate):
        @pl.when(pl.program_id(1) == 0)
        def _():
            dstate[...] = jnp.zeros_like(dstate)
            dlb_ref[...] = jnp.zeros_like(dlb_ref)

        silu, dsilu = _silu_parts(g_ref[...])
        o_v, dp_v, gain_row = o_ref[...], dp_ref[...], gain_ref[...]
        rms = _head_rms(o_v)
        nrm = o_v * rms
        dg_ref[...] = (dp_v * nrm * gain_row * dsilu).astype(BF16)
        dlb_ref[1:2, :] += jnp.sum(dp_v * nrm * silu, axis=0, keepdims=True)
        dn = dp_v * gain_row * silu
        do_v = rms * (dn - nrm * (_head_lane_sum(dn * nrm) * (1.0 / HGRN_HEAD_DIM)))

        rows = lax.broadcasted_iota(jnp.int32, (ch, wb), 0)
        lb_row = lb_ref[...]
        q_raw = q_ref[...]
        qf, sq, sg, f = _hgrn_gates(q_raw, f_ref[...], lb_row)
        k = 1.0 - f
        b = _exact_tri_matmul(_tri(True), jnp.log(f))
        a = jnp.exp(b)
        b_end = b[ch - 1:ch, :]
        a_end = jnp.exp(b_end)
        to_end = jnp.exp(b_end - b)
        v = v_ref[...]
        st0 = [st_ref[i, 0] for i in range(hp)]
        ds = [dstate[i] for i in range(hp)]
        st0_b = [s_.astype(BF16) for s_ in st0]
        ds_b = [s_.astype(BF16) for s_ in ds]
        do_b, v_b, kd_b, qa_b = do_v.astype(BF16), v.astype(BF16), (k * to_end).astype(BF16), (qf * a).astype(BF16)

        dq_inter = a * _per_head(_dot_nn, do_b, st0_b)
        dk_end = to_end * _per_head(_dot_nn, v_b, ds_b)
        dv_end = _per_head(_dot_nt, kd_b, ds_b)
        a_end_h = _heads(a_end)
        st_end = [st0[i] * a_end_h[i] + _dot_tn(_heads(v_b)[i], _heads(kd_b)[i]) for i in range(hp)]
        db_end = jnp.concatenate([jnp.sum(ds[i] * st_end[i], axis=0, keepdims=True) for i in range(hp)], axis=1)
        ds_new = [ds[i] * a_end_h[i] + _dot_tn(_heads(do_b)[i], _heads(qa_b)[i]) for i in range(hp)]

        dq_intra, dk_intra, dv_intra = _hgrn_intra_bwd(qf, k, v, b, do_v)
        dqf = dq_inter + dq_intra
        dk = dk_end + dk_intra
        dv = dv_end + dv_intra
        db = qf * dqf - k * dk
        db = db + jnp.where(rows == ch - 1, db_end, 0.0)
        dg = _exact_tri_matmul(_tri(False), db)
        df = dg / f - dk
        for i in range(hp):
            dstate[i] = ds_new[i]
        dq_ref[...] = (dqf * (HGRN_HEAD_DIM ** -0.5) * (sq + q_raw * sq * (1.0 - sq))).astype(BF16)
        df_ref[...] = (df * (1.0 - lb_row) * sg * (1.0 - sg)).astype(BF16)
        dv_ref[...] = dv.astype(BF16)
        dlb_ref[0:1, :] += jnp.sum(df * (1.0 - sg), axis=0, keepdims=True)

    rev = lambda n: nc - 1 - n
    tile = lambda c: pl.BlockSpec((ch, wb), lambda h, n, c=c: (rev(n), c + h))
    return pl.pallas_call(
        body, grid=(nh // hp, nc),
        in_specs=[tile(cq), tile(cf), tile(cv), tile(cg), tile(0), tile(0),
                  pl.BlockSpec((hp, 1, hd, hd), lambda h, n: (h, rev(n), 0, 0)),
                  pl.BlockSpec((1, wb), lambda h, n: (0, h)), pl.BlockSpec((1, wb), lambda h, n: (0, h))],
        out_specs=[tile(0), tile(0), tile(0), tile(0), pl.BlockSpec((SUBLANES, wb), lambda h, n: (0, h))],
        out_shape=[jax.ShapeDtypeStruct((t, w), BF16)] * 4 + [jax.ShapeDtypeStruct((SUBLANES, w), F32)],
        scratch_shapes=[pltpu.VMEM((hp, hd, hd), F32)],
        name=name, compiler_params=_cp(("parallel", "arbitrary")))(u, u, u, u, o, dp, states, lb, gain)


def _bucket_map():
    i = np.arange(WINDOW)[:, None]
    j = np.arange(2 * WINDOW)[None, :]
    dist = np.clip(WINDOW + i - j, 0, WINDOW - 1)
    max_exact = N_BUCKETS // 2
    logd = (np.log(np.maximum(dist, 1).astype(np.float32) / max_exact) / math.log(MAX_DISTANCE / max_exact))
    large = np.minimum(max_exact + (logd.astype(np.float32) * (N_BUCKETS - max_exact)).astype(np.int32), N_BUCKETS - 1)
    return np.where(dist < max_exact, dist, large).astype(np.int32)


def _bias_table(rel_bias, n_heads):
    bucket = jnp.asarray(_bucket_map())

    def body(rb_ref, bk_ref, o_ref):
        bk = bk_ref[...]
        for h in range(n_heads):
            def step(bi, acc):
                return jnp.where(bk == bi, rb_ref[bi, h], acc)
            o_ref[h] = lax.fori_loop(0, N_BUCKETS, step, jnp.zeros((WINDOW, 2 * WINDOW), F32))

    return pl.pallas_call(
        body, in_specs=[pl.BlockSpec(memory_space=pltpu.SMEM), pl.BlockSpec(memory_space=pltpu.VMEM)],
        out_specs=pl.BlockSpec(memory_space=pltpu.VMEM),
        out_shape=jax.ShapeDtypeStruct((n_heads, WINDOW, 2 * WINDOW), F32), name="bias_table",
        compiler_params=_cp())(rel_bias, bucket)


def _bias_grad(dbias, n_heads):
    bucket = jnp.asarray(_bucket_map())

    def body(db_ref, bk_ref, o_ref):
        bk = bk_ref[...]
        lane = lax.broadcasted_iota(jnp.int32, (1, LANES), 1)

        def step(bi, carry):
            row = jnp.zeros((1, LANES), F32)
            for h in range(n_heads):
                val = jnp.sum(jnp.where(bk == bi, db_ref[h], 0.0))
                row = jnp.where(lane == h, val, row)
            o_ref[pl.ds(bi, 1), :] = row
            return carry

        lax.fori_loop(0, N_BUCKETS, step, 0)

    return pl.pallas_call(
        body, in_specs=[pl.BlockSpec(memory_space=pltpu.VMEM), pl.BlockSpec(memory_space=pltpu.VMEM)],
        out_specs=pl.BlockSpec(memory_space=pltpu.VMEM),
        out_shape=jax.ShapeDtypeStruct((N_BUCKETS, LANES), F32), name="bias_grad",
        compiler_params=_cp())(dbias, bucket)


def _attn_probs(n, q_ref, kp_ref, kc_ref, bias_ref, sink_ref, hh, grp):
    ad, wn = ATTN_HEAD_DIM, WINDOW
    ksl = slice(hh * ad, (hh + 1) * ad)
    kw = jnp.concatenate([kp_ref[:, ksl], kc_ref[:, ksl]], axis=0).astype(BF16)
    qs = jnp.concatenate([q_ref[:, (hh * grp + g) * ad:(hh * grp + g + 1) * ad] for g in range(grp)], axis=0).astype(BF16)
    s = _dot_nt(qs, kw) * (ad ** -0.5) + bias_ref[hh]
    r = lax.broadcasted_iota(jnp.int32, (grp * wn, 2 * wn), 0)
    j = lax.broadcasted_iota(jnp.int32, (grp * wn, 2 * wn), 1)
    i = r & (wn - 1)
    valid = ((j >= wn) & (j - wn <= i)) | ((j < wn) & (j > i) & (n > 0))
    s = jnp.where(valid, s, NEG_INF)
    rr = lax.broadcasted_iota(jnp.int32, (grp * wn, 1), 0) >> WINDOW_SHIFT
    sink = jnp.zeros((grp * wn, 1), F32)
    for g in range(grp):
        sink = jnp.where(rr == g, sink_ref[hh * grp + g], sink)
    m = jnp.maximum(jnp.max(s, axis=1, keepdims=True), sink)
    p = jnp.exp(s - m)
    es = jnp.exp(sink - m)
    inv = 1.0 / (jnp.sum(p, axis=1, keepdims=True) + es)
    return qs, kw, p * inv, es * inv


GATE_BLOCK = 512


def _attn_fwd(u, bias_g, sinks, off, w, name):
    t = u.shape[0]
    wn, ad, kvw, gb = WINDOW, ATTN_HEAD_DIM, KV_WIDTH, GATE_BLOCK
    grp = (w // ad) // ATTN_KV_HEADS
    nb = t // wn
    n_gb = w // gb
    cq, ck, cv, cg = off["b_q"] // w, off["b_k"] // kvw, off["b_v"] // kvw, off["b_g"] // gb

    def body(q_ref, kp_ref, kc_ref, vp_ref, vc_ref, bias_ref, sink_ref, *rest):
        g_refs, (o_ref, p_ref) = rest[:n_gb], rest[n_gb:]
        n = pl.program_id(0)
        for hh in range(ATTN_KV_HEADS):
            _, _, p, _ = _attn_probs(n, q_ref, kp_ref, kc_ref, bias_ref, sink_ref, hh, grp)
            ksl = slice(hh * ad, (hh + 1) * ad)
            vw = jnp.concatenate([vp_ref[:, ksl], vc_ref[:, ksl]], axis=0).astype(BF16)
            o = _dot_nn(p.astype(BF16), vw)
            for g in range(grp):
                o_ref[:, (hh * grp + g) * ad:(hh * grp + g + 1) * ad] = o[g * wn:(g + 1) * wn]
        for i in range(n_gb):
            sl = slice(i * gb, (i + 1) * gb)
            silu, _ = _silu_parts(g_refs[i][...])
            p_ref[:, sl] = (o_ref[:, sl] * silu).astype(BF16)

    prev = lambda n: jnp.maximum(n - 1, 0)
    row = pl.BlockSpec((wn, w), lambda n: (n, 0))
    return pl.pallas_call(
        body, grid=(nb,),
        in_specs=[pl.BlockSpec((wn, w), lambda n: (n, cq)),
                  pl.BlockSpec((wn, kvw), lambda n: (prev(n), ck)), pl.BlockSpec((wn, kvw), lambda n: (n, ck)),
                  pl.BlockSpec((wn, kvw), lambda n: (prev(n), cv)), pl.BlockSpec((wn, kvw), lambda n: (n, cv)),
                  pl.BlockSpec((ATTN_KV_HEADS, grp * wn, 2 * wn), lambda n: (0, 0, 0)),
                  pl.BlockSpec(memory_space=pltpu.SMEM)]
        + [pl.BlockSpec((wn, gb), lambda n, i=i: (n, cg + i)) for i in range(n_gb)],
        out_specs=[row, row],
        out_shape=[jax.ShapeDtypeStruct((t, w), F32), jax.ShapeDtypeStruct((t, w), BF16)], name=name,
        compiler_params=_cp(("parallel",)))(u, u, u, u, u, bias_g, sinks, *([u] * n_gb))


def _attn_bwd(u, o, dp, bias_g, sinks, off, w, name):
    t = u.shape[0]
    wn, ad, kvw, gb = WINDOW, ATTN_HEAD_DIM, KV_WIDTH, GATE_BLOCK
    grp = (w // ad) // ATTN_KV_HEADS
    nb = t // wn
    n_gb = w // gb
    cq, ck, cv, cg = off["b_q"] // w, off["b_k"] // kvw, off["b_v"] // kvw, off["b_g"] // gb

    def body(q_ref, kp_ref, kc_ref, vp_ref, vc_ref, o_ref, dp_ref, bias_ref, sink_ref, *rest):
        g_refs = rest[:n_gb]
        dq_ref, dkc_ref, dkp_ref, dvc_ref, dvp_ref, dbias_ref, dsink_ref, dg_ref, do_ref = rest[n_gb:]
        n = pl.program_id(0)

        @pl.when(n == 0)
        def _():
            dbias_ref[...] = jnp.zeros_like(dbias_ref)
            dsink_ref[...] = jnp.zeros_like(dsink_ref)

        for i in range(n_gb):
            sl = slice(i * gb, (i + 1) * gb)
            silu, dsilu = _silu_parts(g_refs[i][...])
            dp_v = dp_ref[:, sl]
            do_ref[:, sl] = dp_v * silu
            dg_ref[:, sl] = (dp_v * o_ref[:, sl] * dsilu).astype(BF16)

        lane = lax.broadcasted_iota(jnp.int32, (1, LANES), 1)
        rr = lax.broadcasted_iota(jnp.int32, (grp * wn, 1), 0) >> WINDOW_SHIFT
        dsink_row = jnp.zeros((1, LANES), F32)
        for hh in range(ATTN_KV_HEADS):
            qs, kw, p, psink = _attn_probs(n, q_ref, kp_ref, kc_ref, bias_ref, sink_ref, hh, grp)
            ksl = slice(hh * ad, (hh + 1) * ad)
            vw = jnp.concatenate([vp_ref[:, ksl], vc_ref[:, ksl]], axis=0).astype(BF16)
            hs = [slice((hh * grp + g) * ad, (hh * grp + g + 1) * ad) for g in range(grp)]
            dos = jnp.concatenate([do_ref[:, sl] for sl in hs], axis=0)
            os_ = jnp.concatenate([o_ref[:, sl] for sl in hs], axis=0)
            delta = jnp.sum(dos * os_, axis=1, keepdims=True)
            dos_b = dos.astype(BF16)
            dp = _dot_nt(dos_b, vw)
            ds = p * (dp - delta)
            dbias_ref[hh] += ds
            sd = psink * delta
            for g in range(grp):
                val = -jnp.sum(jnp.where(rr == g, sd, 0.0))
                dsink_row = jnp.where(lane == hh * grp + g, val, dsink_row)
            ds_b = (ds * (ad ** -0.5)).astype(BF16)
            dq = _dot_nn(ds_b, kw)
            for g in range(grp):
                dq_ref[:, hs[g]] = dq[g * wn:(g + 1) * wn].astype(BF16)
            dkw = _dot_tn(ds_b, qs)
            dvw = _dot_tn(p.astype(BF16), dos_b)
            dkp_ref[:, ksl] = dkw[:wn]
            dkc_ref[:, ksl] = dkw[wn:]
            dvp_ref[:, ksl] = dvw[:wn]
            dvc_ref[:, ksl] = dvw[wn:]
        dsink_ref[0:1, :] += dsink_row

    prev = lambda n: jnp.maximum(n - 1, 0)
    kv_out = pl.BlockSpec((wn, kvw), lambda n: (n, 0))
    row = pl.BlockSpec((wn, w), lambda n: (n, 0))
    return pl.pallas_call(
        body, grid=(nb,),
        in_specs=[pl.BlockSpec((wn, w), lambda n: (n, cq)),
                  pl.BlockSpec((wn, kvw), lambda n: (prev(n), ck)), pl.BlockSpec((wn, kvw), lambda n: (n, ck)),
                  pl.BlockSpec((wn, kvw), lambda n: (prev(n), cv)), pl.BlockSpec((wn, kvw), lambda n: (n, cv)),
                  row, row,
                  pl.BlockSpec((ATTN_KV_HEADS, grp * wn, 2 * wn), lambda n: (0, 0, 0)),
                  pl.BlockSpec(memory_space=pltpu.SMEM)]
        + [pl.BlockSpec((wn, gb), lambda n, i=i: (n, cg + i)) for i in range(n_gb)],
        out_specs=[row, kv_out, kv_out, kv_out, kv_out,
                   pl.BlockSpec((ATTN_KV_HEADS, grp * wn, 2 * wn), lambda n: (0, 0, 0)),
                   pl.BlockSpec((SUBLANES, LANES), lambda n: (0, 0)), row],
        out_shape=[jax.ShapeDtypeStruct((t, w), BF16)] + [jax.ShapeDtypeStruct((t, kvw), F32)] * 4
        + [jax.ShapeDtypeStruct((ATTN_KV_HEADS, grp * wn, 2 * wn), F32), jax.ShapeDtypeStruct((SUBLANES, LANES), F32),
           jax.ShapeDtypeStruct((t, w), BF16)],
        scratch_shapes=[pltpu.VMEM((wn, w), F32)],
        name=name, compiler_params=_cp(("arbitrary",)))(u, u, u, u, u, o, dp, bias_g, sinks, *([u] * n_gb))


def _kv_combine(cur, prv, name):
    t, kvw = cur.shape

    def body(i, nt, c_ref, p_ref, o_ref):
        nxt = jnp.where(i < nt - 1, p_ref[...], 0.0)
        o_ref[...] = (c_ref[...] + nxt).astype(BF16)

    return _ew(body, name, t, 1, kvw, [(cur, "tile", 0), (prv, "next", 0)], [BF16], tt=WINDOW)[0]


def _shift_down(h, tail, k, rows):
    tt = h.shape[0]
    out = pltpu.roll(h, k, 0)
    for r in range(k):
        out = jnp.where(rows == r, tail[tt - k + r:tt - k + r + 1, :], out)
    return out


def _shift_up(h, head, k, rows):
    tt = h.shape[0]
    out = pltpu.roll(h, tt - k, 0)
    for r in range(k):
        out = jnp.where(rows == tt - k + r, head[r:r + 1, :], out)
    return out


def _conv_fwd(u, conv_w, off, w, name):
    t = u.shape[0]
    wb = 512
    c = lambda nme: off[nme] // wb

    def body(i, nt, cb_ref, cc_ref, ccp_ref, cx_ref, cxp_ref, cg_ref, w_ref, p_ref):
        h = cc_ref[...] * cx_ref[...]
        hp = jnp.where(i > 0, ccp_ref[...] * cxp_ref[...], 0.0)
        rows = lax.broadcasted_iota(jnp.int32, h.shape, 0)
        y = w_ref[0:1, :] * _shift_down(h, hp, 2, rows) + w_ref[1:2, :] * _shift_down(h, hp, 1, rows) + w_ref[2:3, :] * h
        silu, _ = _silu_parts(cg_ref[...])
        p_ref[...] = (cb_ref[...] * y * silu).astype(BF16)

    return _ew(body, name, t, w // wb, wb,
               [(u, "tile", c("c_b")), (u, "tile", c("c_c")), (u, "prev", c("c_c")), (u, "tile", c("c_x")),
                (u, "prev", c("c_x")), (u, "tile", c("c_g")), (conv_w, "row", 0)], [BF16])[0]


def _conv_bwd(dp, u, conv_w, off, w, name):
    t = u.shape[0]
    wb = 512
    c = lambda nme: off[nme] // wb

    def body(i, nt, dp_ref, dpn_ref, cb_ref, cbn_ref, cg_ref, cgn_ref, cc_ref, ccp_ref, cx_ref, cxp_ref, w_ref,
             dcb_ref, dcc_ref, dcx_ref, dcg_ref, acc_ref):
        @pl.when(i == 0)
        def _():
            acc_ref[...] = jnp.zeros_like(acc_ref)

        cc, cx, cb = cc_ref[...], cx_ref[...], cb_ref[...]
        h = cc * cx
        hp = jnp.where(i > 0, ccp_ref[...] * cxp_ref[...], 0.0)
        rows = lax.broadcasted_iota(jnp.int32, h.shape, 0)
        h1 = _shift_down(h, hp, 1, rows)
        h2 = _shift_down(h, hp, 2, rows)
        w0, w1, w2 = w_ref[0:1, :], w_ref[1:2, :], w_ref[2:3, :]
        y = w0 * h2 + w1 * h1 + w2 * h
        silu, dsilu = _silu_parts(cg_ref[...])
        dp_v = dp_ref[...]
        dcg_ref[...] = (dp_v * cb * y * dsilu).astype(BF16)
        dcb_ref[...] = (dp_v * y * silu).astype(BF16)
        dy = dp_v * cb * silu
        silu_n, _ = _silu_parts(cgn_ref[...])
        dyn = jnp.where(i < nt - 1, dpn_ref[...] * cbn_ref[...] * silu_n, 0.0)
        dh = w2 * dy + w1 * _shift_up(dy, dyn, 1, rows) + w0 * _shift_up(dy, dyn, 2, rows)
        dcc_ref[...] = (dh * cx).astype(BF16)
        dcx_ref[...] = (dh * cc).astype(BF16)
        acc_ref[0:1, :] += jnp.sum(dy * h2, axis=0, keepdims=True)
        acc_ref[1:2, :] += jnp.sum(dy * h1, axis=0, keepdims=True)
        acc_ref[2:3, :] += jnp.sum(dy * h, axis=0, keepdims=True)

    return _ew(body, name, t, w // wb, wb,
               [(dp, "tile", 0), (dp, "next", 0), (u, "tile", c("c_b")), (u, "next", c("c_b")),
                (u, "tile", c("c_g")), (u, "next", c("c_g")), (u, "tile", c("c_c")), (u, "prev", c("c_c")),
                (u, "tile", c("c_x")), (u, "prev", c("c_x")), (conv_w, "row", 0)],
               [BF16] * 4, accs=[SUBLANES])


def _merge_fwd(u, ya, yb, yc, off, d, name):
    t = u.shape[0]
    wb = 512
    c = lambda nme: off[nme] // wb

    def body(i, nt, ma_ref, mb_ref, mc_ref, ya_ref, yb_ref, yc_ref, o_ref):
        o_ref[...] = (_sigmoid(ma_ref[...]) * ya_ref[...] + _sigmoid(mb_ref[...]) * yb_ref[...]
                      + _sigmoid(mc_ref[...]) * yc_ref[...]).astype(BF16)

    return _ew(body, name, t, d // wb, wb,
               [(u, "tile", c("m_a")), (u, "tile", c("m_b")), (u, "tile", c("m_c")),
                (ya, "tile", 0), (yb, "tile", 0), (yc, "tile", 0)], [BF16])[0]


def _merge_bwd(dm, u, ya, yb, yc, off, d, name):
    t = u.shape[0]
    wb = 512
    c = lambda nme: off[nme] // wb

    def body(i, nt, dm_ref, ma_ref, mb_ref, mc_ref, ya_ref, yb_ref, yc_ref, da_ref, db_ref, dc_ref, ga_ref, gb_ref, gc_ref):
        dm_v = dm_ref[...]
        for m_ref, y_ref, dy_ref, dg_ref in ((ma_ref, ya_ref, da_ref, ga_ref), (mb_ref, yb_ref, db_ref, gb_ref),
                                             (mc_ref, yc_ref, dc_ref, gc_ref)):
            s = _sigmoid(m_ref[...])
            dy_ref[...] = (dm_v * s).astype(BF16)
            dg_ref[...] = (dm_v * y_ref[...] * s * (1.0 - s)).astype(BF16)

    return _ew(body, name, t, d // wb, wb,
               [(dm, "tile", 0), (u, "tile", c("m_a")), (u, "tile", c("m_b")), (u, "tile", c("m_c")),
                (ya, "tile", 0), (yb, "tile", 0), (yc, "tile", 0)], [BF16] * 6)


def _lower_bounds(lb_param):
    def body(p_ref, o_ref):
        p = p_ref[...]
        e = jnp.exp(p - jnp.max(p, axis=0, keepdims=True))
        soft = e / jnp.sum(e, axis=0, keepdims=True)
        acc = jnp.zeros_like(soft[0:1])
        o_ref[0:1, :] = acc
        for l in range(1, DEPTH):
            acc = acc + soft[l:l + 1]
            o_ref[l:l + 1, :] = acc

    return pl.pallas_call(body, out_shape=jax.ShapeDtypeStruct(lb_param.shape, F32), name="lower_bounds",
                          compiler_params=_cp())(lb_param)


def _lower_bounds_bwd(lb_param, dlower):
    def body(p_ref, d_ref, o_ref):
        p = p_ref[...]
        e = jnp.exp(p - jnp.max(p, axis=0, keepdims=True))
        soft = e / jnp.sum(e, axis=0, keepdims=True)
        dl = d_ref[...]
        ds = [jnp.zeros_like(dl[0:1])]
        for j in range(1, DEPTH):
            acc = dl[j:j + 1]
            for l in range(j + 1, DEPTH):
                acc = acc + dl[l:l + 1]
            ds.append(acc)
        inner = ds[0] * soft[0:1]
        for j in range(1, DEPTH):
            inner = inner + ds[j] * soft[j:j + 1]
        for j in range(DEPTH):
            o_ref[j:j + 1, :] = soft[j:j + 1] * (ds[j] - inner)

    return pl.pallas_call(body, out_shape=jax.ShapeDtypeStruct(lb_param.shape, F32), name="lower_bounds_bwd",
                          compiler_params=_cp())(lb_param, dlower)


def _exchange(arrays, scatter, name, chips=False):
    n_arr = len(arrays)
    n_slot = N_DEV // 2 if chips else N_DEV

    def body(*refs):
        srcs, dsts = refs[:n_arr], refs[n_arr:2 * n_arr]
        send_sems, recv_sems, local_sems = refs[2 * n_arr:]
        me = (2 * lax.axis_index("x") + lax.axis_index("y") if chips
              else 4 * lax.axis_index("x") + 2 * lax.axis_index("y") + lax.axis_index("c"))
        copies = _peer_copies(srcs, dsts, send_sems, recv_sems, scatter, chips)
        for a in range(n_arr):
            copies.append(pltpu.make_async_copy(srcs[a].at[me] if scatter else srcs[a], dsts[a].at[me], local_sems.at[a]))
        for cp in copies:
            cp.start()
        for cp in copies:
            cp.wait()

    out_shape = [jax.ShapeDtypeStruct(a.shape if scatter else (n_slot,) + a.shape, a.dtype) for a in arrays]
    anyspec = pl.BlockSpec(memory_space=pl.ANY)
    res = pl.pallas_call(
        body, in_specs=[anyspec] * n_arr, out_specs=[anyspec] * n_arr, out_shape=out_shape,
        scratch_shapes=[pltpu.SemaphoreType.DMA((n_arr * (n_slot - 1),)), pltpu.SemaphoreType.DMA((n_arr * (n_slot - 1),)),
                        pltpu.SemaphoreType.DMA((n_arr,))],
        name=name)(*arrays)
    return list(res)


def _peer_copies(srcs, lands, send_sems, recv_sems, scatter, chips=False):
    x, y, c = lax.axis_index("x"), lax.axis_index("y"), lax.axis_index("c")
    flips = [k for k in range(1, N_DEV) if not (chips and k & 1)]
    slot = (lambda px, py, pc: 2 * px + py) if chips else (lambda px, py, pc: 4 * px + 2 * py + pc)
    copies = []
    for a in range(len(srcs)):
        for i, k in enumerate(flips):
            px = 1 - x if k & 4 else x
            py = 1 - y if k & 2 else y
            pc = 1 - c if k & 1 else c
            src = srcs[a].at[slot(px, py, pc)] if scatter else srcs[a]
            copies.append(pltpu.make_async_remote_copy(
                src_ref=src, dst_ref=lands[a].at[slot(x, y, c)],
                send_sem=send_sems.at[a * len(flips) + i], recv_sem=recv_sems.at[a * len(flips) + i],
                device_id=(px, py, pc), device_id_type=pl.DeviceIdType.MESH))
    return copies


def _gather_two_level(arrays, name):
    n_arr = len(arrays)
    per = N_DEV - 1

    def body(*refs):
        srcs, outs = refs[:n_arr], refs[n_arr:2 * n_arr]
        send_sems, recv_sems, local_sems = refs[2 * n_arr:]
        x, y, c = lax.axis_index("x"), lax.axis_index("y"), lax.axis_index("c")
        me, sibling = (x, y, c), (x, y, 1 - c)
        chips = [(1 - x, y), (x, 1 - y), (1 - x, 1 - y)]

        def copy(a, k, block, to, src=None):
            dst = outs[a].at[4 * block[0] + 2 * block[1] + block[2]]
            return pltpu.make_async_remote_copy(
                src_ref=dst if src is None else src, dst_ref=dst,
                send_sem=send_sems.at[a * per + k], recv_sem=recv_sems.at[a * per + k],
                device_id=to, device_id_type=pl.DeviceIdType.MESH)

        own, first, passed = [], [], []
        for a in range(n_arr):
            own.append(pltpu.make_async_copy(srcs[a], outs[a].at[4 * x + 2 * y + c], local_sems.at[a]))
            first.append(copy(a, 0, me, sibling, src=srcs[a]))
            first += [copy(a, 1 + j, me, (*chip, c), src=srcs[a]) for j, chip in enumerate(chips)]
        for cp in own + first:
            cp.start()
        for a in range(n_arr):
            for j, chip in enumerate(chips):
                copy(a, 1 + j, (*chip, c), me).wait_recv()
                passed.append(copy(a, 4 + j, (*chip, c), sibling))
                passed[-1].start()
        for a in range(n_arr):
            copy(a, 0, sibling, me).wait_recv()
            for j, chip in enumerate(chips):
                copy(a, 4 + j, (*chip, 1 - c), me).wait_recv()
        for cp in first + passed:
            cp.wait_send()
        for cp in own:
            cp.wait()

    anyspec = pl.BlockSpec(memory_space=pl.ANY)
    res = pl.pallas_call(
        body, in_specs=[anyspec] * n_arr, out_specs=[anyspec] * n_arr,
        out_shape=[jax.ShapeDtypeStruct((N_DEV,) + a.shape, a.dtype) for a in arrays],
        scratch_shapes=[pltpu.SemaphoreType.DMA((n_arr * per,)), pltpu.SemaphoreType.DMA((n_arr * per,)),
                        pltpu.SemaphoreType.DMA((n_arr,))],
        name=name)(*arrays)
    return list(res)


def _sibling_swap(arrays, name):
    n_arr = len(arrays)
    n_chip = N_DEV // 2

    def body(*refs):
        srcs, outs = refs[:n_arr], refs[n_arr:2 * n_arr]
        send_sems, recv_sems = refs[2 * n_arr:]
        x, y, c = lax.axis_index("x"), lax.axis_index("y"), lax.axis_index("c")
        copies = []
        for a in range(n_arr):
            for j in range(n_chip):
                copies.append(pltpu.make_async_remote_copy(
                    src_ref=srcs[a].at[2 * j + 1 - c], dst_ref=outs[a].at[j],
                    send_sem=send_sems.at[a * n_chip + j], recv_sem=recv_sems.at[a * n_chip + j],
                    device_id=(x, y, 1 - c), device_id_type=pl.DeviceIdType.MESH))
        for cp in copies:
            cp.start()
        for cp in copies:
            cp.wait()

    anyspec = pl.BlockSpec(memory_space=pl.ANY)
    res = pl.pallas_call(
        body, in_specs=[anyspec] * n_arr, out_specs=[anyspec] * n_arr,
        out_shape=[jax.ShapeDtypeStruct((n_chip,) + a.shape[1:], a.dtype) for a in arrays],
        scratch_shapes=[pltpu.SemaphoreType.DMA((n_arr * n_chip,)), pltpu.SemaphoreType.DMA((n_arr * n_chip,))],
        name=name)(*arrays)
    return list(res)


def _pair_sum(send, stage, core, name):
    _, r, c = send.shape
    n_chip = stage.shape[0]
    tr = _tile(r, 128)

    def body(core_ref, a_ref, b_ref, o_ref):
        o_ref[...] = a_ref[...] + b_ref[...]

    return pl.pallas_call(
        body,
        grid_spec=pltpu.PrefetchScalarGridSpec(
            num_scalar_prefetch=1, grid=(n_chip, r // tr),
            in_specs=[pl.BlockSpec((1, tr, c), lambda j, i, core_ref: (2 * j + core_ref[0], i, 0)),
                      pl.BlockSpec((1, tr, c), lambda j, i, core_ref: (j, i, 0))],
            out_specs=pl.BlockSpec((1, tr, c), lambda j, i, core_ref: (j, i, 0))),
        out_shape=jax.ShapeDtypeStruct(stage.shape, F32), name=name,
        compiler_params=_cp(("parallel", "parallel")))(core, send, stage)


_HBM_SPEC = pl.BlockSpec(memory_space=pltpu.HBM)
_SEM_SPEC = pl.BlockSpec(memory_space=pltpu.SEMAPHORE)
_ANY_SPEC = pl.BlockSpec(memory_space=pl.ANY)
_DATAFLOW = pltpu.SideEffectType.DATAFLOW_SIDE_EFFECTING


def _exchange_start(arrays, scatter, name, dep=None, chips=False):
    n_arr = len(arrays)
    n_slot = N_DEV // 2 if chips else N_DEV
    n_sem = n_arr * (n_slot - 1)
    me = (2 * lax.axis_index("x") + lax.axis_index("y") if chips
          else 4 * lax.axis_index("x") + 2 * lax.axis_index("y") + lax.axis_index("c"))
    lands = []
    for a in arrays:
        own = lax.dynamic_index_in_dim(a, me, 0, keepdims=False) if scatter else a
        shape = a.shape if scatter else (n_slot,) + a.shape
        lands.append(lax.dynamic_update_index_in_dim(lax.empty(shape, a.dtype), own, me, 0))
    dep_specs, dep_args = _dep_specs(dep)

    def body(*refs):
        srcs, lnds = refs[:n_arr], refs[n_arr:2 * n_arr]
        outs = refs[2 * n_arr + len(dep_args):]
        send_sems, recv_sems, token = outs[0], outs[1], outs[2 + 2 * n_arr]
        for cp in _peer_copies(srcs, lnds, send_sems, recv_sems, scatter, chips):
            cp.start()
        token[...] = jnp.zeros_like(token)

    thru = [pltpu.HBM(a.shape, a.dtype) for a in list(arrays) + lands]
    return pl.pallas_call(
        body, name=name,
        out_shape=(pltpu.SemaphoreType.DMA((n_sem,)), pltpu.SemaphoreType.DMA((n_sem,)), *thru,
                   jax.ShapeDtypeStruct((SUBLANES, LANES), F32)),
        in_specs=[_HBM_SPEC] * (2 * n_arr) + dep_specs,
        out_specs=(_SEM_SPEC, _SEM_SPEC, *[_HBM_SPEC] * (2 * n_arr), pl.BlockSpec(memory_space=pltpu.VMEM)),
        input_output_aliases={i: 2 + i for i in range(2 * n_arr)},
        compiler_params=pltpu.CompilerParams(has_side_effects=_DATAFLOW),
    )(*[pltpu.with_memory_space_constraint(a, pltpu.HBM) for a in list(arrays) + lands], *dep_args)


def _exchange_wait(started, scatter, name, after, chips=False):
    send_sems, recv_sems = started[0], started[1]
    thru = list(started[2:-1])
    n_arr = len(thru) // 2

    def body(*refs):
        srcs, lnds = refs[:n_arr], refs[n_arr:2 * n_arr]
        for cp in _peer_copies(srcs, lnds, refs[2 * n_arr], refs[2 * n_arr + 1], scatter, chips):
            cp.wait_send()
            cp.wait_recv()

    res = pl.pallas_call(
        body, name=name, out_shape=tuple(pltpu.HBM(a.shape, a.dtype) for a in thru),
        in_specs=[_HBM_SPEC] * (2 * n_arr) + [_SEM_SPEC, _SEM_SPEC, _ANY_SPEC],
        out_specs=tuple([_HBM_SPEC] * (2 * n_arr)),
        input_output_aliases={i: i for i in range(2 * n_arr)},
        compiler_params=pltpu.CompilerParams(has_side_effects=_DATAFLOW),
    )(*thru, send_sems, recv_sems, after)
    return list(res[n_arr:])


def _unshard_cols(g, name):
    nd, r, s = g.shape
    tr = _tile(r, 64)

    def body(i_ref, o_ref):
        for p in range(nd):
            o_ref[:, p * s:(p + 1) * s] = i_ref[p]

    return pl.pallas_call(
        body, grid=(r // tr,), in_specs=[pl.BlockSpec((nd, tr, s), lambda i: (0, i, 0))],
        out_specs=pl.BlockSpec((tr, nd * s), lambda i: (i, 0)),
        out_shape=jax.ShapeDtypeStruct((r, nd * s), g.dtype), name=name, compiler_params=_cp(("parallel",)))(g)


def _shard_cols(g, name):
    r, n = g.shape
    s = n // N_DEV
    tr = _tile(r, 64)

    def body(i_ref, o_ref):
        for p in range(N_DEV):
            o_ref[p] = i_ref[:, p * s:(p + 1) * s]

    return pl.pallas_call(
        body, grid=(r // tr,), in_specs=[pl.BlockSpec((tr, n), lambda i: (i, 0))],
        out_specs=pl.BlockSpec((N_DEV, tr, s), lambda i: (0, i, 0)),
        out_shape=jax.ShapeDtypeStruct((N_DEV, r, s), g.dtype), name=name, compiler_params=_cp(("parallel",)))(g)


def _slot_sum(slots, name):
    nd, r, c = slots.shape
    tr = _tile(r, 64)

    def body(s_ref, o_ref):
        acc = s_ref[0]
        for p in range(1, nd):
            acc = acc + s_ref[p]
        o_ref[...] = acc

    return pl.pallas_call(
        body, grid=(r // tr,), in_specs=[pl.BlockSpec((nd, tr, c), lambda i: (0, i, 0))],
        out_specs=pl.BlockSpec((tr, c), lambda i: (i, 0)),
        out_shape=jax.ShapeDtypeStruct((r, c), F32), name=name, compiler_params=_cp(("parallel",)))(slots)


def _adamw(w, g, m, v, name):
    r, c = w.shape
    tr = _tile(r, 256)
    c1 = 1.0 - ADAM_B1 ** ADAM_STEP
    c2 = 1.0 - ADAM_B2 ** ADAM_STEP

    def body(w_ref, g_ref, m_ref, v_ref, d_ref, nm_ref, nv_ref):
        gv = g_ref[...]
        nm = ADAM_B1 * m_ref[...] + (1.0 - ADAM_B1) * gv
        nv = ADAM_B2 * v_ref[...] + (1.0 - ADAM_B2) * (gv * gv)
        nm_ref[...] = nm
        nv_ref[...] = nv
        d_ref[...] = -ADAM_LR * ((nm / c1) / (jnp.sqrt(nv / c2) + ADAM_EPS) + ADAM_WD * w_ref[...])

    spec = pl.BlockSpec((tr, c), lambda i: (i, 0))
    return pl.pallas_call(
        body, grid=(r // tr,), in_specs=[spec] * 4, out_specs=[spec] * 3,
        out_shape=[jax.ShapeDtypeStruct((r, c), F32)] * 3, name=name, compiler_params=_cp(("parallel",)))(w, g, m, v)


def _forward_backward(x, target, weights_hook, grads_hook, lb_param, hgrn_norm_g, attn_sinks, rel_bias, ln_g, ln_b):
    t, d = x.shape
    w = d // 2
    off, n_in = _offsets(d)
    n_heads = w // ATTN_HEAD_DIM
    grp = n_heads // ATTN_KV_HEADS

    lower = _lower_bounds(lb_param)
    bias = _bias_table(rel_bias, n_heads)
    bias_g = bias.reshape(ATTN_KV_HEADS, grp * WINDOW, 2 * WINDOW)

    saved, weights = [], []
    xb = x.astype(BF16)
    for l in range(DEPTH):
        wl, token = weights_hook(l, x)
        weights.append(wl)
        s = {"x": x, "xb": xb}
        u = _mm_nn(xb, wl["w_in"], f"in_proj", dep=token)
        s["u"] = u
        lb_l, gain_l, cw_l = lower[l:l + 1], hgrn_norm_g[l:l + 1], wl["conv_w"]
        o_a, states, p_a = _hgrn_fwd(u, lb_l, gain_l, off, f"hgrn_fwd")
        o_b, p_b = _attn_fwd(u, bias_g, attn_sinks[l], off, w, f"attn_fwd")
        p_c = _conv_fwd(u, cw_l, off, w, f"conv_fwd")
        y_a = _mm_nn(p_a, wl["w_proj_hgrn"], f"proj_a", tn=2048)
        y_b = _mm_nn(p_b, wl["w_proj_attn"], f"proj_b", tn=2048)
        y_c = _mm_nn(p_c, wl["w_proj_conv"], f"proj_c", tn=2048)
        merged = _merge_fwd(u, y_a, y_b, y_c, off, d, f"merge_fwd")
        y = _mm_nn(merged, wl["w_out"], f"out_proj", tm=512, tn=2048)
        x, xb, xhat, rstd = _ln_fwd(x, y, ln_g[l:l + 1], ln_b[l:l + 1], f"ln_fwd")
        s.update(o_a=o_a, states=states, p_a=p_a, o_b=o_b, p_b=p_b, p_c=p_c, y_a=y_a, y_b=y_b, y_c=y_c,
                 merged=merged, xhat=xhat, rstd=rstd)
        saved.append(s)

    loss_acc, dx = _loss_head(x, target)

    d_ln, d_lower, d_gain, d_sink, d_conv = [None] * DEPTH, [None] * DEPTH, [None] * DEPTH, [None] * DEPTH, [None] * DEPTH
    dbias_total = None
    for l in reversed(range(DEPTH)):
        wl, s = weights[l], saved[l]
        u = s["u"]
        lb_l, gain_l, cw_l = lower[l:l + 1], hgrn_norm_g[l:l + 1], wl["conv_w"]
        dz, dzb, d_ln[l] = _ln_bwd(dx, s["xhat"], s["rstd"], ln_g[l:l + 1], f"ln_bwd")
        g_out = _mm_tn(s["merged"], dzb, f"g_out", tn=2048)
        dmerged = _mm_nt(dzb, wl["w_out"], f"d_merged", tk=2048)
        dya, dyb, dyc, dma, dmb, dmc = _merge_bwd(dmerged, u, s["y_a"], s["y_b"], s["y_c"], off, d, f"merge_bwd")
        g_pa = _mm_tn(s["p_a"], dya, f"g_proj_a", tn=2048)
        g_pb = _mm_tn(s["p_b"], dyb, f"g_proj_b", tn=2048)
        g_pc = _mm_tn(s["p_c"], dyc, f"g_proj_c", tn=2048)
        dpa = _mm_nt(dya, wl["w_proj_hgrn"], f"d_p_a", tk=2048)
        dpb = _mm_nt(dyb, wl["w_proj_attn"], f"d_p_b", tk=2048)
        dpc = _mm_nt(dyc, wl["w_proj_conv"], f"d_p_c", tk=2048)
        d_aq, d_af, d_ai, d_ag, acc_a = _hgrn_bwd(u, lb_l, gain_l, s["states"], s["o_a"], dpa, off, f"hgrn_bwd")
        d_lower[l], d_gain[l] = acc_a[0:1], acc_a[1:2]
        d_bq, dkc, dkp, dvc, dvp, dbias_l, d_sink[l], d_bg = _attn_bwd(
            u, s["o_b"], dpb, bias_g, attn_sinks[l], off, w, f"attn_bwd")
        d_bk = _kv_combine(dkc, dkp, f"k_combine")
        d_bv = _kv_combine(dvc, dvp, f"v_combine")
        dbias_total = dbias_l if dbias_total is None else dbias_total + dbias_l
        d_cb, d_cc, d_cx, d_cg, d_conv[l] = _conv_bwd(dpc, u, cw_l, off, w, f"conv_bwd")
        du = jnp.concatenate([d_aq, d_af, d_ai, d_ag, d_bq, d_bk, d_bv, d_bg, d_cb, d_cc, d_cx, d_cg, dma, dmb, dmc], axis=1)
        g_in = _mm_tn(s["xb"], du, f"g_in")
        token = grads_hook(l, {"w_in": g_in, "w_proj_hgrn": g_pa, "w_proj_attn": g_pb, "w_proj_conv": g_pc, "w_out": g_out})
        dx = _mm_nt(du, wl["w_in"], f"d_x", add=dz, add_scale=ALPHA, dep=token)

    d_lower_all = jnp.concatenate([a[0:1] for a in d_lower], axis=0)
    small = {
        "lb_param": _lower_bounds_bwd(lb_param, d_lower_all),
        "hgrn_norm_g": jnp.concatenate([a[0:1] for a in d_gain], axis=0),
        "attn_sinks": jnp.concatenate([a[0:1, :n_heads] for a in d_sink], axis=0),
        "conv_w": jnp.stack([a[0:3] for a in d_conv], axis=0),
        "rel_bias": _bias_grad(dbias_total.reshape(n_heads, WINDOW, 2 * WINDOW), n_heads)[:, :n_heads],
        "ln_g": jnp.concatenate([a[0:1] for a in d_ln], axis=0),
        "ln_b": jnp.concatenate([a[1:2] for a in d_ln], axis=0),
    }
    return loss_acc, dx, small


BIG = ("w_in", "w_proj_hgrn", "w_proj_attn", "w_proj_conv", "w_out")
SMALL = ("lb_param", "hgrn_norm_g", "attn_sinks", "conv_w", "rel_bias", "ln_g", "ln_b")
ORDER = ("w_in", "w_proj_hgrn", "w_proj_attn", "w_proj_conv", "w_out", "lb_param", "hgrn_norm_g", "attn_sinks",
         "conv_w", "rel_bias", "ln_g", "ln_b")


def _pack(parts):
    flat = jnp.concatenate([p.reshape(-1) for p in parts])
    n = flat.shape[0]
    unit = SUBLANES * LANES
    total = -(-n // unit) * unit
    return jnp.pad(flat, (0, total - n)).reshape(total // LANES, LANES)


def _unpack(packed, shapes):
    flat = packed.reshape(-1)
    out, o = [], 0
    for shp in shapes:
        n = int(np.prod(shp))
        out.append(flat[o:o + n].reshape(shp))
        o += n
    return out


def kernel(x, w_in, w_proj_hgrn, w_proj_attn, w_proj_conv, w_out, lb_param, hgrn_norm_g, attn_sinks, conv_w, rel_bias, ln_g, ln_b, loss_target, m_w_in, m_w_proj_hgrn, m_w_proj_attn, m_w_proj_conv, m_w_out, m_lb_param, m_hgrn_norm_g, m_attn_sinks, m_conv_w, m_rel_bias, m_ln_g, m_ln_b, v_w_in, v_w_proj_hgrn, v_w_proj_attn, v_w_proj_conv, v_w_out, v_lb_param, v_hgrn_norm_g, v_attn_sinks, v_conv_w, v_rel_bias, v_ln_g, v_ln_b):
    params = dict(w_in=w_in, w_proj_hgrn=w_proj_hgrn, w_proj_attn=w_proj_attn, w_proj_conv=w_proj_conv, w_out=w_out,
                  lb_param=lb_param, hgrn_norm_g=hgrn_norm_g, attn_sinks=attn_sinks, conv_w=conv_w, rel_bias=rel_bias,
                  ln_g=ln_g, ln_b=ln_b)
    mom_m = dict(w_in=m_w_in, w_proj_hgrn=m_w_proj_hgrn, w_proj_attn=m_w_proj_attn, w_proj_conv=m_w_proj_conv,
                 w_out=m_w_out, lb_param=m_lb_param, hgrn_norm_g=m_hgrn_norm_g, attn_sinks=m_attn_sinks,
                 conv_w=m_conv_w, rel_bias=m_rel_bias, ln_g=m_ln_g, ln_b=m_ln_b)
    mom_v = dict(w_in=v_w_in, w_proj_hgrn=v_w_proj_hgrn, w_proj_attn=v_w_proj_attn, w_proj_conv=v_w_proj_conv,
                 w_out=v_w_out, lb_param=v_lb_param, hgrn_norm_g=v_hgrn_norm_g, attn_sinks=v_attn_sinks,
                 conv_w=v_conv_w, rel_bias=v_rel_bias, ln_g=v_ln_g, ln_b=v_ln_b)
    d = x.shape[-1]
    me = 4 * lax.axis_index("x") + 2 * lax.axis_index("y") + lax.axis_index("c")

    def shards_of(l):
        return [params[n][l].astype(BF16) for n in BIG] + [conv_w[l]]

    gathers = {}

    def weights_hook(l, x_in):
        if l == 0:
            got = _gather_two_level(shards_of(0), "gather_weights_0")
        else:
            got = _exchange_wait(gathers.pop(l), False, f"gather_wait_{l}", x_in)
        token = None
        if l + 1 < DEPTH:
            gathers[l + 1] = _exchange_start(shards_of(l + 1), False, f"gather_start_{l + 1}", dep=got[0])
            token = gathers[l + 1][-1]
        wl = {
            "w_in": _unshard_cols(got[0], "unshard_w_in"),
            "w_proj_hgrn": _unshard_cols(got[1], "unshard_w_proj_hgrn"),
            "w_proj_attn": _unshard_cols(got[2], "unshard_w_proj_attn"),
            "w_proj_conv": _unshard_cols(got[3], "unshard_w_proj_conv"),
            "w_out": got[4].reshape(d, d),
            "conv_w": _unshard_cols(got[5], "unshard_conv_w"),
        }
        return wl, token

    grads = {n: [None] * DEPTH for n in BIG}
    scatters = {}

    def finish_scatter(l, after):
        got = _exchange_wait(scatters.pop(l), True, f"scatter_wait_{l}", after, chips=(l == 0))
        for n, slots in zip(BIG, got):
            grads[n][l] = _slot_sum(slots, f"sum_{n}_chips" if l == 0 else f"sum_{n}")
        return got[0]

    def grads_hook(l, g):
        send = [_shard_cols(g["w_in"], "shard_g_in"), _shard_cols(g["w_proj_hgrn"], "shard_g_proj_a"),
                _shard_cols(g["w_proj_attn"], "shard_g_proj_b"), _shard_cols(g["w_proj_conv"], "shard_g_proj_c"),
                g["w_out"].reshape(N_DEV, d // N_DEV, d)]
        dep = finish_scatter(l + 1, send[0]) if l + 1 < DEPTH else None
        if l == 0:
            core = lax.axis_index("c").astype(jnp.int32).reshape(1)
            staged = _sibling_swap(send, "pair_swap_grads")
            send = [_pair_sum(s, st, core, f"pair_sum_{n}") for n, s, st in zip(BIG, send, staged)]
        scatters[l] = _exchange_start(send, True, f"scatter_start_{l}", dep=dep, chips=(l == 0))
        return scatters[l][-1]

    loss_acc, dx, small = _forward_backward(
        x[0], loss_target[0], weights_hook, grads_hook, lb_param, hgrn_norm_g, attn_sinks, rel_bias, ln_g, ln_b)
    loss = lax.psum(0.5 * jnp.sum(loss_acc[0]) / d, ("x", "y", "c"))
    finish_scatter(0, dx)
    for n in BIG:
        grads[n] = jnp.stack(grads[n], axis=0)

    small_shapes = [small[n].shape for n in SMALL]
    packed = _pack([small[n] for n in SMALL])
    got = _exchange([packed], False, "gather_small_grads")[0]
    summed = _unpack(_slot_sum(got, "sum_small_grads"), small_shapes)
    for n, g in zip(SMALL, summed):
        grads[n] = g
    cs = conv_w.shape[-1]
    grads["conv_w"] = lax.dynamic_slice_in_dim(grads["conv_w"], me * cs, cs, axis=2)

    delta, new_m, new_v = {}, {}, {}
    for n in BIG:
        shp = params[n].shape
        flat = lambda a: a.reshape(-1, shp[-1])
        dl, nm, nv = _adamw(flat(params[n]), flat(grads[n]), flat(mom_m[n]), flat(mom_v[n]), f"adamw_{n}")
        delta[n], new_m[n], new_v[n] = dl.reshape(shp), nm.reshape(shp), nv.reshape(shp)
    shapes = [params[n].shape for n in SMALL]
    res = _adamw(_pack([params[n] for n in SMALL]), _pack([grads[n] for n in SMALL]),
                 _pack([mom_m[n] for n in SMALL]), _pack([mom_v[n] for n in SMALL]), "adamw_small")
    for dst, packed_res in zip((delta, new_m, new_v), res):
        for n, a in zip(SMALL, _unpack(packed_res, shapes)):
            dst[n] = a

    return (loss, dx[None], *[grads[n] for n in ORDER], *[delta[n] for n in ORDER],
            *[new_m[n] for n in ORDER], *[new_v[n] for n in ORDER])
```

```python
import functools
import math

import numpy as np
import jax
import jax.numpy as jnp
from jax import lax
from jax.experimental import pallas as pl
from jax.experimental.pallas import tpu as pltpu

F32 = jnp.float32
BF16 = jnp.bfloat16

N_DEV = 8
DEPTH = 4
HGRN_HEAD_DIM = 128
HGRN_CHUNK = 64
ATTN_HEAD_DIM = 64
ATTN_KV_HEADS = 4
KV_WIDTH = ATTN_KV_HEADS * ATTN_HEAD_DIM
WINDOW = 128
WINDOW_SHIFT = 7
N_BUCKETS = 32
MAX_DISTANCE = 128
ALPHA = (2.0 * DEPTH) ** 0.25
LN_EPS = 1e-5
RMS_EPS = 1e-6
ADAM_LR = 0.001
ADAM_B1 = 0.9
ADAM_B2 = 0.999
ADAM_EPS = 1e-08
ADAM_WD = 0.01
ADAM_STEP = 10

LANES = 128
SUBLANES = 8
VMEM_LIMIT = 56 << 20
NEG_INF = float("-inf")


def _offsets(d_model):
    w = d_model // 2
    sizes = (w, w, w, w, w, KV_WIDTH, KV_WIDTH, w, w, w, w, w, d_model, d_model, d_model)
    names = ("a_q", "a_f", "a_i", "a_g", "b_q", "b_k", "b_v", "b_g", "c_b", "c_c", "c_x", "c_g", "m_a", "m_b", "m_c")
    off, o = {}, 0
    for n, s in zip(names, sizes):
        off[n] = o
        o += s
    return off, o


def _tile(n, pref):
    t = min(pref, n)
    while n % t:
        t //= 2
    return t


def _cp(sem=None, vmem=VMEM_LIMIT):
    return pltpu.CompilerParams(dimension_semantics=sem, vmem_limit_bytes=vmem)


def _sigmoid(x):
    return 1.0 / (1.0 + jnp.exp(-x))


def _dot_nn(a, b):
    return jnp.dot(a, b, preferred_element_type=F32)


def _dot_nt(a, b):
    return lax.dot_general(a, b, (((1,), (1,)), ((), ())), preferred_element_type=F32)


def _dot_tn(a, b):
    return lax.dot_general(a, b, (((0,), (0,)), ((), ())), preferred_element_type=F32)


def _dep_specs(dep):
    return ([], []) if dep is None else ([pl.BlockSpec(memory_space=pl.ANY)], [dep])


def _mm_nn(a, b, name, out_dtype=F32, tm=1024, tn=1536, dep=None):
    m, k = a.shape
    _, n = b.shape
    tm, tn = _tile(m, tm), _tile(n, tn)
    dep_specs, dep_args = _dep_specs(dep)

    def body(a_ref, b_ref, *rest):
        o_ref = rest[-1]
        o_ref[...] = _dot_nn(a_ref[...], b_ref[...]).astype(o_ref.dtype)

    return pl.pallas_call(
        body, grid=(n // tn, m // tm),
        in_specs=[pl.BlockSpec((tm, k), lambda j, i: (i, 0)), pl.BlockSpec((k, tn), lambda j, i: (0, j))] + dep_specs,
        out_specs=pl.BlockSpec((tm, tn), lambda j, i: (i, j)),
        out_shape=jax.ShapeDtypeStruct((m, n), out_dtype), name=name,
        compiler_params=_cp(("parallel", "parallel")))(a, b, *dep_args)


def _mm_nt(a, b, name, tm=1024, tk=1536, add=None, add_scale=1.0, dep=None):
    m, k = a.shape
    n, _ = b.shape
    tm, tk = _tile(m, tm), _tile(k, tk)
    has_add = add is not None
    dep_specs, dep_args = _dep_specs(dep)

    def body(*refs):
        if has_add:
            a_ref, b_ref, add_ref = refs[:3]
        else:
            a_ref, b_ref = refs[:2]
        o_ref = refs[-1]
        if k == tk:
            prod = _dot_nt(a_ref[...], b_ref[...])
            o_ref[...] = prod + add_ref[...] * add_scale if has_add else prod
            return

        @pl.when(pl.program_id(1) == 0)
        def _():
            if has_add:
                o_ref[...] = add_ref[...] * add_scale
            else:
                o_ref[...] = jnp.zeros_like(o_ref)

        o_ref[...] += _dot_nt(a_ref[...], b_ref[...])

    in_specs = [pl.BlockSpec((tm, tk), lambda i, kk: (i, kk)), pl.BlockSpec((n, tk), lambda i, kk: (0, kk))]
    args = [a, b]
    if has_add:
        in_specs.append(pl.BlockSpec((tm, n), lambda i, kk: (i, 0)))
        args.append(add)
    in_specs += dep_specs
    args += dep_args
    return pl.pallas_call(
        body, grid=(m // tm, k // tk), in_specs=in_specs,
        out_specs=pl.BlockSpec((tm, n), lambda i, kk: (i, 0)),
        out_shape=jax.ShapeDtypeStruct((m, n), F32), name=name,
        compiler_params=_cp(("parallel", "arbitrary")))(*args)


def _mm_tn(a, b, name, tt=1024, tn=1536):
    t, k = a.shape
    _, n = b.shape
    tt, tn = _tile(t, tt), _tile(n, tn)

    def body(a_ref, b_ref, o_ref):
        @pl.when(pl.program_id(1) == 0)
        def _():
            o_ref[...] = jnp.zeros_like(o_ref)

        o_ref[...] += _dot_tn(a_ref[...], b_ref[...])

    return pl.pallas_call(
        body, grid=(n // tn, t // tt),
        in_specs=[pl.BlockSpec((tt, k), lambda j, s: (s, 0)), pl.BlockSpec((tt, tn), lambda j, s: (s, j))],
        out_specs=pl.BlockSpec((k, tn), lambda j, s: (0, j)),
        out_shape=jax.ShapeDtypeStruct((k, n), F32), name=name,
        compiler_params=_cp(("parallel", "arbitrary")))(a, b)


def _ew(body, name, t, ncol, wb, ins, outs, accs=(), tt=512):
    tt = _tile(t, tt)
    nt = t // tt
    in_specs, args = [], []
    for arr, kind, coff in ins:
        if kind == "tile":
            spec = pl.BlockSpec((tt, wb), lambda j, i, c=coff: (i, c + j))
        elif kind == "prev":
            spec = pl.BlockSpec((tt, wb), lambda j, i, c=coff: (jnp.maximum(i - 1, 0), c + j))
        elif kind == "next":
            spec = pl.BlockSpec((tt, wb), lambda j, i, c=coff: (jnp.minimum(i + 1, nt - 1), c + j))
        else:
            spec = pl.BlockSpec((arr.shape[0], wb), lambda j, i, c=coff: (0, c + j))
        in_specs.append(spec)
        args.append(arr)
    out_specs = [pl.BlockSpec((tt, wb), lambda j, i: (i, j)) for _ in outs]
    out_shape = [jax.ShapeDtypeStruct((t, ncol * wb), d) for d in outs]
    for r in accs:
        out_specs.append(pl.BlockSpec((r, wb), lambda j, i: (0, j)))
        out_shape.append(jax.ShapeDtypeStruct((r, ncol * wb), F32))

    def kern(*refs):
        body(pl.program_id(1), nt, *refs)

    res = pl.pallas_call(
        kern, grid=(ncol, nt), in_specs=in_specs, out_specs=out_specs, out_shape=out_shape, name=name,
        compiler_params=_cp(("parallel", "arbitrary")))(*args)
    return res


def _silu_parts(x):
    s = _sigmoid(x)
    return x * s, s + x * s * (1.0 - s)


def _ln_fwd(x, y, g, b, name):
    t, d = x.shape
    tt = _tile(t, 256)

    def body(x_ref, y_ref, g_ref, b_ref, o_ref, ob_ref, xh_ref, r_ref):
        z = ALPHA * x_ref[...] + y_ref[...]
        mu = jnp.mean(z, axis=1, keepdims=True)
        zc = z - mu
        var = jnp.mean(zc * zc, axis=1, keepdims=True)
        rstd = lax.rsqrt(var + LN_EPS)
        xh = zc * rstd
        o = xh * g_ref[...] + b_ref[...]
        o_ref[...] = o
        ob_ref[...] = o.astype(BF16)
        xh_ref[...] = xh
        r_ref[...] = rstd

    row = pl.BlockSpec((tt, d), lambda i: (i, 0))
    vec = pl.BlockSpec((1, d), lambda i: (0, 0))
    return pl.pallas_call(
        body, grid=(t // tt,), in_specs=[row, row, vec, vec],
        out_specs=[row, row, row, pl.BlockSpec((tt, 1), lambda i: (i, 0))],
        out_shape=[jax.ShapeDtypeStruct((t, d), F32), jax.ShapeDtypeStruct((t, d), BF16),
                   jax.ShapeDtypeStruct((t, d), F32), jax.ShapeDtypeStruct((t, 1), F32)],
        name=name, compiler_params=_cp(("parallel",)))(x, y, g, b)


def _ln_bwd(dout, xhat, rstd, g, name):
    t, d = dout.shape
    tt = _tile(t, 256)

    def body(do_ref, xh_ref, r_ref, g_ref, dz_ref, dzb_ref, acc_ref):
        @pl.when(pl.program_id(0) == 0)
        def _():
            acc_ref[...] = jnp.zeros_like(acc_ref)

        do = do_ref[...]
        xh = xh_ref[...]
        dxh = do * g_ref[...]
        m1 = jnp.mean(dxh, axis=1, keepdims=True)
        m2 = jnp.mean(dxh * xh, axis=1, keepdims=True)
        dz = r_ref[...] * (dxh - m1 - xh * m2)
        dz_ref[...] = dz
        dzb_ref[...] = dz.astype(BF16)
        acc_ref[0:1, :] += jnp.sum(do * xh, axis=0, keepdims=True)
        acc_ref[1:2, :] += jnp.sum(do, axis=0, keepdims=True)

    row = pl.BlockSpec((tt, d), lambda i: (i, 0))
    return pl.pallas_call(
        body, grid=(t // tt,),
        in_specs=[row, row, pl.BlockSpec((tt, 1), lambda i: (i, 0)), pl.BlockSpec((1, d), lambda i: (0, 0))],
        out_specs=[row, row, pl.BlockSpec((SUBLANES, d), lambda i: (0, 0))],
        out_shape=[jax.ShapeDtypeStruct((t, d), F32), jax.ShapeDtypeStruct((t, d), BF16),
                   jax.ShapeDtypeStruct((SUBLANES, d), F32)],
        name=name, compiler_params=_cp(("arbitrary",)))(dout, xhat, rstd, g)


def _loss_head(y, target):
    t, d = y.shape
    tt = _tile(t, 256)

    def body(y_ref, t_ref, acc_ref, dy_ref):
        @pl.when(pl.program_id(0) == 0)
        def _():
            acc_ref[...] = jnp.zeros_like(acc_ref)

        err = y_ref[...] - t_ref[...]
        dy_ref[...] = err * (1.0 / d)
        acc_ref[0:1, :] += jnp.sum(err * err, axis=0, keepdims=True)

    row = pl.BlockSpec((tt, d), lambda i: (i, 0))
    acc, dy = pl.pallas_call(
        body, grid=(t // tt,), in_specs=[row, row],
        out_specs=[pl.BlockSpec((SUBLANES, d), lambda i: (0, 0)), row],
        out_shape=[jax.ShapeDtypeStruct((SUBLANES, d), F32), jax.ShapeDtypeStruct((t, d), F32)],
        name="loss_head", compiler_params=_cp(("arbitrary",)))(y, target)
    return acc, dy


def _tri(lower):
    r = lax.broadcasted_iota(jnp.int32, (HGRN_CHUNK, HGRN_CHUNK), 0)
    c = lax.broadcasted_iota(jnp.int32, (HGRN_CHUNK, HGRN_CHUNK), 1)
    return jnp.where((r >= c) if lower else (r <= c), 1.0, 0.0).astype(BF16)


def _exact_tri_matmul(tri, x):
    hi = x.astype(BF16)
    r1 = x - hi.astype(F32)
    mid = r1.astype(BF16)
    lo = (r1 - mid.astype(F32)).astype(BF16)
    return _dot_nn(tri, hi) + _dot_nn(tri, mid) + _dot_nn(tri, lo)


def _hgrn_gates(q_raw, fl, lb):
    sq = _sigmoid(q_raw)
    qf = q_raw * sq * (HGRN_HEAD_DIM ** -0.5)
    sg = _sigmoid(fl)
    f = lb + (1.0 - lb) * sg
    return qf, sq, sg, f


HGRN_SUB = 16
HGRN_NSUB = HGRN_CHUNK // HGRN_SUB
HGRN_HEADS_PER_STEP = 8
HGRN_HEADS_PER_STEP_BWD = 8


def _diag_rows(r):
    return (r // SUBLANES) * SUBLANES


def _heads(x):
    hd = HGRN_HEAD_DIM
    return [x[:, i * hd:(i + 1) * hd] for i in range(x.shape[1] // hd)]


def _per_head(fn, *xs):
    split = [x if isinstance(x, (list, tuple)) else _heads(x) for x in xs]
    return jnp.concatenate([fn(*hs) for hs in zip(*split)], axis=1)


def _head_lane_sum(x):
    return _per_head(lambda h: jnp.broadcast_to(jnp.sum(h, axis=1, keepdims=True), h.shape), x)


def _hgrn_intra_fwd(qf, k, v, b):
    ch, sub, wd = HGRN_CHUNK, HGRN_SUB, qf.shape[1]
    tl = lax.broadcasted_iota(jnp.int32, (sub, wd), 0)
    blocks = []
    for m in range(HGRN_NSUB):
        rs = slice(m * sub, (m + 1) * sub)
        bm, qm, km, vm = b[rs], qf[rs], k[rs], v[rs]
        parts = {0: jnp.zeros((sub, wd), F32), SUBLANES: jnp.zeros((sub - SUBLANES, wd), F32)}
        for r in range(sub):
            lo = _diag_rows(r)
            e = jnp.exp(jnp.where(tl[lo:] >= r, bm[lo:] - bm[r:r + 1], NEG_INF))
            parts[lo] = parts[lo] + _head_lane_sum(qm[lo:] * e * km[r:r + 1]) * vm[r:r + 1]
        blocks.append(parts[0] + jnp.concatenate([jnp.zeros((SUBLANES, wd), F32), parts[SUBLANES]], axis=0))
    acc = jnp.concatenate(blocks, axis=0)
    for j in range(HGRN_NSUB - 1):
        lo = sub * (j + 1)
        c = b[lo - 1:lo, :]
        qj = (qf[lo:] * jnp.exp(b[lo:] - c)).astype(BF16)
        kj = (k[lo - sub:lo] * jnp.exp(c - b[lo - sub:lo])).astype(BF16)
        vj = v[lo - sub:lo].astype(BF16)
        contrib = _per_head(lambda q_, k_, v_: _dot_nn(_dot_nt(q_, k_).astype(BF16), v_), qj, kj, vj)
        acc = acc + jnp.concatenate([jnp.zeros((lo, wd), F32), contrib], axis=0)
    return acc


def _hgrn_intra_bwd(qf, k, v, b, do_v):
    ch, sub, wd = HGRN_CHUNK, HGRN_SUB, qf.shape[1]
    tl = lax.broadcasted_iota(jnp.int32, (sub, wd), 0)
    dq_blocks, dk_blocks, dv_blocks = [], [], []
    for m in range(HGRN_NSUB):
        rs = slice(m * sub, (m + 1) * sub)
        bm, qm, km, vm, dom = b[rs], qf[rs], k[rs], v[rs], do_v[rs]
        parts = {0: jnp.zeros((sub, wd), F32), SUBLANES: jnp.zeros((sub - SUBLANES, wd), F32)}
        dk_parts = {sub: jnp.zeros((sub, wd), F32), SUBLANES: jnp.zeros((SUBLANES, wd), F32)}
        dv_parts = {sub: jnp.zeros((sub, wd), F32), SUBLANES: jnp.zeros((SUBLANES, wd), F32)}
        for r in range(sub):
            lo = _diag_rows(r)
            b_r, k_r, v_r, q_r, do_r = bm[r:r + 1], km[r:r + 1], vm[r:r + 1], qm[r:r + 1], dom[r:r + 1]
            e = jnp.exp(jnp.where(tl[lo:] >= r, bm[lo:] - b_r, NEG_INF))
            parts[lo] = parts[lo] + _head_lane_sum(dom[lo:] * v_r) * (k_r * e)
            hi = lo + SUBLANES
            e2 = jnp.exp(jnp.where(tl[:hi] <= r, b_r - bm[:hi], NEG_INF))
            qe2 = q_r * e2
            dk_parts[hi] = dk_parts[hi] + _head_lane_sum(vm[:hi] * do_r) * qe2
            dv_parts[hi] = dv_parts[hi] + _head_lane_sum(km[:hi] * qe2) * do_r
        pad = jnp.zeros((SUBLANES, wd), F32)
        dq_blocks.append(parts[0] + jnp.concatenate([pad, parts[SUBLANES]], axis=0))
        dk_blocks.append(dk_parts[sub] + jnp.concatenate([dk_parts[SUBLANES], pad], axis=0))
        dv_blocks.append(dv_parts[sub] + jnp.concatenate([dv_parts[SUBLANES], pad], axis=0))
    dq = jnp.concatenate(dq_blocks, axis=0)
    dk = jnp.concatenate(dk_blocks, axis=0)
    dv = jnp.concatenate(dv_blocks, axis=0)
    do_b, v_b = do_v.astype(BF16), v.astype(BF16)
    dk_off, dv_off = [], []
    for j in range(HGRN_NSUB - 1):
        lo = sub * (j + 1)
        c = b[lo - 1:lo, :]
        eq = jnp.exp(b[lo:] - c)
        ek = jnp.exp(c - b[lo - sub:lo])
        qj = (qf[lo:] * eq).astype(BF16)
        kj = (k[lo - sub:lo] * ek).astype(BF16)
        doj, vj = do_b[lo:], v_b[lo - sub:lo]
        dq_j = _per_head(lambda do_, v_, k_: _dot_nn(_dot_nt(do_, v_).astype(BF16), k_), doj, vj, kj)
        dk_j = _per_head(lambda do_, v_, q_: _dot_nn(_dot_nt(v_, do_).astype(BF16), q_), doj, vj, qj)
        dv_j = _per_head(lambda do_, k_, q_: _dot_nn(_dot_nt(k_, q_).astype(BF16), do_), doj, kj, qj)
        dq = dq + jnp.concatenate([jnp.zeros((lo, wd), F32), dq_j * eq], axis=0)
        dk_off.append(dk_j * ek)
        dv_off.append(dv_j)
    zero = jnp.zeros((sub, wd), F32)
    dk = dk + jnp.concatenate(dk_off + [zero], axis=0)
    dv = dv + jnp.concatenate(dv_off + [zero], axis=0)
    return dq, dk, dv


def _head_rms(o):
    return lax.rsqrt(_head_lane_sum(o * o) * (1.0 / HGRN_HEAD_DIM) + RMS_EPS)


def _hgrn_fwd(u, lb, gain, off, name):
    t = u.shape[0]
    w = lb.shape[1]
    hd, ch, hp = HGRN_HEAD_DIM, HGRN_CHUNK, HGRN_HEADS_PER_STEP
    nh, nc = w // hd, t // ch
    wb = hp * hd
    cq, cf, cv, cg = off["a_q"] // wb, off["a_f"] // wb, off["a_i"] // wb, off["a_g"] // wb

    def body(q_ref, f_ref, v_ref, g_ref, lb_ref, gain_ref, o_ref, st_ref, p_ref, state):
        @pl.when(pl.program_id(1) == 0)
        def _():
            state[...] = jnp.zeros_like(state)

        sts = [state[i] for i in range(hp)]
        qf, _, _, f = _hgrn_gates(q_ref[...], f_ref[...], lb_ref[...])
        k = 1.0 - f
        v = v_ref[...]
        b = _exact_tri_matmul(_tri(True), jnp.log(f))
        inter = _per_head(lambda qa_, st_: _dot_nt(qa_, st_.astype(BF16)), (qf * jnp.exp(b)).astype(BF16), sts)
        o = inter + _hgrn_intra_fwd(qf, k, v, b)
        o_ref[...] = o
        silu, _ = _silu_parts(g_ref[...])
        p_ref[...] = (o * _head_rms(o) * gain_ref[...] * silu).astype(BF16)
        b_end = b[ch - 1:ch, :]
        a_end = _heads(jnp.exp(b_end))
        kd = _heads((k * jnp.exp(b_end - b)).astype(BF16))
        v_b = _heads(v.astype(BF16))
        for i in range(hp):
            st_ref[i, 0] = sts[i]
            state[i] = sts[i] * a_end[i] + _dot_tn(v_b[i], kd[i])

    return pl.pallas_call(
        body, grid=(nh // hp, nc),
        in_specs=[pl.BlockSpec((ch, wb), lambda h, n: (n, cq + h)),
                  pl.BlockSpec((ch, wb), lambda h, n: (n, cf + h)),
                  pl.BlockSpec((ch, wb), lambda h, n: (n, cv + h)),
                  pl.BlockSpec((ch, wb), lambda h, n: (n, cg + h)),
                  pl.BlockSpec((1, wb), lambda h, n: (0, h)),
                  pl.BlockSpec((1, wb), lambda h, n: (0, h))],
        out_specs=[pl.BlockSpec((ch, wb), lambda h, n: (n, h)),
                   pl.BlockSpec((hp, 1, hd, hd), lambda h, n: (h, n, 0, 0)),
                   pl.BlockSpec((ch, wb), lambda h, n: (n, h))],
        out_shape=[jax.ShapeDtypeStruct((t, w), F32), jax.ShapeDtypeStruct((nh, nc, hd, hd), F32),
                   jax.ShapeDtypeStruct((t, w), BF16)],
        scratch_shapes=[pltpu.VMEM((hp, hd, hd), F32)],
        name=name, compiler_params=_cp(("parallel", "arbitrary")))(u, u, u, u, lb, gain)


def _hgrn_bwd(u, lb, gain, states, o, dp, off, name):
    t = u.shape[0]
    w = lb.shape[1]
    hd, ch, hp = HGRN_HEAD_DIM, HGRN_CHUNK, HGRN_HEADS_PER_STEP_BWD
    nh, nc = w // hd, t // ch
    wb = hp * hd
    cq, cf, cv, cg = off["a_q"] // wb, off["a_f"] // wb, off["a_i"] // wb, off["a_g"] // wb

    def body(q_ref, f_ref, v_ref, g_ref, o_ref, dp_ref, st_ref, lb_ref, gain_ref,
             dq_ref, df_ref, dv_ref, dg_ref, dlb_ref, dstate):
        @pl.when(pl.program_id(1) == 0)
        def _():
            dstate[...] = jnp.zeros_like(dstate)
            dlb_ref[...] = jnp.zeros_like(dlb_ref)

        silu, dsilu = _silu_parts(g_ref[...])
        o_v, dp_v, gain_row = o_ref[...], dp_ref[...], gain_ref[...]
        rms = _head_rms(o_v)
        nrm = o_v * rms
        dg_ref[...] = (dp_v * nrm * gain_row * dsilu).astype(BF16)
        dlb_ref[1:2, :] += jnp.sum(dp_v * nrm * silu, axis=0, keepdims=True)
        dn = dp_v * gain_row * silu
        do_v = rms * (dn - nrm * (_head_lane_sum(dn * nrm) * (1.0 / HGRN_HEAD_DIM)))

        rows = lax.broadcasted_iota(jnp.int32, (ch, wb), 0)
        lb_row = lb_ref[...]
        q_raw = q_ref[...]
        qf, sq, sg, f = _hgrn_gates(q_raw, f_ref[...], lb_row)
        k = 1.0 - f
        b = _exact_tri_matmul(_tri(True), jnp.log(f))
        a = jnp.exp(b)
        b_end = b[ch - 1:ch, :]
        a_end = jnp.exp(b_end)
        to_end = jnp.exp(b_end - b)
        v = v_ref[...]
        st0 = [st_ref[i, 0] for i in range(hp)]
        ds = [dstate[i] for i in range(hp)]
        st0_b = [s_.astype(BF16) for s_ in st0]
        ds_b = [s_.astype(BF16) for s_ in ds]
        do_b, v_b, kd_b, qa_b = do_v.astype(BF16), v.astype(BF16), (k * to_end).astype(BF16), (qf * a).astype(BF16)

        dq_inter = a * _per_head(_dot_nn, do_b, st0_b)
        dk_end = to_end * _per_head(_dot_nn, v_b, ds_b)
        dv_end = _per_head(_dot_nt, kd_b, ds_b)
        a_end_h = _heads(a_end)
        st_end = [st0[i] * a_end_h[i] + _dot_tn(_heads(v_b)[i], _heads(kd_b)[i]) for i in range(hp)]
        db_end = jnp.concatenate([jnp.sum(ds[i] * st_end[i], axis=0, keepdims=True) for i in range(hp)], axis=1)
        ds_new = [ds[i] * a_end_h[i] + _dot_tn(_heads(do_b)[i], _heads(qa_b)[i]) for i in range(hp)]

        dq_intra, dk_intra, dv_intra = _hgrn_intra_bwd(qf, k, v, b, do_v)
        dqf = dq_inter + dq_intra
        dk = dk_end + dk_intra
        dv = dv_end + dv_intra
        db = qf * dqf - k * dk
        db = db + jnp.where(rows == ch - 1, db_end, 0.0)
        dg = _exact_tri_matmul(_tri(False), db)
        df = dg / f - dk
        for i in range(hp):
            dstate[i] = ds_new[i]
        dq_ref[...] = (dqf * (HGRN_HEAD_DIM ** -0.5) * (sq + q_raw * sq * (1.0 - sq))).astype(BF16)
        df_ref[...] = (df * (1.0 - lb_row) * sg * (1.0 - sg)).astype(BF16)
        dv_ref[...] = dv.astype(BF16)
        dlb_ref[0:1, :] += jnp.sum(df * (1.0 - sg), axis=0, keepdims=True)

    rev = lambda n: nc - 1 - n
    tile = lambda c: pl.BlockSpec((ch, wb), lambda h, n, c=c: (rev(n), c + h))
    return pl.pallas_call(
        body, grid=(nh // hp, nc),
        in_specs=[tile(cq), tile(cf), tile(cv), tile(cg), tile(0), tile(0),
                  pl.BlockSpec((hp, 1, hd, hd), lambda h, n: (h, rev(n), 0, 0)),
                  pl.BlockSpec((1, wb), lambda h, n: (0, h)), pl.BlockSpec((1, wb), lambda h, n: (0, h))],
        out_specs=[tile(0), tile(0), tile(0), tile(0), pl.BlockSpec((SUBLANES, wb), lambda h, n: (0, h))],
        out_shape=[jax.ShapeDtypeStruct((t, w), BF16)] * 4 + [jax.ShapeDtypeStruct((SUBLANES, w), F32)],
        scratch_shapes=[pltpu.VMEM((hp, hd, hd), F32)],
        name=name, compiler_params=_cp(("parallel", "arbitrary")))(u, u, u, u, o, dp, states, lb, gain)


def _bucket_map():
    i = np.arange(WINDOW)[:, None]
    j = np.arange(2 * WINDOW)[None, :]
    dist = np.clip(WINDOW + i - j, 0, WINDOW - 1)
    max_exact = N_BUCKETS // 2
    logd = (np.log(np.maximum(dist, 1).astype(np.float32) / max_exact) / math.log(MAX_DISTANCE / max_exact))
    large = np.minimum(max_exact + (logd.astype(np.float32) * (N_BUCKETS - max_exact)).astype(np.int32), N_BUCKETS - 1)
    return np.where(dist < max_exact, dist, large).astype(np.int32)


def _bias_table(rel_bias, n_heads):
    bucket = jnp.asarray(_bucket_map())

    def body(rb_ref, bk_ref, o_ref):
        bk = bk_ref[...]
        i = lax.broadcasted_iota(jnp.int32, (WINDOW, 2 * WINDOW), 0)
        j = lax.broadcasted_iota(jnp.int32, (WINDOW, 2 * WINDOW), 1)
        band = ((j >= WINDOW) & (j - WINDOW <= i)) | ((j < WINDOW) & (j > i))
        for h in range(n_heads):
            def step(bi, acc):
                return jnp.where(bk == bi, rb_ref[bi, h], acc)
            table = lax.fori_loop(0, N_BUCKETS, step, jnp.zeros((WINDOW, 2 * WINDOW), F32))
            o_ref[h] = jnp.where(band, table, NEG_INF)

    return pl.pallas_call(
        body, in_specs=[pl.BlockSpec(memory_space=pltpu.SMEM), pl.BlockSpec(memory_space=pltpu.VMEM)],
        out_specs=pl.BlockSpec(memory_space=pltpu.VMEM),
        out_shape=jax.ShapeDtypeStruct((n_heads, WINDOW, 2 * WINDOW), F32), name="bias_table",
        compiler_params=_cp())(rel_bias, bucket)


def _bias_grad(dbias, n_heads):
    bucket = jnp.asarray(_bucket_map())

    def body(db_ref, bk_ref, o_ref):
        bk = bk_ref[...]
        lane = lax.broadcasted_iota(jnp.int32, (1, LANES), 1)

        def step(bi, carry):
            row = jnp.zeros((1, LANES), F32)
            for h in range(n_heads):
                val = jnp.sum(jnp.where(bk == bi, db_ref[h], 0.0))
                row = jnp.where(lane == h, val, row)
            o_ref[pl.ds(bi, 1), :] = row
            return carry

        lax.fori_loop(0, N_BUCKETS, step, 0)

    return pl.pallas_call(
        body, in_specs=[pl.BlockSpec(memory_space=pltpu.VMEM), pl.BlockSpec(memory_space=pltpu.VMEM)],
        out_specs=pl.BlockSpec(memory_space=pltpu.VMEM),
        out_shape=jax.ShapeDtypeStruct((N_BUCKETS, LANES), F32), name="bias_grad",
        compiler_params=_cp())(dbias, bucket)


def _no_prev_block(n, grp):
    j = lax.broadcasted_iota(jnp.int32, (grp * WINDOW, 2 * WINDOW), 1)
    return (j < WINDOW) & (n == 0)


def _attn_probs(no_prev, q_ref, kp_ref, kc_ref, bias_ref, sink_ref, hh, grp):
    ad, wn = ATTN_HEAD_DIM, WINDOW
    ksl = slice(hh * ad, (hh + 1) * ad)
    kw = jnp.concatenate([kp_ref[:, ksl], kc_ref[:, ksl]], axis=0).astype(BF16)
    qs = jnp.concatenate([q_ref[:, (hh * grp + g) * ad:(hh * grp + g + 1) * ad] for g in range(grp)], axis=0).astype(BF16)
    s = _dot_nt(qs, kw) * (ad ** -0.5) + bias_ref[hh]
    s = jnp.where(no_prev, NEG_INF, s)
    rr = lax.broadcasted_iota(jnp.int32, (grp * wn, 1), 0) >> WINDOW_SHIFT
    sink = jnp.zeros((grp * wn, 1), F32)
    for g in range(grp):
        sink = jnp.where(rr == g, sink_ref[hh * grp + g], sink)
    m = jnp.maximum(jnp.max(s, axis=1, keepdims=True), sink)
    p = jnp.exp(s - m)
    es = jnp.exp(sink - m)
    inv = 1.0 / (jnp.sum(p, axis=1, keepdims=True) + es)
    return qs, kw, p * inv, es * inv


GATE_BLOCK = 512


def _attn_fwd(u, bias_g, sinks, off, w, name):
    t = u.shape[0]
    wn, ad, kvw, gb = WINDOW, ATTN_HEAD_DIM, KV_WIDTH, GATE_BLOCK
    grp = (w // ad) // ATTN_KV_HEADS
    nb = t // wn
    n_gb = w // gb
    cq, ck, cv, cg = off["b_q"] // w, off["b_k"] // kvw, off["b_v"] // kvw, off["b_g"] // gb

    def body(q_ref, kp_ref, kc_ref, vp_ref, vc_ref, bias_ref, sink_ref, *rest):
        g_refs, (o_ref, p_ref) = rest[:n_gb], rest[n_gb:]
        no_prev = _no_prev_block(pl.program_id(0), grp)
        for hh in range(ATTN_KV_HEADS):
            _, _, p, _ = _attn_probs(no_prev, q_ref, kp_ref, kc_ref, bias_ref, sink_ref, hh, grp)
            ksl = slice(hh * ad, (hh + 1) * ad)
            vw = jnp.concatenate([vp_ref[:, ksl], vc_ref[:, ksl]], axis=0).astype(BF16)
            o = _dot_nn(p.astype(BF16), vw)
            for g in range(grp):
                o_ref[:, (hh * grp + g) * ad:(hh * grp + g + 1) * ad] = o[g * wn:(g + 1) * wn]
        for i in range(n_gb):
            sl = slice(i * gb, (i + 1) * gb)
            silu, _ = _silu_parts(g_refs[i][...])
            p_ref[:, sl] = (o_ref[:, sl] * silu).astype(BF16)

    prev = lambda n: jnp.maximum(n - 1, 0)
    row = pl.BlockSpec((wn, w), lambda n: (n, 0))
    return pl.pallas_call(
        body, grid=(nb,),
        in_specs=[pl.BlockSpec((wn, w), lambda n: (n, cq)),
                  pl.BlockSpec((wn, kvw), lambda n: (prev(n), ck)), pl.BlockSpec((wn, kvw), lambda n: (n, ck)),
                  pl.BlockSpec((wn, kvw), lambda n: (prev(n), cv)), pl.BlockSpec((wn, kvw), lambda n: (n, cv)),
                  pl.BlockSpec((ATTN_KV_HEADS, grp * wn, 2 * wn), lambda n: (0, 0, 0)),
                  pl.BlockSpec(memory_space=pltpu.SMEM)]
        + [pl.BlockSpec((wn, gb), lambda n, i=i: (n, cg + i)) for i in range(n_gb)],
        out_specs=[row, row],
        out_shape=[jax.ShapeDtypeStruct((t, w), F32), jax.ShapeDtypeStruct((t, w), BF16)], name=name,
        compiler_params=_cp(("parallel",)))(u, u, u, u, u, bias_g, sinks, *([u] * n_gb))


def _attn_bwd(u, o, dp, bias_g, sinks, off, w, name):
    t = u.shape[0]
    wn, ad, kvw, gb = WINDOW, ATTN_HEAD_DIM, KV_WIDTH, GATE_BLOCK
    grp = (w // ad) // ATTN_KV_HEADS
    nb = t // wn
    n_gb = w // gb
    cq, ck, cv, cg = off["b_q"] // w, off["b_k"] // kvw, off["b_v"] // kvw, off["b_g"] // gb

    def body(q_ref, kp_ref, kc_ref, vp_ref, vc_ref, o_ref, dp_ref, bias_ref, sink_ref, *rest):
        g_refs = rest[:n_gb]
        dq_ref, dkc_ref, dkp_ref, dvc_ref, dvp_ref, dbias_ref, dsink_ref, dg_ref, do_ref = rest[n_gb:]
        n = pl.program_id(0)

        @pl.when(n == 0)
        def _():
            dbias_ref[...] = jnp.zeros_like(dbias_ref)
            dsink_ref[...] = jnp.zeros_like(dsink_ref)

        for i in range(n_gb):
            sl = slice(i * gb, (i + 1) * gb)
            silu, dsilu = _silu_parts(g_refs[i][...])
            dp_v = dp_ref[:, sl]
            do_ref[:, sl] = dp_v * silu
            dg_ref[:, sl] = (dp_v * o_ref[:, sl] * dsilu).astype(BF16)

        lane = lax.broadcasted_iota(jnp.int32, (1, LANES), 1)
        rr = lax.broadcasted_iota(jnp.int32, (grp * wn, 1), 0) >> WINDOW_SHIFT
        dsink_row = jnp.zeros((1, LANES), F32)
        no_prev = _no_prev_block(n, grp)
        for hh in range(ATTN_KV_HEADS):
            qs, kw, p, psink = _attn_probs(no_prev, q_ref, kp_ref, kc_ref, bias_ref, sink_ref, hh, grp)
            ksl = slice(hh * ad, (hh + 1) * ad)
            vw = jnp.concatenate([vp_ref[:, ksl], vc_ref[:, ksl]], axis=0).astype(BF16)
            hs = [slice((hh * grp + g) * ad, (hh * grp + g + 1) * ad) for g in range(grp)]
            dos = jnp.concatenate([do_ref[:, sl] for sl in hs], axis=0)
            os_ = jnp.concatenate([o_ref[:, sl] for sl in hs], axis=0)
            delta = jnp.sum(dos * os_, axis=1, keepdims=True)
            dos_b = dos.astype(BF16)
            dp = _dot_nt(dos_b, vw)
            ds = p * (dp - delta)
            dbias_ref[hh] += ds
            sd = psink * delta
            for g in range(grp):
                val = -jnp.sum(jnp.where(rr == g, sd, 0.0))
                dsink_row = jnp.where(lane == hh * grp + g, val, dsink_row)
            ds_b = (ds * (ad ** -0.5)).astype(BF16)
            dq = _dot_nn(ds_b, kw)
            for g in range(grp):
                dq_ref[:, hs[g]] = dq[g * wn:(g + 1) * wn].astype(BF16)
            dkw = _dot_tn(ds_b, qs)
            dvw = _dot_tn(p.astype(BF16), dos_b)
            dkp_ref[:, ksl] = dkw[:wn]
            dkc_ref[:, ksl] = dkw[wn:]
            dvp_ref[:, ksl] = dvw[:wn]
            dvc_ref[:, ksl] = dvw[wn:]
        dsink_ref[0:1, :] += dsink_row

    prev = lambda n: jnp.maximum(n - 1, 0)
    kv_out = pl.BlockSpec((wn, kvw), lambda n: (n, 0))
    row = pl.BlockSpec((wn, w), lambda n: (n, 0))
    return pl.pallas_call(
        body, grid=(nb,),
        in_specs=[pl.BlockSpec((wn, w), lambda n: (n, cq)),
                  pl.BlockSpec((wn, kvw), lambda n: (prev(n), ck)), pl.BlockSpec((wn, kvw), lambda n: (n, ck)),
                  pl.BlockSpec((wn, kvw), lambda n: (prev(n), cv)), pl.BlockSpec((wn, kvw), lambda n: (n, cv)),
                  row, row,
                  pl.BlockSpec((ATTN_KV_HEADS, grp * wn, 2 * wn), lambda n: (0, 0, 0)),
                  pl.BlockSpec(memory_space=pltpu.SMEM)]
        + [pl.BlockSpec((wn, gb), lambda n, i=i: (n, cg + i)) for i in range(n_gb)],
        out_specs=[row, kv_out, kv_out, kv_out, kv_out,
                   pl.BlockSpec((ATTN_KV_HEADS, grp * wn, 2 * wn), lambda n: (0, 0, 0)),
                   pl.BlockSpec((SUBLANES, LANES), lambda n: (0, 0)), row],
        out_shape=[jax.ShapeDtypeStruct((t, w), BF16)] + [jax.ShapeDtypeStruct((t, kvw), F32)] * 4
        + [jax.ShapeDtypeStruct((ATTN_KV_HEADS, grp * wn, 2 * wn), F32), jax.ShapeDtypeStruct((SUBLANES, LANES), F32),
           jax.ShapeDtypeStruct((t, w), BF16)],
        scratch_shapes=[pltpu.VMEM((wn, w), F32)],
        name=name, compiler_params=_cp(("arbitrary",)))(u, u, u, u, u, o, dp, bias_g, sinks, *([u] * n_gb))


def _kv_combine(cur, prv, name):
    t, kvw = cur.shape

    def body(i, nt, c_ref, p_ref, o_ref):
        nxt = jnp.where(i < nt - 1, p_ref[...], 0.0)
        o_ref[...] = (c_ref[...] + nxt).astype(BF16)

    return _ew(body, name, t, 1, kvw, [(cur, "tile", 0), (prv, "next", 0)], [BF16], tt=WINDOW)[0]


def _shift_down(h, tail, k, rows):
    tt = h.shape[0]
    out = pltpu.roll(h, k, 0)
    for r in range(k):
        out = jnp.where(rows == r, tail[tt - k + r:tt - k + r + 1, :], out)
    return out


def _shift_up(h, head, k, rows):
    tt = h.shape[0]
    out = pltpu.roll(h, tt - k, 0)
    for r in range(k):
        out = jnp.where(rows == tt - k + r, head[r:r + 1, :], out)
    return out


def _conv_fwd(u, conv_w, off, w, name):
    t = u.shape[0]
    wb = 512
    c = lambda nme: off[nme] // wb

    def body(i, nt, cb_ref, cc_ref, ccp_ref, cx_ref, cxp_ref, cg_ref, w_ref, p_ref):
        h = cc_ref[...] * cx_ref[...]
        hp = jnp.where(i > 0, ccp_ref[...] * cxp_ref[...], 0.0)
        rows = lax.broadcasted_iota(jnp.int32, h.shape, 0)
        y = w_ref[0:1, :] * _shift_down(h, hp, 2, rows) + w_ref[1:2, :] * _shift_down(h, hp, 1, rows) + w_ref[2:3, :] * h
        silu, _ = _silu_parts(cg_ref[...])
        p_ref[...] = (cb_ref[...] * y * silu).astype(BF16)

    return _ew(body, name, t, w // wb, wb,
               [(u, "tile", c("c_b")), (u, "tile", c("c_c")), (u, "prev", c("c_c")), (u, "tile", c("c_x")),
                (u, "prev", c("c_x")), (u, "tile", c("c_g")), (conv_w, "row", 0)], [BF16])[0]


def _conv_bwd(dp, u, conv_w, off, w, name):
    t = u.shape[0]
    wb = 512
    c = lambda nme: off[nme] // wb

    def body(i, nt, dp_ref, dpn_ref, cb_ref, cbn_ref, cg_ref, cgn_ref, cc_ref, ccp_ref, cx_ref, cxp_ref, w_ref,
             dcb_ref, dcc_ref, dcx_ref, dcg_ref, acc_ref):
        @pl.when(i == 0)
        def _():
            acc_ref[...] = jnp.zeros_like(acc_ref)

        cc, cx, cb = cc_ref[...], cx_ref[...], cb_ref[...]
        h = cc * cx
        hp = jnp.where(i > 0, ccp_ref[...] * cxp_ref[...], 0.0)
        rows = lax.broadcasted_iota(jnp.int32, h.shape, 0)
        h1 = _shift_down(h, hp, 1, rows)
        h2 = _shift_down(h, hp, 2, rows)
        w0, w1, w2 = w_ref[0:1, :], w_ref[1:2, :], w_ref[2:3, :]
        y = w0 * h2 + w1 * h1 + w2 * h
        silu, dsilu = _silu_parts(cg_ref[...])
        dp_v = dp_ref[...]
        dcg_ref[...] = (dp_v * cb * y * dsilu).astype(BF16)
        dcb_ref[...] = (dp_v * y * silu).astype(BF16)
        dy = dp_v * cb * silu
        silu_n, _ = _silu_parts(cgn_ref[...])
        dyn = jnp.where(i < nt - 1, dpn_ref[...] * cbn_ref[...] * silu_n, 0.0)
        dh = w2 * dy + w1 * _shift_up(dy, dyn, 1, rows) + w0 * _shift_up(dy, dyn, 2, rows)
        dcc_ref[...] = (dh * cx).astype(BF16)
        dcx_ref[...] = (dh * cc).astype(BF16)
        acc_ref[0:1, :] += jnp.sum(dy * h2, axis=0, keepdims=True)
        acc_ref[1:2, :] += jnp.sum(dy * h1, axis=0, keepdims=True)
        acc_ref[2:3, :] += jnp.sum(dy * h, axis=0, keepdims=True)

    return _ew(body, name, t, w // wb, wb,
               [(dp, "tile", 0), (dp, "next", 0), (u, "tile", c("c_b")), (u, "next", c("c_b")),
                (u, "tile", c("c_g")), (u, "next", c("c_g")), (u, "tile", c("c_c")), (u, "prev", c("c_c")),
                (u, "tile", c("c_x")), (u, "prev", c("c_x")), (conv_w, "row", 0)],
               [BF16] * 4, accs=[SUBLANES])


def _merge_fwd(u, ya, yb, yc, off, d, name):
    t = u.shape[0]
    wb = 512
    c = lambda nme: off[nme] // wb

    def body(i, nt, ma_ref, mb_ref, mc_ref, ya_ref, yb_ref, yc_ref, o_ref):
        o_ref[...] = (_sigmoid(ma_ref[...]) * ya_ref[...] + _sigmoid(mb_ref[...]) * yb_ref[...]
                      + _sigmoid(mc_ref[...]) * yc_ref[...]).astype(BF16)

    return _ew(body, name, t, d // wb, wb,
               [(u, "tile", c("m_a")), (u, "tile", c("m_b")), (u, "tile", c("m_c")),
                (ya, "tile", 0), (yb, "tile", 0), (yc, "tile", 0)], [BF16])[0]


def _merge_bwd(dm, u, ya, yb, yc, off, d, name):
    t = u.shape[0]
    wb = 512
    c = lambda nme: off[nme] // wb

    def body(i, nt, dm_ref, ma_ref, mb_ref, mc_ref, ya_ref, yb_ref, yc_ref, da_ref, db_ref, dc_ref, ga_ref, gb_ref, gc_ref):
        dm_v = dm_ref[...]
        for m_ref, y_ref, dy_ref, dg_ref in ((ma_ref, ya_ref, da_ref, ga_ref), (mb_ref, yb_ref, db_ref, gb_ref),
                                             (mc_ref, yc_ref, dc_ref, gc_ref)):
            s = _sigmoid(m_ref[...])
            dy_ref[...] = (dm_v * s).astype(BF16)
            dg_ref[...] = (dm_v * y_ref[...] * s * (1.0 - s)).astype(BF16)

    return _ew(body, name, t, d // wb, wb,
               [(dm, "tile", 0), (u, "tile", c("m_a")), (u, "tile", c("m_b")), (u, "tile", c("m_c")),
                (ya, "tile", 0), (yb, "tile", 0), (yc, "tile", 0)], [BF16] * 6)


def _lower_bounds(lb_param):
    def body(p_ref, o_ref):
        p = p_ref[...]
        e = jnp.exp(p - jnp.max(p, axis=0, keepdims=True))
        soft = e / jnp.sum(e, axis=0, keepdims=True)
        acc = jnp.zeros_like(soft[0:1])
        o_ref[0:1, :] = acc
        for l in range(1, DEPTH):
            acc = acc + soft[l:l + 1]
            o_ref[l:l + 1, :] = acc

    return pl.pallas_call(body, out_shape=jax.ShapeDtypeStruct(lb_param.shape, F32), name="lower_bounds",
                          compiler_params=_cp())(lb_param)


def _lower_bounds_bwd(lb_param, dlower):
    def body(p_ref, d_ref, o_ref):
        p = p_ref[...]
        e = jnp.exp(p - jnp.max(p, axis=0, keepdims=True))
        soft = e / jnp.sum(e, axis=0, keepdims=True)
        dl = d_ref[...]
        ds = [jnp.zeros_like(dl[0:1])]
        for j in range(1, DEPTH):
            acc = dl[j:j + 1]
            for l in range(j + 1, DEPTH):
                acc = acc + dl[l:l + 1]
            ds.append(acc)
        inner = ds[0] * soft[0:1]
        for j in range(1, DEPTH):
            inner = inner + ds[j] * soft[j:j + 1]
        for j in range(DEPTH):
            o_ref[j:j + 1, :] = soft[j:j + 1] * (ds[j] - inner)

    return pl.pallas_call(body, out_shape=jax.ShapeDtypeStruct(lb_param.shape, F32), name="lower_bounds_bwd",
                          compiler_params=_cp())(lb_param, dlower)


def _exchange(arrays, scatter, name, chips=False):
    n_arr = len(arrays)
    n_slot = N_DEV // 2 if chips else N_DEV

    def body(*refs):
        srcs, dsts = refs[:n_arr], refs[n_arr:2 * n_arr]
        send_sems, recv_sems, local_sems = refs[2 * n_arr:]
        me = (2 * lax.axis_index("x") + lax.axis_index("y") if chips
              else 4 * lax.axis_index("x") + 2 * lax.axis_index("y") + lax.axis_index("c"))
        copies = _peer_copies(srcs, dsts, send_sems, recv_sems, scatter, chips)
        for a in range(n_arr):
            copies.append(pltpu.make_async_copy(srcs[a].at[me] if scatter else srcs[a], dsts[a].at[me], local_sems.at[a]))
        for cp in copies:
            cp.start()
        for cp in copies:
            cp.wait()

    out_shape = [jax.ShapeDtypeStruct(a.shape if scatter else (n_slot,) + a.shape, a.dtype) for a in arrays]
    anyspec = pl.BlockSpec(memory_space=pl.ANY)
    res = pl.pallas_call(
        body, in_specs=[anyspec] * n_arr, out_specs=[anyspec] * n_arr, out_shape=out_shape,
        scratch_shapes=[pltpu.SemaphoreType.DMA((n_arr * (n_slot - 1),)), pltpu.SemaphoreType.DMA((n_arr * (n_slot - 1),)),
                        pltpu.SemaphoreType.DMA((n_arr,))],
        name=name)(*arrays)
    return list(res)


def _peer_copies(srcs, lands, send_sems, recv_sems, scatter, chips=False):
    x, y, c = lax.axis_index("x"), lax.axis_index("y"), lax.axis_index("c")
    flips = [k for k in range(1, N_DEV) if not (chips and k & 1)]
    slot = (lambda px, py, pc: 2 * px + py) if chips else (lambda px, py, pc: 4 * px + 2 * py + pc)
    copies = []
    for a in range(len(srcs)):
        for i, k in enumerate(flips):
            px = 1 - x if k & 4 else x
            py = 1 - y if k & 2 else y
            pc = 1 - c if k & 1 else c
            src = srcs[a].at[slot(px, py, pc)] if scatter else srcs[a]
            copies.append(pltpu.make_async_remote_copy(
                src_ref=src, dst_ref=lands[a].at[slot(x, y, c)],
                send_sem=send_sems.at[a * len(flips) + i], recv_sem=recv_sems.at[a * len(flips) + i],
                device_id=(px, py, pc), device_id_type=pl.DeviceIdType.MESH))
    return copies


def _gather_two_level(arrays, name):
    n_arr = len(arrays)
    per = N_DEV - 1

    def body(*refs):
        srcs, outs = refs[:n_arr], refs[n_arr:2 * n_arr]
        send_sems, recv_sems, local_sems = refs[2 * n_arr:]
        x, y, c = lax.axis_index("x"), lax.axis_index("y"), lax.axis_index("c")
        me, sibling = (x, y, c), (x, y, 1 - c)
        chips = [(1 - x, y), (x, 1 - y), (1 - x, 1 - y)]

        def copy(a, k, block, to, src=None):
            dst = outs[a].at[4 * block[0] + 2 * block[1] + block[2]]
            return pltpu.make_async_remote_copy(
                src_ref=dst if src is None else src, dst_ref=dst,
                send_sem=send_sems.at[a * per + k], recv_sem=recv_sems.at[a * per + k],
                device_id=to, device_id_type=pl.DeviceIdType.MESH)

        own, first, passed = [], [], []
        for a in range(n_arr):
            own.append(pltpu.make_async_copy(srcs[a], outs[a].at[4 * x + 2 * y + c], local_sems.at[a]))
            first.append(copy(a, 0, me, sibling, src=srcs[a]))
            first += [copy(a, 1 + j, me, (*chip, c), src=srcs[a]) for j, chip in enumerate(chips)]
        for cp in own + first:
            cp.start()
        for a in range(n_arr):
            for j, chip in enumerate(chips):
                copy(a, 1 + j, (*chip, c), me).wait_recv()
                passed.append(copy(a, 4 + j, (*chip, c), sibling))
                passed[-1].start()
        for a in range(n_arr):
            copy(a, 0, sibling, me).wait_recv()
            for j, chip in enumerate(chips):
                copy(a, 4 + j, (*chip, 1 - c), me).wait_recv()
        for cp in first + passed:
            cp.wait_send()
        for cp in own:
            cp.wait()

    anyspec = pl.BlockSpec(memory_space=pl.ANY)
    res = pl.pallas_call(
        body, in_specs=[anyspec] * n_arr, out_specs=[anyspec] * n_arr,
        out_shape=[jax.ShapeDtypeStruct((N_DEV,) + a.shape, a.dtype) for a in arrays],
        scratch_shapes=[pltpu.SemaphoreType.DMA((n_arr * per,)), pltpu.SemaphoreType.DMA((n_arr * per,)),
                        pltpu.SemaphoreType.DMA((n_arr,))],
        name=name)(*arrays)
    return list(res)


def _sibling_swap(arrays, name):
    n_arr = len(arrays)
    n_chip = N_DEV // 2

    def body(*refs):
        srcs, outs = refs[:n_arr], refs[n_arr:2 * n_arr]
        send_sems, recv_sems = refs[2 * n_arr:]
        x, y, c = lax.axis_index("x"), lax.axis_index("y"), lax.axis_index("c")
        copies = []
        for a in range(n_arr):
            for j in range(n_chip):
                copies.append(pltpu.make_async_remote_copy(
                    src_ref=srcs[a].at[2 * j + 1 - c], dst_ref=outs[a].at[j],
                    send_sem=send_sems.at[a * n_chip + j], recv_sem=recv_sems.at[a * n_chip + j],
                    device_id=(x, y, 1 - c), device_id_type=pl.DeviceIdType.MESH))
        for cp in copies:
            cp.start()
        for cp in copies:
            cp.wait()

    anyspec = pl.BlockSpec(memory_space=pl.ANY)
    res = pl.pallas_call(
        body, in_specs=[anyspec] * n_arr, out_specs=[anyspec] * n_arr,
        out_shape=[jax.ShapeDtypeStruct((n_chip,) + a.shape[1:], a.dtype) for a in arrays],
        scratch_shapes=[pltpu.SemaphoreType.DMA((n_arr * n_chip,)), pltpu.SemaphoreType.DMA((n_arr * n_chip,))],
        name=name)(*arrays)
    return list(res)


def _pair_sum(send, stage, core, name):
    _, r, c = send.shape
    n_chip = stage.shape[0]
    tr = _tile(r, 128)

    def body(core_ref, a_ref, b_ref, o_ref):
        o_ref[...] = a_ref[...] + b_ref[...]

    return pl.pallas_call(
        body,
        grid_spec=pltpu.PrefetchScalarGridSpec(
            num_scalar_prefetch=1, grid=(n_chip, r // tr),
            in_specs=[pl.BlockSpec((1, tr, c), lambda j, i, core_ref: (2 * j + core_ref[0], i, 0)),
                      pl.BlockSpec((1, tr, c), lambda j, i, core_ref: (j, i, 0))],
            out_specs=pl.BlockSpec((1, tr, c), lambda j, i, core_ref: (j, i, 0))),
        out_shape=jax.ShapeDtypeStruct(stage.shape, F32), name=name,
        compiler_params=_cp(("parallel", "parallel")))(core, send, stage)


_HBM_SPEC = pl.BlockSpec(memory_space=pltpu.HBM)
_SEM_SPEC = pl.BlockSpec(memory_space=pltpu.SEMAPHORE)
_ANY_SPEC = pl.BlockSpec(memory_space=pl.ANY)
_DATAFLOW = pltpu.SideEffectType.DATAFLOW_SIDE_EFFECTING


def _exchange_start(arrays, scatter, name, dep=None, chips=False):
    n_arr = len(arrays)
    n_slot = N_DEV // 2 if chips else N_DEV
    n_sem = n_arr * (n_slot - 1)
    me = (2 * lax.axis_index("x") + lax.axis_index("y") if chips
          else 4 * lax.axis_index("x") + 2 * lax.axis_index("y") + lax.axis_index("c"))
    lands = []
    for a in arrays:
        own = lax.dynamic_index_in_dim(a, me, 0, keepdims=False) if scatter else a
        shape = a.shape if scatter else (n_slot,) + a.shape
        lands.append(lax.dynamic_update_index_in_dim(lax.empty(shape, a.dtype), own, me, 0))
    dep_specs, dep_args = _dep_specs(dep)

    def body(*refs):
        srcs, lnds = refs[:n_arr], refs[n_arr:2 * n_arr]
        outs = refs[2 * n_arr + len(dep_args):]
        send_sems, recv_sems, token = outs[0], outs[1], outs[2 + 2 * n_arr]
        for cp in _peer_copies(srcs, lnds, send_sems, recv_sems, scatter, chips):
            cp.start()
        token[...] = jnp.zeros_like(token)

    thru = [pltpu.HBM(a.shape, a.dtype) for a in list(arrays) + lands]
    return pl.pallas_call(
        body, name=name,
        out_shape=(pltpu.SemaphoreType.DMA((n_sem,)), pltpu.SemaphoreType.DMA((n_sem,)), *thru,
                   jax.ShapeDtypeStruct((SUBLANES, LANES), F32)),
        in_specs=[_HBM_SPEC] * (2 * n_arr) + dep_specs,
        out_specs=(_SEM_SPEC, _SEM_SPEC, *[_HBM_SPEC] * (2 * n_arr), pl.BlockSpec(memory_space=pltpu.VMEM)),
        input_output_aliases={i: 2 + i for i in range(2 * n_arr)},
        compiler_params=pltpu.CompilerParams(has_side_effects=_DATAFLOW),
    )(*[pltpu.with_memory_space_constraint(a, pltpu.HBM) for a in list(arrays) + lands], *dep_args)


def _exchange_wait(started, scatter, name, after, chips=False):
    send_sems, recv_sems = started[0], started[1]
    thru = list(started[2:-1])
    n_arr = len(thru) // 2

    def body(*refs):
        srcs, lnds = refs[:n_arr], refs[n_arr:2 * n_arr]
        for cp in _peer_copies(srcs, lnds, refs[2 * n_arr], refs[2 * n_arr + 1], scatter, chips):
            cp.wait_send()
            cp.wait_recv()

    res = pl.pallas_call(
        body, name=name, out_shape=tuple(pltpu.HBM(a.shape, a.dtype) for a in thru),
        in_specs=[_HBM_SPEC] * (2 * n_arr) + [_SEM_SPEC, _SEM_SPEC, _ANY_SPEC],
        out_specs=tuple([_HBM_SPEC] * (2 * n_arr)),
        input_output_aliases={i: i for i in range(2 * n_arr)},
        compiler_params=pltpu.CompilerParams(has_side_effects=_DATAFLOW),
    )(*thru, send_sems, recv_sems, after)
    return list(res[n_arr:])


def _unshard_cols(g, name):
    nd, r, s = g.shape
    tr = _tile(r, 64)

    def body(i_ref, o_ref):
        for p in range(nd):
            o_ref[:, p * s:(p + 1) * s] = i_ref[p]

    return pl.pallas_call(
        body, grid=(r // tr,), in_specs=[pl.BlockSpec((nd, tr, s), lambda i: (0, i, 0))],
        out_specs=pl.BlockSpec((tr, nd * s), lambda i: (i, 0)),
        out_shape=jax.ShapeDtypeStruct((r, nd * s), g.dtype), name=name, compiler_params=_cp(("parallel",)))(g)


def _shard_cols(g, name):
    r, n = g.shape
    s = n // N_DEV
    tr = _tile(r, 64)

    def body(i_ref, o_ref):
        for p in range(N_DEV):
            o_ref[p] = i_ref[:, p * s:(p + 1) * s]

    return pl.pallas_call(
        body, grid=(r // tr,), in_specs=[pl.BlockSpec((tr, n), lambda i: (i, 0))],
        out_specs=pl.BlockSpec((N_DEV, tr, s), lambda i: (0, i, 0)),
        out_shape=jax.ShapeDtypeStruct((N_DEV, r, s), g.dtype), name=name, compiler_params=_cp(("parallel",)))(g)


def _slot_sum(slots, name):
    nd, r, c = slots.shape
    tr = _tile(r, 64)

    def body(s_ref, o_ref):
        acc = s_ref[0]
        for p in range(1, nd):
            acc = acc + s_ref[p]
        o_ref[...] = acc

    return pl.pallas_call(
        body, grid=(r // tr,), in_specs=[pl.BlockSpec((nd, tr, c), lambda i: (0, i, 0))],
        out_specs=pl.BlockSpec((tr, c), lambda i: (i, 0)),
        out_shape=jax.ShapeDtypeStruct((r, c), F32), name=name, compiler_params=_cp(("parallel",)))(slots)


def _adamw(w, g, m, v, name):
    r, c = w.shape
    tr = _tile(r, 256)
    c1 = 1.0 - ADAM_B1 ** ADAM_STEP
    c2 = 1.0 - ADAM_B2 ** ADAM_STEP

    def body(w_ref, g_ref, m_ref, v_ref, d_ref, nm_ref, nv_ref):
        gv = g_ref[...]
        nm = ADAM_B1 * m_ref[...] + (1.0 - ADAM_B1) * gv
        nv = ADAM_B2 * v_ref[...] + (1.0 - ADAM_B2) * (gv * gv)
        nm_ref[...] = nm
        nv_ref[...] = nv
        d_ref[...] = -ADAM_LR * ((nm / c1) / (jnp.sqrt(nv / c2) + ADAM_EPS) + ADAM_WD * w_ref[...])

    spec = pl.BlockSpec((tr, c), lambda i: (i, 0))
    return pl.pallas_call(
        body, grid=(r // tr,), in_specs=[spec] * 4, out_specs=[spec] * 3,
        out_shape=[jax.ShapeDtypeStruct((r, c), F32)] * 3, name=name, compiler_params=_cp(("parallel",)))(w, g, m, v)


def _forward_backward(x, target, weights_hook, grads_hook, lb_param, hgrn_norm_g, attn_sinks, rel_bias, ln_g, ln_b):
    t, d = x.shape
    w = d // 2
    off, n_in = _offsets(d)
    n_heads = w // ATTN_HEAD_DIM
    grp = n_heads // ATTN_KV_HEADS

    lower = _lower_bounds(lb_param)
    bias = _bias_table(rel_bias, n_heads)
    bias_g = bias.reshape(ATTN_KV_HEADS, grp * WINDOW, 2 * WINDOW)

    saved, weights = [], []
    xb = x.astype(BF16)
    for l in range(DEPTH):
        wl, token = weights_hook(l, x)
        weights.append(wl)
        s = {"x": x, "xb": xb}
        u = _mm_nn(xb, wl["w_in"], f"in_proj", dep=token)
        s["u"] = u
        lb_l, gain_l, cw_l = lower[l:l + 1], hgrn_norm_g[l:l + 1], wl["conv_w"]
        o_a, states, p_a = _hgrn_fwd(u, lb_l, gain_l, off, f"hgrn_fwd")
        o_b, p_b = _attn_fwd(u, bias_g, attn_sinks[l], off, w, f"attn_fwd")
        p_c = _conv_fwd(u, cw_l, off, w, f"conv_fwd")
        y_a = _mm_nn(p_a, wl["w_proj_hgrn"], f"proj_a", tn=2048)
        y_b = _mm_nn(p_b, wl["w_proj_attn"], f"proj_b", tn=2048)
        y_c = _mm_nn(p_c, wl["w_proj_conv"], f"proj_c", tn=2048)
        merged = _merge_fwd(u, y_a, y_b, y_c, off, d, f"merge_fwd")
        y = _mm_nn(merged, wl["w_out"], f"out_proj", tm=512, tn=2048)
        x, xb, xhat, rstd = _ln_fwd(x, y, ln_g[l:l + 1], ln_b[l:l + 1], f"ln_fwd")
        s.update(o_a=o_a, states=states, p_a=p_a, o_b=o_b, p_b=p_b, p_c=p_c, y_a=y_a, y_b=y_b, y_c=y_c,
                 merged=merged, xhat=xhat, rstd=rstd)
        saved.append(s)

    loss_acc, dx = _loss_head(x, target)

    d_ln, d_lower, d_gain, d_sink, d_conv = [None] * DEPTH, [None] * DEPTH, [None] * DEPTH, [None] * DEPTH, [None] * DEPTH
    dbias_total = None
    for l in reversed(range(DEPTH)):
        wl, s = weights[l], saved[l]
        u = s["u"]
        lb_l, gain_l, cw_l = lower[l:l + 1], hgrn_norm_g[l:l + 1], wl["conv_w"]
        dz, dzb, d_ln[l] = _ln_bwd(dx, s["xhat"], s["rstd"], ln_g[l:l + 1], f"ln_bwd")
        g_out = _mm_tn(s["merged"], dzb, f"g_out", tn=2048)
        dmerged = _mm_nt(dzb, wl["w_out"], f"d_merged", tk=2048)
        dya, dyb, dyc, dma, dmb, dmc = _merge_bwd(dmerged, u, s["y_a"], s["y_b"], s["y_c"], off, d, f"merge_bwd")
        g_pa = _mm_tn(s["p_a"], dya, f"g_proj_a", tn=2048)
        g_pb = _mm_tn(s["p_b"], dyb, f"g_proj_b", tn=2048)
        g_pc = _mm_tn(s["p_c"], dyc, f"g_proj_c", tn=2048)
        dpa = _mm_nt(dya, wl["w_proj_hgrn"], f"d_p_a", tk=2048)
        dpb = _mm_nt(dyb, wl["w_proj_attn"], f"d_p_b", tk=2048)
        dpc = _mm_nt(dyc, wl["w_proj_conv"], f"d_p_c", tk=2048)
        d_aq, d_af, d_ai, d_ag, acc_a = _hgrn_bwd(u, lb_l, gain_l, s["states"], s["o_a"], dpa, off, f"hgrn_bwd")
        d_lower[l], d_gain[l] = acc_a[0:1], acc_a[1:2]
        d_bq, dkc, dkp, dvc, dvp, dbias_l, d_sink[l], d_bg = _attn_bwd(
            u, s["o_b"], dpb, bias_g, attn_sinks[l], off, w, f"attn_bwd")
        d_bk = _kv_combine(dkc, dkp, f"k_combine")
        d_bv = _kv_combine(dvc, dvp, f"v_combine")
        dbias_total = dbias_l if dbias_total is None else dbias_total + dbias_l
        d_cb, d_cc, d_cx, d_cg, d_conv[l] = _conv_bwd(dpc, u, cw_l, off, w, f"conv_bwd")
        du = jnp.concatenate([d_aq, d_af, d_ai, d_ag, d_bq, d_bk, d_bv, d_bg, d_cb, d_cc, d_cx, d_cg, dma, dmb, dmc], axis=1)
        g_in = _mm_tn(s["xb"], du, f"g_in")
        token = grads_hook(l, {"w_in": g_in, "w_proj_hgrn": g_pa, "w_proj_attn": g_pb, "w_proj_conv": g_pc, "w_out": g_out})
        dx = _mm_nt(du, wl["w_in"], f"d_x", add=dz, add_scale=ALPHA, dep=token)

    d_lower_all = jnp.concatenate([a[0:1] for a in d_lower], axis=0)
    small = {
        "lb_param": _lower_bounds_bwd(lb_param, d_lower_all),
        "hgrn_norm_g": jnp.concatenate([a[0:1] for a in d_gain], axis=0),
        "attn_sinks": jnp.concatenate([a[0:1, :n_heads] for a in d_sink], axis=0),
        "conv_w": jnp.stack([a[0:3] for a in d_conv], axis=0),
        "rel_bias": _bias_grad(dbias_total.reshape(n_heads, WINDOW, 2 * WINDOW), n_heads)[:, :n_heads],
        "ln_g": jnp.concatenate([a[0:1] for a in d_ln], axis=0),
        "ln_b": jnp.concatenate([a[1:2] for a in d_ln], axis=0),
    }
    return loss_acc, dx, small


BIG = ("w_in", "w_proj_hgrn", "w_proj_attn", "w_proj_conv", "w_out")
SMALL = ("lb_param", "hgrn_norm_g", "attn_sinks", "conv_w", "rel_bias", "ln_g", "ln_b")
ORDER = ("w_in", "w_proj_hgrn", "w_proj_attn", "w_proj_conv", "w_out", "lb_param", "hgrn_norm_g", "attn_sinks",
         "conv_w", "rel_bias", "ln_g", "ln_b")


def _pack(parts):
    flat = jnp.concatenate([p.reshape(-1) for p in parts])
    n = flat.shape[0]
    unit = SUBLANES * LANES
    total = -(-n // unit) * unit
    return jnp.pad(flat, (0, total - n)).reshape(total // LANES, LANES)


def _unpack(packed, shapes):
    flat = packed.reshape(-1)
    out, o = [], 0
    for shp in shapes:
        n = int(np.prod(shp))
        out.append(flat[o:o + n].reshape(shp))
        o += n
    return out


def kernel(x, w_in, w_proj_hgrn, w_proj_attn, w_proj_conv, w_out, lb_param, hgrn_norm_g, attn_sinks, conv_w, rel_bias, ln_g, ln_b, loss_target, m_w_in, m_w_proj_hgrn, m_w_proj_attn, m_w_proj_conv, m_w_out, m_lb_param, m_hgrn_norm_g, m_attn_sinks, m_conv_w, m_rel_bias, m_ln_g, m_ln_b, v_w_in, v_w_proj_hgrn, v_w_proj_attn, v_w_proj_conv, v_w_out, v_lb_param, v_hgrn_norm_g, v_attn_sinks, v_conv_w, v_rel_bias, v_ln_g, v_ln_b):
    params = dict(w_in=w_in, w_proj_hgrn=w_proj_hgrn, w_proj_attn=w_proj_attn, w_proj_conv=w_proj_conv, w_out=w_out,
                  lb_param=lb_param, hgrn_norm_g=hgrn_norm_g, attn_sinks=attn_sinks, conv_w=conv_w, rel_bias=rel_bias,
                  ln_g=ln_g, ln_b=ln_b)
    mom_m = dict(w_in=m_w_in, w_proj_hgrn=m_w_proj_hgrn, w_proj_attn=m_w_proj_attn, w_proj_conv=m_w_proj_conv,
                 w_out=m_w_out, lb_param=m_lb_param, hgrn_norm_g=m_hgrn_norm_g, attn_sinks=m_attn_sinks,
                 conv_w=m_conv_w, rel_bias=m_rel_bias, ln_g=m_ln_g, ln_b=m_ln_b)
    mom_v = dict(w_in=v_w_in, w_proj_hgrn=v_w_proj_hgrn, w_proj_attn=v_w_proj_attn, w_proj_conv=v_w_proj_conv,
                 w_out=v_w_out, lb_param=v_lb_param, hgrn_norm_g=v_hgrn_norm_g, attn_sinks=v_attn_sinks,
                 conv_w=v_conv_w, rel_bias=v_rel_bias, ln_g=v_ln_g, ln_b=v_ln_b)
    d = x.shape[-1]
    me = 4 * lax.axis_index("x") + 2 * lax.axis_index("y") + lax.axis_index("c")

    def shards_of(l):
        return [params[n][l].astype(BF16) for n in BIG] + [conv_w[l]]

    gathers = {}

    def weights_hook(l, x_in):
        if l == 0:
            got = _gather_two_level(shards_of(0), "gather_weights_0")
        else:
            got = _exchange_wait(gathers.pop(l), False, f"gather_wait_{l}", x_in)
        token = None
        if l + 1 < DEPTH:
            gathers[l + 1] = _exchange_start(shards_of(l + 1), False, f"gather_start_{l + 1}", dep=got[0])
            token = gathers[l + 1][-1]
        wl = {
            "w_in": _unshard_cols(got[0], "unshard_w_in"),
            "w_proj_hgrn": _unshard_cols(got[1], "unshard_w_proj_hgrn"),
            "w_proj_attn": _unshard_cols(got[2], "unshard_w_proj_attn"),
            "w_proj_conv": _unshard_cols(got[3], "unshard_w_proj_conv"),
            "w_out": got[4].reshape(d, d),
            "conv_w": _unshard_cols(got[5], "unshard_conv_w"),
        }
        return wl, token

    grads = {n: [None] * DEPTH for n in BIG}
    scatters = {}

    def finish_scatter(l, after):
        got = _exchange_wait(scatters.pop(l), True, f"scatter_wait_{l}", after, chips=(l == 0))
        for n, slots in zip(BIG, got):
            grads[n][l] = _slot_sum(slots, f"sum_{n}_chips" if l == 0 else f"sum_{n}")
        return got[0]

    def grads_hook(l, g):
        send = [_shard_cols(g["w_in"], "shard_g_in"), _shard_cols(g["w_proj_hgrn"], "shard_g_proj_a"),
                _shard_cols(g["w_proj_attn"], "shard_g_proj_b"), _shard_cols(g["w_proj_conv"], "shard_g_proj_c"),
                g["w_out"].reshape(N_DEV, d // N_DEV, d)]
        dep = finish_scatter(l + 1, send[0]) if l + 1 < DEPTH else None
        if l == 0:
            core = lax.axis_index("c").astype(jnp.int32).reshape(1)
            staged = _sibling_swap(send, "pair_swap_grads")
            send = [_pair_sum(s, st, core, f"pair_sum_{n}") for n, s, st in zip(BIG, send, staged)]
        scatters[l] = _exchange_start(send, True, f"scatter_start_{l}", dep=dep, chips=(l == 0))
        return scatters[l][-1]

    loss_acc, dx, small = _forward_backward(
        x[0], loss_target[0], weights_hook, grads_hook, lb_param, hgrn_norm_g, attn_sinks, rel_bias, ln_g, ln_b)
    loss = lax.psum(0.5 * jnp.sum(loss_acc[0]) / d, ("x", "y", "c"))
    finish_scatter(0, dx)
    for n in BIG:
        grads[n] = jnp.stack(grads[n], axis=0)

    small_shapes = [small[n].shape for n in SMALL]
    packed = _pack([small[n] for n in SMALL])
    got = _exchange([packed], False, "gather_small_grads")[0]
    summed = _unpack(_slot_sum(got, "sum_small_grads"), small_shapes)
    for n, g in zip(SMALL, summed):
        grads[n] = g
    cs = conv_w.shape[-1]
    grads["conv_w"] = lax.dynamic_slice_in_dim(grads["conv_w"], me * cs, cs, axis=2)

    delta, new_m, new_v = {}, {}, {}
    for n in BIG:
        shp = params[n].shape
        flat = lambda a: a.reshape(-1, shp[-1])
        dl, nm, nv = _adamw(flat(params[n]), flat(grads[n]), flat(mom_m[n]), flat(mom_v[n]), f"adamw_{n}")
        delta[n], new_m[n], new_v[n] = dl.reshape(shp), nm.reshape(shp), nv.reshape(shp)
    shapes = [params[n].shape for n in SMALL]
    res = _adamw(_pack([params[n] for n in SMALL]), _pack([grads[n] for n in SMALL]),
                 _pack([mom_m[n] for n in SMALL]), _pack([mom_v[n] for n in SMALL]), "adamw_small")
    for dst, packed_res in zip((delta, new_m, new_v), res):
        for n, a in zip(SMALL, _unpack(packed_res, shapes)):
            dst[n] = a

    return (loss, dx[None], *[grads[n] for n in ORDER], *[delta[n] for n in ORDER],
            *[new_m[n] for n in ORDER], *[new_v[n] for n in ORDER])
```

```python
import functools
import math

import numpy as np
import jax
import jax.numpy as jnp
from jax import lax
from jax.experimental import pallas as pl
from jax.experimental.pallas import tpu as pltpu

F32 = jnp.float32
BF16 = jnp.bfloat16

N_DEV = 8
DEPTH = 4
HGRN_HEAD_DIM = 128
HGRN_CHUNK = 64
ATTN_HEAD_DIM = 64
ATTN_KV_HEADS = 4
KV_WIDTH = ATTN_KV_HEADS * ATTN_HEAD_DIM
WINDOW = 128
WINDOW_SHIFT = 7
N_BUCKETS = 32
MAX_DISTANCE = 128
ALPHA = (2.0 * DEPTH) ** 0.25
LN_EPS = 1e-5
RMS_EPS = 1e-6
ADAM_LR = 0.001
ADAM_B1 = 0.9
ADAM_B2 = 0.999
ADAM_EPS = 1e-08
ADAM_WD = 0.01
ADAM_STEP = 10

LANES = 128
SUBLANES = 8
VMEM_LIMIT = 56 << 20
NEG_INF = float("-inf")


def _offsets(d_model):
    w = d_model // 2
    sizes = (w, w, w, w, w, KV_WIDTH, KV_WIDTH, w, w, w, w, w, d_model, d_model, d_model)
    names = ("a_q", "a_f", "a_i", "a_g", "b_q", "b_k", "b_v", "b_g", "c_b", "c_c", "c_x", "c_g", "m_a", "m_b", "m_c")
    off, o = {}, 0
    for n, s in zip(names, sizes):
        off[n] = o
        o += s
    return off, o


def _tile(n, pref):
    t = min(pref, n)
    while n % t:
        t //= 2
    return t


def _cp(sem=None, vmem=VMEM_LIMIT):
    return pltpu.CompilerParams(dimension_semantics=sem, vmem_limit_bytes=vmem)


def _sigmoid(x):
    return 1.0 / (1.0 + jnp.exp(-x))


def _dot_nn(a, b):
    return jnp.dot(a, b, preferred_element_type=F32)


def _dot_nt(a, b):
    return lax.dot_general(a, b, (((1,), (1,)), ((), ())), preferred_element_type=F32)


def _dot_tn(a, b):
    return lax.dot_general(a, b, (((0,), (0,)), ((), ())), preferred_element_type=F32)


def _dep_specs(dep):
    return ([], []) if dep is None else ([pl.BlockSpec(memory_space=pl.ANY)], [dep])


def _mm_nn(a, b, name, out_dtype=F32, tm=1024, tn=1536, dep=None):
    m, k = a.shape
    _, n = b.shape
    tm, tn = _tile(m, tm), _tile(n, tn)
    dep_specs, dep_args = _dep_specs(dep)

    def body(a_ref, b_ref, *rest):
        o_ref = rest[-1]
        o_ref[...] = _dot_nn(a_ref[...], b_ref[...]).astype(o_ref.dtype)

    return pl.pallas_call(
        body, grid=(n // tn, m // tm),
        in_specs=[pl.BlockSpec((tm, k), lambda j, i: (i, 0)), pl.BlockSpec((k, tn), lambda j, i: (0, j))] + dep_specs,
        out_specs=pl.BlockSpec((tm, tn), lambda j, i: (i, j)),
        out_shape=jax.ShapeDtypeStruct((m, n), out_dtype), name=name,
        compiler_params=_cp(("parallel", "parallel")))(a, b, *dep_args)


def _mm_nt(a, b, name, tm=1024, tk=1536, add=None, add_scale=1.0, dep=None):
    m, k = a.shape
    n, _ = b.shape
    tm, tk = _tile(m, tm), _tile(k, tk)
    has_add = add is not None
    dep_specs, dep_args = _dep_specs(dep)

    def body(*refs):
        if has_add:
            a_ref, b_ref, add_ref = refs[:3]
        else:
            a_ref, b_ref = refs[:2]
        o_ref = refs[-1]
        if k == tk:
            prod = _dot_nt(a_ref[...], b_ref[...])
            o_ref[...] = prod + add_ref[...] * add_scale if has_add else prod
            return

        @pl.when(pl.program_id(1) == 0)
        def _():
            if has_add:
                o_ref[...] = add_ref[...] * add_scale
            else:
                o_ref[...] = jnp.zeros_like(o_ref)

        o_ref[...] += _dot_nt(a_ref[...], b_ref[...])

    in_specs = [pl.BlockSpec((tm, tk), lambda i, kk: (i, kk)), pl.BlockSpec((n, tk), lambda i, kk: (0, kk))]
    args = [a, b]
    if has_add:
        in_specs.append(pl.BlockSpec((tm, n), lambda i, kk: (i, 0)))
        args.append(add)
    in_specs += dep_specs
    args += dep_args
    return pl.pallas_call(
        body, grid=(m // tm, k // tk), in_specs=in_specs,
        out_specs=pl.BlockSpec((tm, n), lambda i, kk: (i, 0)),
        out_shape=jax.ShapeDtypeStruct((m, n), F32), name=name,
        compiler_params=_cp(("parallel", "arbitrary")))(*args)


def _mm_nt_cols(a, b, name, tm=1024, tn=1536, dep=None):
    m, k = a.shape
    n, _ = b.shape
    tm, tn = _tile(m, tm), _tile(n, tn)
    dep_specs, dep_args = _dep_specs(dep)

    def body(a_ref, b_ref, *rest):
        rest[-1][...] = _dot_nt(a_ref[...], b_ref[...])

    return pl.pallas_call(
        body, grid=(n // tn, m // tm),
        in_specs=[pl.BlockSpec((tm, k), lambda j, i: (i, 0)), pl.BlockSpec((tn, k), lambda j, i: (j, 0))] + dep_specs,
        out_specs=pl.BlockSpec((tm, tn), lambda j, i: (i, j)),
        out_shape=jax.ShapeDtypeStruct((m, n), F32), name=name,
        compiler_params=_cp(("parallel", "parallel")))(a, b, *dep_args)


def _mm_nn_acc(a, b, name, tm=1024, tk=1536, add=None, add_scale=1.0, dep=None):
    m, k = a.shape
    _, n = b.shape
    tm, tk = _tile(m, tm), _tile(k, tk)
    dep_specs, dep_args = _dep_specs(dep)

    def body(a_ref, b_ref, add_ref, *rest):
        o_ref = rest[-1]

        @pl.when(pl.program_id(1) == 0)
        def _():
            o_ref[...] = add_ref[...] * add_scale

        o_ref[...] += _dot_nn(a_ref[...], b_ref[...])

    return pl.pallas_call(
        body, grid=(m // tm, k // tk),
        in_specs=[pl.BlockSpec((tm, tk), lambda i, kk: (i, kk)), pl.BlockSpec((tk, n), lambda i, kk: (kk, 0)),
                  pl.BlockSpec((tm, n), lambda i, kk: (i, 0))] + dep_specs,
        out_specs=pl.BlockSpec((tm, n), lambda i, kk: (i, 0)),
        out_shape=jax.ShapeDtypeStruct((m, n), F32), name=name,
        compiler_params=_cp(("parallel", "arbitrary")))(a, b, add, *dep_args)


def _mm_tn_rows(a, b, name, tt=1024, tr=1536):
    t, k = a.shape
    _, n = b.shape
    tt, tr = _tile(t, tt), _tile(k, tr)

    def body(a_ref, b_ref, o_ref):
        @pl.when(pl.program_id(1) == 0)
        def _():
            o_ref[...] = jnp.zeros_like(o_ref)

        o_ref[...] += _dot_tn(a_ref[...], b_ref[...])

    return pl.pallas_call(
        body, grid=(k // tr, t // tt),
        in_specs=[pl.BlockSpec((tt, tr), lambda j, s: (s, j)), pl.BlockSpec((tt, n), lambda j, s: (s, 0))],
        out_specs=pl.BlockSpec((tr, n), lambda j, s: (j, 0)),
        out_shape=jax.ShapeDtypeStruct((k, n), F32), name=name,
        compiler_params=_cp(("parallel", "arbitrary")))(a, b)


def _mm_tn(a, b, name, tt=1024, tn=1536):
    t, k = a.shape
    _, n = b.shape
    tt, tn = _tile(t, tt), _tile(n, tn)

    def body(a_ref, b_ref, o_ref):
        @pl.when(pl.program_id(1) == 0)
        def _():
            o_ref[...] = jnp.zeros_like(o_ref)

        o_ref[...] += _dot_tn(a_ref[...], b_ref[...])

    return pl.pallas_call(
        body, grid=(n // tn, t // tt),
        in_specs=[pl.BlockSpec((tt, k), lambda j, s: (s, 0)), pl.BlockSpec((tt, tn), lambda j, s: (s, j))],
        out_specs=pl.BlockSpec((k, tn), lambda j, s: (0, j)),
        out_shape=jax.ShapeDtypeStruct((k, n), F32), name=name,
        compiler_params=_cp(("parallel", "arbitrary")))(a, b)


def _ew(body, name, t, ncol, wb, ins, outs, accs=(), tt=512):
    tt = _tile(t, tt)
    nt = t // tt
    in_specs, args = [], []
    for arr, kind, coff in ins:
        if kind == "tile":
            spec = pl.BlockSpec((tt, wb), lambda j, i, c=coff: (i, c + j))
        elif kind == "prev":
            spec = pl.BlockSpec((tt, wb), lambda j, i, c=coff: (jnp.maximum(i - 1, 0), c + j))
        elif kind == "next":
            spec = pl.BlockSpec((tt, wb), lambda j, i, c=coff: (jnp.minimum(i + 1, nt - 1), c + j))
        else:
            spec = pl.BlockSpec((arr.shape[0], wb), lambda j, i, c=coff: (0, c + j))
        in_specs.append(spec)
        args.append(arr)
    out_specs = [pl.BlockSpec((tt, wb), lambda j, i: (i, j)) for _ in outs]
    out_shape = [jax.ShapeDtypeStruct((t, ncol * wb), d) for d in outs]
    for r in accs:
        out_specs.append(pl.BlockSpec((r, wb), lambda j, i: (0, j)))
        out_shape.append(jax.ShapeDtypeStruct((r, ncol * wb), F32))

    def kern(*refs):
        body(pl.program_id(1), nt, *refs)

    res = pl.pallas_call(
        kern, grid=(ncol, nt), in_specs=in_specs, out_specs=out_specs, out_shape=out_shape, name=name,
        compiler_params=_cp(("parallel", "arbitrary")))(*args)
    return res


def _silu_parts(x):
    s = _sigmoid(x)
    return x * s, s + x * s * (1.0 - s)


def _ln_fwd(x, y, g, b, name):
    t, d = x.shape
    tt = _tile(t, 256)

    def body(x_ref, y_ref, g_ref, b_ref, o_ref, ob_ref, xh_ref, r_ref):
        z = ALPHA * x_ref[...] + y_ref[...]
        mu = jnp.mean(z, axis=1, keepdims=True)
        zc = z - mu
        var = jnp.mean(zc * zc, axis=1, keepdims=True)
        rstd = lax.rsqrt(var + LN_EPS)
        xh = zc * rstd
        o = xh * g_ref[...] + b_ref[...]
        o_ref[...] = o
        ob_ref[...] = o.astype(BF16)
        xh_ref[...] = xh
        r_ref[...] = rstd

    row = pl.BlockSpec((tt, d), lambda i: (i, 0))
    vec = pl.BlockSpec((1, d), lambda i: (0, 0))
    return pl.pallas_call(
        body, grid=(t // tt,), in_specs=[row, row, vec, vec],
        out_specs=[row, row, row, pl.BlockSpec((tt, 1), lambda i: (i, 0))],
        out_shape=[jax.ShapeDtypeStruct((t, d), F32), jax.ShapeDtypeStruct((t, d), BF16),
                   jax.ShapeDtypeStruct((t, d), F32), jax.ShapeDtypeStruct((t, 1), F32)],
        name=name, compiler_params=_cp(("parallel",)))(x, y, g, b)


def _ln_bwd(dout, xhat, rstd, g, name):
    t, d = dout.shape
    tt = _tile(t, 256)

    def body(do_ref, xh_ref, r_ref, g_ref, dz_ref, dzb_ref, acc_ref):
        @pl.when(pl.program_id(0) == 0)
        def _():
            acc_ref[...] = jnp.zeros_like(acc_ref)

        do = do_ref[...]
        xh = xh_ref[...]
        dxh = do * g_ref[...]
        m1 = jnp.mean(dxh, axis=1, keepdims=True)
        m2 = jnp.mean(dxh * xh, axis=1, keepdims=True)
        dz = r_ref[...] * (dxh - m1 - xh * m2)
        dz_ref[...] = dz
        dzb_ref[...] = dz.astype(BF16)
        acc_ref[0:1, :] += jnp.sum(do * xh, axis=0, keepdims=True)
        acc_ref[1:2, :] += jnp.sum(do, axis=0, keepdims=True)

    row = pl.BlockSpec((tt, d), lambda i: (i, 0))
    return pl.pallas_call(
        body, grid=(t // tt,),
        in_specs=[row, row, pl.BlockSpec((tt, 1), lambda i: (i, 0)), pl.BlockSpec((1, d), lambda i: (0, 0))],
        out_specs=[row, row, pl.BlockSpec((SUBLANES, d), lambda i: (0, 0))],
        out_shape=[jax.ShapeDtypeStruct((t, d), F32), jax.ShapeDtypeStruct((t, d), BF16),
                   jax.ShapeDtypeStruct((SUBLANES, d), F32)],
        name=name, compiler_params=_cp(("arbitrary",)))(dout, xhat, rstd, g)


def _loss_head(y, target):
    t, d = y.shape
    tt = _tile(t, 256)

    def body(y_ref, t_ref, acc_ref, dy_ref):
        @pl.when(pl.program_id(0) == 0)
        def _():
            acc_ref[...] = jnp.zeros_like(acc_ref)

        err = y_ref[...] - t_ref[...]
        dy_ref[...] = err * (1.0 / d)
        acc_ref[0:1, :] += jnp.sum(err * err, axis=0, keepdims=True)

    row = pl.BlockSpec((tt, d), lambda i: (i, 0))
    acc, dy = pl.pallas_call(
        body, grid=(t // tt,), in_specs=[row, row],
        out_specs=[pl.BlockSpec((SUBLANES, d), lambda i: (0, 0)), row],
        out_shape=[jax.ShapeDtypeStruct((SUBLANES, d), F32), jax.ShapeDtypeStruct((t, d), F32)],
        name="loss_head", compiler_params=_cp(("arbitrary",)))(y, target)
    return acc, dy


def _tri(lower):
    r = lax.broadcasted_iota(jnp.int32, (HGRN_CHUNK, HGRN_CHUNK), 0)
    c = lax.broadcasted_iota(jnp.int32, (HGRN_CHUNK, HGRN_CHUNK), 1)
    return jnp.where((r >= c) if lower else (r <= c), 1.0, 0.0).astype(BF16)


def _exact_tri_matmul(tri, x):
    hi = x.astype(BF16)
    r1 = x - hi.astype(F32)
    mid = r1.astype(BF16)
    lo = (r1 - mid.astype(F32)).astype(BF16)
    return _dot_nn(tri, hi) + _dot_nn(tri, mid) + _dot_nn(tri, lo)


def _hgrn_gates(q_raw, fl, lb):
    sq = _sigmoid(q_raw)
    qf = q_raw * sq * (HGRN_HEAD_DIM ** -0.5)
    sg = _sigmoid(fl)
    f = lb + (1.0 - lb) * sg
    return qf, sq, sg, f


HGRN_SUB = 16
HGRN_NSUB = HGRN_CHUNK // HGRN_SUB
HGRN_HEADS_PER_STEP = 8
HGRN_HEADS_PER_STEP_BWD = 8


def _diag_rows(r):
    return (r // SUBLANES) * SUBLANES


def _heads(x):
    hd = HGRN_HEAD_DIM
    return [x[:, i * hd:(i + 1) * hd] for i in range(x.shape[1] // hd)]


def _per_head(fn, *xs):
    split = [x if isinstance(x, (list, tuple)) else _heads(x) for x in xs]
    return jnp.concatenate([fn(*hs) for hs in zip(*split)], axis=1)


def _head_lane_sum(x):
    return _per_head(lambda h: jnp.broadcast_to(jnp.sum(h, axis=1, keepdims=True), h.shape), x)


def _hgrn_intra_fwd(qf, k, v, b):
    ch, sub, wd = HGRN_CHUNK, HGRN_SUB, qf.shape[1]
    tl = lax.broadcasted_iota(jnp.int32, (sub, wd), 0)
    blocks = []
    for m in range(HGRN_NSUB):
        rs = slice(m * sub, (m + 1) * sub)
        bm, qm, km, vm = b[rs], qf[rs], k[rs], v[rs]
        parts = {0: jnp.zeros((sub, wd), F32), SUBLANES: jnp.zeros((sub - SUBLANES, wd), F32)}
        for r in range(sub):
            lo = _diag_rows(r)
            e = jnp.exp(jnp.where(tl[lo:] >= r, bm[lo:] - bm[r:r + 1], NEG_INF))
            parts[lo] = parts[lo] + _head_lane_sum(qm[lo:] * e * km[r:r + 1]) * vm[r:r + 1]
        blocks.append(parts[0] + jnp.concatenate([jnp.zeros((SUBLANES, wd), F32), parts[SUBLANES]], axis=0))
    acc = jnp.concatenate(blocks, axis=0)
    for j in range(HGRN_NSUB - 1):
        lo = sub * (j + 1)
        c = b[lo - 1:lo, :]
        qj = (qf[lo:] * jnp.exp(b[lo:] - c)).astype(BF16)
        kj = (k[lo - sub:lo] * jnp.exp(c - b[lo - sub:lo])).astype(BF16)
        vj = v[lo - sub:lo].astype(BF16)
        contrib = _per_head(lambda q_, k_, v_: _dot_nn(_dot_nt(q_, k_).astype(BF16), v_), qj, kj, vj)
        acc = acc + jnp.concatenate([jnp.zeros((lo, wd), F32), contrib], axis=0)
    return acc


def _hgrn_intra_bwd(qf, k, v, b, do_v):
    ch, sub, wd = HGRN_CHUNK, HGRN_SUB, qf.shape[1]
    tl = lax.broadcasted_iota(jnp.int32, (sub, wd), 0)
    dq_blocks, dk_blocks, dv_blocks = [], [], []
    for m in range(HGRN_NSUB):
        rs = slice(m * sub, (m + 1) * sub)
        bm, qm, km, vm, dom = b[rs], qf[rs], k[rs], v[rs], do_v[rs]
        parts = {0: jnp.zeros((sub, wd), F32), SUBLANES: jnp.zeros((sub - SUBLANES, wd), F32)}
        dk_parts = {sub: jnp.zeros((sub, wd), F32), SUBLANES: jnp.zeros((SUBLANES, wd), F32)}
        dv_parts = {sub: jnp.zeros((sub, wd), F32), SUBLANES: jnp.zeros((SUBLANES, wd), F32)}
        for r in range(sub):
            lo = _diag_rows(r)
            b_r, k_r, v_r, q_r, do_r = bm[r:r + 1], km[r:r + 1], vm[r:r + 1], qm[r:r + 1], dom[r:r + 1]
            e = jnp.exp(jnp.where(tl[lo:] >= r, bm[lo:] - b_r, NEG_INF))
            parts[lo] = parts[lo] + _head_lane_sum(dom[lo:] * v_r) * (k_r * e)
            hi = lo + SUBLANES
            e2 = jnp.exp(jnp.where(tl[:hi] <= r, b_r - bm[:hi], NEG_INF))
            qe2 = q_r * e2
            dk_parts[hi] = dk_parts[hi] + _head_lane_sum(vm[:hi] * do_r) * qe2
            dv_parts[hi] = dv_parts[hi] + _head_lane_sum(km[:hi] * qe2) * do_r
        pad = jnp.zeros((SUBLANES, wd), F32)
        dq_blocks.append(parts[0] + jnp.concatenate([pad, parts[SUBLANES]], axis=0))
        dk_blocks.append(dk_parts[sub] + jnp.concatenate([dk_parts[SUBLANES], pad], axis=0))
        dv_blocks.append(dv_parts[sub] + jnp.concatenate([dv_parts[SUBLANES], pad], axis=0))
    dq = jnp.concatenate(dq_blocks, axis=0)
    dk = jnp.concatenate(dk_blocks, axis=0)
    dv = jnp.concatenate(dv_blocks, axis=0)
    do_b, v_b = do_v.astype(BF16), v.astype(BF16)
    dk_off, dv_off = [], []
    for j in range(HGRN_NSUB - 1):
        lo = sub * (j + 1)
        c = b[lo - 1:lo, :]
        eq = jnp.exp(b[lo:] - c)
        ek = jnp.exp(c - b[lo - sub:lo])
        qj = (qf[lo:] * eq).astype(BF16)
        kj = (k[lo - sub:lo] * ek).astype(BF16)
        doj, vj = do_b[lo:], v_b[lo - sub:lo]
        dq_j = _per_head(lambda do_, v_, k_: _dot_nn(_dot_nt(do_, v_).astype(BF16), k_), doj, vj, kj)
        dk_j = _per_head(lambda do_, v_, q_: _dot_nn(_dot_nt(v_, do_).astype(BF16), q_), doj, vj, qj)
        dv_j = _per_head(lambda do_, k_, q_: _dot_nn(_dot_nt(k_, q_).astype(BF16), do_), doj, kj, qj)
        dq = dq + jnp.concatenate([jnp.zeros((lo, wd), F32), dq_j * eq], axis=0)
        dk_off.append(dk_j * ek)
        dv_off.append(dv_j)
    zero = jnp.zeros((sub, wd), F32)
    dk = dk + jnp.concatenate(dk_off + [zero], axis=0)
    dv = dv + jnp.concatenate(dv_off + [zero], axis=0)
    return dq, dk, dv


def _head_rms(o):
    return lax.rsqrt(_head_lane_sum(o * o) * (1.0 / HGRN_HEAD_DIM) + RMS_EPS)


def _hgrn_fwd(u, lb, gain, off, name):
    t = u.shape[0]
    w = lb.shape[1]
    hd, ch, hp = HGRN_HEAD_DIM, HGRN_CHUNK, HGRN_HEADS_PER_STEP
    nh, nc = w // hd, t // ch
    wb = hp * hd
    cq, cf, cv, cg = off["a_q"] // wb, off["a_f"] // wb, off["a_i"] // wb, off["a_g"] // wb

    def body(q_ref, f_ref, v_ref, g_ref, lb_ref, gain_ref, o_ref, st_ref, p_ref, state):
        @pl.when(pl.program_id(1) == 0)
        def _():
            state[...] = jnp.zeros_like(state)

        sts = [state[i] for i in range(hp)]
        qf, _, _, f = _hgrn_gates(q_ref[...], f_ref[...], lb_ref[...])
        k = 1.0 - f
        v = v_ref[...]
        b = _exact_tri_matmul(_tri(True), jnp.log(f))
        inter = _per_head(lambda qa_, st_: _dot_nt(qa_, st_.astype(BF16)), (qf * jnp.exp(b)).astype(BF16), sts)
        o = inter + _hgrn_intra_fwd(qf, k, v, b)
        o_ref[...] = o
        silu, _ = _silu_parts(g_ref[...])
        p_ref[...] = (o * _head_rms(o) * gain_ref[...] * silu).astype(BF16)
        b_end = b[ch - 1:ch, :]
        a_end = _heads(jnp.exp(b_end))
        kd = _heads((k * jnp.exp(b_end - b)).astype(BF16))
        v_b = _heads(v.astype(BF16))
        for i in range(hp):
            st_ref[i, 0] = sts[i]
            state[i] = sts[i] * a_end[i] + _dot_tn(v_b[i], kd[i])

    return pl.pallas_call(
        body, grid=(nh // hp, nc),
        in_specs=[pl.BlockSpec((ch, wb), lambda h, n: (n, cq + h)),
                  pl.BlockSpec((ch, wb), lambda h, n: (n, cf + h)),
                  pl.BlockSpec((ch, wb), lambda h, n: (n, cv + h)),
                  pl.BlockSpec((ch, wb), lambda h, n: (n, cg + h)),
                  pl.BlockSpec((1, wb), lambda h, n: (0, h)),
                  pl.BlockSpec((1, wb), lambda h, n: (0, h))],
        out_specs=[pl.BlockSpec((ch, wb), lambda h, n: (n, h)),
                   pl.BlockSpec((hp, 1, hd, hd), lambda h, n: (h, n, 0, 0)),
                   pl.BlockSpec((ch, wb), lambda h, n: (n, h))],
        out_shape=[jax.ShapeDtypeStruct((t, w), F32), jax.ShapeDtypeStruct((nh, nc, hd, hd), F32),
                   jax.ShapeDtypeStruct((t, w), BF16)],
        scratch_shapes=[pltpu.VMEM((hp, hd, hd), F32)],
        name=name, compiler_params=_cp(("parallel", "arbitrary")))(u, u, u, u, lb, gain)


def _hgrn_bwd(u, lb, gain, states, o, dp, off, name):
    t = u.shape[0]
    w = lb.shape[1]
    hd, ch, hp = HGRN_HEAD_DIM, HGRN_CHUNK, HGRN_HEADS_PER_STEP_BWD
    nh, nc = w // hd, t // ch
    wb = hp * hd
    cq, cf, cv, cg = off["a_q"] // wb, off["a_f"] // wb, off["a_i"] // wb, off["a_g"] // wb

    def body(q_ref, f_ref, v_ref, g_ref, o_ref, dp_ref, st_ref, lb_ref, gain_ref,
             dq_ref, df_ref, dv_ref, dg_ref, dlb_ref, dstate):
        @pl.when(pl.program_id(1) == 0)
        def _():
            dstate[...] = jnp.zeros_like(dstate)
            dlb_ref[...] = jnp.zeros_like(dlb_ref)

        silu, dsilu = _silu_parts(g_ref[...])
        o_v, dp_v, gain_row = o_ref[...], dp_ref[...], gain_ref[...]
        rms = _head_rms(o_v)
        nrm = o_v * rms
        dg_ref[...] = (dp_v * nrm * gain_row * dsilu).astype(BF16)
        dlb_ref[1:2, :] += jnp.sum(dp_v * nrm * silu, axis=0, keepdims=True)
        dn = dp_v * gain_row * silu
        do_v = rms * (dn - nrm * (_head_lane_sum(dn * nrm) * (1.0 / HGRN_HEAD_DIM)))

        rows = lax.broadcasted_iota(jnp.int32, (ch, wb), 0)
        lb_row = lb_ref[...]
        q_raw = q_ref[...]
        qf, sq, sg, f = _hgrn_gates(q_raw, f_ref[...], lb_row)
        k = 1.0 - f
        b = _exact_tri_matmul(_tri(True), jnp.log(f))
        a = jnp.exp(b)
        b_end = b[ch - 1:ch, :]
        a_end = jnp.exp(b_end)
        to_end = jnp.exp(b_end - b)
        v = v_ref[...]
        st0 = [st_ref[i, 0] for i in range(hp)]
        ds = [dstate[i] for i in range(hp)]
        st0_b = [s_.astype(BF16) for s_ in st0]
        ds_b = [s_.astype(BF16) for s_ in ds]
        do_b, v_b, kd_b, qa_b = do_v.astype(BF16), v.astype(BF16), (k * to_end).astype(BF16), (qf * a).astype(BF16)

        dq_inter = a * _per_head(_dot_nn, do_b, st0_b)
        dk_end = to_end * _per_head(_dot_nn, v_b, ds_b)
        dv_end = _per_head(_dot_nt, kd_b, ds_b)
        a_end_h = _heads(a_end)
        st_end = [st0[i] * a_end_h[i] + _dot_tn(_heads(v_b)[i], _heads(kd_b)[i]) for i in range(hp)]
        db_end = jnp.concatenate([jnp.sum(ds[i] * st_end[i], axis=0, keepdims=True) for i in range(hp)], axis=1)
        ds_new = [ds[i] * a_end_h[i] + _dot_tn(_heads(do_b)[i], _heads(qa_b)[i]) for i in range(hp)]

        dq_intra, dk_intra, dv_intra = _hgrn_intra_bwd(qf, k, v, b, do_v)
        dqf = dq_inter + dq_intra
        dk = dk_end + dk_intra
        dv = dv_end + dv_intra
        db = qf * dqf - k * dk
        db = db + jnp.where(rows == ch - 1, db_end, 0.0)
        dg = _exact_tri_matmul(_tri(False), db)
        df = dg / f - dk
        for i in range(hp):
            dstate[i] = ds_new[i]
        dq_ref[...] = (dqf * (HGRN_HEAD_DIM ** -0.5) * (sq + q_raw * sq * (1.0 - sq))).astype(BF16)
        df_ref[...] = (df * (1.0 - lb_row) * sg * (1.0 - sg)).astype(BF16)
        dv_ref[...] = dv.astype(BF16)
        dlb_ref[0:1, :] += jnp.sum(df * (1.0 - sg), axis=0, keepdims=True)

    rev = lambda n: nc - 1 - n
    tile = lambda c: pl.BlockSpec((ch, wb), lambda h, n, c=c: (rev(n), c + h))
    return pl.pallas_call(
        body, grid=(nh // hp, nc),
        in_specs=[tile(cq), tile(cf), tile(cv), tile(cg), tile(0), tile(0),
                  pl.BlockSpec((hp, 1, hd, hd), lambda h, n: (h, rev(n), 0, 0)),
                  pl.BlockSpec((1, wb), lambda h, n: (0, h)), pl.BlockSpec((1, wb), lambda h, n: (0, h))],
        out_specs=[tile(0), tile(0), tile(0), tile(0), pl.BlockSpec((SUBLANES, wb), lambda h, n: (0, h))],
        out_shape=[jax.ShapeDtypeStruct((t, w), BF16)] * 4 + [jax.ShapeDtypeStruct((SUBLANES, w), F32)],
        scratch_shapes=[pltpu.VMEM((hp, hd, hd), F32)],
        name=name, compiler_params=_cp(("parallel", "arbitrary")))(u, u, u, u, o, dp, states, lb, gain)


def _bucket_map():
    i = np.arange(WINDOW)[:, None]
    j = np.arange(2 * WINDOW)[None, :]
    dist = np.clip(WINDOW + i - j, 0, WINDOW - 1)
    max_exact = N_BUCKETS // 2
    logd = (np.log(np.maximum(dist, 1).astype(np.float32) / max_exact) / math.log(MAX_DISTANCE / max_exact))
    large = np.minimum(max_exact + (logd.astype(np.float32) * (N_BUCKETS - max_exact)).astype(np.int32), N_BUCKETS - 1)
    return np.where(dist < max_exact, dist, large).astype(np.int32)


def _bias_table(rel_bias, n_heads):
    bucket = jnp.asarray(_bucket_map())

    def body(rb_ref, bk_ref, o_ref):
        bk = bk_ref[...]
        i = lax.broadcasted_iota(jnp.int32, (WINDOW, 2 * WINDOW), 0)
        j = lax.broadcasted_iota(jnp.int32, (WINDOW, 2 * WINDOW), 1)
        band = ((j >= WINDOW) & (j - WINDOW <= i)) | ((j < WINDOW) & (j > i))
        for h in range(n_heads):
            def step(bi, acc):
                return jnp.where(bk == bi, rb_ref[bi, h], acc)
            table = lax.fori_loop(0, N_BUCKETS, step, jnp.zeros((WINDOW, 2 * WINDOW), F32))
            o_ref[h] = jnp.where(band, table, NEG_INF)

    return pl.pallas_call(
        body, in_specs=[pl.BlockSpec(memory_space=pltpu.SMEM), pl.BlockSpec(memory_space=pltpu.VMEM)],
        out_specs=pl.BlockSpec(memory_space=pltpu.VMEM),
        out_shape=jax.ShapeDtypeStruct((n_heads, WINDOW, 2 * WINDOW), F32), name="bias_table",
        compiler_params=_cp())(rel_bias, bucket)


def _bias_grad(dbias, n_heads):
    bucket = jnp.asarray(_bucket_map())

    def body(db_ref, bk_ref, o_ref):
        bk = bk_ref[...]
        lane = lax.broadcasted_iota(jnp.int32, (1, LANES), 1)

        def step(bi, carry):
            row = jnp.zeros((1, LANES), F32)
            for h in range(n_heads):
                val = jnp.sum(jnp.where(bk == bi, db_ref[h], 0.0))
                row = jnp.where(lane == h, val, row)
            o_ref[pl.ds(bi, 1), :] = row
            return carry

        lax.fori_loop(0, N_BUCKETS, step, 0)

    return pl.pallas_call(
        body, in_specs=[pl.BlockSpec(memory_space=pltpu.VMEM), pl.BlockSpec(memory_space=pltpu.VMEM)],
        out_specs=pl.BlockSpec(memory_space=pltpu.VMEM),
        out_shape=jax.ShapeDtypeStruct((N_BUCKETS, LANES), F32), name="bias_grad",
        compiler_params=_cp())(dbias, bucket)


def _no_prev_block(n, grp):
    j = lax.broadcasted_iota(jnp.int32, (grp * WINDOW, 2 * WINDOW), 1)
    return (j < WINDOW) & (n == 0)


def _attn_probs(no_prev, q_ref, kp_ref, kc_ref, bias_ref, sink_ref, hh, grp):
    ad, wn = ATTN_HEAD_DIM, WINDOW
    ksl = slice(hh * ad, (hh + 1) * ad)
    kw = jnp.concatenate([kp_ref[:, ksl], kc_ref[:, ksl]], axis=0).astype(BF16)
    qs = jnp.concatenate([q_ref[:, (hh * grp + g) * ad:(hh * grp + g + 1) * ad] for g in range(grp)], axis=0).astype(BF16)
    s = _dot_nt(qs, kw) * (ad ** -0.5) + bias_ref[hh]
    s = jnp.where(no_prev, NEG_INF, s)
    rr = lax.broadcasted_iota(jnp.int32, (grp * wn, 1), 0) >> WINDOW_SHIFT
    sink = jnp.zeros((grp * wn, 1), F32)
    for g in range(grp):
        sink = jnp.where(rr == g, sink_ref[hh * grp + g], sink)
    m = jnp.maximum(jnp.max(s, axis=1, keepdims=True), sink)
    p = jnp.exp(s - m)
    es = jnp.exp(sink - m)
    inv = 1.0 / (jnp.sum(p, axis=1, keepdims=True) + es)
    return qs, kw, p * inv, es * inv


GATE_BLOCK = 512


def _attn_fwd(u, bias_g, sinks, off, w, name):
    t = u.shape[0]
    wn, ad, kvw, gb = WINDOW, ATTN_HEAD_DIM, KV_WIDTH, GATE_BLOCK
    grp = (w // ad) // ATTN_KV_HEADS
    nb = t // wn
    n_gb = w // gb
    cq, ck, cv, cg = off["b_q"] // w, off["b_k"] // kvw, off["b_v"] // kvw, off["b_g"] // gb

    def body(q_ref, kp_ref, kc_ref, vp_ref, vc_ref, bias_ref, sink_ref, *rest):
        g_refs, (o_ref, p_ref) = rest[:n_gb], rest[n_gb:]
        no_prev = _no_prev_block(pl.program_id(0), grp)
        for hh in range(ATTN_KV_HEADS):
            _, _, p, _ = _attn_probs(no_prev, q_ref, kp_ref, kc_ref, bias_ref, sink_ref, hh, grp)
            ksl = slice(hh * ad, (hh + 1) * ad)
            vw = jnp.concatenate([vp_ref[:, ksl], vc_ref[:, ksl]], axis=0).astype(BF16)
            o = _dot_nn(p.astype(BF16), vw)
            for g in range(grp):
                o_ref[:, (hh * grp + g) * ad:(hh * grp + g + 1) * ad] = o[g * wn:(g + 1) * wn]
        for i in range(n_gb):
            sl = slice(i * gb, (i + 1) * gb)
            silu, _ = _silu_parts(g_refs[i][...])
            p_ref[:, sl] = (o_ref[:, sl] * silu).astype(BF16)

    prev = lambda n: jnp.maximum(n - 1, 0)
    row = pl.BlockSpec((wn, w), lambda n: (n, 0))
    return pl.pallas_call(
        body, grid=(nb,),
        in_specs=[pl.BlockSpec((wn, w), lambda n: (n, cq)),
                  pl.BlockSpec((wn, kvw), lambda n: (prev(n), ck)), pl.BlockSpec((wn, kvw), lambda n: (n, ck)),
                  pl.BlockSpec((wn, kvw), lambda n: (prev(n), cv)), pl.BlockSpec((wn, kvw), lambda n: (n, cv)),
                  pl.BlockSpec((ATTN_KV_HEADS, grp * wn, 2 * wn), lambda n: (0, 0, 0)),
                  pl.BlockSpec(memory_space=pltpu.SMEM)]
        + [pl.BlockSpec((wn, gb), lambda n, i=i: (n, cg + i)) for i in range(n_gb)],
        out_specs=[row, row],
        out_shape=[jax.ShapeDtypeStruct((t, w), F32), jax.ShapeDtypeStruct((t, w), BF16)], name=name,
        compiler_params=_cp(("parallel",)))(u, u, u, u, u, bias_g, sinks, *([u] * n_gb))


def _attn_bwd(u, o, dp, bias_g, sinks, off, w, name):
    t = u.shape[0]
    wn, ad, kvw, gb = WINDOW, ATTN_HEAD_DIM, KV_WIDTH, GATE_BLOCK
    grp = (w // ad) // ATTN_KV_HEADS
    nb = t // wn
    n_gb = w // gb
    cq, ck, cv, cg = off["b_q"] // w, off["b_k"] // kvw, off["b_v"] // kvw, off["b_g"] // gb

    def body(q_ref, kp_ref, kc_ref, vp_ref, vc_ref, o_ref, dp_ref, bias_ref, sink_ref, *rest):
        g_refs = rest[:n_gb]
        dq_ref, dkc_ref, dkp_ref, dvc_ref, dvp_ref, dbias_ref, dsink_ref, dg_ref, do_ref = rest[n_gb:]
        n = pl.program_id(0)

        @pl.when(n == 0)
        def _():
            dbias_ref[...] = jnp.zeros_like(dbias_ref)
            dsink_ref[...] = jnp.zeros_like(dsink_ref)

        for i in range(n_gb):
            sl = slice(i * gb, (i + 1) * gb)
            silu, dsilu = _silu_parts(g_refs[i][...])
            dp_v = dp_ref[:, sl]
            do_ref[:, sl] = dp_v * silu
            dg_ref[:, sl] = (dp_v * o_ref[:, sl] * dsilu).astype(BF16)

        lane = lax.broadcasted_iota(jnp.int32, (1, LANES), 1)
        rr = lax.broadcasted_iota(jnp.int32, (grp * wn, 1), 0) >> WINDOW_SHIFT
        dsink_row = jnp.zeros((1, LANES), F32)
        no_prev = _no_prev_block(n, grp)
        for hh in range(ATTN_KV_HEADS):
            qs, kw, p, psink = _attn_probs(no_prev, q_ref, kp_ref, kc_ref, bias_ref, sink_ref, hh, grp)
            ksl = slice(hh * ad, (hh + 1) * ad)
            vw = jnp.concatenate([vp_ref[:, ksl], vc_ref[:, ksl]], axis=0).astype(BF16)
            hs = [slice((hh * grp + g) * ad, (hh * grp + g + 1) * ad) for g in range(grp)]
            dos = jnp.concatenate([do_ref[:, sl] for sl in hs], axis=0)
            os_ = jnp.concatenate([o_ref[:, sl] for sl in hs], axis=0)
            delta = jnp.sum(dos * os_, axis=1, keepdims=True)
            dos_b = dos.astype(BF16)
            dp = _dot_nt(dos_b, vw)
            ds = p * (dp - delta)
            dbias_ref[hh] += ds
            sd = psink * delta
            for g in range(grp):
                val = -jnp.sum(jnp.where(rr == g, sd, 0.0))
                dsink_row = jnp.where(lane == hh * grp + g, val, dsink_row)
            ds_b = (ds * (ad ** -0.5)).astype(BF16)
            dq = _dot_nn(ds_b, kw)
            for g in range(grp):
                dq_ref[:, hs[g]] = dq[g * wn:(g + 1) * wn].astype(BF16)
            dkw = _dot_tn(ds_b, qs)
            dvw = _dot_tn(p.astype(BF16), dos_b)
            dkp_ref[:, ksl] = dkw[:wn]
            dkc_ref[:, ksl] = dkw[wn:]
            dvp_ref[:, ksl] = dvw[:wn]
            dvc_ref[:, ksl] = dvw[wn:]
        dsink_ref[0:1, :] += dsink_row

    prev = lambda n: jnp.maximum(n - 1, 0)
    kv_out = pl.BlockSpec((wn, kvw), lambda n: (n, 0))
    row = pl.BlockSpec((wn, w), lambda n: (n, 0))
    return pl.pallas_call(
        body, grid=(nb,),
        in_specs=[pl.BlockSpec((wn, w), lambda n: (n, cq)),
                  pl.BlockSpec((wn, kvw), lambda n: (prev(n), ck)), pl.BlockSpec((wn, kvw), lambda n: (n, ck)),
                  pl.BlockSpec((wn, kvw), lambda n: (prev(n), cv)), pl.BlockSpec((wn, kvw), lambda n: (n, cv)),
                  row, row,
                  pl.BlockSpec((ATTN_KV_HEADS, grp * wn, 2 * wn), lambda n: (0, 0, 0)),
                  pl.BlockSpec(memory_space=pltpu.SMEM)]
        + [pl.BlockSpec((wn, gb), lambda n, i=i: (n, cg + i)) for i in range(n_gb)],
        out_specs=[row, kv_out, kv_out, kv_out, kv_out,
                   pl.BlockSpec((ATTN_KV_HEADS, grp * wn, 2 * wn), lambda n: (0, 0, 0)),
                   pl.BlockSpec((SUBLANES, LANES), lambda n: (0, 0)), row],
        out_shape=[jax.ShapeDtypeStruct((t, w), BF16)] + [jax.ShapeDtypeStruct((t, kvw), F32)] * 4
        + [jax.ShapeDtypeStruct((ATTN_KV_HEADS, grp * wn, 2 * wn), F32), jax.ShapeDtypeStruct((SUBLANES, LANES), F32),
           jax.ShapeDtypeStruct((t, w), BF16)],
        scratch_shapes=[pltpu.VMEM((wn, w), F32)],
        name=name, compiler_params=_cp(("arbitrary",)))(u, u, u, u, u, o, dp, bias_g, sinks, *([u] * n_gb))


def _kv_combine(cur, prv, name):
    t, kvw = cur.shape

    def body(i, nt, c_ref, p_ref, o_ref):
        nxt = jnp.where(i < nt - 1, p_ref[...], 0.0)
        o_ref[...] = (c_ref[...] + nxt).astype(BF16)

    return _ew(body, name, t, 1, kvw, [(cur, "tile", 0), (prv, "next", 0)], [BF16], tt=WINDOW)[0]


def _shift_down(h, tail, k, rows):
    tt = h.shape[0]
    out = pltpu.roll(h, k, 0)
    for r in range(k):
        out = jnp.where(rows == r, tail[tt - k + r:tt - k + r + 1, :], out)
    return out


def _shift_up(h, head, k, rows):
    tt = h.shape[0]
    out = pltpu.roll(h, tt - k, 0)
    for r in range(k):
        out = jnp.where(rows == tt - k + r, head[r:r + 1, :], out)
    return out


def _conv_fwd(u, conv_w, off, w, name):
    t = u.shape[0]
    wb = 512
    c = lambda nme: off[nme] // wb

    def body(i, nt, cb_ref, cc_ref, ccp_ref, cx_ref, cxp_ref, cg_ref, w_ref, p_ref):
        h = cc_ref[...] * cx_ref[...]
        hp = jnp.where(i > 0, ccp_ref[...] * cxp_ref[...], 0.0)
        rows = lax.broadcasted_iota(jnp.int32, h.shape, 0)
        y = w_ref[0:1, :] * _shift_down(h, hp, 2, rows) + w_ref[1:2, :] * _shift_down(h, hp, 1, rows) + w_ref[2:3, :] * h
        silu, _ = _silu_parts(cg_ref[...])
        p_ref[...] = (cb_ref[...] * y * silu).astype(BF16)

    return _ew(body, name, t, w // wb, wb,
               [(u, "tile", c("c_b")), (u, "tile", c("c_c")), (u, "prev", c("c_c")), (u, "tile", c("c_x")),
                (u, "prev", c("c_x")), (u, "tile", c("c_g")), (conv_w, "row", 0)], [BF16])[0]


def _conv_bwd(dp, u, conv_w, off, w, name):
    t = u.shape[0]
    wb = 512
    c = lambda nme: off[nme] // wb

    def body(i, nt, dp_ref, dpn_ref, cb_ref, cbn_ref, cg_ref, cgn_ref, cc_ref, ccp_ref, cx_ref, cxp_ref, w_ref,
             dcb_ref, dcc_ref, dcx_ref, dcg_ref, acc_ref):
        @pl.when(i == 0)
        def _():
            acc_ref[...] = jnp.zeros_like(acc_ref)

        cc, cx, cb = cc_ref[...], cx_ref[...], cb_ref[...]
        h = cc * cx
        hp = jnp.where(i > 0, ccp_ref[...] * cxp_ref[...], 0.0)
        rows = lax.broadcasted_iota(jnp.int32, h.shape, 0)
        h1 = _shift_down(h, hp, 1, rows)
        h2 = _shift_down(h, hp, 2, rows)
        w0, w1, w2 = w_ref[0:1, :], w_ref[1:2, :], w_ref[2:3, :]
        y = w0 * h2 + w1 * h1 + w2 * h
        silu, dsilu = _silu_parts(cg_ref[...])
        dp_v = dp_ref[...]
        dcg_ref[...] = (dp_v * cb * y * dsilu).astype(BF16)
        dcb_ref[...] = (dp_v * y * silu).astype(BF16)
        dy = dp_v * cb * silu
        silu_n, _ = _silu_parts(cgn_ref[...])
        dyn = jnp.where(i < nt - 1, dpn_ref[...] * cbn_ref[...] * silu_n, 0.0)
        dh = w2 * dy + w1 * _shift_up(dy, dyn, 1, rows) + w0 * _shift_up(dy, dyn, 2, rows)
        dcc_ref[...] = (dh * cx).astype(BF16)
        dcx_ref[...] = (dh * cc).astype(BF16)
        acc_ref[0:1, :] += jnp.sum(dy * h2, axis=0, keepdims=True)
        acc_ref[1:2, :] += jnp.sum(dy * h1, axis=0, keepdims=True)
        acc_ref[2:3, :] += jnp.sum(dy * h, axis=0, keepdims=True)

    return _ew(body, name, t, w // wb, wb,
               [(dp, "tile", 0), (dp, "next", 0), (u, "tile", c("c_b")), (u, "next", c("c_b")),
                (u, "tile", c("c_g")), (u, "next", c("c_g")), (u, "tile", c("c_c")), (u, "prev", c("c_c")),
                (u, "tile", c("c_x")), (u, "prev", c("c_x")), (conv_w, "row", 0)],
               [BF16] * 4, accs=[SUBLANES])


def _merge_fwd(u, ya, yb, yc, off, d, name):
    t = u.shape[0]
    wb = 512
    c = lambda nme: off[nme] // wb

    def body(i, nt, ma_ref, mb_ref, mc_ref, ya_ref, yb_ref, yc_ref, o_ref):
        o_ref[...] = (_sigmoid(ma_ref[...]) * ya_ref[...] + _sigmoid(mb_ref[...]) * yb_ref[...]
                      + _sigmoid(mc_ref[...]) * yc_ref[...]).astype(BF16)

    return _ew(body, name, t, d // wb, wb,
               [(u, "tile", c("m_a")), (u, "tile", c("m_b")), (u, "tile", c("m_c")),
                (ya, "tile", 0), (yb, "tile", 0), (yc, "tile", 0)], [BF16])[0]


def _merge_bwd(dm, u, ya, yb, yc, off, d, name):
    t = u.shape[0]
    wb = 512
    c = lambda nme: off[nme] // wb

    def body(i, nt, dm_ref, ma_ref, mb_ref, mc_ref, ya_ref, yb_ref, yc_ref, da_ref, db_ref, dc_ref, ga_ref, gb_ref, gc_ref):
        dm_v = dm_ref[...]
        for m_ref, y_ref, dy_ref, dg_ref in ((ma_ref, ya_ref, da_ref, ga_ref), (mb_ref, yb_ref, db_ref, gb_ref),
                                             (mc_ref, yc_ref, dc_ref, gc_ref)):
            s = _sigmoid(m_ref[...])
            dy_ref[...] = (dm_v * s).astype(BF16)
            dg_ref[...] = (dm_v * y_ref[...] * s * (1.0 - s)).astype(BF16)

    return _ew(body, name, t, d // wb, wb,
               [(dm, "tile", 0), (u, "tile", c("m_a")), (u, "tile", c("m_b")), (u, "tile", c("m_c")),
                (ya, "tile", 0), (yb, "tile", 0), (yc, "tile", 0)], [BF16] * 6)


def _lower_bounds(lb_param):
    def body(p_ref, o_ref):
        p = p_ref[...]
        e = jnp.exp(p - jnp.max(p, axis=0, keepdims=True))
        soft = e / jnp.sum(e, axis=0, keepdims=True)
        acc = jnp.zeros_like(soft[0:1])
        o_ref[0:1, :] = acc
        for l in range(1, DEPTH):
            acc = acc + soft[l:l + 1]
            o_ref[l:l + 1, :] = acc

    return pl.pallas_call(body, out_shape=jax.ShapeDtypeStruct(lb_param.shape, F32), name="lower_bounds",
                          compiler_params=_cp())(lb_param)


def _lower_bounds_bwd(lb_param, dlower):
    def body(p_ref, d_ref, o_ref):
        p = p_ref[...]
        e = jnp.exp(p - jnp.max(p, axis=0, keepdims=True))
        soft = e / jnp.sum(e, axis=0, keepdims=True)
        dl = d_ref[...]
        ds = [jnp.zeros_like(dl[0:1])]
        for j in range(1, DEPTH):
            acc = dl[j:j + 1]
            for l in range(j + 1, DEPTH):
                acc = acc + dl[l:l + 1]
            ds.append(acc)
        inner = ds[0] * soft[0:1]
        for j in range(1, DEPTH):
            inner = inner + ds[j] * soft[j:j + 1]
        for j in range(DEPTH):
            o_ref[j:j + 1, :] = soft[j:j + 1] * (ds[j] - inner)

    return pl.pallas_call(body, out_shape=jax.ShapeDtypeStruct(lb_param.shape, F32), name="lower_bounds_bwd",
                          compiler_params=_cp())(lb_param, dlower)


def _exchange(arrays, scatter, name, chips=False):
    n_arr = len(arrays)
    n_slot = N_DEV // 2 if chips else N_DEV

    def body(*refs):
        srcs, dsts = refs[:n_arr], refs[n_arr:2 * n_arr]
        send_sems, recv_sems, local_sems = refs[2 * n_arr:]
        me = (2 * lax.axis_index("x") + lax.axis_index("y") if chips
              else 4 * lax.axis_index("x") + 2 * lax.axis_index("y") + lax.axis_index("c"))
        copies = _peer_copies(srcs, dsts, send_sems, recv_sems, scatter, chips)
        for a in range(n_arr):
            copies.append(pltpu.make_async_copy(srcs[a].at[me] if scatter else srcs[a], dsts[a].at[me], local_sems.at[a]))
        for cp in copies:
            cp.start()
        for cp in copies:
            cp.wait()

    out_shape = [jax.ShapeDtypeStruct(a.shape if scatter else (n_slot,) + a.shape, a.dtype) for a in arrays]
    anyspec = pl.BlockSpec(memory_space=pl.ANY)
    res = pl.pallas_call(
        body, in_specs=[anyspec] * n_arr, out_specs=[anyspec] * n_arr, out_shape=out_shape,
        scratch_shapes=[pltpu.SemaphoreType.DMA((n_arr * (n_slot - 1),)), pltpu.SemaphoreType.DMA((n_arr * (n_slot - 1),)),
                        pltpu.SemaphoreType.DMA((n_arr,))],
        name=name)(*arrays)
    return list(res)


def _peer_copies(srcs, lands, send_sems, recv_sems, scatter, chips=False):
    x, y, c = lax.axis_index("x"), lax.axis_index("y"), lax.axis_index("c")
    flips = [k for k in range(1, N_DEV) if not (chips and k & 1)]
    slot = (lambda px, py, pc: 2 * px + py) if chips else (lambda px, py, pc: 4 * px + 2 * py + pc)
    copies = []
    for a in range(len(srcs)):
        for i, k in enumerate(flips):
            px = 1 - x if k & 4 else x
            py = 1 - y if k & 2 else y
            pc = 1 - c if k & 1 else c
            src = srcs[a].at[slot(px, py, pc)] if scatter else srcs[a]
            copies.append(pltpu.make_async_remote_copy(
                src_ref=src, dst_ref=lands[a].at[slot(x, y, c)],
                send_sem=send_sems.at[a * len(flips) + i], recv_sem=recv_sems.at[a * len(flips) + i],
                device_id=(px, py, pc), device_id_type=pl.DeviceIdType.MESH))
    return copies


def _gather_two_level(arrays, name):
    n_arr = len(arrays)
    per = N_DEV - 1

    def body(*refs):
        srcs, outs = refs[:n_arr], refs[n_arr:2 * n_arr]
        send_sems, recv_sems, local_sems = refs[2 * n_arr:]
        x, y, c = lax.axis_index("x"), lax.axis_index("y"), lax.axis_index("c")
        me, sibling = (x, y, c), (x, y, 1 - c)
        chips = [(1 - x, y), (x, 1 - y), (1 - x, 1 - y)]

        def copy(a, k, block, to, src=None):
            dst = outs[a].at[4 * block[0] + 2 * block[1] + block[2]]
            return pltpu.make_async_remote_copy(
                src_ref=dst if src is None else src, dst_ref=dst,
                send_sem=send_sems.at[a * per + k], recv_sem=recv_sems.at[a * per + k],
                device_id=to, device_id_type=pl.DeviceIdType.MESH)

        own, first, passed = [], [], []
        for a in range(n_arr):
            own.append(pltpu.make_async_copy(srcs[a], outs[a].at[4 * x + 2 * y + c], local_sems.at[a]))
            first.append(copy(a, 0, me, sibling, src=srcs[a]))
            first += [copy(a, 1 + j, me, (*chip, c), src=srcs[a]) for j, chip in enumerate(chips)]
        for cp in own + first:
            cp.start()
        for a in range(n_arr):
            for j, chip in enumerate(chips):
                copy(a, 1 + j, (*chip, c), me).wait_recv()
                passed.append(copy(a, 4 + j, (*chip, c), sibling))
                passed[-1].start()
        for a in range(n_arr):
            copy(a, 0, sibling, me).wait_recv()
            for j, chip in enumerate(chips):
                copy(a, 4 + j, (*chip, 1 - c), me).wait_recv()
        for cp in first + passed:
            cp.wait_send()
        for cp in own:
            cp.wait()

    anyspec = pl.BlockSpec(memory_space=pl.ANY)
    res = pl.pallas_call(
        body, in_specs=[anyspec] * n_arr, out_specs=[anyspec] * n_arr,
        out_shape=[jax.ShapeDtypeStruct((N_DEV,) + a.shape, a.dtype) for a in arrays],
        scratch_shapes=[pltpu.SemaphoreType.DMA((n_arr * per,)), pltpu.SemaphoreType.DMA((n_arr * per,)),
                        pltpu.SemaphoreType.DMA((n_arr,))],
        name=name)(*arrays)
    return list(res)


def _sibling_swap(arrays, name):
    n_arr = len(arrays)
    n_chip = N_DEV // 2

    def body(*refs):
        srcs, outs = refs[:n_arr], refs[n_arr:2 * n_arr]
        send_sems, recv_sems = refs[2 * n_arr:]
        x, y, c = lax.axis_index("x"), lax.axis_index("y"), lax.axis_index("c")
        copies = []
        for a in range(n_arr):
            for j in range(n_chip):
                copies.append(pltpu.make_async_remote_copy(
                    src_ref=srcs[a].at[2 * j + 1 - c], dst_ref=outs[a].at[j],
                    send_sem=send_sems.at[a * n_chip + j], recv_sem=recv_sems.at[a * n_chip + j],
                    device_id=(x, y, 1 - c), device_id_type=pl.DeviceIdType.MESH))
        for cp in copies:
            cp.start()
        for cp in copies:
            cp.wait()

    anyspec = pl.BlockSpec(memory_space=pl.ANY)
    res = pl.pallas_call(
        body, in_specs=[anyspec] * n_arr, out_specs=[anyspec] * n_arr,
        out_shape=[jax.ShapeDtypeStruct((n_chip,) + a.shape[1:], a.dtype) for a in arrays],
        scratch_shapes=[pltpu.SemaphoreType.DMA((n_arr * n_chip,)), pltpu.SemaphoreType.DMA((n_arr * n_chip,))],
        name=name)(*arrays)
    return list(res)


def _pair_sum(send, stage, core, name):
    _, r, c = send.shape
    n_chip = stage.shape[0]
    tr = _tile(r, 128)

    def body(core_ref, a_ref, b_ref, o_ref):
        o_ref[...] = a_ref[...] + b_ref[...]

    return pl.pallas_call(
        body,
        grid_spec=pltpu.PrefetchScalarGridSpec(
            num_scalar_prefetch=1, grid=(n_chip, r // tr),
            in_specs=[pl.BlockSpec((1, tr, c), lambda j, i, core_ref: (2 * j + core_ref[0], i, 0)),
                      pl.BlockSpec((1, tr, c), lambda j, i, core_ref: (j, i, 0))],
            out_specs=pl.BlockSpec((1, tr, c), lambda j, i, core_ref: (j, i, 0))),
        out_shape=jax.ShapeDtypeStruct(stage.shape, F32), name=name,
        compiler_params=_cp(("parallel", "parallel")))(core, send, stage)


_HBM_SPEC = pl.BlockSpec(memory_space=pltpu.HBM)
_SEM_SPEC = pl.BlockSpec(memory_space=pltpu.SEMAPHORE)
_ANY_SPEC = pl.BlockSpec(memory_space=pl.ANY)
_DATAFLOW = pltpu.SideEffectType.DATAFLOW_SIDE_EFFECTING


def _exchange_start(arrays, scatter, name, dep=None, chips=False):
    n_arr = len(arrays)
    n_slot = N_DEV // 2 if chips else N_DEV
    n_sem = n_arr * (n_slot - 1)
    me = (2 * lax.axis_index("x") + lax.axis_index("y") if chips
          else 4 * lax.axis_index("x") + 2 * lax.axis_index("y") + lax.axis_index("c"))
    lands = []
    for a in arrays:
        own = lax.dynamic_index_in_dim(a, me, 0, keepdims=False) if scatter else a
        shape = a.shape if scatter else (n_slot,) + a.shape
        lands.append(lax.dynamic_update_index_in_dim(lax.empty(shape, a.dtype), own, me, 0))
    dep_specs, dep_args = _dep_specs(dep)

    def body(*refs):
        srcs, lnds = refs[:n_arr], refs[n_arr:2 * n_arr]
        outs = refs[2 * n_arr + len(dep_args):]
        send_sems, recv_sems, token = outs[0], outs[1], outs[2 + 2 * n_arr]
        for cp in _peer_copies(srcs, lnds, send_sems, recv_sems, scatter, chips):
            cp.start()
        token[...] = jnp.zeros_like(token)

    thru = [pltpu.HBM(a.shape, a.dtype) for a in list(arrays) + lands]
    return pl.pallas_call(
        body, name=name,
        out_shape=(pltpu.SemaphoreType.DMA((n_sem,)), pltpu.SemaphoreType.DMA((n_sem,)), *thru,
                   jax.ShapeDtypeStruct((SUBLANES, LANES), F32)),
        in_specs=[_HBM_SPEC] * (2 * n_arr) + dep_specs,
        out_specs=(_SEM_SPEC, _SEM_SPEC, *[_HBM_SPEC] * (2 * n_arr), pl.BlockSpec(memory_space=pltpu.VMEM)),
        input_output_aliases={i: 2 + i for i in range(2 * n_arr)},
        compiler_params=pltpu.CompilerParams(has_side_effects=_DATAFLOW),
    )(*[pltpu.with_memory_space_constraint(a, pltpu.HBM) for a in list(arrays) + lands], *dep_args)


def _exchange_wait(started, scatter, name, after, chips=False):
    send_sems, recv_sems = started[0], started[1]
    thru = list(started[2:-1])
    n_arr = len(thru) // 2

    def body(*refs):
        srcs, lnds = refs[:n_arr], refs[n_arr:2 * n_arr]
        for cp in _peer_copies(srcs, lnds, refs[2 * n_arr], refs[2 * n_arr + 1], scatter, chips):
            cp.wait_send()
            cp.wait_recv()

    res = pl.pallas_call(
        body, name=name, out_shape=tuple(pltpu.HBM(a.shape, a.dtype) for a in thru),
        in_specs=[_HBM_SPEC] * (2 * n_arr) + [_SEM_SPEC, _SEM_SPEC, _ANY_SPEC],
        out_specs=tuple([_HBM_SPEC] * (2 * n_arr)),
        input_output_aliases={i: i for i in range(2 * n_arr)},
        compiler_params=pltpu.CompilerParams(has_side_effects=_DATAFLOW),
    )(*thru, send_sems, recv_sems, after)
    return list(res[n_arr:])


def _unshard_cols(g, name):
    nd, r, s = g.shape
    tr = _tile(r, 64)

    def body(i_ref, o_ref):
        for p in range(nd):
            o_ref[:, p * s:(p + 1) * s] = i_ref[p]

    return pl.pallas_call(
        body, grid=(r // tr,), in_specs=[pl.BlockSpec((nd, tr, s), lambda i: (0, i, 0))],
        out_specs=pl.BlockSpec((tr, nd * s), lambda i: (i, 0)),
        out_shape=jax.ShapeDtypeStruct((r, nd * s), g.dtype), name=name, compiler_params=_cp(("parallel",)))(g)


def _shard_cols(g, name):
    r, n = g.shape
    s = n // N_DEV
    tr = _tile(r, 64)

    def body(i_ref, o_ref):
        for p in range(N_DEV):
            o_ref[p] = i_ref[:, p * s:(p + 1) * s]

    return pl.pallas_call(
        body, grid=(r // tr,), in_specs=[pl.BlockSpec((tr, n), lambda i: (i, 0))],
        out_specs=pl.BlockSpec((N_DEV, tr, s), lambda i: (0, i, 0)),
        out_shape=jax.ShapeDtypeStruct((N_DEV, r, s), g.dtype), name=name, compiler_params=_cp(("parallel",)))(g)


def _slot_sum(slots, name):
    nd, r, c = slots.shape
    tr = _tile(r, 64)

    def body(s_ref, o_ref):
        acc = s_ref[0]
        for p in range(1, nd):
            acc = acc + s_ref[p]
        o_ref[...] = acc

    return pl.pallas_call(
        body, grid=(r // tr,), in_specs=[pl.BlockSpec((nd, tr, c), lambda i: (0, i, 0))],
        out_specs=pl.BlockSpec((tr, c), lambda i: (i, 0)),
        out_shape=jax.ShapeDtypeStruct((r, c), F32), name=name, compiler_params=_cp(("parallel",)))(slots)


def _adamw(w, g, m, v, name):
    r, c = w.shape
    tr = _tile(r, 256)
    c1 = 1.0 - ADAM_B1 ** ADAM_STEP
    c2 = 1.0 - ADAM_B2 ** ADAM_STEP

    def body(w_ref, g_ref, m_ref, v_ref, d_ref, nm_ref, nv_ref):
        gv = g_ref[...]
        nm = ADAM_B1 * m_ref[...] + (1.0 - ADAM_B1) * gv
        nv = ADAM_B2 * v_ref[...] + (1.0 - ADAM_B2) * (gv * gv)
        nm_ref[...] = nm
        nv_ref[...] = nv
        d_ref[...] = -ADAM_LR * ((nm / c1) / (jnp.sqrt(nv / c2) + ADAM_EPS) + ADAM_WD * w_ref[...])

    spec = pl.BlockSpec((tr, c), lambda i: (i, 0))
    return pl.pallas_call(
        body, grid=(r // tr,), in_specs=[spec] * 4, out_specs=[spec] * 3,
        out_shape=[jax.ShapeDtypeStruct((r, c), F32)] * 3, name=name, compiler_params=_cp(("parallel",)))(w, g, m, v)


def _forward_backward(x, target, weights_hook, grads_hook, lb_param, hgrn_norm_g, attn_sinks, rel_bias, ln_g, ln_b):
    t, d = x.shape
    w = d // 2
    off, n_in = _offsets(d)
    n_heads = w // ATTN_HEAD_DIM
    grp = n_heads // ATTN_KV_HEADS

    lower = _lower_bounds(lb_param)
    bias = _bias_table(rel_bias, n_heads)
    bias_g = bias.reshape(ATTN_KV_HEADS, grp * WINDOW, 2 * WINDOW)

    saved, weights = [], []
    xb = x.astype(BF16)
    for l in range(DEPTH):
        wl, token = weights_hook(l, x)
        weights.append(wl)
        s = {"x": x, "xb": xb}
        u = _mm_nt_cols(xb, wl["w_in_t"], f"in_proj", dep=token)
        s["u"] = u
        lb_l, gain_l, cw_l = lower[l:l + 1], hgrn_norm_g[l:l + 1], wl["conv_w"]
        o_a, states, p_a = _hgrn_fwd(u, lb_l, gain_l, off, f"hgrn_fwd")
        o_b, p_b = _attn_fwd(u, bias_g, attn_sinks[l], off, w, f"attn_fwd")
        p_c = _conv_fwd(u, cw_l, off, w, f"conv_fwd")
        y_a = _mm_nn(p_a, wl["w_proj_hgrn"], f"proj_a", tn=2048)
        y_b = _mm_nn(p_b, wl["w_proj_attn"], f"proj_b", tn=2048)
        y_c = _mm_nn(p_c, wl["w_proj_conv"], f"proj_c", tn=2048)
        merged = _merge_fwd(u, y_a, y_b, y_c, off, d, f"merge_fwd")
        y = _mm_nn(merged, wl["w_out"], f"out_proj", tm=512, tn=2048)
        x, xb, xhat, rstd = _ln_fwd(x, y, ln_g[l:l + 1], ln_b[l:l + 1], f"ln_fwd")
        s.update(o_a=o_a, states=states, p_a=p_a, o_b=o_b, p_b=p_b, p_c=p_c, y_a=y_a, y_b=y_b, y_c=y_c,
                 merged=merged, xhat=xhat, rstd=rstd)
        saved.append(s)

    loss_acc, dx = _loss_head(x, target)

    d_ln, d_lower, d_gain, d_sink, d_conv = [None] * DEPTH, [None] * DEPTH, [None] * DEPTH, [None] * DEPTH, [None] * DEPTH
    dbias_total = None
    for l in reversed(range(DEPTH)):
        wl, s = weights[l], saved[l]
        u = s["u"]
        lb_l, gain_l, cw_l = lower[l:l + 1], hgrn_norm_g[l:l + 1], wl["conv_w"]
        dz, dzb, d_ln[l] = _ln_bwd(dx, s["xhat"], s["rstd"], ln_g[l:l + 1], f"ln_bwd")
        g_out = _mm_tn(s["merged"], dzb, f"g_out", tn=2048)
        dmerged = _mm_nt(dzb, wl["w_out"], f"d_merged", tk=2048)
        dya, dyb, dyc, dma, dmb, dmc = _merge_bwd(dmerged, u, s["y_a"], s["y_b"], s["y_c"], off, d, f"merge_bwd")
        g_pa = _mm_tn(s["p_a"], dya, f"g_proj_a", tn=2048)
        g_pb = _mm_tn(s["p_b"], dyb, f"g_proj_b", tn=2048)
        g_pc = _mm_tn(s["p_c"], dyc, f"g_proj_c", tn=2048)
        dpa = _mm_nt(dya, wl["w_proj_hgrn"], f"d_p_a", tk=2048)
        dpb = _mm_nt(dyb, wl["w_proj_attn"], f"d_p_b", tk=2048)
        dpc = _mm_nt(dyc, wl["w_proj_conv"], f"d_p_c", tk=2048)
        d_aq, d_af, d_ai, d_ag, acc_a = _hgrn_bwd(u, lb_l, gain_l, s["states"], s["o_a"], dpa, off, f"hgrn_bwd")
        d_lower[l], d_gain[l] = acc_a[0:1], acc_a[1:2]
        d_bq, dkc, dkp, dvc, dvp, dbias_l, d_sink[l], d_bg = _attn_bwd(
            u, s["o_b"], dpb, bias_g, attn_sinks[l], off, w, f"attn_bwd")
        d_bk = _kv_combine(dkc, dkp, f"k_combine")
        d_bv = _kv_combine(dvc, dvp, f"v_combine")
        dbias_total = dbias_l if dbias_total is None else dbias_total + dbias_l
        d_cb, d_cc, d_cx, d_cg, d_conv[l] = _conv_bwd(dpc, u, cw_l, off, w, f"conv_bwd")
        du = jnp.concatenate([d_aq, d_af, d_ai, d_ag, d_bq, d_bk, d_bv, d_bg, d_cb, d_cc, d_cx, d_cg, dma, dmb, dmc], axis=1)
        g_in_t = _mm_tn_rows(du, s["xb"], f"g_in")
        token = grads_hook(l, {"w_in_t": g_in_t, "w_proj_hgrn": g_pa, "w_proj_attn": g_pb, "w_proj_conv": g_pc, "w_out": g_out})
        dx = _mm_nn_acc(du, wl["w_in_t"], f"d_x", add=dz, add_scale=ALPHA, dep=token)

    d_lower_all = jnp.concatenate([a[0:1] for a in d_lower], axis=0)
    small = {
        "lb_param": _lower_bounds_bwd(lb_param, d_lower_all),
        "hgrn_norm_g": jnp.concatenate([a[0:1] for a in d_gain], axis=0),
        "attn_sinks": jnp.concatenate([a[0:1, :n_heads] for a in d_sink], axis=0),
        "conv_w": jnp.stack([a[0:3] for a in d_conv], axis=0),
        "rel_bias": _bias_grad(dbias_total.reshape(n_heads, WINDOW, 2 * WINDOW), n_heads)[:, :n_heads],
        "ln_g": jnp.concatenate([a[0:1] for a in d_ln], axis=0),
        "ln_b": jnp.concatenate([a[1:2] for a in d_ln], axis=0),
    }
    return loss_acc, dx, small


BIG = ("w_in", "w_proj_hgrn", "w_proj_attn", "w_proj_conv", "w_out")
SMALL = ("lb_param", "hgrn_norm_g", "attn_sinks", "conv_w", "rel_bias", "ln_g", "ln_b")
ORDER = ("w_in", "w_proj_hgrn", "w_proj_attn", "w_proj_conv", "w_out", "lb_param", "hgrn_norm_g", "attn_sinks",
         "conv_w", "rel_bias", "ln_g", "ln_b")


def _pack(parts):
    flat = jnp.concatenate([p.reshape(-1) for p in parts])
    n = flat.shape[0]
    unit = SUBLANES * LANES
    total = -(-n // unit) * unit
    return jnp.pad(flat, (0, total - n)).reshape(total // LANES, LANES)


def _unpack(packed, shapes):
    flat = packed.reshape(-1)
    out, o = [], 0
    for shp in shapes:
        n = int(np.prod(shp))
        out.append(flat[o:o + n].reshape(shp))
        o += n
    return out


def kernel(x, w_in, w_proj_hgrn, w_proj_attn, w_proj_conv, w_out, lb_param, hgrn_norm_g, attn_sinks, conv_w, rel_bias, ln_g, ln_b, loss_target, m_w_in, m_w_proj_hgrn, m_w_proj_attn, m_w_proj_conv, m_w_out, m_lb_param, m_hgrn_norm_g, m_attn_sinks, m_conv_w, m_rel_bias, m_ln_g, m_ln_b, v_w_in, v_w_proj_hgrn, v_w_proj_attn, v_w_proj_conv, v_w_out, v_lb_param, v_hgrn_norm_g, v_attn_sinks, v_conv_w, v_rel_bias, v_ln_g, v_ln_b):
    params = dict(w_in=w_in, w_proj_hgrn=w_proj_hgrn, w_proj_attn=w_proj_attn, w_proj_conv=w_proj_conv, w_out=w_out,
                  lb_param=lb_param, hgrn_norm_g=hgrn_norm_g, attn_sinks=attn_sinks, conv_w=conv_w, rel_bias=rel_bias,
                  ln_g=ln_g, ln_b=ln_b)
    mom_m = dict(w_in=m_w_in, w_proj_hgrn=m_w_proj_hgrn, w_proj_attn=m_w_proj_attn, w_proj_conv=m_w_proj_conv,
                 w_out=m_w_out, lb_param=m_lb_param, hgrn_norm_g=m_hgrn_norm_g, attn_sinks=m_attn_sinks,
                 conv_w=m_conv_w, rel_bias=m_rel_bias, ln_g=m_ln_g, ln_b=m_ln_b)
    mom_v = dict(w_in=v_w_in, w_proj_hgrn=v_w_proj_hgrn, w_proj_attn=v_w_proj_attn, w_proj_conv=v_w_proj_conv,
                 w_out=v_w_out, lb_param=v_lb_param, hgrn_norm_g=v_hgrn_norm_g, attn_sinks=v_attn_sinks,
                 conv_w=v_conv_w, rel_bias=v_rel_bias, ln_g=v_ln_g, ln_b=v_ln_b)
    d = x.shape[-1]
    me = 4 * lax.axis_index("x") + 2 * lax.axis_index("y") + lax.axis_index("c")
    for group in (params, mom_m, mom_v):
        group["w_in"] = jnp.swapaxes(group["w_in"], 1, 2)

    def shards_of(l):
        return [params[n][l].astype(BF16) for n in BIG] + [conv_w[l]]

    gathers = {}

    def weights_hook(l, x_in):
        if l == 0:
            got = _gather_two_level(shards_of(0), "gather_weights_0")
        else:
            got = _exchange_wait(gathers.pop(l), False, f"gather_wait_{l}", x_in)
        token = None
        if l + 1 < DEPTH:
            gathers[l + 1] = _exchange_start(shards_of(l + 1), False, f"gather_start_{l + 1}", dep=got[0])
            token = gathers[l + 1][-1]
        wl = {
            "w_in_t": got[0].reshape(-1, d),
            "w_proj_hgrn": _unshard_cols(got[1], "unshard_w_proj_hgrn"),
            "w_proj_attn": _unshard_cols(got[2], "unshard_w_proj_attn"),
            "w_proj_conv": _unshard_cols(got[3], "unshard_w_proj_conv"),
            "w_out": got[4].reshape(d, d),
            "conv_w": _unshard_cols(got[5], "unshard_conv_w"),
        }
        return wl, token

    grads = {n: [None] * DEPTH for n in BIG}
    scatters = {}

    def finish_scatter(l, after):
        got = _exchange_wait(scatters.pop(l), True, f"scatter_wait_{l}", after, chips=(l == 0))
        for n, slots in zip(BIG, got):
            grads[n][l] = _slot_sum(slots, f"sum_{n}_chips" if l == 0 else f"sum_{n}")
        return got[0]

    def grads_hook(l, g):
        send = [g["w_in_t"].reshape(N_DEV, -1, d), _shard_cols(g["w_proj_hgrn"], "shard_g_proj_a"),
                _shard_cols(g["w_proj_attn"], "shard_g_proj_b"), _shard_cols(g["w_proj_conv"], "shard_g_proj_c"),
                g["w_out"].reshape(N_DEV, d // N_DEV, d)]
        dep = finish_scatter(l + 1, send[0]) if l + 1 < DEPTH else None
        if l == 0:
            core = lax.axis_index("c").astype(jnp.int32).reshape(1)
            staged = _sibling_swap(send, "pair_swap_grads")
            send = [_pair_sum(s, st, core, f"pair_sum_{n}") for n, s, st in zip(BIG, send, staged)]
        scatters[l] = _exchange_start(send, True, f"scatter_start_{l}", dep=dep, chips=(l == 0))
        return scatters[l][-1]

    loss_acc, dx, small = _forward_backward(
        x[0], loss_target[0], weights_hook, grads_hook, lb_param, hgrn_norm_g, attn_sinks, rel_bias, ln_g, ln_b)
    loss = lax.psum(0.5 * jnp.sum(loss_acc[0]) / d, ("x", "y", "c"))
    finish_scatter(0, dx)
    for n in BIG:
        grads[n] = jnp.stack(grads[n], axis=0)

    small_shapes = [small[n].shape for n in SMALL]
    packed = _pack([small[n] for n in SMALL])
    got = _exchange([packed], False, "gather_small_grads")[0]
    summed = _unpack(_slot_sum(got, "sum_small_grads"), small_shapes)
    for n, g in zip(SMALL, summed):
        grads[n] = g
    cs = conv_w.shape[-1]
    grads["conv_w"] = lax.dynamic_slice_in_dim(grads["conv_w"], me * cs, cs, axis=2)

    delta, new_m, new_v = {}, {}, {}
    for n in BIG:
        shp = params[n].shape
        flat = lambda a: a.reshape(-1, shp[-1])
        dl, nm, nv = _adamw(flat(params[n]), flat(grads[n]), flat(mom_m[n]), flat(mom_v[n]), f"adamw_{n}")
        delta[n], new_m[n], new_v[n] = dl.reshape(shp), nm.reshape(shp), nv.reshape(shp)
    shapes = [params[n].shape for n in SMALL]
    res = _adamw(_pack([params[n] for n in SMALL]), _pack([grads[n] for n in SMALL]),
                 _pack([mom_m[n] for n in SMALL]), _pack([mom_v[n] for n in SMALL]), "adamw_small")
    for dst, packed_res in zip((delta, new_m, new_v), res):
        for n, a in zip(SMALL, _unpack(packed_res, shapes)):
            dst[n] = a

    for group in (grads, delta, new_m, new_v):
        group["w_in"] = jnp.swapaxes(group["w_in"], 1, 2)
    return (loss, dx[None], *[grads[n] for n in ORDER], *[delta[n] for n in ORDER],
            *[new_m[n] for n in ORDER], *[new_v[n] for n in ORDER])
```

```python
import functools
import math

import numpy as np
import jax
import jax.numpy as jnp
from jax import lax
from jax.experimental import pallas as pl
from jax.experimental.pallas import tpu as pltpu

F32 = jnp.float32
BF16 = jnp.bfloat16

N_DEV = 8
DEPTH = 4
HGRN_HEAD_DIM = 128
HGRN_CHUNK = 64
ATTN_HEAD_DIM = 64
ATTN_KV_HEADS = 4
KV_WIDTH = ATTN_KV_HEADS * ATTN_HEAD_DIM
WINDOW = 128
WINDOW_SHIFT = 7
N_BUCKETS = 32
MAX_DISTANCE = 128
ALPHA = (2.0 * DEPTH) ** 0.25
LN_EPS = 1e-5
RMS_EPS = 1e-6
ADAM_LR = 0.001
ADAM_B1 = 0.9
ADAM_B2 = 0.999
ADAM_EPS = 1e-08
ADAM_WD = 0.01
ADAM_STEP = 10

LANES = 128
SUBLANES = 8
VMEM_LIMIT = 56 << 20
NEG_INF = float("-inf")


def _offsets(d_model):
    w = d_model // 2
    sizes = (w, w, w, w, w, KV_WIDTH, KV_WIDTH, w, w, w, w, w, d_model, d_model, d_model)
    names = ("a_q", "a_f", "a_i", "a_g", "b_q", "b_k", "b_v", "b_g", "c_b", "c_c", "c_x", "c_g", "m_a", "m_b", "m_c")
    off, o = {}, 0
    for n, s in zip(names, sizes):
        off[n] = o
        o += s
    return off, o


def _tile(n, pref):
    t = min(pref, n)
    while n % t:
        t //= 2
    return t


def _cp(sem=None, vmem=VMEM_LIMIT):
    return pltpu.CompilerParams(dimension_semantics=sem, vmem_limit_bytes=vmem)


def _sigmoid(x):
    return 1.0 / (1.0 + jnp.exp(-x))


def _dot_nn(a, b):
    return jnp.dot(a, b, preferred_element_type=F32)


def _dot_nt(a, b):
    return lax.dot_general(a, b, (((1,), (1,)), ((), ())), preferred_element_type=F32)


def _dot_tn(a, b):
    return lax.dot_general(a, b, (((0,), (0,)), ((), ())), preferred_element_type=F32)


def _dep_specs(dep):
    return ([], []) if dep is None else ([pl.BlockSpec(memory_space=pl.ANY)], [dep])


def _mm_nn(a, b, name, out_dtype=F32, tm=1024, tn=1536, dep=None):
    m, k = a.shape
    _, n = b.shape
    tm, tn = _tile(m, tm), _tile(n, tn)
    dep_specs, dep_args = _dep_specs(dep)

    def body(a_ref, b_ref, *rest):
        o_ref = rest[-1]
        o_ref[...] = _dot_nn(a_ref[...], b_ref[...]).astype(o_ref.dtype)

    return pl.pallas_call(
        body, grid=(n // tn, m // tm),
        in_specs=[pl.BlockSpec((tm, k), lambda j, i: (i, 0)), pl.BlockSpec((k, tn), lambda j, i: (0, j))] + dep_specs,
        out_specs=pl.BlockSpec((tm, tn), lambda j, i: (i, j)),
        out_shape=jax.ShapeDtypeStruct((m, n), out_dtype), name=name,
        compiler_params=_cp(("parallel", "parallel")))(a, b, *dep_args)


def _mm_nt(a, b, name, tm=1024, tk=1536, add=None, add_scale=1.0, dep=None):
    m, k = a.shape
    n, _ = b.shape
    tm, tk = _tile(m, tm), _tile(k, tk)
    has_add = add is not None
    dep_specs, dep_args = _dep_specs(dep)

    def body(*refs):
        if has_add:
            a_ref, b_ref, add_ref = refs[:3]
        else:
            a_ref, b_ref = refs[:2]
        o_ref = refs[-1]
        if k == tk:
            prod = _dot_nt(a_ref[...], b_ref[...])
            o_ref[...] = prod + add_ref[...] * add_scale if has_add else prod
            return

        @pl.when(pl.program_id(1) == 0)
        def _():
            if has_add:
                o_ref[...] = add_ref[...] * add_scale
            else:
                o_ref[...] = jnp.zeros_like(o_ref)

        o_ref[...] += _dot_nt(a_ref[...], b_ref[...])

    in_specs = [pl.BlockSpec((tm, tk), lambda i, kk: (i, kk)), pl.BlockSpec((n, tk), lambda i, kk: (0, kk))]
    args = [a, b]
    if has_add:
        in_specs.append(pl.BlockSpec((tm, n), lambda i, kk: (i, 0)))
        args.append(add)
    in_specs += dep_specs
    args += dep_args
    return pl.pallas_call(
        body, grid=(m // tm, k // tk), in_specs=in_specs,
        out_specs=pl.BlockSpec((tm, n), lambda i, kk: (i, 0)),
        out_shape=jax.ShapeDtypeStruct((m, n), F32), name=name,
        compiler_params=_cp(("parallel", "arbitrary")))(*args)


def _mm_nt_cols(a, b, name, tm=1024, tn=1536, dep=None):
    m, k = a.shape
    n, _ = b.shape
    tm, tn = _tile(m, tm), _tile(n, tn)
    dep_specs, dep_args = _dep_specs(dep)

    def body(a_ref, b_ref, *rest):
        rest[-1][...] = _dot_nt(a_ref[...], b_ref[...])

    return pl.pallas_call(
        body, grid=(n // tn, m // tm),
        in_specs=[pl.BlockSpec((tm, k), lambda j, i: (i, 0)), pl.BlockSpec((tn, k), lambda j, i: (j, 0))] + dep_specs,
        out_specs=pl.BlockSpec((tm, tn), lambda j, i: (i, j)),
        out_shape=jax.ShapeDtypeStruct((m, n), F32), name=name,
        compiler_params=_cp(("parallel", "parallel")))(a, b, *dep_args)


def _mm_nn_acc(a, b, name, tm=1024, tk=1536, add=None, add_scale=1.0, dep=None):
    m, k = a.shape
    _, n = b.shape
    tm, tk = _tile(m, tm), _tile(k, tk)
    dep_specs, dep_args = _dep_specs(dep)

    def body(a_ref, b_ref, add_ref, *rest):
        o_ref = rest[-1]

        @pl.when(pl.program_id(1) == 0)
        def _():
            o_ref[...] = add_ref[...] * add_scale

        o_ref[...] += _dot_nn(a_ref[...], b_ref[...])

    return pl.pallas_call(
        body, grid=(m // tm, k // tk),
        in_specs=[pl.BlockSpec((tm, tk), lambda i, kk: (i, kk)), pl.BlockSpec((tk, n), lambda i, kk: (kk, 0)),
                  pl.BlockSpec((tm, n), lambda i, kk: (i, 0))] + dep_specs,
        out_specs=pl.BlockSpec((tm, n), lambda i, kk: (i, 0)),
        out_shape=jax.ShapeDtypeStruct((m, n), F32), name=name,
        compiler_params=_cp(("parallel", "arbitrary")))(a, b, add, *dep_args)


def _mm_tn_rows(a, b, name, tt=1024, tr=1536):
    t, k = a.shape
    _, n = b.shape
    tt, tr = _tile(t, tt), _tile(k, tr)

    def body(a_ref, b_ref, o_ref):
        @pl.when(pl.program_id(1) == 0)
        def _():
            o_ref[...] = jnp.zeros_like(o_ref)

        o_ref[...] += _dot_tn(a_ref[...], b_ref[...])

    return pl.pallas_call(
        body, grid=(k // tr, t // tt),
        in_specs=[pl.BlockSpec((tt, tr), lambda j, s: (s, j)), pl.BlockSpec((tt, n), lambda j, s: (s, 0))],
        out_specs=pl.BlockSpec((tr, n), lambda j, s: (j, 0)),
        out_shape=jax.ShapeDtypeStruct((k, n), F32), name=name,
        compiler_params=_cp(("parallel", "arbitrary")))(a, b)


def _mm_tn(a, b, name, tt=1024, tn=1536):
    t, k = a.shape
    _, n = b.shape
    tt, tn = _tile(t, tt), _tile(n, tn)

    def body(a_ref, b_ref, o_ref):
        @pl.when(pl.program_id(1) == 0)
        def _():
            o_ref[...] = jnp.zeros_like(o_ref)

        o_ref[...] += _dot_tn(a_ref[...], b_ref[...])

    return pl.pallas_call(
        body, grid=(n // tn, t // tt),
        in_specs=[pl.BlockSpec((tt, k), lambda j, s: (s, 0)), pl.BlockSpec((tt, tn), lambda j, s: (s, j))],
        out_specs=pl.BlockSpec((k, tn), lambda j, s: (0, j)),
        out_shape=jax.ShapeDtypeStruct((k, n), F32), name=name,
        compiler_params=_cp(("parallel", "arbitrary")))(a, b)


def _ew(body, name, t, ncol, wb, ins, outs, accs=(), tt=512):
    tt = _tile(t, tt)
    nt = t // tt
    in_specs, args = [], []
    for arr, kind, coff in ins:
        if kind == "tile":
            spec = pl.BlockSpec((tt, wb), lambda j, i, c=coff: (i, c + j))
        elif kind == "prev":
            spec = pl.BlockSpec((tt, wb), lambda j, i, c=coff: (jnp.maximum(i - 1, 0), c + j))
        elif kind == "next":
            spec = pl.BlockSpec((tt, wb), lambda j, i, c=coff: (jnp.minimum(i + 1, nt - 1), c + j))
        else:
            spec = pl.BlockSpec((arr.shape[0], wb), lambda j, i, c=coff: (0, c + j))
        in_specs.append(spec)
        args.append(arr)
    out_specs = [pl.BlockSpec((tt, wb), lambda j, i: (i, j)) for _ in outs]
    out_shape = [jax.ShapeDtypeStruct((t, ncol * wb), d) for d in outs]
    for r in accs:
        out_specs.append(pl.BlockSpec((r, wb), lambda j, i: (0, j)))
        out_shape.append(jax.ShapeDtypeStruct((r, ncol * wb), F32))

    def kern(*refs):
        body(pl.program_id(1), nt, *refs)

    res = pl.pallas_call(
        kern, grid=(ncol, nt), in_specs=in_specs, out_specs=out_specs, out_shape=out_shape, name=name,
        compiler_params=_cp(("parallel", "arbitrary")))(*args)
    return res


def _silu_parts(x):
    s = _sigmoid(x)
    return x * s, s + x * s * (1.0 - s)


def _out_proj_ln(a, w, x, g, b, name):
    t, d = x.shape
    k = a.shape[1]
    tt = _tile(t, 256)

    def body(a_ref, w_ref, x_ref, g_ref, b_ref, o_ref, ob_ref, xh_ref, r_ref):
        z = ALPHA * x_ref[...] + _dot_nn(a_ref[...], w_ref[...])
        mu = jnp.mean(z, axis=1, keepdims=True)
        zc = z - mu
        var = jnp.mean(zc * zc, axis=1, keepdims=True)
        rstd = lax.rsqrt(var + LN_EPS)
        xh = zc * rstd
        o = xh * g_ref[...] + b_ref[...]
        o_ref[...] = o
        ob_ref[...] = o.astype(BF16)
        xh_ref[...] = xh
        r_ref[...] = rstd

    row = pl.BlockSpec((tt, d), lambda i: (i, 0))
    vec = pl.BlockSpec((1, d), lambda i: (0, 0))
    return pl.pallas_call(
        body, grid=(t // tt,),
        in_specs=[pl.BlockSpec((tt, k), lambda i: (i, 0)), pl.BlockSpec((k, d), lambda i: (0, 0)), row, vec, vec],
        out_specs=[row, row, row, pl.BlockSpec((tt, 1), lambda i: (i, 0))],
        out_shape=[jax.ShapeDtypeStruct((t, d), F32), jax.ShapeDtypeStruct((t, d), BF16),
                   jax.ShapeDtypeStruct((t, d), F32), jax.ShapeDtypeStruct((t, 1), F32)],
        name=name, compiler_params=_cp(("parallel",)))(a, w, x, g, b)


def _ln_bwd_out_proj(dout, xhat, rstd, g, w, name):
    t, d = dout.shape
    k = w.shape[0]
    tt = _tile(t, 256)

    def body(do_ref, xh_ref, r_ref, g_ref, w_ref, dz_ref, dzb_ref, acc_ref, dm_ref):
        @pl.when(pl.program_id(0) == 0)
        def _():
            acc_ref[...] = jnp.zeros_like(acc_ref)

        do = do_ref[...]
        xh = xh_ref[...]
        dxh = do * g_ref[...]
        m1 = jnp.mean(dxh, axis=1, keepdims=True)
        m2 = jnp.mean(dxh * xh, axis=1, keepdims=True)
        dz = r_ref[...] * (dxh - m1 - xh * m2)
        dzb = dz.astype(BF16)
        dz_ref[...] = dz
        dzb_ref[...] = dzb
        dm_ref[...] = _dot_nt(dzb, w_ref[...])
        acc_ref[0:1, :] += jnp.sum(do * xh, axis=0, keepdims=True)
        acc_ref[1:2, :] += jnp.sum(do, axis=0, keepdims=True)

    row = pl.BlockSpec((tt, d), lambda i: (i, 0))
    return pl.pallas_call(
        body, grid=(t // tt,),
        in_specs=[row, row, pl.BlockSpec((tt, 1), lambda i: (i, 0)), pl.BlockSpec((1, d), lambda i: (0, 0)),
                  pl.BlockSpec((k, d), lambda i: (0, 0))],
        out_specs=[row, row, pl.BlockSpec((SUBLANES, d), lambda i: (0, 0)), pl.BlockSpec((tt, k), lambda i: (i, 0))],
        out_shape=[jax.ShapeDtypeStruct((t, d), F32), jax.ShapeDtypeStruct((t, d), BF16),
                   jax.ShapeDtypeStruct((SUBLANES, d), F32), jax.ShapeDtypeStruct((t, k), F32)],
        name=name, compiler_params=_cp(("arbitrary",)))(dout, xhat, rstd, g, w)


def _loss_head(y, target):
    t, d = y.shape
    tt = _tile(t, 256)

    def body(y_ref, t_ref, acc_ref, dy_ref):
        @pl.when(pl.program_id(0) == 0)
        def _():
            acc_ref[...] = jnp.zeros_like(acc_ref)

        err = y_ref[...] - t_ref[...]
        dy_ref[...] = err * (1.0 / d)
        acc_ref[0:1, :] += jnp.sum(err * err, axis=0, keepdims=True)

    row = pl.BlockSpec((tt, d), lambda i: (i, 0))
    acc, dy = pl.pallas_call(
        body, grid=(t // tt,), in_specs=[row, row],
        out_specs=[pl.BlockSpec((SUBLANES, d), lambda i: (0, 0)), row],
        out_shape=[jax.ShapeDtypeStruct((SUBLANES, d), F32), jax.ShapeDtypeStruct((t, d), F32)],
        name="loss_head", compiler_params=_cp(("arbitrary",)))(y, target)
    return acc, dy


def _tri(lower):
    r = lax.broadcasted_iota(jnp.int32, (HGRN_CHUNK, HGRN_CHUNK), 0)
    c = lax.broadcasted_iota(jnp.int32, (HGRN_CHUNK, HGRN_CHUNK), 1)
    return jnp.where((r >= c) if lower else (r <= c), 1.0, 0.0).astype(BF16)


def _exact_tri_matmul(tri, x):
    hi = x.astype(BF16)
    r1 = x - hi.astype(F32)
    mid = r1.astype(BF16)
    lo = (r1 - mid.astype(F32)).astype(BF16)
    return _dot_nn(tri, hi) + _dot_nn(tri, mid) + _dot_nn(tri, lo)


def _hgrn_gates(q_raw, fl, lb):
    sq = _sigmoid(q_raw)
    qf = q_raw * sq * (HGRN_HEAD_DIM ** -0.5)
    sg = _sigmoid(fl)
    f = lb + (1.0 - lb) * sg
    return qf, sq, sg, f


HGRN_SUB = 16
HGRN_NSUB = HGRN_CHUNK // HGRN_SUB
HGRN_HEADS_PER_STEP = 8
HGRN_HEADS_PER_STEP_BWD = 8


def _diag_rows(r):
    return (r // SUBLANES) * SUBLANES


def _heads(x):
    hd = HGRN_HEAD_DIM
    return [x[:, i * hd:(i + 1) * hd] for i in range(x.shape[1] // hd)]


def _per_head(fn, *xs):
    split = [x if isinstance(x, (list, tuple)) else _heads(x) for x in xs]
    return jnp.concatenate([fn(*hs) for hs in zip(*split)], axis=1)


def _head_lane_sum(x):
    return _per_head(lambda h: jnp.broadcast_to(jnp.sum(h, axis=1, keepdims=True), h.shape), x)


def _hgrn_intra_fwd(qf, k, v, b):
    ch, sub, wd = HGRN_CHUNK, HGRN_SUB, qf.shape[1]
    tl = lax.broadcasted_iota(jnp.int32, (sub, wd), 0)
    blocks = []
    for m in range(HGRN_NSUB):
        rs = slice(m * sub, (m + 1) * sub)
        bm, qm, km, vm = b[rs], qf[rs], k[rs], v[rs]
        parts = {0: jnp.zeros((sub, wd), F32), SUBLANES: jnp.zeros((sub - SUBLANES, wd), F32)}
        for r in range(sub):
            lo = _diag_rows(r)
            e = jnp.exp(jnp.where(tl[lo:] >= r, bm[lo:] - bm[r:r + 1], NEG_INF))
            parts[lo] = parts[lo] + _head_lane_sum(qm[lo:] * e * km[r:r + 1]) * vm[r:r + 1]
        blocks.append(parts[0] + jnp.concatenate([jnp.zeros((SUBLANES, wd), F32), parts[SUBLANES]], axis=0))
    acc = jnp.concatenate(blocks, axis=0)
    for j in range(HGRN_NSUB - 1):
        lo = sub * (j + 1)
        c = b[lo - 1:lo, :]
        qj = (qf[lo:] * jnp.exp(b[lo:] - c)).astype(BF16)
        kj = (k[lo - sub:lo] * jnp.exp(c - b[lo - sub:lo])).astype(BF16)
        vj = v[lo - sub:lo].astype(BF16)
        contrib = _per_head(lambda q_, k_, v_: _dot_nn(_dot_nt(q_, k_).astype(BF16), v_), qj, kj, vj)
        acc = acc + jnp.concatenate([jnp.zeros((lo, wd), F32), contrib], axis=0)
    return acc


def _hgrn_intra_bwd(qf, k, v, b, do_v):
    ch, sub, wd = HGRN_CHUNK, HGRN_SUB, qf.shape[1]
    tl = lax.broadcasted_iota(jnp.int32, (sub, wd), 0)
    dq_blocks, dk_blocks, dv_blocks = [], [], []
    for m in range(HGRN_NSUB):
        rs = slice(m * sub, (m + 1) * sub)
        bm, qm, km, vm, dom = b[rs], qf[rs], k[rs], v[rs], do_v[rs]
        parts = {0: jnp.zeros((sub, wd), F32), SUBLANES: jnp.zeros((sub - SUBLANES, wd), F32)}
        dk_parts = {sub: jnp.zeros((sub, wd), F32), SUBLANES: jnp.zeros((SUBLANES, wd), F32)}
        dv_parts = {sub: jnp.zeros((sub, wd), F32), SUBLANES: jnp.zeros((SUBLANES, wd), F32)}
        for r in range(sub):
            lo = _diag_rows(r)
            b_r, k_r, v_r, q_r, do_r = bm[r:r + 1], km[r:r + 1], vm[r:r + 1], qm[r:r + 1], dom[r:r + 1]
            e = jnp.exp(jnp.where(tl[lo:] >= r, bm[lo:] - b_r, NEG_INF))
            parts[lo] = parts[lo] + _head_lane_sum(dom[lo:] * v_r) * (k_r * e)
            hi = lo + SUBLANES
            e2 = jnp.exp(jnp.where(tl[:hi] <= r, b_r - bm[:hi], NEG_INF))
            qe2 = q_r * e2
            dk_parts[hi] = dk_parts[hi] + _head_lane_sum(vm[:hi] * do_r) * qe2
            dv_parts[hi] = dv_parts[hi] + _head_lane_sum(km[:hi] * qe2) * do_r
        pad = jnp.zeros((SUBLANES, wd), F32)
        dq_blocks.append(parts[0] + jnp.concatenate([pad, parts[SUBLANES]], axis=0))
        dk_blocks.append(dk_parts[sub] + jnp.concatenate([dk_parts[SUBLANES], pad], axis=0))
        dv_blocks.append(dv_parts[sub] + jnp.concatenate([dv_parts[SUBLANES], pad], axis=0))
    dq = jnp.concatenate(dq_blocks, axis=0)
    dk = jnp.concatenate(dk_blocks, axis=0)
    dv = jnp.concatenate(dv_blocks, axis=0)
    do_b, v_b = do_v.astype(BF16), v.astype(BF16)
    dk_off, dv_off = [], []
    for j in range(HGRN_NSUB - 1):
        lo = sub * (j + 1)
        c = b[lo - 1:lo, :]
        eq = jnp.exp(b[lo:] - c)
        ek = jnp.exp(c - b[lo - sub:lo])
        qj = (qf[lo:] * eq).astype(BF16)
        kj = (k[lo - sub:lo] * ek).astype(BF16)
        doj, vj = do_b[lo:], v_b[lo - sub:lo]
        dq_j = _per_head(lambda do_, v_, k_: _dot_nn(_dot_nt(do_, v_).astype(BF16), k_), doj, vj, kj)
        dk_j = _per_head(lambda do_, v_, q_: _dot_nn(_dot_nt(v_, do_).astype(BF16), q_), doj, vj, qj)
        dv_j = _per_head(lambda do_, k_, q_: _dot_nn(_dot_nt(k_, q_).astype(BF16), do_), doj, kj, qj)
        dq = dq + jnp.concatenate([jnp.zeros((lo, wd), F32), dq_j * eq], axis=0)
        dk_off.append(dk_j * ek)
        dv_off.append(dv_j)
    zero = jnp.zeros((sub, wd), F32)
    dk = dk + jnp.concatenate(dk_off + [zero], axis=0)
    dv = dv + jnp.concatenate(dv_off + [zero], axis=0)
    return dq, dk, dv


def _head_rms(o):
    return lax.rsqrt(_head_lane_sum(o * o) * (1.0 / HGRN_HEAD_DIM) + RMS_EPS)


def _hgrn_fwd(u, lb, gain, off, name):
    t = u.shape[0]
    w = lb.shape[1]
    hd, ch, hp = HGRN_HEAD_DIM, HGRN_CHUNK, HGRN_HEADS_PER_STEP
    nh, nc = w // hd, t // ch
    wb = hp * hd
    cq, cf, cv, cg = off["a_q"] // wb, off["a_f"] // wb, off["a_i"] // wb, off["a_g"] // wb

    def body(q_ref, f_ref, v_ref, g_ref, lb_ref, gain_ref, o_ref, st_ref, p_ref, state):
        @pl.when(pl.program_id(1) == 0)
        def _():
            state[...] = jnp.zeros_like(state)

        sts = [state[i] for i in range(hp)]
        qf, _, _, f = _hgrn_gates(q_ref[...], f_ref[...], lb_ref[...])
        k = 1.0 - f
        v = v_ref[...]
        b = _exact_tri_matmul(_tri(True), jnp.log(f))
        inter = _per_head(lambda qa_, st_: _dot_nt(qa_, st_.astype(BF16)), (qf * jnp.exp(b)).astype(BF16), sts)
        o = inter + _hgrn_intra_fwd(qf, k, v, b)
        o_ref[...] = o
        silu, _ = _silu_parts(g_ref[...])
        p_ref[...] = (o * _head_rms(o) * gain_ref[...] * silu).astype(BF16)
        b_end = b[ch - 1:ch, :]
        a_end = _heads(jnp.exp(b_end))
        kd = _heads((k * jnp.exp(b_end - b)).astype(BF16))
        v_b = _heads(v.astype(BF16))
        for i in range(hp):
            st_ref[i, 0] = sts[i]
            state[i] = sts[i] * a_end[i] + _dot_tn(v_b[i], kd[i])

    return pl.pallas_call(
        body, grid=(nh // hp, nc),
        in_specs=[pl.BlockSpec((ch, wb), lambda h, n: (n, cq + h)),
                  pl.BlockSpec((ch, wb), lambda h, n: (n, cf + h)),
                  pl.BlockSpec((ch, wb), lambda h, n: (n, cv + h)),
                  pl.BlockSpec((ch, wb), lambda h, n: (n, cg + h)),
                  pl.BlockSpec((1, wb), lambda h, n: (0, h)),
                  pl.BlockSpec((1, wb), lambda h, n: (0, h))],
        out_specs=[pl.BlockSpec((ch, wb), lambda h, n: (n, h)),
                   pl.BlockSpec((hp, 1, hd, hd), lambda h, n: (h, n, 0, 0)),
                   pl.BlockSpec((ch, wb), lambda h, n: (n, h))],
        out_shape=[jax.ShapeDtypeStruct((t, w), F32), jax.ShapeDtypeStruct((nh, nc, hd, hd), F32),
                   jax.ShapeDtypeStruct((t, w), BF16)],
        scratch_shapes=[pltpu.VMEM((hp, hd, hd), F32)],
        name=name, compiler_params=_cp(("parallel", "arbitrary")))(u, u, u, u, lb, gain)


def _hgrn_bwd(u, lb, gain, states, o, dp, off, name):
    t = u.shape[0]
    w = lb.shape[1]
    hd, ch, hp = HGRN_HEAD_DIM, HGRN_CHUNK, HGRN_HEADS_PER_STEP_BWD
    nh, nc = w // hd, t // ch
    wb = hp * hd
    cq, cf, cv, cg = off["a_q"] // wb, off["a_f"] // wb, off["a_i"] // wb, off["a_g"] // wb

    def body(q_ref, f_ref, v_ref, g_ref, o_ref, dp_ref, st_ref, lb_ref, gain_ref,
             dq_ref, df_ref, dv_ref, dg_ref, dlb_ref, dstate):
        @pl.when(pl.program_id(1) == 0)
        def _():
            dstate[...] = jnp.zeros_like(dstate)
            dlb_ref[...] = jnp.zeros_like(dlb_ref)

        silu, dsilu = _silu_parts(g_ref[...])
        o_v, dp_v, gain_row = o_ref[...], dp_ref[...], gain_ref[...]
        rms = _head_rms(o_v)
        nrm = o_v * rms
        dg_ref[...] = (dp_v * nrm * gain_row * dsilu).astype(BF16)
        dlb_ref[1:2, :] += jnp.sum(dp_v * nrm * silu, axis=0, keepdims=True)
        dn = dp_v * gain_row * silu
        do_v = rms * (dn - nrm * (_head_lane_sum(dn * nrm) * (1.0 / HGRN_HEAD_DIM)))

        rows = lax.broadcasted_iota(jnp.int32, (ch, wb), 0)
        lb_row = lb_ref[...]
        q_raw = q_ref[...]
        qf, sq, sg, f = _hgrn_gates(q_raw, f_ref[...], lb_row)
        k = 1.0 - f
        b = _exact_tri_matmul(_tri(True), jnp.log(f))
        a = jnp.exp(b)
        b_end = b[ch - 1:ch, :]
        a_end = jnp.exp(b_end)
        to_end = jnp.exp(b_end - b)
        v = v_ref[...]
        st0 = [st_ref[i, 0] for i in range(hp)]
        ds = [dstate[i] for i in range(hp)]
        st0_b = [s_.astype(BF16) for s_ in st0]
        ds_b = [s_.astype(BF16) for s_ in ds]
        do_b, v_b, kd_b, qa_b = do_v.astype(BF16), v.astype(BF16), (k * to_end).astype(BF16), (qf * a).astype(BF16)

        dq_inter = a * _per_head(_dot_nn, do_b, st0_b)
        dk_end = to_end * _per_head(_dot_nn, v_b, ds_b)
        dv_end = _per_head(_dot_nt, kd_b, ds_b)
        a_end_h = _heads(a_end)
        st_end = [st0[i] * a_end_h[i] + _dot_tn(_heads(v_b)[i], _heads(kd_b)[i]) for i in range(hp)]
        db_end = jnp.concatenate([jnp.sum(ds[i] * st_end[i], axis=0, keepdims=True) for i in range(hp)], axis=1)
        ds_new = [ds[i] * a_end_h[i] + _dot_tn(_heads(do_b)[i], _heads(qa_b)[i]) for i in range(hp)]

        dq_intra, dk_intra, dv_intra = _hgrn_intra_bwd(qf, k, v, b, do_v)
        dqf = dq_inter + dq_intra
        dk = dk_end + dk_intra
        dv = dv_end + dv_intra
        db = qf * dqf - k * dk
        db = db + jnp.where(rows == ch - 1, db_end, 0.0)
        dg = _exact_tri_matmul(_tri(False), db)
        df = dg / f - dk
        for i in range(hp):
            dstate[i] = ds_new[i]
        dq_ref[...] = (dqf * (HGRN_HEAD_DIM ** -0.5) * (sq + q_raw * sq * (1.0 - sq))).astype(BF16)
        df_ref[...] = (df * (1.0 - lb_row) * sg * (1.0 - sg)).astype(BF16)
        dv_ref[...] = dv.astype(BF16)
        dlb_ref[0:1, :] += jnp.sum(df * (1.0 - sg), axis=0, keepdims=True)

    rev = lambda n: nc - 1 - n
    tile = lambda c: pl.BlockSpec((ch, wb), lambda h, n, c=c: (rev(n), c + h))
    return pl.pallas_call(
        body, grid=(nh // hp, nc),
        in_specs=[tile(cq), tile(cf), tile(cv), tile(cg), tile(0), tile(0),
                  pl.BlockSpec((hp, 1, hd, hd), lambda h, n: (h, rev(n), 0, 0)),
                  pl.BlockSpec((1, wb), lambda h, n: (0, h)), pl.BlockSpec((1, wb), lambda h, n: (0, h))],
        out_specs=[tile(0), tile(0), tile(0), tile(0), pl.BlockSpec((SUBLANES, wb), lambda h, n: (0, h))],
        out_shape=[jax.ShapeDtypeStruct((t, w), BF16)] * 4 + [jax.ShapeDtypeStruct((SUBLANES, w), F32)],
        scratch_shapes=[pltpu.VMEM((hp, hd, hd), F32)],
        name=name, compiler_params=_cp(("parallel", "arbitrary")))(u, u, u, u, o, dp, states, lb, gain)


def _bucket_map():
    i = np.arange(WINDOW)[:, None]
    j = np.arange(2 * WINDOW)[None, :]
    dist = np.clip(WINDOW + i - j, 0, WINDOW - 1)
    max_exact = N_BUCKETS // 2
    logd = (np.log(np.maximum(dist, 1).astype(np.float32) / max_exact) / math.log(MAX_DISTANCE / max_exact))
    large = np.minimum(max_exact + (logd.astype(np.float32) * (N_BUCKETS - max_exact)).astype(np.int32), N_BUCKETS - 1)
    return np.where(dist < max_exact, dist, large).astype(np.int32)


def _bias_table(rel_bias, n_heads):
    bucket = jnp.asarray(_bucket_map())

    def body(rb_ref, bk_ref, o_ref):
        bk = bk_ref[...]
        i = lax.broadcasted_iota(jnp.int32, (WINDOW, 2 * WINDOW), 0)
        j = lax.broadcasted_iota(jnp.int32, (WINDOW, 2 * WINDOW), 1)
        band = ((j >= WINDOW) & (j - WINDOW <= i)) | ((j < WINDOW) & (j > i))
        for h in range(n_heads):
            def step(bi, acc):
                return jnp.where(bk == bi, rb_ref[bi, h], acc)
            table = lax.fori_loop(0, N_BUCKETS, step, jnp.zeros((WINDOW, 2 * WINDOW), F32))
            o_ref[h] = jnp.where(band, table, NEG_INF)

    return pl.pallas_call(
        body, in_specs=[pl.BlockSpec(memory_space=pltpu.SMEM), pl.BlockSpec(memory_space=pltpu.VMEM)],
        out_specs=pl.BlockSpec(memory_space=pltpu.VMEM),
        out_shape=jax.ShapeDtypeStruct((n_heads, WINDOW, 2 * WINDOW), F32), name="bias_table",
        compiler_params=_cp())(rel_bias, bucket)


def _bias_grad(dbias, n_heads):
    bucket = jnp.asarray(_bucket_map())

    def body(db_ref, bk_ref, o_ref):
        bk = bk_ref[...]
        lane = lax.broadcasted_iota(jnp.int32, (1, LANES), 1)

        def step(bi, carry):
            row = jnp.zeros((1, LANES), F32)
            for h in range(n_heads):
                val = jnp.sum(jnp.where(bk == bi, db_ref[h], 0.0))
                row = jnp.where(lane == h, val, row)
            o_ref[pl.ds(bi, 1), :] = row
            return carry

        lax.fori_loop(0, N_BUCKETS, step, 0)

    return pl.pallas_call(
        body, in_specs=[pl.BlockSpec(memory_space=pltpu.VMEM), pl.BlockSpec(memory_space=pltpu.VMEM)],
        out_specs=pl.BlockSpec(memory_space=pltpu.VMEM),
        out_shape=jax.ShapeDtypeStruct((N_BUCKETS, LANES), F32), name="bias_grad",
        compiler_params=_cp())(dbias, bucket)


def _no_prev_block(n, grp):
    j = lax.broadcasted_iota(jnp.int32, (grp * WINDOW, 2 * WINDOW), 1)
    return (j < WINDOW) & (n == 0)


def _attn_probs(no_prev, q_ref, kp_ref, kc_ref, bias_ref, sink_ref, hh, grp):
    ad, wn = ATTN_HEAD_DIM, WINDOW
    ksl = slice(hh * ad, (hh + 1) * ad)
    kw = jnp.concatenate([kp_ref[:, ksl], kc_ref[:, ksl]], axis=0).astype(BF16)
    qs = jnp.concatenate([q_ref[:, (hh * grp + g) * ad:(hh * grp + g + 1) * ad] for g in range(grp)], axis=0).astype(BF16)
    s = _dot_nt(qs, kw) * (ad ** -0.5) + bias_ref[hh]
    s = jnp.where(no_prev, NEG_INF, s)
    rr = lax.broadcasted_iota(jnp.int32, (grp * wn, 1), 0) >> WINDOW_SHIFT
    sink = jnp.zeros((grp * wn, 1), F32)
    for g in range(grp):
        sink = jnp.where(rr == g, sink_ref[hh * grp + g], sink)
    m = jnp.maximum(jnp.max(s, axis=1, keepdims=True), sink)
    p = jnp.exp(s - m)
    es = jnp.exp(sink - m)
    inv = 1.0 / (jnp.sum(p, axis=1, keepdims=True) + es)
    return qs, kw, p * inv, es * inv


GATE_BLOCK = 512


def _attn_fwd(u, bias_g, sinks, off, w, name):
    t = u.shape[0]
    wn, ad, kvw, gb = WINDOW, ATTN_HEAD_DIM, KV_WIDTH, GATE_BLOCK
    grp = (w // ad) // ATTN_KV_HEADS
    nb = t // wn
    n_gb = w // gb
    cq, ck, cv, cg = off["b_q"] // w, off["b_k"] // kvw, off["b_v"] // kvw, off["b_g"] // gb

    def body(q_ref, kp_ref, kc_ref, vp_ref, vc_ref, bias_ref, sink_ref, *rest):
        g_refs, (o_ref, p_ref) = rest[:n_gb], rest[n_gb:]
        no_prev = _no_prev_block(pl.program_id(0), grp)
        for hh in range(ATTN_KV_HEADS):
            _, _, p, _ = _attn_probs(no_prev, q_ref, kp_ref, kc_ref, bias_ref, sink_ref, hh, grp)
            ksl = slice(hh * ad, (hh + 1) * ad)
            vw = jnp.concatenate([vp_ref[:, ksl], vc_ref[:, ksl]], axis=0).astype(BF16)
            o = _dot_nn(p.astype(BF16), vw)
            for g in range(grp):
                o_ref[:, (hh * grp + g) * ad:(hh * grp + g + 1) * ad] = o[g * wn:(g + 1) * wn]
        for i in range(n_gb):
            sl = slice(i * gb, (i + 1) * gb)
            silu, _ = _silu_parts(g_refs[i][...])
            p_ref[:, sl] = (o_ref[:, sl] * silu).astype(BF16)

    prev = lambda n: jnp.maximum(n - 1, 0)
    row = pl.BlockSpec((wn, w), lambda n: (n, 0))
    return pl.pallas_call(
        body, grid=(nb,),
        in_specs=[pl.BlockSpec((wn, w), lambda n: (n, cq)),
                  pl.BlockSpec((wn, kvw), lambda n: (prev(n), ck)), pl.BlockSpec((wn, kvw), lambda n: (n, ck)),
                  pl.BlockSpec((wn, kvw), lambda n: (prev(n), cv)), pl.BlockSpec((wn, kvw), lambda n: (n, cv)),
                  pl.BlockSpec((ATTN_KV_HEADS, grp * wn, 2 * wn), lambda n: (0, 0, 0)),
                  pl.BlockSpec(memory_space=pltpu.SMEM)]
        + [pl.BlockSpec((wn, gb), lambda n, i=i: (n, cg + i)) for i in range(n_gb)],
        out_specs=[row, row],
        out_shape=[jax.ShapeDtypeStruct((t, w), F32), jax.ShapeDtypeStruct((t, w), BF16)], name=name,
        compiler_params=_cp(("parallel",)))(u, u, u, u, u, bias_g, sinks, *([u] * n_gb))


def _attn_bwd(u, o, dp, bias_g, sinks, off, w, name):
    t = u.shape[0]
    wn, ad, kvw, gb = WINDOW, ATTN_HEAD_DIM, KV_WIDTH, GATE_BLOCK
    grp = (w // ad) // ATTN_KV_HEADS
    nb = t // wn
    n_gb = w // gb
    cq, ck, cv, cg = off["b_q"] // w, off["b_k"] // kvw, off["b_v"] // kvw, off["b_g"] // gb

    def body(q_ref, kp_ref, kc_ref, vp_ref, vc_ref, o_ref, dp_ref, bias_ref, sink_ref, *rest):
        g_refs = rest[:n_gb]
        dq_ref, dkc_ref, dkp_ref, dvc_ref, dvp_ref, dbias_ref, dsink_ref, dg_ref, do_ref = rest[n_gb:]
        n = pl.program_id(0)

        @pl.when(n == 0)
        def _():
            dbias_ref[...] = jnp.zeros_like(dbias_ref)
            dsink_ref[...] = jnp.zeros_like(dsink_ref)

        for i in range(n_gb):
            sl = slice(i * gb, (i + 1) * gb)
            silu, dsilu = _silu_parts(g_refs[i][...])
            dp_v = dp_ref[:, sl]
            do_ref[:, sl] = dp_v * silu
            dg_ref[:, sl] = (dp_v * o_ref[:, sl] * dsilu).astype(BF16)

        lane = lax.broadcasted_iota(jnp.int32, (1, LANES), 1)
        rr = lax.broadcasted_iota(jnp.int32, (grp * wn, 1), 0) >> WINDOW_SHIFT
        dsink_row = jnp.zeros((1, LANES), F32)
        no_prev = _no_prev_block(n, grp)
        for hh in range(ATTN_KV_HEADS):
            qs, kw, p, psink = _attn_probs(no_prev, q_ref, kp_ref, kc_ref, bias_ref, sink_ref, hh, grp)
            ksl = slice(hh * ad, (hh + 1) * ad)
            vw = jnp.concatenate([vp_ref[:, ksl], vc_ref[:, ksl]], axis=0).astype(BF16)
            hs = [slice((hh * grp + g) * ad, (hh * grp + g + 1) * ad) for g in range(grp)]
            dos = jnp.concatenate([do_ref[:, sl] for sl in hs], axis=0)
            os_ = jnp.concatenate([o_ref[:, sl] for sl in hs], axis=0)
            delta = jnp.sum(dos * os_, axis=1, keepdims=True)
            dos_b = dos.astype(BF16)
            dp = _dot_nt(dos_b, vw)
            ds = p * (dp - delta)
            dbias_ref[hh] += ds
            sd = psink * delta
            for g in range(grp):
                val = -jnp.sum(jnp.where(rr == g, sd, 0.0))
                dsink_row = jnp.where(lane == hh * grp + g, val, dsink_row)
            ds_b = (ds * (ad ** -0.5)).astype(BF16)
            dq = _dot_nn(ds_b, kw)
            for g in range(grp):
                dq_ref[:, hs[g]] = dq[g * wn:(g + 1) * wn].astype(BF16)
            dkw = _dot_tn(ds_b, qs)
            dvw = _dot_tn(p.astype(BF16), dos_b)
            dkp_ref[:, ksl] = dkw[:wn]
            dkc_ref[:, ksl] = dkw[wn:]
            dvp_ref[:, ksl] = dvw[:wn]
            dvc_ref[:, ksl] = dvw[wn:]
        dsink_ref[0:1, :] += dsink_row

    prev = lambda n: jnp.maximum(n - 1, 0)
    kv_out = pl.BlockSpec((wn, kvw), lambda n: (n, 0))
    row = pl.BlockSpec((wn, w), lambda n: (n, 0))
    return pl.pallas_call(
        body, grid=(nb,),
        in_specs=[pl.BlockSpec((wn, w), lambda n: (n, cq)),
                  pl.BlockSpec((wn, kvw), lambda n: (prev(n), ck)), pl.BlockSpec((wn, kvw), lambda n: (n, ck)),
                  pl.BlockSpec((wn, kvw), lambda n: (prev(n), cv)), pl.BlockSpec((wn, kvw), lambda n: (n, cv)),
                  row, row,
                  pl.BlockSpec((ATTN_KV_HEADS, grp * wn, 2 * wn), lambda n: (0, 0, 0)),
                  pl.BlockSpec(memory_space=pltpu.SMEM)]
        + [pl.BlockSpec((wn, gb), lambda n, i=i: (n, cg + i)) for i in range(n_gb)],
        out_specs=[row, kv_out, kv_out, kv_out, kv_out,
                   pl.BlockSpec((ATTN_KV_HEADS, grp * wn, 2 * wn), lambda n: (0, 0, 0)),
                   pl.BlockSpec((SUBLANES, LANES), lambda n: (0, 0)), row],
        out_shape=[jax.ShapeDtypeStruct((t, w), BF16)] + [jax.ShapeDtypeStruct((t, kvw), F32)] * 4
        + [jax.ShapeDtypeStruct((ATTN_KV_HEADS, grp * wn, 2 * wn), F32), jax.ShapeDtypeStruct((SUBLANES, LANES), F32),
           jax.ShapeDtypeStruct((t, w), BF16)],
        scratch_shapes=[pltpu.VMEM((wn, w), F32)],
        name=name, compiler_params=_cp(("arbitrary",)))(u, u, u, u, u, o, dp, bias_g, sinks, *([u] * n_gb))


def _kv_combine(cur, prv, name):
    t, kvw = cur.shape

    def body(i, nt, c_ref, p_ref, o_ref):
        nxt = jnp.where(i < nt - 1, p_ref[...], 0.0)
        o_ref[...] = (c_ref[...] + nxt).astype(BF16)

    return _ew(body, name, t, 1, kvw, [(cur, "tile", 0), (prv, "next", 0)], [BF16], tt=WINDOW)[0]


def _shift_down(h, tail, k, rows):
    tt = h.shape[0]
    out = pltpu.roll(h, k, 0)
    for r in range(k):
        out = jnp.where(rows == r, tail[tt - k + r:tt - k + r + 1, :], out)
    return out


def _shift_up(h, head, k, rows):
    tt = h.shape[0]
    out = pltpu.roll(h, tt - k, 0)
    for r in range(k):
        out = jnp.where(rows == tt - k + r, head[r:r + 1, :], out)
    return out


def _conv_fwd(u, conv_w, off, w, name):
    t = u.shape[0]
    wb = 512
    c = lambda nme: off[nme] // wb

    def body(i, nt, cb_ref, cc_ref, ccp_ref, cx_ref, cxp_ref, cg_ref, w_ref, p_ref):
        h = cc_ref[...] * cx_ref[...]
        hp = jnp.where(i > 0, ccp_ref[...] * cxp_ref[...], 0.0)
        rows = lax.broadcasted_iota(jnp.int32, h.shape, 0)
        y = w_ref[0:1, :] * _shift_down(h, hp, 2, rows) + w_ref[1:2, :] * _shift_down(h, hp, 1, rows) + w_ref[2:3, :] * h
        silu, _ = _silu_parts(cg_ref[...])
        p_ref[...] = (cb_ref[...] * y * silu).astype(BF16)

    return _ew(body, name, t, w // wb, wb,
               [(u, "tile", c("c_b")), (u, "tile", c("c_c")), (u, "prev", c("c_c")), (u, "tile", c("c_x")),
                (u, "prev", c("c_x")), (u, "tile", c("c_g")), (conv_w, "row", 0)], [BF16])[0]


def _conv_bwd(dp, u, conv_w, off, w, name):
    t = u.shape[0]
    wb = 512
    c = lambda nme: off[nme] // wb

    def body(i, nt, dp_ref, dpn_ref, cb_ref, cbn_ref, cg_ref, cgn_ref, cc_ref, ccp_ref, cx_ref, cxp_ref, w_ref,
             dcb_ref, dcc_ref, dcx_ref, dcg_ref, acc_ref):
        @pl.when(i == 0)
        def _():
            acc_ref[...] = jnp.zeros_like(acc_ref)

        cc, cx, cb = cc_ref[...], cx_ref[...], cb_ref[...]
        h = cc * cx
        hp = jnp.where(i > 0, ccp_ref[...] * cxp_ref[...], 0.0)
        rows = lax.broadcasted_iota(jnp.int32, h.shape, 0)
        h1 = _shift_down(h, hp, 1, rows)
        h2 = _shift_down(h, hp, 2, rows)
        w0, w1, w2 = w_ref[0:1, :], w_ref[1:2, :], w_ref[2:3, :]
        y = w0 * h2 + w1 * h1 + w2 * h
        silu, dsilu = _silu_parts(cg_ref[...])
        dp_v = dp_ref[...]
        dcg_ref[...] = (dp_v * cb * y * dsilu).astype(BF16)
        dcb_ref[...] = (dp_v * y * silu).astype(BF16)
        dy = dp_v * cb * silu
        silu_n, _ = _silu_parts(cgn_ref[...])
        dyn = jnp.where(i < nt - 1, dpn_ref[...] * cbn_ref[...] * silu_n, 0.0)
        dh = w2 * dy + w1 * _shift_up(dy, dyn, 1, rows) + w0 * _shift_up(dy, dyn, 2, rows)
        dcc_ref[...] = (dh * cx).astype(BF16)
        dcx_ref[...] = (dh * cc).astype(BF16)
        acc_ref[0:1, :] += jnp.sum(dy * h2, axis=0, keepdims=True)
        acc_ref[1:2, :] += jnp.sum(dy * h1, axis=0, keepdims=True)
        acc_ref[2:3, :] += jnp.sum(dy * h, axis=0, keepdims=True)

    return _ew(body, name, t, w // wb, wb,
               [(dp, "tile", 0), (dp, "next", 0), (u, "tile", c("c_b")), (u, "next", c("c_b")),
                (u, "tile", c("c_g")), (u, "next", c("c_g")), (u, "tile", c("c_c")), (u, "prev", c("c_c")),
                (u, "tile", c("c_x")), (u, "prev", c("c_x")), (conv_w, "row", 0)],
               [BF16] * 4, accs=[SUBLANES])


def _merge_fwd(u, ya, yb, yc, off, d, name):
    t = u.shape[0]
    wb = 512
    c = lambda nme: off[nme] // wb

    def body(i, nt, ma_ref, mb_ref, mc_ref, ya_ref, yb_ref, yc_ref, o_ref):
        o_ref[...] = (_sigmoid(ma_ref[...]) * ya_ref[...] + _sigmoid(mb_ref[...]) * yb_ref[...]
                      + _sigmoid(mc_ref[...]) * yc_ref[...]).astype(BF16)

    return _ew(body, name, t, d // wb, wb,
               [(u, "tile", c("m_a")), (u, "tile", c("m_b")), (u, "tile", c("m_c")),
                (ya, "tile", 0), (yb, "tile", 0), (yc, "tile", 0)], [BF16])[0]


def _merge_bwd(dm, u, ya, yb, yc, off, d, name):
    t = u.shape[0]
    wb = 512
    c = lambda nme: off[nme] // wb

    def body(i, nt, dm_ref, ma_ref, mb_ref, mc_ref, ya_ref, yb_ref, yc_ref, da_ref, db_ref, dc_ref, ga_ref, gb_ref, gc_ref):
        dm_v = dm_ref[...]
        for m_ref, y_ref, dy_ref, dg_ref in ((ma_ref, ya_ref, da_ref, ga_ref), (mb_ref, yb_ref, db_ref, gb_ref),
                                             (mc_ref, yc_ref, dc_ref, gc_ref)):
            s = _sigmoid(m_ref[...])
            dy_ref[...] = (dm_v * s).astype(BF16)
            dg_ref[...] = (dm_v * y_ref[...] * s * (1.0 - s)).astype(BF16)

    return _ew(body, name, t, d // wb, wb,
               [(dm, "tile", 0), (u, "tile", c("m_a")), (u, "tile", c("m_b")), (u, "tile", c("m_c")),
                (ya, "tile", 0), (yb, "tile", 0), (yc, "tile", 0)], [BF16] * 6)


def _lower_bounds(lb_param):
    def body(p_ref, o_ref):
        p = p_ref[...]
        e = jnp.exp(p - jnp.max(p, axis=0, keepdims=True))
        soft = e / jnp.sum(e, axis=0, keepdims=True)
        acc = jnp.zeros_like(soft[0:1])
        o_ref[0:1, :] = acc
        for l in range(1, DEPTH):
            acc = acc + soft[l:l + 1]
            o_ref[l:l + 1, :] = acc

    return pl.pallas_call(body, out_shape=jax.ShapeDtypeStruct(lb_param.shape, F32), name="lower_bounds",
                          compiler_params=_cp())(lb_param)


def _lower_bounds_bwd(lb_param, dlower):
    def body(p_ref, d_ref, o_ref):
        p = p_ref[...]
        e = jnp.exp(p - jnp.max(p, axis=0, keepdims=True))
        soft = e / jnp.sum(e, axis=0, keepdims=True)
        dl = d_ref[...]
        ds = [jnp.zeros_like(dl[0:1])]
        for j in range(1, DEPTH):
            acc = dl[j:j + 1]
            for l in range(j + 1, DEPTH):
                acc = acc + dl[l:l + 1]
            ds.append(acc)
        inner = ds[0] * soft[0:1]
        for j in range(1, DEPTH):
            inner = inner + ds[j] * soft[j:j + 1]
        for j in range(DEPTH):
            o_ref[j:j + 1, :] = soft[j:j + 1] * (ds[j] - inner)

    return pl.pallas_call(body, out_shape=jax.ShapeDtypeStruct(lb_param.shape, F32), name="lower_bounds_bwd",
                          compiler_params=_cp())(lb_param, dlower)


def _exchange(arrays, scatter, name, chips=False):
    n_arr = len(arrays)
    n_slot = N_DEV // 2 if chips else N_DEV

    def body(*refs):
        srcs, dsts = refs[:n_arr], refs[n_arr:2 * n_arr]
        send_sems, recv_sems, local_sems = refs[2 * n_arr:]
        me = (2 * lax.axis_index("x") + lax.axis_index("y") if chips
              else 4 * lax.axis_index("x") + 2 * lax.axis_index("y") + lax.axis_index("c"))
        copies = _peer_copies(srcs, dsts, send_sems, recv_sems, scatter, chips)
        for a in range(n_arr):
            copies.append(pltpu.make_async_copy(srcs[a].at[me] if scatter else srcs[a], dsts[a].at[me], local_sems.at[a]))
        for cp in copies:
            cp.start()
        for cp in copies:
            cp.wait()

    out_shape = [jax.ShapeDtypeStruct(a.shape if scatter else (n_slot,) + a.shape, a.dtype) for a in arrays]
    anyspec = pl.BlockSpec(memory_space=pl.ANY)
    res = pl.pallas_call(
        body, in_specs=[anyspec] * n_arr, out_specs=[anyspec] * n_arr, out_shape=out_shape,
        scratch_shapes=[pltpu.SemaphoreType.DMA((n_arr * (n_slot - 1),)), pltpu.SemaphoreType.DMA((n_arr * (n_slot - 1),)),
                        pltpu.SemaphoreType.DMA((n_arr,))],
        name=name)(*arrays)
    return list(res)


def _peer_copies(srcs, lands, send_sems, recv_sems, scatter, chips=False):
    x, y, c = lax.axis_index("x"), lax.axis_index("y"), lax.axis_index("c")
    flips = [k for k in range(1, N_DEV) if not (chips and k & 1)]
    slot = (lambda px, py, pc: 2 * px + py) if chips else (lambda px, py, pc: 4 * px + 2 * py + pc)
    copies = []
    for a in range(len(srcs)):
        for i, k in enumerate(flips):
            px = 1 - x if k & 4 else x
            py = 1 - y if k & 2 else y
            pc = 1 - c if k & 1 else c
            src = srcs[a].at[slot(px, py, pc)] if scatter else srcs[a]
            copies.append(pltpu.make_async_remote_copy(
                src_ref=src, dst_ref=lands[a].at[slot(x, y, c)],
                send_sem=send_sems.at[a * len(flips) + i], recv_sem=recv_sems.at[a * len(flips) + i],
                device_id=(px, py, pc), device_id_type=pl.DeviceIdType.MESH))
    return copies


def _gather_two_level(arrays, name):
    n_arr = len(arrays)
    per = N_DEV - 1

    def body(*refs):
        srcs, outs = refs[:n_arr], refs[n_arr:2 * n_arr]
        send_sems, recv_sems, local_sems = refs[2 * n_arr:]
        x, y, c = lax.axis_index("x"), lax.axis_index("y"), lax.axis_index("c")
        me, sibling = (x, y, c), (x, y, 1 - c)
        chips = [(1 - x, y), (x, 1 - y), (1 - x, 1 - y)]

        def copy(a, k, block, to, src=None):
            dst = outs[a].at[4 * block[0] + 2 * block[1] + block[2]]
            return pltpu.make_async_remote_copy(
                src_ref=dst if src is None else src, dst_ref=dst,
                send_sem=send_sems.at[a * per + k], recv_sem=recv_sems.at[a * per + k],
                device_id=to, device_id_type=pl.DeviceIdType.MESH)

        own, first, passed = [], [], []
        for a in range(n_arr):
            own.append(pltpu.make_async_copy(srcs[a], outs[a].at[4 * x + 2 * y + c], local_sems.at[a]))
            first.append(copy(a, 0, me, sibling, src=srcs[a]))
            first += [copy(a, 1 + j, me, (*chip, c), src=srcs[a]) for j, chip in enumerate(chips)]
        for cp in own + first:
            cp.start()
        for a in range(n_arr):
            for j, chip in enumerate(chips):
                copy(a, 1 + j, (*chip, c), me).wait_recv()
                passed.append(copy(a, 4 + j, (*chip, c), sibling))
                passed[-1].start()
        for a in range(n_arr):
            copy(a, 0, sibling, me).wait_recv()
            for j, chip in enumerate(chips):
                copy(a, 4 + j, (*chip, 1 - c), me).wait_recv()
        for cp in first + passed:
            cp.wait_send()
        for cp in own:
            cp.wait()

    anyspec = pl.BlockSpec(memory_space=pl.ANY)
    res = pl.pallas_call(
        body, in_specs=[anyspec] * n_arr, out_specs=[anyspec] * n_arr,
        out_shape=[jax.ShapeDtypeStruct((N_DEV,) + a.shape, a.dtype) for a in arrays],
        scratch_shapes=[pltpu.SemaphoreType.DMA((n_arr * per,)), pltpu.SemaphoreType.DMA((n_arr * per,)),
                        pltpu.SemaphoreType.DMA((n_arr,))],
        name=name)(*arrays)
    return list(res)


def _sibling_swap(arrays, name):
    n_arr = len(arrays)
    n_chip = N_DEV // 2

    def body(*refs):
        srcs, outs = refs[:n_arr], refs[n_arr:2 * n_arr]
        send_sems, recv_sems = refs[2 * n_arr:]
        x, y, c = lax.axis_index("x"), lax.axis_index("y"), lax.axis_index("c")
        copies = []
        for a in range(n_arr):
            for j in range(n_chip):
                copies.append(pltpu.make_async_remote_copy(
                    src_ref=srcs[a].at[2 * j + 1 - c], dst_ref=outs[a].at[j],
                    send_sem=send_sems.at[a * n_chip + j], recv_sem=recv_sems.at[a * n_chip + j],
                    device_id=(x, y, 1 - c), device_id_type=pl.DeviceIdType.MESH))
        for cp in copies:
            cp.start()
        for cp in copies:
            cp.wait()

    anyspec = pl.BlockSpec(memory_space=pl.ANY)
    res = pl.pallas_call(
        body, in_specs=[anyspec] * n_arr, out_specs=[anyspec] * n_arr,
        out_shape=[jax.ShapeDtypeStruct((n_chip,) + a.shape[1:], a.dtype) for a in arrays],
        scratch_shapes=[pltpu.SemaphoreType.DMA((n_arr * n_chip,)), pltpu.SemaphoreType.DMA((n_arr * n_chip,))],
        name=name)(*arrays)
    return list(res)


def _pair_sum(send, stage, core, name):
    _, r, c = send.shape
    n_chip = stage.shape[0]
    tr = _tile(r, 128)

    def body(core_ref, a_ref, b_ref, o_ref):
        o_ref[...] = a_ref[...] + b_ref[...]

    return pl.pallas_call(
        body,
        grid_spec=pltpu.PrefetchScalarGridSpec(
            num_scalar_prefetch=1, grid=(n_chip, r // tr),
            in_specs=[pl.BlockSpec((1, tr, c), lambda j, i, core_ref: (2 * j + core_ref[0], i, 0)),
                      pl.BlockSpec((1, tr, c), lambda j, i, core_ref: (j, i, 0))],
            out_specs=pl.BlockSpec((1, tr, c), lambda j, i, core_ref: (j, i, 0))),
        out_shape=jax.ShapeDtypeStruct(stage.shape, F32), name=name,
        compiler_params=_cp(("parallel", "parallel")))(core, send, stage)


_HBM_SPEC = pl.BlockSpec(memory_space=pltpu.HBM)
_SEM_SPEC = pl.BlockSpec(memory_space=pltpu.SEMAPHORE)
_ANY_SPEC = pl.BlockSpec(memory_space=pl.ANY)
_DATAFLOW = pltpu.SideEffectType.DATAFLOW_SIDE_EFFECTING


def _exchange_start(arrays, scatter, name, dep=None, chips=False):
    n_arr = len(arrays)
    n_slot = N_DEV // 2 if chips else N_DEV
    n_sem = n_arr * (n_slot - 1)
    me = (2 * lax.axis_index("x") + lax.axis_index("y") if chips
          else 4 * lax.axis_index("x") + 2 * lax.axis_index("y") + lax.axis_index("c"))
    lands = []
    for a in arrays:
        own = lax.dynamic_index_in_dim(a, me, 0, keepdims=False) if scatter else a
        shape = a.shape if scatter else (n_slot,) + a.shape
        lands.append(lax.dynamic_update_index_in_dim(lax.empty(shape, a.dtype), own, me, 0))
    dep_specs, dep_args = _dep_specs(dep)

    def body(*refs):
        srcs, lnds = refs[:n_arr], refs[n_arr:2 * n_arr]
        outs = refs[2 * n_arr + len(dep_args):]
        send_sems, recv_sems, token = outs[0], outs[1], outs[2 + 2 * n_arr]
        for cp in _peer_copies(srcs, lnds, send_sems, recv_sems, scatter, chips):
            cp.start()
        token[...] = jnp.zeros_like(token)

    thru = [pltpu.HBM(a.shape, a.dtype) for a in list(arrays) + lands]
    return pl.pallas_call(
        body, name=name,
        out_shape=(pltpu.SemaphoreType.DMA((n_sem,)), pltpu.SemaphoreType.DMA((n_sem,)), *thru,
                   jax.ShapeDtypeStruct((SUBLANES, LANES), F32)),
        in_specs=[_HBM_SPEC] * (2 * n_arr) + dep_specs,
        out_specs=(_SEM_SPEC, _SEM_SPEC, *[_HBM_SPEC] * (2 * n_arr), pl.BlockSpec(memory_space=pltpu.VMEM)),
        input_output_aliases={i: 2 + i for i in range(2 * n_arr)},
        compiler_params=pltpu.CompilerParams(has_side_effects=_DATAFLOW),
    )(*[pltpu.with_memory_space_constraint(a, pltpu.HBM) for a in list(arrays) + lands], *dep_args)


def _exchange_wait(started, scatter, name, after, chips=False):
    send_sems, recv_sems = started[0], started[1]
    thru = list(started[2:-1])
    n_arr = len(thru) // 2

    def body(*refs):
        srcs, lnds = refs[:n_arr], refs[n_arr:2 * n_arr]
        for cp in _peer_copies(srcs, lnds, refs[2 * n_arr], refs[2 * n_arr + 1], scatter, chips):
            cp.wait_send()
            cp.wait_recv()

    res = pl.pallas_call(
        body, name=name, out_shape=tuple(pltpu.HBM(a.shape, a.dtype) for a in thru),
        in_specs=[_HBM_SPEC] * (2 * n_arr) + [_SEM_SPEC, _SEM_SPEC, _ANY_SPEC],
        out_specs=tuple([_HBM_SPEC] * (2 * n_arr)),
        input_output_aliases={i: i for i in range(2 * n_arr)},
        compiler_params=pltpu.CompilerParams(has_side_effects=_DATAFLOW),
    )(*thru, send_sems, recv_sems, after)
    return list(res[n_arr:])


def _unshard_cols(g, name):
    nd, r, s = g.shape
    tr = _tile(r, 64)

    def body(i_ref, o_ref):
        for p in range(nd):
            o_ref[:, p * s:(p + 1) * s] = i_ref[p]

    return pl.pallas_call(
        body, grid=(r // tr,), in_specs=[pl.BlockSpec((nd, tr, s), lambda i: (0, i, 0))],
        out_specs=pl.BlockSpec((tr, nd * s), lambda i: (i, 0)),
        out_shape=jax.ShapeDtypeStruct((r, nd * s), g.dtype), name=name, compiler_params=_cp(("parallel",)))(g)


def _shard_cols(g, name):
    r, n = g.shape
    s = n // N_DEV
    tr = _tile(r, 64)

    def body(i_ref, o_ref):
        for p in range(N_DEV):
            o_ref[p] = i_ref[:, p * s:(p + 1) * s]

    return pl.pallas_call(
        body, grid=(r // tr,), in_specs=[pl.BlockSpec((tr, n), lambda i: (i, 0))],
        out_specs=pl.BlockSpec((N_DEV, tr, s), lambda i: (0, i, 0)),
        out_shape=jax.ShapeDtypeStruct((N_DEV, r, s), g.dtype), name=name, compiler_params=_cp(("parallel",)))(g)


def _slot_sum(slots, name):
    nd, r, c = slots.shape
    tr = _tile(r, 64)

    def body(s_ref, o_ref):
        acc = s_ref[0]
        for p in range(1, nd):
            acc = acc + s_ref[p]
        o_ref[...] = acc

    return pl.pallas_call(
        body, grid=(r // tr,), in_specs=[pl.BlockSpec((nd, tr, c), lambda i: (0, i, 0))],
        out_specs=pl.BlockSpec((tr, c), lambda i: (i, 0)),
        out_shape=jax.ShapeDtypeStruct((r, c), F32), name=name, compiler_params=_cp(("parallel",)))(slots)


def _adamw(w, g, m, v, name):
    r, c = w.shape
    tr = _tile(r, 256)
    c1 = 1.0 - ADAM_B1 ** ADAM_STEP
    c2 = 1.0 - ADAM_B2 ** ADAM_STEP

    def body(w_ref, g_ref, m_ref, v_ref, d_ref, nm_ref, nv_ref):
        gv = g_ref[...]
        nm = ADAM_B1 * m_ref[...] + (1.0 - ADAM_B1) * gv
        nv = ADAM_B2 * v_ref[...] + (1.0 - ADAM_B2) * (gv * gv)
        nm_ref[...] = nm
        nv_ref[...] = nv
        d_ref[...] = -ADAM_LR * ((nm / c1) / (jnp.sqrt(nv / c2) + ADAM_EPS) + ADAM_WD * w_ref[...])

    spec = pl.BlockSpec((tr, c), lambda i: (i, 0))
    return pl.pallas_call(
        body, grid=(r // tr,), in_specs=[spec] * 4, out_specs=[spec] * 3,
        out_shape=[jax.ShapeDtypeStruct((r, c), F32)] * 3, name=name, compiler_params=_cp(("parallel",)))(w, g, m, v)


def _forward_backward(x, target, weights_hook, grads_hook, lb_param, hgrn_norm_g, attn_sinks, rel_bias, ln_g, ln_b):
    t, d = x.shape
    w = d // 2
    off, n_in = _offsets(d)
    n_heads = w // ATTN_HEAD_DIM
    grp = n_heads // ATTN_KV_HEADS

    lower = _lower_bounds(lb_param)
    bias = _bias_table(rel_bias, n_heads)
    bias_g = bias.reshape(ATTN_KV_HEADS, grp * WINDOW, 2 * WINDOW)

    saved, weights = [], []
    xb = x.astype(BF16)
    for l in range(DEPTH):
        wl, token = weights_hook(l, x)
        weights.append(wl)
        s = {"x": x, "xb": xb}
        u = _mm_nt_cols(xb, wl["w_in_t"], f"in_proj", dep=token)
        s["u"] = u
        lb_l, gain_l, cw_l = lower[l:l + 1], hgrn_norm_g[l:l + 1], wl["conv_w"]
        o_a, states, p_a = _hgrn_fwd(u, lb_l, gain_l, off, f"hgrn_fwd")
        o_b, p_b = _attn_fwd(u, bias_g, attn_sinks[l], off, w, f"attn_fwd")
        p_c = _conv_fwd(u, cw_l, off, w, f"conv_fwd")
        y_a = _mm_nn(p_a, wl["w_proj_hgrn"], f"proj_a", tn=2048)
        y_b = _mm_nn(p_b, wl["w_proj_attn"], f"proj_b", tn=2048)
        y_c = _mm_nn(p_c, wl["w_proj_conv"], f"proj_c", tn=2048)
        merged = _merge_fwd(u, y_a, y_b, y_c, off, d, f"merge_fwd")
        x, xb, xhat, rstd = _out_proj_ln(merged, wl["w_out"], x, ln_g[l:l + 1], ln_b[l:l + 1], f"out_proj_ln")
        s.update(o_a=o_a, states=states, p_a=p_a, o_b=o_b, p_b=p_b, p_c=p_c, y_a=y_a, y_b=y_b, y_c=y_c,
                 merged=merged, xhat=xhat, rstd=rstd)
        saved.append(s)

    loss_acc, dx = _loss_head(x, target)

    d_ln, d_lower, d_gain, d_sink, d_conv = [None] * DEPTH, [None] * DEPTH, [None] * DEPTH, [None] * DEPTH, [None] * DEPTH
    dbias_total = None
    for l in reversed(range(DEPTH)):
        wl, s = weights[l], saved[l]
        u = s["u"]
        lb_l, gain_l, cw_l = lower[l:l + 1], hgrn_norm_g[l:l + 1], wl["conv_w"]
        dz, dzb, d_ln[l], dmerged = _ln_bwd_out_proj(
            dx, s["xhat"], s["rstd"], ln_g[l:l + 1], wl["w_out"], f"ln_bwd_d_merged")
        g_out = _mm_tn(s["merged"], dzb, f"g_out", tn=2048)
        dya, dyb, dyc, dma, dmb, dmc = _merge_bwd(dmerged, u, s["y_a"], s["y_b"], s["y_c"], off, d, f"merge_bwd")
        g_pa = _mm_tn(s["p_a"], dya, f"g_proj_a", tn=2048)
        g_pb = _mm_tn(s["p_b"], dyb, f"g_proj_b", tn=2048)
        g_pc = _mm_tn(s["p_c"], dyc, f"g_proj_c", tn=2048)
        dpa = _mm_nt(dya, wl["w_proj_hgrn"], f"d_p_a", tk=2048)
        dpb = _mm_nt(dyb, wl["w_proj_attn"], f"d_p_b", tk=2048)
        dpc = _mm_nt(dyc, wl["w_proj_conv"], f"d_p_c", tk=2048)
        d_aq, d_af, d_ai, d_ag, acc_a = _hgrn_bwd(u, lb_l, gain_l, s["states"], s["o_a"], dpa, off, f"hgrn_bwd")
        d_lower[l], d_gain[l] = acc_a[0:1], acc_a[1:2]
        d_bq, dkc, dkp, dvc, dvp, dbias_l, d_sink[l], d_bg = _attn_bwd(
            u, s["o_b"], dpb, bias_g, attn_sinks[l], off, w, f"attn_bwd")
        d_bk = _kv_combine(dkc, dkp, f"k_combine")
        d_bv = _kv_combine(dvc, dvp, f"v_combine")
        dbias_total = dbias_l if dbias_total is None else dbias_total + dbias_l
        d_cb, d_cc, d_cx, d_cg, d_conv[l] = _conv_bwd(dpc, u, cw_l, off, w, f"conv_bwd")
        du = jnp.concatenate([d_aq, d_af, d_ai, d_ag, d_bq, d_bk, d_bv, d_bg, d_cb, d_cc, d_cx, d_cg, dma, dmb, dmc], axis=1)
        g_in_t = _mm_tn_rows(du, s["xb"], f"g_in")
        token = grads_hook(l, {"w_in_t": g_in_t, "w_proj_hgrn": g_pa, "w_proj_attn": g_pb, "w_proj_conv": g_pc, "w_out": g_out})
        dx = _mm_nn_acc(du, wl["w_in_t"], f"d_x", add=dz, add_scale=ALPHA, dep=token)

    d_lower_all = jnp.concatenate([a[0:1] for a in d_lower], axis=0)
    small = {
        "lb_param": _lower_bounds_bwd(lb_param, d_lower_all),
        "hgrn_norm_g": jnp.concatenate([a[0:1] for a in d_gain], axis=0),
        "attn_sinks": jnp.concatenate([a[0:1, :n_heads] for a in d_sink], axis=0),
        "conv_w": jnp.stack([a[0:3] for a in d_conv], axis=0),
        "rel_bias": _bias_grad(dbias_total.reshape(n_heads, WINDOW, 2 * WINDOW), n_heads)[:, :n_heads],
        "ln_g": jnp.concatenate([a[0:1] for a in d_ln], axis=0),
        "ln_b": jnp.concatenate([a[1:2] for a in d_ln], axis=0),
    }
    return loss_acc, dx, small


BIG = ("w_in", "w_proj_hgrn", "w_proj_attn", "w_proj_conv", "w_out")
SMALL = ("lb_param", "hgrn_norm_g", "attn_sinks", "conv_w", "rel_bias", "ln_g", "ln_b")
ORDER = ("w_in", "w_proj_hgrn", "w_proj_attn", "w_proj_conv", "w_out", "lb_param", "hgrn_norm_g", "attn_sinks",
         "conv_w", "rel_bias", "ln_g", "ln_b")


def _pack(parts):
    flat = jnp.concatenate([p.reshape(-1) for p in parts])
    n = flat.shape[0]
    unit = SUBLANES * LANES
    total = -(-n // unit) * unit
    return jnp.pad(flat, (0, total - n)).reshape(total // LANES, LANES)


def _unpack(packed, shapes):
    flat = packed.reshape(-1)
    out, o = [], 0
    for shp in shapes:
        n = int(np.prod(shp))
        out.append(flat[o:o + n].reshape(shp))
        o += n
    return out


def kernel(x, w_in, w_proj_hgrn, w_proj_attn, w_proj_conv, w_out, lb_param, hgrn_norm_g, attn_sinks, conv_w, rel_bias, ln_g, ln_b, loss_target, m_w_in, m_w_proj_hgrn, m_w_proj_attn, m_w_proj_conv, m_w_out, m_lb_param, m_hgrn_norm_g, m_attn_sinks, m_conv_w, m_rel_bias, m_ln_g, m_ln_b, v_w_in, v_w_proj_hgrn, v_w_proj_attn, v_w_proj_conv, v_w_out, v_lb_param, v_hgrn_norm_g, v_attn_sinks, v_conv_w, v_rel_bias, v_ln_g, v_ln_b):
    params = dict(w_in=w_in, w_proj_hgrn=w_proj_hgrn, w_proj_attn=w_proj_attn, w_proj_conv=w_proj_conv, w_out=w_out,
                  lb_param=lb_param, hgrn_norm_g=hgrn_norm_g, attn_sinks=attn_sinks, conv_w=conv_w, rel_bias=rel_bias,
                  ln_g=ln_g, ln_b=ln_b)
    mom_m = dict(w_in=m_w_in, w_proj_hgrn=m_w_proj_hgrn, w_proj_attn=m_w_proj_attn, w_proj_conv=m_w_proj_conv,
                 w_out=m_w_out, lb_param=m_lb_param, hgrn_norm_g=m_hgrn_norm_g, attn_sinks=m_attn_sinks,
                 conv_w=m_conv_w, rel_bias=m_rel_bias, ln_g=m_ln_g, ln_b=m_ln_b)
    mom_v = dict(w_in=v_w_in, w_proj_hgrn=v_w_proj_hgrn, w_proj_attn=v_w_proj_attn, w_proj_conv=v_w_proj_conv,
                 w_out=v_w_out, lb_param=v_lb_param, hgrn_norm_g=v_hgrn_norm_g, attn_sinks=v_attn_sinks,
                 conv_w=v_conv_w, rel_bias=v_rel_bias, ln_g=v_ln_g, ln_b=v_ln_b)
    d = x.shape[-1]
    me = 4 * lax.axis_index("x") + 2 * lax.axis_index("y") + lax.axis_index("c")
    for group in (params, mom_m, mom_v):
        group["w_in"] = jnp.swapaxes(group["w_in"], 1, 2)

    def shards_of(l):
        return [params[n][l].astype(BF16) for n in BIG] + [conv_w[l]]

    gathers = {}

    def weights_hook(l, x_in):
        if l == 0:
            got = _gather_two_level(shards_of(0), "gather_weights_0")
        else:
            got = _exchange_wait(gathers.pop(l), False, f"gather_wait_{l}", x_in)
        token = None
        if l + 1 < DEPTH:
            gathers[l + 1] = _exchange_start(shards_of(l + 1), False, f"gather_start_{l + 1}", dep=got[0])
            token = gathers[l + 1][-1]
        wl = {
            "w_in_t": got[0].reshape(-1, d),
            "w_proj_hgrn": _unshard_cols(got[1], "unshard_w_proj_hgrn"),
            "w_proj_attn": _unshard_cols(got[2], "unshard_w_proj_attn"),
            "w_proj_conv": _unshard_cols(got[3], "unshard_w_proj_conv"),
            "w_out": got[4].reshape(d, d),
            "conv_w": _unshard_cols(got[5], "unshard_conv_w"),
        }
        return wl, token

    grads = {n: [None] * DEPTH for n in BIG}
    scatters = {}

    def finish_scatter(l, after):
        got = _exchange_wait(scatters.pop(l), True, f"scatter_wait_{l}", after, chips=(l == 0))
        for n, slots in zip(BIG, got):
            grads[n][l] = _slot_sum(slots, f"sum_{n}_chips" if l == 0 else f"sum_{n}")
        return got[0]

    def grads_hook(l, g):
        send = [g["w_in_t"].reshape(N_DEV, -1, d), _shard_cols(g["w_proj_hgrn"], "shard_g_proj_a"),
                _shard_cols(g["w_proj_attn"], "shard_g_proj_b"), _shard_cols(g["w_proj_conv"], "shard_g_proj_c"),
                g["w_out"].reshape(N_DEV, d // N_DEV, d)]
        dep = finish_scatter(l + 1, send[0]) if l + 1 < DEPTH else None
        if l == 0:
            core = lax.axis_index("c").astype(jnp.int32).reshape(1)
            staged = _sibling_swap(send, "pair_swap_grads")
            send = [_pair_sum(s, st, core, f"pair_sum_{n}") for n, s, st in zip(BIG, send, staged)]
        scatters[l] = _exchange_start(send, True, f"scatter_start_{l}", dep=dep, chips=(l == 0))
        return scatters[l][-1]

    loss_acc, dx, small = _forward_backward(
        x[0], loss_target[0], weights_hook, grads_hook, lb_param, hgrn_norm_g, attn_sinks, rel_bias, ln_g, ln_b)
    loss = lax.psum(0.5 * jnp.sum(loss_acc[0]) / d, ("x", "y", "c"))
    finish_scatter(0, dx)
    for n in BIG:
        grads[n] = jnp.stack(grads[n], axis=0)

    small_shapes = [small[n].shape for n in SMALL]
    packed = _pack([small[n] for n in SMALL])
    got = _exchange([packed], False, "gather_small_grads")[0]
    summed = _unpack(_slot_sum(got, "sum_small_grads"), small_shapes)
    for n, g in zip(SMALL, summed):
        grads[n] = g
    cs = conv_w.shape[-1]
    grads["conv_w"] = lax.dynamic_slice_in_dim(grads["conv_w"], me * cs, cs, axis=2)

    delta, new_m, new_v = {}, {}, {}
    for n in BIG:
        shp = params[n].shape
        flat = lambda a: a.reshape(-1, shp[-1])
        dl, nm, nv = _adamw(flat(params[n]), flat(grads[n]), flat(mom_m[n]), flat(mom_v[n]), f"adamw_{n}")
        delta[n], new_m[n], new_v[n] = dl.reshape(shp), nm.reshape(shp), nv.reshape(shp)
    shapes = [params[n].shape for n in SMALL]
    res = _adamw(_pack([params[n] for n in SMALL]), _pack([grads[n] for n in SMALL]),
                 _pack([mom_m[n] for n in SMALL]), _pack([mom_v[n] for n in SMALL]), "adamw_small")
    for dst, packed_res in zip((delta, new_m, new_v), res):
        for n, a in zip(SMALL, _unpack(packed_res, shapes)):
            dst[n] = a

    for group in (grads, delta, new_m, new_v):
        group["w_in"] = jnp.swapaxes(group["w_in"], 1, 2)
    return (loss, dx[None], *[grads[n] for n in ORDER], *[delta[n] for n in ORDER],
            *[new_m[n] for n in ORDER], *[new_v[n] for n in ORDER])
```

```python
import functools
import math

import numpy as np
import jax
import jax.numpy as jnp
from jax import lax
from jax.experimental import pallas as pl
from jax.experimental.pallas import tpu as pltpu

F32 = jnp.float32
BF16 = jnp.bfloat16

N_DEV = 8
DEPTH = 4
HGRN_HEAD_DIM = 128
HGRN_CHUNK = 64
ATTN_HEAD_DIM = 64
ATTN_KV_HEADS = 4
KV_WIDTH = ATTN_KV_HEADS * ATTN_HEAD_DIM
WINDOW = 128
WINDOW_SHIFT = 7
N_BUCKETS = 32
MAX_DISTANCE = 128
ALPHA = (2.0 * DEPTH) ** 0.25
LN_EPS = 1e-5
RMS_EPS = 1e-6
ADAM_LR = 0.001
ADAM_B1 = 0.9
ADAM_B2 = 0.999
ADAM_EPS = 1e-08
ADAM_WD = 0.01
ADAM_STEP = 10

LANES = 128
SUBLANES = 8
VMEM_LIMIT = 56 << 20
NEG_INF = float("-inf")


def _offsets(d_model):
    w = d_model // 2
    sizes = (w, w, w, w, w, KV_WIDTH, KV_WIDTH, w, w, w, w, w, d_model, d_model, d_model)
    names = ("a_q", "a_f", "a_i", "a_g", "b_q", "b_k", "b_v", "b_g", "c_b", "c_c", "c_x", "c_g", "m_a", "m_b", "m_c")
    off, o = {}, 0
    for n, s in zip(names, sizes):
        off[n] = o
        o += s
    return off, o


def _tile(n, pref):
    t = min(pref, n)
    while n % t:
        t //= 2
    return t


def _cp(sem=None, vmem=VMEM_LIMIT):
    return pltpu.CompilerParams(dimension_semantics=sem, vmem_limit_bytes=vmem)


def _sigmoid(x):
    return 1.0 / (1.0 + jnp.exp(-x))


def _dot_nn(a, b):
    return jnp.dot(a, b, preferred_element_type=F32)


def _dot_nt(a, b):
    return lax.dot_general(a, b, (((1,), (1,)), ((), ())), preferred_element_type=F32)


def _dot_tn(a, b):
    return lax.dot_general(a, b, (((0,), (0,)), ((), ())), preferred_element_type=F32)


def _dep_specs(dep):
    return ([], []) if dep is None else ([pl.BlockSpec(memory_space=pl.ANY)], [dep])


def _mm_nn(a, b, name, out_dtype=F32, tm=1024, tn=1536, dep=None):
    m, k = a.shape
    _, n = b.shape
    tm, tn = _tile(m, tm), _tile(n, tn)
    dep_specs, dep_args = _dep_specs(dep)

    def body(a_ref, b_ref, *rest):
        o_ref = rest[-1]
        o_ref[...] = _dot_nn(a_ref[...], b_ref[...]).astype(o_ref.dtype)

    return pl.pallas_call(
        body, grid=(n // tn, m // tm),
        in_specs=[pl.BlockSpec((tm, k), lambda j, i: (i, 0)), pl.BlockSpec((k, tn), lambda j, i: (0, j))] + dep_specs,
        out_specs=pl.BlockSpec((tm, tn), lambda j, i: (i, j)),
        out_shape=jax.ShapeDtypeStruct((m, n), out_dtype), name=name,
        compiler_params=_cp(("parallel", "parallel")))(a, b, *dep_args)


def _mm_nt(a, b, name, tm=1024, tk=1536, add=None, add_scale=1.0, dep=None):
    m, k = a.shape
    n, _ = b.shape
    tm, tk = _tile(m, tm), _tile(k, tk)
    has_add = add is not None
    dep_specs, dep_args = _dep_specs(dep)

    def body(*refs):
        if has_add:
            a_ref, b_ref, add_ref = refs[:3]
        else:
            a_ref, b_ref = refs[:2]
        o_ref = refs[-1]
        if k == tk:
            prod = _dot_nt(a_ref[...], b_ref[...])
            o_ref[...] = prod + add_ref[...] * add_scale if has_add else prod
            return

        @pl.when(pl.program_id(1) == 0)
        def _():
            if has_add:
                o_ref[...] = add_ref[...] * add_scale
            else:
                o_ref[...] = jnp.zeros_like(o_ref)

        o_ref[...] += _dot_nt(a_ref[...], b_ref[...])

    in_specs = [pl.BlockSpec((tm, tk), lambda i, kk: (i, kk)), pl.BlockSpec((n, tk), lambda i, kk: (0, kk))]
    args = [a, b]
    if has_add:
        in_specs.append(pl.BlockSpec((tm, n), lambda i, kk: (i, 0)))
        args.append(add)
    in_specs += dep_specs
    args += dep_args
    return pl.pallas_call(
        body, grid=(m // tm, k // tk), in_specs=in_specs,
        out_specs=pl.BlockSpec((tm, n), lambda i, kk: (i, 0)),
        out_shape=jax.ShapeDtypeStruct((m, n), F32), name=name,
        compiler_params=_cp(("parallel", "arbitrary")))(*args)


def _mm_nt_cols(a, b, name, tm=1024, tn=1536, dep=None):
    m, k = a.shape
    n, _ = b.shape
    tm, tn = _tile(m, tm), _tile(n, tn)
    dep_specs, dep_args = _dep_specs(dep)

    def body(a_ref, b_ref, *rest):
        rest[-1][...] = _dot_nt(a_ref[...], b_ref[...])

    return pl.pallas_call(
        body, grid=(n // tn, m // tm),
        in_specs=[pl.BlockSpec((tm, k), lambda j, i: (i, 0)), pl.BlockSpec((tn, k), lambda j, i: (j, 0))] + dep_specs,
        out_specs=pl.BlockSpec((tm, tn), lambda j, i: (i, j)),
        out_shape=jax.ShapeDtypeStruct((m, n), F32), name=name,
        compiler_params=_cp(("parallel", "parallel")))(a, b, *dep_args)


def _mm_nn_acc(a, b, name, tm=1024, tk=1536, add=None, add_scale=1.0, dep=None):
    m, k = a.shape
    _, n = b.shape
    tm, tk = _tile(m, tm), _tile(k, tk)
    dep_specs, dep_args = _dep_specs(dep)

    def body(a_ref, b_ref, add_ref, *rest):
        o_ref = rest[-1]

        @pl.when(pl.program_id(1) == 0)
        def _():
            o_ref[...] = add_ref[...] * add_scale

        o_ref[...] += _dot_nn(a_ref[...], b_ref[...])

    return pl.pallas_call(
        body, grid=(m // tm, k // tk),
        in_specs=[pl.BlockSpec((tm, tk), lambda i, kk: (i, kk)), pl.BlockSpec((tk, n), lambda i, kk: (kk, 0)),
                  pl.BlockSpec((tm, n), lambda i, kk: (i, 0))] + dep_specs,
        out_specs=pl.BlockSpec((tm, n), lambda i, kk: (i, 0)),
        out_shape=jax.ShapeDtypeStruct((m, n), F32), name=name,
        compiler_params=_cp(("parallel", "arbitrary")))(a, b, add, *dep_args)


def _mm_tn_rows(a, b, name, tt=1024, tr=1536):
    t, k = a.shape
    _, n = b.shape
    tt, tr = _tile(t, tt), _tile(k, tr)

    def body(a_ref, b_ref, o_ref):
        @pl.when(pl.program_id(1) == 0)
        def _():
            o_ref[...] = jnp.zeros_like(o_ref)

        o_ref[...] += _dot_tn(a_ref[...], b_ref[...])

    return pl.pallas_call(
        body, grid=(k // tr, t // tt),
        in_specs=[pl.BlockSpec((tt, tr), lambda j, s: (s, j)), pl.BlockSpec((tt, n), lambda j, s: (s, 0))],
        out_specs=pl.BlockSpec((tr, n), lambda j, s: (j, 0)),
        out_shape=jax.ShapeDtypeStruct((k, n), F32), name=name,
        compiler_params=_cp(("parallel", "arbitrary")))(a, b)


def _mm_tn(a, b, name, tt=1024, tn=1536):
    t, k = a.shape
    _, n = b.shape
    tt, tn = _tile(t, tt), _tile(n, tn)

    def body(a_ref, b_ref, o_ref):
        @pl.when(pl.program_id(1) == 0)
        def _():
            o_ref[...] = jnp.zeros_like(o_ref)

        o_ref[...] += _dot_tn(a_ref[...], b_ref[...])

    return pl.pallas_call(
        body, grid=(n // tn, t // tt),
        in_specs=[pl.BlockSpec((tt, k), lambda j, s: (s, 0)), pl.BlockSpec((tt, tn), lambda j, s: (s, j))],
        out_specs=pl.BlockSpec((k, tn), lambda j, s: (0, j)),
        out_shape=jax.ShapeDtypeStruct((k, n), F32), name=name,
        compiler_params=_cp(("parallel", "arbitrary")))(a, b)


def _ew(body, name, t, ncol, wb, ins, outs, accs=(), tt=512):
    tt = _tile(t, tt)
    nt = t // tt
    in_specs, args = [], []
    for arr, kind, coff in ins:
        if kind == "tile":
            spec = pl.BlockSpec((tt, wb), lambda j, i, c=coff: (i, c + j))
        elif kind == "prev":
            spec = pl.BlockSpec((tt, wb), lambda j, i, c=coff: (jnp.maximum(i - 1, 0), c + j))
        elif kind == "next":
            spec = pl.BlockSpec((tt, wb), lambda j, i, c=coff: (jnp.minimum(i + 1, nt - 1), c + j))
        else:
            spec = pl.BlockSpec((arr.shape[0], wb), lambda j, i, c=coff: (0, c + j))
        in_specs.append(spec)
        args.append(arr)
    out_specs = [pl.BlockSpec((tt, wb), lambda j, i: (i, j)) for _ in outs]
    out_shape = [jax.ShapeDtypeStruct((t, ncol * wb), d) for d in outs]
    for r in accs:
        out_specs.append(pl.BlockSpec((r, wb), lambda j, i: (0, j)))
        out_shape.append(jax.ShapeDtypeStruct((r, ncol * wb), F32))

    def kern(*refs):
        body(pl.program_id(1), nt, *refs)

    res = pl.pallas_call(
        kern, grid=(ncol, nt), in_specs=in_specs, out_specs=out_specs, out_shape=out_shape, name=name,
        compiler_params=_cp(("parallel", "arbitrary")))(*args)
    return res


def _silu_parts(x):
    s = _sigmoid(x)
    return x * s, s + x * s * (1.0 - s)


def _out_proj_ln(a, w, x, g, b, name):
    t, d = x.shape
    k = a.shape[1]
    tt = _tile(t, 256)

    def body(a_ref, w_ref, x_ref, g_ref, b_ref, o_ref, ob_ref, xh_ref, r_ref):
        z = ALPHA * x_ref[...] + _dot_nn(a_ref[...], w_ref[...])
        mu = jnp.mean(z, axis=1, keepdims=True)
        zc = z - mu
        var = jnp.mean(zc * zc, axis=1, keepdims=True)
        rstd = lax.rsqrt(var + LN_EPS)
        xh = zc * rstd
        o = xh * g_ref[...] + b_ref[...]
        o_ref[...] = o
        ob_ref[...] = o.astype(BF16)
        xh_ref[...] = xh
        r_ref[...] = rstd

    row = pl.BlockSpec((tt, d), lambda i: (i, 0))
    vec = pl.BlockSpec((1, d), lambda i: (0, 0))
    return pl.pallas_call(
        body, grid=(t // tt,),
        in_specs=[pl.BlockSpec((tt, k), lambda i: (i, 0)), pl.BlockSpec((k, d), lambda i: (0, 0)), row, vec, vec],
        out_specs=[row, row, row, pl.BlockSpec((tt, 1), lambda i: (i, 0))],
        out_shape=[jax.ShapeDtypeStruct((t, d), F32), jax.ShapeDtypeStruct((t, d), BF16),
                   jax.ShapeDtypeStruct((t, d), F32), jax.ShapeDtypeStruct((t, 1), F32)],
        name=name, compiler_params=_cp(("parallel",)))(a, w, x, g, b)


def _ln_bwd_out_proj_merge(dout, xhat, rstd, g, w, u, ya, yb, yc, off, name):
    t, d = dout.shape
    gb = GATE_BLOCK
    nb = d // gb
    tt = _tile(t, 128)
    cols = [off[nme] // gb for nme in ("m_a", "m_b", "m_c")]

    def body(do_ref, xh_ref, r_ref, g_ref, w_ref, ya_ref, yb_ref, yc_ref, *rest):
        m_refs = rest[:3 * nb]
        dz_ref, dzb_ref, acc_ref = rest[3 * nb:3 * nb + 3]
        dy_refs, dg_refs = rest[3 * nb + 3:3 * nb + 6], rest[3 * nb + 6:]

        @pl.when(pl.program_id(0) == 0)
        def _():
            acc_ref[...] = jnp.zeros_like(acc_ref)

        do = do_ref[...]
        xh = xh_ref[...]
        dxh = do * g_ref[...]
        m1 = jnp.mean(dxh, axis=1, keepdims=True)
        m2 = jnp.mean(dxh * xh, axis=1, keepdims=True)
        dz = r_ref[...] * (dxh - m1 - xh * m2)
        dzb = dz.astype(BF16)
        dz_ref[...] = dz
        dzb_ref[...] = dzb
        acc_ref[0:1, :] += jnp.sum(do * xh, axis=0, keepdims=True)
        acc_ref[1:2, :] += jnp.sum(do, axis=0, keepdims=True)
        dm = _dot_nt(dzb, w_ref[...])
        for gate, y_ref in enumerate((ya_ref, yb_ref, yc_ref)):
            for j in range(nb):
                sl = slice(j * gb, (j + 1) * gb)
                s = _sigmoid(m_refs[gate * nb + j][...])
                dm_j = dm[:, sl]
                dy_refs[gate][:, sl] = (dm_j * s).astype(BF16)
                dg_refs[gate][:, sl] = (dm_j * y_ref[:, sl] * s * (1.0 - s)).astype(BF16)

    row = pl.BlockSpec((tt, d), lambda i: (i, 0))
    gates = [pl.BlockSpec((tt, gb), lambda i, c=c0 + j: (i, c)) for c0 in cols for j in range(nb)]
    return pl.pallas_call(
        body, grid=(t // tt,),
        in_specs=[row, row, pl.BlockSpec((tt, 1), lambda i: (i, 0)), pl.BlockSpec((1, d), lambda i: (0, 0)),
                  pl.BlockSpec((d, d), lambda i: (0, 0)), row, row, row] + gates,
        out_specs=[row, row, pl.BlockSpec((SUBLANES, d), lambda i: (0, 0))] + [row] * 6,
        out_shape=[jax.ShapeDtypeStruct((t, d), F32), jax.ShapeDtypeStruct((t, d), BF16),
                   jax.ShapeDtypeStruct((SUBLANES, d), F32)] + [jax.ShapeDtypeStruct((t, d), BF16)] * 6,
        name=name, compiler_params=_cp(("arbitrary",)))(dout, xhat, rstd, g, w, ya, yb, yc, *([u] * (3 * nb)))


def _loss_head(y, target):
    t, d = y.shape
    tt = _tile(t, 256)

    def body(y_ref, t_ref, acc_ref, dy_ref):
        @pl.when(pl.program_id(0) == 0)
        def _():
            acc_ref[...] = jnp.zeros_like(acc_ref)

        err = y_ref[...] - t_ref[...]
        dy_ref[...] = err * (1.0 / d)
        acc_ref[0:1, :] += jnp.sum(err * err, axis=0, keepdims=True)

    row = pl.BlockSpec((tt, d), lambda i: (i, 0))
    acc, dy = pl.pallas_call(
        body, grid=(t // tt,), in_specs=[row, row],
        out_specs=[pl.BlockSpec((SUBLANES, d), lambda i: (0, 0)), row],
        out_shape=[jax.ShapeDtypeStruct((SUBLANES, d), F32), jax.ShapeDtypeStruct((t, d), F32)],
        name="loss_head", compiler_params=_cp(("arbitrary",)))(y, target)
    return acc, dy


def _tri(lower):
    r = lax.broadcasted_iota(jnp.int32, (HGRN_CHUNK, HGRN_CHUNK), 0)
    c = lax.broadcasted_iota(jnp.int32, (HGRN_CHUNK, HGRN_CHUNK), 1)
    return jnp.where((r >= c) if lower else (r <= c), 1.0, 0.0).astype(BF16)


def _exact_tri_matmul(tri, x):
    hi = x.astype(BF16)
    r1 = x - hi.astype(F32)
    mid = r1.astype(BF16)
    lo = (r1 - mid.astype(F32)).astype(BF16)
    return _dot_nn(tri, hi) + _dot_nn(tri, mid) + _dot_nn(tri, lo)


def _hgrn_gates(q_raw, fl, lb):
    sq = _sigmoid(q_raw)
    qf = q_raw * sq * (HGRN_HEAD_DIM ** -0.5)
    sg = _sigmoid(fl)
    f = lb + (1.0 - lb) * sg
    return qf, sq, sg, f


HGRN_SUB = 16
HGRN_NSUB = HGRN_CHUNK // HGRN_SUB
HGRN_HEADS_PER_STEP = 8
HGRN_HEADS_PER_STEP_BWD = 8


def _diag_rows(r):
    return (r // SUBLANES) * SUBLANES


def _heads(x):
    hd = HGRN_HEAD_DIM
    return [x[:, i * hd:(i + 1) * hd] for i in range(x.shape[1] // hd)]


def _per_head(fn, *xs):
    split = [x if isinstance(x, (list, tuple)) else _heads(x) for x in xs]
    return jnp.concatenate([fn(*hs) for hs in zip(*split)], axis=1)


def _head_lane_sum(x):
    return _per_head(lambda h: jnp.broadcast_to(jnp.sum(h, axis=1, keepdims=True), h.shape), x)


def _hgrn_intra_fwd(qf, k, v, b):
    ch, sub, wd = HGRN_CHUNK, HGRN_SUB, qf.shape[1]
    tl = lax.broadcasted_iota(jnp.int32, (sub, wd), 0)
    blocks = []
    for m in range(HGRN_NSUB):
        rs = slice(m * sub, (m + 1) * sub)
        bm, qm, km, vm = b[rs], qf[rs], k[rs], v[rs]
        parts = {0: jnp.zeros((sub, wd), F32), SUBLANES: jnp.zeros((sub - SUBLANES, wd), F32)}
        for r in range(sub):
            lo = _diag_rows(r)
            e = jnp.exp(jnp.where(tl[lo:] >= r, bm[lo:] - bm[r:r + 1], NEG_INF))
            parts[lo] = parts[lo] + _head_lane_sum(qm[lo:] * e * km[r:r + 1]) * vm[r:r + 1]
        blocks.append(parts[0] + jnp.concatenate([jnp.zeros((SUBLANES, wd), F32), parts[SUBLANES]], axis=0))
    acc = jnp.concatenate(blocks, axis=0)
    for j in range(HGRN_NSUB - 1):
        lo = sub * (j + 1)
        c = b[lo - 1:lo, :]
        qj = (qf[lo:] * jnp.exp(b[lo:] - c)).astype(BF16)
        kj = (k[lo - sub:lo] * jnp.exp(c - b[lo - sub:lo])).astype(BF16)
        vj = v[lo - sub:lo].astype(BF16)
        contrib = _per_head(lambda q_, k_, v_: _dot_nn(_dot_nt(q_, k_).astype(BF16), v_), qj, kj, vj)
        acc = acc + jnp.concatenate([jnp.zeros((lo, wd), F32), contrib], axis=0)
    return acc


def _hgrn_intra_bwd(qf, k, v, b, do_v):
    ch, sub, wd = HGRN_CHUNK, HGRN_SUB, qf.shape[1]
    tl = lax.broadcasted_iota(jnp.int32, (sub, wd), 0)
    dq_blocks, dk_blocks, dv_blocks = [], [], []
    for m in range(HGRN_NSUB):
        rs = slice(m * sub, (m + 1) * sub)
        bm, qm, km, vm, dom = b[rs], qf[rs], k[rs], v[rs], do_v[rs]
        parts = {0: jnp.zeros((sub, wd), F32), SUBLANES: jnp.zeros((sub - SUBLANES, wd), F32)}
        dk_parts = {sub: jnp.zeros((sub, wd), F32), SUBLANES: jnp.zeros((SUBLANES, wd), F32)}
        dv_parts = {sub: jnp.zeros((sub, wd), F32), SUBLANES: jnp.zeros((SUBLANES, wd), F32)}
        for r in range(sub):
            lo = _diag_rows(r)
            b_r, k_r, v_r, q_r, do_r = bm[r:r + 1], km[r:r + 1], vm[r:r + 1], qm[r:r + 1], dom[r:r + 1]
            e = jnp.exp(jnp.where(tl[lo:] >= r, bm[lo:] - b_r, NEG_INF))
            parts[lo] = parts[lo] + _head_lane_sum(dom[lo:] * v_r) * (k_r * e)
            hi = lo + SUBLANES
            e2 = jnp.exp(jnp.where(tl[:hi] <= r, b_r - bm[:hi], NEG_INF))
            qe2 = q_r * e2
            dk_parts[hi] = dk_parts[hi] + _head_lane_sum(vm[:hi] * do_r) * qe2
            dv_parts[hi] = dv_parts[hi] + _head_lane_sum(km[:hi] * qe2) * do_r
        pad = jnp.zeros((SUBLANES, wd), F32)
        dq_blocks.append(parts[0] + jnp.concatenate([pad, parts[SUBLANES]], axis=0))
        dk_blocks.append(dk_parts[sub] + jnp.concatenate([dk_parts[SUBLANES], pad], axis=0))
        dv_blocks.append(dv_parts[sub] + jnp.concatenate([dv_parts[SUBLANES], pad], axis=0))
    dq = jnp.concatenate(dq_blocks, axis=0)
    dk = jnp.concatenate(dk_blocks, axis=0)
    dv = jnp.concatenate(dv_blocks, axis=0)
    do_b, v_b = do_v.astype(BF16), v.astype(BF16)
    dk_off, dv_off = [], []
    for j in range(HGRN_NSUB - 1):
        lo = sub * (j + 1)
        c = b[lo - 1:lo, :]
        eq = jnp.exp(b[lo:] - c)
        ek = jnp.exp(c - b[lo - sub:lo])
        qj = (qf[lo:] * eq).astype(BF16)
        kj = (k[lo - sub:lo] * ek).astype(BF16)
        doj, vj = do_b[lo:], v_b[lo - sub:lo]
        dq_j = _per_head(lambda do_, v_, k_: _dot_nn(_dot_nt(do_, v_).astype(BF16), k_), doj, vj, kj)
        dk_j = _per_head(lambda do_, v_, q_: _dot_nn(_dot_nt(v_, do_).astype(BF16), q_), doj, vj, qj)
        dv_j = _per_head(lambda do_, k_, q_: _dot_nn(_dot_nt(k_, q_).astype(BF16), do_), doj, kj, qj)
        dq = dq + jnp.concatenate([jnp.zeros((lo, wd), F32), dq_j * eq], axis=0)
        dk_off.append(dk_j * ek)
        dv_off.append(dv_j)
    zero = jnp.zeros((sub, wd), F32)
    dk = dk + jnp.concatenate(dk_off + [zero], axis=0)
    dv = dv + jnp.concatenate(dv_off + [zero], axis=0)
    return dq, dk, dv


def _head_rms(o):
    return lax.rsqrt(_head_lane_sum(o * o) * (1.0 / HGRN_HEAD_DIM) + RMS_EPS)


def _hgrn_fwd(u, lb, gain, off, name):
    t = u.shape[0]
    w = lb.shape[1]
    hd, ch, hp = HGRN_HEAD_DIM, HGRN_CHUNK, HGRN_HEADS_PER_STEP
    nh, nc = w // hd, t // ch
    wb = hp * hd
    cq, cf, cv, cg = off["a_q"] // wb, off["a_f"] // wb, off["a_i"] // wb, off["a_g"] // wb

    def body(q_ref, f_ref, v_ref, g_ref, lb_ref, gain_ref, o_ref, st_ref, p_ref, state):
        @pl.when(pl.program_id(1) == 0)
        def _():
            state[...] = jnp.zeros_like(state)

        sts = [state[i] for i in range(hp)]
        qf, _, _, f = _hgrn_gates(q_ref[...], f_ref[...], lb_ref[...])
        k = 1.0 - f
        v = v_ref[...]
        b = _exact_tri_matmul(_tri(True), jnp.log(f))
        inter = _per_head(lambda qa_, st_: _dot_nt(qa_, st_.astype(BF16)), (qf * jnp.exp(b)).astype(BF16), sts)
        o = inter + _hgrn_intra_fwd(qf, k, v, b)
        o_ref[...] = o
        silu, _ = _silu_parts(g_ref[...])
        p_ref[...] = (o * _head_rms(o) * gain_ref[...] * silu).astype(BF16)
        b_end = b[ch - 1:ch, :]
        a_end = _heads(jnp.exp(b_end))
        kd = _heads((k * jnp.exp(b_end - b)).astype(BF16))
        v_b = _heads(v.astype(BF16))
        for i in range(hp):
            st_ref[i, 0] = sts[i]
            state[i] = sts[i] * a_end[i] + _dot_tn(v_b[i], kd[i])

    return pl.pallas_call(
        body, grid=(nh // hp, nc),
        in_specs=[pl.BlockSpec((ch, wb), lambda h, n: (n, cq + h)),
                  pl.BlockSpec((ch, wb), lambda h, n: (n, cf + h)),
                  pl.BlockSpec((ch, wb), lambda h, n: (n, cv + h)),
                  pl.BlockSpec((ch, wb), lambda h, n: (n, cg + h)),
                  pl.BlockSpec((1, wb), lambda h, n: (0, h)),
                  pl.BlockSpec((1, wb), lambda h, n: (0, h))],
        out_specs=[pl.BlockSpec((ch, wb), lambda h, n: (n, h)),
                   pl.BlockSpec((hp, 1, hd, hd), lambda h, n: (h, n, 0, 0)),
                   pl.BlockSpec((ch, wb), lambda h, n: (n, h))],
        out_shape=[jax.ShapeDtypeStruct((t, w), F32), jax.ShapeDtypeStruct((nh, nc, hd, hd), F32),
                   jax.ShapeDtypeStruct((t, w), BF16)],
        scratch_shapes=[pltpu.VMEM((hp, hd, hd), F32)],
        name=name, compiler_params=_cp(("parallel", "arbitrary")))(u, u, u, u, lb, gain)


def _hgrn_bwd(u, lb, gain, states, o, dp, off, name):
    t = u.shape[0]
    w = lb.shape[1]
    hd, ch, hp = HGRN_HEAD_DIM, HGRN_CHUNK, HGRN_HEADS_PER_STEP_BWD
    nh, nc = w // hd, t // ch
    wb = hp * hd
    cq, cf, cv, cg = off["a_q"] // wb, off["a_f"] // wb, off["a_i"] // wb, off["a_g"] // wb

    def body(q_ref, f_ref, v_ref, g_ref, o_ref, dp_ref, st_ref, lb_ref, gain_ref,
             dq_ref, df_ref, dv_ref, dg_ref, dlb_ref, dstate):
        @pl.when(pl.program_id(1) == 0)
        def _():
            dstate[...] = jnp.zeros_like(dstate)
            dlb_ref[...] = jnp.zeros_like(dlb_ref)

        silu, dsilu = _silu_parts(g_ref[...])
        o_v, dp_v, gain_row = o_ref[...], dp_ref[...], gain_ref[...]
        rms = _head_rms(o_v)
        nrm = o_v * rms
        dg_ref[...] = (dp_v * nrm * gain_row * dsilu).astype(BF16)
        dlb_ref[1:2, :] += jnp.sum(dp_v * nrm * silu, axis=0, keepdims=True)
        dn = dp_v * gain_row * silu
        do_v = rms * (dn - nrm * (_head_lane_sum(dn * nrm) * (1.0 / HGRN_HEAD_DIM)))

        rows = lax.broadcasted_iota(jnp.int32, (ch, wb), 0)
        lb_row = lb_ref[...]
        q_raw = q_ref[...]
        qf, sq, sg, f = _hgrn_gates(q_raw, f_ref[...], lb_row)
        k = 1.0 - f
        b = _exact_tri_matmul(_tri(True), jnp.log(f))
        a = jnp.exp(b)
        b_end = b[ch - 1:ch, :]
        a_end = jnp.exp(b_end)
        to_end = jnp.exp(b_end - b)
        v = v_ref[...]
        st0 = [st_ref[i, 0] for i in range(hp)]
        ds = [dstate[i] for i in range(hp)]
        st0_b = [s_.astype(BF16) for s_ in st0]
        ds_b = [s_.astype(BF16) for s_ in ds]
        do_b, v_b, kd_b, qa_b = do_v.astype(BF16), v.astype(BF16), (k * to_end).astype(BF16), (qf * a).astype(BF16)

        dq_inter = a * _per_head(_dot_nn, do_b, st0_b)
        dk_end = to_end * _per_head(_dot_nn, v_b, ds_b)
        dv_end = _per_head(_dot_nt, kd_b, ds_b)
        a_end_h = _heads(a_end)
        st_end = [st0[i] * a_end_h[i] + _dot_tn(_heads(v_b)[i], _heads(kd_b)[i]) for i in range(hp)]
        db_end = jnp.concatenate([jnp.sum(ds[i] * st_end[i], axis=0, keepdims=True) for i in range(hp)], axis=1)
        ds_new = [ds[i] * a_end_h[i] + _dot_tn(_heads(do_b)[i], _heads(qa_b)[i]) for i in range(hp)]

        dq_intra, dk_intra, dv_intra = _hgrn_intra_bwd(qf, k, v, b, do_v)
        dqf = dq_inter + dq_intra
        dk = dk_end + dk_intra
        dv = dv_end + dv_intra
        db = qf * dqf - k * dk
        db = db + jnp.where(rows == ch - 1, db_end, 0.0)
        dg = _exact_tri_matmul(_tri(False), db)
        df = dg / f - dk
        for i in range(hp):
            dstate[i] = ds_new[i]
        dq_ref[...] = (dqf * (HGRN_HEAD_DIM ** -0.5) * (sq + q_raw * sq * (1.0 - sq))).astype(BF16)
        df_ref[...] = (df * (1.0 - lb_row) * sg * (1.0 - sg)).astype(BF16)
        dv_ref[...] = dv.astype(BF16)
        dlb_ref[0:1, :] += jnp.sum(df * (1.0 - sg), axis=0, keepdims=True)

    rev = lambda n: nc - 1 - n
    tile = lambda c: pl.BlockSpec((ch, wb), lambda h, n, c=c: (rev(n), c + h))
    return pl.pallas_call(
        body, grid=(nh // hp, nc),
        in_specs=[tile(cq), tile(cf), tile(cv), tile(cg), tile(0), tile(0),
                  pl.BlockSpec((hp, 1, hd, hd), lambda h, n: (h, rev(n), 0, 0)),
                  pl.BlockSpec((1, wb), lambda h, n: (0, h)), pl.BlockSpec((1, wb), lambda h, n: (0, h))],
        out_specs=[tile(0), tile(0), tile(0), tile(0), pl.BlockSpec((SUBLANES, wb), lambda h, n: (0, h))],
        out_shape=[jax.ShapeDtypeStruct((t, w), BF16)] * 4 + [jax.ShapeDtypeStruct((SUBLANES, w), F32)],
        scratch_shapes=[pltpu.VMEM((hp, hd, hd), F32)],
        name=name, compiler_params=_cp(("parallel", "arbitrary")))(u, u, u, u, o, dp, states, lb, gain)


def _bucket_map():
    i = np.arange(WINDOW)[:, None]
    j = np.arange(2 * WINDOW)[None, :]
    dist = np.clip(WINDOW + i - j, 0, WINDOW - 1)
    max_exact = N_BUCKETS // 2
    logd = (np.log(np.maximum(dist, 1).astype(np.float32) / max_exact) / math.log(MAX_DISTANCE / max_exact))
    large = np.minimum(max_exact + (logd.astype(np.float32) * (N_BUCKETS - max_exact)).astype(np.int32), N_BUCKETS - 1)
    return np.where(dist < max_exact, dist, large).astype(np.int32)


def _bias_table(rel_bias, n_heads):
    bucket = jnp.asarray(_bucket_map())

    def body(rb_ref, bk_ref, o_ref):
        bk = bk_ref[...]
        i = lax.broadcasted_iota(jnp.int32, (WINDOW, 2 * WINDOW), 0)
        j = lax.broadcasted_iota(jnp.int32, (WINDOW, 2 * WINDOW), 1)
        band = ((j >= WINDOW) & (j - WINDOW <= i)) | ((j < WINDOW) & (j > i))
        for h in range(n_heads):
            def step(bi, acc):
                return jnp.where(bk == bi, rb_ref[bi, h], acc)
            table = lax.fori_loop(0, N_BUCKETS, step, jnp.zeros((WINDOW, 2 * WINDOW), F32))
            o_ref[h] = jnp.where(band, table, NEG_INF)

    return pl.pallas_call(
        body, in_specs=[pl.BlockSpec(memory_space=pltpu.SMEM), pl.BlockSpec(memory_space=pltpu.VMEM)],
        out_specs=pl.BlockSpec(memory_space=pltpu.VMEM),
        out_shape=jax.ShapeDtypeStruct((n_heads, WINDOW, 2 * WINDOW), F32), name="bias_table",
        compiler_params=_cp())(rel_bias, bucket)


def _bias_grad(dbias, n_heads):
    bucket = jnp.asarray(_bucket_map())

    def body(db_ref, bk_ref, o_ref):
        bk = bk_ref[...]
        lane = lax.broadcasted_iota(jnp.int32, (1, LANES), 1)

        def step(bi, carry):
            row = jnp.zeros((1, LANES), F32)
            for h in range(n_heads):
                val = jnp.sum(jnp.where(bk == bi, db_ref[h], 0.0))
                row = jnp.where(lane == h, val, row)
            o_ref[pl.ds(bi, 1), :] = row
            return carry

        lax.fori_loop(0, N_BUCKETS, step, 0)

    return pl.pallas_call(
        body, in_specs=[pl.BlockSpec(memory_space=pltpu.VMEM), pl.BlockSpec(memory_space=pltpu.VMEM)],
        out_specs=pl.BlockSpec(memory_space=pltpu.VMEM),
        out_shape=jax.ShapeDtypeStruct((N_BUCKETS, LANES), F32), name="bias_grad",
        compiler_params=_cp())(dbias, bucket)


def _no_prev_block(n, grp):
    j = lax.broadcasted_iota(jnp.int32, (grp * WINDOW, 2 * WINDOW), 1)
    return (j < WINDOW) & (n == 0)


def _attn_probs(no_prev, q_ref, kp_ref, kc_ref, bias_ref, sink_ref, hh, grp):
    ad, wn = ATTN_HEAD_DIM, WINDOW
    ksl = slice(hh * ad, (hh + 1) * ad)
    kw = jnp.concatenate([kp_ref[:, ksl], kc_ref[:, ksl]], axis=0).astype(BF16)
    qs = jnp.concatenate([q_ref[:, (hh * grp + g) * ad:(hh * grp + g + 1) * ad] for g in range(grp)], axis=0).astype(BF16)
    s = _dot_nt(qs, kw) * (ad ** -0.5) + bias_ref[hh]
    s = jnp.where(no_prev, NEG_INF, s)
    rr = lax.broadcasted_iota(jnp.int32, (grp * wn, 1), 0) >> WINDOW_SHIFT
    sink = jnp.zeros((grp * wn, 1), F32)
    for g in range(grp):
        sink = jnp.where(rr == g, sink_ref[hh * grp + g], sink)
    m = jnp.maximum(jnp.max(s, axis=1, keepdims=True), sink)
    p = jnp.exp(s - m)
    es = jnp.exp(sink - m)
    inv = 1.0 / (jnp.sum(p, axis=1, keepdims=True) + es)
    return qs, kw, p * inv, es * inv


GATE_BLOCK = 512


def _attn_fwd(u, bias_g, sinks, off, w, name):
    t = u.shape[0]
    wn, ad, kvw, gb = WINDOW, ATTN_HEAD_DIM, KV_WIDTH, GATE_BLOCK
    grp = (w // ad) // ATTN_KV_HEADS
    nb = t // wn
    n_gb = w // gb
    cq, ck, cv, cg = off["b_q"] // w, off["b_k"] // kvw, off["b_v"] // kvw, off["b_g"] // gb

    def body(q_ref, kp_ref, kc_ref, vp_ref, vc_ref, bias_ref, sink_ref, *rest):
        g_refs, (o_ref, p_ref) = rest[:n_gb], rest[n_gb:]
        no_prev = _no_prev_block(pl.program_id(0), grp)
        for hh in range(ATTN_KV_HEADS):
            _, _, p, _ = _attn_probs(no_prev, q_ref, kp_ref, kc_ref, bias_ref, sink_ref, hh, grp)
            ksl = slice(hh * ad, (hh + 1) * ad)
            vw = jnp.concatenate([vp_ref[:, ksl], vc_ref[:, ksl]], axis=0).astype(BF16)
            o = _dot_nn(p.astype(BF16), vw)
            for g in range(grp):
                o_ref[:, (hh * grp + g) * ad:(hh * grp + g + 1) * ad] = o[g * wn:(g + 1) * wn]
        for i in range(n_gb):
            sl = slice(i * gb, (i + 1) * gb)
            silu, _ = _silu_parts(g_refs[i][...])
            p_ref[:, sl] = (o_ref[:, sl] * silu).astype(BF16)

    prev = lambda n: jnp.maximum(n - 1, 0)
    row = pl.BlockSpec((wn, w), lambda n: (n, 0))
    return pl.pallas_call(
        body, grid=(nb,),
        in_specs=[pl.BlockSpec((wn, w), lambda n: (n, cq)),
                  pl.BlockSpec((wn, kvw), lambda n: (prev(n), ck)), pl.BlockSpec((wn, kvw), lambda n: (n, ck)),
                  pl.BlockSpec((wn, kvw), lambda n: (prev(n), cv)), pl.BlockSpec((wn, kvw), lambda n: (n, cv)),
                  pl.BlockSpec((ATTN_KV_HEADS, grp * wn, 2 * wn), lambda n: (0, 0, 0)),
                  pl.BlockSpec(memory_space=pltpu.SMEM)]
        + [pl.BlockSpec((wn, gb), lambda n, i=i: (n, cg + i)) for i in range(n_gb)],
        out_specs=[row, row],
        out_shape=[jax.ShapeDtypeStruct((t, w), F32), jax.ShapeDtypeStruct((t, w), BF16)], name=name,
        compiler_params=_cp(("parallel",)))(u, u, u, u, u, bias_g, sinks, *([u] * n_gb))


def _attn_bwd(u, o, dp, bias_g, sinks, off, w, name):
    t = u.shape[0]
    wn, ad, kvw, gb = WINDOW, ATTN_HEAD_DIM, KV_WIDTH, GATE_BLOCK
    grp = (w // ad) // ATTN_KV_HEADS
    nb = t // wn
    n_gb = w // gb
    cq, ck, cv, cg = off["b_q"] // w, off["b_k"] // kvw, off["b_v"] // kvw, off["b_g"] // gb

    def body(q_ref, kp_ref, kc_ref, vp_ref, vc_ref, o_ref, dp_ref, bias_ref, sink_ref, *rest):
        g_refs = rest[:n_gb]
        dq_ref, dkc_ref, dkp_ref, dvc_ref, dvp_ref, dbias_ref, dsink_ref, dg_ref, do_ref = rest[n_gb:]
        n = pl.program_id(0)

        @pl.when(n == 0)
        def _():
            dbias_ref[...] = jnp.zeros_like(dbias_ref)
            dsink_ref[...] = jnp.zeros_like(dsink_ref)

        for i in range(n_gb):
            sl = slice(i * gb, (i + 1) * gb)
            silu, dsilu = _silu_parts(g_refs[i][...])
            dp_v = dp_ref[:, sl]
            do_ref[:, sl] = dp_v * silu
            dg_ref[:, sl] = (dp_v * o_ref[:, sl] * dsilu).astype(BF16)

        lane = lax.broadcasted_iota(jnp.int32, (1, LANES), 1)
        rr = lax.broadcasted_iota(jnp.int32, (grp * wn, 1), 0) >> WINDOW_SHIFT
        dsink_row = jnp.zeros((1, LANES), F32)
        no_prev = _no_prev_block(n, grp)
        for hh in range(ATTN_KV_HEADS):
            qs, kw, p, psink = _attn_probs(no_prev, q_ref, kp_ref, kc_ref, bias_ref, sink_ref, hh, grp)
            ksl = slice(hh * ad, (hh + 1) * ad)
            vw = jnp.concatenate([vp_ref[:, ksl], vc_ref[:, ksl]], axis=0).astype(BF16)
            hs = [slice((hh * grp + g) * ad, (hh * grp + g + 1) * ad) for g in range(grp)]
            dos = jnp.concatenate([do_ref[:, sl] for sl in hs], axis=0)
            os_ = jnp.concatenate([o_ref[:, sl] for sl in hs], axis=0)
            delta = jnp.sum(dos * os_, axis=1, keepdims=True)
            dos_b = dos.astype(BF16)
            dp = _dot_nt(dos_b, vw)
            ds = p * (dp - delta)
            dbias_ref[hh] += ds
            sd = psink * delta
            for g in range(grp):
                val = -jnp.sum(jnp.where(rr == g, sd, 0.0))
                dsink_row = jnp.where(lane == hh * grp + g, val, dsink_row)
            ds_b = (ds * (ad ** -0.5)).astype(BF16)
            dq = _dot_nn(ds_b, kw)
            for g in range(grp):
                dq_ref[:, hs[g]] = dq[g * wn:(g + 1) * wn].astype(BF16)
            dkw = _dot_tn(ds_b, qs)
            dvw = _dot_tn(p.astype(BF16), dos_b)
            dkp_ref[:, ksl] = dkw[:wn]
            dkc_ref[:, ksl] = dkw[wn:]
            dvp_ref[:, ksl] = dvw[:wn]
            dvc_ref[:, ksl] = dvw[wn:]
        dsink_ref[0:1, :] += dsink_row

    prev = lambda n: jnp.maximum(n - 1, 0)
    kv_out = pl.BlockSpec((wn, kvw), lambda n: (n, 0))
    row = pl.BlockSpec((wn, w), lambda n: (n, 0))
    return pl.pallas_call(
        body, grid=(nb,),
        in_specs=[pl.BlockSpec((wn, w), lambda n: (n, cq)),
                  pl.BlockSpec((wn, kvw), lambda n: (prev(n), ck)), pl.BlockSpec((wn, kvw), lambda n: (n, ck)),
                  pl.BlockSpec((wn, kvw), lambda n: (prev(n), cv)), pl.BlockSpec((wn, kvw), lambda n: (n, cv)),
                  row, row,
                  pl.BlockSpec((ATTN_KV_HEADS, grp * wn, 2 * wn), lambda n: (0, 0, 0)),
                  pl.BlockSpec(memory_space=pltpu.SMEM)]
        + [pl.BlockSpec((wn, gb), lambda n, i=i: (n, cg + i)) for i in range(n_gb)],
        out_specs=[row, kv_out, kv_out, kv_out, kv_out,
                   pl.BlockSpec((ATTN_KV_HEADS, grp * wn, 2 * wn), lambda n: (0, 0, 0)),
                   pl.BlockSpec((SUBLANES, LANES), lambda n: (0, 0)), row],
        out_shape=[jax.ShapeDtypeStruct((t, w), BF16)] + [jax.ShapeDtypeStruct((t, kvw), F32)] * 4
        + [jax.ShapeDtypeStruct((ATTN_KV_HEADS, grp * wn, 2 * wn), F32), jax.ShapeDtypeStruct((SUBLANES, LANES), F32),
           jax.ShapeDtypeStruct((t, w), BF16)],
        scratch_shapes=[pltpu.VMEM((wn, w), F32)],
        name=name, compiler_params=_cp(("arbitrary",)))(u, u, u, u, u, o, dp, bias_g, sinks, *([u] * n_gb))


def _kv_combine(cur, prv, name):
    t, kvw = cur.shape

    def body(i, nt, c_ref, p_ref, o_ref):
        nxt = jnp.where(i < nt - 1, p_ref[...], 0.0)
        o_ref[...] = (c_ref[...] + nxt).astype(BF16)

    return _ew(body, name, t, 1, kvw, [(cur, "tile", 0), (prv, "next", 0)], [BF16], tt=WINDOW)[0]


def _shift_down(h, tail, k, rows):
    tt = h.shape[0]
    out = pltpu.roll(h, k, 0)
    for r in range(k):
        out = jnp.where(rows == r, tail[tt - k + r:tt - k + r + 1, :], out)
    return out


def _shift_up(h, head, k, rows):
    tt = h.shape[0]
    out = pltpu.roll(h, tt - k, 0)
    for r in range(k):
        out = jnp.where(rows == tt - k + r, head[r:r + 1, :], out)
    return out


def _conv_fwd(u, conv_w, off, w, name):
    t = u.shape[0]
    wb = 512
    c = lambda nme: off[nme] // wb

    def body(i, nt, cb_ref, cc_ref, ccp_ref, cx_ref, cxp_ref, cg_ref, w_ref, p_ref):
        h = cc_ref[...] * cx_ref[...]
        hp = jnp.where(i > 0, ccp_ref[...] * cxp_ref[...], 0.0)
        rows = lax.broadcasted_iota(jnp.int32, h.shape, 0)
        y = w_ref[0:1, :] * _shift_down(h, hp, 2, rows) + w_ref[1:2, :] * _shift_down(h, hp, 1, rows) + w_ref[2:3, :] * h
        silu, _ = _silu_parts(cg_ref[...])
        p_ref[...] = (cb_ref[...] * y * silu).astype(BF16)

    return _ew(body, name, t, w // wb, wb,
               [(u, "tile", c("c_b")), (u, "tile", c("c_c")), (u, "prev", c("c_c")), (u, "tile", c("c_x")),
                (u, "prev", c("c_x")), (u, "tile", c("c_g")), (conv_w, "row", 0)], [BF16])[0]


def _conv_bwd(dp, u, conv_w, off, w, name):
    t = u.shape[0]
    wb = 512
    c = lambda nme: off[nme] // wb

    def body(i, nt, dp_ref, dpn_ref, cb_ref, cbn_ref, cg_ref, cgn_ref, cc_ref, ccp_ref, cx_ref, cxp_ref, w_ref,
             dcb_ref, dcc_ref, dcx_ref, dcg_ref, acc_ref):
        @pl.when(i == 0)
        def _():
            acc_ref[...] = jnp.zeros_like(acc_ref)

        cc, cx, cb = cc_ref[...], cx_ref[...], cb_ref[...]
        h = cc * cx
        hp = jnp.where(i > 0, ccp_ref[...] * cxp_ref[...], 0.0)
        rows = lax.broadcasted_iota(jnp.int32, h.shape, 0)
        h1 = _shift_down(h, hp, 1, rows)
        h2 = _shift_down(h, hp, 2, rows)
        w0, w1, w2 = w_ref[0:1, :], w_ref[1:2, :], w_ref[2:3, :]
        y = w0 * h2 + w1 * h1 + w2 * h
        silu, dsilu = _silu_parts(cg_ref[...])
        dp_v = dp_ref[...]
        dcg_ref[...] = (dp_v * cb * y * dsilu).astype(BF16)
        dcb_ref[...] = (dp_v * y * silu).astype(BF16)
        dy = dp_v * cb * silu
        silu_n, _ = _silu_parts(cgn_ref[...])
        dyn = jnp.where(i < nt - 1, dpn_ref[...] * cbn_ref[...] * silu_n, 0.0)
        dh = w2 * dy + w1 * _shift_up(dy, dyn, 1, rows) + w0 * _shift_up(dy, dyn, 2, rows)
        dcc_ref[...] = (dh * cx).astype(BF16)
        dcx_ref[...] = (dh * cc).astype(BF16)
        acc_ref[0:1, :] += jnp.sum(dy * h2, axis=0, keepdims=True)
        acc_ref[1:2, :] += jnp.sum(dy * h1, axis=0, keepdims=True)
        acc_ref[2:3, :] += jnp.sum(dy * h, axis=0, keepdims=True)

    return _ew(body, name, t, w // wb, wb,
               [(dp, "tile", 0), (dp, "next", 0), (u, "tile", c("c_b")), (u, "next", c("c_b")),
                (u, "tile", c("c_g")), (u, "next", c("c_g")), (u, "tile", c("c_c")), (u, "prev", c("c_c")),
                (u, "tile", c("c_x")), (u, "prev", c("c_x")), (conv_w, "row", 0)],
               [BF16] * 4, accs=[SUBLANES])


def _proj_merge(p, w, ya, yb, u, off, name):
    t, k = p.shape
    d = w.shape[1]
    gb = GATE_BLOCK
    nb = d // gb
    tt = _tile(t, 256)
    cols = [off[nme] // gb for nme in ("m_a", "m_b", "m_c")]

    def body(p_ref, w_ref, ya_ref, yb_ref, *rest):
        m_refs, (yc_ref, mg_ref) = rest[:3 * nb], rest[3 * nb:]
        yc = _dot_nn(p_ref[...], w_ref[...])
        yc_ref[...] = yc
        for j in range(nb):
            sl = slice(j * gb, (j + 1) * gb)
            mg_ref[:, sl] = (_sigmoid(m_refs[j][...]) * ya_ref[:, sl] + _sigmoid(m_refs[nb + j][...]) * yb_ref[:, sl]
                             + _sigmoid(m_refs[2 * nb + j][...]) * yc[:, sl]).astype(BF16)

    row = pl.BlockSpec((tt, d), lambda i: (i, 0))
    gates = [pl.BlockSpec((tt, gb), lambda i, c=c0 + j: (i, c)) for c0 in cols for j in range(nb)]
    return pl.pallas_call(
        body, grid=(t // tt,),
        in_specs=[pl.BlockSpec((tt, k), lambda i: (i, 0)), pl.BlockSpec((k, d), lambda i: (0, 0)), row, row] + gates,
        out_specs=[row, row],
        out_shape=[jax.ShapeDtypeStruct((t, d), F32), jax.ShapeDtypeStruct((t, d), BF16)],
        name=name, compiler_params=_cp(("parallel",)))(p, w, ya, yb, *([u] * (3 * nb)))


def _lower_bounds(lb_param):
    def body(p_ref, o_ref):
        p = p_ref[...]
        e = jnp.exp(p - jnp.max(p, axis=0, keepdims=True))
        soft = e / jnp.sum(e, axis=0, keepdims=True)
        acc = jnp.zeros_like(soft[0:1])
        o_ref[0:1, :] = acc
        for l in range(1, DEPTH):
            acc = acc + soft[l:l + 1]
            o_ref[l:l + 1, :] = acc

    return pl.pallas_call(body, out_shape=jax.ShapeDtypeStruct(lb_param.shape, F32), name="lower_bounds",
                          compiler_params=_cp())(lb_param)


def _lower_bounds_bwd(lb_param, dlower):
    def body(p_ref, d_ref, o_ref):
        p = p_ref[...]
        e = jnp.exp(p - jnp.max(p, axis=0, keepdims=True))
        soft = e / jnp.sum(e, axis=0, keepdims=True)
        dl = d_ref[...]
        ds = [jnp.zeros_like(dl[0:1])]
        for j in range(1, DEPTH):
            acc = dl[j:j + 1]
            for l in range(j + 1, DEPTH):
                acc = acc + dl[l:l + 1]
            ds.append(acc)
        inner = ds[0] * soft[0:1]
        for j in range(1, DEPTH):
            inner = inner + ds[j] * soft[j:j + 1]
        for j in range(DEPTH):
            o_ref[j:j + 1, :] = soft[j:j + 1] * (ds[j] - inner)

    return pl.pallas_call(body, out_shape=jax.ShapeDtypeStruct(lb_param.shape, F32), name="lower_bounds_bwd",
                          compiler_params=_cp())(lb_param, dlower)


def _exchange(arrays, scatter, name, chips=False):
    n_arr = len(arrays)
    n_slot = N_DEV // 2 if chips else N_DEV

    def body(*refs):
        srcs, dsts = refs[:n_arr], refs[n_arr:2 * n_arr]
        send_sems, recv_sems, local_sems = refs[2 * n_arr:]
        me = (2 * lax.axis_index("x") + lax.axis_index("y") if chips
              else 4 * lax.axis_index("x") + 2 * lax.axis_index("y") + lax.axis_index("c"))
        copies = _peer_copies(srcs, dsts, send_sems, recv_sems, scatter, chips)
        for a in range(n_arr):
            copies.append(pltpu.make_async_copy(srcs[a].at[me] if scatter else srcs[a], dsts[a].at[me], local_sems.at[a]))
        for cp in copies:
            cp.start()
        for cp in copies:
            cp.wait()

    out_shape = [jax.ShapeDtypeStruct(a.shape if scatter else (n_slot,) + a.shape, a.dtype) for a in arrays]
    anyspec = pl.BlockSpec(memory_space=pl.ANY)
    res = pl.pallas_call(
        body, in_specs=[anyspec] * n_arr, out_specs=[anyspec] * n_arr, out_shape=out_shape,
        scratch_shapes=[pltpu.SemaphoreType.DMA((n_arr * (n_slot - 1),)), pltpu.SemaphoreType.DMA((n_arr * (n_slot - 1),)),
                        pltpu.SemaphoreType.DMA((n_arr,))],
        name=name)(*arrays)
    return list(res)


def _peer_copies(srcs, lands, send_sems, recv_sems, scatter, chips=False):
    x, y, c = lax.axis_index("x"), lax.axis_index("y"), lax.axis_index("c")
    flips = [k for k in range(1, N_DEV) if not (chips and k & 1)]
    slot = (lambda px, py, pc: 2 * px + py) if chips else (lambda px, py, pc: 4 * px + 2 * py + pc)
    copies = []
    for a in range(len(srcs)):
        for i, k in enumerate(flips):
            px = 1 - x if k & 4 else x
            py = 1 - y if k & 2 else y
            pc = 1 - c if k & 1 else c
            src = srcs[a].at[slot(px, py, pc)] if scatter else srcs[a]
            copies.append(pltpu.make_async_remote_copy(
                src_ref=src, dst_ref=lands[a].at[slot(x, y, c)],
                send_sem=send_sems.at[a * len(flips) + i], recv_sem=recv_sems.at[a * len(flips) + i],
                device_id=(px, py, pc), device_id_type=pl.DeviceIdType.MESH))
    return copies


def _gather_two_level(arrays, name):
    n_arr = len(arrays)
    per = N_DEV - 1

    def body(*refs):
        srcs, outs = refs[:n_arr], refs[n_arr:2 * n_arr]
        send_sems, recv_sems, local_sems = refs[2 * n_arr:]
        x, y, c = lax.axis_index("x"), lax.axis_index("y"), lax.axis_index("c")
        me, sibling = (x, y, c), (x, y, 1 - c)
        chips = [(1 - x, y), (x, 1 - y), (1 - x, 1 - y)]

        def copy(a, k, block, to, src=None):
            dst = outs[a].at[4 * block[0] + 2 * block[1] + block[2]]
            return pltpu.make_async_remote_copy(
                src_ref=dst if src is None else src, dst_ref=dst,
                send_sem=send_sems.at[a * per + k], recv_sem=recv_sems.at[a * per + k],
                device_id=to, device_id_type=pl.DeviceIdType.MESH)

        own, first, passed = [], [], []
        for a in range(n_arr):
            own.append(pltpu.make_async_copy(srcs[a], outs[a].at[4 * x + 2 * y + c], local_sems.at[a]))
            first.append(copy(a, 0, me, sibling, src=srcs[a]))
            first += [copy(a, 1 + j, me, (*chip, c), src=srcs[a]) for j, chip in enumerate(chips)]
        for cp in own + first:
            cp.start()
        for a in range(n_arr):
            for j, chip in enumerate(chips):
                copy(a, 1 + j, (*chip, c), me).wait_recv()
                passed.append(copy(a, 4 + j, (*chip, c), sibling))
                passed[-1].start()
        for a in range(n_arr):
            copy(a, 0, sibling, me).wait_recv()
            for j, chip in enumerate(chips):
                copy(a, 4 + j, (*chip, 1 - c), me).wait_recv()
        for cp in first + passed:
            cp.wait_send()
        for cp in own:
            cp.wait()

    anyspec = pl.BlockSpec(memory_space=pl.ANY)
    res = pl.pallas_call(
        body, in_specs=[anyspec] * n_arr, out_specs=[anyspec] * n_arr,
        out_shape=[jax.ShapeDtypeStruct((N_DEV,) + a.shape, a.dtype) for a in arrays],
        scratch_shapes=[pltpu.SemaphoreType.DMA((n_arr * per,)), pltpu.SemaphoreType.DMA((n_arr * per,)),
                        pltpu.SemaphoreType.DMA((n_arr,))],
        name=name)(*arrays)
    return list(res)


def _sibling_swap(arrays, name):
    n_arr = len(arrays)
    n_chip = N_DEV // 2

    def body(*refs):
        srcs, outs = refs[:n_arr], refs[n_arr:2 * n_arr]
        send_sems, recv_sems = refs[2 * n_arr:]
        x, y, c = lax.axis_index("x"), lax.axis_index("y"), lax.axis_index("c")
        copies = []
        for a in range(n_arr):
            for j in range(n_chip):
                copies.append(pltpu.make_async_remote_copy(
                    src_ref=srcs[a].at[2 * j + 1 - c], dst_ref=outs[a].at[j],
                    send_sem=send_sems.at[a * n_chip + j], recv_sem=recv_sems.at[a * n_chip + j],
                    device_id=(x, y, 1 - c), device_id_type=pl.DeviceIdType.MESH))
        for cp in copies:
            cp.start()
        for cp in copies:
            cp.wait()

    anyspec = pl.BlockSpec(memory_space=pl.ANY)
    res = pl.pallas_call(
        body, in_specs=[anyspec] * n_arr, out_specs=[anyspec] * n_arr,
        out_shape=[jax.ShapeDtypeStruct((n_chip,) + a.shape[1:], a.dtype) for a in arrays],
        scratch_shapes=[pltpu.SemaphoreType.DMA((n_arr * n_chip,)), pltpu.SemaphoreType.DMA((n_arr * n_chip,))],
        name=name)(*arrays)
    return list(res)


def _pair_sum(send, stage, core, name):
    _, r, c = send.shape
    n_chip = stage.shape[0]
    tr = _tile(r, 128)

    def body(core_ref, a_ref, b_ref, o_ref):
        o_ref[...] = a_ref[...] + b_ref[...]

    return pl.pallas_call(
        body,
        grid_spec=pltpu.PrefetchScalarGridSpec(
            num_scalar_prefetch=1, grid=(n_chip, r // tr),
            in_specs=[pl.BlockSpec((1, tr, c), lambda j, i, core_ref: (2 * j + core_ref[0], i, 0)),
                      pl.BlockSpec((1, tr, c), lambda j, i, core_ref: (j, i, 0))],
            out_specs=pl.BlockSpec((1, tr, c), lambda j, i, core_ref: (j, i, 0))),
        out_shape=jax.ShapeDtypeStruct(stage.shape, F32), name=name,
        compiler_params=_cp(("parallel", "parallel")))(core, send, stage)


_HBM_SPEC = pl.BlockSpec(memory_space=pltpu.HBM)
_SEM_SPEC = pl.BlockSpec(memory_space=pltpu.SEMAPHORE)
_ANY_SPEC = pl.BlockSpec(memory_space=pl.ANY)
_DATAFLOW = pltpu.SideEffectType.DATAFLOW_SIDE_EFFECTING


def _exchange_start(arrays, scatter, name, dep=None, chips=False):
    n_arr = len(arrays)
    n_slot = N_DEV // 2 if chips else N_DEV
    n_sem = n_arr * (n_slot - 1)
    me = (2 * lax.axis_index("x") + lax.axis_index("y") if chips
          else 4 * lax.axis_index("x") + 2 * lax.axis_index("y") + lax.axis_index("c"))
    lands = []
    for a in arrays:
        own = lax.dynamic_index_in_dim(a, me, 0, keepdims=False) if scatter else a
        shape = a.shape if scatter else (n_slot,) + a.shape
        lands.append(lax.dynamic_update_index_in_dim(lax.empty(shape, a.dtype), own, me, 0))
    dep_specs, dep_args = _dep_specs(dep)

    def body(*refs):
        srcs, lnds = refs[:n_arr], refs[n_arr:2 * n_arr]
        outs = refs[2 * n_arr + len(dep_args):]
        send_sems, recv_sems, token = outs[0], outs[1], outs[2 + 2 * n_arr]
        for cp in _peer_copies(srcs, lnds, send_sems, recv_sems, scatter, chips):
            cp.start()
        token[...] = jnp.zeros_like(token)

    thru = [pltpu.HBM(a.shape, a.dtype) for a in list(arrays) + lands]
    return pl.pallas_call(
        body, name=name,
        out_shape=(pltpu.SemaphoreType.DMA((n_sem,)), pltpu.SemaphoreType.DMA((n_sem,)), *thru,
                   jax.ShapeDtypeStruct((SUBLANES, LANES), F32)),
        in_specs=[_HBM_SPEC] * (2 * n_arr) + dep_specs,
        out_specs=(_SEM_SPEC, _SEM_SPEC, *[_HBM_SPEC] * (2 * n_arr), pl.BlockSpec(memory_space=pltpu.VMEM)),
        input_output_aliases={i: 2 + i for i in range(2 * n_arr)},
        compiler_params=pltpu.CompilerParams(has_side_effects=_DATAFLOW),
    )(*[pltpu.with_memory_space_constraint(a, pltpu.HBM) for a in list(arrays) + lands], *dep_args)


def _exchange_wait(started, scatter, name, after, chips=False):
    send_sems, recv_sems = started[0], started[1]
    thru = list(started[2:-1])
    n_arr = len(thru) // 2

    def body(*refs):
        srcs, lnds = refs[:n_arr], refs[n_arr:2 * n_arr]
        for cp in _peer_copies(srcs, lnds, refs[2 * n_arr], refs[2 * n_arr + 1], scatter, chips):
            cp.wait_send()
            cp.wait_recv()

    res = pl.pallas_call(
        body, name=name, out_shape=tuple(pltpu.HBM(a.shape, a.dtype) for a in thru),
        in_specs=[_HBM_SPEC] * (2 * n_arr) + [_SEM_SPEC, _SEM_SPEC, _ANY_SPEC],
        out_specs=tuple([_HBM_SPEC] * (2 * n_arr)),
        input_output_aliases={i: i for i in range(2 * n_arr)},
        compiler_params=pltpu.CompilerParams(has_side_effects=_DATAFLOW),
    )(*thru, send_sems, recv_sems, after)
    return list(res[n_arr:])


def _unshard_cols(g, name):
    nd, r, s = g.shape
    tr = _tile(r, 64)

    def body(i_ref, o_ref):
        for p in range(nd):
            o_ref[:, p * s:(p + 1) * s] = i_ref[p]

    return pl.pallas_call(
        body, grid=(r // tr,), in_specs=[pl.BlockSpec((nd, tr, s), lambda i: (0, i, 0))],
        out_specs=pl.BlockSpec((tr, nd * s), lambda i: (i, 0)),
        out_shape=jax.ShapeDtypeStruct((r, nd * s), g.dtype), name=name, compiler_params=_cp(("parallel",)))(g)


def _shard_cols(g, name):
    r, n = g.shape
    s = n // N_DEV
    tr = _tile(r, 64)

    def body(i_ref, o_ref):
        for p in range(N_DEV):
            o_ref[p] = i_ref[:, p * s:(p + 1) * s]

    return pl.pallas_call(
        body, grid=(r // tr,), in_specs=[pl.BlockSpec((tr, n), lambda i: (i, 0))],
        out_specs=pl.BlockSpec((N_DEV, tr, s), lambda i: (0, i, 0)),
        out_shape=jax.ShapeDtypeStruct((N_DEV, r, s), g.dtype), name=name, compiler_params=_cp(("parallel",)))(g)


def _slot_sum(slots, name):
    nd, r, c = slots.shape
    tr = _tile(r, 64)

    def body(s_ref, o_ref):
        acc = s_ref[0]
        for p in range(1, nd):
            acc = acc + s_ref[p]
        o_ref[...] = acc

    return pl.pallas_call(
        body, grid=(r // tr,), in_specs=[pl.BlockSpec((nd, tr, c), lambda i: (0, i, 0))],
        out_specs=pl.BlockSpec((tr, c), lambda i: (i, 0)),
        out_shape=jax.ShapeDtypeStruct((r, c), F32), name=name, compiler_params=_cp(("parallel",)))(slots)


def _adamw(w, g, m, v, name):
    r, c = w.shape
    tr = _tile(r, 256)
    c1 = 1.0 - ADAM_B1 ** ADAM_STEP
    c2 = 1.0 - ADAM_B2 ** ADAM_STEP

    def body(w_ref, g_ref, m_ref, v_ref, d_ref, nm_ref, nv_ref):
        gv = g_ref[...]
        nm = ADAM_B1 * m_ref[...] + (1.0 - ADAM_B1) * gv
        nv = ADAM_B2 * v_ref[...] + (1.0 - ADAM_B2) * (gv * gv)
        nm_ref[...] = nm
        nv_ref[...] = nv
        d_ref[...] = -ADAM_LR * ((nm / c1) / (jnp.sqrt(nv / c2) + ADAM_EPS) + ADAM_WD * w_ref[...])

    spec = pl.BlockSpec((tr, c), lambda i: (i, 0))
    return pl.pallas_call(
        body, grid=(r // tr,), in_specs=[spec] * 4, out_specs=[spec] * 3,
        out_shape=[jax.ShapeDtypeStruct((r, c), F32)] * 3, name=name, compiler_params=_cp(("parallel",)))(w, g, m, v)


def _forward_backward(x, target, weights_hook, grads_hook, lb_param, hgrn_norm_g, attn_sinks, rel_bias, ln_g, ln_b):
    t, d = x.shape
    w = d // 2
    off, n_in = _offsets(d)
    n_heads = w // ATTN_HEAD_DIM
    grp = n_heads // ATTN_KV_HEADS

    lower = _lower_bounds(lb_param)
    bias = _bias_table(rel_bias, n_heads)
    bias_g = bias.reshape(ATTN_KV_HEADS, grp * WINDOW, 2 * WINDOW)

    saved, weights = [], []
    xb = x.astype(BF16)
    for l in range(DEPTH):
        wl, token = weights_hook(l, x)
        weights.append(wl)
        s = {"x": x, "xb": xb}
        u = _mm_nt_cols(xb, wl["w_in_t"], f"in_proj", dep=token)
        s["u"] = u
        lb_l, gain_l, cw_l = lower[l:l + 1], hgrn_norm_g[l:l + 1], wl["conv_w"]
        o_a, states, p_a = _hgrn_fwd(u, lb_l, gain_l, off, f"hgrn_fwd")
        o_b, p_b = _attn_fwd(u, bias_g, attn_sinks[l], off, w, f"attn_fwd")
        p_c = _conv_fwd(u, cw_l, off, w, f"conv_fwd")
        y_a = _mm_nn(p_a, wl["w_proj_hgrn"], f"proj_a", tn=2048)
        y_b = _mm_nn(p_b, wl["w_proj_attn"], f"proj_b", tn=2048)
        y_c, merged = _proj_merge(p_c, wl["w_proj_conv"], y_a, y_b, u, off, f"proj_c_merge")
        x, xb, xhat, rstd = _out_proj_ln(merged, wl["w_out"], x, ln_g[l:l + 1], ln_b[l:l + 1], f"out_proj_ln")
        s.update(o_a=o_a, states=states, p_a=p_a, o_b=o_b, p_b=p_b, p_c=p_c, y_a=y_a, y_b=y_b, y_c=y_c,
                 merged=merged, xhat=xhat, rstd=rstd)
        saved.append(s)

    loss_acc, dx = _loss_head(x, target)

    d_ln, d_lower, d_gain, d_sink, d_conv = [None] * DEPTH, [None] * DEPTH, [None] * DEPTH, [None] * DEPTH, [None] * DEPTH
    dbias_total = None
    for l in reversed(range(DEPTH)):
        wl, s = weights[l], saved[l]
        u = s["u"]
        lb_l, gain_l, cw_l = lower[l:l + 1], hgrn_norm_g[l:l + 1], wl["conv_w"]
        dz, dzb, d_ln[l], dya, dyb, dyc, dma, dmb, dmc = _ln_bwd_out_proj_merge(
            dx, s["xhat"], s["rstd"], ln_g[l:l + 1], wl["w_out"], u, s["y_a"], s["y_b"], s["y_c"], off,
            f"ln_bwd_merge_bwd")
        g_out = _mm_tn(s["merged"], dzb, f"g_out", tn=2048)
        g_pa = _mm_tn(s["p_a"], dya, f"g_proj_a", tn=2048)
        g_pb = _mm_tn(s["p_b"], dyb, f"g_proj_b", tn=2048)
        g_pc = _mm_tn(s["p_c"], dyc, f"g_proj_c", tn=2048)
        dpa = _mm_nt(dya, wl["w_proj_hgrn"], f"d_p_a", tk=2048)
        dpb = _mm_nt(dyb, wl["w_proj_attn"], f"d_p_b", tk=2048)
        dpc = _mm_nt(dyc, wl["w_proj_conv"], f"d_p_c", tk=2048)
        d_aq, d_af, d_ai, d_ag, acc_a = _hgrn_bwd(u, lb_l, gain_l, s["states"], s["o_a"], dpa, off, f"hgrn_bwd")
        d_lower[l], d_gain[l] = acc_a[0:1], acc_a[1:2]
        d_bq, dkc, dkp, dvc, dvp, dbias_l, d_sink[l], d_bg = _attn_bwd(
            u, s["o_b"], dpb, bias_g, attn_sinks[l], off, w, f"attn_bwd")
        d_bk = _kv_combine(dkc, dkp, f"k_combine")
        d_bv = _kv_combine(dvc, dvp, f"v_combine")
        dbias_total = dbias_l if dbias_total is None else dbias_total + dbias_l
        d_cb, d_cc, d_cx, d_cg, d_conv[l] = _conv_bwd(dpc, u, cw_l, off, w, f"conv_bwd")
        du = jnp.concatenate([d_aq, d_af, d_ai, d_ag, d_bq, d_bk, d_bv, d_bg, d_cb, d_cc, d_cx, d_cg, dma, dmb, dmc], axis=1)
        g_in_t = _mm_tn_rows(du, s["xb"], f"g_in")
        token = grads_hook(l, {"w_in_t": g_in_t, "w_proj_hgrn": g_pa, "w_proj_attn": g_pb, "w_proj_conv": g_pc, "w_out": g_out})
        dx = _mm_nn_acc(du, wl["w_in_t"], f"d_x", add=dz, add_scale=ALPHA, dep=token)

    d_lower_all = jnp.concatenate([a[0:1] for a in d_lower], axis=0)
    small = {
        "lb_param": _lower_bounds_bwd(lb_param, d_lower_all),
        "hgrn_norm_g": jnp.concatenate([a[0:1] for a in d_gain], axis=0),
        "attn_sinks": jnp.concatenate([a[0:1, :n_heads] for a in d_sink], axis=0),
        "conv_w": jnp.stack([a[0:3] for a in d_conv], axis=0),
        "rel_bias": _bias_grad(dbias_total.reshape(n_heads, WINDOW, 2 * WINDOW), n_heads)[:, :n_heads],
        "ln_g": jnp.concatenate([a[0:1] for a in d_ln], axis=0),
        "ln_b": jnp.concatenate([a[1:2] for a in d_ln], axis=0),
    }
    return loss_acc, dx, small


BIG = ("w_in", "w_proj_hgrn", "w_proj_attn", "w_proj_conv", "w_out")
SMALL = ("lb_param", "hgrn_norm_g", "attn_sinks", "conv_w", "rel_bias", "ln_g", "ln_b")
ORDER = ("w_in", "w_proj_hgrn", "w_proj_attn", "w_proj_conv", "w_out", "lb_param", "hgrn_norm_g", "attn_sinks",
         "conv_w", "rel_bias", "ln_g", "ln_b")


def _pack(parts):
    flat = jnp.concatenate([p.reshape(-1) for p in parts])
    n = flat.shape[0]
    unit = SUBLANES * LANES
    total = -(-n // unit) * unit
    return jnp.pad(flat, (0, total - n)).reshape(total // LANES, LANES)


def _unpack(packed, shapes):
    flat = packed.reshape(-1)
    out, o = [], 0
    for shp in shapes:
        n = int(np.prod(shp))
        out.append(flat[o:o + n].reshape(shp))
        o += n
    return out


def kernel(x, w_in, w_proj_hgrn, w_proj_attn, w_proj_conv, w_out, lb_param, hgrn_norm_g, attn_sinks, conv_w, rel_bias, ln_g, ln_b, loss_target, m_w_in, m_w_proj_hgrn, m_w_proj_attn, m_w_proj_conv, m_w_out, m_lb_param, m_hgrn_norm_g, m_attn_sinks, m_conv_w, m_rel_bias, m_ln_g, m_ln_b, v_w_in, v_w_proj_hgrn, v_w_proj_attn, v_w_proj_conv, v_w_out, v_lb_param, v_hgrn_norm_g, v_attn_sinks, v_conv_w, v_rel_bias, v_ln_g, v_ln_b):
    params = dict(w_in=w_in, w_proj_hgrn=w_proj_hgrn, w_proj_attn=w_proj_attn, w_proj_conv=w_proj_conv, w_out=w_out,
                  lb_param=lb_param, hgrn_norm_g=hgrn_norm_g, attn_sinks=attn_sinks, conv_w=conv_w, rel_bias=rel_bias,
                  ln_g=ln_g, ln_b=ln_b)
    mom_m = dict(w_in=m_w_in, w_proj_hgrn=m_w_proj_hgrn, w_proj_attn=m_w_proj_attn, w_proj_conv=m_w_proj_conv,
                 w_out=m_w_out, lb_param=m_lb_param, hgrn_norm_g=m_hgrn_norm_g, attn_sinks=m_attn_sinks,
                 conv_w=m_conv_w, rel_bias=m_rel_bias, ln_g=m_ln_g, ln_b=m_ln_b)
    mom_v = dict(w_in=v_w_in, w_proj_hgrn=v_w_proj_hgrn, w_proj_attn=v_w_proj_attn, w_proj_conv=v_w_proj_conv,
                 w_out=v_w_out, lb_param=v_lb_param, hgrn_norm_g=v_hgrn_norm_g, attn_sinks=v_attn_sinks,
                 conv_w=v_conv_w, rel_bias=v_rel_bias, ln_g=v_ln_g, ln_b=v_ln_b)
    d = x.shape[-1]
    me = 4 * lax.axis_index("x") + 2 * lax.axis_index("y") + lax.axis_index("c")
    for group in (params, mom_m, mom_v):
        group["w_in"] = jnp.swapaxes(group["w_in"], 1, 2)

    def shards_of(l):
        return [params[n][l].astype(BF16) for n in BIG] + [conv_w[l]]

    gathers = {}

    def weights_hook(l, x_in):
        if l == 0:
            got = _gather_two_level(shards_of(0), "gather_weights_0")
        else:
            got = _exchange_wait(gathers.pop(l), False, f"gather_wait_{l}", x_in)
        token = None
        if l + 1 < DEPTH:
            gathers[l + 1] = _exchange_start(shards_of(l + 1), False, f"gather_start_{l + 1}", dep=got[0])
            token = gathers[l + 1][-1]
        wl = {
            "w_in_t": got[0].reshape(-1, d),
            "w_proj_hgrn": _unshard_cols(got[1], "unshard_w_proj_hgrn"),
            "w_proj_attn": _unshard_cols(got[2], "unshard_w_proj_attn"),
            "w_proj_conv": _unshard_cols(got[3], "unshard_w_proj_conv"),
            "w_out": got[4].reshape(d, d),
            "conv_w": _unshard_cols(got[5], "unshard_conv_w"),
        }
        return wl, token

    grads = {n: [None] * DEPTH for n in BIG}
    scatters = {}

    def finish_scatter(l, after):
        got = _exchange_wait(scatters.pop(l), True, f"scatter_wait_{l}", after, chips=(l == 0))
        for n, slots in zip(BIG, got):
            grads[n][l] = _slot_sum(slots, f"sum_{n}_chips" if l == 0 else f"sum_{n}")
        return got[0]

    def grads_hook(l, g):
        send = [g["w_in_t"].reshape(N_DEV, -1, d), _shard_cols(g["w_proj_hgrn"], "shard_g_proj_a"),
                _shard_cols(g["w_proj_attn"], "shard_g_proj_b"), _shard_cols(g["w_proj_conv"], "shard_g_proj_c"),
                g["w_out"].reshape(N_DEV, d // N_DEV, d)]
        dep = finish_scatter(l + 1, send[0]) if l + 1 < DEPTH else None
        if l == 0:
            core = lax.axis_index("c").astype(jnp.int32).reshape(1)
            staged = _sibling_swap(send, "pair_swap_grads")
            send = [_pair_sum(s, st, core, f"pair_sum_{n}") for n, s, st in zip(BIG, send, staged)]
        scatters[l] = _exchange_start(send, True, f"scatter_start_{l}", dep=dep, chips=(l == 0))
        return scatters[l][-1]

    loss_acc, dx, small = _forward_backward(
        x[0], loss_target[0], weights_hook, grads_hook, lb_param, hgrn_norm_g, attn_sinks, rel_bias, ln_g, ln_b)
    loss = lax.psum(0.5 * jnp.sum(loss_acc[0]) / d, ("x", "y", "c"))
    finish_scatter(0, dx)
    for n in BIG:
        grads[n] = jnp.stack(grads[n], axis=0)

    small_shapes = [small[n].shape for n in SMALL]
    packed = _pack([small[n] for n in SMALL])
    got = _exchange([packed], False, "gather_small_grads")[0]
    summed = _unpack(_slot_sum(got, "sum_small_grads"), small_shapes)
    for n, g in zip(SMALL, summed):
        grads[n] = g
    cs = conv_w.shape[-1]
    grads["conv_w"] = lax.dynamic_slice_in_dim(grads["conv_w"], me * cs, cs, axis=2)

    delta, new_m, new_v = {}, {}, {}
    for n in BIG:
        shp = params[n].shape
        flat = lambda a: a.reshape(-1, shp[-1])
        dl, nm, nv = _adamw(flat(params[n]), flat(grads[n]), flat(mom_m[n]), flat(mom_v[n]), f"adamw_{n}")
        delta[n], new_m[n], new_v[n] = dl.reshape(shp), nm.reshape(shp), nv.reshape(shp)
    shapes = [params[n].shape for n in SMALL]
    res = _adamw(_pack([params[n] for n in SMALL]), _pack([grads[n] for n in SMALL]),
                 _pack([mom_m[n] for n in SMALL]), _pack([mom_v[n] for n in SMALL]), "adamw_small")
    for dst, packed_res in zip((delta, new_m, new_v), res):
        for n, a in zip(SMALL, _unpack(packed_res, shapes)):
            dst[n] = a

    for group in (grads, delta, new_m, new_v):
        group["w_in"] = jnp.swapaxes(group["w_in"], 1, 2)
    return (loss, dx[None], *[grads[n] for n in ORDER], *[delta[n] for n in ORDER],
            *[new_m[n] for n in ORDER], *[new_v[n] for n in ORDER])
```

```python
import functools
import math

import numpy as np
import jax
import jax.numpy as jnp
from jax import lax
from jax.experimental import pallas as pl
from jax.experimental.pallas import tpu as pltpu

F32 = jnp.float32
BF16 = jnp.bfloat16

N_DEV = 8
DEPTH = 4
HGRN_HEAD_DIM = 128
HGRN_CHUNK = 64
ATTN_HEAD_DIM = 64
ATTN_KV_HEADS = 4
KV_WIDTH = ATTN_KV_HEADS * ATTN_HEAD_DIM
WINDOW = 128
WINDOW_SHIFT = 7
N_BUCKETS = 32
MAX_DISTANCE = 128
ALPHA = (2.0 * DEPTH) ** 0.25
LN_EPS = 1e-5
RMS_EPS = 1e-6
ADAM_LR = 0.001
ADAM_B1 = 0.9
ADAM_B2 = 0.999
ADAM_EPS = 1e-08
ADAM_WD = 0.01
ADAM_STEP = 10

LANES = 128
SUBLANES = 8
VMEM_LIMIT = 56 << 20
NEG_INF = float("-inf")


def _offsets(d_model):
    w = d_model // 2
    sizes = (w, w, w, w, w, KV_WIDTH, KV_WIDTH, w, w, w, w, w, d_model, d_model, d_model)
    names = ("a_q", "a_f", "a_i", "a_g", "b_q", "b_k", "b_v", "b_g", "c_b", "c_c", "c_x", "c_g", "m_a", "m_b", "m_c")
    off, o = {}, 0
    for n, s in zip(names, sizes):
        off[n] = o
        o += s
    return off, o


def _tile(n, pref):
    t = min(pref, n)
    while n % t:
        t //= 2
    return t


def _cp(sem=None, vmem=VMEM_LIMIT):
    return pltpu.CompilerParams(dimension_semantics=sem, vmem_limit_bytes=vmem)


def _sigmoid(x):
    return 1.0 / (1.0 + jnp.exp(-x))


def _dot_nn(a, b):
    return jnp.dot(a, b, preferred_element_type=F32)


def _dot_nt(a, b):
    return lax.dot_general(a, b, (((1,), (1,)), ((), ())), preferred_element_type=F32)


def _dot_tn(a, b):
    return lax.dot_general(a, b, (((0,), (0,)), ((), ())), preferred_element_type=F32)


def _dep_specs(dep):
    return ([], []) if dep is None else ([pl.BlockSpec(memory_space=pl.ANY)], [dep])


def _mm_nn(a, b, name, out_dtype=F32, tm=1024, tn=1536, dep=None):
    m, k = a.shape
    _, n = b.shape
    tm, tn = _tile(m, tm), _tile(n, tn)
    dep_specs, dep_args = _dep_specs(dep)

    def body(a_ref, b_ref, *rest):
        o_ref = rest[-1]
        o_ref[...] = _dot_nn(a_ref[...], b_ref[...]).astype(o_ref.dtype)

    return pl.pallas_call(
        body, grid=(n // tn, m // tm),
        in_specs=[pl.BlockSpec((tm, k), lambda j, i: (i, 0)), pl.BlockSpec((k, tn), lambda j, i: (0, j))] + dep_specs,
        out_specs=pl.BlockSpec((tm, tn), lambda j, i: (i, j)),
        out_shape=jax.ShapeDtypeStruct((m, n), out_dtype), name=name,
        compiler_params=_cp(("parallel", "parallel")))(a, b, *dep_args)


def _mm_nt(a, b, name, tm=1024, tk=1536, add=None, add_scale=1.0, dep=None):
    m, k = a.shape
    n, _ = b.shape
    tm, tk = _tile(m, tm), _tile(k, tk)
    has_add = add is not None
    dep_specs, dep_args = _dep_specs(dep)

    def body(*refs):
        if has_add:
            a_ref, b_ref, add_ref = refs[:3]
        else:
            a_ref, b_ref = refs[:2]
        o_ref = refs[-1]
        if k == tk:
            prod = _dot_nt(a_ref[...], b_ref[...])
            o_ref[...] = prod + add_ref[...] * add_scale if has_add else prod
            return

        @pl.when(pl.program_id(1) == 0)
        def _():
            if has_add:
                o_ref[...] = add_ref[...] * add_scale
            else:
                o_ref[...] = jnp.zeros_like(o_ref)

        o_ref[...] += _dot_nt(a_ref[...], b_ref[...])

    in_specs = [pl.BlockSpec((tm, tk), lambda i, kk: (i, kk)), pl.BlockSpec((n, tk), lambda i, kk: (0, kk))]
    args = [a, b]
    if has_add:
        in_specs.append(pl.BlockSpec((tm, n), lambda i, kk: (i, 0)))
        args.append(add)
    in_specs += dep_specs
    args += dep_args
    return pl.pallas_call(
        body, grid=(m // tm, k // tk), in_specs=in_specs,
        out_specs=pl.BlockSpec((tm, n), lambda i, kk: (i, 0)),
        out_shape=jax.ShapeDtypeStruct((m, n), F32), name=name,
        compiler_params=_cp(("parallel", "arbitrary")))(*args)


def _mm_nt_cols(a, b, name, tm=1024, tn=1536, dep=None):
    m, k = a.shape
    n, _ = b.shape
    tm, tn = _tile(m, tm), _tile(n, tn)
    dep_specs, dep_args = _dep_specs(dep)

    def body(a_ref, b_ref, *rest):
        rest[-1][...] = _dot_nt(a_ref[...], b_ref[...])

    return pl.pallas_call(
        body, grid=(n // tn, m // tm),
        in_specs=[pl.BlockSpec((tm, k), lambda j, i: (i, 0)), pl.BlockSpec((tn, k), lambda j, i: (j, 0))] + dep_specs,
        out_specs=pl.BlockSpec((tm, tn), lambda j, i: (i, j)),
        out_shape=jax.ShapeDtypeStruct((m, n), F32), name=name,
        compiler_params=_cp(("parallel", "parallel")))(a, b, *dep_args)


def _mm_nn_acc(a, b, name, tm=1024, tk=1536, add=None, add_scale=1.0, dep=None):
    m, k = a.shape
    _, n = b.shape
    tm, tk = _tile(m, tm), _tile(k, tk)
    dep_specs, dep_args = _dep_specs(dep)

    def body(a_ref, b_ref, add_ref, *rest):
        o_ref = rest[-1]

        @pl.when(pl.program_id(1) == 0)
        def _():
            o_ref[...] = add_ref[...] * add_scale

        o_ref[...] += _dot_nn(a_ref[...], b_ref[...])

    return pl.pallas_call(
        body, grid=(m // tm, k // tk),
        in_specs=[pl.BlockSpec((tm, tk), lambda i, kk: (i, kk)), pl.BlockSpec((tk, n), lambda i, kk: (kk, 0)),
                  pl.BlockSpec((tm, n), lambda i, kk: (i, 0))] + dep_specs,
        out_specs=pl.BlockSpec((tm, n), lambda i, kk: (i, 0)),
        out_shape=jax.ShapeDtypeStruct((m, n), F32), name=name,
        compiler_params=_cp(("parallel", "arbitrary")))(a, b, add, *dep_args)


def _mm_tn_rows(a, b, name, tt=1024, tr=1536):
    t, k = a.shape
    _, n = b.shape
    tt, tr = _tile(t, tt), _tile(k, tr)

    def body(a_ref, b_ref, o_ref):
        @pl.when(pl.program_id(1) == 0)
        def _():
            o_ref[...] = jnp.zeros_like(o_ref)

        o_ref[...] += _dot_tn(a_ref[...], b_ref[...])

    return pl.pallas_call(
        body, grid=(k // tr, t // tt),
        in_specs=[pl.BlockSpec((tt, tr), lambda j, s: (s, j)), pl.BlockSpec((tt, n), lambda j, s: (s, 0))],
        out_specs=pl.BlockSpec((tr, n), lambda j, s: (j, 0)),
        out_shape=jax.ShapeDtypeStruct((k, n), F32), name=name,
        compiler_params=_cp(("parallel", "arbitrary")))(a, b)


def _mm_tn(a, b, name, tt=1024, tn=1536):
    t, k = a.shape
    _, n = b.shape
    tt, tn = _tile(t, tt), _tile(n, tn)

    def body(a_ref, b_ref, o_ref):
        @pl.when(pl.program_id(1) == 0)
        def _():
            o_ref[...] = jnp.zeros_like(o_ref)

        o_ref[...] += _dot_tn(a_ref[...], b_ref[...])

    return pl.pallas_call(
        body, grid=(n // tn, t // tt),
        in_specs=[pl.BlockSpec((tt, k), lambda j, s: (s, 0)), pl.BlockSpec((tt, tn), lambda j, s: (s, j))],
        out_specs=pl.BlockSpec((k, tn), lambda j, s: (0, j)),
        out_shape=jax.ShapeDtypeStruct((k, n), F32), name=name,
        compiler_params=_cp(("parallel", "arbitrary")))(a, b)


def _ew(body, name, t, ncol, wb, ins, outs, accs=(), tt=512):
    tt = _tile(t, tt)
    nt = t // tt
    in_specs, args = [], []
    for arr, kind, coff in ins:
        if kind == "tile":
            spec = pl.BlockSpec((tt, wb), lambda j, i, c=coff: (i, c + j))
        elif kind == "prev":
            spec = pl.BlockSpec((tt, wb), lambda j, i, c=coff: (jnp.maximum(i - 1, 0), c + j))
        elif kind == "next":
            spec = pl.BlockSpec((tt, wb), lambda j, i, c=coff: (jnp.minimum(i + 1, nt - 1), c + j))
        else:
            spec = pl.BlockSpec((arr.shape[0], wb), lambda j, i, c=coff: (0, c + j))
        in_specs.append(spec)
        args.append(arr)
    out_specs = [pl.BlockSpec((tt, wb), lambda j, i: (i, j)) for _ in outs]
    out_shape = [jax.ShapeDtypeStruct((t, ncol * wb), d) for d in outs]
    for r in accs:
        out_specs.append(pl.BlockSpec((r, wb), lambda j, i: (0, j)))
        out_shape.append(jax.ShapeDtypeStruct((r, ncol * wb), F32))

    def kern(*refs):
        body(pl.program_id(1), nt, *refs)

    res = pl.pallas_call(
        kern, grid=(ncol, nt), in_specs=in_specs, out_specs=out_specs, out_shape=out_shape, name=name,
        compiler_params=_cp(("parallel", "arbitrary")))(*args)
    return res


def _silu_parts(x):
    s = _sigmoid(x)
    return x * s, s + x * s * (1.0 - s)


def _out_proj_ln(a, w, x, g, b, name):
    t, d = x.shape
    k = a.shape[1]
    tt = _tile(t, 256)

    def body(a_ref, w_ref, x_ref, g_ref, b_ref, o_ref, ob_ref, xh_ref, r_ref):
        z = ALPHA * x_ref[...] + _dot_nn(a_ref[...], w_ref[...])
        mu = jnp.mean(z, axis=1, keepdims=True)
        zc = z - mu
        var = jnp.mean(zc * zc, axis=1, keepdims=True)
        rstd = lax.rsqrt(var + LN_EPS)
        xh = zc * rstd
        o = xh * g_ref[...] + b_ref[...]
        o_ref[...] = o
        ob_ref[...] = o.astype(BF16)
        xh_ref[...] = xh
        r_ref[...] = rstd

    row = pl.BlockSpec((tt, d), lambda i: (i, 0))
    vec = pl.BlockSpec((1, d), lambda i: (0, 0))
    return pl.pallas_call(
        body, grid=(t // tt,),
        in_specs=[pl.BlockSpec((tt, k), lambda i: (i, 0)), pl.BlockSpec((k, d), lambda i: (0, 0)), row, vec, vec],
        out_specs=[row, row, row, pl.BlockSpec((tt, 1), lambda i: (i, 0))],
        out_shape=[jax.ShapeDtypeStruct((t, d), F32), jax.ShapeDtypeStruct((t, d), BF16),
                   jax.ShapeDtypeStruct((t, d), F32), jax.ShapeDtypeStruct((t, 1), F32)],
        name=name, compiler_params=_cp(("parallel",)))(a, w, x, g, b)


def _ln_bwd_out_proj_merge(dout, xhat, rstd, g, w, u, ya, yb, yc, off, name):
    t, d = dout.shape
    gb = GATE_BLOCK
    nb = d // gb
    tt = _tile(t, 128)
    cols = [off[nme] // gb for nme in ("m_a", "m_b", "m_c")]

    def body(do_ref, xh_ref, r_ref, g_ref, w_ref, ya_ref, yb_ref, yc_ref, *rest):
        m_refs = rest[:3 * nb]
        dz_ref, dzb_ref, acc_ref = rest[3 * nb:3 * nb + 3]
        dy_refs, dg_refs = rest[3 * nb + 3:3 * nb + 6], rest[3 * nb + 6:]

        @pl.when(pl.program_id(0) == 0)
        def _():
            acc_ref[...] = jnp.zeros_like(acc_ref)

        do = do_ref[...]
        xh = xh_ref[...]
        dxh = do * g_ref[...]
        m1 = jnp.mean(dxh, axis=1, keepdims=True)
        m2 = jnp.mean(dxh * xh, axis=1, keepdims=True)
        dz = r_ref[...] * (dxh - m1 - xh * m2)
        dzb = dz.astype(BF16)
        dz_ref[...] = dz
        dzb_ref[...] = dzb
        acc_ref[0:1, :] += jnp.sum(do * xh, axis=0, keepdims=True)
        acc_ref[1:2, :] += jnp.sum(do, axis=0, keepdims=True)
        dm = _dot_nt(dzb, w_ref[...])
        for gate, y_ref in enumerate((ya_ref, yb_ref, yc_ref)):
            for j in range(nb):
                sl = slice(j * gb, (j + 1) * gb)
                s = _sigmoid(m_refs[gate * nb + j][...])
                dm_j = dm[:, sl]
                dy_refs[gate][:, sl] = (dm_j * s).astype(BF16)
                dg_refs[gate][:, sl] = (dm_j * y_ref[:, sl] * s * (1.0 - s)).astype(BF16)

    row = pl.BlockSpec((tt, d), lambda i: (i, 0))
    gates = [pl.BlockSpec((tt, gb), lambda i, c=c0 + j: (i, c)) for c0 in cols for j in range(nb)]
    return pl.pallas_call(
        body, grid=(t // tt,),
        in_specs=[row, row, pl.BlockSpec((tt, 1), lambda i: (i, 0)), pl.BlockSpec((1, d), lambda i: (0, 0)),
                  pl.BlockSpec((d, d), lambda i: (0, 0)), row, row, row] + gates,
        out_specs=[row, row, pl.BlockSpec((SUBLANES, d), lambda i: (0, 0))] + [row] * 6,
        out_shape=[jax.ShapeDtypeStruct((t, d), F32), jax.ShapeDtypeStruct((t, d), BF16),
                   jax.ShapeDtypeStruct((SUBLANES, d), F32)] + [jax.ShapeDtypeStruct((t, d), BF16)] * 6,
        name=name, compiler_params=_cp(("arbitrary",)))(dout, xhat, rstd, g, w, ya, yb, yc, *([u] * (3 * nb)))


def _loss_head(y, target):
    t, d = y.shape
    tt = _tile(t, 256)

    def body(y_ref, t_ref, acc_ref, dy_ref):
        @pl.when(pl.program_id(0) == 0)
        def _():
            acc_ref[...] = jnp.zeros_like(acc_ref)

        err = y_ref[...] - t_ref[...]
        dy_ref[...] = err * (1.0 / d)
        acc_ref[0:1, :] += jnp.sum(err * err, axis=0, keepdims=True)

    row = pl.BlockSpec((tt, d), lambda i: (i, 0))
    acc, dy = pl.pallas_call(
        body, grid=(t // tt,), in_specs=[row, row],
        out_specs=[pl.BlockSpec((SUBLANES, d), lambda i: (0, 0)), row],
        out_shape=[jax.ShapeDtypeStruct((SUBLANES, d), F32), jax.ShapeDtypeStruct((t, d), F32)],
        name="loss_head", compiler_params=_cp(("arbitrary",)))(y, target)
    return acc, dy


def _tri(lower):
    r = lax.broadcasted_iota(jnp.int32, (HGRN_CHUNK, HGRN_CHUNK), 0)
    c = lax.broadcasted_iota(jnp.int32, (HGRN_CHUNK, HGRN_CHUNK), 1)
    return jnp.where((r >= c) if lower else (r <= c), 1.0, 0.0).astype(BF16)


def _exact_tri_matmul(tri, x):
    hi = x.astype(BF16)
    r1 = x - hi.astype(F32)
    mid = r1.astype(BF16)
    lo = (r1 - mid.astype(F32)).astype(BF16)
    return _dot_nn(tri, hi) + _dot_nn(tri, mid) + _dot_nn(tri, lo)


def _hgrn_gates(q_raw, fl, lb):
    sq = _sigmoid(q_raw)
    qf = q_raw * sq * (HGRN_HEAD_DIM ** -0.5)
    sg = _sigmoid(fl)
    f = lb + (1.0 - lb) * sg
    return qf, sq, sg, f


HGRN_SUB = 16
HGRN_NSUB = HGRN_CHUNK // HGRN_SUB
HGRN_HEADS_PER_STEP = 8
HGRN_HEADS_PER_STEP_BWD = 8


def _diag_rows(r):
    return (r // SUBLANES) * SUBLANES


def _heads(x):
    hd = HGRN_HEAD_DIM
    return [x[:, i * hd:(i + 1) * hd] for i in range(x.shape[1] // hd)]


def _per_head(fn, *xs):
    split = [x if isinstance(x, (list, tuple)) else _heads(x) for x in xs]
    return jnp.concatenate([fn(*hs) for hs in zip(*split)], axis=1)


def _head_lane_sum(x):
    return _per_head(lambda h: jnp.broadcast_to(jnp.sum(h, axis=1, keepdims=True), h.shape), x)


def _hgrn_intra_fwd(qf, k, v, b):
    ch, sub, wd = HGRN_CHUNK, HGRN_SUB, qf.shape[1]
    tl = lax.broadcasted_iota(jnp.int32, (sub, wd), 0)
    blocks = []
    for m in range(HGRN_NSUB):
        rs = slice(m * sub, (m + 1) * sub)
        bm, qm, km, vm = b[rs], qf[rs], k[rs], v[rs]
        parts = {0: jnp.zeros((sub, wd), F32), SUBLANES: jnp.zeros((sub - SUBLANES, wd), F32)}
        for r in range(sub):
            lo = _diag_rows(r)
            e = jnp.exp(jnp.where(tl[lo:] >= r, bm[lo:] - bm[r:r + 1], NEG_INF))
            parts[lo] = parts[lo] + _head_lane_sum(qm[lo:] * e * km[r:r + 1]) * vm[r:r + 1]
        blocks.append(parts[0] + jnp.concatenate([jnp.zeros((SUBLANES, wd), F32), parts[SUBLANES]], axis=0))
    acc = jnp.concatenate(blocks, axis=0)
    for j in range(HGRN_NSUB - 1):
        lo = sub * (j + 1)
        c = b[lo - 1:lo, :]
        qj = (qf[lo:] * jnp.exp(b[lo:] - c)).astype(BF16)
        kj = (k[lo - sub:lo] * jnp.exp(c - b[lo - sub:lo])).astype(BF16)
        vj = v[lo - sub:lo].astype(BF16)
        contrib = _per_head(lambda q_, k_, v_: _dot_nn(_dot_nt(q_, k_).astype(BF16), v_), qj, kj, vj)
        acc = acc + jnp.concatenate([jnp.zeros((lo, wd), F32), contrib], axis=0)
    return acc


def _hgrn_intra_bwd(qf, k, v, b, do_v):
    ch, sub, wd = HGRN_CHUNK, HGRN_SUB, qf.shape[1]
    tl = lax.broadcasted_iota(jnp.int32, (sub, wd), 0)
    dq_blocks, dk_blocks, dv_blocks = [], [], []
    for m in range(HGRN_NSUB):
        rs = slice(m * sub, (m + 1) * sub)
        bm, qm, km, vm, dom = b[rs], qf[rs], k[rs], v[rs], do_v[rs]
        parts = {0: jnp.zeros((sub, wd), F32), SUBLANES: jnp.zeros((sub - SUBLANES, wd), F32)}
        dk_parts = {sub: jnp.zeros((sub, wd), F32), SUBLANES: jnp.zeros((SUBLANES, wd), F32)}
        dv_parts = {sub: jnp.zeros((sub, wd), F32), SUBLANES: jnp.zeros((SUBLANES, wd), F32)}
        for r in range(sub):
            lo = _diag_rows(r)
            b_r, k_r, v_r, q_r, do_r = bm[r:r + 1], km[r:r + 1], vm[r:r + 1], qm[r:r + 1], dom[r:r + 1]
            e = jnp.exp(jnp.where(tl[lo:] >= r, bm[lo:] - b_r, NEG_INF))
            parts[lo] = parts[lo] + _head_lane_sum(dom[lo:] * v_r) * (k_r * e)
            hi = lo + SUBLANES
            e2 = jnp.exp(jnp.where(tl[:hi] <= r, b_r - bm[:hi], NEG_INF))
            qe2 = q_r * e2
            dk_parts[hi] = dk_parts[hi] + _head_lane_sum(vm[:hi] * do_r) * qe2
            dv_parts[hi] = dv_parts[hi] + _head_lane_sum(km[:hi] * qe2) * do_r
        pad = jnp.zeros((SUBLANES, wd), F32)
        dq_blocks.append(parts[0] + jnp.concatenate([pad, parts[SUBLANES]], axis=0))
        dk_blocks.append(dk_parts[sub] + jnp.concatenate([dk_parts[SUBLANES], pad], axis=0))
        dv_blocks.append(dv_parts[sub] + jnp.concatenate([dv_parts[SUBLANES], pad], axis=0))
    dq = jnp.concatenate(dq_blocks, axis=0)
    dk = jnp.concatenate(dk_blocks, axis=0)
    dv = jnp.concatenate(dv_blocks, axis=0)
    do_b, v_b = do_v.astype(BF16), v.astype(BF16)
    dk_off, dv_off = [], []
    for j in range(HGRN_NSUB - 1):
        lo = sub * (j + 1)
        c = b[lo - 1:lo, :]
        eq = jnp.exp(b[lo:] - c)
        ek = jnp.exp(c - b[lo - sub:lo])
        qj = (qf[lo:] * eq).astype(BF16)
        kj = (k[lo - sub:lo] * ek).astype(BF16)
        doj, vj = do_b[lo:], v_b[lo - sub:lo]
        dq_j = _per_head(lambda do_, v_, k_: _dot_nn(_dot_nt(do_, v_).astype(BF16), k_), doj, vj, kj)
        dk_j = _per_head(lambda do_, v_, q_: _dot_nn(_dot_nt(v_, do_).astype(BF16), q_), doj, vj, qj)
        dv_j = _per_head(lambda do_, k_, q_: _dot_nn(_dot_nt(k_, q_).astype(BF16), do_), doj, kj, qj)
        dq = dq + jnp.concatenate([jnp.zeros((lo, wd), F32), dq_j * eq], axis=0)
        dk_off.append(dk_j * ek)
        dv_off.append(dv_j)
    zero = jnp.zeros((sub, wd), F32)
    dk = dk + jnp.concatenate(dk_off + [zero], axis=0)
    dv = dv + jnp.concatenate(dv_off + [zero], axis=0)
    return dq, dk, dv


def _head_rms(o):
    return lax.rsqrt(_head_lane_sum(o * o) * (1.0 / HGRN_HEAD_DIM) + RMS_EPS)


def _hgrn_fwd(u, lb, gain, off, name):
    t = u.shape[0]
    w = lb.shape[1]
    hd, ch, hp = HGRN_HEAD_DIM, HGRN_CHUNK, HGRN_HEADS_PER_STEP
    nh, nc = w // hd, t // ch
    wb = hp * hd
    cq, cf, cv, cg = off["a_q"] // wb, off["a_f"] // wb, off["a_i"] // wb, off["a_g"] // wb

    def body(q_ref, f_ref, v_ref, g_ref, lb_ref, gain_ref, o_ref, st_ref, p_ref, state):
        @pl.when(pl.program_id(1) == 0)
        def _():
            state[...] = jnp.zeros_like(state)

        sts = [state[i] for i in range(hp)]
        qf, _, _, f = _hgrn_gates(q_ref[...], f_ref[...], lb_ref[...])
        k = 1.0 - f
        v = v_ref[...]
        b = _exact_tri_matmul(_tri(True), jnp.log(f))
        inter = _per_head(lambda qa_, st_: _dot_nt(qa_, st_.astype(BF16)), (qf * jnp.exp(b)).astype(BF16), sts)
        o = inter + _hgrn_intra_fwd(qf, k, v, b)
        o_ref[...] = o
        silu, _ = _silu_parts(g_ref[...])
        p_ref[...] = (o * _head_rms(o) * gain_ref[...] * silu).astype(BF16)
        b_end = b[ch - 1:ch, :]
        a_end = _heads(jnp.exp(b_end))
        kd = _heads((k * jnp.exp(b_end - b)).astype(BF16))
        v_b = _heads(v.astype(BF16))
        for i in range(hp):
            st_ref[i, 0] = sts[i]
            state[i] = sts[i] * a_end[i] + _dot_tn(v_b[i], kd[i])

    return pl.pallas_call(
        body, grid=(nh // hp, nc),
        in_specs=[pl.BlockSpec((ch, wb), lambda h, n: (n, cq + h)),
                  pl.BlockSpec((ch, wb), lambda h, n: (n, cf + h)),
                  pl.BlockSpec((ch, wb), lambda h, n: (n, cv + h)),
                  pl.BlockSpec((ch, wb), lambda h, n: (n, cg + h)),
                  pl.BlockSpec((1, wb), lambda h, n: (0, h)),
                  pl.BlockSpec((1, wb), lambda h, n: (0, h))],
        out_specs=[pl.BlockSpec((ch, wb), lambda h, n: (n, h)),
                   pl.BlockSpec((hp, 1, hd, hd), lambda h, n: (h, n, 0, 0)),
                   pl.BlockSpec((ch, wb), lambda h, n: (n, h))],
        out_shape=[jax.ShapeDtypeStruct((t, w), F32), jax.ShapeDtypeStruct((nh, nc, hd, hd), F32),
                   jax.ShapeDtypeStruct((t, w), BF16)],
        scratch_shapes=[pltpu.VMEM((hp, hd, hd), F32)],
        name=name, compiler_params=_cp(("parallel", "arbitrary")))(u, u, u, u, lb, gain)


def _hgrn_bwd(u, lb, gain, states, o, dp, off, name):
    t = u.shape[0]
    w = lb.shape[1]
    hd, ch, hp = HGRN_HEAD_DIM, HGRN_CHUNK, HGRN_HEADS_PER_STEP_BWD
    nh, nc = w // hd, t // ch
    wb = hp * hd
    cq, cf, cv, cg = off["a_q"] // wb, off["a_f"] // wb, off["a_i"] // wb, off["a_g"] // wb

    def body(q_ref, f_ref, v_ref, g_ref, o_ref, dp_ref, st_ref, lb_ref, gain_ref,
             dq_ref, df_ref, dv_ref, dg_ref, dlb_ref, dstate):
        @pl.when(pl.program_id(1) == 0)
        def _():
            dstate[...] = jnp.zeros_like(dstate)
            dlb_ref[...] = jnp.zeros_like(dlb_ref)

        silu, dsilu = _silu_parts(g_ref[...])
        o_v, dp_v, gain_row = o_ref[...], dp_ref[...], gain_ref[...]
        rms = _head_rms(o_v)
        nrm = o_v * rms
        dg_ref[...] = (dp_v * nrm * gain_row * dsilu).astype(BF16)
        dlb_ref[1:2, :] += jnp.sum(dp_v * nrm * silu, axis=0, keepdims=True)
        dn = dp_v * gain_row * silu
        do_v = rms * (dn - nrm * (_head_lane_sum(dn * nrm) * (1.0 / HGRN_HEAD_DIM)))

        rows = lax.broadcasted_iota(jnp.int32, (ch, wb), 0)
        lb_row = lb_ref[...]
        q_raw = q_ref[...]
        qf, sq, sg, f = _hgrn_gates(q_raw, f_ref[...], lb_row)
        k = 1.0 - f
        b = _exact_tri_matmul(_tri(True), jnp.log(f))
        a = jnp.exp(b)
        b_end = b[ch - 1:ch, :]
        a_end = jnp.exp(b_end)
        to_end = jnp.exp(b_end - b)
        v = v_ref[...]
        st0 = [st_ref[i, 0] for i in range(hp)]
        ds = [dstate[i] for i in range(hp)]
        st0_b = [s_.astype(BF16) for s_ in st0]
        ds_b = [s_.astype(BF16) for s_ in ds]
        do_b, v_b, kd_b, qa_b = do_v.astype(BF16), v.astype(BF16), (k * to_end).astype(BF16), (qf * a).astype(BF16)

        dq_inter = a * _per_head(_dot_nn, do_b, st0_b)
        dk_end = to_end * _per_head(_dot_nn, v_b, ds_b)
        dv_end = _per_head(_dot_nt, kd_b, ds_b)
        a_end_h = _heads(a_end)
        st_end = [st0[i] * a_end_h[i] + _dot_tn(_heads(v_b)[i], _heads(kd_b)[i]) for i in range(hp)]
        db_end = jnp.concatenate([jnp.sum(ds[i] * st_end[i], axis=0, keepdims=True) for i in range(hp)], axis=1)
        ds_new = [ds[i] * a_end_h[i] + _dot_tn(_heads(do_b)[i], _heads(qa_b)[i]) for i in range(hp)]

        dq_intra, dk_intra, dv_intra = _hgrn_intra_bwd(qf, k, v, b, do_v)
        dqf = dq_inter + dq_intra
        dk = dk_end + dk_intra
        dv = dv_end + dv_intra
        db = qf * dqf - k * dk
        db = db + jnp.where(rows == ch - 1, db_end, 0.0)
        dg = _exact_tri_matmul(_tri(False), db)
        df = dg / f - dk
        for i in range(hp):
            dstate[i] = ds_new[i]
        dq_ref[...] = (dqf * (HGRN_HEAD_DIM ** -0.5) * (sq + q_raw * sq * (1.0 - sq))).astype(BF16)
        df_ref[...] = (df * (1.0 - lb_row) * sg * (1.0 - sg)).astype(BF16)
        dv_ref[...] = dv.astype(BF16)
        dlb_ref[0:1, :] += jnp.sum(df * (1.0 - sg), axis=0, keepdims=True)

    rev = lambda n: nc - 1 - n
    tile = lambda c: pl.BlockSpec((ch, wb), lambda h, n, c=c: (rev(n), c + h))
    return pl.pallas_call(
        body, grid=(nh // hp, nc),
        in_specs=[tile(cq), tile(cf), tile(cv), tile(cg), tile(0), tile(0),
                  pl.BlockSpec((hp, 1, hd, hd), lambda h, n: (h, rev(n), 0, 0)),
                  pl.BlockSpec((1, wb), lambda h, n: (0, h)), pl.BlockSpec((1, wb), lambda h, n: (0, h))],
        out_specs=[tile(0), tile(0), tile(0), tile(0), pl.BlockSpec((SUBLANES, wb), lambda h, n: (0, h))],
        out_shape=[jax.ShapeDtypeStruct((t, w), BF16)] * 4 + [jax.ShapeDtypeStruct((SUBLANES, w), F32)],
        scratch_shapes=[pltpu.VMEM((hp, hd, hd), F32)],
        name=name, compiler_params=_cp(("parallel", "arbitrary")))(u, u, u, u, o, dp, states, lb, gain)


def _bucket_map():
    i = np.arange(WINDOW)[:, None]
    j = np.arange(2 * WINDOW)[None, :]
    dist = np.clip(WINDOW + i - j, 0, WINDOW - 1)
    max_exact = N_BUCKETS // 2
    logd = (np.log(np.maximum(dist, 1).astype(np.float32) / max_exact) / math.log(MAX_DISTANCE / max_exact))
    large = np.minimum(max_exact + (logd.astype(np.float32) * (N_BUCKETS - max_exact)).astype(np.int32), N_BUCKETS - 1)
    return np.where(dist < max_exact, dist, large).astype(np.int32)


def _bias_table(rel_bias, n_heads):
    bucket = jnp.asarray(_bucket_map())

    def body(rb_ref, bk_ref, o_ref):
        bk = bk_ref[...]
        i = lax.broadcasted_iota(jnp.int32, (WINDOW, 2 * WINDOW), 0)
        j = lax.broadcasted_iota(jnp.int32, (WINDOW, 2 * WINDOW), 1)
        band = ((j >= WINDOW) & (j - WINDOW <= i)) | ((j < WINDOW) & (j > i))
        for h in range(n_heads):
            def step(bi, acc):
                return jnp.where(bk == bi, rb_ref[bi, h], acc)
            table = lax.fori_loop(0, N_BUCKETS, step, jnp.zeros((WINDOW, 2 * WINDOW), F32))
            o_ref[h] = jnp.where(band, table, NEG_INF)

    return pl.pallas_call(
        body, in_specs=[pl.BlockSpec(memory_space=pltpu.SMEM), pl.BlockSpec(memory_space=pltpu.VMEM)],
        out_specs=pl.BlockSpec(memory_space=pltpu.VMEM),
        out_shape=jax.ShapeDtypeStruct((n_heads, WINDOW, 2 * WINDOW), F32), name="bias_table",
        compiler_params=_cp())(rel_bias, bucket)


def _bias_grad(dbias, n_heads):
    bucket = jnp.asarray(_bucket_map())

    def body(db_ref, bk_ref, o_ref):
        bk = bk_ref[...]
        lane = lax.broadcasted_iota(jnp.int32, (1, LANES), 1)

        def step(bi, carry):
            row = jnp.zeros((1, LANES), F32)
            for h in range(n_heads):
                val = jnp.sum(jnp.where(bk == bi, db_ref[h], 0.0))
                row = jnp.where(lane == h, val, row)
            o_ref[pl.ds(bi, 1), :] = row
            return carry

        lax.fori_loop(0, N_BUCKETS, step, 0)

    return pl.pallas_call(
        body, in_specs=[pl.BlockSpec(memory_space=pltpu.VMEM), pl.BlockSpec(memory_space=pltpu.VMEM)],
        out_specs=pl.BlockSpec(memory_space=pltpu.VMEM),
        out_shape=jax.ShapeDtypeStruct((N_BUCKETS, LANES), F32), name="bias_grad",
        compiler_params=_cp())(dbias, bucket)


def _no_prev_block(n, grp):
    j = lax.broadcasted_iota(jnp.int32, (grp * WINDOW, 2 * WINDOW), 1)
    return (j < WINDOW) & (n == 0)


def _attn_probs(no_prev, q_ref, kp_ref, kc_ref, bias_ref, sink_ref, hh, grp):
    ad, wn = ATTN_HEAD_DIM, WINDOW
    ksl = slice(hh * ad, (hh + 1) * ad)
    kw = jnp.concatenate([kp_ref[:, ksl], kc_ref[:, ksl]], axis=0).astype(BF16)
    qs = jnp.concatenate([q_ref[:, (hh * grp + g) * ad:(hh * grp + g + 1) * ad] for g in range(grp)], axis=0).astype(BF16)
    s = _dot_nt(qs, kw) * (ad ** -0.5) + bias_ref[hh]
    s = jnp.where(no_prev, NEG_INF, s)
    rr = lax.broadcasted_iota(jnp.int32, (grp * wn, 1), 0) >> WINDOW_SHIFT
    sink = jnp.zeros((grp * wn, 1), F32)
    for g in range(grp):
        sink = jnp.where(rr == g, sink_ref[hh * grp + g], sink)
    m = jnp.maximum(jnp.max(s, axis=1, keepdims=True), sink)
    p = jnp.exp(s - m)
    es = jnp.exp(sink - m)
    inv = 1.0 / (jnp.sum(p, axis=1, keepdims=True) + es)
    return qs, kw, p * inv, es * inv


GATE_BLOCK = 512


def _attn_fwd(u, bias_g, sinks, off, w, name):
    t = u.shape[0]
    wn, ad, kvw, gb = WINDOW, ATTN_HEAD_DIM, KV_WIDTH, GATE_BLOCK
    grp = (w // ad) // ATTN_KV_HEADS
    nb = t // wn
    n_gb = w // gb
    cq, ck, cv, cg = off["b_q"] // w, off["b_k"] // kvw, off["b_v"] // kvw, off["b_g"] // gb

    def body(q_ref, kp_ref, kc_ref, vp_ref, vc_ref, bias_ref, sink_ref, *rest):
        g_refs, (o_ref, p_ref) = rest[:n_gb], rest[n_gb:]
        no_prev = _no_prev_block(pl.program_id(0), grp)
        for hh in range(ATTN_KV_HEADS):
            _, _, p, _ = _attn_probs(no_prev, q_ref, kp_ref, kc_ref, bias_ref, sink_ref, hh, grp)
            ksl = slice(hh * ad, (hh + 1) * ad)
            vw = jnp.concatenate([vp_ref[:, ksl], vc_ref[:, ksl]], axis=0).astype(BF16)
            o = _dot_nn(p.astype(BF16), vw)
            for g in range(grp):
                o_ref[:, (hh * grp + g) * ad:(hh * grp + g + 1) * ad] = o[g * wn:(g + 1) * wn]
        for i in range(n_gb):
            sl = slice(i * gb, (i + 1) * gb)
            silu, _ = _silu_parts(g_refs[i][...])
            p_ref[:, sl] = (o_ref[:, sl] * silu).astype(BF16)

    prev = lambda n: jnp.maximum(n - 1, 0)
    row = pl.BlockSpec((wn, w), lambda n: (n, 0))
    return pl.pallas_call(
        body, grid=(nb,),
        in_specs=[pl.BlockSpec((wn, w), lambda n: (n, cq)),
                  pl.BlockSpec((wn, kvw), lambda n: (prev(n), ck)), pl.BlockSpec((wn, kvw), lambda n: (n, ck)),
                  pl.BlockSpec((wn, kvw), lambda n: (prev(n), cv)), pl.BlockSpec((wn, kvw), lambda n: (n, cv)),
                  pl.BlockSpec((ATTN_KV_HEADS, grp * wn, 2 * wn), lambda n: (0, 0, 0)),
                  pl.BlockSpec(memory_space=pltpu.SMEM)]
        + [pl.BlockSpec((wn, gb), lambda n, i=i: (n, cg + i)) for i in range(n_gb)],
        out_specs=[row, row],
        out_shape=[jax.ShapeDtypeStruct((t, w), F32), jax.ShapeDtypeStruct((t, w), BF16)], name=name,
        compiler_params=_cp(("parallel",)))(u, u, u, u, u, bias_g, sinks, *([u] * n_gb))


def _attn_bwd(u, o, dp, bias_g, sinks, off, w, name):
    t = u.shape[0]
    wn, ad, kvw, gb = WINDOW, ATTN_HEAD_DIM, KV_WIDTH, GATE_BLOCK
    grp = (w // ad) // ATTN_KV_HEADS
    nb = t // wn
    n_gb = w // gb
    cq, ck, cv, cg = off["b_q"] // w, off["b_k"] // kvw, off["b_v"] // kvw, off["b_g"] // gb

    def body(q_ref, kp_ref, kc_ref, vp_ref, vc_ref, o_ref, dp_ref, bias_ref, sink_ref, *rest):
        g_refs = rest[:n_gb]
        dq_ref, dkv_ref, dbias_ref, dsink_ref, dg_ref, do_ref, carry = rest[n_gb:]
        n = pl.program_id(0)

        @pl.when(n == 0)
        def _():
            dbias_ref[...] = jnp.zeros_like(dbias_ref)
            dsink_ref[...] = jnp.zeros_like(dsink_ref)
            carry[...] = jnp.zeros_like(carry)

        @pl.when(n == nb)
        def _():
            dkv_ref[...] = carry[...].astype(BF16)

        @pl.when(n < nb)
        def _():
            block(n, q_ref, kp_ref, kc_ref, vp_ref, vc_ref, o_ref, dp_ref, bias_ref, sink_ref, g_refs,
                  dq_ref, dkv_ref, dbias_ref, dsink_ref, dg_ref, do_ref, carry)

    def block(n, q_ref, kp_ref, kc_ref, vp_ref, vc_ref, o_ref, dp_ref, bias_ref, sink_ref, g_refs,
              dq_ref, dkv_ref, dbias_ref, dsink_ref, dg_ref, do_ref, carry):
        for i in range(n_gb):
            sl = slice(i * gb, (i + 1) * gb)
            silu, dsilu = _silu_parts(g_refs[i][...])
            dp_v = dp_ref[:, sl]
            do_ref[:, sl] = dp_v * silu
            dg_ref[:, sl] = (dp_v * o_ref[:, sl] * dsilu).astype(BF16)

        lane = lax.broadcasted_iota(jnp.int32, (1, LANES), 1)
        rr = lax.broadcasted_iota(jnp.int32, (grp * wn, 1), 0) >> WINDOW_SHIFT
        dsink_row = jnp.zeros((1, LANES), F32)
        no_prev = _no_prev_block(n, grp)
        for hh in range(ATTN_KV_HEADS):
            qs, kw, p, psink = _attn_probs(no_prev, q_ref, kp_ref, kc_ref, bias_ref, sink_ref, hh, grp)
            ksl = slice(hh * ad, (hh + 1) * ad)
            vw = jnp.concatenate([vp_ref[:, ksl], vc_ref[:, ksl]], axis=0).astype(BF16)
            hs = [slice((hh * grp + g) * ad, (hh * grp + g + 1) * ad) for g in range(grp)]
            dos = jnp.concatenate([do_ref[:, sl] for sl in hs], axis=0)
            os_ = jnp.concatenate([o_ref[:, sl] for sl in hs], axis=0)
            delta = jnp.sum(dos * os_, axis=1, keepdims=True)
            dos_b = dos.astype(BF16)
            dp = _dot_nt(dos_b, vw)
            ds = p * (dp - delta)
            dbias_ref[hh] += ds
            sd = psink * delta
            for g in range(grp):
                val = -jnp.sum(jnp.where(rr == g, sd, 0.0))
                dsink_row = jnp.where(lane == hh * grp + g, val, dsink_row)
            ds_b = (ds * (ad ** -0.5)).astype(BF16)
            dq = _dot_nn(ds_b, kw)
            for g in range(grp):
                dq_ref[:, hs[g]] = dq[g * wn:(g + 1) * wn].astype(BF16)
            dkw = _dot_tn(ds_b, qs)
            dvw = _dot_tn(p.astype(BF16), dos_b)
            vsl = slice(kvw + hh * ad, kvw + (hh + 1) * ad)
            dkv_ref[:, ksl] = (carry[:, ksl] + dkw[:wn]).astype(BF16)
            dkv_ref[:, vsl] = (carry[:, vsl] + dvw[:wn]).astype(BF16)
            carry[:, ksl] = dkw[wn:]
            carry[:, vsl] = dvw[wn:]
        dsink_ref[0:1, :] += dsink_row

    cur = lambda n: jnp.minimum(n, nb - 1)
    prev = lambda n: jnp.maximum(cur(n) - 1, 0)
    row = pl.BlockSpec((wn, w), lambda n: (cur(n), 0))
    return pl.pallas_call(
        body, grid=(nb + 1,),
        in_specs=[pl.BlockSpec((wn, w), lambda n: (cur(n), cq)),
                  pl.BlockSpec((wn, kvw), lambda n: (prev(n), ck)), pl.BlockSpec((wn, kvw), lambda n: (cur(n), ck)),
                  pl.BlockSpec((wn, kvw), lambda n: (prev(n), cv)), pl.BlockSpec((wn, kvw), lambda n: (cur(n), cv)),
                  row, row,
                  pl.BlockSpec((ATTN_KV_HEADS, grp * wn, 2 * wn), lambda n: (0, 0, 0)),
                  pl.BlockSpec(memory_space=pltpu.SMEM)]
        + [pl.BlockSpec((wn, gb), lambda n, i=i: (cur(n), cg + i)) for i in range(n_gb)],
        out_specs=[row, pl.BlockSpec((wn, 2 * kvw), lambda n: (jnp.maximum(n - 1, 0), 0)),
                   pl.BlockSpec((ATTN_KV_HEADS, grp * wn, 2 * wn), lambda n: (0, 0, 0)),
                   pl.BlockSpec((SUBLANES, LANES), lambda n: (0, 0)), row],
        out_shape=[jax.ShapeDtypeStruct((t, w), BF16), jax.ShapeDtypeStruct((t, 2 * kvw), BF16),
                   jax.ShapeDtypeStruct((ATTN_KV_HEADS, grp * wn, 2 * wn), F32), jax.ShapeDtypeStruct((SUBLANES, LANES), F32),
                   jax.ShapeDtypeStruct((t, w), BF16)],
        scratch_shapes=[pltpu.VMEM((wn, w), F32), pltpu.VMEM((wn, 2 * kvw), F32)],
        name=name, compiler_params=_cp(("arbitrary",)))(u, u, u, u, u, o, dp, bias_g, sinks, *([u] * n_gb))


def _shift_down(h, tail, k, rows):
    tt = h.shape[0]
    out = pltpu.roll(h, k, 0)
    for r in range(k):
        out = jnp.where(rows == r, tail[tt - k + r:tt - k + r + 1, :], out)
    return out


def _shift_up(h, head, k, rows):
    tt = h.shape[0]
    out = pltpu.roll(h, tt - k, 0)
    for r in range(k):
        out = jnp.where(rows == tt - k + r, head[r:r + 1, :], out)
    return out


def _conv_fwd(u, conv_w, off, w, name):
    t = u.shape[0]
    wb = 512
    c = lambda nme: off[nme] // wb

    def body(i, nt, cb_ref, cc_ref, ccp_ref, cx_ref, cxp_ref, cg_ref, w_ref, p_ref):
        h = cc_ref[...] * cx_ref[...]
        hp = jnp.where(i > 0, ccp_ref[...] * cxp_ref[...], 0.0)
        rows = lax.broadcasted_iota(jnp.int32, h.shape, 0)
        y = w_ref[0:1, :] * _shift_down(h, hp, 2, rows) + w_ref[1:2, :] * _shift_down(h, hp, 1, rows) + w_ref[2:3, :] * h
        silu, _ = _silu_parts(cg_ref[...])
        p_ref[...] = (cb_ref[...] * y * silu).astype(BF16)

    return _ew(body, name, t, w // wb, wb,
               [(u, "tile", c("c_b")), (u, "tile", c("c_c")), (u, "prev", c("c_c")), (u, "tile", c("c_x")),
                (u, "prev", c("c_x")), (u, "tile", c("c_g")), (conv_w, "row", 0)], [BF16])[0]


def _conv_bwd(dp, u, conv_w, off, w, name):
    t = u.shape[0]
    wb = 512
    c = lambda nme: off[nme] // wb

    def body(i, nt, dp_ref, dpn_ref, cb_ref, cbn_ref, cg_ref, cgn_ref, cc_ref, ccp_ref, cx_ref, cxp_ref, w_ref,
             dcb_ref, dcc_ref, dcx_ref, dcg_ref, acc_ref):
        @pl.when(i == 0)
        def _():
            acc_ref[...] = jnp.zeros_like(acc_ref)

        cc, cx, cb = cc_ref[...], cx_ref[...], cb_ref[...]
        h = cc * cx
        hp = jnp.where(i > 0, ccp_ref[...] * cxp_ref[...], 0.0)
        rows = lax.broadcasted_iota(jnp.int32, h.shape, 0)
        h1 = _shift_down(h, hp, 1, rows)
        h2 = _shift_down(h, hp, 2, rows)
        w0, w1, w2 = w_ref[0:1, :], w_ref[1:2, :], w_ref[2:3, :]
        y = w0 * h2 + w1 * h1 + w2 * h
        silu, dsilu = _silu_parts(cg_ref[...])
        dp_v = dp_ref[...]
        dcg_ref[...] = (dp_v * cb * y * dsilu).astype(BF16)
        dcb_ref[...] = (dp_v * y * silu).astype(BF16)
        dy = dp_v * cb * silu
        silu_n, _ = _silu_parts(cgn_ref[...])
        dyn = jnp.where(i < nt - 1, dpn_ref[...] * cbn_ref[...] * silu_n, 0.0)
        dh = w2 * dy + w1 * _shift_up(dy, dyn, 1, rows) + w0 * _shift_up(dy, dyn, 2, rows)
        dcc_ref[...] = (dh * cx).astype(BF16)
        dcx_ref[...] = (dh * cc).astype(BF16)
        acc_ref[0:1, :] += jnp.sum(dy * h2, axis=0, keepdims=True)
        acc_ref[1:2, :] += jnp.sum(dy * h1, axis=0, keepdims=True)
        acc_ref[2:3, :] += jnp.sum(dy * h, axis=0, keepdims=True)

    return _ew(body, name, t, w // wb, wb,
               [(dp, "tile", 0), (dp, "next", 0), (u, "tile", c("c_b")), (u, "next", c("c_b")),
                (u, "tile", c("c_g")), (u, "next", c("c_g")), (u, "tile", c("c_c")), (u, "prev", c("c_c")),
                (u, "tile", c("c_x")), (u, "prev", c("c_x")), (conv_w, "row", 0)],
               [BF16] * 4, accs=[SUBLANES])


def _proj_merge(p, w, ya, yb, u, off, name):
    t, k = p.shape
    d = w.shape[1]
    gb = GATE_BLOCK
    nb = d // gb
    tt = _tile(t, 256)
    cols = [off[nme] // gb for nme in ("m_a", "m_b", "m_c")]

    def body(p_ref, w_ref, ya_ref, yb_ref, *rest):
        m_refs, (yc_ref, mg_ref) = rest[:3 * nb], rest[3 * nb:]
        yc = _dot_nn(p_ref[...], w_ref[...])
        yc_ref[...] = yc
        for j in range(nb):
            sl = slice(j * gb, (j + 1) * gb)
            mg_ref[:, sl] = (_sigmoid(m_refs[j][...]) * ya_ref[:, sl] + _sigmoid(m_refs[nb + j][...]) * yb_ref[:, sl]
                             + _sigmoid(m_refs[2 * nb + j][...]) * yc[:, sl]).astype(BF16)

    row = pl.BlockSpec((tt, d), lambda i: (i, 0))
    gates = [pl.BlockSpec((tt, gb), lambda i, c=c0 + j: (i, c)) for c0 in cols for j in range(nb)]
    return pl.pallas_call(
        body, grid=(t // tt,),
        in_specs=[pl.BlockSpec((tt, k), lambda i: (i, 0)), pl.BlockSpec((k, d), lambda i: (0, 0)), row, row] + gates,
        out_specs=[row, row],
        out_shape=[jax.ShapeDtypeStruct((t, d), F32), jax.ShapeDtypeStruct((t, d), BF16)],
        name=name, compiler_params=_cp(("parallel",)))(p, w, ya, yb, *([u] * (3 * nb)))


def _lower_bounds(lb_param):
    def body(p_ref, o_ref):
        p = p_ref[...]
        e = jnp.exp(p - jnp.max(p, axis=0, keepdims=True))
        soft = e / jnp.sum(e, axis=0, keepdims=True)
        acc = jnp.zeros_like(soft[0:1])
        o_ref[0:1, :] = acc
        for l in range(1, DEPTH):
            acc = acc + soft[l:l + 1]
            o_ref[l:l + 1, :] = acc

    return pl.pallas_call(body, out_shape=jax.ShapeDtypeStruct(lb_param.shape, F32), name="lower_bounds",
                          compiler_params=_cp())(lb_param)


def _lower_bounds_bwd(lb_param, dlower):
    def body(p_ref, d_ref, o_ref):
        p = p_ref[...]
        e = jnp.exp(p - jnp.max(p, axis=0, keepdims=True))
        soft = e / jnp.sum(e, axis=0, keepdims=True)
        dl = d_ref[...]
        ds = [jnp.zeros_like(dl[0:1])]
        for j in range(1, DEPTH):
            acc = dl[j:j + 1]
            for l in range(j + 1, DEPTH):
                acc = acc + dl[l:l + 1]
            ds.append(acc)
        inner = ds[0] * soft[0:1]
        for j in range(1, DEPTH):
            inner = inner + ds[j] * soft[j:j + 1]
        for j in range(DEPTH):
            o_ref[j:j + 1, :] = soft[j:j + 1] * (ds[j] - inner)

    return pl.pallas_call(body, out_shape=jax.ShapeDtypeStruct(lb_param.shape, F32), name="lower_bounds_bwd",
                          compiler_params=_cp())(lb_param, dlower)


def _exchange(arrays, scatter, name, chips=False):
    n_arr = len(arrays)
    n_slot = N_DEV // 2 if chips else N_DEV

    def body(*refs):
        srcs, dsts = refs[:n_arr], refs[n_arr:2 * n_arr]
        send_sems, recv_sems, local_sems = refs[2 * n_arr:]
        me = (2 * lax.axis_index("x") + lax.axis_index("y") if chips
              else 4 * lax.axis_index("x") + 2 * lax.axis_index("y") + lax.axis_index("c"))
        copies = _peer_copies(srcs, dsts, send_sems, recv_sems, scatter, chips)
        for a in range(n_arr):
            copies.append(pltpu.make_async_copy(srcs[a].at[me] if scatter else srcs[a], dsts[a].at[me], local_sems.at[a]))
        for cp in copies:
            cp.start()
        for cp in copies:
            cp.wait()

    out_shape = [jax.ShapeDtypeStruct(a.shape if scatter else (n_slot,) + a.shape, a.dtype) for a in arrays]
    anyspec = pl.BlockSpec(memory_space=pl.ANY)
    res = pl.pallas_call(
        body, in_specs=[anyspec] * n_arr, out_specs=[anyspec] * n_arr, out_shape=out_shape,
        scratch_shapes=[pltpu.SemaphoreType.DMA((n_arr * (n_slot - 1),)), pltpu.SemaphoreType.DMA((n_arr * (n_slot - 1),)),
                        pltpu.SemaphoreType.DMA((n_arr,))],
        name=name)(*arrays)
    return list(res)


def _peer_copies(srcs, lands, send_sems, recv_sems, scatter, chips=False):
    x, y, c = lax.axis_index("x"), lax.axis_index("y"), lax.axis_index("c")
    flips = [k for k in range(1, N_DEV) if not (chips and k & 1)]
    slot = (lambda px, py, pc: 2 * px + py) if chips else (lambda px, py, pc: 4 * px + 2 * py + pc)
    copies = []
    for a in range(len(srcs)):
        for i, k in enumerate(flips):
            px = 1 - x if k & 4 else x
            py = 1 - y if k & 2 else y
            pc = 1 - c if k & 1 else c
            src = srcs[a].at[slot(px, py, pc)] if scatter else srcs[a]
            copies.append(pltpu.make_async_remote_copy(
                src_ref=src, dst_ref=lands[a].at[slot(x, y, c)],
                send_sem=send_sems.at[a * len(flips) + i], recv_sem=recv_sems.at[a * len(flips) + i],
                device_id=(px, py, pc), device_id_type=pl.DeviceIdType.MESH))
    return copies


def _gather_two_level(arrays, name):
    n_arr = len(arrays)
    per = N_DEV - 1

    def body(*refs):
        srcs, outs = refs[:n_arr], refs[n_arr:2 * n_arr]
        send_sems, recv_sems, local_sems = refs[2 * n_arr:]
        x, y, c = lax.axis_index("x"), lax.axis_index("y"), lax.axis_index("c")
        me, sibling = (x, y, c), (x, y, 1 - c)
        chips = [(1 - x, y), (x, 1 - y), (1 - x, 1 - y)]

        def copy(a, k, block, to, src=None):
            dst = outs[a].at[4 * block[0] + 2 * block[1] + block[2]]
            return pltpu.make_async_remote_copy(
                src_ref=dst if src is None else src, dst_ref=dst,
                send_sem=send_sems.at[a * per + k], recv_sem=recv_sems.at[a * per + k],
                device_id=to, device_id_type=pl.DeviceIdType.MESH)

        own, first, passed = [], [], []
        for a in range(n_arr):
            own.append(pltpu.make_async_copy(srcs[a], outs[a].at[4 * x + 2 * y + c], local_sems.at[a]))
            first.append(copy(a, 0, me, sibling, src=srcs[a]))
            first += [copy(a, 1 + j, me, (*chip, c), src=srcs[a]) for j, chip in enumerate(chips)]
        for cp in own + first:
            cp.start()
        for a in range(n_arr):
            for j, chip in enumerate(chips):
                copy(a, 1 + j, (*chip, c), me).wait_recv()
                passed.append(copy(a, 4 + j, (*chip, c), sibling))
                passed[-1].start()
        for a in range(n_arr):
            copy(a, 0, sibling, me).wait_recv()
            for j, chip in enumerate(chips):
                copy(a, 4 + j, (*chip, 1 - c), me).wait_recv()
        for cp in first + passed:
            cp.wait_send()
        for cp in own:
            cp.wait()

    anyspec = pl.BlockSpec(memory_space=pl.ANY)
    res = pl.pallas_call(
        body, in_specs=[anyspec] * n_arr, out_specs=[anyspec] * n_arr,
        out_shape=[jax.ShapeDtypeStruct((N_DEV,) + a.shape, a.dtype) for a in arrays],
        scratch_shapes=[pltpu.SemaphoreType.DMA((n_arr * per,)), pltpu.SemaphoreType.DMA((n_arr * per,)),
                        pltpu.SemaphoreType.DMA((n_arr,))],
        name=name)(*arrays)
    return list(res)


def _sibling_swap(arrays, name):
    n_arr = len(arrays)
    n_chip = N_DEV // 2

    def body(*refs):
        srcs, outs = refs[:n_arr], refs[n_arr:2 * n_arr]
        send_sems, recv_sems = refs[2 * n_arr:]
        x, y, c = lax.axis_index("x"), lax.axis_index("y"), lax.axis_index("c")
        copies = []
        for a in range(n_arr):
            for j in range(n_chip):
                copies.append(pltpu.make_async_remote_copy(
                    src_ref=srcs[a].at[2 * j + 1 - c], dst_ref=outs[a].at[j],
                    send_sem=send_sems.at[a * n_chip + j], recv_sem=recv_sems.at[a * n_chip + j],
                    device_id=(x, y, 1 - c), device_id_type=pl.DeviceIdType.MESH))
        for cp in copies:
            cp.start()
        for cp in copies:
            cp.wait()

    anyspec = pl.BlockSpec(memory_space=pl.ANY)
    res = pl.pallas_call(
        body, in_specs=[anyspec] * n_arr, out_specs=[anyspec] * n_arr,
        out_shape=[jax.ShapeDtypeStruct((n_chip,) + a.shape[1:], a.dtype) for a in arrays],
        scratch_shapes=[pltpu.SemaphoreType.DMA((n_arr * n_chip,)), pltpu.SemaphoreType.DMA((n_arr * n_chip,))],
        name=name)(*arrays)
    return list(res)


def _pair_sum(send, stage, core, name):
    _, r, c = send.shape
    n_chip = stage.shape[0]
    tr = _tile(r, 128)

    def body(core_ref, a_ref, b_ref, o_ref):
        o_ref[...] = a_ref[...] + b_ref[...]

    return pl.pallas_call(
        body,
        grid_spec=pltpu.PrefetchScalarGridSpec(
            num_scalar_prefetch=1, grid=(n_chip, r // tr),
            in_specs=[pl.BlockSpec((1, tr, c), lambda j, i, core_ref: (2 * j + core_ref[0], i, 0)),
                      pl.BlockSpec((1, tr, c), lambda j, i, core_ref: (j, i, 0))],
            out_specs=pl.BlockSpec((1, tr, c), lambda j, i, core_ref: (j, i, 0))),
        out_shape=jax.ShapeDtypeStruct(stage.shape, F32), name=name,
        compiler_params=_cp(("parallel", "parallel")))(core, send, stage)


_HBM_SPEC = pl.BlockSpec(memory_space=pltpu.HBM)
_SEM_SPEC = pl.BlockSpec(memory_space=pltpu.SEMAPHORE)
_ANY_SPEC = pl.BlockSpec(memory_space=pl.ANY)
_DATAFLOW = pltpu.SideEffectType.DATAFLOW_SIDE_EFFECTING


def _exchange_start(arrays, scatter, name, dep=None, chips=False):
    n_arr = len(arrays)
    n_slot = N_DEV // 2 if chips else N_DEV
    n_sem = n_arr * (n_slot - 1)
    me = (2 * lax.axis_index("x") + lax.axis_index("y") if chips
          else 4 * lax.axis_index("x") + 2 * lax.axis_index("y") + lax.axis_index("c"))
    lands = []
    for a in arrays:
        own = lax.dynamic_index_in_dim(a, me, 0, keepdims=False) if scatter else a
        shape = a.shape if scatter else (n_slot,) + a.shape
        lands.append(lax.dynamic_update_index_in_dim(lax.empty(shape, a.dtype), own, me, 0))
    dep_specs, dep_args = _dep_specs(dep)

    def body(*refs):
        srcs, lnds = refs[:n_arr], refs[n_arr:2 * n_arr]
        outs = refs[2 * n_arr + len(dep_args):]
        send_sems, recv_sems, token = outs[0], outs[1], outs[2 + 2 * n_arr]
        for cp in _peer_copies(srcs, lnds, send_sems, recv_sems, scatter, chips):
            cp.start()
        token[...] = jnp.zeros_like(token)

    thru = [pltpu.HBM(a.shape, a.dtype) for a in list(arrays) + lands]
    return pl.pallas_call(
        body, name=name,
        out_shape=(pltpu.SemaphoreType.DMA((n_sem,)), pltpu.SemaphoreType.DMA((n_sem,)), *thru,
                   jax.ShapeDtypeStruct((SUBLANES, LANES), F32)),
        in_specs=[_HBM_SPEC] * (2 * n_arr) + dep_specs,
        out_specs=(_SEM_SPEC, _SEM_SPEC, *[_HBM_SPEC] * (2 * n_arr), pl.BlockSpec(memory_space=pltpu.VMEM)),
        input_output_aliases={i: 2 + i for i in range(2 * n_arr)},
        compiler_params=pltpu.CompilerParams(has_side_effects=_DATAFLOW),
    )(*[pltpu.with_memory_space_constraint(a, pltpu.HBM) for a in list(arrays) + lands], *dep_args)


def _exchange_wait(started, scatter, name, after, chips=False):
    send_sems, recv_sems = started[0], started[1]
    thru = list(started[2:-1])
    n_arr = len(thru) // 2

    def body(*refs):
        srcs, lnds = refs[:n_arr], refs[n_arr:2 * n_arr]
        for cp in _peer_copies(srcs, lnds, refs[2 * n_arr], refs[2 * n_arr + 1], scatter, chips):
            cp.wait_send()
            cp.wait_recv()

    res = pl.pallas_call(
        body, name=name, out_shape=tuple(pltpu.HBM(a.shape, a.dtype) for a in thru),
        in_specs=[_HBM_SPEC] * (2 * n_arr) + [_SEM_SPEC, _SEM_SPEC, _ANY_SPEC],
        out_specs=tuple([_HBM_SPEC] * (2 * n_arr)),
        input_output_aliases={i: i for i in range(2 * n_arr)},
        compiler_params=pltpu.CompilerParams(has_side_effects=_DATAFLOW),
    )(*thru, send_sems, recv_sems, after)
    return list(res[n_arr:])


def _unshard_cols(g, name):
    nd, r, s = g.shape
    tr = _tile(r, 64)

    def body(i_ref, o_ref):
        for p in range(nd):
            o_ref[:, p * s:(p + 1) * s] = i_ref[p]

    return pl.pallas_call(
        body, grid=(r // tr,), in_specs=[pl.BlockSpec((nd, tr, s), lambda i: (0, i, 0))],
        out_specs=pl.BlockSpec((tr, nd * s), lambda i: (i, 0)),
        out_shape=jax.ShapeDtypeStruct((r, nd * s), g.dtype), name=name, compiler_params=_cp(("parallel",)))(g)


def _shard_cols(g, name):
    r, n = g.shape
    s = n // N_DEV
    tr = _tile(r, 64)

    def body(i_ref, o_ref):
        for p in range(N_DEV):
            o_ref[p] = i_ref[:, p * s:(p + 1) * s]

    return pl.pallas_call(
        body, grid=(r // tr,), in_specs=[pl.BlockSpec((tr, n), lambda i: (i, 0))],
        out_specs=pl.BlockSpec((N_DEV, tr, s), lambda i: (0, i, 0)),
        out_shape=jax.ShapeDtypeStruct((N_DEV, r, s), g.dtype), name=name, compiler_params=_cp(("parallel",)))(g)


def _slot_sum(slots, name):
    nd, r, c = slots.shape
    tr = _tile(r, 64)

    def body(s_ref, o_ref):
        acc = s_ref[0]
        for p in range(1, nd):
            acc = acc + s_ref[p]
        o_ref[...] = acc

    return pl.pallas_call(
        body, grid=(r // tr,), in_specs=[pl.BlockSpec((nd, tr, c), lambda i: (0, i, 0))],
        out_specs=pl.BlockSpec((tr, c), lambda i: (i, 0)),
        out_shape=jax.ShapeDtypeStruct((r, c), F32), name=name, compiler_params=_cp(("parallel",)))(slots)


def _adamw(w, g, m, v, name):
    r, c = w.shape
    tr = _tile(r, 256)
    c1 = 1.0 - ADAM_B1 ** ADAM_STEP
    c2 = 1.0 - ADAM_B2 ** ADAM_STEP

    def body(w_ref, g_ref, m_ref, v_ref, d_ref, nm_ref, nv_ref):
        gv = g_ref[...]
        nm = ADAM_B1 * m_ref[...] + (1.0 - ADAM_B1) * gv
        nv = ADAM_B2 * v_ref[...] + (1.0 - ADAM_B2) * (gv * gv)
        nm_ref[...] = nm
        nv_ref[...] = nv
        d_ref[...] = -ADAM_LR * ((nm / c1) / (jnp.sqrt(nv / c2) + ADAM_EPS) + ADAM_WD * w_ref[...])

    spec = pl.BlockSpec((tr, c), lambda i: (i, 0))
    return pl.pallas_call(
        body, grid=(r // tr,), in_specs=[spec] * 4, out_specs=[spec] * 3,
        out_shape=[jax.ShapeDtypeStruct((r, c), F32)] * 3, name=name, compiler_params=_cp(("parallel",)))(w, g, m, v)


def _forward_backward(x, target, weights_hook, grads_hook, lb_param, hgrn_norm_g, attn_sinks, rel_bias, ln_g, ln_b):
    t, d = x.shape
    w = d // 2
    off, n_in = _offsets(d)
    n_heads = w // ATTN_HEAD_DIM
    grp = n_heads // ATTN_KV_HEADS

    lower = _lower_bounds(lb_param)
    bias = _bias_table(rel_bias, n_heads)
    bias_g = bias.reshape(ATTN_KV_HEADS, grp * WINDOW, 2 * WINDOW)

    saved, weights = [], []
    xb = x.astype(BF16)
    for l in range(DEPTH):
        wl, token = weights_hook(l, x)
        weights.append(wl)
        s = {"x": x, "xb": xb}
        u = _mm_nt_cols(xb, wl["w_in_t"], f"in_proj", dep=token)
        s["u"] = u
        lb_l, gain_l, cw_l = lower[l:l + 1], hgrn_norm_g[l:l + 1], wl["conv_w"]
        o_a, states, p_a = _hgrn_fwd(u, lb_l, gain_l, off, f"hgrn_fwd")
        o_b, p_b = _attn_fwd(u, bias_g, attn_sinks[l], off, w, f"attn_fwd")
        p_c = _conv_fwd(u, cw_l, off, w, f"conv_fwd")
        y_a = _mm_nn(p_a, wl["w_proj_hgrn"], f"proj_a", tn=2048)
        y_b = _mm_nn(p_b, wl["w_proj_attn"], f"proj_b", tn=2048)
        y_c, merged = _proj_merge(p_c, wl["w_proj_conv"], y_a, y_b, u, off, f"proj_c_merge")
        x, xb, xhat, rstd = _out_proj_ln(merged, wl["w_out"], x, ln_g[l:l + 1], ln_b[l:l + 1], f"out_proj_ln")
        s.update(o_a=o_a, states=states, p_a=p_a, o_b=o_b, p_b=p_b, p_c=p_c, y_a=y_a, y_b=y_b, y_c=y_c,
                 merged=merged, xhat=xhat, rstd=rstd)
        saved.append(s)

    loss_acc, dx = _loss_head(x, target)

    d_ln, d_lower, d_gain, d_sink, d_conv = [None] * DEPTH, [None] * DEPTH, [None] * DEPTH, [None] * DEPTH, [None] * DEPTH
    dbias_total = None
    for l in reversed(range(DEPTH)):
        wl, s = weights[l], saved[l]
        u = s["u"]
        lb_l, gain_l, cw_l = lower[l:l + 1], hgrn_norm_g[l:l + 1], wl["conv_w"]
        dz, dzb, d_ln[l], dya, dyb, dyc, dma, dmb, dmc = _ln_bwd_out_proj_merge(
            dx, s["xhat"], s["rstd"], ln_g[l:l + 1], wl["w_out"], u, s["y_a"], s["y_b"], s["y_c"], off,
            f"ln_bwd_merge_bwd")
        g_out = _mm_tn(s["merged"], dzb, f"g_out", tn=2048)
        g_pa = _mm_tn(s["p_a"], dya, f"g_proj_a", tn=2048)
        g_pb = _mm_tn(s["p_b"], dyb, f"g_proj_b", tn=2048)
        g_pc = _mm_tn(s["p_c"], dyc, f"g_proj_c", tn=2048)
        dpa = _mm_nt(dya, wl["w_proj_hgrn"], f"d_p_a", tk=2048)
        dpb = _mm_nt(dyb, wl["w_proj_attn"], f"d_p_b", tk=2048)
        dpc = _mm_nt(dyc, wl["w_proj_conv"], f"d_p_c", tk=2048)
        d_aq, d_af, d_ai, d_ag, acc_a = _hgrn_bwd(u, lb_l, gain_l, s["states"], s["o_a"], dpa, off, f"hgrn_bwd")
        d_lower[l], d_gain[l] = acc_a[0:1], acc_a[1:2]
        d_bq, d_bkv, dbias_l, d_sink[l], d_bg = _attn_bwd(
            u, s["o_b"], dpb, bias_g, attn_sinks[l], off, w, f"attn_bwd")
        dbias_total = dbias_l if dbias_total is None else dbias_total + dbias_l
        d_cb, d_cc, d_cx, d_cg, d_conv[l] = _conv_bwd(dpc, u, cw_l, off, w, f"conv_bwd")
        du = jnp.concatenate([d_aq, d_af, d_ai, d_ag, d_bq, d_bkv, d_bg, d_cb, d_cc, d_cx, d_cg, dma, dmb, dmc], axis=1)
        g_in_t = _mm_tn_rows(du, s["xb"], f"g_in")
        token = grads_hook(l, {"w_in_t": g_in_t, "w_proj_hgrn": g_pa, "w_proj_attn": g_pb, "w_proj_conv": g_pc, "w_out": g_out})
        dx = _mm_nn_acc(du, wl["w_in_t"], f"d_x", add=dz, add_scale=ALPHA, dep=token)

    d_lower_all = jnp.concatenate([a[0:1] for a in d_lower], axis=0)
    small = {
        "lb_param": _lower_bounds_bwd(lb_param, d_lower_all),
        "hgrn_norm_g": jnp.concatenate([a[0:1] for a in d_gain], axis=0),
        "attn_sinks": jnp.concatenate([a[0:1, :n_heads] for a in d_sink], axis=0),
        "conv_w": jnp.stack([a[0:3] for a in d_conv], axis=0),
        "rel_bias": _bias_grad(dbias_total.reshape(n_heads, WINDOW, 2 * WINDOW), n_heads)[:, :n_heads],
        "ln_g": jnp.concatenate([a[0:1] for a in d_ln], axis=0),
        "ln_b": jnp.concatenate([a[1:2] for a in d_ln], axis=0),
    }
    return loss_acc, dx, small


BIG = ("w_in", "w_proj_hgrn", "w_proj_attn", "w_proj_conv", "w_out")
SMALL = ("lb_param", "hgrn_norm_g", "attn_sinks", "conv_w", "rel_bias", "ln_g", "ln_b")
ORDER = ("w_in", "w_proj_hgrn", "w_proj_attn", "w_proj_conv", "w_out", "lb_param", "hgrn_norm_g", "attn_sinks",
         "conv_w", "rel_bias", "ln_g", "ln_b")


def _pack(parts):
    flat = jnp.concatenate([p.reshape(-1) for p in parts])
    n = flat.shape[0]
    unit = SUBLANES * LANES
    total = -(-n // unit) * unit
    return jnp.pad(flat, (0, total - n)).reshape(total // LANES, LANES)


def _unpack(packed, shapes):
    flat = packed.reshape(-1)
    out, o = [], 0
    for shp in shapes:
        n = int(np.prod(shp))
        out.append(flat[o:o + n].reshape(shp))
        o += n
    return out


def kernel(x, w_in, w_proj_hgrn, w_proj_attn, w_proj_conv, w_out, lb_param, hgrn_norm_g, attn_sinks, conv_w, rel_bias, ln_g, ln_b, loss_target, m_w_in, m_w_proj_hgrn, m_w_proj_attn, m_w_proj_conv, m_w_out, m_lb_param, m_hgrn_norm_g, m_attn_sinks, m_conv_w, m_rel_bias, m_ln_g, m_ln_b, v_w_in, v_w_proj_hgrn, v_w_proj_attn, v_w_proj_conv, v_w_out, v_lb_param, v_hgrn_norm_g, v_attn_sinks, v_conv_w, v_rel_bias, v_ln_g, v_ln_b):
    params = dict(w_in=w_in, w_proj_hgrn=w_proj_hgrn, w_proj_attn=w_proj_attn, w_proj_conv=w_proj_conv, w_out=w_out,
                  lb_param=lb_param, hgrn_norm_g=hgrn_norm_g, attn_sinks=attn_sinks, conv_w=conv_w, rel_bias=rel_bias,
                  ln_g=ln_g, ln_b=ln_b)
    mom_m = dict(w_in=m_w_in, w_proj_hgrn=m_w_proj_hgrn, w_proj_attn=m_w_proj_attn, w_proj_conv=m_w_proj_conv,
                 w_out=m_w_out, lb_param=m_lb_param, hgrn_norm_g=m_hgrn_norm_g, attn_sinks=m_attn_sinks,
                 conv_w=m_conv_w, rel_bias=m_rel_bias, ln_g=m_ln_g, ln_b=m_ln_b)
    mom_v = dict(w_in=v_w_in, w_proj_hgrn=v_w_proj_hgrn, w_proj_attn=v_w_proj_attn, w_proj_conv=v_w_proj_conv,
                 w_out=v_w_out, lb_param=v_lb_param, hgrn_norm_g=v_hgrn_norm_g, attn_sinks=v_attn_sinks,
                 conv_w=v_conv_w, rel_bias=v_rel_bias, ln_g=v_ln_g, ln_b=v_ln_b)
    d = x.shape[-1]
    me = 4 * lax.axis_index("x") + 2 * lax.axis_index("y") + lax.axis_index("c")
    for group in (params, mom_m, mom_v):
        group["w_in"] = jnp.swapaxes(group["w_in"], 1, 2)

    def shards_of(l):
        return [params[n][l].astype(BF16) for n in BIG] + [conv_w[l]]

    gathers = {}

    def weights_hook(l, x_in):
        if l == 0:
            got = _gather_two_level(shards_of(0), "gather_weights_0")
        else:
            got = _exchange_wait(gathers.pop(l), False, f"gather_wait_{l}", x_in)
        token = None
        if l + 1 < DEPTH:
            gathers[l + 1] = _exchange_start(shards_of(l + 1), False, f"gather_start_{l + 1}", dep=got[0])
            token = gathers[l + 1][-1]
        wl = {
            "w_in_t": got[0].reshape(-1, d),
            "w_proj_hgrn": _unshard_cols(got[1], "unshard_w_proj_hgrn"),
            "w_proj_attn": _unshard_cols(got[2], "unshard_w_proj_attn"),
            "w_proj_conv": _unshard_cols(got[3], "unshard_w_proj_conv"),
            "w_out": got[4].reshape(d, d),
            "conv_w": _unshard_cols(got[5], "unshard_conv_w"),
        }
        return wl, token

    grads = {n: [None] * DEPTH for n in BIG}
    scatters = {}

    def finish_scatter(l, after):
        got = _exchange_wait(scatters.pop(l), True, f"scatter_wait_{l}", after, chips=(l == 0))
        for n, slots in zip(BIG, got):
            grads[n][l] = _slot_sum(slots, f"sum_{n}_chips" if l == 0 else f"sum_{n}")
        return got[0]

    def grads_hook(l, g):
        send = [g["w_in_t"].reshape(N_DEV, -1, d), _shard_cols(g["w_proj_hgrn"], "shard_g_proj_a"),
                _shard_cols(g["w_proj_attn"], "shard_g_proj_b"), _shard_cols(g["w_proj_conv"], "shard_g_proj_c"),
                g["w_out"].reshape(N_DEV, d // N_DEV, d)]
        dep = finish_scatter(l + 1, send[0]) if l + 1 < DEPTH else None
        if l == 0:
            core = lax.axis_index("c").astype(jnp.int32).reshape(1)
            staged = _sibling_swap(send, "pair_swap_grads")
            send = [_pair_sum(s, st, core, f"pair_sum_{n}") for n, s, st in zip(BIG, send, staged)]
        scatters[l] = _exchange_start(send, True, f"scatter_start_{l}", dep=dep, chips=(l == 0))
        return scatters[l][-1]

    loss_acc, dx, small = _forward_backward(
        x[0], loss_target[0], weights_hook, grads_hook, lb_param, hgrn_norm_g, attn_sinks, rel_bias, ln_g, ln_b)
    loss = lax.psum(0.5 * jnp.sum(loss_acc[0]) / d, ("x", "y", "c"))
    finish_scatter(0, dx)
    for n in BIG:
        grads[n] = jnp.stack(grads[n], axis=0)

    small_shapes = [small[n].shape for n in SMALL]
    packed = _pack([small[n] for n in SMALL])
    got = _exchange([packed], False, "gather_small_grads")[0]
    summed = _unpack(_slot_sum(got, "sum_small_grads"), small_shapes)
    for n, g in zip(SMALL, summed):
        grads[n] = g
    cs = conv_w.shape[-1]
    grads["conv_w"] = lax.dynamic_slice_in_dim(grads["conv_w"], me * cs, cs, axis=2)

    delta, new_m, new_v = {}, {}, {}
    for n in BIG:
        shp = params[n].shape
        flat = lambda a: a.reshape(-1, shp[-1])
        dl, nm, nv = _adamw(flat(params[n]), flat(grads[n]), flat(mom_m[n]), flat(mom_v[n]), f"adamw_{n}")
        delta[n], new_m[n], new_v[n] = dl.reshape(shp), nm.reshape(shp), nv.reshape(shp)
    shapes = [params[n].shape for n in SMALL]
    res = _adamw(_pack([params[n] for n in SMALL]), _pack([grads[n] for n in SMALL]),
                 _pack([mom_m[n] for n in SMALL]), _pack([mom_v[n] for n in SMALL]), "adamw_small")
    for dst, packed_res in zip((delta, new_m, new_v), res):
        for n, a in zip(SMALL, _unpack(packed_res, shapes)):
            dst[n] = a

    for group in (grads, delta, new_m, new_v):
        group["w_in"] = jnp.swapaxes(group["w_in"], 1, 2)
    return (loss, dx[None], *[grads[n] for n in ORDER], *[delta[n] for n in ORDER],
            *[new_m[n] for n in ORDER], *[new_v[n] for n in ORDER])
```

```python
import functools
import math

import numpy as np
import jax
import jax.numpy as jnp
from jax import lax
from jax.experimental import pallas as pl
from jax.experimental.pallas import tpu as pltpu

F32 = jnp.float32
BF16 = jnp.bfloat16

N_DEV = 8
DEPTH = 4
HGRN_HEAD_DIM = 128
HGRN_CHUNK = 64
ATTN_HEAD_DIM = 64
ATTN_KV_HEADS = 4
KV_WIDTH = ATTN_KV_HEADS * ATTN_HEAD_DIM
WINDOW = 128
WINDOW_SHIFT = 7
N_BUCKETS = 32
MAX_DISTANCE = 128
ALPHA = (2.0 * DEPTH) ** 0.25
LN_EPS = 1e-5
RMS_EPS = 1e-6
ADAM_LR = 0.001
ADAM_B1 = 0.9
ADAM_B2 = 0.999
ADAM_EPS = 1e-08
ADAM_WD = 0.01
ADAM_STEP = 10

LANES = 128
SUBLANES = 8
VMEM_LIMIT = 56 << 20
NEG_INF = float("-inf")


def _offsets(d_model):
    w = d_model // 2
    sizes = (w, w, w, w, w, KV_WIDTH, KV_WIDTH, w, w, w, w, w, d_model, d_model, d_model)
    names = ("a_q", "a_f", "a_i", "a_g", "b_q", "b_k", "b_v", "b_g", "c_b", "c_c", "c_x", "c_g", "m_a", "m_b", "m_c")
    off, o = {}, 0
    for n, s in zip(names, sizes):
        off[n] = o
        o += s
    return off, o


def _tile(n, pref):
    t = min(pref, n)
    while n % t:
        t //= 2
    return t


def _cp(sem=None, vmem=VMEM_LIMIT):
    return pltpu.CompilerParams(dimension_semantics=sem, vmem_limit_bytes=vmem)


def _sigmoid(x):
    return 1.0 / (1.0 + jnp.exp(-x))


def _dot_nn(a, b):
    return jnp.dot(a, b, preferred_element_type=F32)


def _dot_nt(a, b):
    return lax.dot_general(a, b, (((1,), (1,)), ((), ())), preferred_element_type=F32)


def _dot_tn(a, b):
    return lax.dot_general(a, b, (((0,), (0,)), ((), ())), preferred_element_type=F32)


def _dep_specs(dep):
    return ([], []) if dep is None else ([pl.BlockSpec(memory_space=pl.ANY)], [dep])


def _mm_nn(a, b, name, out_dtype=F32, tm=1024, tn=1536, dep=None):
    m, k = a.shape
    _, n = b.shape
    tm, tn = _tile(m, tm), _tile(n, tn)
    dep_specs, dep_args = _dep_specs(dep)

    def body(a_ref, b_ref, *rest):
        o_ref = rest[-1]
        o_ref[...] = _dot_nn(a_ref[...], b_ref[...]).astype(o_ref.dtype)

    return pl.pallas_call(
        body, grid=(n // tn, m // tm),
        in_specs=[pl.BlockSpec((tm, k), lambda j, i: (i, 0)), pl.BlockSpec((k, tn), lambda j, i: (0, j))] + dep_specs,
        out_specs=pl.BlockSpec((tm, tn), lambda j, i: (i, j)),
        out_shape=jax.ShapeDtypeStruct((m, n), out_dtype), name=name,
        compiler_params=_cp(("parallel", "parallel")))(a, b, *dep_args)


def _mm_nt(a, b, name, tm=1024, tk=1536, add=None, add_scale=1.0, dep=None):
    m, k = a.shape
    n, _ = b.shape
    tm, tk = _tile(m, tm), _tile(k, tk)
    has_add = add is not None
    dep_specs, dep_args = _dep_specs(dep)

    def body(*refs):
        if has_add:
            a_ref, b_ref, add_ref = refs[:3]
        else:
            a_ref, b_ref = refs[:2]
        o_ref = refs[-1]
        if k == tk:
            prod = _dot_nt(a_ref[...], b_ref[...])
            o_ref[...] = prod + add_ref[...] * add_scale if has_add else prod
            return

        @pl.when(pl.program_id(1) == 0)
        def _():
            if has_add:
                o_ref[...] = add_ref[...] * add_scale
            else:
                o_ref[...] = jnp.zeros_like(o_ref)

        o_ref[...] += _dot_nt(a_ref[...], b_ref[...])

    in_specs = [pl.BlockSpec((tm, tk), lambda i, kk: (i, kk)), pl.BlockSpec((n, tk), lambda i, kk: (0, kk))]
    args = [a, b]
    if has_add:
        in_specs.append(pl.BlockSpec((tm, n), lambda i, kk: (i, 0)))
        args.append(add)
    in_specs += dep_specs
    args += dep_args
    return pl.pallas_call(
        body, grid=(m // tm, k // tk), in_specs=in_specs,
        out_specs=pl.BlockSpec((tm, n), lambda i, kk: (i, 0)),
        out_shape=jax.ShapeDtypeStruct((m, n), F32), name=name,
        compiler_params=_cp(("parallel", "arbitrary")))(*args)


def _mm_nt_cols(a, b, name, tm=1024, tn=1536, dep=None):
    m, k = a.shape
    n, _ = b.shape
    tm, tn = _tile(m, tm), _tile(n, tn)
    dep_specs, dep_args = _dep_specs(dep)

    def body(a_ref, b_ref, *rest):
        rest[-1][...] = _dot_nt(a_ref[...], b_ref[...])

    return pl.pallas_call(
        body, grid=(n // tn, m // tm),
        in_specs=[pl.BlockSpec((tm, k), lambda j, i: (i, 0)), pl.BlockSpec((tn, k), lambda j, i: (j, 0))] + dep_specs,
        out_specs=pl.BlockSpec((tm, tn), lambda j, i: (i, j)),
        out_shape=jax.ShapeDtypeStruct((m, n), F32), name=name,
        compiler_params=_cp(("parallel", "parallel")))(a, b, *dep_args)


def _mm_nn_acc(a, b, name, tm=1024, tk=1536, add=None, add_scale=1.0, dep=None):
    m, k = a.shape
    _, n = b.shape
    tm, tk = _tile(m, tm), _tile(k, tk)
    dep_specs, dep_args = _dep_specs(dep)

    def body(a_ref, b_ref, add_ref, *rest):
        o_ref = rest[-1]

        @pl.when(pl.program_id(1) == 0)
        def _():
            o_ref[...] = add_ref[...] * add_scale

        o_ref[...] += _dot_nn(a_ref[...], b_ref[...])

    return pl.pallas_call(
        body, grid=(m // tm, k // tk),
        in_specs=[pl.BlockSpec((tm, tk), lambda i, kk: (i, kk)), pl.BlockSpec((tk, n), lambda i, kk: (kk, 0)),
                  pl.BlockSpec((tm, n), lambda i, kk: (i, 0))] + dep_specs,
        out_specs=pl.BlockSpec((tm, n), lambda i, kk: (i, 0)),
        out_shape=jax.ShapeDtypeStruct((m, n), F32), name=name,
        compiler_params=_cp(("parallel", "arbitrary")))(a, b, add, *dep_args)


def _mm_tn_rows(a, b, name, tt=1024, tr=1536):
    t, k = a.shape
    _, n = b.shape
    tt, tr = _tile(t, tt), _tile(k, tr)

    def body(a_ref, b_ref, o_ref):
        @pl.when(pl.program_id(1) == 0)
        def _():
            o_ref[...] = jnp.zeros_like(o_ref)

        o_ref[...] += _dot_tn(a_ref[...], b_ref[...])

    return pl.pallas_call(
        body, grid=(k // tr, t // tt),
        in_specs=[pl.BlockSpec((tt, tr), lambda j, s: (s, j)), pl.BlockSpec((tt, n), lambda j, s: (s, 0))],
        out_specs=pl.BlockSpec((tr, n), lambda j, s: (j, 0)),
        out_shape=jax.ShapeDtypeStruct((k, n), F32), name=name,
        compiler_params=_cp(("parallel", "arbitrary")))(a, b)


def _mm_tn(a, b, name, tt=1024, tn=1536):
    t, k = a.shape
    _, n = b.shape
    tt, tn = _tile(t, tt), _tile(n, tn)

    def body(a_ref, b_ref, o_ref):
        @pl.when(pl.program_id(1) == 0)
        def _():
            o_ref[...] = jnp.zeros_like(o_ref)

        o_ref[...] += _dot_tn(a_ref[...], b_ref[...])

    return pl.pallas_call(
        body, grid=(n // tn, t // tt),
        in_specs=[pl.BlockSpec((tt, k), lambda j, s: (s, 0)), pl.BlockSpec((tt, tn), lambda j, s: (s, j))],
        out_specs=pl.BlockSpec((k, tn), lambda j, s: (0, j)),
        out_shape=jax.ShapeDtypeStruct((k, n), F32), name=name,
        compiler_params=_cp(("parallel", "arbitrary")))(a, b)


def _ew(body, name, t, ncol, wb, ins, outs, accs=(), tt=512):
    tt = _tile(t, tt)
    nt = t // tt
    per = tt // SUBLANES
    in_specs, args = [], []
    for arr, kind, coff in ins:
        if kind == "tile":
            spec = pl.BlockSpec((tt, wb), lambda j, i, c=coff: (i, c + j))
        elif kind == "prev":
            spec = pl.BlockSpec((SUBLANES, wb), lambda j, i, c=coff: (jnp.maximum(i * per - 1, 0), c + j))
        elif kind == "next":
            spec = pl.BlockSpec((SUBLANES, wb), lambda j, i, c=coff: (jnp.minimum((i + 1) * per, nt * per - 1), c + j))
        else:
            spec = pl.BlockSpec((arr.shape[0], wb), lambda j, i, c=coff: (0, c + j))
        in_specs.append(spec)
        args.append(arr)
    out_specs = [pl.BlockSpec((tt, wb), lambda j, i: (i, j)) for _ in outs]
    out_shape = [jax.ShapeDtypeStruct((t, ncol * wb), d) for d in outs]
    for r in accs:
        out_specs.append(pl.BlockSpec((r, wb), lambda j, i: (0, j)))
        out_shape.append(jax.ShapeDtypeStruct((r, ncol * wb), F32))

    def kern(*refs):
        body(pl.program_id(1), nt, *refs)

    res = pl.pallas_call(
        kern, grid=(ncol, nt), in_specs=in_specs, out_specs=out_specs, out_shape=out_shape, name=name,
        compiler_params=_cp(("parallel", "arbitrary")))(*args)
    return res


def _silu_parts(x):
    s = _sigmoid(x)
    return x * s, s + x * s * (1.0 - s)


def _out_proj_ln(a, w, x, g, b, name):
    t, d = x.shape
    k = a.shape[1]
    tt = _tile(t, 256)

    def body(a_ref, w_ref, x_ref, g_ref, b_ref, o_ref, ob_ref, xh_ref, r_ref):
        z = ALPHA * x_ref[...] + _dot_nn(a_ref[...], w_ref[...])
        mu = jnp.mean(z, axis=1, keepdims=True)
        zc = z - mu
        var = jnp.mean(zc * zc, axis=1, keepdims=True)
        rstd = lax.rsqrt(var + LN_EPS)
        xh = zc * rstd
        o = xh * g_ref[...] + b_ref[...]
        o_ref[...] = o
        ob_ref[...] = o.astype(BF16)
        xh_ref[...] = xh
        r_ref[...] = rstd

    row = pl.BlockSpec((tt, d), lambda i: (i, 0))
    vec = pl.BlockSpec((1, d), lambda i: (0, 0))
    return pl.pallas_call(
        body, grid=(t // tt,),
        in_specs=[pl.BlockSpec((tt, k), lambda i: (i, 0)), pl.BlockSpec((k, d), lambda i: (0, 0)), row, vec, vec],
        out_specs=[row, row, row, pl.BlockSpec((tt, 1), lambda i: (i, 0))],
        out_shape=[jax.ShapeDtypeStruct((t, d), F32), jax.ShapeDtypeStruct((t, d), BF16),
                   jax.ShapeDtypeStruct((t, d), F32), jax.ShapeDtypeStruct((t, 1), F32)],
        name=name, compiler_params=_cp(("parallel",)))(a, w, x, g, b)


def _ln_bwd_out_proj_merge(dout, xhat, rstd, g, w, u, ya, yb, yc, off, name):
    t, d = dout.shape
    gb = GATE_BLOCK
    nb = d // gb
    tt = _tile(t, 128)
    cols = [off[nme] // gb for nme in ("m_a", "m_b", "m_c")]

    def body(do_ref, xh_ref, r_ref, g_ref, w_ref, ya_ref, yb_ref, yc_ref, *rest):
        m_refs = rest[:3 * nb]
        dz_ref, dzb_ref, acc_ref = rest[3 * nb:3 * nb + 3]
        dy_refs, dg_refs = rest[3 * nb + 3:3 * nb + 6], rest[3 * nb + 6:]

        @pl.when(pl.program_id(0) == 0)
        def _():
            acc_ref[...] = jnp.zeros_like(acc_ref)

        do = do_ref[...]
        xh = xh_ref[...]
        dxh = do * g_ref[...]
        m1 = jnp.mean(dxh, axis=1, keepdims=True)
        m2 = jnp.mean(dxh * xh, axis=1, keepdims=True)
        dz = r_ref[...] * (dxh - m1 - xh * m2)
        dzb = dz.astype(BF16)
        dz_ref[...] = dz
        dzb_ref[...] = dzb
        acc_ref[0:1, :] += jnp.sum(do * xh, axis=0, keepdims=True)
        acc_ref[1:2, :] += jnp.sum(do, axis=0, keepdims=True)
        dm = _dot_nt(dzb, w_ref[...])
        for gate, y_ref in enumerate((ya_ref, yb_ref, yc_ref)):
            for j in range(nb):
                sl = slice(j * gb, (j + 1) * gb)
                s = _sigmoid(m_refs[gate * nb + j][...])
                dm_j = dm[:, sl]
                dy_refs[gate][:, sl] = (dm_j * s).astype(BF16)
                dg_refs[gate][:, sl] = (dm_j * y_ref[:, sl] * s * (1.0 - s)).astype(BF16)

    row = pl.BlockSpec((tt, d), lambda i: (i, 0))
    gates = [pl.BlockSpec((tt, gb), lambda i, c=c0 + j: (i, c)) for c0 in cols for j in range(nb)]
    return pl.pallas_call(
        body, grid=(t // tt,),
        in_specs=[row, row, pl.BlockSpec((tt, 1), lambda i: (i, 0)), pl.BlockSpec((1, d), lambda i: (0, 0)),
                  pl.BlockSpec((d, d), lambda i: (0, 0)), row, row, row] + gates,
        out_specs=[row, row, pl.BlockSpec((SUBLANES, d), lambda i: (0, 0))] + [row] * 6,
        out_shape=[jax.ShapeDtypeStruct((t, d), F32), jax.ShapeDtypeStruct((t, d), BF16),
                   jax.ShapeDtypeStruct((SUBLANES, d), F32)] + [jax.ShapeDtypeStruct((t, d), BF16)] * 6,
        name=name, compiler_params=_cp(("arbitrary",)))(dout, xhat, rstd, g, w, ya, yb, yc, *([u] * (3 * nb)))


def _loss_head(y, target):
    t, d = y.shape
    tt = _tile(t, 256)

    def body(y_ref, t_ref, acc_ref, dy_ref):
        @pl.when(pl.program_id(0) == 0)
        def _():
            acc_ref[...] = jnp.zeros_like(acc_ref)

        err = y_ref[...] - t_ref[...]
        dy_ref[...] = err * (1.0 / d)
        acc_ref[0:1, :] += jnp.sum(err * err, axis=0, keepdims=True)

    row = pl.BlockSpec((tt, d), lambda i: (i, 0))
    acc, dy = pl.pallas_call(
        body, grid=(t // tt,), in_specs=[row, row],
        out_specs=[pl.BlockSpec((SUBLANES, d), lambda i: (0, 0)), row],
        out_shape=[jax.ShapeDtypeStruct((SUBLANES, d), F32), jax.ShapeDtypeStruct((t, d), F32)],
        name="loss_head", compiler_params=_cp(("arbitrary",)))(y, target)
    return acc, dy


def _tri(lower):
    r = lax.broadcasted_iota(jnp.int32, (HGRN_CHUNK, HGRN_CHUNK), 0)
    c = lax.broadcasted_iota(jnp.int32, (HGRN_CHUNK, HGRN_CHUNK), 1)
    return jnp.where((r >= c) if lower else (r <= c), 1.0, 0.0).astype(BF16)


def _exact_tri_matmul(tri, x):
    hi = x.astype(BF16)
    r1 = x - hi.astype(F32)
    mid = r1.astype(BF16)
    lo = (r1 - mid.astype(F32)).astype(BF16)
    return _dot_nn(tri, hi) + _dot_nn(tri, mid) + _dot_nn(tri, lo)


def _hgrn_gates(q_raw, fl, lb):
    sq = _sigmoid(q_raw)
    qf = q_raw * sq * (HGRN_HEAD_DIM ** -0.5)
    sg = _sigmoid(fl)
    f = lb + (1.0 - lb) * sg
    return qf, sq, sg, f


HGRN_SUB = 16
HGRN_NSUB = HGRN_CHUNK // HGRN_SUB
HGRN_HEADS_PER_STEP = 8
HGRN_HEADS_PER_STEP_BWD = 8


def _diag_rows(r):
    return (r // SUBLANES) * SUBLANES


def _heads(x):
    hd = HGRN_HEAD_DIM
    return [x[:, i * hd:(i + 1) * hd] for i in range(x.shape[1] // hd)]


def _per_head(fn, *xs):
    split = [x if isinstance(x, (list, tuple)) else _heads(x) for x in xs]
    return jnp.concatenate([fn(*hs) for hs in zip(*split)], axis=1)


def _head_lane_sum(x):
    return _per_head(lambda h: jnp.broadcast_to(jnp.sum(h, axis=1, keepdims=True), h.shape), x)


def _hgrn_intra_fwd(qf, k, v, b):
    ch, sub, wd = HGRN_CHUNK, HGRN_SUB, qf.shape[1]
    tl = lax.broadcasted_iota(jnp.int32, (sub, wd), 0)
    blocks = []
    for m in range(HGRN_NSUB):
        rs = slice(m * sub, (m + 1) * sub)
        bm, qm, km, vm = b[rs], qf[rs], k[rs], v[rs]
        parts = {0: jnp.zeros((sub, wd), F32), SUBLANES: jnp.zeros((sub - SUBLANES, wd), F32)}
        for r in range(sub):
            lo = _diag_rows(r)
            e = jnp.exp(jnp.where(tl[lo:] >= r, bm[lo:] - bm[r:r + 1], NEG_INF))
            parts[lo] = parts[lo] + _head_lane_sum(qm[lo:] * e * km[r:r + 1]) * vm[r:r + 1]
        blocks.append(parts[0] + jnp.concatenate([jnp.zeros((SUBLANES, wd), F32), parts[SUBLANES]], axis=0))
    acc = jnp.concatenate(blocks, axis=0)
    for j in range(HGRN_NSUB - 1):
        lo = sub * (j + 1)
        c = b[lo - 1:lo, :]
        qj = (qf[lo:] * jnp.exp(b[lo:] - c)).astype(BF16)
        kj = (k[lo - sub:lo] * jnp.exp(c - b[lo - sub:lo])).astype(BF16)
        vj = v[lo - sub:lo].astype(BF16)
        contrib = _per_head(lambda q_, k_, v_: _dot_nn(_dot_nt(q_, k_).astype(BF16), v_), qj, kj, vj)
        acc = acc + jnp.concatenate([jnp.zeros((lo, wd), F32), contrib], axis=0)
    return acc


def _hgrn_intra_bwd(qf, k, v, b, do_v):
    ch, sub, wd = HGRN_CHUNK, HGRN_SUB, qf.shape[1]
    tl = lax.broadcasted_iota(jnp.int32, (sub, wd), 0)
    dq_blocks, dk_blocks, dv_blocks = [], [], []
    for m in range(HGRN_NSUB):
        rs = slice(m * sub, (m + 1) * sub)
        bm, qm, km, vm, dom = b[rs], qf[rs], k[rs], v[rs], do_v[rs]
        parts = {0: jnp.zeros((sub, wd), F32), SUBLANES: jnp.zeros((sub - SUBLANES, wd), F32)}
        dk_parts = {sub: jnp.zeros((sub, wd), F32), SUBLANES: jnp.zeros((SUBLANES, wd), F32)}
        dv_parts = {sub: jnp.zeros((sub, wd), F32), SUBLANES: jnp.zeros((SUBLANES, wd), F32)}
        for r in range(sub):
            lo = _diag_rows(r)
            b_r, k_r, v_r, q_r, do_r = bm[r:r + 1], km[r:r + 1], vm[r:r + 1], qm[r:r + 1], dom[r:r + 1]
            e = jnp.exp(jnp.where(tl[lo:] >= r, bm[lo:] - b_r, NEG_INF))
            parts[lo] = parts[lo] + _head_lane_sum(dom[lo:] * v_r) * (k_r * e)
            hi = lo + SUBLANES
            e2 = jnp.exp(jnp.where(tl[:hi] <= r, b_r - bm[:hi], NEG_INF))
            qe2 = q_r * e2
            dk_parts[hi] = dk_parts[hi] + _head_lane_sum(vm[:hi] * do_r) * qe2
            dv_parts[hi] = dv_parts[hi] + _head_lane_sum(km[:hi] * qe2) * do_r
        pad = jnp.zeros((SUBLANES, wd), F32)
        dq_blocks.append(parts[0] + jnp.concatenate([pad, parts[SUBLANES]], axis=0))
        dk_blocks.append(dk_parts[sub] + jnp.concatenate([dk_parts[SUBLANES], pad], axis=0))
        dv_blocks.append(dv_parts[sub] + jnp.concatenate([dv_parts[SUBLANES], pad], axis=0))
    dq = jnp.concatenate(dq_blocks, axis=0)
    dk = jnp.concatenate(dk_blocks, axis=0)
    dv = jnp.concatenate(dv_blocks, axis=0)
    do_b, v_b = do_v.astype(BF16), v.astype(BF16)
    dk_off, dv_off = [], []
    for j in range(HGRN_NSUB - 1):
        lo = sub * (j + 1)
        c = b[lo - 1:lo, :]
        eq = jnp.exp(b[lo:] - c)
        ek = jnp.exp(c - b[lo - sub:lo])
        qj = (qf[lo:] * eq).astype(BF16)
        kj = (k[lo - sub:lo] * ek).astype(BF16)
        doj, vj = do_b[lo:], v_b[lo - sub:lo]
        dq_j = _per_head(lambda do_, v_, k_: _dot_nn(_dot_nt(do_, v_).astype(BF16), k_), doj, vj, kj)
        dk_j = _per_head(lambda do_, v_, q_: _dot_nn(_dot_nt(v_, do_).astype(BF16), q_), doj, vj, qj)
        dv_j = _per_head(lambda do_, k_, q_: _dot_nn(_dot_nt(k_, q_).astype(BF16), do_), doj, kj, qj)
        dq = dq + jnp.concatenate([jnp.zeros((lo, wd), F32), dq_j * eq], axis=0)
        dk_off.append(dk_j * ek)
        dv_off.append(dv_j)
    zero = jnp.zeros((sub, wd), F32)
    dk = dk + jnp.concatenate(dk_off + [zero], axis=0)
    dv = dv + jnp.concatenate(dv_off + [zero], axis=0)
    return dq, dk, dv


def _head_rms(o):
    return lax.rsqrt(_head_lane_sum(o * o) * (1.0 / HGRN_HEAD_DIM) + RMS_EPS)


def _hgrn_fwd(u, lb, gain, off, name):
    t = u.shape[0]
    w = lb.shape[1]
    hd, ch, hp = HGRN_HEAD_DIM, HGRN_CHUNK, HGRN_HEADS_PER_STEP
    nh, nc = w // hd, t // ch
    wb = hp * hd
    cq, cf, cv, cg = off["a_q"] // wb, off["a_f"] // wb, off["a_i"] // wb, off["a_g"] // wb

    def body(q_ref, f_ref, v_ref, g_ref, lb_ref, gain_ref, o_ref, st_ref, p_ref, state):
        @pl.when(pl.program_id(1) == 0)
        def _():
            state[...] = jnp.zeros_like(state)

        sts = [state[i] for i in range(hp)]
        qf, _, _, f = _hgrn_gates(q_ref[...], f_ref[...], lb_ref[...])
        k = 1.0 - f
        v = v_ref[...]
        b = _exact_tri_matmul(_tri(True), jnp.log(f))
        inter = _per_head(lambda qa_, st_: _dot_nt(qa_, st_.astype(BF16)), (qf * jnp.exp(b)).astype(BF16), sts)
        o = inter + _hgrn_intra_fwd(qf, k, v, b)
        o_ref[...] = o
        silu, _ = _silu_parts(g_ref[...])
        p_ref[...] = (o * _head_rms(o) * gain_ref[...] * silu).astype(BF16)
        b_end = b[ch - 1:ch, :]
        a_end = _heads(jnp.exp(b_end))
        kd = _heads((k * jnp.exp(b_end - b)).astype(BF16))
        v_b = _heads(v.astype(BF16))
        for i in range(hp):
            st_ref[i, 0] = sts[i]
            state[i] = sts[i] * a_end[i] + _dot_tn(v_b[i], kd[i])

    return pl.pallas_call(
        body, grid=(nh // hp, nc),
        in_specs=[pl.BlockSpec((ch, wb), lambda h, n: (n, cq + h)),
                  pl.BlockSpec((ch, wb), lambda h, n: (n, cf + h)),
                  pl.BlockSpec((ch, wb), lambda h, n: (n, cv + h)),
                  pl.BlockSpec((ch, wb), lambda h, n: (n, cg + h)),
                  pl.BlockSpec((1, wb), lambda h, n: (0, h)),
                  pl.BlockSpec((1, wb), lambda h, n: (0, h))],
        out_specs=[pl.BlockSpec((ch, wb), lambda h, n: (n, h)),
                   pl.BlockSpec((hp, 1, hd, hd), lambda h, n: (h, n, 0, 0)),
                   pl.BlockSpec((ch, wb), lambda h, n: (n, h))],
        out_shape=[jax.ShapeDtypeStruct((t, w), F32), jax.ShapeDtypeStruct((nh, nc, hd, hd), F32),
                   jax.ShapeDtypeStruct((t, w), BF16)],
        scratch_shapes=[pltpu.VMEM((hp, hd, hd), F32)],
        name=name, compiler_params=_cp(("parallel", "arbitrary")))(u, u, u, u, lb, gain)


def _hgrn_bwd(u, lb, gain, states, o, dp, off, name):
    t = u.shape[0]
    w = lb.shape[1]
    hd, ch, hp = HGRN_HEAD_DIM, HGRN_CHUNK, HGRN_HEADS_PER_STEP_BWD
    nh, nc = w // hd, t // ch
    wb = hp * hd
    cq, cf, cv, cg = off["a_q"] // wb, off["a_f"] // wb, off["a_i"] // wb, off["a_g"] // wb

    def body(q_ref, f_ref, v_ref, g_ref, o_ref, dp_ref, st_ref, lb_ref, gain_ref,
             dq_ref, df_ref, dv_ref, dg_ref, dlb_ref, dstate):
        @pl.when(pl.program_id(1) == 0)
        def _():
            dstate[...] = jnp.zeros_like(dstate)
            dlb_ref[...] = jnp.zeros_like(dlb_ref)

        silu, dsilu = _silu_parts(g_ref[...])
        o_v, dp_v, gain_row = o_ref[...], dp_ref[...], gain_ref[...]
        rms = _head_rms(o_v)
        nrm = o_v * rms
        dg_ref[...] = (dp_v * nrm * gain_row * dsilu).astype(BF16)
        dlb_ref[1:2, :] += jnp.sum(dp_v * nrm * silu, axis=0, keepdims=True)
        dn = dp_v * gain_row * silu
        do_v = rms * (dn - nrm * (_head_lane_sum(dn * nrm) * (1.0 / HGRN_HEAD_DIM)))

        rows = lax.broadcasted_iota(jnp.int32, (ch, wb), 0)
        lb_row = lb_ref[...]
        q_raw = q_ref[...]
        qf, sq, sg, f = _hgrn_gates(q_raw, f_ref[...], lb_row)
        k = 1.0 - f
        b = _exact_tri_matmul(_tri(True), jnp.log(f))
        a = jnp.exp(b)
        b_end = b[ch - 1:ch, :]
        a_end = jnp.exp(b_end)
        to_end = jnp.exp(b_end - b)
        v = v_ref[...]
        st0 = [st_ref[i, 0] for i in range(hp)]
        ds = [dstate[i] for i in range(hp)]
        st0_b = [s_.astype(BF16) for s_ in st0]
        ds_b = [s_.astype(BF16) for s_ in ds]
        do_b, v_b, kd_b, qa_b = do_v.astype(BF16), v.astype(BF16), (k * to_end).astype(BF16), (qf * a).astype(BF16)

        dq_inter = a * _per_head(_dot_nn, do_b, st0_b)
        dk_end = to_end * _per_head(_dot_nn, v_b, ds_b)
        dv_end = _per_head(_dot_nt, kd_b, ds_b)
        a_end_h = _heads(a_end)
        st_end = [st0[i] * a_end_h[i] + _dot_tn(_heads(v_b)[i], _heads(kd_b)[i]) for i in range(hp)]
        db_end = jnp.concatenate([jnp.sum(ds[i] * st_end[i], axis=0, keepdims=True) for i in range(hp)], axis=1)
        ds_new = [ds[i] * a_end_h[i] + _dot_tn(_heads(do_b)[i], _heads(qa_b)[i]) for i in range(hp)]

        dq_intra, dk_intra, dv_intra = _hgrn_intra_bwd(qf, k, v, b, do_v)
        dqf = dq_inter + dq_intra
        dk = dk_end + dk_intra
        dv = dv_end + dv_intra
        db = qf * dqf - k * dk
        db = db + jnp.where(rows == ch - 1, db_end, 0.0)
        dg = _exact_tri_matmul(_tri(False), db)
        df = dg / f - dk
        for i in range(hp):
            dstate[i] = ds_new[i]
        dq_ref[...] = (dqf * (HGRN_HEAD_DIM ** -0.5) * (sq + q_raw * sq * (1.0 - sq))).astype(BF16)
        df_ref[...] = (df * (1.0 - lb_row) * sg * (1.0 - sg)).astype(BF16)
        dv_ref[...] = dv.astype(BF16)
        dlb_ref[0:1, :] += jnp.sum(df * (1.0 - sg), axis=0, keepdims=True)

    rev = lambda n: nc - 1 - n
    tile = lambda c: pl.BlockSpec((ch, wb), lambda h, n, c=c: (rev(n), c + h))
    return pl.pallas_call(
        body, grid=(nh // hp, nc),
        in_specs=[tile(cq), tile(cf), tile(cv), tile(cg), tile(0), tile(0),
                  pl.BlockSpec((hp, 1, hd, hd), lambda h, n: (h, rev(n), 0, 0)),
                  pl.BlockSpec((1, wb), lambda h, n: (0, h)), pl.BlockSpec((1, wb), lambda h, n: (0, h))],
        out_specs=[tile(0), tile(0), tile(0), tile(0), pl.BlockSpec((SUBLANES, wb), lambda h, n: (0, h))],
        out_shape=[jax.ShapeDtypeStruct((t, w), BF16)] * 4 + [jax.ShapeDtypeStruct((SUBLANES, w), F32)],
        scratch_shapes=[pltpu.VMEM((hp, hd, hd), F32)],
        name=name, compiler_params=_cp(("parallel", "arbitrary")))(u, u, u, u, o, dp, states, lb, gain)


def _bucket_map():
    i = np.arange(WINDOW)[:, None]
    j = np.arange(2 * WINDOW)[None, :]
    dist = np.clip(WINDOW + i - j, 0, WINDOW - 1)
    max_exact = N_BUCKETS // 2
    logd = (np.log(np.maximum(dist, 1).astype(np.float32) / max_exact) / math.log(MAX_DISTANCE / max_exact))
    large = np.minimum(max_exact + (logd.astype(np.float32) * (N_BUCKETS - max_exact)).astype(np.int32), N_BUCKETS - 1)
    return np.where(dist < max_exact, dist, large).astype(np.int32)


def _bias_table(rel_bias, n_heads):
    bucket = jnp.asarray(_bucket_map())

    def body(rb_ref, bk_ref, o_ref):
        bk = bk_ref[...]
        i = lax.broadcasted_iota(jnp.int32, (WINDOW, 2 * WINDOW), 0)
        j = lax.broadcasted_iota(jnp.int32, (WINDOW, 2 * WINDOW), 1)
        band = ((j >= WINDOW) & (j - WINDOW <= i)) | ((j < WINDOW) & (j > i))
        for h in range(n_heads):
            def step(bi, acc):
                return jnp.where(bk == bi, rb_ref[bi, h], acc)
            table = lax.fori_loop(0, N_BUCKETS, step, jnp.zeros((WINDOW, 2 * WINDOW), F32))
            o_ref[h] = jnp.where(band, table, NEG_INF)

    return pl.pallas_call(
        body, in_specs=[pl.BlockSpec(memory_space=pltpu.SMEM), pl.BlockSpec(memory_space=pltpu.VMEM)],
        out_specs=pl.BlockSpec(memory_space=pltpu.VMEM),
        out_shape=jax.ShapeDtypeStruct((n_heads, WINDOW, 2 * WINDOW), F32), name="bias_table",
        compiler_params=_cp())(rel_bias, bucket)


def _bias_grad(dbias, n_heads):
    bucket = jnp.asarray(_bucket_map())

    def body(db_ref, bk_ref, o_ref):
        bk = bk_ref[...]
        lane = lax.broadcasted_iota(jnp.int32, (1, LANES), 1)

        def step(bi, carry):
            row = jnp.zeros((1, LANES), F32)
            for h in range(n_heads):
                val = jnp.sum(jnp.where(bk == bi, db_ref[h], 0.0))
                row = jnp.where(lane == h, val, row)
            o_ref[pl.ds(bi, 1), :] = row
            return carry

        lax.fori_loop(0, N_BUCKETS, step, 0)

    return pl.pallas_call(
        body, in_specs=[pl.BlockSpec(memory_space=pltpu.VMEM), pl.BlockSpec(memory_space=pltpu.VMEM)],
        out_specs=pl.BlockSpec(memory_space=pltpu.VMEM),
        out_shape=jax.ShapeDtypeStruct((N_BUCKETS, LANES), F32), name="bias_grad",
        compiler_params=_cp())(dbias, bucket)


def _no_prev_block(n, grp):
    j = lax.broadcasted_iota(jnp.int32, (grp * WINDOW, 2 * WINDOW), 1)
    return (j < WINDOW) & (n == 0)


def _attn_probs(no_prev, q_ref, kp_ref, kc_ref, bias_ref, sink_ref, hh, grp):
    ad, wn = ATTN_HEAD_DIM, WINDOW
    ksl = slice(hh * ad, (hh + 1) * ad)
    kw = jnp.concatenate([kp_ref[:, ksl], kc_ref[:, ksl]], axis=0).astype(BF16)
    qs = jnp.concatenate([q_ref[:, (hh * grp + g) * ad:(hh * grp + g + 1) * ad] for g in range(grp)], axis=0).astype(BF16)
    s = _dot_nt(qs, kw) * (ad ** -0.5) + bias_ref[hh]
    s = jnp.where(no_prev, NEG_INF, s)
    rr = lax.broadcasted_iota(jnp.int32, (grp * wn, 1), 0) >> WINDOW_SHIFT
    sink = jnp.zeros((grp * wn, 1), F32)
    for g in range(grp):
        sink = jnp.where(rr == g, sink_ref[hh * grp + g], sink)
    m = jnp.maximum(jnp.max(s, axis=1, keepdims=True), sink)
    p = jnp.exp(s - m)
    es = jnp.exp(sink - m)
    inv = 1.0 / (jnp.sum(p, axis=1, keepdims=True) + es)
    return qs, kw, p * inv, es * inv


GATE_BLOCK = 512


def _attn_fwd(u, bias_g, sinks, off, w, name):
    t = u.shape[0]
    wn, ad, kvw, gb = WINDOW, ATTN_HEAD_DIM, KV_WIDTH, GATE_BLOCK
    grp = (w // ad) // ATTN_KV_HEADS
    nb = t // wn
    n_gb = w // gb
    cq, ck, cv, cg = off["b_q"] // w, off["b_k"] // kvw, off["b_v"] // kvw, off["b_g"] // gb

    def body(q_ref, kp_ref, kc_ref, vp_ref, vc_ref, bias_ref, sink_ref, *rest):
        g_refs, (o_ref, p_ref) = rest[:n_gb], rest[n_gb:]
        no_prev = _no_prev_block(pl.program_id(0), grp)
        for hh in range(ATTN_KV_HEADS):
            _, _, p, _ = _attn_probs(no_prev, q_ref, kp_ref, kc_ref, bias_ref, sink_ref, hh, grp)
            ksl = slice(hh * ad, (hh + 1) * ad)
            vw = jnp.concatenate([vp_ref[:, ksl], vc_ref[:, ksl]], axis=0).astype(BF16)
            o = _dot_nn(p.astype(BF16), vw)
            for g in range(grp):
                o_ref[:, (hh * grp + g) * ad:(hh * grp + g + 1) * ad] = o[g * wn:(g + 1) * wn]
        for i in range(n_gb):
            sl = slice(i * gb, (i + 1) * gb)
            silu, _ = _silu_parts(g_refs[i][...])
            p_ref[:, sl] = (o_ref[:, sl] * silu).astype(BF16)

    prev = lambda n: jnp.maximum(n - 1, 0)
    row = pl.BlockSpec((wn, w), lambda n: (n, 0))
    return pl.pallas_call(
        body, grid=(nb,),
        in_specs=[pl.BlockSpec((wn, w), lambda n: (n, cq)),
                  pl.BlockSpec((wn, kvw), lambda n: (prev(n), ck)), pl.BlockSpec((wn, kvw), lambda n: (n, ck)),
                  pl.BlockSpec((wn, kvw), lambda n: (prev(n), cv)), pl.BlockSpec((wn, kvw), lambda n: (n, cv)),
                  pl.BlockSpec((ATTN_KV_HEADS, grp * wn, 2 * wn), lambda n: (0, 0, 0)),
                  pl.BlockSpec(memory_space=pltpu.SMEM)]
        + [pl.BlockSpec((wn, gb), lambda n, i=i: (n, cg + i)) for i in range(n_gb)],
        out_specs=[row, row],
        out_shape=[jax.ShapeDtypeStruct((t, w), F32), jax.ShapeDtypeStruct((t, w), BF16)], name=name,
        compiler_params=_cp(("parallel",)))(u, u, u, u, u, bias_g, sinks, *([u] * n_gb))


def _attn_bwd(u, o, dp, bias_g, sinks, off, w, name):
    t = u.shape[0]
    wn, ad, kvw, gb = WINDOW, ATTN_HEAD_DIM, KV_WIDTH, GATE_BLOCK
    grp = (w // ad) // ATTN_KV_HEADS
    nb = t // wn
    n_gb = w // gb
    cq, ck, cv, cg = off["b_q"] // w, off["b_k"] // kvw, off["b_v"] // kvw, off["b_g"] // gb

    def body(q_ref, kp_ref, kc_ref, vp_ref, vc_ref, o_ref, dp_ref, bias_ref, sink_ref, *rest):
        g_refs = rest[:n_gb]
        dq_ref, dkv_ref, dbias_ref, dsink_ref, dg_ref, do_ref, carry = rest[n_gb:]
        n = pl.program_id(0)

        @pl.when(n == 0)
        def _():
            dbias_ref[...] = jnp.zeros_like(dbias_ref)
            dsink_ref[...] = jnp.zeros_like(dsink_ref)
            carry[...] = jnp.zeros_like(carry)

        @pl.when(n == nb)
        def _():
            dkv_ref[...] = carry[...].astype(BF16)

        @pl.when(n < nb)
        def _():
            block(n, q_ref, kp_ref, kc_ref, vp_ref, vc_ref, o_ref, dp_ref, bias_ref, sink_ref, g_refs,
                  dq_ref, dkv_ref, dbias_ref, dsink_ref, dg_ref, do_ref, carry)

    def block(n, q_ref, kp_ref, kc_ref, vp_ref, vc_ref, o_ref, dp_ref, bias_ref, sink_ref, g_refs,
              dq_ref, dkv_ref, dbias_ref, dsink_ref, dg_ref, do_ref, carry):
        for i in range(n_gb):
            sl = slice(i * gb, (i + 1) * gb)
            silu, dsilu = _silu_parts(g_refs[i][...])
            dp_v = dp_ref[:, sl]
            do_ref[:, sl] = dp_v * silu
            dg_ref[:, sl] = (dp_v * o_ref[:, sl] * dsilu).astype(BF16)

        lane = lax.broadcasted_iota(jnp.int32, (1, LANES), 1)
        rr = lax.broadcasted_iota(jnp.int32, (grp * wn, 1), 0) >> WINDOW_SHIFT
        dsink_row = jnp.zeros((1, LANES), F32)
        no_prev = _no_prev_block(n, grp)
        for hh in range(ATTN_KV_HEADS):
            qs, kw, p, psink = _attn_probs(no_prev, q_ref, kp_ref, kc_ref, bias_ref, sink_ref, hh, grp)
            ksl = slice(hh * ad, (hh + 1) * ad)
            vw = jnp.concatenate([vp_ref[:, ksl], vc_ref[:, ksl]], axis=0).astype(BF16)
            hs = [slice((hh * grp + g) * ad, (hh * grp + g + 1) * ad) for g in range(grp)]
            dos = jnp.concatenate([do_ref[:, sl] for sl in hs], axis=0)
            os_ = jnp.concatenate([o_ref[:, sl] for sl in hs], axis=0)
            delta = jnp.sum(dos * os_, axis=1, keepdims=True)
            dos_b = dos.astype(BF16)
            dp = _dot_nt(dos_b, vw)
            ds = p * (dp - delta)
            dbias_ref[hh] += ds
            sd = psink * delta
            for g in range(grp):
                val = -jnp.sum(jnp.where(rr == g, sd, 0.0))
                dsink_row = jnp.where(lane == hh * grp + g, val, dsink_row)
            ds_b = (ds * (ad ** -0.5)).astype(BF16)
            dq = _dot_nn(ds_b, kw)
            for g in range(grp):
                dq_ref[:, hs[g]] = dq[g * wn:(g + 1) * wn].astype(BF16)
            dkw = _dot_tn(ds_b, qs)
            dvw = _dot_tn(p.astype(BF16), dos_b)
            vsl = slice(kvw + hh * ad, kvw + (hh + 1) * ad)
            dkv_ref[:, ksl] = (carry[:, ksl] + dkw[:wn]).astype(BF16)
            dkv_ref[:, vsl] = (carry[:, vsl] + dvw[:wn]).astype(BF16)
            carry[:, ksl] = dkw[wn:]
            carry[:, vsl] = dvw[wn:]
        dsink_ref[0:1, :] += dsink_row

    cur = lambda n: jnp.minimum(n, nb - 1)
    prev = lambda n: jnp.maximum(cur(n) - 1, 0)
    row = pl.BlockSpec((wn, w), lambda n: (cur(n), 0))
    return pl.pallas_call(
        body, grid=(nb + 1,),
        in_specs=[pl.BlockSpec((wn, w), lambda n: (cur(n), cq)),
                  pl.BlockSpec((wn, kvw), lambda n: (prev(n), ck)), pl.BlockSpec((wn, kvw), lambda n: (cur(n), ck)),
                  pl.BlockSpec((wn, kvw), lambda n: (prev(n), cv)), pl.BlockSpec((wn, kvw), lambda n: (cur(n), cv)),
                  row, row,
                  pl.BlockSpec((ATTN_KV_HEADS, grp * wn, 2 * wn), lambda n: (0, 0, 0)),
                  pl.BlockSpec(memory_space=pltpu.SMEM)]
        + [pl.BlockSpec((wn, gb), lambda n, i=i: (cur(n), cg + i)) for i in range(n_gb)],
        out_specs=[row, pl.BlockSpec((wn, 2 * kvw), lambda n: (jnp.maximum(n - 1, 0), 0)),
                   pl.BlockSpec((ATTN_KV_HEADS, grp * wn, 2 * wn), lambda n: (0, 0, 0)),
                   pl.BlockSpec((SUBLANES, LANES), lambda n: (0, 0)), row],
        out_shape=[jax.ShapeDtypeStruct((t, w), BF16), jax.ShapeDtypeStruct((t, 2 * kvw), BF16),
                   jax.ShapeDtypeStruct((ATTN_KV_HEADS, grp * wn, 2 * wn), F32), jax.ShapeDtypeStruct((SUBLANES, LANES), F32),
                   jax.ShapeDtypeStruct((t, w), BF16)],
        scratch_shapes=[pltpu.VMEM((wn, w), F32), pltpu.VMEM((wn, 2 * kvw), F32)],
        name=name, compiler_params=_cp(("arbitrary",)))(u, u, u, u, u, o, dp, bias_g, sinks, *([u] * n_gb))


def _shift_down(h, tail, k, rows):
    nt = tail.shape[0]
    out = pltpu.roll(h, k, 0)
    for r in range(k):
        out = jnp.where(rows == r, tail[nt - k + r:nt - k + r + 1, :], out)
    return out


def _shift_up(h, head, k, rows):
    tt = h.shape[0]
    out = pltpu.roll(h, tt - k, 0)
    for r in range(k):
        out = jnp.where(rows == tt - k + r, head[r:r + 1, :], out)
    return out


def _conv_fwd(u, conv_w, off, w, name):
    t = u.shape[0]
    wb = 512
    c = lambda nme: off[nme] // wb

    def body(i, nt, cb_ref, cc_ref, ccp_ref, cx_ref, cxp_ref, cg_ref, w_ref, p_ref):
        h = cc_ref[...] * cx_ref[...]
        hp = jnp.where(i > 0, ccp_ref[...] * cxp_ref[...], 0.0)
        rows = lax.broadcasted_iota(jnp.int32, h.shape, 0)
        y = w_ref[0:1, :] * _shift_down(h, hp, 2, rows) + w_ref[1:2, :] * _shift_down(h, hp, 1, rows) + w_ref[2:3, :] * h
        silu, _ = _silu_parts(cg_ref[...])
        p_ref[...] = (cb_ref[...] * y * silu).astype(BF16)

    return _ew(body, name, t, w // wb, wb,
               [(u, "tile", c("c_b")), (u, "tile", c("c_c")), (u, "prev", c("c_c")), (u, "tile", c("c_x")),
                (u, "prev", c("c_x")), (u, "tile", c("c_g")), (conv_w, "row", 0)], [BF16])[0]


def _conv_bwd(dp, u, conv_w, off, w, name):
    t = u.shape[0]
    wb = 512
    c = lambda nme: off[nme] // wb

    def body(i, nt, dp_ref, dpn_ref, cb_ref, cbn_ref, cg_ref, cgn_ref, cc_ref, ccp_ref, cx_ref, cxp_ref, w_ref,
             dcb_ref, dcc_ref, dcx_ref, dcg_ref, acc_ref):
        @pl.when(i == 0)
        def _():
            acc_ref[...] = jnp.zeros_like(acc_ref)

        cc, cx, cb = cc_ref[...], cx_ref[...], cb_ref[...]
        h = cc * cx
        hp = jnp.where(i > 0, ccp_ref[...] * cxp_ref[...], 0.0)
        rows = lax.broadcasted_iota(jnp.int32, h.shape, 0)
        h1 = _shift_down(h, hp, 1, rows)
        h2 = _shift_down(h, hp, 2, rows)
        w0, w1, w2 = w_ref[0:1, :], w_ref[1:2, :], w_ref[2:3, :]
        y = w0 * h2 + w1 * h1 + w2 * h
        silu, dsilu = _silu_parts(cg_ref[...])
        dp_v = dp_ref[...]
        dcg_ref[...] = (dp_v * cb * y * dsilu).astype(BF16)
        dcb_ref[...] = (dp_v * y * silu).astype(BF16)
        dy = dp_v * cb * silu
        silu_n, _ = _silu_parts(cgn_ref[...])
        dyn = jnp.where(i < nt - 1, dpn_ref[...] * cbn_ref[...] * silu_n, 0.0)
        dh = w2 * dy + w1 * _shift_up(dy, dyn, 1, rows) + w0 * _shift_up(dy, dyn, 2, rows)
        dcc_ref[...] = (dh * cx).astype(BF16)
        dcx_ref[...] = (dh * cc).astype(BF16)
        acc_ref[0:1, :] += jnp.sum(dy * h2, axis=0, keepdims=True)
        acc_ref[1:2, :] += jnp.sum(dy * h1, axis=0, keepdims=True)
        acc_ref[2:3, :] += jnp.sum(dy * h, axis=0, keepdims=True)

    return _ew(body, name, t, w // wb, wb,
               [(dp, "tile", 0), (dp, "next", 0), (u, "tile", c("c_b")), (u, "next", c("c_b")),
                (u, "tile", c("c_g")), (u, "next", c("c_g")), (u, "tile", c("c_c")), (u, "prev", c("c_c")),
                (u, "tile", c("c_x")), (u, "prev", c("c_x")), (conv_w, "row", 0)],
               [BF16] * 4, accs=[SUBLANES])


def _proj_merge(p, w, ya, yb, u, off, name):
    t, k = p.shape
    d = w.shape[1]
    gb = GATE_BLOCK
    nb = d // gb
    tt = _tile(t, 256)
    cols = [off[nme] // gb for nme in ("m_a", "m_b", "m_c")]

    def body(p_ref, w_ref, ya_ref, yb_ref, *rest):
        m_refs, (yc_ref, mg_ref) = rest[:3 * nb], rest[3 * nb:]
        yc = _dot_nn(p_ref[...], w_ref[...])
        yc_ref[...] = yc
        for j in range(nb):
            sl = slice(j * gb, (j + 1) * gb)
            mg_ref[:, sl] = (_sigmoid(m_refs[j][...]) * ya_ref[:, sl] + _sigmoid(m_refs[nb + j][...]) * yb_ref[:, sl]
                             + _sigmoid(m_refs[2 * nb + j][...]) * yc[:, sl]).astype(BF16)

    row = pl.BlockSpec((tt, d), lambda i: (i, 0))
    gates = [pl.BlockSpec((tt, gb), lambda i, c=c0 + j: (i, c)) for c0 in cols for j in range(nb)]
    return pl.pallas_call(
        body, grid=(t // tt,),
        in_specs=[pl.BlockSpec((tt, k), lambda i: (i, 0)), pl.BlockSpec((k, d), lambda i: (0, 0)), row, row] + gates,
        out_specs=[row, row],
        out_shape=[jax.ShapeDtypeStruct((t, d), F32), jax.ShapeDtypeStruct((t, d), BF16)],
        name=name, compiler_params=_cp(("parallel",)))(p, w, ya, yb, *([u] * (3 * nb)))


def _lower_bounds(lb_param):
    def body(p_ref, o_ref):
        p = p_ref[...]
        e = jnp.exp(p - jnp.max(p, axis=0, keepdims=True))
        soft = e / jnp.sum(e, axis=0, keepdims=True)
        acc = jnp.zeros_like(soft[0:1])
        o_ref[0:1, :] = acc
        for l in range(1, DEPTH):
            acc = acc + soft[l:l + 1]
            o_ref[l:l + 1, :] = acc

    return pl.pallas_call(body, out_shape=jax.ShapeDtypeStruct(lb_param.shape, F32), name="lower_bounds",
                          compiler_params=_cp())(lb_param)


def _lower_bounds_bwd(lb_param, dlower):
    def body(p_ref, d_ref, o_ref):
        p = p_ref[...]
        e = jnp.exp(p - jnp.max(p, axis=0, keepdims=True))
        soft = e / jnp.sum(e, axis=0, keepdims=True)
        dl = d_ref[...]
        ds = [jnp.zeros_like(dl[0:1])]
        for j in range(1, DEPTH):
            acc = dl[j:j + 1]
            for l in range(j + 1, DEPTH):
                acc = acc + dl[l:l + 1]
            ds.append(acc)
        inner = ds[0] * soft[0:1]
        for j in range(1, DEPTH):
            inner = inner + ds[j] * soft[j:j + 1]
        for j in range(DEPTH):
            o_ref[j:j + 1, :] = soft[j:j + 1] * (ds[j] - inner)

    return pl.pallas_call(body, out_shape=jax.ShapeDtypeStruct(lb_param.shape, F32), name="lower_bounds_bwd",
                          compiler_params=_cp())(lb_param, dlower)


def _exchange(arrays, scatter, name, chips=False):
    n_arr = len(arrays)
    n_slot = N_DEV // 2 if chips else N_DEV

    def body(*refs):
        srcs, dsts = refs[:n_arr], refs[n_arr:2 * n_arr]
        send_sems, recv_sems, local_sems = refs[2 * n_arr:]
        me = (2 * lax.axis_index("x") + lax.axis_index("y") if chips
              else 4 * lax.axis_index("x") + 2 * lax.axis_index("y") + lax.axis_index("c"))
        copies = _peer_copies(srcs, dsts, send_sems, recv_sems, scatter, chips)
        for a in range(n_arr):
            copies.append(pltpu.make_async_copy(srcs[a].at[me] if scatter else srcs[a], dsts[a].at[me], local_sems.at[a]))
        for cp in copies:
            cp.start()
        for cp in copies:
            cp.wait()

    out_shape = [jax.ShapeDtypeStruct(a.shape if scatter else (n_slot,) + a.shape, a.dtype) for a in arrays]
    anyspec = pl.BlockSpec(memory_space=pl.ANY)
    res = pl.pallas_call(
        body, in_specs=[anyspec] * n_arr, out_specs=[anyspec] * n_arr, out_shape=out_shape,
        scratch_shapes=[pltpu.SemaphoreType.DMA((n_arr * (n_slot - 1),)), pltpu.SemaphoreType.DMA((n_arr * (n_slot - 1),)),
                        pltpu.SemaphoreType.DMA((n_arr,))],
        name=name)(*arrays)
    return list(res)


def _peer_copies(srcs, lands, send_sems, recv_sems, scatter, chips=False):
    x, y, c = lax.axis_index("x"), lax.axis_index("y"), lax.axis_index("c")
    flips = [k for k in range(1, N_DEV) if not (chips and k & 1)]
    slot = (lambda px, py, pc: 2 * px + py) if chips else (lambda px, py, pc: 4 * px + 2 * py + pc)
    copies = []
    for a in range(len(srcs)):
        for i, k in enumerate(flips):
            px = 1 - x if k & 4 else x
            py = 1 - y if k & 2 else y
            pc = 1 - c if k & 1 else c
            src = srcs[a].at[slot(px, py, pc)] if scatter else srcs[a]
            copies.append(pltpu.make_async_remote_copy(
                src_ref=src, dst_ref=lands[a].at[slot(x, y, c)],
                send_sem=send_sems.at[a * len(flips) + i], recv_sem=recv_sems.at[a * len(flips) + i],
                device_id=(px, py, pc), device_id_type=pl.DeviceIdType.MESH))
    return copies


def _gather_two_level(arrays, name):
    n_arr = len(arrays)
    per = N_DEV - 1

    def body(*refs):
        srcs, outs = refs[:n_arr], refs[n_arr:2 * n_arr]
        send_sems, recv_sems, local_sems = refs[2 * n_arr:]
        x, y, c = lax.axis_index("x"), lax.axis_index("y"), lax.axis_index("c")
        me, sibling = (x, y, c), (x, y, 1 - c)
        chips = [(1 - x, y), (x, 1 - y), (1 - x, 1 - y)]

        def copy(a, k, block, to, src=None):
            dst = outs[a].at[4 * block[0] + 2 * block[1] + block[2]]
            return pltpu.make_async_remote_copy(
                src_ref=dst if src is None else src, dst_ref=dst,
                send_sem=send_sems.at[a * per + k], recv_sem=recv_sems.at[a * per + k],
                device_id=to, device_id_type=pl.DeviceIdType.MESH)

        own, first, passed = [], [], []
        for a in range(n_arr):
            own.append(pltpu.make_async_copy(srcs[a], outs[a].at[4 * x + 2 * y + c], local_sems.at[a]))
            first.append(copy(a, 0, me, sibling, src=srcs[a]))
            first += [copy(a, 1 + j, me, (*chip, c), src=srcs[a]) for j, chip in enumerate(chips)]
        for cp in own + first:
            cp.start()
        for a in range(n_arr):
            for j, chip in enumerate(chips):
                copy(a, 1 + j, (*chip, c), me).wait_recv()
                passed.append(copy(a, 4 + j, (*chip, c), sibling))
                passed[-1].start()
        for a in range(n_arr):
            copy(a, 0, sibling, me).wait_recv()
            for j, chip in enumerate(chips):
                copy(a, 4 + j, (*chip, 1 - c), me).wait_recv()
        for cp in first + passed:
            cp.wait_send()
        for cp in own:
            cp.wait()

    anyspec = pl.BlockSpec(memory_space=pl.ANY)
    res = pl.pallas_call(
        body, in_specs=[anyspec] * n_arr, out_specs=[anyspec] * n_arr,
        out_shape=[jax.ShapeDtypeStruct((N_DEV,) + a.shape, a.dtype) for a in arrays],
        scratch_shapes=[pltpu.SemaphoreType.DMA((n_arr * per,)), pltpu.SemaphoreType.DMA((n_arr * per,)),
                        pltpu.SemaphoreType.DMA((n_arr,))],
        name=name)(*arrays)
    return list(res)


def _sibling_swap(arrays, name):
    n_arr = len(arrays)
    n_chip = N_DEV // 2

    def body(*refs):
        srcs, outs = refs[:n_arr], refs[n_arr:2 * n_arr]
        send_sems, recv_sems = refs[2 * n_arr:]
        x, y, c = lax.axis_index("x"), lax.axis_index("y"), lax.axis_index("c")
        copies = []
        for a in range(n_arr):
            for j in range(n_chip):
                copies.append(pltpu.make_async_remote_copy(
                    src_ref=srcs[a].at[2 * j + 1 - c], dst_ref=outs[a].at[j],
                    send_sem=send_sems.at[a * n_chip + j], recv_sem=recv_sems.at[a * n_chip + j],
                    device_id=(x, y, 1 - c), device_id_type=pl.DeviceIdType.MESH))
        for cp in copies:
            cp.start()
        for cp in copies:
            cp.wait()

    anyspec = pl.BlockSpec(memory_space=pl.ANY)
    res = pl.pallas_call(
        body, in_specs=[anyspec] * n_arr, out_specs=[anyspec] * n_arr,
        out_shape=[jax.ShapeDtypeStruct((n_chip,) + a.shape[1:], a.dtype) for a in arrays],
        scratch_shapes=[pltpu.SemaphoreType.DMA((n_arr * n_chip,)), pltpu.SemaphoreType.DMA((n_arr * n_chip,))],
        name=name)(*arrays)
    return list(res)


def _pair_sum(send, stage, core, name):
    _, r, c = send.shape
    n_chip = stage.shape[0]
    tr = _tile(r, 128)

    def body(core_ref, a_ref, b_ref, o_ref):
        o_ref[...] = a_ref[...] + b_ref[...]

    return pl.pallas_call(
        body,
        grid_spec=pltpu.PrefetchScalarGridSpec(
            num_scalar_prefetch=1, grid=(n_chip, r // tr),
            in_specs=[pl.BlockSpec((1, tr, c), lambda j, i, core_ref: (2 * j + core_ref[0], i, 0)),
                      pl.BlockSpec((1, tr, c), lambda j, i, core_ref: (j, i, 0))],
            out_specs=pl.BlockSpec((1, tr, c), lambda j, i, core_ref: (j, i, 0))),
        out_shape=jax.ShapeDtypeStruct(stage.shape, F32), name=name,
        compiler_params=_cp(("parallel", "parallel")))(core, send, stage)


_HBM_SPEC = pl.BlockSpec(memory_space=pltpu.HBM)
_SEM_SPEC = pl.BlockSpec(memory_space=pltpu.SEMAPHORE)
_ANY_SPEC = pl.BlockSpec(memory_space=pl.ANY)
_DATAFLOW = pltpu.SideEffectType.DATAFLOW_SIDE_EFFECTING


def _exchange_start(arrays, scatter, name, dep=None, chips=False):
    n_arr = len(arrays)
    n_slot = N_DEV // 2 if chips else N_DEV
    n_sem = n_arr * (n_slot - 1)
    me = (2 * lax.axis_index("x") + lax.axis_index("y") if chips
          else 4 * lax.axis_index("x") + 2 * lax.axis_index("y") + lax.axis_index("c"))
    lands = []
    for a in arrays:
        own = lax.dynamic_index_in_dim(a, me, 0, keepdims=False) if scatter else a
        shape = a.shape if scatter else (n_slot,) + a.shape
        lands.append(lax.dynamic_update_index_in_dim(lax.empty(shape, a.dtype), own, me, 0))
    dep_specs, dep_args = _dep_specs(dep)

    def body(*refs):
        srcs, lnds = refs[:n_arr], refs[n_arr:2 * n_arr]
        outs = refs[2 * n_arr + len(dep_args):]
        send_sems, recv_sems, token = outs[0], outs[1], outs[2 + 2 * n_arr]
        for cp in _peer_copies(srcs, lnds, send_sems, recv_sems, scatter, chips):
            cp.start()
        token[...] = jnp.zeros_like(token)

    thru = [pltpu.HBM(a.shape, a.dtype) for a in list(arrays) + lands]
    return pl.pallas_call(
        body, name=name,
        out_shape=(pltpu.SemaphoreType.DMA((n_sem,)), pltpu.SemaphoreType.DMA((n_sem,)), *thru,
                   jax.ShapeDtypeStruct((SUBLANES, LANES), F32)),
        in_specs=[_HBM_SPEC] * (2 * n_arr) + dep_specs,
        out_specs=(_SEM_SPEC, _SEM_SPEC, *[_HBM_SPEC] * (2 * n_arr), pl.BlockSpec(memory_space=pltpu.VMEM)),
        input_output_aliases={i: 2 + i for i in range(2 * n_arr)},
        compiler_params=pltpu.CompilerParams(has_side_effects=_DATAFLOW),
    )(*[pltpu.with_memory_space_constraint(a, pltpu.HBM) for a in list(arrays) + lands], *dep_args)


def _exchange_wait(started, scatter, name, after, chips=False):
    send_sems, recv_sems = started[0], started[1]
    thru = list(started[2:-1])
    n_arr = len(thru) // 2

    def body(*refs):
        srcs, lnds = refs[:n_arr], refs[n_arr:2 * n_arr]
        for cp in _peer_copies(srcs, lnds, refs[2 * n_arr], refs[2 * n_arr + 1], scatter, chips):
            cp.wait_send()
            cp.wait_recv()

    res = pl.pallas_call(
        body, name=name, out_shape=tuple(pltpu.HBM(a.shape, a.dtype) for a in thru),
        in_specs=[_HBM_SPEC] * (2 * n_arr) + [_SEM_SPEC, _SEM_SPEC, _ANY_SPEC],
        out_specs=tuple([_HBM_SPEC] * (2 * n_arr)),
        input_output_aliases={i: i for i in range(2 * n_arr)},
        compiler_params=pltpu.CompilerParams(has_side_effects=_DATAFLOW),
    )(*thru, send_sems, recv_sems, after)
    return list(res[n_arr:])


def _unshard_cols(g, name):
    nd, r, s = g.shape
    tr = _tile(r, 64)

    def body(i_ref, o_ref):
        for p in range(nd):
            o_ref[:, p * s:(p + 1) * s] = i_ref[p]

    return pl.pallas_call(
        body, grid=(r // tr,), in_specs=[pl.BlockSpec((nd, tr, s), lambda i: (0, i, 0))],
        out_specs=pl.BlockSpec((tr, nd * s), lambda i: (i, 0)),
        out_shape=jax.ShapeDtypeStruct((r, nd * s), g.dtype), name=name, compiler_params=_cp(("parallel",)))(g)


def _shard_cols(g, name):
    r, n = g.shape
    s = n // N_DEV
    tr = _tile(r, 64)

    def body(i_ref, o_ref):
        for p in range(N_DEV):
            o_ref[p] = i_ref[:, p * s:(p + 1) * s]

    return pl.pallas_call(
        body, grid=(r // tr,), in_specs=[pl.BlockSpec((tr, n), lambda i: (i, 0))],
        out_specs=pl.BlockSpec((N_DEV, tr, s), lambda i: (0, i, 0)),
        out_shape=jax.ShapeDtypeStruct((N_DEV, r, s), g.dtype), name=name, compiler_params=_cp(("parallel",)))(g)


def _slot_sum(slots, name):
    nd, r, c = slots.shape
    tr = _tile(r, 64)

    def body(s_ref, o_ref):
        acc = s_ref[0]
        for p in range(1, nd):
            acc = acc + s_ref[p]
        o_ref[...] = acc

    return pl.pallas_call(
        body, grid=(r // tr,), in_specs=[pl.BlockSpec((nd, tr, c), lambda i: (0, i, 0))],
        out_specs=pl.BlockSpec((tr, c), lambda i: (i, 0)),
        out_shape=jax.ShapeDtypeStruct((r, c), F32), name=name, compiler_params=_cp(("parallel",)))(slots)


def _adamw(w, g, m, v, name):
    r, c = w.shape
    tr = _tile(r, 256)
    c1 = 1.0 - ADAM_B1 ** ADAM_STEP
    c2 = 1.0 - ADAM_B2 ** ADAM_STEP

    def body(w_ref, g_ref, m_ref, v_ref, d_ref, nm_ref, nv_ref):
        gv = g_ref[...]
        nm = ADAM_B1 * m_ref[...] + (1.0 - ADAM_B1) * gv
        nv = ADAM_B2 * v_ref[...] + (1.0 - ADAM_B2) * (gv * gv)
        nm_ref[...] = nm
        nv_ref[...] = nv
        d_ref[...] = -ADAM_LR * ((nm / c1) / (jnp.sqrt(nv / c2) + ADAM_EPS) + ADAM_WD * w_ref[...])

    spec = pl.BlockSpec((tr, c), lambda i: (i, 0))
    return pl.pallas_call(
        body, grid=(r // tr,), in_specs=[spec] * 4, out_specs=[spec] * 3,
        out_shape=[jax.ShapeDtypeStruct((r, c), F32)] * 3, name=name, compiler_params=_cp(("parallel",)))(w, g, m, v)


def _forward_backward(x, target, weights_hook, grads_hook, lb_param, hgrn_norm_g, attn_sinks, rel_bias, ln_g, ln_b):
    t, d = x.shape
    w = d // 2
    off, n_in = _offsets(d)
    n_heads = w // ATTN_HEAD_DIM
    grp = n_heads // ATTN_KV_HEADS

    lower = _lower_bounds(lb_param)
    bias = _bias_table(rel_bias, n_heads)
    bias_g = bias.reshape(ATTN_KV_HEADS, grp * WINDOW, 2 * WINDOW)

    saved, weights = [], []
    xb = x.astype(BF16)
    for l in range(DEPTH):
        wl, token = weights_hook(l, x)
        weights.append(wl)
        s = {"x": x, "xb": xb}
        u = _mm_nt_cols(xb, wl["w_in_t"], f"in_proj", dep=token)
        s["u"] = u
        lb_l, gain_l, cw_l = lower[l:l + 1], hgrn_norm_g[l:l + 1], wl["conv_w"]
        o_a, states, p_a = _hgrn_fwd(u, lb_l, gain_l, off, f"hgrn_fwd")
        o_b, p_b = _attn_fwd(u, bias_g, attn_sinks[l], off, w, f"attn_fwd")
        p_c = _conv_fwd(u, cw_l, off, w, f"conv_fwd")
        y_a = _mm_nn(p_a, wl["w_proj_hgrn"], f"proj_a", tn=2048)
        y_b = _mm_nn(p_b, wl["w_proj_attn"], f"proj_b", tn=2048)
        y_c, merged = _proj_merge(p_c, wl["w_proj_conv"], y_a, y_b, u, off, f"proj_c_merge")
        x, xb, xhat, rstd = _out_proj_ln(merged, wl["w_out"], x, ln_g[l:l + 1], ln_b[l:l + 1], f"out_proj_ln")
        s.update(o_a=o_a, states=states, p_a=p_a, o_b=o_b, p_b=p_b, p_c=p_c, y_a=y_a, y_b=y_b, y_c=y_c,
                 merged=merged, xhat=xhat, rstd=rstd)
        saved.append(s)

    loss_acc, dx = _loss_head(x, target)

    d_ln, d_lower, d_gain, d_sink, d_conv = [None] * DEPTH, [None] * DEPTH, [None] * DEPTH, [None] * DEPTH, [None] * DEPTH
    dbias_total = None
    for l in reversed(range(DEPTH)):
        wl, s = weights[l], saved[l]
        u = s["u"]
        lb_l, gain_l, cw_l = lower[l:l + 1], hgrn_norm_g[l:l + 1], wl["conv_w"]
        dz, dzb, d_ln[l], dya, dyb, dyc, dma, dmb, dmc = _ln_bwd_out_proj_merge(
            dx, s["xhat"], s["rstd"], ln_g[l:l + 1], wl["w_out"], u, s["y_a"], s["y_b"], s["y_c"], off,
            f"ln_bwd_merge_bwd")
        g_out = _mm_tn(s["merged"], dzb, f"g_out", tn=2048)
        g_pa = _mm_tn(s["p_a"], dya, f"g_proj_a", tn=2048)
        g_pb = _mm_tn(s["p_b"], dyb, f"g_proj_b", tn=2048)
        g_pc = _mm_tn(s["p_c"], dyc, f"g_proj_c", tn=2048)
        dpa = _mm_nt(dya, wl["w_proj_hgrn"], f"d_p_a", tk=2048)
        dpb = _mm_nt(dyb, wl["w_proj_attn"], f"d_p_b", tk=2048)
        dpc = _mm_nt(dyc, wl["w_proj_conv"], f"d_p_c", tk=2048)
        d_aq, d_af, d_ai, d_ag, acc_a = _hgrn_bwd(u, lb_l, gain_l, s["states"], s["o_a"], dpa, off, f"hgrn_bwd")
        d_lower[l], d_gain[l] = acc_a[0:1], acc_a[1:2]
        d_bq, d_bkv, dbias_l, d_sink[l], d_bg = _attn_bwd(
            u, s["o_b"], dpb, bias_g, attn_sinks[l], off, w, f"attn_bwd")
        dbias_total = dbias_l if dbias_total is None else dbias_total + dbias_l
        d_cb, d_cc, d_cx, d_cg, d_conv[l] = _conv_bwd(dpc, u, cw_l, off, w, f"conv_bwd")
        du = jnp.concatenate([d_aq, d_af, d_ai, d_ag, d_bq, d_bkv, d_bg, d_cb, d_cc, d_cx, d_cg, dma, dmb, dmc], axis=1)
        g_in_t = _mm_tn_rows(du, s["xb"], f"g_in")
        token = grads_hook(l, {"w_in_t": g_in_t, "w_proj_hgrn": g_pa, "w_proj_attn": g_pb, "w_proj_conv": g_pc, "w_out": g_out})
        dx = _mm_nn_acc(du, wl["w_in_t"], f"d_x", add=dz, add_scale=ALPHA, dep=token)

    d_lower_all = jnp.concatenate([a[0:1] for a in d_lower], axis=0)
    small = {
        "lb_param": _lower_bounds_bwd(lb_param, d_lower_all),
        "hgrn_norm_g": jnp.concatenate([a[0:1] for a in d_gain], axis=0),
        "attn_sinks": jnp.concatenate([a[0:1, :n_heads] for a in d_sink], axis=0),
        "conv_w": jnp.stack([a[0:3] for a in d_conv], axis=0),
        "rel_bias": _bias_grad(dbias_total.reshape(n_heads, WINDOW, 2 * WINDOW), n_heads)[:, :n_heads],
        "ln_g": jnp.concatenate([a[0:1] for a in d_ln], axis=0),
        "ln_b": jnp.concatenate([a[1:2] for a in d_ln], axis=0),
    }
    return loss_acc, dx, small


BIG = ("w_in", "w_proj_hgrn", "w_proj_attn", "w_proj_conv", "w_out")
SMALL = ("lb_param", "hgrn_norm_g", "attn_sinks", "conv_w", "rel_bias", "ln_g", "ln_b")
ORDER = ("w_in", "w_proj_hgrn", "w_proj_attn", "w_proj_conv", "w_out", "lb_param", "hgrn_norm_g", "attn_sinks",
         "conv_w", "rel_bias", "ln_g", "ln_b")


def _pack(parts):
    flat = jnp.concatenate([p.reshape(-1) for p in parts])
    n = flat.shape[0]
    unit = SUBLANES * LANES
    total = -(-n // unit) * unit
    return jnp.pad(flat, (0, total - n)).reshape(total // LANES, LANES)


def _unpack(packed, shapes):
    flat = packed.reshape(-1)
    out, o = [], 0
    for shp in shapes:
        n = int(np.prod(shp))
        out.append(flat[o:o + n].reshape(shp))
        o += n
    return out


def kernel(x, w_in, w_proj_hgrn, w_proj_attn, w_proj_conv, w_out, lb_param, hgrn_norm_g, attn_sinks, conv_w, rel_bias, ln_g, ln_b, loss_target, m_w_in, m_w_proj_hgrn, m_w_proj_attn, m_w_proj_conv, m_w_out, m_lb_param, m_hgrn_norm_g, m_attn_sinks, m_conv_w, m_rel_bias, m_ln_g, m_ln_b, v_w_in, v_w_proj_hgrn, v_w_proj_attn, v_w_proj_conv, v_w_out, v_lb_param, v_hgrn_norm_g, v_attn_sinks, v_conv_w, v_rel_bias, v_ln_g, v_ln_b):
    params = dict(w_in=w_in, w_proj_hgrn=w_proj_hgrn, w_proj_attn=w_proj_attn, w_proj_conv=w_proj_conv, w_out=w_out,
                  lb_param=lb_param, hgrn_norm_g=hgrn_norm_g, attn_sinks=attn_sinks, conv_w=conv_w, rel_bias=rel_bias,
                  ln_g=ln_g, ln_b=ln_b)
    mom_m = dict(w_in=m_w_in, w_proj_hgrn=m_w_proj_hgrn, w_proj_attn=m_w_proj_attn, w_proj_conv=m_w_proj_conv,
                 w_out=m_w_out, lb_param=m_lb_param, hgrn_norm_g=m_hgrn_norm_g, attn_sinks=m_attn_sinks,
                 conv_w=m_conv_w, rel_bias=m_rel_bias, ln_g=m_ln_g, ln_b=m_ln_b)
    mom_v = dict(w_in=v_w_in, w_proj_hgrn=v_w_proj_hgrn, w_proj_attn=v_w_proj_attn, w_proj_conv=v_w_proj_conv,
                 w_out=v_w_out, lb_param=v_lb_param, hgrn_norm_g=v_hgrn_norm_g, attn_sinks=v_attn_sinks,
                 conv_w=v_conv_w, rel_bias=v_rel_bias, ln_g=v_ln_g, ln_b=v_ln_b)
    d = x.shape[-1]
    me = 4 * lax.axis_index("x") + 2 * lax.axis_index("y") + lax.axis_index("c")
    for group in (params, mom_m, mom_v):
        group["w_in"] = jnp.swapaxes(group["w_in"], 1, 2)

    def shards_of(l):
        return [params[n][l].astype(BF16) for n in BIG] + [conv_w[l]]

    gathers = {}

    def weights_hook(l, x_in):
        if l == 0:
            got = _gather_two_level(shards_of(0), "gather_weights_0")
        else:
            got = _exchange_wait(gathers.pop(l), False, f"gather_wait_{l}", x_in)
        token = None
        if l + 1 < DEPTH:
            gathers[l + 1] = _exchange_start(shards_of(l + 1), False, f"gather_start_{l + 1}", dep=got[0])
            token = gathers[l + 1][-1]
        wl = {
            "w_in_t": got[0].reshape(-1, d),
            "w_proj_hgrn": _unshard_cols(got[1], "unshard_w_proj_hgrn"),
            "w_proj_attn": _unshard_cols(got[2], "unshard_w_proj_attn"),
            "w_proj_conv": _unshard_cols(got[3], "unshard_w_proj_conv"),
            "w_out": got[4].reshape(d, d),
            "conv_w": _unshard_cols(got[5], "unshard_conv_w"),
        }
        return wl, token

    grads = {n: [None] * DEPTH for n in BIG}
    scatters = {}

    def finish_scatter(l, after):
        got = _exchange_wait(scatters.pop(l), True, f"scatter_wait_{l}", after, chips=(l == 0))
        for n, slots in zip(BIG, got):
            grads[n][l] = _slot_sum(slots, f"sum_{n}_chips" if l == 0 else f"sum_{n}")
        return got[0]

    def grads_hook(l, g):
        send = [g["w_in_t"].reshape(N_DEV, -1, d), _shard_cols(g["w_proj_hgrn"], "shard_g_proj_a"),
                _shard_cols(g["w_proj_attn"], "shard_g_proj_b"), _shard_cols(g["w_proj_conv"], "shard_g_proj_c"),
                g["w_out"].reshape(N_DEV, d // N_DEV, d)]
        dep = finish_scatter(l + 1, send[0]) if l + 1 < DEPTH else None
        if l == 0:
            core = lax.axis_index("c").astype(jnp.int32).reshape(1)
            staged = _sibling_swap(send, "pair_swap_grads")
            send = [_pair_sum(s, st, core, f"pair_sum_{n}") for n, s, st in zip(BIG, send, staged)]
        scatters[l] = _exchange_start(send, True, f"scatter_start_{l}", dep=dep, chips=(l == 0))
        return scatters[l][-1]

    loss_acc, dx, small = _forward_backward(
        x[0], loss_target[0], weights_hook, grads_hook, lb_param, hgrn_norm_g, attn_sinks, rel_bias, ln_g, ln_b)
    loss = lax.psum(0.5 * jnp.sum(loss_acc[0]) / d, ("x", "y", "c"))
    finish_scatter(0, dx)
    for n in BIG:
        grads[n] = jnp.stack(grads[n], axis=0)

    small_shapes = [small[n].shape for n in SMALL]
    packed = _pack([small[n] for n in SMALL])
    got = _exchange([packed], False, "gather_small_grads")[0]
    summed = _unpack(_slot_sum(got, "sum_small_grads"), small_shapes)
    for n, g in zip(SMALL, summed):
        grads[n] = g
    cs = conv_w.shape[-1]
    grads["conv_w"] = lax.dynamic_slice_in_dim(grads["conv_w"], me * cs, cs, axis=2)

    delta, new_m, new_v = {}, {}, {}
    for n in BIG:
        shp = params[n].shape
        flat = lambda a: a.reshape(-1, shp[-1])
        dl, nm, nv = _adamw(flat(params[n]), flat(grads[n]), flat(mom_m[n]), flat(mom_v[n]), f"adamw_{n}")
        delta[n], new_m[n], new_v[n] = dl.reshape(shp), nm.reshape(shp), nv.reshape(shp)
    shapes = [params[n].shape for n in SMALL]
    res = _adamw(_pack([params[n] for n in SMALL]), _pack([grads[n] for n in SMALL]),
                 _pack([mom_m[n] for n in SMALL]), _pack([mom_v[n] for n in SMALL]), "adamw_small")
    for dst, packed_res in zip((delta, new_m, new_v), res):
        for n, a in zip(SMALL, _unpack(packed_res, shapes)):
            dst[n] = a

    for group in (grads, delta, new_m, new_v):
        group["w_in"] = jnp.swapaxes(group["w_in"], 1, 2)
    return (loss, dx[None], *[grads[n] for n in ORDER], *[delta[n] for n in ORDER],
            *[new_m[n] for n in ORDER], *[new_v[n] for n in ORDER])
```

```python
import functools
import math

import numpy as np
import jax
import jax.numpy as jnp
from jax import lax
from jax.experimental import pallas as pl
from jax.experimental.pallas import tpu as pltpu

F32 = jnp.float32
BF16 = jnp.bfloat16

N_DEV = 8
DEPTH = 4
HGRN_HEAD_DIM = 128
HGRN_CHUNK = 64
ATTN_HEAD_DIM = 64
ATTN_KV_HEADS = 4
KV_WIDTH = ATTN_KV_HEADS * ATTN_HEAD_DIM
WINDOW = 128
WINDOW_SHIFT = 7
N_BUCKETS = 32
MAX_DISTANCE = 128
ALPHA = (2.0 * DEPTH) ** 0.25
LN_EPS = 1e-5
RMS_EPS = 1e-6
ADAM_LR = 0.001
ADAM_B1 = 0.9
ADAM_B2 = 0.999
ADAM_EPS = 1e-08
ADAM_WD = 0.01
ADAM_STEP = 10

LANES = 128
SUBLANES = 8
VMEM_LIMIT = 56 << 20
NEG_INF = float("-inf")


def _offsets(d_model):
    w = d_model // 2
    sizes = (w, w, w, w, w, KV_WIDTH, KV_WIDTH, w, w, w, w, w, d_model, d_model, d_model)
    names = ("a_q", "a_f", "a_i", "a_g", "b_q", "b_k", "b_v", "b_g", "c_b", "c_c", "c_x", "c_g", "m_a", "m_b", "m_c")
    off, o = {}, 0
    for n, s in zip(names, sizes):
        off[n] = o
        o += s
    return off, o


def _tile(n, pref):
    t = min(pref, n)
    while n % t:
        t //= 2
    return t


def _cp(sem=None, vmem=VMEM_LIMIT):
    return pltpu.CompilerParams(dimension_semantics=sem, vmem_limit_bytes=vmem)


def _sigmoid(x):
    return 1.0 / (1.0 + jnp.exp(-x))


def _dot_nn(a, b):
    return jnp.dot(a, b, preferred_element_type=F32)


def _dot_nt(a, b):
    return lax.dot_general(a, b, (((1,), (1,)), ((), ())), preferred_element_type=F32)


def _dot_tn(a, b):
    return lax.dot_general(a, b, (((0,), (0,)), ((), ())), preferred_element_type=F32)


def _dep_specs(dep):
    return ([], []) if dep is None else ([pl.BlockSpec(memory_space=pl.ANY)], [dep])


def _mm_nn(a, b, name, out_dtype=F32, tm=1024, tn=1536, dep=None):
    m, k = a.shape
    _, n = b.shape
    tm, tn = _tile(m, tm), _tile(n, tn)
    dep_specs, dep_args = _dep_specs(dep)

    def body(a_ref, b_ref, *rest):
        o_ref = rest[-1]
        o_ref[...] = _dot_nn(a_ref[...], b_ref[...]).astype(o_ref.dtype)

    return pl.pallas_call(
        body, grid=(n // tn, m // tm),
        in_specs=[pl.BlockSpec((tm, k), lambda j, i: (i, 0)), pl.BlockSpec((k, tn), lambda j, i: (0, j))] + dep_specs,
        out_specs=pl.BlockSpec((tm, tn), lambda j, i: (i, j)),
        out_shape=jax.ShapeDtypeStruct((m, n), out_dtype), name=name,
        compiler_params=_cp(("parallel", "parallel")))(a, b, *dep_args)


def _mm_nt(a, b, name, tm=1024, tk=1536, add=None, add_scale=1.0, dep=None):
    m, k = a.shape
    n, _ = b.shape
    tm, tk = _tile(m, tm), _tile(k, tk)
    has_add = add is not None
    dep_specs, dep_args = _dep_specs(dep)

    def body(*refs):
        if has_add:
            a_ref, b_ref, add_ref = refs[:3]
        else:
            a_ref, b_ref = refs[:2]
        o_ref = refs[-1]
        if k == tk:
            prod = _dot_nt(a_ref[...], b_ref[...])
            o_ref[...] = prod + add_ref[...] * add_scale if has_add else prod
            return

        @pl.when(pl.program_id(1) == 0)
        def _():
            if has_add:
                o_ref[...] = add_ref[...] * add_scale
            else:
                o_ref[...] = jnp.zeros_like(o_ref)

        o_ref[...] += _dot_nt(a_ref[...], b_ref[...])

    in_specs = [pl.BlockSpec((tm, tk), lambda i, kk: (i, kk)), pl.BlockSpec((n, tk), lambda i, kk: (0, kk))]
    args = [a, b]
    if has_add:
        in_specs.append(pl.BlockSpec((tm, n), lambda i, kk: (i, 0)))
        args.append(add)
    in_specs += dep_specs
    args += dep_args
    return pl.pallas_call(
        body, grid=(m // tm, k // tk), in_specs=in_specs,
        out_specs=pl.BlockSpec((tm, n), lambda i, kk: (i, 0)),
        out_shape=jax.ShapeDtypeStruct((m, n), F32), name=name,
        compiler_params=_cp(("parallel", "arbitrary")))(*args)


def _mm_nt_cols(a, b, name, tm=1024, tn=1536, dep=None):
    m, k = a.shape
    n, _ = b.shape
    tm, tn = _tile(m, tm), _tile(n, tn)
    dep_specs, dep_args = _dep_specs(dep)

    def body(a_ref, b_ref, *rest):
        rest[-1][...] = _dot_nt(a_ref[...], b_ref[...])

    return pl.pallas_call(
        body, grid=(n // tn, m // tm),
        in_specs=[pl.BlockSpec((tm, k), lambda j, i: (i, 0)), pl.BlockSpec((tn, k), lambda j, i: (j, 0))] + dep_specs,
        out_specs=pl.BlockSpec((tm, tn), lambda j, i: (i, j)),
        out_shape=jax.ShapeDtypeStruct((m, n), F32), name=name,
        compiler_params=_cp(("parallel", "parallel")))(a, b, *dep_args)


def _mm_nn_acc(a, b, name, tm=1024, tk=1536, add=None, add_scale=1.0, dep=None):
    m, k = a.shape
    _, n = b.shape
    tm, tk = _tile(m, tm), _tile(k, tk)
    dep_specs, dep_args = _dep_specs(dep)

    def body(a_ref, b_ref, add_ref, *rest):
        o_ref = rest[-1]

        @pl.when(pl.program_id(1) == 0)
        def _():
            o_ref[...] = add_ref[...] * add_scale

        o_ref[...] += _dot_nn(a_ref[...], b_ref[...])

    return pl.pallas_call(
        body, grid=(m // tm, k // tk),
        in_specs=[pl.BlockSpec((tm, tk), lambda i, kk: (i, kk)), pl.BlockSpec((tk, n), lambda i, kk: (kk, 0)),
                  pl.BlockSpec((tm, n), lambda i, kk: (i, 0))] + dep_specs,
        out_specs=pl.BlockSpec((tm, n), lambda i, kk: (i, 0)),
        out_shape=jax.ShapeDtypeStruct((m, n), F32), name=name,
        compiler_params=_cp(("parallel", "arbitrary")))(a, b, add, *dep_args)


def _mm_tn_rows(a, b, name, tt=1024, tr=1536):
    t, k = a.shape
    _, n = b.shape
    tt, tr = _tile(t, tt), _tile(k, tr)

    def body(a_ref, b_ref, o_ref):
        @pl.when(pl.program_id(1) == 0)
        def _():
            o_ref[...] = jnp.zeros_like(o_ref)

        o_ref[...] += _dot_tn(a_ref[...], b_ref[...])

    return pl.pallas_call(
        body, grid=(k // tr, t // tt),
        in_specs=[pl.BlockSpec((tt, tr), lambda j, s: (s, j)), pl.BlockSpec((tt, n), lambda j, s: (s, 0))],
        out_specs=pl.BlockSpec((tr, n), lambda j, s: (j, 0)),
        out_shape=jax.ShapeDtypeStruct((k, n), F32), name=name,
        compiler_params=_cp(("parallel", "arbitrary")))(a, b)


def _mm_tn(a, b, name, tt=1024, tn=1536):
    t, k = a.shape
    _, n = b.shape
    tt, tn = _tile(t, tt), _tile(n, tn)

    def body(a_ref, b_ref, o_ref):
        @pl.when(pl.program_id(1) == 0)
        def _():
            o_ref[...] = jnp.zeros_like(o_ref)

        o_ref[...] += _dot_tn(a_ref[...], b_ref[...])

    return pl.pallas_call(
        body, grid=(n // tn, t // tt),
        in_specs=[pl.BlockSpec((tt, k), lambda j, s: (s, 0)), pl.BlockSpec((tt, tn), lambda j, s: (s, j))],
        out_specs=pl.BlockSpec((k, tn), lambda j, s: (0, j)),
        out_shape=jax.ShapeDtypeStruct((k, n), F32), name=name,
        compiler_params=_cp(("parallel", "arbitrary")))(a, b)


def _ew(body, name, t, ncol, wb, ins, outs, accs=(), tt=512):
    tt = _tile(t, tt)
    nt = t // tt
    per = tt // SUBLANES
    in_specs, args = [], []
    for arr, kind, coff in ins:
        if kind == "tile":
            spec = pl.BlockSpec((tt, wb), lambda j, i, c=coff: (i, c + j))
        elif kind == "prev":
            spec = pl.BlockSpec((SUBLANES, wb), lambda j, i, c=coff: (jnp.maximum(i * per - 1, 0), c + j))
        elif kind == "next":
            spec = pl.BlockSpec((SUBLANES, wb), lambda j, i, c=coff: (jnp.minimum((i + 1) * per, nt * per - 1), c + j))
        else:
            spec = pl.BlockSpec((arr.shape[0], wb), lambda j, i, c=coff: (0, c + j))
        in_specs.append(spec)
        args.append(arr)
    out_specs = [pl.BlockSpec((tt, wb), lambda j, i: (i, j)) for _ in outs]
    out_shape = [jax.ShapeDtypeStruct((t, ncol * wb), d) for d in outs]
    for r in accs:
        out_specs.append(pl.BlockSpec((r, wb), lambda j, i: (0, j)))
        out_shape.append(jax.ShapeDtypeStruct((r, ncol * wb), F32))

    def kern(*refs):
        body(pl.program_id(1), nt, *refs)

    res = pl.pallas_call(
        kern, grid=(ncol, nt), in_specs=in_specs, out_specs=out_specs, out_shape=out_shape, name=name,
        compiler_params=_cp(("parallel", "arbitrary")))(*args)
    return res


def _silu_parts(x):
    s = _sigmoid(x)
    return x * s, s + x * s * (1.0 - s)


def _out_proj_ln(a, w, x, g, b, name):
    t, d = x.shape
    k = a.shape[1]
    tt = _tile(t, 256)

    def body(a_ref, w_ref, x_ref, g_ref, b_ref, o_ref, ob_ref, xh_ref, r_ref):
        z = ALPHA * x_ref[...] + _dot_nn(a_ref[...], w_ref[...])
        mu = jnp.mean(z, axis=1, keepdims=True)
        zc = z - mu
        var = jnp.mean(zc * zc, axis=1, keepdims=True)
        rstd = lax.rsqrt(var + LN_EPS)
        xh = zc * rstd
        o = xh * g_ref[...] + b_ref[...]
        o_ref[...] = o
        ob_ref[...] = o.astype(BF16)
        xh_ref[...] = xh
        r_ref[...] = rstd

    row = pl.BlockSpec((tt, d), lambda i: (i, 0))
    vec = pl.BlockSpec((1, d), lambda i: (0, 0))
    return pl.pallas_call(
        body, grid=(t // tt,),
        in_specs=[pl.BlockSpec((tt, k), lambda i: (i, 0)), pl.BlockSpec((k, d), lambda i: (0, 0)), row, vec, vec],
        out_specs=[row, row, row, pl.BlockSpec((tt, 1), lambda i: (i, 0))],
        out_shape=[jax.ShapeDtypeStruct((t, d), F32), jax.ShapeDtypeStruct((t, d), BF16),
                   jax.ShapeDtypeStruct((t, d), F32), jax.ShapeDtypeStruct((t, 1), F32)],
        name=name, compiler_params=_cp(("parallel",)))(a, w, x, g, b)


def _ln_bwd_out_proj_merge(dout, xhat, rstd, g, w, u, ya, yb, yc, off, name):
    t, d = dout.shape
    gb = GATE_BLOCK
    nb = d // gb
    tt = _tile(t, 128)
    cols = [off[nme] // gb for nme in ("m_a", "m_b", "m_c")]

    def body(do_ref, xh_ref, r_ref, g_ref, w_ref, ya_ref, yb_ref, yc_ref, *rest):
        m_refs = rest[:3 * nb]
        dz_ref, dzb_ref, acc_ref = rest[3 * nb:3 * nb + 3]
        dy_refs, dg_refs = rest[3 * nb + 3:3 * nb + 6], rest[3 * nb + 6:]

        @pl.when(pl.program_id(0) == 0)
        def _():
            acc_ref[...] = jnp.zeros_like(acc_ref)

        do = do_ref[...]
        xh = xh_ref[...]
        dxh = do * g_ref[...]
        m1 = jnp.mean(dxh, axis=1, keepdims=True)
        m2 = jnp.mean(dxh * xh, axis=1, keepdims=True)
        dz = r_ref[...] * (dxh - m1 - xh * m2)
        dzb = dz.astype(BF16)
        dz_ref[...] = dz
        dzb_ref[...] = dzb
        acc_ref[0:1, :] += jnp.sum(do * xh, axis=0, keepdims=True)
        acc_ref[1:2, :] += jnp.sum(do, axis=0, keepdims=True)
        dm = _dot_nt(dzb, w_ref[...])
        for gate, y_ref in enumerate((ya_ref, yb_ref, yc_ref)):
            for j in range(nb):
                sl = slice(j * gb, (j + 1) * gb)
                s = _sigmoid(m_refs[gate * nb + j][...])
                dm_j = dm[:, sl]
                dy_refs[gate][:, sl] = (dm_j * s).astype(BF16)
                dg_refs[gate][:, sl] = (dm_j * y_ref[:, sl] * s * (1.0 - s)).astype(BF16)

    row = pl.BlockSpec((tt, d), lambda i: (i, 0))
    gates = [pl.BlockSpec((tt, gb), lambda i, c=c0 + j: (i, c)) for c0 in cols for j in range(nb)]
    return pl.pallas_call(
        body, grid=(t // tt,),
        in_specs=[row, row, pl.BlockSpec((tt, 1), lambda i: (i, 0)), pl.BlockSpec((1, d), lambda i: (0, 0)),
                  pl.BlockSpec((d, d), lambda i: (0, 0)), row, row, row] + gates,
        out_specs=[row, row, pl.BlockSpec((SUBLANES, d), lambda i: (0, 0))] + [row] * 6,
        out_shape=[jax.ShapeDtypeStruct((t, d), F32), jax.ShapeDtypeStruct((t, d), BF16),
                   jax.ShapeDtypeStruct((SUBLANES, d), F32)] + [jax.ShapeDtypeStruct((t, d), BF16)] * 6,
        name=name, compiler_params=_cp(("arbitrary",)))(dout, xhat, rstd, g, w, ya, yb, yc, *([u] * (3 * nb)))


def _loss_head(y, target):
    t, d = y.shape
    tt = _tile(t, 256)

    def body(y_ref, t_ref, acc_ref, dy_ref):
        @pl.when(pl.program_id(0) == 0)
        def _():
            acc_ref[...] = jnp.zeros_like(acc_ref)

        err = y_ref[...] - t_ref[...]
        dy_ref[...] = err * (1.0 / d)
        acc_ref[0:1, :] += jnp.sum(err * err, axis=0, keepdims=True)

    row = pl.BlockSpec((tt, d), lambda i: (i, 0))
    acc, dy = pl.pallas_call(
        body, grid=(t // tt,), in_specs=[row, row],
        out_specs=[pl.BlockSpec((SUBLANES, d), lambda i: (0, 0)), row],
        out_shape=[jax.ShapeDtypeStruct((SUBLANES, d), F32), jax.ShapeDtypeStruct((t, d), F32)],
        name="loss_head", compiler_params=_cp(("arbitrary",)))(y, target)
    return acc, dy


def _tri(lower):
    r = lax.broadcasted_iota(jnp.int32, (HGRN_CHUNK, HGRN_CHUNK), 0)
    c = lax.broadcasted_iota(jnp.int32, (HGRN_CHUNK, HGRN_CHUNK), 1)
    return jnp.where((r >= c) if lower else (r <= c), 1.0, 0.0).astype(BF16)


def _exact_tri_matmul(tri, x):
    hi = x.astype(BF16)
    r1 = x - hi.astype(F32)
    mid = r1.astype(BF16)
    lo = (r1 - mid.astype(F32)).astype(BF16)
    return _dot_nn(tri, hi) + _dot_nn(tri, mid) + _dot_nn(tri, lo)


def _hgrn_gates(q_raw, fl, lb):
    sq = _sigmoid(q_raw)
    qf = q_raw * sq * (HGRN_HEAD_DIM ** -0.5)
    sg = _sigmoid(fl)
    f = lb + (1.0 - lb) * sg
    return qf, sq, sg, f


HGRN_SUB = 16
HGRN_NSUB = HGRN_CHUNK // HGRN_SUB
HGRN_HEADS_PER_STEP = 8
HGRN_HEADS_PER_STEP_BWD = 8


def _diag_rows(r):
    return (r // SUBLANES) * SUBLANES


def _heads(x):
    hd = HGRN_HEAD_DIM
    return [x[:, i * hd:(i + 1) * hd] for i in range(x.shape[1] // hd)]


def _per_head(fn, *xs):
    split = [x if isinstance(x, (list, tuple)) else _heads(x) for x in xs]
    return jnp.concatenate([fn(*hs) for hs in zip(*split)], axis=1)


def _head_lane_sum(x):
    return _per_head(lambda h: jnp.broadcast_to(jnp.sum(h, axis=1, keepdims=True), h.shape), x)


def _hgrn_intra_fwd(qf, k, v, b):
    ch, sub, wd = HGRN_CHUNK, HGRN_SUB, qf.shape[1]
    tl = lax.broadcasted_iota(jnp.int32, (sub, wd), 0)
    blocks = []
    for m in range(HGRN_NSUB):
        rs = slice(m * sub, (m + 1) * sub)
        bm, qm, km, vm = b[rs], qf[rs], k[rs], v[rs]
        parts = {0: jnp.zeros((sub, wd), F32), SUBLANES: jnp.zeros((sub - SUBLANES, wd), F32)}
        for r in range(sub):
            lo = _diag_rows(r)
            e = jnp.exp(jnp.where(tl[lo:] >= r, bm[lo:] - bm[r:r + 1], NEG_INF))
            parts[lo] = parts[lo] + _head_lane_sum(qm[lo:] * e * km[r:r + 1]) * vm[r:r + 1]
        blocks.append(parts[0] + jnp.concatenate([jnp.zeros((SUBLANES, wd), F32), parts[SUBLANES]], axis=0))
    acc = jnp.concatenate(blocks, axis=0)
    for j in range(HGRN_NSUB - 1):
        lo = sub * (j + 1)
        c = b[lo - 1:lo, :]
        qj = (qf[lo:] * jnp.exp(b[lo:] - c)).astype(BF16)
        kj = (k[lo - sub:lo] * jnp.exp(c - b[lo - sub:lo])).astype(BF16)
        vj = v[lo - sub:lo].astype(BF16)
        contrib = _per_head(lambda q_, k_, v_: _dot_nn(_dot_nt(q_, k_).astype(BF16), v_), qj, kj, vj)
        acc = acc + jnp.concatenate([jnp.zeros((lo, wd), F32), contrib], axis=0)
    return acc


def _hgrn_intra_bwd(qf, k, v, b, do_v):
    ch, sub, wd = HGRN_CHUNK, HGRN_SUB, qf.shape[1]
    tl = lax.broadcasted_iota(jnp.int32, (sub, wd), 0)
    dq_blocks, dk_blocks, dv_blocks = [], [], []
    for m in range(HGRN_NSUB):
        rs = slice(m * sub, (m + 1) * sub)
        bm, qm, km, vm, dom = b[rs], qf[rs], k[rs], v[rs], do_v[rs]
        parts = {0: jnp.zeros((sub, wd), F32), SUBLANES: jnp.zeros((sub - SUBLANES, wd), F32)}
        dk_parts = {sub: jnp.zeros((sub, wd), F32), SUBLANES: jnp.zeros((SUBLANES, wd), F32)}
        dv_parts = {sub: jnp.zeros((sub, wd), F32), SUBLANES: jnp.zeros((SUBLANES, wd), F32)}
        for r in range(sub):
            lo = _diag_rows(r)
            b_r, k_r, v_r, q_r, do_r = bm[r:r + 1], km[r:r + 1], vm[r:r + 1], qm[r:r + 1], dom[r:r + 1]
            e = jnp.exp(jnp.where(tl[lo:] >= r, bm[lo:] - b_r, NEG_INF))
            parts[lo] = parts[lo] + _head_lane_sum(dom[lo:] * v_r) * (k_r * e)
            hi = lo + SUBLANES
            e2 = jnp.exp(jnp.where(tl[:hi] <= r, b_r - bm[:hi], NEG_INF))
            qe2 = q_r * e2
            dk_parts[hi] = dk_parts[hi] + _head_lane_sum(vm[:hi] * do_r) * qe2
            dv_parts[hi] = dv_parts[hi] + _head_lane_sum(km[:hi] * qe2) * do_r
        pad = jnp.zeros((SUBLANES, wd), F32)
        dq_blocks.append(parts[0] + jnp.concatenate([pad, parts[SUBLANES]], axis=0))
        dk_blocks.append(dk_parts[sub] + jnp.concatenate([dk_parts[SUBLANES], pad], axis=0))
        dv_blocks.append(dv_parts[sub] + jnp.concatenate([dv_parts[SUBLANES], pad], axis=0))
    dq = jnp.concatenate(dq_blocks, axis=0)
    dk = jnp.concatenate(dk_blocks, axis=0)
    dv = jnp.concatenate(dv_blocks, axis=0)
    do_b, v_b = do_v.astype(BF16), v.astype(BF16)
    dk_off, dv_off = [], []
    for j in range(HGRN_NSUB - 1):
        lo = sub * (j + 1)
        c = b[lo - 1:lo, :]
        eq = jnp.exp(b[lo:] - c)
        ek = jnp.exp(c - b[lo - sub:lo])
        qj = (qf[lo:] * eq).astype(BF16)
        kj = (k[lo - sub:lo] * ek).astype(BF16)
        doj, vj = do_b[lo:], v_b[lo - sub:lo]
        dq_j = _per_head(lambda do_, v_, k_: _dot_nn(_dot_nt(do_, v_).astype(BF16), k_), doj, vj, kj)
        dk_j = _per_head(lambda do_, v_, q_: _dot_nn(_dot_nt(v_, do_).astype(BF16), q_), doj, vj, qj)
        dv_j = _per_head(lambda do_, k_, q_: _dot_nn(_dot_nt(k_, q_).astype(BF16), do_), doj, kj, qj)
        dq = dq + jnp.concatenate([jnp.zeros((lo, wd), F32), dq_j * eq], axis=0)
        dk_off.append(dk_j * ek)
        dv_off.append(dv_j)
    zero = jnp.zeros((sub, wd), F32)
    dk = dk + jnp.concatenate(dk_off + [zero], axis=0)
    dv = dv + jnp.concatenate(dv_off + [zero], axis=0)
    return dq, dk, dv


def _head_rms(o):
    return lax.rsqrt(_head_lane_sum(o * o) * (1.0 / HGRN_HEAD_DIM) + RMS_EPS)


def _hgrn_fwd(u, lb, gain, off, name):
    t = u.shape[0]
    w = lb.shape[1]
    hd, ch, hp = HGRN_HEAD_DIM, HGRN_CHUNK, HGRN_HEADS_PER_STEP
    nh, nc = w // hd, t // ch
    wb = hp * hd
    cq, cf, cv, cg = off["a_q"] // wb, off["a_f"] // wb, off["a_i"] // wb, off["a_g"] // wb

    def body(q_ref, f_ref, v_ref, g_ref, lb_ref, gain_ref, o_ref, st_ref, p_ref, state):
        @pl.when(pl.program_id(1) == 0)
        def _():
            state[...] = jnp.zeros_like(state)

        sts = [state[i] for i in range(hp)]
        qf, _, _, f = _hgrn_gates(q_ref[...], f_ref[...], lb_ref[...])
        k = 1.0 - f
        v = v_ref[...]
        b = _exact_tri_matmul(_tri(True), jnp.log(f))
        inter = _per_head(lambda qa_, st_: _dot_nt(qa_, st_.astype(BF16)), (qf * jnp.exp(b)).astype(BF16), sts)
        o = inter + _hgrn_intra_fwd(qf, k, v, b)
        o_ref[...] = o
        silu, _ = _silu_parts(g_ref[...])
        p_ref[...] = (o * _head_rms(o) * gain_ref[...] * silu).astype(BF16)
        b_end = b[ch - 1:ch, :]
        a_end = _heads(jnp.exp(b_end))
        kd = _heads((k * jnp.exp(b_end - b)).astype(BF16))
        v_b = _heads(v.astype(BF16))
        for i in range(hp):
            st_ref[i, 0] = sts[i]
            state[i] = sts[i] * a_end[i] + _dot_tn(v_b[i], kd[i])

    return pl.pallas_call(
        body, grid=(nh // hp, nc),
        in_specs=[pl.BlockSpec((ch, wb), lambda h, n: (n, cq + h)),
                  pl.BlockSpec((ch, wb), lambda h, n: (n, cf + h)),
                  pl.BlockSpec((ch, wb), lambda h, n: (n, cv + h)),
                  pl.BlockSpec((ch, wb), lambda h, n: (n, cg + h)),
                  pl.BlockSpec((1, wb), lambda h, n: (0, h)),
                  pl.BlockSpec((1, wb), lambda h, n: (0, h))],
        out_specs=[pl.BlockSpec((ch, wb), lambda h, n: (n, h)),
                   pl.BlockSpec((hp, 1, hd, hd), lambda h, n: (h, n, 0, 0)),
                   pl.BlockSpec((ch, wb), lambda h, n: (n, h))],
        out_shape=[jax.ShapeDtypeStruct((t, w), F32), jax.ShapeDtypeStruct((nh, nc, hd, hd), F32),
                   jax.ShapeDtypeStruct((t, w), BF16)],
        scratch_shapes=[pltpu.VMEM((hp, hd, hd), F32)],
        name=name, compiler_params=_cp(("parallel", "arbitrary")))(u, u, u, u, lb, gain)


def _hgrn_bwd(u, lb, gain, states, o, dp, off, name):
    t = u.shape[0]
    w = lb.shape[1]
    hd, ch, hp = HGRN_HEAD_DIM, HGRN_CHUNK, HGRN_HEADS_PER_STEP_BWD
    nh, nc = w // hd, t // ch
    wb = hp * hd
    cq, cf, cv, cg = off["a_q"] // wb, off["a_f"] // wb, off["a_i"] // wb, off["a_g"] // wb

    def body(q_ref, f_ref, v_ref, g_ref, o_ref, dp_ref, st_ref, lb_ref, gain_ref,
             dq_ref, df_ref, dv_ref, dg_ref, dlb_ref, dstate):
        @pl.when(pl.program_id(1) == 0)
        def _():
            dstate[...] = jnp.zeros_like(dstate)
            dlb_ref[...] = jnp.zeros_like(dlb_ref)

        silu, dsilu = _silu_parts(g_ref[...])
        o_v, dp_v, gain_row = o_ref[...], dp_ref[...], gain_ref[...]
        rms = _head_rms(o_v)
        nrm = o_v * rms
        dg_ref[...] = (dp_v * nrm * gain_row * dsilu).astype(BF16)
        dlb_ref[1:2, :] += jnp.sum(dp_v * nrm * silu, axis=0, keepdims=True)
        dn = dp_v * gain_row * silu
        do_v = rms * (dn - nrm * (_head_lane_sum(dn * nrm) * (1.0 / HGRN_HEAD_DIM)))

        rows = lax.broadcasted_iota(jnp.int32, (ch, wb), 0)
        lb_row = lb_ref[...]
        q_raw = q_ref[...]
        qf, sq, sg, f = _hgrn_gates(q_raw, f_ref[...], lb_row)
        k = 1.0 - f
        b = _exact_tri_matmul(_tri(True), jnp.log(f))
        a = jnp.exp(b)
        b_end = b[ch - 1:ch, :]
        a_end = jnp.exp(b_end)
        to_end = jnp.exp(b_end - b)
        v = v_ref[...]
        st0 = [st_ref[i, 0] for i in range(hp)]
        ds = [dstate[i] for i in range(hp)]
        st0_b = [s_.astype(BF16) for s_ in st0]
        ds_b = [s_.astype(BF16) for s_ in ds]
        do_b, v_b, kd_b, qa_b = do_v.astype(BF16), v.astype(BF16), (k * to_end).astype(BF16), (qf * a).astype(BF16)

        dq_inter = a * _per_head(_dot_nn, do_b, st0_b)
        dk_end = to_end * _per_head(_dot_nn, v_b, ds_b)
        dv_end = _per_head(_dot_nt, kd_b, ds_b)
        a_end_h = _heads(a_end)
        st_end = [st0[i] * a_end_h[i] + _dot_tn(_heads(v_b)[i], _heads(kd_b)[i]) for i in range(hp)]
        db_end = jnp.concatenate([jnp.sum(ds[i] * st_end[i], axis=0, keepdims=True) for i in range(hp)], axis=1)
        ds_new = [ds[i] * a_end_h[i] + _dot_tn(_heads(do_b)[i], _heads(qa_b)[i]) for i in range(hp)]

        dq_intra, dk_intra, dv_intra = _hgrn_intra_bwd(qf, k, v, b, do_v)
        dqf = dq_inter + dq_intra
        dk = dk_end + dk_intra
        dv = dv_end + dv_intra
        db = qf * dqf - k * dk
        db = db + jnp.where(rows == ch - 1, db_end, 0.0)
        dg = _exact_tri_matmul(_tri(False), db)
        df = dg / f - dk
        for i in range(hp):
            dstate[i] = ds_new[i]
        dq_ref[...] = (dqf * (HGRN_HEAD_DIM ** -0.5) * (sq + q_raw * sq * (1.0 - sq))).astype(BF16)
        df_ref[...] = (df * (1.0 - lb_row) * sg * (1.0 - sg)).astype(BF16)
        dv_ref[...] = dv.astype(BF16)
        dlb_ref[0:1, :] += jnp.sum(df * (1.0 - sg), axis=0, keepdims=True)

    rev = lambda n: nc - 1 - n
    tile = lambda c: pl.BlockSpec((ch, wb), lambda h, n, c=c: (rev(n), c + h))
    return pl.pallas_call(
        body, grid=(nh // hp, nc),
        in_specs=[tile(cq), tile(cf), tile(cv), tile(cg), tile(0), tile(0),
                  pl.BlockSpec((hp, 1, hd, hd), lambda h, n: (h, rev(n), 0, 0)),
                  pl.BlockSpec((1, wb), lambda h, n: (0, h)), pl.BlockSpec((1, wb), lambda h, n: (0, h))],
        out_specs=[tile(0), tile(0), tile(0), tile(0), pl.BlockSpec((SUBLANES, wb), lambda h, n: (0, h))],
        out_shape=[jax.ShapeDtypeStruct((t, w), BF16)] * 4 + [jax.ShapeDtypeStruct((SUBLANES, w), F32)],
        scratch_shapes=[pltpu.VMEM((hp, hd, hd), F32)],
        name=name, compiler_params=_cp(("parallel", "arbitrary")))(u, u, u, u, o, dp, states, lb, gain)


def _bucket_map():
    i = np.arange(WINDOW)[:, None]
    j = np.arange(2 * WINDOW)[None, :]
    dist = np.clip(WINDOW + i - j, 0, WINDOW - 1)
    max_exact = N_BUCKETS // 2
    logd = (np.log(np.maximum(dist, 1).astype(np.float32) / max_exact) / math.log(MAX_DISTANCE / max_exact))
    large = np.minimum(max_exact + (logd.astype(np.float32) * (N_BUCKETS - max_exact)).astype(np.int32), N_BUCKETS - 1)
    return np.where(dist < max_exact, dist, large).astype(np.int32)


def _bias_table(rel_bias, n_heads):
    bucket = jnp.asarray(_bucket_map())

    def body(rb_ref, bk_ref, o_ref):
        bk = bk_ref[...]
        i = lax.broadcasted_iota(jnp.int32, (WINDOW, 2 * WINDOW), 0)
        j = lax.broadcasted_iota(jnp.int32, (WINDOW, 2 * WINDOW), 1)
        band = ((j >= WINDOW) & (j - WINDOW <= i)) | ((j < WINDOW) & (j > i))
        for h in range(n_heads):
            def step(bi, acc):
                return jnp.where(bk == bi, rb_ref[bi, h], acc)
            table = lax.fori_loop(0, N_BUCKETS, step, jnp.zeros((WINDOW, 2 * WINDOW), F32))
            o_ref[h] = jnp.where(band, table, NEG_INF)

    return pl.pallas_call(
        body, in_specs=[pl.BlockSpec(memory_space=pltpu.SMEM), pl.BlockSpec(memory_space=pltpu.VMEM)],
        out_specs=pl.BlockSpec(memory_space=pltpu.VMEM),
        out_shape=jax.ShapeDtypeStruct((n_heads, WINDOW, 2 * WINDOW), F32), name="bias_table",
        compiler_params=_cp())(rel_bias, bucket)


def _bias_grad(dbias, n_heads):
    bucket = jnp.asarray(_bucket_map())

    def body(db_ref, bk_ref, o_ref):
        bk = bk_ref[...]
        lane = lax.broadcasted_iota(jnp.int32, (1, LANES), 1)

        def step(bi, carry):
            row = jnp.zeros((1, LANES), F32)
            for h in range(n_heads):
                val = jnp.sum(jnp.where(bk == bi, db_ref[h], 0.0))
                row = jnp.where(lane == h, val, row)
            o_ref[pl.ds(bi, 1), :] = row
            return carry

        lax.fori_loop(0, N_BUCKETS, step, 0)

    return pl.pallas_call(
        body, in_specs=[pl.BlockSpec(memory_space=pltpu.VMEM), pl.BlockSpec(memory_space=pltpu.VMEM)],
        out_specs=pl.BlockSpec(memory_space=pltpu.VMEM),
        out_shape=jax.ShapeDtypeStruct((N_BUCKETS, LANES), F32), name="bias_grad",
        compiler_params=_cp())(dbias, bucket)


def _no_prev_block(n, grp):
    j = lax.broadcasted_iota(jnp.int32, (grp * WINDOW, 2 * WINDOW), 1)
    return (j < WINDOW) & (n == 0)


def _attn_probs(no_prev, q_ref, kp_ref, kc_ref, bias_ref, sink_ref, hh, grp):
    ad, wn = ATTN_HEAD_DIM, WINDOW
    ksl = slice(hh * ad, (hh + 1) * ad)
    kw = jnp.concatenate([kp_ref[:, ksl], kc_ref[:, ksl]], axis=0).astype(BF16)
    qs = jnp.concatenate([q_ref[:, (hh * grp + g) * ad:(hh * grp + g + 1) * ad] for g in range(grp)], axis=0).astype(BF16)
    s = _dot_nt(qs, kw) * (ad ** -0.5) + bias_ref[hh]
    s = jnp.where(no_prev, NEG_INF, s)
    rr = lax.broadcasted_iota(jnp.int32, (grp * wn, 1), 0) >> WINDOW_SHIFT
    sink = jnp.zeros((grp * wn, 1), F32)
    for g in range(grp):
        sink = jnp.where(rr == g, sink_ref[hh * grp + g], sink)
    m = jnp.maximum(jnp.max(s, axis=1, keepdims=True), sink)
    p = jnp.exp(s - m)
    es = jnp.exp(sink - m)
    inv = 1.0 / (jnp.sum(p, axis=1, keepdims=True) + es)
    return qs, kw, p * inv, es * inv


GATE_BLOCK = 512


def _attn_fwd(u, bias_g, sinks, off, w, name):
    t = u.shape[0]
    wn, ad, kvw, gb = WINDOW, ATTN_HEAD_DIM, KV_WIDTH, GATE_BLOCK
    grp = (w // ad) // ATTN_KV_HEADS
    nb = t // wn
    n_gb = w // gb
    cq, ck, cv, cg = off["b_q"] // w, off["b_k"] // kvw, off["b_v"] // kvw, off["b_g"] // gb

    def body(q_ref, kp_ref, kc_ref, vp_ref, vc_ref, bias_ref, sink_ref, *rest):
        g_refs, (o_ref, p_ref) = rest[:n_gb], rest[n_gb:]
        no_prev = _no_prev_block(pl.program_id(0), grp)
        for hh in range(ATTN_KV_HEADS):
            _, _, p, _ = _attn_probs(no_prev, q_ref, kp_ref, kc_ref, bias_ref, sink_ref, hh, grp)
            ksl = slice(hh * ad, (hh + 1) * ad)
            vw = jnp.concatenate([vp_ref[:, ksl], vc_ref[:, ksl]], axis=0).astype(BF16)
            o = _dot_nn(p.astype(BF16), vw)
            for g in range(grp):
                o_ref[:, (hh * grp + g) * ad:(hh * grp + g + 1) * ad] = o[g * wn:(g + 1) * wn]
        for i in range(n_gb):
            sl = slice(i * gb, (i + 1) * gb)
            silu, _ = _silu_parts(g_refs[i][...])
            p_ref[:, sl] = (o_ref[:, sl] * silu).astype(BF16)

    prev = lambda n: jnp.maximum(n - 1, 0)
    row = pl.BlockSpec((wn, w), lambda n: (n, 0))
    return pl.pallas_call(
        body, grid=(nb,),
        in_specs=[pl.BlockSpec((wn, w), lambda n: (n, cq)),
                  pl.BlockSpec((wn, kvw), lambda n: (prev(n), ck)), pl.BlockSpec((wn, kvw), lambda n: (n, ck)),
                  pl.BlockSpec((wn, kvw), lambda n: (prev(n), cv)), pl.BlockSpec((wn, kvw), lambda n: (n, cv)),
                  pl.BlockSpec((ATTN_KV_HEADS, grp * wn, 2 * wn), lambda n: (0, 0, 0)),
                  pl.BlockSpec(memory_space=pltpu.SMEM)]
        + [pl.BlockSpec((wn, gb), lambda n, i=i: (n, cg + i)) for i in range(n_gb)],
        out_specs=[row, row],
        out_shape=[jax.ShapeDtypeStruct((t, w), F32), jax.ShapeDtypeStruct((t, w), BF16)], name=name,
        compiler_params=_cp(("parallel",)))(u, u, u, u, u, bias_g, sinks, *([u] * n_gb))


def _attn_bwd(u, o, dp, bias_g, sinks, off, w, name):
    t = u.shape[0]
    wn, ad, kvw, gb = WINDOW, ATTN_HEAD_DIM, KV_WIDTH, GATE_BLOCK
    grp = (w // ad) // ATTN_KV_HEADS
    nb = t // wn
    n_gb = w // gb
    cq, ck, cv, cg = off["b_q"] // w, off["b_k"] // kvw, off["b_v"] // kvw, off["b_g"] // gb

    def body(q_ref, kp_ref, kc_ref, vp_ref, vc_ref, o_ref, dp_ref, bias_ref, sink_ref, *rest):
        g_refs = rest[:n_gb]
        dq_ref, dkv_ref, dbias_ref, dsink_ref, dg_ref, do_ref, carry = rest[n_gb:]
        n = pl.program_id(0)

        @pl.when(n == 0)
        def _():
            dbias_ref[...] = jnp.zeros_like(dbias_ref)
            dsink_ref[...] = jnp.zeros_like(dsink_ref)
            carry[...] = jnp.zeros_like(carry)

        @pl.when(n == nb)
        def _():
            dkv_ref[...] = carry[...].astype(BF16)

        @pl.when(n < nb)
        def _():
            block(n, q_ref, kp_ref, kc_ref, vp_ref, vc_ref, o_ref, dp_ref, bias_ref, sink_ref, g_refs,
                  dq_ref, dkv_ref, dbias_ref, dsink_ref, dg_ref, do_ref, carry)

    def block(n, q_ref, kp_ref, kc_ref, vp_ref, vc_ref, o_ref, dp_ref, bias_ref, sink_ref, g_refs,
              dq_ref, dkv_ref, dbias_ref, dsink_ref, dg_ref, do_ref, carry):
        for i in range(n_gb):
            sl = slice(i * gb, (i + 1) * gb)
            silu, dsilu = _silu_parts(g_refs[i][...])
            dp_v = dp_ref[:, sl]
            do_ref[:, sl] = dp_v * silu
            dg_ref[:, sl] = (dp_v * o_ref[:, sl] * dsilu).astype(BF16)

        lane = lax.broadcasted_iota(jnp.int32, (1, LANES), 1)
        rr = lax.broadcasted_iota(jnp.int32, (grp * wn, 1), 0) >> WINDOW_SHIFT
        dsink_row = jnp.zeros((1, LANES), F32)
        no_prev = _no_prev_block(n, grp)
        for hh in range(ATTN_KV_HEADS):
            qs, kw, p, psink = _attn_probs(no_prev, q_ref, kp_ref, kc_ref, bias_ref, sink_ref, hh, grp)
            ksl = slice(hh * ad, (hh + 1) * ad)
            vw = jnp.concatenate([vp_ref[:, ksl], vc_ref[:, ksl]], axis=0).astype(BF16)
            hs = [slice((hh * grp + g) * ad, (hh * grp + g + 1) * ad) for g in range(grp)]
            dos = jnp.concatenate([do_ref[:, sl] for sl in hs], axis=0)
            os_ = jnp.concatenate([o_ref[:, sl] for sl in hs], axis=0)
            delta = jnp.sum(dos * os_, axis=1, keepdims=True)
            dos_b = dos.astype(BF16)
            dp = _dot_nt(dos_b, vw)
            ds = p * (dp - delta)
            dbias_ref[hh] += ds
            sd = psink * delta
            for g in range(grp):
                val = -jnp.sum(jnp.where(rr == g, sd, 0.0))
                dsink_row = jnp.where(lane == hh * grp + g, val, dsink_row)
            ds_b = (ds * (ad ** -0.5)).astype(BF16)
            dq = _dot_nn(ds_b, kw)
            for g in range(grp):
                dq_ref[:, hs[g]] = dq[g * wn:(g + 1) * wn].astype(BF16)
            dkw = _dot_tn(ds_b, qs)
            dvw = _dot_tn(p.astype(BF16), dos_b)
            vsl = slice(kvw + hh * ad, kvw + (hh + 1) * ad)
            dkv_ref[:, ksl] = (carry[:, ksl] + dkw[:wn]).astype(BF16)
            dkv_ref[:, vsl] = (carry[:, vsl] + dvw[:wn]).astype(BF16)
            carry[:, ksl] = dkw[wn:]
            carry[:, vsl] = dvw[wn:]
        dsink_ref[0:1, :] += dsink_row

    cur = lambda n: jnp.minimum(n, nb - 1)
    prev = lambda n: jnp.maximum(cur(n) - 1, 0)
    row = pl.BlockSpec((wn, w), lambda n: (cur(n), 0))
    return pl.pallas_call(
        body, grid=(nb + 1,),
        in_specs=[pl.BlockSpec((wn, w), lambda n: (cur(n), cq)),
                  pl.BlockSpec((wn, kvw), lambda n: (prev(n), ck)), pl.BlockSpec((wn, kvw), lambda n: (cur(n), ck)),
                  pl.BlockSpec((wn, kvw), lambda n: (prev(n), cv)), pl.BlockSpec((wn, kvw), lambda n: (cur(n), cv)),
                  row, row,
                  pl.BlockSpec((ATTN_KV_HEADS, grp * wn, 2 * wn), lambda n: (0, 0, 0)),
                  pl.BlockSpec(memory_space=pltpu.SMEM)]
        + [pl.BlockSpec((wn, gb), lambda n, i=i: (cur(n), cg + i)) for i in range(n_gb)],
        out_specs=[row, pl.BlockSpec((wn, 2 * kvw), lambda n: (jnp.maximum(n - 1, 0), 0)),
                   pl.BlockSpec((ATTN_KV_HEADS, grp * wn, 2 * wn), lambda n: (0, 0, 0)),
                   pl.BlockSpec((SUBLANES, LANES), lambda n: (0, 0)), row],
        out_shape=[jax.ShapeDtypeStruct((t, w), BF16), jax.ShapeDtypeStruct((t, 2 * kvw), BF16),
                   jax.ShapeDtypeStruct((ATTN_KV_HEADS, grp * wn, 2 * wn), F32), jax.ShapeDtypeStruct((SUBLANES, LANES), F32),
                   jax.ShapeDtypeStruct((t, w), BF16)],
        scratch_shapes=[pltpu.VMEM((wn, w), F32), pltpu.VMEM((wn, 2 * kvw), F32)],
        name=name, compiler_params=_cp(("arbitrary",)))(u, u, u, u, u, o, dp, bias_g, sinks, *([u] * n_gb))


def _shift_down(h, tail, k, rows):
    nt = tail.shape[0]
    out = pltpu.roll(h, k, 0)
    for r in range(k):
        out = jnp.where(rows == r, tail[nt - k + r:nt - k + r + 1, :], out)
    return out


def _shift_up(h, head, k, rows):
    tt = h.shape[0]
    out = pltpu.roll(h, tt - k, 0)
    for r in range(k):
        out = jnp.where(rows == tt - k + r, head[r:r + 1, :], out)
    return out


def _conv_fwd(u, conv_w, off, w, name):
    t = u.shape[0]
    wb = 512
    c = lambda nme: off[nme] // wb

    def body(i, nt, cb_ref, cc_ref, ccp_ref, cx_ref, cxp_ref, cg_ref, w_ref, p_ref):
        h = cc_ref[...] * cx_ref[...]
        hp = jnp.where(i > 0, ccp_ref[...] * cxp_ref[...], 0.0)
        rows = lax.broadcasted_iota(jnp.int32, h.shape, 0)
        y = w_ref[0:1, :] * _shift_down(h, hp, 2, rows) + w_ref[1:2, :] * _shift_down(h, hp, 1, rows) + w_ref[2:3, :] * h
        silu, _ = _silu_parts(cg_ref[...])
        p_ref[...] = (cb_ref[...] * y * silu).astype(BF16)

    return _ew(body, name, t, w // wb, wb,
               [(u, "tile", c("c_b")), (u, "tile", c("c_c")), (u, "prev", c("c_c")), (u, "tile", c("c_x")),
                (u, "prev", c("c_x")), (u, "tile", c("c_g")), (conv_w, "row", 0)], [BF16])[0]


def _conv_bwd(dp, u, conv_w, off, w, name):
    t = u.shape[0]
    wb = 512
    c = lambda nme: off[nme] // wb

    def body(i, nt, dp_ref, dpn_ref, cb_ref, cbn_ref, cg_ref, cgn_ref, cc_ref, ccp_ref, cx_ref, cxp_ref, w_ref,
             dcb_ref, dcc_ref, dcx_ref, dcg_ref, acc_ref):
        @pl.when(i == 0)
        def _():
            acc_ref[...] = jnp.zeros_like(acc_ref)

        cc, cx, cb = cc_ref[...], cx_ref[...], cb_ref[...]
        h = cc * cx
        hp = jnp.where(i > 0, ccp_ref[...] * cxp_ref[...], 0.0)
        rows = lax.broadcasted_iota(jnp.int32, h.shape, 0)
        h1 = _shift_down(h, hp, 1, rows)
        h2 = _shift_down(h, hp, 2, rows)
        w0, w1, w2 = w_ref[0:1, :], w_ref[1:2, :], w_ref[2:3, :]
        y = w0 * h2 + w1 * h1 + w2 * h
        silu, dsilu = _silu_parts(cg_ref[...])
        dp_v = dp_ref[...]
        dcg_ref[...] = (dp_v * cb * y * dsilu).astype(BF16)
        dcb_ref[...] = (dp_v * y * silu).astype(BF16)
        dy = dp_v * cb * silu
        silu_n, _ = _silu_parts(cgn_ref[...])
        dyn = jnp.where(i < nt - 1, dpn_ref[...] * cbn_ref[...] * silu_n, 0.0)
        dh = w2 * dy + w1 * _shift_up(dy, dyn, 1, rows) + w0 * _shift_up(dy, dyn, 2, rows)
        dcc_ref[...] = (dh * cx).astype(BF16)
        dcx_ref[...] = (dh * cc).astype(BF16)
        acc_ref[0:1, :] += jnp.sum(dy * h2, axis=0, keepdims=True)
        acc_ref[1:2, :] += jnp.sum(dy * h1, axis=0, keepdims=True)
        acc_ref[2:3, :] += jnp.sum(dy * h, axis=0, keepdims=True)

    return _ew(body, name, t, w // wb, wb,
               [(dp, "tile", 0), (dp, "next", 0), (u, "tile", c("c_b")), (u, "next", c("c_b")),
                (u, "tile", c("c_g")), (u, "next", c("c_g")), (u, "tile", c("c_c")), (u, "prev", c("c_c")),
                (u, "tile", c("c_x")), (u, "prev", c("c_x")), (conv_w, "row", 0)],
               [BF16] * 4, accs=[SUBLANES])


def _proj_merge(p, w, ya, yb, u, off, name):
    t, k = p.shape
    d = w.shape[1]
    gb = GATE_BLOCK
    nb = d // gb
    tt = _tile(t, 256)
    cols = [off[nme] // gb for nme in ("m_a", "m_b", "m_c")]

    def body(p_ref, w_ref, ya_ref, yb_ref, *rest):
        m_refs, (yc_ref, mg_ref) = rest[:3 * nb], rest[3 * nb:]
        yc = _dot_nn(p_ref[...], w_ref[...])
        yc_ref[...] = yc
        for j in range(nb):
            sl = slice(j * gb, (j + 1) * gb)
            mg_ref[:, sl] = (_sigmoid(m_refs[j][...]) * ya_ref[:, sl] + _sigmoid(m_refs[nb + j][...]) * yb_ref[:, sl]
                             + _sigmoid(m_refs[2 * nb + j][...]) * yc[:, sl]).astype(BF16)

    row = pl.BlockSpec((tt, d), lambda i: (i, 0))
    gates = [pl.BlockSpec((tt, gb), lambda i, c=c0 + j: (i, c)) for c0 in cols for j in range(nb)]
    return pl.pallas_call(
        body, grid=(t // tt,),
        in_specs=[pl.BlockSpec((tt, k), lambda i: (i, 0)), pl.BlockSpec((k, d), lambda i: (0, 0)), row, row] + gates,
        out_specs=[row, row],
        out_shape=[jax.ShapeDtypeStruct((t, d), F32), jax.ShapeDtypeStruct((t, d), BF16)],
        name=name, compiler_params=_cp(("parallel",)))(p, w, ya, yb, *([u] * (3 * nb)))


def _lower_bounds(lb_param):
    def body(p_ref, o_ref):
        p = p_ref[...]
        e = jnp.exp(p - jnp.max(p, axis=0, keepdims=True))
        soft = e / jnp.sum(e, axis=0, keepdims=True)
        acc = jnp.zeros_like(soft[0:1])
        o_ref[0:1, :] = acc
        for l in range(1, DEPTH):
            acc = acc + soft[l:l + 1]
            o_ref[l:l + 1, :] = acc

    return pl.pallas_call(body, out_shape=jax.ShapeDtypeStruct(lb_param.shape, F32), name="lower_bounds",
                          compiler_params=_cp())(lb_param)


def _lower_bounds_bwd(lb_param, dlower):
    def body(p_ref, d_ref, o_ref):
        p = p_ref[...]
        e = jnp.exp(p - jnp.max(p, axis=0, keepdims=True))
        soft = e / jnp.sum(e, axis=0, keepdims=True)
        dl = d_ref[...]
        ds = [jnp.zeros_like(dl[0:1])]
        for j in range(1, DEPTH):
            acc = dl[j:j + 1]
            for l in range(j + 1, DEPTH):
                acc = acc + dl[l:l + 1]
            ds.append(acc)
        inner = ds[0] * soft[0:1]
        for j in range(1, DEPTH):
            inner = inner + ds[j] * soft[j:j + 1]
        for j in range(DEPTH):
            o_ref[j:j + 1, :] = soft[j:j + 1] * (ds[j] - inner)

    return pl.pallas_call(body, out_shape=jax.ShapeDtypeStruct(lb_param.shape, F32), name="lower_bounds_bwd",
                          compiler_params=_cp())(lb_param, dlower)


def _exchange(arrays, scatter, name, chips=False):
    n_arr = len(arrays)
    n_slot = N_DEV // 2 if chips else N_DEV

    def body(*refs):
        srcs, dsts = refs[:n_arr], refs[n_arr:2 * n_arr]
        send_sems, recv_sems, local_sems = refs[2 * n_arr:]
        me = (2 * lax.axis_index("x") + lax.axis_index("y") if chips
              else 4 * lax.axis_index("x") + 2 * lax.axis_index("y") + lax.axis_index("c"))
        copies = _peer_copies(srcs, dsts, send_sems, recv_sems, scatter, chips)
        for a in range(n_arr):
            copies.append(pltpu.make_async_copy(srcs[a].at[me] if scatter else srcs[a], dsts[a].at[me], local_sems.at[a]))
        for cp in copies:
            cp.start()
        for cp in copies:
            cp.wait()

    out_shape = [jax.ShapeDtypeStruct(a.shape if scatter else (n_slot,) + a.shape, a.dtype) for a in arrays]
    anyspec = pl.BlockSpec(memory_space=pl.ANY)
    res = pl.pallas_call(
        body, in_specs=[anyspec] * n_arr, out_specs=[anyspec] * n_arr, out_shape=out_shape,
        scratch_shapes=[pltpu.SemaphoreType.DMA((n_arr * (n_slot - 1),)), pltpu.SemaphoreType.DMA((n_arr * (n_slot - 1),)),
                        pltpu.SemaphoreType.DMA((n_arr,))],
        name=name)(*arrays)
    return list(res)


def _peer_copies(srcs, lands, send_sems, recv_sems, scatter, chips=False):
    x, y, c = lax.axis_index("x"), lax.axis_index("y"), lax.axis_index("c")
    flips = [k for k in range(1, N_DEV) if not (chips and k & 1)]
    slot = (lambda px, py, pc: 2 * px + py) if chips else (lambda px, py, pc: 4 * px + 2 * py + pc)
    copies = []
    for a in range(len(srcs)):
        for i, k in enumerate(flips):
            px = 1 - x if k & 4 else x
            py = 1 - y if k & 2 else y
            pc = 1 - c if k & 1 else c
            src = srcs[a].at[slot(px, py, pc)] if scatter else srcs[a]
            copies.append(pltpu.make_async_remote_copy(
                src_ref=src, dst_ref=lands[a].at[slot(x, y, c)],
                send_sem=send_sems.at[a * len(flips) + i], recv_sem=recv_sems.at[a * len(flips) + i],
                device_id=(px, py, pc), device_id_type=pl.DeviceIdType.MESH))
    return copies


def _gather_two_level(arrays, name):
    n_arr = len(arrays)
    per = N_DEV - 1

    def body(*refs):
        srcs, outs = refs[:n_arr], refs[n_arr:2 * n_arr]
        send_sems, recv_sems, local_sems = refs[2 * n_arr:]
        x, y, c = lax.axis_index("x"), lax.axis_index("y"), lax.axis_index("c")
        me, sibling = (x, y, c), (x, y, 1 - c)
        chips = [(1 - x, y), (x, 1 - y), (1 - x, 1 - y)]

        def copy(a, k, block, to, src=None):
            dst = outs[a].at[4 * block[0] + 2 * block[1] + block[2]]
            return pltpu.make_async_remote_copy(
                src_ref=dst if src is None else src, dst_ref=dst,
                send_sem=send_sems.at[a * per + k], recv_sem=recv_sems.at[a * per + k],
                device_id=to, device_id_type=pl.DeviceIdType.MESH)

        own, first, passed = [], [], []
        for a in range(n_arr):
            own.append(pltpu.make_async_copy(srcs[a], outs[a].at[4 * x + 2 * y + c], local_sems.at[a]))
            first.append(copy(a, 0, me, sibling, src=srcs[a]))
            first += [copy(a, 1 + j, me, (*chip, c), src=srcs[a]) for j, chip in enumerate(chips)]
        for cp in own + first:
            cp.start()
        for a in range(n_arr):
            for j, chip in enumerate(chips):
                copy(a, 1 + j, (*chip, c), me).wait_recv()
                passed.append(copy(a, 4 + j, (*chip, c), sibling))
                passed[-1].start()
        for a in range(n_arr):
            copy(a, 0, sibling, me).wait_recv()
            for j, chip in enumerate(chips):
                copy(a, 4 + j, (*chip, 1 - c), me).wait_recv()
        for cp in first + passed:
            cp.wait_send()
        for cp in own:
            cp.wait()

    anyspec = pl.BlockSpec(memory_space=pl.ANY)
    res = pl.pallas_call(
        body, in_specs=[anyspec] * n_arr, out_specs=[anyspec] * n_arr,
        out_shape=[jax.ShapeDtypeStruct((N_DEV,) + a.shape, a.dtype) for a in arrays],
        scratch_shapes=[pltpu.SemaphoreType.DMA((n_arr * per,)), pltpu.SemaphoreType.DMA((n_arr * per,)),
                        pltpu.SemaphoreType.DMA((n_arr,))],
        name=name)(*arrays)
    return list(res)


def _sibling_swap(arrays, name):
    n_arr = len(arrays)
    n_chip = N_DEV // 2

    def body(*refs):
        srcs, outs = refs[:n_arr], refs[n_arr:2 * n_arr]
        send_sems, recv_sems = refs[2 * n_arr:]
        x, y, c = lax.axis_index("x"), lax.axis_index("y"), lax.axis_index("c")
        copies = []
        for a in range(n_arr):
            for j in range(n_chip):
                copies.append(pltpu.make_async_remote_copy(
                    src_ref=srcs[a].at[2 * j + 1 - c], dst_ref=outs[a].at[j],
                    send_sem=send_sems.at[a * n_chip + j], recv_sem=recv_sems.at[a * n_chip + j],
                    device_id=(x, y, 1 - c), device_id_type=pl.DeviceIdType.MESH))
        for cp in copies:
            cp.start()
        for cp in copies:
            cp.wait()

    anyspec = pl.BlockSpec(memory_space=pl.ANY)
    res = pl.pallas_call(
        body, in_specs=[anyspec] * n_arr, out_specs=[anyspec] * n_arr,
        out_shape=[jax.ShapeDtypeStruct((n_chip,) + a.shape[1:], a.dtype) for a in arrays],
        scratch_shapes=[pltpu.SemaphoreType.DMA((n_arr * n_chip,)), pltpu.SemaphoreType.DMA((n_arr * n_chip,))],
        name=name)(*arrays)
    return list(res)


def _pair_sum(send, stage, core, name):
    _, r, c = send.shape
    n_chip = stage.shape[0]
    tr = _tile(r, 128)

    def body(core_ref, a_ref, b_ref, o_ref):
        o_ref[...] = a_ref[...] + b_ref[...]

    return pl.pallas_call(
        body,
        grid_spec=pltpu.PrefetchScalarGridSpec(
            num_scalar_prefetch=1, grid=(n_chip, r // tr),
            in_specs=[pl.BlockSpec((1, tr, c), lambda j, i, core_ref: (2 * j + core_ref[0], i, 0)),
                      pl.BlockSpec((1, tr, c), lambda j, i, core_ref: (j, i, 0))],
            out_specs=pl.BlockSpec((1, tr, c), lambda j, i, core_ref: (j, i, 0))),
        out_shape=jax.ShapeDtypeStruct(stage.shape, F32), name=name,
        compiler_params=_cp(("parallel", "parallel")))(core, send, stage)


_HBM_SPEC = pl.BlockSpec(memory_space=pltpu.HBM)
_SEM_SPEC = pl.BlockSpec(memory_space=pltpu.SEMAPHORE)
_ANY_SPEC = pl.BlockSpec(memory_space=pl.ANY)
_DATAFLOW = pltpu.SideEffectType.DATAFLOW_SIDE_EFFECTING


def _exchange_start(arrays, scatter, name, dep=None, chips=False):
    n_arr = len(arrays)
    n_slot = N_DEV // 2 if chips else N_DEV
    n_sem = n_arr * (n_slot - 1)
    me = (2 * lax.axis_index("x") + lax.axis_index("y") if chips
          else 4 * lax.axis_index("x") + 2 * lax.axis_index("y") + lax.axis_index("c"))
    lands = []
    for a in arrays:
        own = lax.dynamic_index_in_dim(a, me, 0, keepdims=False) if scatter else a
        shape = a.shape if scatter else (n_slot,) + a.shape
        lands.append(lax.dynamic_update_index_in_dim(lax.empty(shape, a.dtype), own, me, 0))
    dep_specs, dep_args = _dep_specs(dep)

    def body(*refs):
        srcs, lnds = refs[:n_arr], refs[n_arr:2 * n_arr]
        outs = refs[2 * n_arr + len(dep_args):]
        send_sems, recv_sems, token = outs[0], outs[1], outs[2 + 2 * n_arr]
        for cp in _peer_copies(srcs, lnds, send_sems, recv_sems, scatter, chips):
            cp.start()
        token[...] = jnp.zeros_like(token)

    thru = [pltpu.HBM(a.shape, a.dtype) for a in list(arrays) + lands]
    return pl.pallas_call(
        body, name=name,
        out_shape=(pltpu.SemaphoreType.DMA((n_sem,)), pltpu.SemaphoreType.DMA((n_sem,)), *thru,
                   jax.ShapeDtypeStruct((SUBLANES, LANES), F32)),
        in_specs=[_HBM_SPEC] * (2 * n_arr) + dep_specs,
        out_specs=(_SEM_SPEC, _SEM_SPEC, *[_HBM_SPEC] * (2 * n_arr), pl.BlockSpec(memory_space=pltpu.VMEM)),
        input_output_aliases={i: 2 + i for i in range(2 * n_arr)},
        compiler_params=pltpu.CompilerParams(has_side_effects=_DATAFLOW),
    )(*[pltpu.with_memory_space_constraint(a, pltpu.HBM) for a in list(arrays) + lands], *dep_args)


def _exchange_wait(started, scatter, name, after, chips=False):
    send_sems, recv_sems = started[0], started[1]
    thru = list(started[2:-1])
    n_arr = len(thru) // 2
    after = list(after) if isinstance(after, (list, tuple)) else [after]

    def body(*refs):
        srcs, lnds = refs[:n_arr], refs[n_arr:2 * n_arr]
        for cp in _peer_copies(srcs, lnds, refs[2 * n_arr], refs[2 * n_arr + 1], scatter, chips):
            cp.wait_send()
            cp.wait_recv()

    res = pl.pallas_call(
        body, name=name, out_shape=tuple(pltpu.HBM(a.shape, a.dtype) for a in thru),
        in_specs=[_HBM_SPEC] * (2 * n_arr) + [_SEM_SPEC, _SEM_SPEC] + [_ANY_SPEC] * len(after),
        out_specs=tuple([_HBM_SPEC] * (2 * n_arr)),
        input_output_aliases={i: i for i in range(2 * n_arr)},
        compiler_params=pltpu.CompilerParams(has_side_effects=_DATAFLOW),
    )(*thru, send_sems, recv_sems, *after)
    return list(res[n_arr:])


def _unshard_cols(g, name):
    nd, r, s = g.shape
    tr = _tile(r, 64)

    def body(i_ref, o_ref):
        for p in range(nd):
            o_ref[:, p * s:(p + 1) * s] = i_ref[p]

    return pl.pallas_call(
        body, grid=(r // tr,), in_specs=[pl.BlockSpec((nd, tr, s), lambda i: (0, i, 0))],
        out_specs=pl.BlockSpec((tr, nd * s), lambda i: (i, 0)),
        out_shape=jax.ShapeDtypeStruct((r, nd * s), g.dtype), name=name, compiler_params=_cp(("parallel",)))(g)


def _shard_cols(g, name):
    r, n = g.shape
    s = n // N_DEV
    tr = _tile(r, 64)

    def body(i_ref, o_ref):
        for p in range(N_DEV):
            o_ref[p] = i_ref[:, p * s:(p + 1) * s]

    return pl.pallas_call(
        body, grid=(r // tr,), in_specs=[pl.BlockSpec((tr, n), lambda i: (i, 0))],
        out_specs=pl.BlockSpec((N_DEV, tr, s), lambda i: (0, i, 0)),
        out_shape=jax.ShapeDtypeStruct((N_DEV, r, s), g.dtype), name=name, compiler_params=_cp(("parallel",)))(g)


def _slot_sum(slots, name):
    nd, r, c = slots.shape
    tr = _tile(r, 64)

    def body(s_ref, o_ref):
        acc = s_ref[0]
        for p in range(1, nd):
            acc = acc + s_ref[p]
        o_ref[...] = acc

    return pl.pallas_call(
        body, grid=(r // tr,), in_specs=[pl.BlockSpec((nd, tr, c), lambda i: (0, i, 0))],
        out_specs=pl.BlockSpec((tr, c), lambda i: (i, 0)),
        out_shape=jax.ShapeDtypeStruct((r, c), F32), name=name, compiler_params=_cp(("parallel",)))(slots)


def _adamw(w, g, m, v, name):
    r, c = w.shape
    tr = _tile(r, 256)
    c1 = 1.0 - ADAM_B1 ** ADAM_STEP
    c2 = 1.0 - ADAM_B2 ** ADAM_STEP

    def body(w_ref, g_ref, m_ref, v_ref, d_ref, nm_ref, nv_ref):
        gv = g_ref[...]
        nm = ADAM_B1 * m_ref[...] + (1.0 - ADAM_B1) * gv
        nv = ADAM_B2 * v_ref[...] + (1.0 - ADAM_B2) * (gv * gv)
        nm_ref[...] = nm
        nv_ref[...] = nv
        d_ref[...] = -ADAM_LR * ((nm / c1) / (jnp.sqrt(nv / c2) + ADAM_EPS) + ADAM_WD * w_ref[...])

    spec = pl.BlockSpec((tr, c), lambda i: (i, 0))
    return pl.pallas_call(
        body, grid=(r // tr,), in_specs=[spec] * 4, out_specs=[spec] * 3,
        out_shape=[jax.ShapeDtypeStruct((r, c), F32)] * 3, name=name, compiler_params=_cp(("parallel",)))(w, g, m, v)


def _forward_backward(x, target, weights_hook, grads_hook, lb_param, hgrn_norm_g, attn_sinks, rel_bias, ln_g, ln_b):
    t, d = x.shape
    w = d // 2
    off, n_in = _offsets(d)
    n_heads = w // ATTN_HEAD_DIM
    grp = n_heads // ATTN_KV_HEADS

    lower = _lower_bounds(lb_param)
    bias = _bias_table(rel_bias, n_heads)
    bias_g = bias.reshape(ATTN_KV_HEADS, grp * WINDOW, 2 * WINDOW)

    saved, weights = [], []
    xb = x.astype(BF16)
    for l in range(DEPTH):
        wl, token = weights_hook(l, x)
        weights.append(wl)
        s = {"x": x, "xb": xb}
        u = _mm_nt_cols(xb, wl["w_in_t"], f"in_proj", dep=token)
        s["u"] = u
        lb_l, gain_l, cw_l = lower[l:l + 1], hgrn_norm_g[l:l + 1], wl["conv_w"]
        o_a, states, p_a = _hgrn_fwd(u, lb_l, gain_l, off, f"hgrn_fwd")
        o_b, p_b = _attn_fwd(u, bias_g, attn_sinks[l], off, w, f"attn_fwd")
        p_c = _conv_fwd(u, cw_l, off, w, f"conv_fwd")
        y_a = _mm_nn(p_a, wl["w_proj_hgrn"], f"proj_a", tn=2048)
        y_b = _mm_nn(p_b, wl["w_proj_attn"], f"proj_b", tn=2048)
        y_c, merged = _proj_merge(p_c, wl["w_proj_conv"], y_a, y_b, u, off, f"proj_c_merge")
        x, xb, xhat, rstd = _out_proj_ln(merged, wl["w_out"], x, ln_g[l:l + 1], ln_b[l:l + 1], f"out_proj_ln")
        s.update(o_a=o_a, states=states, p_a=p_a, o_b=o_b, p_b=p_b, p_c=p_c, y_a=y_a, y_b=y_b, y_c=y_c,
                 merged=merged, xhat=xhat, rstd=rstd)
        saved.append(s)

    loss_acc, dx = _loss_head(x, target)

    d_ln, d_lower, d_gain, d_sink, d_conv = [None] * DEPTH, [None] * DEPTH, [None] * DEPTH, [None] * DEPTH, [None] * DEPTH
    dbias_total = None
    for l in reversed(range(DEPTH)):
        wl, s = weights[l], saved[l]
        u = s["u"]
        lb_l, gain_l, cw_l = lower[l:l + 1], hgrn_norm_g[l:l + 1], wl["conv_w"]
        dz, dzb, d_ln[l], dya, dyb, dyc, dma, dmb, dmc = _ln_bwd_out_proj_merge(
            dx, s["xhat"], s["rstd"], ln_g[l:l + 1], wl["w_out"], u, s["y_a"], s["y_b"], s["y_c"], off,
            f"ln_bwd_merge_bwd")
        g_out = _mm_tn(s["merged"], dzb, f"g_out", tn=2048)
        g_pa = _mm_tn(s["p_a"], dya, f"g_proj_a", tn=2048)
        g_pb = _mm_tn(s["p_b"], dyb, f"g_proj_b", tn=2048)
        g_pc = _mm_tn(s["p_c"], dyc, f"g_proj_c", tn=2048)
        dpa = _mm_nt(dya, wl["w_proj_hgrn"], f"d_p_a", tk=2048)
        dpb = _mm_nt(dyb, wl["w_proj_attn"], f"d_p_b", tk=2048)
        dpc = _mm_nt(dyc, wl["w_proj_conv"], f"d_p_c", tk=2048)
        d_aq, d_af, d_ai, d_ag, acc_a = _hgrn_bwd(u, lb_l, gain_l, s["states"], s["o_a"], dpa, off, f"hgrn_bwd")
        d_lower[l], d_gain[l] = acc_a[0:1], acc_a[1:2]
        d_bq, d_bkv, dbias_l, d_sink[l], d_bg = _attn_bwd(
            u, s["o_b"], dpb, bias_g, attn_sinks[l], off, w, f"attn_bwd")
        dbias_total = dbias_l if dbias_total is None else dbias_total + dbias_l
        d_cb, d_cc, d_cx, d_cg, d_conv[l] = _conv_bwd(dpc, u, cw_l, off, w, f"conv_bwd")
        du = jnp.concatenate([d_aq, d_af, d_ai, d_ag, d_bq, d_bkv, d_bg, d_cb, d_cc, d_cx, d_cg, dma, dmb, dmc], axis=1)
        g_in_t = _mm_tn_rows(du, s["xb"], f"g_in")
        token = grads_hook(l, {"w_in_t": g_in_t, "w_proj_hgrn": g_pa, "w_proj_attn": g_pb, "w_proj_conv": g_pc, "w_out": g_out})
        dx = _mm_nn_acc(du, wl["w_in_t"], f"d_x", add=dz, add_scale=ALPHA, dep=token)

    d_lower_all = jnp.concatenate([a[0:1] for a in d_lower], axis=0)
    small = {
        "lb_param": _lower_bounds_bwd(lb_param, d_lower_all),
        "hgrn_norm_g": jnp.concatenate([a[0:1] for a in d_gain], axis=0),
        "attn_sinks": jnp.concatenate([a[0:1, :n_heads] for a in d_sink], axis=0),
        "conv_w": jnp.stack([a[0:3] for a in d_conv], axis=0),
        "rel_bias": _bias_grad(dbias_total.reshape(n_heads, WINDOW, 2 * WINDOW), n_heads)[:, :n_heads],
        "ln_g": jnp.concatenate([a[0:1] for a in d_ln], axis=0),
        "ln_b": jnp.concatenate([a[1:2] for a in d_ln], axis=0),
    }
    return loss_acc, dx, small


BIG = ("w_in", "w_proj_hgrn", "w_proj_attn", "w_proj_conv", "w_out")
SMALL = ("lb_param", "hgrn_norm_g", "attn_sinks", "conv_w", "rel_bias", "ln_g", "ln_b")
ORDER = ("w_in", "w_proj_hgrn", "w_proj_attn", "w_proj_conv", "w_out", "lb_param", "hgrn_norm_g", "attn_sinks",
         "conv_w", "rel_bias", "ln_g", "ln_b")


def _pack(parts):
    flat = jnp.concatenate([p.reshape(-1) for p in parts])
    n = flat.shape[0]
    unit = SUBLANES * LANES
    total = -(-n // unit) * unit
    return jnp.pad(flat, (0, total - n)).reshape(total // LANES, LANES)


def _unpack(packed, shapes):
    flat = packed.reshape(-1)
    out, o = [], 0
    for shp in shapes:
        n = int(np.prod(shp))
        out.append(flat[o:o + n].reshape(shp))
        o += n
    return out


def kernel(x, w_in, w_proj_hgrn, w_proj_attn, w_proj_conv, w_out, lb_param, hgrn_norm_g, attn_sinks, conv_w, rel_bias, ln_g, ln_b, loss_target, m_w_in, m_w_proj_hgrn, m_w_proj_attn, m_w_proj_conv, m_w_out, m_lb_param, m_hgrn_norm_g, m_attn_sinks, m_conv_w, m_rel_bias, m_ln_g, m_ln_b, v_w_in, v_w_proj_hgrn, v_w_proj_attn, v_w_proj_conv, v_w_out, v_lb_param, v_hgrn_norm_g, v_attn_sinks, v_conv_w, v_rel_bias, v_ln_g, v_ln_b):
    params = dict(w_in=w_in, w_proj_hgrn=w_proj_hgrn, w_proj_attn=w_proj_attn, w_proj_conv=w_proj_conv, w_out=w_out,
                  lb_param=lb_param, hgrn_norm_g=hgrn_norm_g, attn_sinks=attn_sinks, conv_w=conv_w, rel_bias=rel_bias,
                  ln_g=ln_g, ln_b=ln_b)
    mom_m = dict(w_in=m_w_in, w_proj_hgrn=m_w_proj_hgrn, w_proj_attn=m_w_proj_attn, w_proj_conv=m_w_proj_conv,
                 w_out=m_w_out, lb_param=m_lb_param, hgrn_norm_g=m_hgrn_norm_g, attn_sinks=m_attn_sinks,
                 conv_w=m_conv_w, rel_bias=m_rel_bias, ln_g=m_ln_g, ln_b=m_ln_b)
    mom_v = dict(w_in=v_w_in, w_proj_hgrn=v_w_proj_hgrn, w_proj_attn=v_w_proj_attn, w_proj_conv=v_w_proj_conv,
                 w_out=v_w_out, lb_param=v_lb_param, hgrn_norm_g=v_hgrn_norm_g, attn_sinks=v_attn_sinks,
                 conv_w=v_conv_w, rel_bias=v_rel_bias, ln_g=v_ln_g, ln_b=v_ln_b)
    d = x.shape[-1]
    me = 4 * lax.axis_index("x") + 2 * lax.axis_index("y") + lax.axis_index("c")
    for group in (params, mom_m, mom_v):
        group["w_in"] = jnp.swapaxes(group["w_in"], 1, 2)

    def shards_of(l):
        return [params[n][l].astype(BF16) for n in BIG] + [conv_w[l]]

    gathers = {}

    def weights_hook(l, x_in):
        if l == 0:
            got = _gather_two_level(shards_of(0), "gather_weights_0")
        else:
            got = _exchange_wait(gathers.pop(l), False, f"gather_wait_{l}", x_in)
        token = None
        if l + 1 < DEPTH:
            gathers[l + 1] = _exchange_start(shards_of(l + 1), False, f"gather_start_{l + 1}", dep=got[0])
            token = gathers[l + 1][-1]
        wl = {
            "w_in_t": got[0].reshape(-1, d),
            "w_proj_hgrn": _unshard_cols(got[1], "unshard_w_proj_hgrn"),
            "w_proj_attn": _unshard_cols(got[2], "unshard_w_proj_attn"),
            "w_proj_conv": _unshard_cols(got[3], "unshard_w_proj_conv"),
            "w_out": got[4].reshape(d, d),
            "conv_w": _unshard_cols(got[5], "unshard_conv_w"),
        }
        return wl, token

    grads = {n: [None] * DEPTH for n in BIG}
    scatters = {}

    def finish_scatter(l, after):
        got = _exchange_wait(scatters.pop(l), True, f"scatter_wait_{l}", after, chips=(l == 0))
        for n, slots in zip(BIG, got):
            grads[n][l] = _slot_sum(slots, f"sum_{n}_chips" if l == 0 else f"sum_{n}")
        return got[0]

    def grads_hook(l, g):
        send = [g["w_in_t"].reshape(N_DEV, -1, d), _shard_cols(g["w_proj_hgrn"], "shard_g_proj_a"),
                _shard_cols(g["w_proj_attn"], "shard_g_proj_b"), _shard_cols(g["w_proj_conv"], "shard_g_proj_c"),
                g["w_out"].reshape(N_DEV, d // N_DEV, d)]
        dep = finish_scatter(l + 1, send[0]) if l + 1 < DEPTH else None
        if l == 0:
            core = lax.axis_index("c").astype(jnp.int32).reshape(1)
            staged = _sibling_swap(send, "pair_swap_grads")
            send = [_pair_sum(s, st, core, f"pair_sum_{n}") for n, s, st in zip(BIG, send, staged)]
        scatters[l] = _exchange_start(send, True, f"scatter_start_{l}", dep=dep, chips=(l == 0))
        return scatters[l][-1]

    loss_acc, dx, small = _forward_backward(
        x[0], loss_target[0], weights_hook, grads_hook, lb_param, hgrn_norm_g, attn_sinks, rel_bias, ln_g, ln_b)
    loss = lax.psum(0.5 * jnp.sum(loss_acc[0]) / d, ("x", "y", "c"))
    finish_scatter(0, [dx] + [grads[n][l] for n in BIG for l in range(1, DEPTH)])
    for n in BIG:
        grads[n] = jnp.stack(grads[n], axis=0)

    small_shapes = [small[n].shape for n in SMALL]
    packed = _pack([small[n] for n in SMALL])
    got = _exchange([packed], False, "gather_small_grads")[0]
    summed = _unpack(_slot_sum(got, "sum_small_grads"), small_shapes)
    for n, g in zip(SMALL, summed):
        grads[n] = g
    cs = conv_w.shape[-1]
    grads["conv_w"] = lax.dynamic_slice_in_dim(grads["conv_w"], me * cs, cs, axis=2)

    delta, new_m, new_v = {}, {}, {}
    for n in BIG:
        shp = params[n].shape
        flat = lambda a: a.reshape(-1, shp[-1])
        dl, nm, nv = _adamw(flat(params[n]), flat(grads[n]), flat(mom_m[n]), flat(mom_v[n]), f"adamw_{n}")
        delta[n], new_m[n], new_v[n] = dl.reshape(shp), nm.reshape(shp), nv.reshape(shp)
    shapes = [params[n].shape for n in SMALL]
    res = _adamw(_pack([params[n] for n in SMALL]), _pack([grads[n] for n in SMALL]),
                 _pack([mom_m[n] for n in SMALL]), _pack([mom_v[n] for n in SMALL]), "adamw_small")
    for dst, packed_res in zip((delta, new_m, new_v), res):
        for n, a in zip(SMALL, _unpack(packed_res, shapes)):
            dst[n] = a

    for group in (grads, delta, new_m, new_v):
        group["w_in"] = jnp.swapaxes(group["w_in"], 1, 2)
    return (loss, dx[None], *[grads[n] for n in ORDER], *[delta[n] for n in ORDER],
            *[new_m[n] for n in ORDER], *[new_v[n] for n in ORDER])
```

```python
import functools
import math

import numpy as np
import jax
import jax.numpy as jnp
from jax import lax
from jax.experimental import pallas as pl
from jax.experimental.pallas import tpu as pltpu

F32 = jnp.float32
BF16 = jnp.bfloat16

N_DEV = 8
DEPTH = 4
HGRN_HEAD_DIM = 128
HGRN_CHUNK = 64
ATTN_HEAD_DIM = 64
ATTN_KV_HEADS = 4
KV_WIDTH = ATTN_KV_HEADS * ATTN_HEAD_DIM
WINDOW = 128
WINDOW_SHIFT = 7
N_BUCKETS = 32
MAX_DISTANCE = 128
ALPHA = (2.0 * DEPTH) ** 0.25
LN_EPS = 1e-5
RMS_EPS = 1e-6
ADAM_LR = 0.001
ADAM_B1 = 0.9
ADAM_B2 = 0.999
ADAM_EPS = 1e-08
ADAM_WD = 0.01
ADAM_STEP = 10

LANES = 128
SUBLANES = 8
VMEM_LIMIT = 56 << 20
NEG_INF = float("-inf")


def _offsets(d_model):
    w = d_model // 2
    sizes = (w, w, w, w, w, KV_WIDTH, KV_WIDTH, w, w, w, w, w, d_model, d_model, d_model)
    names = ("a_q", "a_f", "a_i", "a_g", "b_q", "b_k", "b_v", "b_g", "c_b", "c_c", "c_x", "c_g", "m_a", "m_b", "m_c")
    off, o = {}, 0
    for n, s in zip(names, sizes):
        off[n] = o
        o += s
    return off, o


def _tile(n, pref):
    t = min(pref, n)
    while n % t:
        t //= 2
    return t


def _cp(sem=None, vmem=VMEM_LIMIT):
    return pltpu.CompilerParams(dimension_semantics=sem, vmem_limit_bytes=vmem)


def _sigmoid(x):
    return 1.0 / (1.0 + jnp.exp(-x))


def _dot_nn(a, b):
    return jnp.dot(a, b, preferred_element_type=F32)


def _dot_nt(a, b):
    return lax.dot_general(a, b, (((1,), (1,)), ((), ())), preferred_element_type=F32)


def _dot_tn(a, b):
    return lax.dot_general(a, b, (((0,), (0,)), ((), ())), preferred_element_type=F32)


def _dep_specs(dep):
    return ([], []) if dep is None else ([pl.BlockSpec(memory_space=pl.ANY)], [dep])


def _mm_nn(a, b, name, out_dtype=F32, tm=1024, tn=1536, dep=None):
    m, k = a.shape
    _, n = b.shape
    tm, tn = _tile(m, tm), _tile(n, tn)
    dep_specs, dep_args = _dep_specs(dep)

    def body(a_ref, b_ref, *rest):
        o_ref = rest[-1]
        o_ref[...] = _dot_nn(a_ref[...], b_ref[...]).astype(o_ref.dtype)

    return pl.pallas_call(
        body, grid=(n // tn, m // tm),
        in_specs=[pl.BlockSpec((tm, k), lambda j, i: (i, 0)), pl.BlockSpec((k, tn), lambda j, i: (0, j))] + dep_specs,
        out_specs=pl.BlockSpec((tm, tn), lambda j, i: (i, j)),
        out_shape=jax.ShapeDtypeStruct((m, n), out_dtype), name=name,
        compiler_params=_cp(("parallel", "parallel")))(a, b, *dep_args)


def _mm_nt(a, b, name, tm=1024, tk=1536, add=None, add_scale=1.0, dep=None):
    m, k = a.shape
    n, _ = b.shape
    tm, tk = _tile(m, tm), _tile(k, tk)
    has_add = add is not None
    dep_specs, dep_args = _dep_specs(dep)

    def body(*refs):
        if has_add:
            a_ref, b_ref, add_ref = refs[:3]
        else:
            a_ref, b_ref = refs[:2]
        o_ref = refs[-1]
        if k == tk:
            prod = _dot_nt(a_ref[...], b_ref[...])
            o_ref[...] = prod + add_ref[...] * add_scale if has_add else prod
            return

        @pl.when(pl.program_id(1) == 0)
        def _():
            if has_add:
                o_ref[...] = add_ref[...] * add_scale
            else:
                o_ref[...] = jnp.zeros_like(o_ref)

        o_ref[...] += _dot_nt(a_ref[...], b_ref[...])

    in_specs = [pl.BlockSpec((tm, tk), lambda i, kk: (i, kk)), pl.BlockSpec((n, tk), lambda i, kk: (0, kk))]
    args = [a, b]
    if has_add:
        in_specs.append(pl.BlockSpec((tm, n), lambda i, kk: (i, 0)))
        args.append(add)
    in_specs += dep_specs
    args += dep_args
    return pl.pallas_call(
        body, grid=(m // tm, k // tk), in_specs=in_specs,
        out_specs=pl.BlockSpec((tm, n), lambda i, kk: (i, 0)),
        out_shape=jax.ShapeDtypeStruct((m, n), F32), name=name,
        compiler_params=_cp(("parallel", "arbitrary")))(*args)


def _mm_nt_cols(a, b, name, tm=1024, tn=1536, dep=None):
    m, k = a.shape
    n, _ = b.shape
    tm, tn = _tile(m, tm), _tile(n, tn)
    dep_specs, dep_args = _dep_specs(dep)

    def body(a_ref, b_ref, *rest):
        rest[-1][...] = _dot_nt(a_ref[...], b_ref[...])

    return pl.pallas_call(
        body, grid=(n // tn, m // tm),
        in_specs=[pl.BlockSpec((tm, k), lambda j, i: (i, 0)), pl.BlockSpec((tn, k), lambda j, i: (j, 0))] + dep_specs,
        out_specs=pl.BlockSpec((tm, tn), lambda j, i: (i, j)),
        out_shape=jax.ShapeDtypeStruct((m, n), F32), name=name,
        compiler_params=_cp(("parallel", "parallel")))(a, b, *dep_args)


def _mm_nn_acc(a, b, name, tm=1024, tk=1536, add=None, add_scale=1.0, dep=None):
    m, k = a.shape
    _, n = b.shape
    tm, tk = _tile(m, tm), _tile(k, tk)
    dep_specs, dep_args = _dep_specs(dep)

    def body(a_ref, b_ref, add_ref, *rest):
        o_ref = rest[-1]

        @pl.when(pl.program_id(1) == 0)
        def _():
            o_ref[...] = add_ref[...] * add_scale

        o_ref[...] += _dot_nn(a_ref[...], b_ref[...])

    return pl.pallas_call(
        body, grid=(m // tm, k // tk),
        in_specs=[pl.BlockSpec((tm, tk), lambda i, kk: (i, kk)), pl.BlockSpec((tk, n), lambda i, kk: (kk, 0)),
                  pl.BlockSpec((tm, n), lambda i, kk: (i, 0))] + dep_specs,
        out_specs=pl.BlockSpec((tm, n), lambda i, kk: (i, 0)),
        out_shape=jax.ShapeDtypeStruct((m, n), F32), name=name,
        compiler_params=_cp(("parallel", "arbitrary")))(a, b, add, *dep_args)


def _mm_tn_rows(a, b, name, tt=1024, tr=1536):
    t, k = a.shape
    _, n = b.shape
    tt, tr = _tile(t, tt), _tile(k, tr)

    def body(a_ref, b_ref, o_ref):
        @pl.when(pl.program_id(1) == 0)
        def _():
            o_ref[...] = jnp.zeros_like(o_ref)

        o_ref[...] += _dot_tn(a_ref[...], b_ref[...])

    return pl.pallas_call(
        body, grid=(k // tr, t // tt),
        in_specs=[pl.BlockSpec((tt, tr), lambda j, s: (s, j)), pl.BlockSpec((tt, n), lambda j, s: (s, 0))],
        out_specs=pl.BlockSpec((tr, n), lambda j, s: (j, 0)),
        out_shape=jax.ShapeDtypeStruct((k, n), F32), name=name,
        compiler_params=_cp(("parallel", "arbitrary")))(a, b)


def _mm_tn(a, b, name, tt=1024, tn=1536):
    t, k = a.shape
    _, n = b.shape
    tt, tn = _tile(t, tt), _tile(n, tn)

    def body(a_ref, b_ref, o_ref):
        @pl.when(pl.program_id(1) == 0)
        def _():
            o_ref[...] = jnp.zeros_like(o_ref)

        o_ref[...] += _dot_tn(a_ref[...], b_ref[...])

    return pl.pallas_call(
        body, grid=(n // tn, t // tt),
        in_specs=[pl.BlockSpec((tt, k), lambda j, s: (s, 0)), pl.BlockSpec((tt, tn), lambda j, s: (s, j))],
        out_specs=pl.BlockSpec((k, tn), lambda j, s: (0, j)),
        out_shape=jax.ShapeDtypeStruct((k, n), F32), name=name,
        compiler_params=_cp(("parallel", "arbitrary")))(a, b)


def _ew(body, name, t, ncol, wb, ins, outs, accs=(), tt=512):
    tt = _tile(t, tt)
    nt = t // tt
    per = tt // SUBLANES
    in_specs, args = [], []
    for arr, kind, coff in ins:
        if kind == "tile":
            spec = pl.BlockSpec((tt, wb), lambda j, i, c=coff: (i, c + j))
        elif kind == "prev":
            spec = pl.BlockSpec((SUBLANES, wb), lambda j, i, c=coff: (jnp.maximum(i * per - 1, 0), c + j))
        elif kind == "next":
            spec = pl.BlockSpec((SUBLANES, wb), lambda j, i, c=coff: (jnp.minimum((i + 1) * per, nt * per - 1), c + j))
        else:
            spec = pl.BlockSpec((arr.shape[0], wb), lambda j, i, c=coff: (0, c + j))
        in_specs.append(spec)
        args.append(arr)
    out_specs = [pl.BlockSpec((tt, wb), lambda j, i: (i, j)) for _ in outs]
    out_shape = [jax.ShapeDtypeStruct((t, ncol * wb), d) for d in outs]
    for r in accs:
        out_specs.append(pl.BlockSpec((r, wb), lambda j, i: (0, j)))
        out_shape.append(jax.ShapeDtypeStruct((r, ncol * wb), F32))

    def kern(*refs):
        body(pl.program_id(1), nt, *refs)

    res = pl.pallas_call(
        kern, grid=(ncol, nt), in_specs=in_specs, out_specs=out_specs, out_shape=out_shape, name=name,
        compiler_params=_cp(("parallel", "arbitrary")))(*args)
    return res


def _silu_parts(x):
    s = _sigmoid(x)
    return x * s, s + x * s * (1.0 - s)


def _out_proj_ln(a, w, x, g, b, name):
    t, d = x.shape
    k = a.shape[1]
    tt = _tile(t, 256)

    def body(a_ref, w_ref, x_ref, g_ref, b_ref, o_ref, ob_ref, xh_ref, r_ref):
        z = ALPHA * x_ref[...] + _dot_nn(a_ref[...], w_ref[...])
        mu = jnp.mean(z, axis=1, keepdims=True)
        zc = z - mu
        var = jnp.mean(zc * zc, axis=1, keepdims=True)
        rstd = lax.rsqrt(var + LN_EPS)
        xh = zc * rstd
        o = xh * g_ref[...] + b_ref[...]
        o_ref[...] = o
        ob_ref[...] = o.astype(BF16)
        xh_ref[...] = xh
        r_ref[...] = rstd

    row = pl.BlockSpec((tt, d), lambda i: (i, 0))
    vec = pl.BlockSpec((1, d), lambda i: (0, 0))
    return pl.pallas_call(
        body, grid=(t // tt,),
        in_specs=[pl.BlockSpec((tt, k), lambda i: (i, 0)), pl.BlockSpec((k, d), lambda i: (0, 0)), row, vec, vec],
        out_specs=[row, row, row, pl.BlockSpec((tt, 1), lambda i: (i, 0))],
        out_shape=[jax.ShapeDtypeStruct((t, d), F32), jax.ShapeDtypeStruct((t, d), BF16),
                   jax.ShapeDtypeStruct((t, d), F32), jax.ShapeDtypeStruct((t, 1), F32)],
        name=name, compiler_params=_cp(("parallel",)))(a, w, x, g, b)


def _ln_bwd_out_proj_merge(dout, xhat, rstd, g, w, u, ya, yb, yc, off, name):
    t, d = dout.shape
    gb = GATE_BLOCK
    nb = d // gb
    tt = _tile(t, 128)
    cols = [off[nme] // gb for nme in ("m_a", "m_b", "m_c")]

    def body(do_ref, xh_ref, r_ref, g_ref, w_ref, ya_ref, yb_ref, yc_ref, *rest):
        m_refs = rest[:3 * nb]
        dz_ref, dzb_ref, acc_ref = rest[3 * nb:3 * nb + 3]
        dy_refs, dg_refs = rest[3 * nb + 3:3 * nb + 6], rest[3 * nb + 6:]

        @pl.when(pl.program_id(0) == 0)
        def _():
            acc_ref[...] = jnp.zeros_like(acc_ref)

        do = do_ref[...]
        xh = xh_ref[...]
        dxh = do * g_ref[...]
        m1 = jnp.mean(dxh, axis=1, keepdims=True)
        m2 = jnp.mean(dxh * xh, axis=1, keepdims=True)
        dz = r_ref[...] * (dxh - m1 - xh * m2)
        dzb = dz.astype(BF16)
        dz_ref[...] = dz
        dzb_ref[...] = dzb
        acc_ref[0:1, :] += jnp.sum(do * xh, axis=0, keepdims=True)
        acc_ref[1:2, :] += jnp.sum(do, axis=0, keepdims=True)
        dm = _dot_nt(dzb, w_ref[...])
        for gate, y_ref in enumerate((ya_ref, yb_ref, yc_ref)):
            for j in range(nb):
                sl = slice(j * gb, (j + 1) * gb)
                s = _sigmoid(m_refs[gate * nb + j][...])
                dm_j = dm[:, sl]
                dy_refs[gate][:, sl] = (dm_j * s).astype(BF16)
                dg_refs[gate][:, sl] = (dm_j * y_ref[:, sl] * s * (1.0 - s)).astype(BF16)

    row = pl.BlockSpec((tt, d), lambda i: (i, 0))
    gates = [pl.BlockSpec((tt, gb), lambda i, c=c0 + j: (i, c)) for c0 in cols for j in range(nb)]
    return pl.pallas_call(
        body, grid=(t // tt,),
        in_specs=[row, row, pl.BlockSpec((tt, 1), lambda i: (i, 0)), pl.BlockSpec((1, d), lambda i: (0, 0)),
                  pl.BlockSpec((d, d), lambda i: (0, 0)), row, row, row] + gates,
        out_specs=[row, row, pl.BlockSpec((SUBLANES, d), lambda i: (0, 0))] + [row] * 6,
        out_shape=[jax.ShapeDtypeStruct((t, d), F32), jax.ShapeDtypeStruct((t, d), BF16),
                   jax.ShapeDtypeStruct((SUBLANES, d), F32)] + [jax.ShapeDtypeStruct((t, d), BF16)] * 6,
        name=name, compiler_params=_cp(("arbitrary",)))(dout, xhat, rstd, g, w, ya, yb, yc, *([u] * (3 * nb)))


def _loss_head(y, target):
    t, d = y.shape
    tt = _tile(t, 256)

    def body(y_ref, t_ref, acc_ref, dy_ref):
        @pl.when(pl.program_id(0) == 0)
        def _():
            acc_ref[...] = jnp.zeros_like(acc_ref)

        err = y_ref[...] - t_ref[...]
        dy_ref[...] = err * (1.0 / d)
        acc_ref[0:1, :] += jnp.sum(err * err, axis=0, keepdims=True)

    row = pl.BlockSpec((tt, d), lambda i: (i, 0))
    acc, dy = pl.pallas_call(
        body, grid=(t // tt,), in_specs=[row, row],
        out_specs=[pl.BlockSpec((SUBLANES, d), lambda i: (0, 0)), row],
        out_shape=[jax.ShapeDtypeStruct((SUBLANES, d), F32), jax.ShapeDtypeStruct((t, d), F32)],
        name="loss_head", compiler_params=_cp(("arbitrary",)))(y, target)
    return acc, dy


def _tri(lower):
    r = lax.broadcasted_iota(jnp.int32, (HGRN_CHUNK, HGRN_CHUNK), 0)
    c = lax.broadcasted_iota(jnp.int32, (HGRN_CHUNK, HGRN_CHUNK), 1)
    return jnp.where((r >= c) if lower else (r <= c), 1.0, 0.0).astype(BF16)


def _exact_tri_matmul(tri, x):
    hi = x.astype(BF16)
    r1 = x - hi.astype(F32)
    mid = r1.astype(BF16)
    lo = (r1 - mid.astype(F32)).astype(BF16)
    return _dot_nn(tri, hi) + _dot_nn(tri, mid) + _dot_nn(tri, lo)


def _hgrn_gates(q_raw, fl, lb):
    sq = _sigmoid(q_raw)
    qf = q_raw * sq * (HGRN_HEAD_DIM ** -0.5)
    sg = _sigmoid(fl)
    f = lb + (1.0 - lb) * sg
    return qf, sq, sg, f


HGRN_SUB = 16
HGRN_NSUB = HGRN_CHUNK // HGRN_SUB
HGRN_HEADS_PER_STEP = 8
HGRN_HEADS_PER_STEP_BWD = 8


def _diag_rows(r):
    return (r // SUBLANES) * SUBLANES


def _heads(x):
    hd = HGRN_HEAD_DIM
    return [x[:, i * hd:(i + 1) * hd] for i in range(x.shape[1] // hd)]


def _per_head(fn, *xs):
    split = [x if isinstance(x, (list, tuple)) else _heads(x) for x in xs]
    return jnp.concatenate([fn(*hs) for hs in zip(*split)], axis=1)


def _head_lane_sum(x):
    return _per_head(lambda h: jnp.broadcast_to(jnp.sum(h, axis=1, keepdims=True), h.shape), x)


def _hgrn_intra_fwd(qf, k, v, b):
    ch, sub, wd = HGRN_CHUNK, HGRN_SUB, qf.shape[1]
    tl = lax.broadcasted_iota(jnp.int32, (sub, wd), 0)
    blocks = []
    for m in range(HGRN_NSUB):
        rs = slice(m * sub, (m + 1) * sub)
        bm, qm, km, vm = b[rs], qf[rs], k[rs], v[rs]
        parts = {0: jnp.zeros((sub, wd), F32), SUBLANES: jnp.zeros((sub - SUBLANES, wd), F32)}
        for r in range(sub):
            lo = _diag_rows(r)
            e = jnp.exp(jnp.where(tl[lo:] >= r, bm[lo:] - bm[r:r + 1], NEG_INF))
            parts[lo] = parts[lo] + _head_lane_sum(qm[lo:] * e * km[r:r + 1]) * vm[r:r + 1]
        blocks.append(parts[0] + jnp.concatenate([jnp.zeros((SUBLANES, wd), F32), parts[SUBLANES]], axis=0))
    acc = jnp.concatenate(blocks, axis=0)
    for j in range(HGRN_NSUB - 1):
        lo = sub * (j + 1)
        c = b[lo - 1:lo, :]
        qj = (qf[lo:] * jnp.exp(b[lo:] - c)).astype(BF16)
        kj = (k[lo - sub:lo] * jnp.exp(c - b[lo - sub:lo])).astype(BF16)
        vj = v[lo - sub:lo].astype(BF16)
        contrib = _per_head(lambda q_, k_, v_: _dot_nn(_dot_nt(q_, k_).astype(BF16), v_), qj, kj, vj)
        acc = acc + jnp.concatenate([jnp.zeros((lo, wd), F32), contrib], axis=0)
    return acc


def _hgrn_intra_bwd(qf, k, v, b, do_v):
    ch, sub, wd = HGRN_CHUNK, HGRN_SUB, qf.shape[1]
    tl = lax.broadcasted_iota(jnp.int32, (sub, wd), 0)
    dq_blocks, dk_blocks, dv_blocks = [], [], []
    for m in range(HGRN_NSUB):
        rs = slice(m * sub, (m + 1) * sub)
        bm, qm, km, vm, dom = b[rs], qf[rs], k[rs], v[rs], do_v[rs]
        parts = {0: jnp.zeros((sub, wd), F32), SUBLANES: jnp.zeros((sub - SUBLANES, wd), F32)}
        dk_parts = {sub: jnp.zeros((sub, wd), F32), SUBLANES: jnp.zeros((SUBLANES, wd), F32)}
        dv_parts = {sub: jnp.zeros((sub, wd), F32), SUBLANES: jnp.zeros((SUBLANES, wd), F32)}
        for r in range(sub):
            lo = _diag_rows(r)
            b_r, k_r, v_r, q_r, do_r = bm[r:r + 1], km[r:r + 1], vm[r:r + 1], qm[r:r + 1], dom[r:r + 1]
            e = jnp.exp(jnp.where(tl[lo:] >= r, bm[lo:] - b_r, NEG_INF))
            parts[lo] = parts[lo] + _head_lane_sum(dom[lo:] * v_r) * (k_r * e)
            hi = lo + SUBLANES
            e2 = jnp.exp(jnp.where(tl[:hi] <= r, b_r - bm[:hi], NEG_INF))
            qe2 = q_r * e2
            dk_parts[hi] = dk_parts[hi] + _head_lane_sum(vm[:hi] * do_r) * qe2
            dv_parts[hi] = dv_parts[hi] + _head_lane_sum(km[:hi] * qe2) * do_r
        pad = jnp.zeros((SUBLANES, wd), F32)
        dq_blocks.append(parts[0] + jnp.concatenate([pad, parts[SUBLANES]], axis=0))
        dk_blocks.append(dk_parts[sub] + jnp.concatenate([dk_parts[SUBLANES], pad], axis=0))
        dv_blocks.append(dv_parts[sub] + jnp.concatenate([dv_parts[SUBLANES], pad], axis=0))
    dq = jnp.concatenate(dq_blocks, axis=0)
    dk = jnp.concatenate(dk_blocks, axis=0)
    dv = jnp.concatenate(dv_blocks, axis=0)
    do_b, v_b = do_v.astype(BF16), v.astype(BF16)
    dk_off, dv_off = [], []
    for j in range(HGRN_NSUB - 1):
        lo = sub * (j + 1)
        c = b[lo - 1:lo, :]
        eq = jnp.exp(b[lo:] - c)
        ek = jnp.exp(c - b[lo - sub:lo])
        qj = (qf[lo:] * eq).astype(BF16)
        kj = (k[lo - sub:lo] * ek).astype(BF16)
        doj, vj = do_b[lo:], v_b[lo - sub:lo]
        dq_j = _per_head(lambda do_, v_, k_: _dot_nn(_dot_nt(do_, v_).astype(BF16), k_), doj, vj, kj)
        dk_j = _per_head(lambda do_, v_, q_: _dot_nn(_dot_nt(v_, do_).astype(BF16), q_), doj, vj, qj)
        dv_j = _per_head(lambda do_, k_, q_: _dot_nn(_dot_nt(k_, q_).astype(BF16), do_), doj, kj, qj)
        dq = dq + jnp.concatenate([jnp.zeros((lo, wd), F32), dq_j * eq], axis=0)
        dk_off.append(dk_j * ek)
        dv_off.append(dv_j)
    zero = jnp.zeros((sub, wd), F32)
    dk = dk + jnp.concatenate(dk_off + [zero], axis=0)
    dv = dv + jnp.concatenate(dv_off + [zero], axis=0)
    return dq, dk, dv


def _head_rms(o):
    return lax.rsqrt(_head_lane_sum(o * o) * (1.0 / HGRN_HEAD_DIM) + RMS_EPS)


def _hgrn_fwd(u, lb, gain, off, name):
    t = u.shape[0]
    w = lb.shape[1]
    hd, ch, hp = HGRN_HEAD_DIM, HGRN_CHUNK, HGRN_HEADS_PER_STEP
    nh, nc = w // hd, t // ch
    wb = hp * hd
    cq, cf, cv, cg = off["a_q"] // wb, off["a_f"] // wb, off["a_i"] // wb, off["a_g"] // wb

    def body(q_ref, f_ref, v_ref, g_ref, lb_ref, gain_ref, o_ref, st_ref, p_ref, state):
        @pl.when(pl.program_id(1) == 0)
        def _():
            state[...] = jnp.zeros_like(state)

        sts = [state[i] for i in range(hp)]
        qf, _, _, f = _hgrn_gates(q_ref[...], f_ref[...], lb_ref[...])
        k = 1.0 - f
        v = v_ref[...]
        b = _exact_tri_matmul(_tri(True), jnp.log(f))
        inter = _per_head(lambda qa_, st_: _dot_nt(qa_, st_.astype(BF16)), (qf * jnp.exp(b)).astype(BF16), sts)
        o = inter + _hgrn_intra_fwd(qf, k, v, b)
        o_ref[...] = o
        silu, _ = _silu_parts(g_ref[...])
        p_ref[...] = (o * _head_rms(o) * gain_ref[...] * silu).astype(BF16)
        b_end = b[ch - 1:ch, :]
        a_end = _heads(jnp.exp(b_end))
        kd = _heads((k * jnp.exp(b_end - b)).astype(BF16))
        v_b = _heads(v.astype(BF16))
        for i in range(hp):
            st_ref[i, 0] = sts[i]
            state[i] = sts[i] * a_end[i] + _dot_tn(v_b[i], kd[i])

    return pl.pallas_call(
        body, grid=(nh // hp, nc),
        in_specs=[pl.BlockSpec((ch, wb), lambda h, n: (n, cq + h)),
                  pl.BlockSpec((ch, wb), lambda h, n: (n, cf + h)),
                  pl.BlockSpec((ch, wb), lambda h, n: (n, cv + h)),
                  pl.BlockSpec((ch, wb), lambda h, n: (n, cg + h)),
                  pl.BlockSpec((1, wb), lambda h, n: (0, h)),
                  pl.BlockSpec((1, wb), lambda h, n: (0, h))],
        out_specs=[pl.BlockSpec((ch, wb), lambda h, n: (n, h)),
                   pl.BlockSpec((hp, 1, hd, hd), lambda h, n: (h, n, 0, 0)),
                   pl.BlockSpec((ch, wb), lambda h, n: (n, h))],
        out_shape=[jax.ShapeDtypeStruct((t, w), F32), jax.ShapeDtypeStruct((nh, nc, hd, hd), F32),
                   jax.ShapeDtypeStruct((t, w), BF16)],
        scratch_shapes=[pltpu.VMEM((hp, hd, hd), F32)],
        name=name, compiler_params=_cp(("parallel", "arbitrary")))(u, u, u, u, lb, gain)


def _hgrn_bwd(u, lb, gain, states, o, dp, off, name):
    t = u.shape[0]
    w = lb.shape[1]
    hd, ch, hp = HGRN_HEAD_DIM, HGRN_CHUNK, HGRN_HEADS_PER_STEP_BWD
    nh, nc = w // hd, t // ch
    wb = hp * hd
    cq, cf, cv, cg = off["a_q"] // wb, off["a_f"] // wb, off["a_i"] // wb, off["a_g"] // wb

    def body(q_ref, f_ref, v_ref, g_ref, o_ref, dp_ref, st_ref, lb_ref, gain_ref,
             dq_ref, df_ref, dv_ref, dg_ref, dlb_ref, dstate):
        @pl.when(pl.program_id(1) == 0)
        def _():
            dstate[...] = jnp.zeros_like(dstate)
            dlb_ref[...] = jnp.zeros_like(dlb_ref)

        silu, dsilu = _silu_parts(g_ref[...])
        o_v, dp_v, gain_row = o_ref[...], dp_ref[...], gain_ref[...]
        rms = _head_rms(o_v)
        nrm = o_v * rms
        dg_ref[...] = (dp_v * nrm * gain_row * dsilu).astype(BF16)
        dlb_ref[1:2, :] += jnp.sum(dp_v * nrm * silu, axis=0, keepdims=True)
        dn = dp_v * gain_row * silu
        do_v = rms * (dn - nrm * (_head_lane_sum(dn * nrm) * (1.0 / HGRN_HEAD_DIM)))

        rows = lax.broadcasted_iota(jnp.int32, (ch, wb), 0)
        lb_row = lb_ref[...]
        q_raw = q_ref[...]
        qf, sq, sg, f = _hgrn_gates(q_raw, f_ref[...], lb_row)
        k = 1.0 - f
        b = _exact_tri_matmul(_tri(True), jnp.log(f))
        a = jnp.exp(b)
        b_end = b[ch - 1:ch, :]
        a_end = jnp.exp(b_end)
        to_end = jnp.exp(b_end - b)
        v = v_ref[...]
        st0 = [st_ref[i, 0] for i in range(hp)]
        ds = [dstate[i] for i in range(hp)]
        st0_b = [s_.astype(BF16) for s_ in st0]
        ds_b = [s_.astype(BF16) for s_ in ds]
        do_b, v_b, kd_b, qa_b = do_v.astype(BF16), v.astype(BF16), (k * to_end).astype(BF16), (qf * a).astype(BF16)

        dq_inter = a * _per_head(_dot_nn, do_b, st0_b)
        dk_end = to_end * _per_head(_dot_nn, v_b, ds_b)
        dv_end = _per_head(_dot_nt, kd_b, ds_b)
        a_end_h = _heads(a_end)
        st_end = [st0[i] * a_end_h[i] + _dot_tn(_heads(v_b)[i], _heads(kd_b)[i]) for i in range(hp)]
        db_end = jnp.concatenate([jnp.sum(ds[i] * st_end[i], axis=0, keepdims=True) for i in range(hp)], axis=1)
        ds_new = [ds[i] * a_end_h[i] + _dot_tn(_heads(do_b)[i], _heads(qa_b)[i]) for i in range(hp)]

        dq_intra, dk_intra, dv_intra = _hgrn_intra_bwd(qf, k, v, b, do_v)
        dqf = dq_inter + dq_intra
        dk = dk_end + dk_intra
        dv = dv_end + dv_intra
        db = qf * dqf - k * dk
        db = db + jnp.where(rows == ch - 1, db_end, 0.0)
        dg = _exact_tri_matmul(_tri(False), db)
        df = dg / f - dk
        for i in range(hp):
            dstate[i] = ds_new[i]
        dq_ref[...] = (dqf * (HGRN_HEAD_DIM ** -0.5) * (sq + q_raw * sq * (1.0 - sq))).astype(BF16)
        df_ref[...] = (df * (1.0 - lb_row) * sg * (1.0 - sg)).astype(BF16)
        dv_ref[...] = dv.astype(BF16)
        dlb_ref[0:1, :] += jnp.sum(df * (1.0 - sg), axis=0, keepdims=True)

    rev = lambda n: nc - 1 - n
    tile = lambda c: pl.BlockSpec((ch, wb), lambda h, n, c=c: (rev(n), c + h))
    return pl.pallas_call(
        body, grid=(nh // hp, nc),
        in_specs=[tile(cq), tile(cf), tile(cv), tile(cg), tile(0), tile(0),
                  pl.BlockSpec((hp, 1, hd, hd), lambda h, n: (h, rev(n), 0, 0)),
                  pl.BlockSpec((1, wb), lambda h, n: (0, h)), pl.BlockSpec((1, wb), lambda h, n: (0, h))],
        out_specs=[tile(0), tile(0), tile(0), tile(0), pl.BlockSpec((SUBLANES, wb), lambda h, n: (0, h))],
        out_shape=[jax.ShapeDtypeStruct((t, w), BF16)] * 4 + [jax.ShapeDtypeStruct((SUBLANES, w), F32)],
        scratch_shapes=[pltpu.VMEM((hp, hd, hd), F32)],
        name=name, compiler_params=_cp(("parallel", "arbitrary")))(u, u, u, u, o, dp, states, lb, gain)


def _bucket_map():
    i = np.arange(WINDOW)[:, None]
    j = np.arange(2 * WINDOW)[None, :]
    dist = np.clip(WINDOW + i - j, 0, WINDOW - 1)
    max_exact = N_BUCKETS // 2
    logd = (np.log(np.maximum(dist, 1).astype(np.float32) / max_exact) / math.log(MAX_DISTANCE / max_exact))
    large = np.minimum(max_exact + (logd.astype(np.float32) * (N_BUCKETS - max_exact)).astype(np.int32), N_BUCKETS - 1)
    return np.where(dist < max_exact, dist, large).astype(np.int32)


def _bias_table(rel_bias, n_heads):
    bucket = jnp.asarray(_bucket_map())

    def body(rb_ref, bk_ref, o_ref):
        bk = bk_ref[...]
        i = lax.broadcasted_iota(jnp.int32, (WINDOW, 2 * WINDOW), 0)
        j = lax.broadcasted_iota(jnp.int32, (WINDOW, 2 * WINDOW), 1)
        band = ((j >= WINDOW) & (j - WINDOW <= i)) | ((j < WINDOW) & (j > i))
        for h in range(n_heads):
            def step(bi, acc):
                return jnp.where(bk == bi, rb_ref[bi, h], acc)
            table = lax.fori_loop(0, N_BUCKETS, step, jnp.zeros((WINDOW, 2 * WINDOW), F32))
            o_ref[h] = jnp.where(band, table, NEG_INF)

    return pl.pallas_call(
        body, in_specs=[pl.BlockSpec(memory_space=pltpu.SMEM), pl.BlockSpec(memory_space=pltpu.VMEM)],
        out_specs=pl.BlockSpec(memory_space=pltpu.VMEM),
        out_shape=jax.ShapeDtypeStruct((n_heads, WINDOW, 2 * WINDOW), F32), name="bias_table",
        compiler_params=_cp())(rel_bias, bucket)


def _bias_grad(dbias, n_heads):
    bucket = jnp.asarray(_bucket_map())

    def body(db_ref, bk_ref, o_ref):
        bk = bk_ref[...]
        lane = lax.broadcasted_iota(jnp.int32, (1, LANES), 1)

        def step(bi, carry):
            row = jnp.zeros((1, LANES), F32)
            for h in range(n_heads):
                val = jnp.sum(jnp.where(bk == bi, db_ref[h], 0.0))
                row = jnp.where(lane == h, val, row)
            o_ref[pl.ds(bi, 1), :] = row
            return carry

        lax.fori_loop(0, N_BUCKETS, step, 0)

    return pl.pallas_call(
        body, in_specs=[pl.BlockSpec(memory_space=pltpu.VMEM), pl.BlockSpec(memory_space=pltpu.VMEM)],
        out_specs=pl.BlockSpec(memory_space=pltpu.VMEM),
        out_shape=jax.ShapeDtypeStruct((N_BUCKETS, LANES), F32), name="bias_grad",
        compiler_params=_cp())(dbias, bucket)


ATTN_STACK = 2


def _no_prev_block(n, grp):
    j = lax.broadcasted_iota(jnp.int32, (ATTN_STACK * WINDOW, 2 * WINDOW), 1)
    return (j < WINDOW) & (n == 0)


def _kv_window(p_ref, c_ref, hh):
    ksl = slice(hh * ATTN_HEAD_DIM, (hh + 1) * ATTN_HEAD_DIM)
    return jnp.concatenate([p_ref[:, ksl], c_ref[:, ksl]], axis=0).astype(BF16)


def _attn_probs(no_prev, q_ref, kw, bias_ref, sink_ref, hh, g0, grp):
    ad, wn, ns = ATTN_HEAD_DIM, WINDOW, ATTN_STACK
    heads = [hh * grp + g0 + g for g in range(ns)]
    qs = jnp.concatenate([q_ref[:, h * ad:(h + 1) * ad] for h in heads], axis=0).astype(BF16)
    s = _dot_nt(qs, kw) * (ad ** -0.5) + bias_ref[hh, g0 * wn:(g0 + ns) * wn, :]
    s = jnp.where(no_prev, NEG_INF, s)
    rr = lax.broadcasted_iota(jnp.int32, (ns * wn, 1), 0) >> WINDOW_SHIFT
    sink = jnp.zeros((ns * wn, 1), F32)
    for g in range(ns):
        sink = jnp.where(rr == g, sink_ref[heads[g]], sink)
    m = jnp.maximum(jnp.max(s, axis=1, keepdims=True), sink)
    p = jnp.exp(s - m)
    es = jnp.exp(sink - m)
    inv = 1.0 / (jnp.sum(p, axis=1, keepdims=True) + es)
    return qs, p * inv, es * inv


GATE_BLOCK = 512


def _attn_fwd(u, bias_g, sinks, off, w, name):
    t = u.shape[0]
    wn, ad, kvw, gb = WINDOW, ATTN_HEAD_DIM, KV_WIDTH, GATE_BLOCK
    grp = (w // ad) // ATTN_KV_HEADS
    nb = t // wn
    n_gb = w // gb
    cq, ck, cv, cg = off["b_q"] // w, off["b_k"] // kvw, off["b_v"] // kvw, off["b_g"] // gb

    def body(q_ref, kp_ref, kc_ref, vp_ref, vc_ref, bias_ref, sink_ref, *rest):
        g_refs, (o_ref, p_ref) = rest[:n_gb], rest[n_gb:]
        no_prev = _no_prev_block(pl.program_id(0), grp)
        for hh in range(ATTN_KV_HEADS):
            kw = _kv_window(kp_ref, kc_ref, hh)
            vw = _kv_window(vp_ref, vc_ref, hh)
            for g0 in range(0, grp, ATTN_STACK):
                _, p, _ = _attn_probs(no_prev, q_ref, kw, bias_ref, sink_ref, hh, g0, grp)
                o = _dot_nn(p.astype(BF16), vw)
                for g in range(ATTN_STACK):
                    head = hh * grp + g0 + g
                    o_ref[:, head * ad:(head + 1) * ad] = o[g * wn:(g + 1) * wn]
        for i in range(n_gb):
            sl = slice(i * gb, (i + 1) * gb)
            silu, _ = _silu_parts(g_refs[i][...])
            p_ref[:, sl] = (o_ref[:, sl] * silu).astype(BF16)

    prev = lambda n: jnp.maximum(n - 1, 0)
    row = pl.BlockSpec((wn, w), lambda n: (n, 0))
    return pl.pallas_call(
        body, grid=(nb,),
        in_specs=[pl.BlockSpec((wn, w), lambda n: (n, cq)),
                  pl.BlockSpec((wn, kvw), lambda n: (prev(n), ck)), pl.BlockSpec((wn, kvw), lambda n: (n, ck)),
                  pl.BlockSpec((wn, kvw), lambda n: (prev(n), cv)), pl.BlockSpec((wn, kvw), lambda n: (n, cv)),
                  pl.BlockSpec((ATTN_KV_HEADS, grp * wn, 2 * wn), lambda n: (0, 0, 0)),
                  pl.BlockSpec(memory_space=pltpu.SMEM)]
        + [pl.BlockSpec((wn, gb), lambda n, i=i: (n, cg + i)) for i in range(n_gb)],
        out_specs=[row, row],
        out_shape=[jax.ShapeDtypeStruct((t, w), F32), jax.ShapeDtypeStruct((t, w), BF16)], name=name,
        compiler_params=_cp(("parallel",)))(u, u, u, u, u, bias_g, sinks, *([u] * n_gb))


def _attn_bwd(u, o, dp, bias_g, sinks, off, w, name):
    t = u.shape[0]
    wn, ad, kvw, gb = WINDOW, ATTN_HEAD_DIM, KV_WIDTH, GATE_BLOCK
    grp = (w // ad) // ATTN_KV_HEADS
    nb = t // wn
    n_gb = w // gb
    cq, ck, cv, cg = off["b_q"] // w, off["b_k"] // kvw, off["b_v"] // kvw, off["b_g"] // gb

    def body(q_ref, kp_ref, kc_ref, vp_ref, vc_ref, o_ref, dp_ref, bias_ref, sink_ref, *rest):
        g_refs = rest[:n_gb]
        dq_ref, dkv_ref, dbias_ref, dsink_ref, dg_ref, do_ref, carry = rest[n_gb:]
        n = pl.program_id(0)

        @pl.when(n == 0)
        def _():
            dbias_ref[...] = jnp.zeros_like(dbias_ref)
            dsink_ref[...] = jnp.zeros_like(dsink_ref)
            carry[...] = jnp.zeros_like(carry)

        @pl.when(n == nb)
        def _():
            dkv_ref[...] = carry[...].astype(BF16)

        @pl.when(n < nb)
        def _():
            block(n, q_ref, kp_ref, kc_ref, vp_ref, vc_ref, o_ref, dp_ref, bias_ref, sink_ref, g_refs,
                  dq_ref, dkv_ref, dbias_ref, dsink_ref, dg_ref, do_ref, carry)

    def block(n, q_ref, kp_ref, kc_ref, vp_ref, vc_ref, o_ref, dp_ref, bias_ref, sink_ref, g_refs,
              dq_ref, dkv_ref, dbias_ref, dsink_ref, dg_ref, do_ref, carry):
        for i in range(n_gb):
            sl = slice(i * gb, (i + 1) * gb)
            silu, dsilu = _silu_parts(g_refs[i][...])
            dp_v = dp_ref[:, sl]
            do_ref[:, sl] = dp_v * silu
            dg_ref[:, sl] = (dp_v * o_ref[:, sl] * dsilu).astype(BF16)

        lane = lax.broadcasted_iota(jnp.int32, (1, LANES), 1)
        rr = lax.broadcasted_iota(jnp.int32, (ATTN_STACK * wn, 1), 0) >> WINDOW_SHIFT
        dsink_row = jnp.zeros((1, LANES), F32)
        no_prev = _no_prev_block(n, grp)
        ns = ATTN_STACK
        for hh in range(ATTN_KV_HEADS):
            ksl = slice(hh * ad, (hh + 1) * ad)
            kw = _kv_window(kp_ref, kc_ref, hh)
            vw = _kv_window(vp_ref, vc_ref, hh)
            dkw = jnp.zeros((2 * wn, ad), F32)
            dvw = jnp.zeros((2 * wn, ad), F32)
            for g0 in range(0, grp, ns):
                qs, p, psink = _attn_probs(no_prev, q_ref, kw, bias_ref, sink_ref, hh, g0, grp)
                hs = [slice((hh * grp + g0 + g) * ad, (hh * grp + g0 + g + 1) * ad) for g in range(ns)]
                dos = jnp.concatenate([do_ref[:, sl] for sl in hs], axis=0)
                os_ = jnp.concatenate([o_ref[:, sl] for sl in hs], axis=0)
                delta = jnp.sum(dos * os_, axis=1, keepdims=True)
                dos_b = dos.astype(BF16)
                ds = p * (_dot_nt(dos_b, vw) - delta)
                dbias_ref[hh, g0 * wn:(g0 + ns) * wn, :] += ds
                sd = psink * delta
                for g in range(ns):
                    val = -jnp.sum(jnp.where(rr == g, sd, 0.0))
                    dsink_row = jnp.where(lane == hh * grp + g0 + g, val, dsink_row)
                ds_b = (ds * (ad ** -0.5)).astype(BF16)
                dq = _dot_nn(ds_b, kw)
                for g in range(ns):
                    dq_ref[:, hs[g]] = dq[g * wn:(g + 1) * wn].astype(BF16)
                dkw = dkw + _dot_tn(ds_b, qs)
                dvw = dvw + _dot_tn(p.astype(BF16), dos_b)
            vsl = slice(kvw + hh * ad, kvw + (hh + 1) * ad)
            dkv_ref[:, ksl] = (carry[:, ksl] + dkw[:wn]).astype(BF16)
            dkv_ref[:, vsl] = (carry[:, vsl] + dvw[:wn]).astype(BF16)
            carry[:, ksl] = dkw[wn:]
            carry[:, vsl] = dvw[wn:]
        dsink_ref[0:1, :] += dsink_row

    cur = lambda n: jnp.minimum(n, nb - 1)
    prev = lambda n: jnp.maximum(cur(n) - 1, 0)
    row = pl.BlockSpec((wn, w), lambda n: (cur(n), 0))
    return pl.pallas_call(
        body, grid=(nb + 1,),
        in_specs=[pl.BlockSpec((wn, w), lambda n: (cur(n), cq)),
                  pl.BlockSpec((wn, kvw), lambda n: (prev(n), ck)), pl.BlockSpec((wn, kvw), lambda n: (cur(n), ck)),
                  pl.BlockSpec((wn, kvw), lambda n: (prev(n), cv)), pl.BlockSpec((wn, kvw), lambda n: (cur(n), cv)),
                  row, row,
                  pl.BlockSpec((ATTN_KV_HEADS, grp * wn, 2 * wn), lambda n: (0, 0, 0)),
                  pl.BlockSpec(memory_space=pltpu.SMEM)]
        + [pl.BlockSpec((wn, gb), lambda n, i=i: (cur(n), cg + i)) for i in range(n_gb)],
        out_specs=[row, pl.BlockSpec((wn, 2 * kvw), lambda n: (jnp.maximum(n - 1, 0), 0)),
                   pl.BlockSpec((ATTN_KV_HEADS, grp * wn, 2 * wn), lambda n: (0, 0, 0)),
                   pl.BlockSpec((SUBLANES, LANES), lambda n: (0, 0)), row],
        out_shape=[jax.ShapeDtypeStruct((t, w), BF16), jax.ShapeDtypeStruct((t, 2 * kvw), BF16),
                   jax.ShapeDtypeStruct((ATTN_KV_HEADS, grp * wn, 2 * wn), F32), jax.ShapeDtypeStruct((SUBLANES, LANES), F32),
                   jax.ShapeDtypeStruct((t, w), BF16)],
        scratch_shapes=[pltpu.VMEM((wn, w), F32), pltpu.VMEM((wn, 2 * kvw), F32)],
        name=name, compiler_params=_cp(("arbitrary",)))(u, u, u, u, u, o, dp, bias_g, sinks, *([u] * n_gb))


def _shift_down(h, tail, k, rows):
    nt = tail.shape[0]
    out = pltpu.roll(h, k, 0)
    for r in range(k):
        out = jnp.where(rows == r, tail[nt - k + r:nt - k + r + 1, :], out)
    return out


def _shift_up(h, head, k, rows):
    tt = h.shape[0]
    out = pltpu.roll(h, tt - k, 0)
    for r in range(k):
        out = jnp.where(rows == tt - k + r, head[r:r + 1, :], out)
    return out


def _conv_fwd(u, conv_w, off, w, name):
    t = u.shape[0]
    wb = 512
    c = lambda nme: off[nme] // wb

    def body(i, nt, cb_ref, cc_ref, ccp_ref, cx_ref, cxp_ref, cg_ref, w_ref, p_ref):
        h = cc_ref[...] * cx_ref[...]
        hp = jnp.where(i > 0, ccp_ref[...] * cxp_ref[...], 0.0)
        rows = lax.broadcasted_iota(jnp.int32, h.shape, 0)
        y = w_ref[0:1, :] * _shift_down(h, hp, 2, rows) + w_ref[1:2, :] * _shift_down(h, hp, 1, rows) + w_ref[2:3, :] * h
        silu, _ = _silu_parts(cg_ref[...])
        p_ref[...] = (cb_ref[...] * y * silu).astype(BF16)

    return _ew(body, name, t, w // wb, wb,
               [(u, "tile", c("c_b")), (u, "tile", c("c_c")), (u, "prev", c("c_c")), (u, "tile", c("c_x")),
                (u, "prev", c("c_x")), (u, "tile", c("c_g")), (conv_w, "row", 0)], [BF16])[0]


def _conv_bwd(dp, u, conv_w, off, w, name):
    t = u.shape[0]
    wb = 512
    c = lambda nme: off[nme] // wb

    def body(i, nt, dp_ref, dpn_ref, cb_ref, cbn_ref, cg_ref, cgn_ref, cc_ref, ccp_ref, cx_ref, cxp_ref, w_ref,
             dcb_ref, dcc_ref, dcx_ref, dcg_ref, acc_ref):
        @pl.when(i == 0)
        def _():
            acc_ref[...] = jnp.zeros_like(acc_ref)

        cc, cx, cb = cc_ref[...], cx_ref[...], cb_ref[...]
        h = cc * cx
        hp = jnp.where(i > 0, ccp_ref[...] * cxp_ref[...], 0.0)
        rows = lax.broadcasted_iota(jnp.int32, h.shape, 0)
        h1 = _shift_down(h, hp, 1, rows)
        h2 = _shift_down(h, hp, 2, rows)
        w0, w1, w2 = w_ref[0:1, :], w_ref[1:2, :], w_ref[2:3, :]
        y = w0 * h2 + w1 * h1 + w2 * h
        silu, dsilu = _silu_parts(cg_ref[...])
        dp_v = dp_ref[...]
        dcg_ref[...] = (dp_v * cb * y * dsilu).astype(BF16)
        dcb_ref[...] = (dp_v * y * silu).astype(BF16)
        dy = dp_v * cb * silu
        silu_n, _ = _silu_parts(cgn_ref[...])
        dyn = jnp.where(i < nt - 1, dpn_ref[...] * cbn_ref[...] * silu_n, 0.0)
        dh = w2 * dy + w1 * _shift_up(dy, dyn, 1, rows) + w0 * _shift_up(dy, dyn, 2, rows)
        dcc_ref[...] = (dh * cx).astype(BF16)
        dcx_ref[...] = (dh * cc).astype(BF16)
        acc_ref[0:1, :] += jnp.sum(dy * h2, axis=0, keepdims=True)
        acc_ref[1:2, :] += jnp.sum(dy * h1, axis=0, keepdims=True)
        acc_ref[2:3, :] += jnp.sum(dy * h, axis=0, keepdims=True)

    return _ew(body, name, t, w // wb, wb,
               [(dp, "tile", 0), (dp, "next", 0), (u, "tile", c("c_b")), (u, "next", c("c_b")),
                (u, "tile", c("c_g")), (u, "next", c("c_g")), (u, "tile", c("c_c")), (u, "prev", c("c_c")),
                (u, "tile", c("c_x")), (u, "prev", c("c_x")), (conv_w, "row", 0)],
               [BF16] * 4, accs=[SUBLANES])


def _proj_merge(p, w, ya, yb, u, off, name):
    t, k = p.shape
    d = w.shape[1]
    gb = GATE_BLOCK
    nb = d // gb
    tt = _tile(t, 256)
    cols = [off[nme] // gb for nme in ("m_a", "m_b", "m_c")]

    def body(p_ref, w_ref, ya_ref, yb_ref, *rest):
        m_refs, (yc_ref, mg_ref) = rest[:3 * nb], rest[3 * nb:]
        yc = _dot_nn(p_ref[...], w_ref[...])
        yc_ref[...] = yc
        for j in range(nb):
            sl = slice(j * gb, (j + 1) * gb)
            mg_ref[:, sl] = (_sigmoid(m_refs[j][...]) * ya_ref[:, sl] + _sigmoid(m_refs[nb + j][...]) * yb_ref[:, sl]
                             + _sigmoid(m_refs[2 * nb + j][...]) * yc[:, sl]).astype(BF16)

    row = pl.BlockSpec((tt, d), lambda i: (i, 0))
    gates = [pl.BlockSpec((tt, gb), lambda i, c=c0 + j: (i, c)) for c0 in cols for j in range(nb)]
    return pl.pallas_call(
        body, grid=(t // tt,),
        in_specs=[pl.BlockSpec((tt, k), lambda i: (i, 0)), pl.BlockSpec((k, d), lambda i: (0, 0)), row, row] + gates,
        out_specs=[row, row],
        out_shape=[jax.ShapeDtypeStruct((t, d), F32), jax.ShapeDtypeStruct((t, d), BF16)],
        name=name, compiler_params=_cp(("parallel",)))(p, w, ya, yb, *([u] * (3 * nb)))


def _lower_bounds(lb_param):
    def body(p_ref, o_ref):
        p = p_ref[...]
        e = jnp.exp(p - jnp.max(p, axis=0, keepdims=True))
        soft = e / jnp.sum(e, axis=0, keepdims=True)
        acc = jnp.zeros_like(soft[0:1])
        o_ref[0:1, :] = acc
        for l in range(1, DEPTH):
            acc = acc + soft[l:l + 1]
            o_ref[l:l + 1, :] = acc

    return pl.pallas_call(body, out_shape=jax.ShapeDtypeStruct(lb_param.shape, F32), name="lower_bounds",
                          compiler_params=_cp())(lb_param)


def _lower_bounds_bwd(lb_param, dlower):
    def body(p_ref, d_ref, o_ref):
        p = p_ref[...]
        e = jnp.exp(p - jnp.max(p, axis=0, keepdims=True))
        soft = e / jnp.sum(e, axis=0, keepdims=True)
        dl = d_ref[...]
        ds = [jnp.zeros_like(dl[0:1])]
        for j in range(1, DEPTH):
            acc = dl[j:j + 1]
            for l in range(j + 1, DEPTH):
                acc = acc + dl[l:l + 1]
            ds.append(acc)
        inner = ds[0] * soft[0:1]
        for j in range(1, DEPTH):
            inner = inner + ds[j] * soft[j:j + 1]
        for j in range(DEPTH):
            o_ref[j:j + 1, :] = soft[j:j + 1] * (ds[j] - inner)

    return pl.pallas_call(body, out_shape=jax.ShapeDtypeStruct(lb_param.shape, F32), name="lower_bounds_bwd",
                          compiler_params=_cp())(lb_param, dlower)


def _exchange(arrays, scatter, name, chips=False):
    n_arr = len(arrays)
    n_slot = N_DEV // 2 if chips else N_DEV

    def body(*refs):
        srcs, dsts = refs[:n_arr], refs[n_arr:2 * n_arr]
        send_sems, recv_sems, local_sems = refs[2 * n_arr:]
        me = (2 * lax.axis_index("x") + lax.axis_index("y") if chips
              else 4 * lax.axis_index("x") + 2 * lax.axis_index("y") + lax.axis_index("c"))
        copies = _peer_copies(srcs, dsts, send_sems, recv_sems, scatter, chips)
        for a in range(n_arr):
            copies.append(pltpu.make_async_copy(srcs[a].at[me] if scatter else srcs[a], dsts[a].at[me], local_sems.at[a]))
        for cp in copies:
            cp.start()
        for cp in copies:
            cp.wait()

    out_shape = [jax.ShapeDtypeStruct(a.shape if scatter else (n_slot,) + a.shape, a.dtype) for a in arrays]
    anyspec = pl.BlockSpec(memory_space=pl.ANY)
    res = pl.pallas_call(
        body, in_specs=[anyspec] * n_arr, out_specs=[anyspec] * n_arr, out_shape=out_shape,
        scratch_shapes=[pltpu.SemaphoreType.DMA((n_arr * (n_slot - 1),)), pltpu.SemaphoreType.DMA((n_arr * (n_slot - 1),)),
                        pltpu.SemaphoreType.DMA((n_arr,))],
        name=name)(*arrays)
    return list(res)


def _peer_copies(srcs, lands, send_sems, recv_sems, scatter, chips=False):
    x, y, c = lax.axis_index("x"), lax.axis_index("y"), lax.axis_index("c")
    flips = [k for k in range(1, N_DEV) if not (chips and k & 1)]
    slot = (lambda px, py, pc: 2 * px + py) if chips else (lambda px, py, pc: 4 * px + 2 * py + pc)
    copies = []
    for a in range(len(srcs)):
        for i, k in enumerate(flips):
            px = 1 - x if k & 4 else x
            py = 1 - y if k & 2 else y
            pc = 1 - c if k & 1 else c
            src = srcs[a].at[slot(px, py, pc)] if scatter else srcs[a]
            copies.append(pltpu.make_async_remote_copy(
                src_ref=src, dst_ref=lands[a].at[slot(x, y, c)],
                send_sem=send_sems.at[a * len(flips) + i], recv_sem=recv_sems.at[a * len(flips) + i],
                device_id=(px, py, pc), device_id_type=pl.DeviceIdType.MESH))
    return copies


def _gather_two_level(arrays, name):
    n_arr = len(arrays)
    per = N_DEV - 1

    def body(*refs):
        srcs, outs = refs[:n_arr], refs[n_arr:2 * n_arr]
        send_sems, recv_sems, local_sems = refs[2 * n_arr:]
        x, y, c = lax.axis_index("x"), lax.axis_index("y"), lax.axis_index("c")
        me, sibling = (x, y, c), (x, y, 1 - c)
        chips = [(1 - x, y), (x, 1 - y), (1 - x, 1 - y)]

        def copy(a, k, block, to, src=None):
            dst = outs[a].at[4 * block[0] + 2 * block[1] + block[2]]
            return pltpu.make_async_remote_copy(
                src_ref=dst if src is None else src, dst_ref=dst,
                send_sem=send_sems.at[a * per + k], recv_sem=recv_sems.at[a * per + k],
                device_id=to, device_id_type=pl.DeviceIdType.MESH)

        own, first, passed = [], [], []
        for a in range(n_arr):
            own.append(pltpu.make_async_copy(srcs[a], outs[a].at[4 * x + 2 * y + c], local_sems.at[a]))
            first.append(copy(a, 0, me, sibling, src=srcs[a]))
            first += [copy(a, 1 + j, me, (*chip, c), src=srcs[a]) for j, chip in enumerate(chips)]
        for cp in own + first:
            cp.start()
        for a in range(n_arr):
            for j, chip in enumerate(chips):
                copy(a, 1 + j, (*chip, c), me).wait_recv()
                passed.append(copy(a, 4 + j, (*chip, c), sibling))
                passed[-1].start()
        for a in range(n_arr):
            copy(a, 0, sibling, me).wait_recv()
            for j, chip in enumerate(chips):
                copy(a, 4 + j, (*chip, 1 - c), me).wait_recv()
        for cp in first + passed:
            cp.wait_send()
        for cp in own:
            cp.wait()

    anyspec = pl.BlockSpec(memory_space=pl.ANY)
    res = pl.pallas_call(
        body, in_specs=[anyspec] * n_arr, out_specs=[anyspec] * n_arr,
        out_shape=[jax.ShapeDtypeStruct((N_DEV,) + a.shape, a.dtype) for a in arrays],
        scratch_shapes=[pltpu.SemaphoreType.DMA((n_arr * per,)), pltpu.SemaphoreType.DMA((n_arr * per,)),
                        pltpu.SemaphoreType.DMA((n_arr,))],
        name=name)(*arrays)
    return list(res)


def _sibling_swap(arrays, name):
    n_arr = len(arrays)
    n_chip = N_DEV // 2

    def body(*refs):
        srcs, outs = refs[:n_arr], refs[n_arr:2 * n_arr]
        send_sems, recv_sems = refs[2 * n_arr:]
        x, y, c = lax.axis_index("x"), lax.axis_index("y"), lax.axis_index("c")
        copies = []
        for a in range(n_arr):
            for j in range(n_chip):
                copies.append(pltpu.make_async_remote_copy(
                    src_ref=srcs[a].at[2 * j + 1 - c], dst_ref=outs[a].at[j],
                    send_sem=send_sems.at[a * n_chip + j], recv_sem=recv_sems.at[a * n_chip + j],
                    device_id=(x, y, 1 - c), device_id_type=pl.DeviceIdType.MESH))
        for cp in copies:
            cp.start()
        for cp in copies:
            cp.wait()

    anyspec = pl.BlockSpec(memory_space=pl.ANY)
    res = pl.pallas_call(
        body, in_specs=[anyspec] * n_arr, out_specs=[anyspec] * n_arr,
        out_shape=[jax.ShapeDtypeStruct((n_chip,) + a.shape[1:], a.dtype) for a in arrays],
        scratch_shapes=[pltpu.SemaphoreType.DMA((n_arr * n_chip,)), pltpu.SemaphoreType.DMA((n_arr * n_chip,))],
        name=name)(*arrays)
    return list(res)


def _pair_sum(send, stage, core, name):
    _, r, c = send.shape
    n_chip = stage.shape[0]
    tr = _tile(r, 128)

    def body(core_ref, a_ref, b_ref, o_ref):
        o_ref[...] = a_ref[...] + b_ref[...]

    return pl.pallas_call(
        body,
        grid_spec=pltpu.PrefetchScalarGridSpec(
            num_scalar_prefetch=1, grid=(n_chip, r // tr),
            in_specs=[pl.BlockSpec((1, tr, c), lambda j, i, core_ref: (2 * j + core_ref[0], i, 0)),
                      pl.BlockSpec((1, tr, c), lambda j, i, core_ref: (j, i, 0))],
            out_specs=pl.BlockSpec((1, tr, c), lambda j, i, core_ref: (j, i, 0))),
        out_shape=jax.ShapeDtypeStruct(stage.shape, F32), name=name,
        compiler_params=_cp(("parallel", "parallel")))(core, send, stage)


_HBM_SPEC = pl.BlockSpec(memory_space=pltpu.HBM)
_SEM_SPEC = pl.BlockSpec(memory_space=pltpu.SEMAPHORE)
_ANY_SPEC = pl.BlockSpec(memory_space=pl.ANY)
_DATAFLOW = pltpu.SideEffectType.DATAFLOW_SIDE_EFFECTING


def _exchange_start(arrays, scatter, name, dep=None, chips=False):
    n_arr = len(arrays)
    n_slot = N_DEV // 2 if chips else N_DEV
    n_sem = n_arr * (n_slot - 1)
    me = (2 * lax.axis_index("x") + lax.axis_index("y") if chips
          else 4 * lax.axis_index("x") + 2 * lax.axis_index("y") + lax.axis_index("c"))
    lands = []
    for a in arrays:
        own = lax.dynamic_index_in_dim(a, me, 0, keepdims=False) if scatter else a
        shape = a.shape if scatter else (n_slot,) + a.shape
        lands.append(lax.dynamic_update_index_in_dim(lax.empty(shape, a.dtype), own, me, 0))
    dep_specs, dep_args = _dep_specs(dep)

    def body(*refs):
        srcs, lnds = refs[:n_arr], refs[n_arr:2 * n_arr]
        outs = refs[2 * n_arr + len(dep_args):]
        send_sems, recv_sems, token = outs[0], outs[1], outs[2 + 2 * n_arr]
        for cp in _peer_copies(srcs, lnds, send_sems, recv_sems, scatter, chips):
            cp.start()
        token[...] = jnp.zeros_like(token)

    thru = [pltpu.HBM(a.shape, a.dtype) for a in list(arrays) + lands]
    return pl.pallas_call(
        body, name=name,
        out_shape=(pltpu.SemaphoreType.DMA((n_sem,)), pltpu.SemaphoreType.DMA((n_sem,)), *thru,
                   jax.ShapeDtypeStruct((SUBLANES, LANES), F32)),
        in_specs=[_HBM_SPEC] * (2 * n_arr) + dep_specs,
        out_specs=(_SEM_SPEC, _SEM_SPEC, *[_HBM_SPEC] * (2 * n_arr), pl.BlockSpec(memory_space=pltpu.VMEM)),
        input_output_aliases={i: 2 + i for i in range(2 * n_arr)},
        compiler_params=pltpu.CompilerParams(has_side_effects=_DATAFLOW),
    )(*[pltpu.with_memory_space_constraint(a, pltpu.HBM) for a in list(arrays) + lands], *dep_args)


def _exchange_wait(started, scatter, name, after, chips=False):
    send_sems, recv_sems = started[0], started[1]
    thru = list(started[2:-1])
    n_arr = len(thru) // 2
    after = list(after) if isinstance(after, (list, tuple)) else [after]

    def body(*refs):
        srcs, lnds = refs[:n_arr], refs[n_arr:2 * n_arr]
        for cp in _peer_copies(srcs, lnds, refs[2 * n_arr], refs[2 * n_arr + 1], scatter, chips):
            cp.wait_send()
            cp.wait_recv()

    res = pl.pallas_call(
        body, name=name, out_shape=tuple(pltpu.HBM(a.shape, a.dtype) for a in thru),
        in_specs=[_HBM_SPEC] * (2 * n_arr) + [_SEM_SPEC, _SEM_SPEC] + [_ANY_SPEC] * len(after),
        out_specs=tuple([_HBM_SPEC] * (2 * n_arr)),
        input_output_aliases={i: i for i in range(2 * n_arr)},
        compiler_params=pltpu.CompilerParams(has_side_effects=_DATAFLOW),
    )(*thru, send_sems, recv_sems, *after)
    return list(res[n_arr:])


def _unshard_cols(g, name):
    nd, r, s = g.shape
    tr = _tile(r, 64)

    def body(i_ref, o_ref):
        for p in range(nd):
            o_ref[:, p * s:(p + 1) * s] = i_ref[p]

    return pl.pallas_call(
        body, grid=(r // tr,), in_specs=[pl.BlockSpec((nd, tr, s), lambda i: (0, i, 0))],
        out_specs=pl.BlockSpec((tr, nd * s), lambda i: (i, 0)),
        out_shape=jax.ShapeDtypeStruct((r, nd * s), g.dtype), name=name, compiler_params=_cp(("parallel",)))(g)


def _shard_cols(g, name):
    r, n = g.shape
    s = n // N_DEV
    tr = _tile(r, 64)

    def body(i_ref, o_ref):
        for p in range(N_DEV):
            o_ref[p] = i_ref[:, p * s:(p + 1) * s]

    return pl.pallas_call(
        body, grid=(r // tr,), in_specs=[pl.BlockSpec((tr, n), lambda i: (i, 0))],
        out_specs=pl.BlockSpec((N_DEV, tr, s), lambda i: (0, i, 0)),
        out_shape=jax.ShapeDtypeStruct((N_DEV, r, s), g.dtype), name=name, compiler_params=_cp(("parallel",)))(g)


def _slot_sum(slots, name):
    nd, r, c = slots.shape
    tr = _tile(r, 64)

    def body(s_ref, o_ref):
        acc = s_ref[0]
        for p in range(1, nd):
            acc = acc + s_ref[p]
        o_ref[...] = acc

    return pl.pallas_call(
        body, grid=(r // tr,), in_specs=[pl.BlockSpec((nd, tr, c), lambda i: (0, i, 0))],
        out_specs=pl.BlockSpec((tr, c), lambda i: (i, 0)),
        out_shape=jax.ShapeDtypeStruct((r, c), F32), name=name, compiler_params=_cp(("parallel",)))(slots)


def _adamw(w, g, m, v, name):
    r, c = w.shape
    tr = _tile(r, 256)
    c1 = 1.0 - ADAM_B1 ** ADAM_STEP
    c2 = 1.0 - ADAM_B2 ** ADAM_STEP

    def body(w_ref, g_ref, m_ref, v_ref, d_ref, nm_ref, nv_ref):
        gv = g_ref[...]
        nm = ADAM_B1 * m_ref[...] + (1.0 - ADAM_B1) * gv
        nv = ADAM_B2 * v_ref[...] + (1.0 - ADAM_B2) * (gv * gv)
        nm_ref[...] = nm
        nv_ref[...] = nv
        d_ref[...] = -ADAM_LR * ((nm / c1) / (jnp.sqrt(nv / c2) + ADAM_EPS) + ADAM_WD * w_ref[...])

    spec = pl.BlockSpec((tr, c), lambda i: (i, 0))
    return pl.pallas_call(
        body, grid=(r // tr,), in_specs=[spec] * 4, out_specs=[spec] * 3,
        out_shape=[jax.ShapeDtypeStruct((r, c), F32)] * 3, name=name, compiler_params=_cp(("parallel",)))(w, g, m, v)


def _forward_backward(x, target, weights_hook, grads_hook, lb_param, hgrn_norm_g, attn_sinks, rel_bias, ln_g, ln_b):
    t, d = x.shape
    w = d // 2
    off, n_in = _offsets(d)
    n_heads = w // ATTN_HEAD_DIM
    grp = n_heads // ATTN_KV_HEADS

    lower = _lower_bounds(lb_param)
    bias = _bias_table(rel_bias, n_heads)
    bias_g = bias.reshape(ATTN_KV_HEADS, grp * WINDOW, 2 * WINDOW)

    saved, weights = [], []
    xb = x.astype(BF16)
    for l in range(DEPTH):
        wl, token = weights_hook(l, x)
        weights.append(wl)
        s = {"x": x, "xb": xb}
        u = _mm_nt_cols(xb, wl["w_in_t"], f"in_proj", dep=token)
        s["u"] = u
        lb_l, gain_l, cw_l = lower[l:l + 1], hgrn_norm_g[l:l + 1], wl["conv_w"]
        o_a, states, p_a = _hgrn_fwd(u, lb_l, gain_l, off, f"hgrn_fwd")
        o_b, p_b = _attn_fwd(u, bias_g, attn_sinks[l], off, w, f"attn_fwd")
        p_c = _conv_fwd(u, cw_l, off, w, f"conv_fwd")
        y_a = _mm_nn(p_a, wl["w_proj_hgrn"], f"proj_a", tn=2048)
        y_b = _mm_nn(p_b, wl["w_proj_attn"], f"proj_b", tn=2048)
        y_c, merged = _proj_merge(p_c, wl["w_proj_conv"], y_a, y_b, u, off, f"proj_c_merge")
        x, xb, xhat, rstd = _out_proj_ln(merged, wl["w_out"], x, ln_g[l:l + 1], ln_b[l:l + 1], f"out_proj_ln")
        s.update(o_a=o_a, states=states, p_a=p_a, o_b=o_b, p_b=p_b, p_c=p_c, y_a=y_a, y_b=y_b, y_c=y_c,
                 merged=merged, xhat=xhat, rstd=rstd)
        saved.append(s)

    loss_acc, dx = _loss_head(x, target)

    d_ln, d_lower, d_gain, d_sink, d_conv = [None] * DEPTH, [None] * DEPTH, [None] * DEPTH, [None] * DEPTH, [None] * DEPTH
    dbias_total = None
    for l in reversed(range(DEPTH)):
        wl, s = weights[l], saved[l]
        u = s["u"]
        lb_l, gain_l, cw_l = lower[l:l + 1], hgrn_norm_g[l:l + 1], wl["conv_w"]
        dz, dzb, d_ln[l], dya, dyb, dyc, dma, dmb, dmc = _ln_bwd_out_proj_merge(
            dx, s["xhat"], s["rstd"], ln_g[l:l + 1], wl["w_out"], u, s["y_a"], s["y_b"], s["y_c"], off,
            f"ln_bwd_merge_bwd")
        g_out = _mm_tn(s["merged"], dzb, f"g_out", tn=2048)
        g_pa = _mm_tn(s["p_a"], dya, f"g_proj_a", tn=2048)
        g_pb = _mm_tn(s["p_b"], dyb, f"g_proj_b", tn=2048)
        g_pc = _mm_tn(s["p_c"], dyc, f"g_proj_c", tn=2048)
        dpa = _mm_nt(dya, wl["w_proj_hgrn"], f"d_p_a", tk=2048)
        dpb = _mm_nt(dyb, wl["w_proj_attn"], f"d_p_b", tk=2048)
        dpc = _mm_nt(dyc, wl["w_proj_conv"], f"d_p_c", tk=2048)
        d_aq, d_af, d_ai, d_ag, acc_a = _hgrn_bwd(u, lb_l, gain_l, s["states"], s["o_a"], dpa, off, f"hgrn_bwd")
        d_lower[l], d_gain[l] = acc_a[0:1], acc_a[1:2]
        d_bq, d_bkv, dbias_l, d_sink[l], d_bg = _attn_bwd(
            u, s["o_b"], dpb, bias_g, attn_sinks[l], off, w, f"attn_bwd")
        dbias_total = dbias_l if dbias_total is None else dbias_total + dbias_l
        d_cb, d_cc, d_cx, d_cg, d_conv[l] = _conv_bwd(dpc, u, cw_l, off, w, f"conv_bwd")
        du = jnp.concatenate([d_aq, d_af, d_ai, d_ag, d_bq, d_bkv, d_bg, d_cb, d_cc, d_cx, d_cg, dma, dmb, dmc], axis=1)
        g_in_t = _mm_tn_rows(du, s["xb"], f"g_in")
        token = grads_hook(l, {"w_in_t": g_in_t, "w_proj_hgrn": g_pa, "w_proj_attn": g_pb, "w_proj_conv": g_pc, "w_out": g_out})
        dx = _mm_nn_acc(du, wl["w_in_t"], f"d_x", add=dz, add_scale=ALPHA, dep=token)

    d_lower_all = jnp.concatenate([a[0:1] for a in d_lower], axis=0)
    small = {
        "lb_param": _lower_bounds_bwd(lb_param, d_lower_all),
        "hgrn_norm_g": jnp.concatenate([a[0:1] for a in d_gain], axis=0),
        "attn_sinks": jnp.concatenate([a[0:1, :n_heads] for a in d_sink], axis=0),
        "conv_w": jnp.stack([a[0:3] for a in d_conv], axis=0),
        "rel_bias": _bias_grad(dbias_total.reshape(n_heads, WINDOW, 2 * WINDOW), n_heads)[:, :n_heads],
        "ln_g": jnp.concatenate([a[0:1] for a in d_ln], axis=0),
        "ln_b": jnp.concatenate([a[1:2] for a in d_ln], axis=0),
    }
    return loss_acc, dx, small


BIG = ("w_in", "w_proj_hgrn", "w_proj_attn", "w_proj_conv", "w_out")
SMALL = ("lb_param", "hgrn_norm_g", "attn_sinks", "conv_w", "rel_bias", "ln_g", "ln_b")
ORDER = ("w_in", "w_proj_hgrn", "w_proj_attn", "w_proj_conv", "w_out", "lb_param", "hgrn_norm_g", "attn_sinks",
         "conv_w", "rel_bias", "ln_g", "ln_b")


def _pack(parts):
    flat = jnp.concatenate([p.reshape(-1) for p in parts])
    n = flat.shape[0]
    unit = SUBLANES * LANES
    total = -(-n // unit) * unit
    return jnp.pad(flat, (0, total - n)).reshape(total // LANES, LANES)


def _unpack(packed, shapes):
    flat = packed.reshape(-1)
    out, o = [], 0
    for shp in shapes:
        n = int(np.prod(shp))
        out.append(flat[o:o + n].reshape(shp))
        o += n
    return out


def kernel(x, w_in, w_proj_hgrn, w_proj_attn, w_proj_conv, w_out, lb_param, hgrn_norm_g, attn_sinks, conv_w, rel_bias, ln_g, ln_b, loss_target, m_w_in, m_w_proj_hgrn, m_w_proj_attn, m_w_proj_conv, m_w_out, m_lb_param, m_hgrn_norm_g, m_attn_sinks, m_conv_w, m_rel_bias, m_ln_g, m_ln_b, v_w_in, v_w_proj_hgrn, v_w_proj_attn, v_w_proj_conv, v_w_out, v_lb_param, v_hgrn_norm_g, v_attn_sinks, v_conv_w, v_rel_bias, v_ln_g, v_ln_b):
    params = dict(w_in=w_in, w_proj_hgrn=w_proj_hgrn, w_proj_attn=w_proj_attn, w_proj_conv=w_proj_conv, w_out=w_out,
                  lb_param=lb_param, hgrn_norm_g=hgrn_norm_g, attn_sinks=attn_sinks, conv_w=conv_w, rel_bias=rel_bias,
                  ln_g=ln_g, ln_b=ln_b)
    mom_m = dict(w_in=m_w_in, w_proj_hgrn=m_w_proj_hgrn, w_proj_attn=m_w_proj_attn, w_proj_conv=m_w_proj_conv,
                 w_out=m_w_out, lb_param=m_lb_param, hgrn_norm_g=m_hgrn_norm_g, attn_sinks=m_attn_sinks,
                 conv_w=m_conv_w, rel_bias=m_rel_bias, ln_g=m_ln_g, ln_b=m_ln_b)
    mom_v = dict(w_in=v_w_in, w_proj_hgrn=v_w_proj_hgrn, w_proj_attn=v_w_proj_attn, w_proj_conv=v_w_proj_conv,
                 w_out=v_w_out, lb_param=v_lb_param, hgrn_norm_g=v_hgrn_norm_g, attn_sinks=v_attn_sinks,
                 conv_w=v_conv_w, rel_bias=v_rel_bias, ln_g=v_ln_g, ln_b=v_ln_b)
    d = x.shape[-1]
    me = 4 * lax.axis_index("x") + 2 * lax.axis_index("y") + lax.axis_index("c")
    for group in (params, mom_m, mom_v):
        group["w_in"] = jnp.swapaxes(group["w_in"], 1, 2)

    def shards_of(l):
        return [params[n][l].astype(BF16) for n in BIG] + [conv_w[l]]

    gathers = {}

    def weights_hook(l, x_in):
        if l == 0:
            got = _gather_two_level(shards_of(0), "gather_weights_0")
        else:
            got = _exchange_wait(gathers.pop(l), False, f"gather_wait_{l}", x_in)
        token = None
        if l + 1 < DEPTH:
            gathers[l + 1] = _exchange_start(shards_of(l + 1), False, f"gather_start_{l + 1}", dep=got[0])
            token = gathers[l + 1][-1]
        wl = {
            "w_in_t": got[0].reshape(-1, d),
            "w_proj_hgrn": _unshard_cols(got[1], "unshard_w_proj_hgrn"),
            "w_proj_attn": _unshard_cols(got[2], "unshard_w_proj_attn"),
            "w_proj_conv": _unshard_cols(got[3], "unshard_w_proj_conv"),
            "w_out": got[4].reshape(d, d),
            "conv_w": _unshard_cols(got[5], "unshard_conv_w"),
        }
        return wl, token

    grads = {n: [None] * DEPTH for n in BIG}
    scatters = {}

    def finish_scatter(l, after):
        got = _exchange_wait(scatters.pop(l), True, f"scatter_wait_{l}", after, chips=(l == 0))
        for n, slots in zip(BIG, got):
            grads[n][l] = _slot_sum(slots, f"sum_{n}_chips" if l == 0 else f"sum_{n}")
        return got[0]

    def grads_hook(l, g):
        send = [g["w_in_t"].reshape(N_DEV, -1, d), _shard_cols(g["w_proj_hgrn"], "shard_g_proj_a"),
                _shard_cols(g["w_proj_attn"], "shard_g_proj_b"), _shard_cols(g["w_proj_conv"], "shard_g_proj_c"),
                g["w_out"].reshape(N_DEV, d // N_DEV, d)]
        dep = finish_scatter(l + 1, send[0]) if l + 1 < DEPTH else None
        if l == 0:
            core = lax.axis_index("c").astype(jnp.int32).reshape(1)
            staged = _sibling_swap(send, "pair_swap_grads")
            send = [_pair_sum(s, st, core, f"pair_sum_{n}") for n, s, st in zip(BIG, send, staged)]
        scatters[l] = _exchange_start(send, True, f"scatter_start_{l}", dep=dep, chips=(l == 0))
        return scatters[l][-1]

    loss_acc, dx, small = _forward_backward(
        x[0], loss_target[0], weights_hook, grads_hook, lb_param, hgrn_norm_g, attn_sinks, rel_bias, ln_g, ln_b)
    loss = lax.psum(0.5 * jnp.sum(loss_acc[0]) / d, ("x", "y", "c"))
    finish_scatter(0, [dx] + [grads[n][l] for n in BIG for l in range(1, DEPTH)])
    for n in BIG:
        grads[n] = jnp.stack(grads[n], axis=0)

    small_shapes = [small[n].shape for n in SMALL]
    packed = _pack([small[n] for n in SMALL])
    got = _exchange([packed], False, "gather_small_grads")[0]
    summed = _unpack(_slot_sum(got, "sum_small_grads"), small_shapes)
    for n, g in zip(SMALL, summed):
        grads[n] = g
    cs = conv_w.shape[-1]
    grads["conv_w"] = lax.dynamic_slice_in_dim(grads["conv_w"], me * cs, cs, axis=2)

    delta, new_m, new_v = {}, {}, {}
    for n in BIG:
        shp = params[n].shape
        flat = lambda a: a.reshape(-1, shp[-1])
        dl, nm, nv = _adamw(flat(params[n]), flat(grads[n]), flat(mom_m[n]), flat(mom_v[n]), f"adamw_{n}")
        delta[n], new_m[n], new_v[n] = dl.reshape(shp), nm.reshape(shp), nv.reshape(shp)
    shapes = [params[n].shape for n in SMALL]
    res = _adamw(_pack([params[n] for n in SMALL]), _pack([grads[n] for n in SMALL]),
                 _pack([mom_m[n] for n in SMALL]), _pack([mom_v[n] for n in SMALL]), "adamw_small")
    for dst, packed_res in zip((delta, new_m, new_v), res):
        for n, a in zip(SMALL, _unpack(packed_res, shapes)):
            dst[n] = a

    for group in (grads, delta, new_m, new_v):
        group["w_in"] = jnp.swapaxes(group["w_in"], 1, 2)
    return (loss, dx[None], *[grads[n] for n in ORDER], *[delta[n] for n in ORDER],
            *[new_m[n] for n in ORDER], *[new_v[n] for n in ORDER])
```
